```python
import math
import jax, jax.numpy as jnp
from jax import lax
import numpy as np

D_MODEL = 1024
BATCH = 8
SEQ = 8192
DEPTH = 1

D_FF = 2816
FFN_RES = 0.5
GLA_HEADS = 4
GLA_DK = 128
GLA_DV = 256
GLA_RANK = 16
GLA_TAU = 16.0
DN_HEADS = 8
DN_DK = 128
DN_DV = 128
CONV_K = 4
CHUNK = 64
EPS = 1e-6

GLA_QK = GLA_HEADS * GLA_DK
GLA_V = GLA_HEADS * GLA_DV
DN_QK = DN_HEADS * DN_DK
DN_V = DN_HEADS * DN_DV
IN_SIZES = (GLA_QK, GLA_QK, GLA_V, GLA_V, GLA_RANK,
            DN_QK, DN_QK, DN_V, DN_V, DN_HEADS, DN_HEADS,
            D_MODEL, D_MODEL)
D_IN = sum(IN_SIZES)

kernel_name = "macaron_gla_gdn_gated_merge"


def rms_norm(x, w):
    xf = x.astype(jnp.float32)
    xf = xf * lax.rsqrt(jnp.mean(xf * xf, axis=-1, keepdims=True) + EPS)
    return xf.astype(x.dtype) * w


def head_rms_norm(o, w, heads):
    b, t, _ = o.shape
    oh = o.reshape(b, t, heads, -1)
    oh = oh * lax.rsqrt(jnp.mean(oh * oh, axis=-1, keepdims=True) + EPS) * w.astype(jnp.float32)
    return oh.reshape(b, t, -1)


def swiglu(h, w_gate, w_up, w_down):
    return (jax.nn.silu(h @ w_gate) * (h @ w_up)) @ w_down


def to_chunks(t, heads):
    b, T, _ = t.shape
    return t.reshape(b, T // CHUNK, CHUNK, heads, -1).transpose(0, 3, 1, 2, 4)


def scalar_chunks(t):
    b, T, h = t.shape
    return t.reshape(b, T // CHUNK, CHUNK, h).transpose(0, 3, 1, 2)


def from_chunks(t):
    b, h, n, c, d = t.shape
    return t.transpose(0, 2, 3, 1, 4).reshape(b, n * c, h * d)


def causal_depthwise_conv(x, w):
    c = x.shape[-1]
    return lax.conv_general_dilated(
        x, w[:, None, :], window_strides=(1,), padding=[(CONV_K - 1, 0)],
        dimension_numbers=("NWC", "WIO", "NWC"), feature_group_count=c)


def gla_attention(q, k, v, log_a):
    f32 = jnp.float32
    q = to_chunks(q.astype(f32), GLA_HEADS) * GLA_DK ** -0.5
    k = to_chunks(k.astype(f32), GLA_HEADS)
    v = to_chunks(v.astype(f32), GLA_HEADS)
    b = jnp.cumsum(to_chunks(log_a.astype(f32), GLA_HEADS), axis=3)
    b_last = b[:, :, :, -1:, :]
    causal = jnp.tril(jnp.ones((CHUNK, CHUNK), bool))
    q_in = q * jnp.exp(b)
    scores = jnp.einsum("bhnid,bhnjd->bhnij", q_in, k * jnp.exp(-b))
    scores = jnp.where(causal, scores, 0.0)
    o_intra = jnp.einsum("bhnij,bhnjv->bhniv", scores, v)
    k_state = k * jnp.exp(b_last - b)
    a_chunk = jnp.exp(b_last[:, :, :, 0, :])

    def step(S, xs):
        qc, kc, vc, ac = xs
        o = jnp.einsum("bhcd,bhdv->bhcv", qc, S)
        S = S * ac[..., None] + jnp.einsum("bhcd,bhcv->bhdv", kc, vc)
        return S, o

    bsz = q.shape[0]
    S0 = jnp.zeros((bsz, GLA_HEADS, GLA_DK, GLA_DV), f32)
    xs = (jnp.moveaxis(q_in, 2, 0), jnp.moveaxis(k_state, 2, 0),
          jnp.moveaxis(v, 2, 0), jnp.moveaxis(a_chunk, 2, 0))
    _, o_inter = lax.scan(step, S0, xs)
    return from_chunks(o_intra + jnp.moveaxis(o_inter, 0, 2))


def gated_delta_attention(q, k, v, g, beta):
    f32 = jnp.float32
    q = to_chunks(q.astype(f32), DN_HEADS)
    k = to_chunks(k.astype(f32), DN_HEADS)
    q = q * lax.rsqrt(jnp.sum(q * q, -1, keepdims=True) + EPS) * DN_DK ** -0.5
    k = k * lax.rsqrt(jnp.sum(k * k, -1, keepdims=True) + EPS)
    v = to_chunks(v.astype(f32), DN_HEADS)
    G = jnp.cumsum(scalar_chunks(g.astype(f32)), axis=-1)
    beta = scalar_chunks(beta.astype(f32))[..., None]
    causal = jnp.tril(jnp.ones((CHUNK, CHUNK), bool))
    strict = jnp.tril(jnp.ones((CHUNK, CHUNK), bool), k=-1)
    decay = jnp.exp(jnp.where(causal, G[..., :, None] - G[..., None, :], -jnp.inf))
    k_beta = k * beta
    kk = jnp.einsum("bhnid,bhnjd->bhnij", k_beta, k) * decay
    M = jnp.eye(CHUNK, dtype=f32) + jnp.where(strict, kk, 0.0)
    rhs = jnp.concatenate([v * beta, k_beta * jnp.exp(G)[..., None]], axis=-1)
    sol = lax.linalg.triangular_solve(M, rhs, left_side=True, lower=True, unit_diagonal=True)
    u, w = sol[..., :DN_DV], sol[..., DN_DV:]
    qk = jnp.einsum("bhnid,bhnjd->bhnij", q, k) * decay
    q_dec = q * jnp.exp(G)[..., None]
    G_last = G[..., -1:]
    k_state = k * jnp.exp(G_last - G)[..., None]
    g_chunk = jnp.exp(G_last[..., 0])

    def step(S, xs):
        qdc, qkc, uc, wc, kc, gc = xs
        v_new = uc - jnp.einsum("bhcd,bhdv->bhcv", wc, S)
        o = jnp.einsum("bhcd,bhdv->bhcv", qdc, S) + jnp.einsum("bhij,bhjv->bhiv", qkc, v_new)
        S = S * gc[..., None, None] + jnp.einsum("bhcd,bhcv->bhdv", kc, v_new)
        return S, o

    bsz = q.shape[0]
    S0 = jnp.zeros((bsz, DN_HEADS, DN_DK, DN_DV), f32)
    xs = tuple(jnp.moveaxis(t, 2, 0) for t in (q_dec, qk, u, w, k_state, g_chunk))
    _, o = lax.scan(step, S0, xs)
    return from_chunks(jnp.moveaxis(o, 0, 2))


def hybrid_mixer(h, w_in, w_gla_gate, b_gla_gate, conv_w, dn_a_log, dn_dt_bias,
                 gla_head_norm, dn_head_norm, w_out):
    proj = h @ w_in
    cuts = [int(c) for c in np.cumsum(IN_SIZES)[:-1]]
    (gq, gk, gv, gr, glr, dq, dk, dv, dgate, dbeta, da, merge_a, merge_b) = jnp.split(proj, cuts, axis=-1)
    log_a = jax.nn.log_sigmoid(glr @ w_gla_gate + b_gla_gate).astype(jnp.float32) / GLA_TAU
    o_a = gla_attention(gq, gk, gv, log_a)
    o_a = head_rms_norm(o_a, gla_head_norm, GLA_HEADS) * jax.nn.silu(gr)
    qkv = jax.nn.silu(causal_depthwise_conv(jnp.concatenate([dq, dk, dv], axis=-1), conv_w))
    dq, dk, dv = qkv[..., :DN_QK], qkv[..., DN_QK:2 * DN_QK], qkv[..., 2 * DN_QK:]
    g = -jnp.exp(dn_a_log.astype(jnp.float32)) * jax.nn.softplus((da + dn_dt_bias).astype(jnp.float32))
    beta = jax.nn.sigmoid(dbeta)
    o_b = gated_delta_attention(dq, dk, dv, g, beta)
    o_b = head_rms_norm(o_b, dn_head_norm, DN_HEADS) * jax.nn.silu(dgate)
    y = jax.nn.sigmoid(merge_a) * o_a + jax.nn.sigmoid(merge_b) * o_b
    return y.astype(h.dtype) @ w_out


def _fwd_setup_inputs(seed: int = 0) -> dict:
    key = jax.random.key(seed)
    ks = jax.random.split(key, 24)
    L = DEPTH
    f32 = jnp.float32

    def dense(k, fan_in, shape):
        return jax.random.normal(k, shape, f32) * fan_in ** -0.5

    def gain(k, shape):
        return 1.0 + 0.02 * jax.random.normal(k, shape, f32)

    dt = jnp.exp(jax.random.uniform(ks[11], (L, DN_HEADS), f32, math.log(1e-3), math.log(1e-1)))
    return {
        "x": jax.random.normal(ks[0], (BATCH, SEQ, D_MODEL), f32),
        "ffn1_norm": gain(ks[1], (L, D_MODEL)),
        "ffn1_w_gate": dense(ks[2], D_MODEL, (L, D_MODEL, D_FF)),
        "ffn1_w_up": dense(ks[3], D_MODEL, (L, D_MODEL, D_FF)),
        "ffn1_w_down": dense(ks[4], D_FF, (L, D_FF, D_MODEL)),
        "mix_norm": gain(ks[5], (L, D_MODEL)),
        "w_in": dense(ks[6], D_MODEL, (L, D_MODEL, D_IN)),
        "w_gla_gate": dense(ks[7], GLA_RANK, (L, GLA_RANK, GLA_QK)),
        "b_gla_gate": 0.1 * jax.random.normal(ks[8], (L, GLA_QK), f32),
        "conv_w": dense(ks[9], CONV_K, (L, CONV_K, 2 * DN_QK + DN_V)),
        "dn_a_log": jnp.log(jax.random.uniform(ks[10], (L, DN_HEADS), f32, 1.0, 16.0)),
        "dn_dt_bias": dt + jnp.log(-jnp.expm1(-dt)),
        "gla_head_norm": gain(ks[12], (L, GLA_DV)),
        "dn_head_norm": gain(ks[13], (L, DN_DV)),
        "w_out": dense(ks[14], D_MODEL, (L, D_MODEL, D_MODEL)),
        "ffn2_norm": gain(ks[15], (L, D_MODEL)),
        "ffn2_w_gate": dense(ks[16], D_MODEL, (L, D_MODEL, D_FF)),
        "ffn2_w_up": dense(ks[17], D_MODEL, (L, D_MODEL, D_FF)),
        "ffn2_w_down": dense(ks[18], D_FF, (L, D_FF, D_MODEL)),
        "final_norm": gain(ks[19], (D_MODEL,)),
    }


def _fwd_reference(x, ffn1_norm, ffn1_w_gate, ffn1_w_up, ffn1_w_down, mix_norm, w_in,
              w_gla_gate, b_gla_gate, conv_w, dn_a_log, dn_dt_bias, gla_head_norm,
              dn_head_norm, w_out, ffn2_norm, ffn2_w_gate, ffn2_w_up, ffn2_w_down,
              final_norm):
    h = x
    for layer in range(DEPTH):
        h = h + FFN_RES * swiglu(rms_norm(h, ffn1_norm[layer]), ffn1_w_gate[layer],
                                 ffn1_w_up[layer], ffn1_w_down[layer])
        h = h + hybrid_mixer(rms_norm(h, mix_norm[layer]), w_in[layer], w_gla_gate[layer],
                             b_gla_gate[layer], conv_w[layer], dn_a_log[layer], dn_dt_bias[layer],
                             gla_head_norm[layer], dn_head_norm[layer], w_out[layer])
        h = h + FFN_RES * swiglu(rms_norm(h, ffn2_norm[layer]), ffn2_w_gate[layer],
                                 ffn2_w_up[layer], ffn2_w_down[layer])
    return rms_norm(h, final_norm)


import jax as _jax
import jax.numpy as _jnp

TWIN_FORMAT = 'train_step'
FWD_PARAMS = ['x', 'ffn1_norm', 'ffn1_w_gate', 'ffn1_w_up', 'ffn1_w_down', 'mix_norm', 'w_in', 'w_gla_gate', 'b_gla_gate', 'conv_w', 'dn_a_log', 'dn_dt_bias', 'gla_head_norm', 'dn_head_norm', 'w_out', 'ffn2_norm', 'ffn2_w_gate', 'ffn2_w_up', 'ffn2_w_down', 'final_norm']
TWIN_WEIGHTS = ['ffn1_norm', 'ffn1_w_gate', 'ffn1_w_up', 'ffn1_w_down', 'mix_norm', 'w_in', 'w_gla_gate', 'b_gla_gate', 'conv_w', 'dn_a_log', 'dn_dt_bias', 'gla_head_norm', 'dn_head_norm', 'w_out', 'ffn2_norm', 'ffn2_w_gate', 'ffn2_w_up', 'ffn2_w_down', 'final_norm']
TWIN_DIFF_INPUT = 'x'
TWIN_INPUTS = ['x', 'ffn1_norm', 'ffn1_w_gate', 'ffn1_w_up', 'ffn1_w_down', 'mix_norm', 'w_in', 'w_gla_gate', 'b_gla_gate', 'conv_w', 'dn_a_log', 'dn_dt_bias', 'gla_head_norm', 'dn_head_norm', 'w_out', 'ffn2_norm', 'ffn2_w_gate', 'ffn2_w_up', 'ffn2_w_down', 'final_norm', 'loss_target', 'm_ffn1_norm', 'm_ffn1_w_gate', 'm_ffn1_w_up', 'm_ffn1_w_down', 'm_mix_norm', 'm_w_in', 'm_w_gla_gate', 'm_b_gla_gate', 'm_conv_w', 'm_dn_a_log', 'm_dn_dt_bias', 'm_gla_head_norm', 'm_dn_head_norm', 'm_w_out', 'm_ffn2_norm', 'm_ffn2_w_gate', 'm_ffn2_w_up', 'm_ffn2_w_down', 'm_final_norm', 'v_ffn1_norm', 'v_ffn1_w_gate', 'v_ffn1_w_up', 'v_ffn1_w_down', 'v_mix_norm', 'v_w_in', 'v_w_gla_gate', 'v_b_gla_gate', 'v_conv_w', 'v_dn_a_log', 'v_dn_dt_bias', 'v_gla_head_norm', 'v_dn_head_norm', 'v_w_out', 'v_ffn2_norm', 'v_ffn2_w_gate', 'v_ffn2_w_up', 'v_ffn2_w_down', 'v_final_norm']
TWIN_OUTPUTS = ['loss', 'grad_x', 'grad_ffn1_norm', 'grad_ffn1_w_gate', 'grad_ffn1_w_up', 'grad_ffn1_w_down', 'grad_mix_norm', 'grad_w_in', 'grad_w_gla_gate', 'grad_b_gla_gate', 'grad_conv_w', 'grad_dn_a_log', 'grad_dn_dt_bias', 'grad_gla_head_norm', 'grad_dn_head_norm', 'grad_w_out', 'grad_ffn2_norm', 'grad_ffn2_w_gate', 'grad_ffn2_w_up', 'grad_ffn2_w_down', 'grad_final_norm', 'delta_ffn1_norm', 'delta_ffn1_w_gate', 'delta_ffn1_w_up', 'delta_ffn1_w_down', 'delta_mix_norm', 'delta_w_in', 'delta_w_gla_gate', 'delta_b_gla_gate', 'delta_conv_w', 'delta_dn_a_log', 'delta_dn_dt_bias', 'delta_gla_head_norm', 'delta_dn_head_norm', 'delta_w_out', 'delta_ffn2_norm', 'delta_ffn2_w_gate', 'delta_ffn2_w_up', 'delta_ffn2_w_down', 'delta_final_norm', 'new_m_ffn1_norm', 'new_m_ffn1_w_gate', 'new_m_ffn1_w_up', 'new_m_ffn1_w_down', 'new_m_mix_norm', 'new_m_w_in', 'new_m_w_gla_gate', 'new_m_b_gla_gate', 'new_m_conv_w', 'new_m_dn_a_log', 'new_m_dn_dt_bias', 'new_m_gla_head_norm', 'new_m_dn_head_norm', 'new_m_w_out', 'new_m_ffn2_norm', 'new_m_ffn2_w_gate', 'new_m_ffn2_w_up', 'new_m_ffn2_w_down', 'new_m_final_norm', 'new_v_ffn1_norm', 'new_v_ffn1_w_gate', 'new_v_ffn1_w_up', 'new_v_ffn1_w_down', 'new_v_mix_norm', 'new_v_w_in', 'new_v_w_gla_gate', 'new_v_b_gla_gate', 'new_v_conv_w', 'new_v_dn_a_log', 'new_v_dn_dt_bias', 'new_v_gla_head_norm', 'new_v_dn_head_norm', 'new_v_w_out', 'new_v_ffn2_norm', 'new_v_ffn2_w_gate', 'new_v_ffn2_w_up', 'new_v_ffn2_w_down', 'new_v_final_norm']
TWIN_LEAF_KINDS = {'loss': 'loss', 'grad_x': 'grad_x', 'grad_ffn1_norm': 'grad_w', 'grad_ffn1_w_gate': 'grad_w', 'grad_ffn1_w_up': 'grad_w', 'grad_ffn1_w_down': 'grad_w', 'grad_mix_norm': 'grad_w', 'grad_w_in': 'grad_w', 'grad_w_gla_gate': 'grad_w', 'grad_b_gla_gate': 'grad_w', 'grad_conv_w': 'grad_w', 'grad_dn_a_log': 'grad_w', 'grad_dn_dt_bias': 'grad_w', 'grad_gla_head_norm': 'grad_w', 'grad_dn_head_norm': 'grad_w', 'grad_w_out': 'grad_w', 'grad_ffn2_norm': 'grad_w', 'grad_ffn2_w_gate': 'grad_w', 'grad_ffn2_w_up': 'grad_w', 'grad_ffn2_w_down': 'grad_w', 'grad_final_norm': 'grad_w', 'delta_ffn1_norm': 'delta_w', 'delta_ffn1_w_gate': 'delta_w', 'delta_ffn1_w_up': 'delta_w', 'delta_ffn1_w_down': 'delta_w', 'delta_mix_norm': 'delta_w', 'delta_w_in': 'delta_w', 'delta_w_gla_gate': 'delta_w', 'delta_b_gla_gate': 'delta_w', 'delta_conv_w': 'delta_w', 'delta_dn_a_log': 'delta_w', 'delta_dn_dt_bias': 'delta_w', 'delta_gla_head_norm': 'delta_w', 'delta_dn_head_norm': 'delta_w', 'delta_w_out': 'delta_w', 'delta_ffn2_norm': 'delta_w', 'delta_ffn2_w_gate': 'delta_w', 'delta_ffn2_w_up': 'delta_w', 'delta_ffn2_w_down': 'delta_w', 'delta_final_norm': 'delta_w', 'new_m_ffn1_norm': 'new_m', 'new_m_ffn1_w_gate': 'new_m', 'new_m_ffn1_w_up': 'new_m', 'new_m_ffn1_w_down': 'new_m', 'new_m_mix_norm': 'new_m', 'new_m_w_in': 'new_m', 'new_m_w_gla_gate': 'new_m', 'new_m_b_gla_gate': 'new_m', 'new_m_conv_w': 'new_m', 'new_m_dn_a_log': 'new_m', 'new_m_dn_dt_bias': 'new_m', 'new_m_gla_head_norm': 'new_m', 'new_m_dn_head_norm': 'new_m', 'new_m_w_out': 'new_m', 'new_m_ffn2_norm': 'new_m', 'new_m_ffn2_w_gate': 'new_m', 'new_m_ffn2_w_up': 'new_m', 'new_m_ffn2_w_down': 'new_m', 'new_m_final_norm': 'new_m', 'new_v_ffn1_norm': 'new_v', 'new_v_ffn1_w_gate': 'new_v', 'new_v_ffn1_w_up': 'new_v', 'new_v_ffn1_w_down': 'new_v', 'new_v_mix_norm': 'new_v', 'new_v_w_in': 'new_v', 'new_v_w_gla_gate': 'new_v', 'new_v_b_gla_gate': 'new_v', 'new_v_conv_w': 'new_v', 'new_v_dn_a_log': 'new_v', 'new_v_dn_dt_bias': 'new_v', 'new_v_gla_head_norm': 'new_v', 'new_v_dn_head_norm': 'new_v', 'new_v_w_out': 'new_v', 'new_v_ffn2_norm': 'new_v', 'new_v_ffn2_w_gate': 'new_v', 'new_v_ffn2_w_up': 'new_v', 'new_v_ffn2_w_down': 'new_v', 'new_v_final_norm': 'new_v'}


def _forward(args):
    return _fwd_reference(*[args[k] for k in FWD_PARAMS])


def _output_shape():
    def fwd():
        inp = _fwd_setup_inputs(0)
        return _fwd_reference(*[inp[k] for k in FWD_PARAMS])
    out = _jax.eval_shape(fwd)
    return out.shape, out.dtype

N_MICROBATCH = 1
ADAM_LR = 0.001
ADAM_B1 = 0.9
ADAM_B2 = 0.999
ADAM_EPS = 1e-08
ADAM_WD = 0.01
ADAM_STEP = 10
PER_EXAMPLE_BATCH_AXIS = {'x': 0, 'loss_target': 0}
SHARED_INPUTS = []
_WEIGHT_DTYPES = {'ffn1_norm': _jnp.float32, 'ffn1_w_gate': _jnp.float32, 'ffn1_w_up': _jnp.float32, 'ffn1_w_down': _jnp.float32, 'mix_norm': _jnp.float32, 'w_in': _jnp.float32, 'w_gla_gate': _jnp.float32, 'b_gla_gate': _jnp.float32, 'conv_w': _jnp.float32, 'dn_a_log': _jnp.float32, 'dn_dt_bias': _jnp.float32, 'gla_head_norm': _jnp.float32, 'dn_head_norm': _jnp.float32, 'w_out': _jnp.float32, 'ffn2_norm': _jnp.float32, 'ffn2_w_gate': _jnp.float32, 'ffn2_w_up': _jnp.float32, 'ffn2_w_down': _jnp.float32, 'final_norm': _jnp.float32}
MOMENT_SCALE = {'ffn1_norm': 1.273526e-01, 'ffn1_w_gate': 5.485599e-02, 'ffn1_w_up': 5.315116e-02, 'ffn1_w_down': 8.811955e-02, 'mix_norm': 2.168248e-01, 'w_in': 6.585629e-02, 'w_gla_gate': 1.262493e-02, 'b_gla_gate': 4.632031e-02, 'conv_w': 5.528744e-02, 'dn_a_log': 4.199548e-01, 'dn_dt_bias': 3.924519e-01, 'gla_head_norm': 1.411100e-01, 'dn_head_norm': 1.945189e-01, 'w_out': 1.034642e-01, 'ffn2_norm': 9.497868e-02, 'ffn2_w_gate': 3.951538e-02, 'ffn2_w_up': 3.835234e-02, 'ffn2_w_down': 6.352011e-02, 'final_norm': 6.392867e+01}


def _to_microbatches(a, axis):
    t = _jnp.moveaxis(a, axis, 0)
    t = t.reshape((N_MICROBATCH, t.shape[0] // N_MICROBATCH) + t.shape[1:])
    return _jnp.moveaxis(t, 1, axis + 1)


def setup_inputs(seed: int = 0) -> dict:
    inp = _fwd_setup_inputs(seed)
    key = _jax.random.fold_in(_jax.random.key(seed), 7919)
    shape, _ = _output_shape()
    out = dict(inp)
    out["loss_target"] = _jax.random.normal(_jax.random.fold_in(key, 0), shape, _jnp.float32)
    for i, name in enumerate(TWIN_WEIGHTS):
        w = inp[name].astype(_jnp.float32)
        if MOMENT_SCALE is None:
            s = _jnp.sqrt(_jnp.mean(_jnp.square(w)) + 1e-30)
        else:
            s = MOMENT_SCALE[name]
        km, kv = _jax.random.split(_jax.random.fold_in(key, i + 1))
        out[name] = w
        out["m_" + name] = s * _jax.random.normal(km, w.shape, _jnp.float32)
        out["v_" + name] = (s * s) * _jax.random.uniform(kv, w.shape, _jnp.float32, 0.5, 1.5)
    if N_MICROBATCH > 1:
        for name, axis in PER_EXAMPLE_BATCH_AXIS.items():
            out[name] = _to_microbatches(out[name], axis)
    return {'x': out['x'], 'ffn1_norm': out['ffn1_norm'], 'ffn1_w_gate': out['ffn1_w_gate'], 'ffn1_w_up': out['ffn1_w_up'], 'ffn1_w_down': out['ffn1_w_down'], 'mix_norm': out['mix_norm'], 'w_in': out['w_in'], 'w_gla_gate': out['w_gla_gate'], 'b_gla_gate': out['b_gla_gate'], 'conv_w': out['conv_w'], 'dn_a_log': out['dn_a_log'], 'dn_dt_bias': out['dn_dt_bias'], 'gla_head_norm': out['gla_head_norm'], 'dn_head_norm': out['dn_head_norm'], 'w_out': out['w_out'], 'ffn2_norm': out['ffn2_norm'], 'ffn2_w_gate': out['ffn2_w_gate'], 'ffn2_w_up': out['ffn2_w_up'], 'ffn2_w_down': out['ffn2_w_down'], 'final_norm': out['final_norm'], 'loss_target': out['loss_target'], 'm_ffn1_norm': out['m_ffn1_norm'], 'm_ffn1_w_gate': out['m_ffn1_w_gate'], 'm_ffn1_w_up': out['m_ffn1_w_up'], 'm_ffn1_w_down': out['m_ffn1_w_down'], 'm_mix_norm': out['m_mix_norm'], 'm_w_in': out['m_w_in'], 'm_w_gla_gate': out['m_w_gla_gate'], 'm_b_gla_gate': out['m_b_gla_gate'], 'm_conv_w': out['m_conv_w'], 'm_dn_a_log': out['m_dn_a_log'], 'm_dn_dt_bias': out['m_dn_dt_bias'], 'm_gla_head_norm': out['m_gla_head_norm'], 'm_dn_head_norm': out['m_dn_head_norm'], 'm_w_out': out['m_w_out'], 'm_ffn2_norm': out['m_ffn2_norm'], 'm_ffn2_w_gate': out['m_ffn2_w_gate'], 'm_ffn2_w_up': out['m_ffn2_w_up'], 'm_ffn2_w_down': out['m_ffn2_w_down'], 'm_final_norm': out['m_final_norm'], 'v_ffn1_norm': out['v_ffn1_norm'], 'v_ffn1_w_gate': out['v_ffn1_w_gate'], 'v_ffn1_w_up': out['v_ffn1_w_up'], 'v_ffn1_w_down': out['v_ffn1_w_down'], 'v_mix_norm': out['v_mix_norm'], 'v_w_in': out['v_w_in'], 'v_w_gla_gate': out['v_w_gla_gate'], 'v_b_gla_gate': out['v_b_gla_gate'], 'v_conv_w': out['v_conv_w'], 'v_dn_a_log': out['v_dn_a_log'], 'v_dn_dt_bias': out['v_dn_dt_bias'], 'v_gla_head_norm': out['v_gla_head_norm'], 'v_dn_head_norm': out['v_dn_head_norm'], 'v_w_out': out['v_w_out'], 'v_ffn2_norm': out['v_ffn2_norm'], 'v_ffn2_w_gate': out['v_ffn2_w_gate'], 'v_ffn2_w_up': out['v_ffn2_w_up'], 'v_ffn2_w_down': out['v_ffn2_w_down'], 'v_final_norm': out['v_final_norm']}


def _loss(weights, diff, rest, loss_target):
    with _jax.named_scope("forward"):
        args = {**rest, TWIN_DIFF_INPUT: diff, **{k: w.astype(_WEIGHT_DTYPES[k]) for k, w in weights.items()}}
        y = _forward(args)
    with _jax.named_scope("loss_head"):
        err = _jnp.square(y.astype(_jnp.float32) - loss_target)
        return 0.5 * _jnp.sum(_jnp.mean(err, axis=-1)) if err.ndim else 0.5 * err


def _adamw(w, g, m, v):
    m = ADAM_B1 * m + (1.0 - ADAM_B1) * g
    v = ADAM_B2 * v + (1.0 - ADAM_B2) * _jnp.square(g)
    m_hat = m / (1.0 - ADAM_B1 ** ADAM_STEP)
    v_hat = v / (1.0 - ADAM_B2 ** ADAM_STEP)
    delta = -ADAM_LR * (m_hat / (_jnp.sqrt(v_hat) + ADAM_EPS) + ADAM_WD * w)
    return delta, m, v


def reference(x, ffn1_norm, ffn1_w_gate, ffn1_w_up, ffn1_w_down, mix_norm, w_in, w_gla_gate, b_gla_gate, conv_w, dn_a_log, dn_dt_bias, gla_head_norm, dn_head_norm, w_out, ffn2_norm, ffn2_w_gate, ffn2_w_up, ffn2_w_down, final_norm, loss_target, m_ffn1_norm, m_ffn1_w_gate, m_ffn1_w_up, m_ffn1_w_down, m_mix_norm, m_w_in, m_w_gla_gate, m_b_gla_gate, m_conv_w, m_dn_a_log, m_dn_dt_bias, m_gla_head_norm, m_dn_head_norm, m_w_out, m_ffn2_norm, m_ffn2_w_gate, m_ffn2_w_up, m_ffn2_w_down, m_final_norm, v_ffn1_norm, v_ffn1_w_gate, v_ffn1_w_up, v_ffn1_w_down, v_mix_norm, v_w_in, v_w_gla_gate, v_b_gla_gate, v_conv_w, v_dn_a_log, v_dn_dt_bias, v_gla_head_norm, v_dn_head_norm, v_w_out, v_ffn2_norm, v_ffn2_w_gate, v_ffn2_w_up, v_ffn2_w_down, v_final_norm):
    given = dict(x=x, ffn1_norm=ffn1_norm, ffn1_w_gate=ffn1_w_gate, ffn1_w_up=ffn1_w_up, ffn1_w_down=ffn1_w_down, mix_norm=mix_norm, w_in=w_in, w_gla_gate=w_gla_gate, b_gla_gate=b_gla_gate, conv_w=conv_w, dn_a_log=dn_a_log, dn_dt_bias=dn_dt_bias, gla_head_norm=gla_head_norm, dn_head_norm=dn_head_norm, w_out=w_out, ffn2_norm=ffn2_norm, ffn2_w_gate=ffn2_w_gate, ffn2_w_up=ffn2_w_up, ffn2_w_down=ffn2_w_down, final_norm=final_norm, loss_target=loss_target, m_ffn1_norm=m_ffn1_norm, m_ffn1_w_gate=m_ffn1_w_gate, m_ffn1_w_up=m_ffn1_w_up, m_ffn1_w_down=m_ffn1_w_down, m_mix_norm=m_mix_norm, m_w_in=m_w_in, m_w_gla_gate=m_w_gla_gate, m_b_gla_gate=m_b_gla_gate, m_conv_w=m_conv_w, m_dn_a_log=m_dn_a_log, m_dn_dt_bias=m_dn_dt_bias, m_gla_head_norm=m_gla_head_norm, m_dn_head_norm=m_dn_head_norm, m_w_out=m_w_out, m_ffn2_norm=m_ffn2_norm, m_ffn2_w_gate=m_ffn2_w_gate, m_ffn2_w_up=m_ffn2_w_up, m_ffn2_w_down=m_ffn2_w_down, m_final_norm=m_final_norm, v_ffn1_norm=v_ffn1_norm, v_ffn1_w_gate=v_ffn1_w_gate, v_ffn1_w_up=v_ffn1_w_up, v_ffn1_w_down=v_ffn1_w_down, v_mix_norm=v_mix_norm, v_w_in=v_w_in, v_w_gla_gate=v_w_gla_gate, v_b_gla_gate=v_b_gla_gate, v_conv_w=v_conv_w, v_dn_a_log=v_dn_a_log, v_dn_dt_bias=v_dn_dt_bias, v_gla_head_norm=v_gla_head_norm, v_dn_head_norm=v_dn_head_norm, v_w_out=v_w_out, v_ffn2_norm=v_ffn2_norm, v_ffn2_w_gate=v_ffn2_w_gate, v_ffn2_w_up=v_ffn2_w_up, v_ffn2_w_down=v_ffn2_w_down, v_final_norm=v_final_norm)
    weights = {n: given[n] for n in TWIN_WEIGHTS}
    shared = {n: given[n] for n in SHARED_INPUTS}
    per_example = {n: given[n] for n in ['x']}
    grad_fn = _jax.value_and_grad(_loss, argnums=(0, 1))

    def one_microbatch(ex, loss_target):
        ex = dict(ex)
        diff = ex.pop(TWIN_DIFF_INPUT)
        return grad_fn(weights, diff, {**shared, **ex}, loss_target)

    if N_MICROBATCH == 1:
        loss, (grad_w, grad_x) = one_microbatch(per_example, given["loss_target"])
    else:
        def body(carry, xs):
            loss_sum, grad_sum = carry
            l_k, (gw_k, gx_k) = one_microbatch(xs[0], xs[1])
            with _jax.named_scope("update"):
                return (loss_sum + l_k, _jax.tree.map(_jnp.add, grad_sum, gw_k)), gx_k

        init = (_jnp.zeros((), _jnp.float32), _jax.tree.map(_jnp.zeros_like, weights))
        (loss, grad_w), grad_x = _jax.lax.scan(body, init, (per_example, given["loss_target"]))
    with _jax.named_scope("update"):
        delta_w, new_m, new_v = {}, {}, {}
        for n in TWIN_WEIGHTS:
            delta_w[n], new_m[n], new_v[n] = _adamw(weights[n], grad_w[n], given["m_" + n], given["v_" + n])
    return (loss, grad_x, *[grad_w[n] for n in TWIN_WEIGHTS], *[delta_w[n] for n in TWIN_WEIGHTS],
            *[new_m[n] for n in TWIN_WEIGHTS], *[new_v[n] for n in TWIN_WEIGHTS])
```

```python
import functools
import math

import numpy as np
import jax
import jax.numpy as jnp
from jax import lax
from jax.experimental import pallas as pl
from jax.experimental.pallas import tpu as pltpu

F32 = jnp.float32
BF16 = jnp.bfloat16
HI = lax.Precision.HIGHEST
MESH = pl.DeviceIdType.MESH
ANY = pl.BlockSpec(memory_space=pl.ANY)

EPS = 1e-6
D = 1024
DFF = 2816
FFN_RES = 0.5
GLA_H, GLA_DK, GLA_DV, GLA_RANK, GLA_TAU = 4, 128, 256, 16, 16.0
DN_H, DN_DK, DN_DV = 8, 128, 128
CONV_K = 4
CHUNK = 64
N_SHARD = 4
ADAM_LR, ADAM_B1, ADAM_B2, ADAM_EPS, ADAM_WD, ADAM_STEP = 0.001, 0.9, 0.999, 1e-08, 0.01, 10

IN_SIZES = (512, 512, 1024, 1024, 16, 1024, 1024, 1024, 1024, 8, 8, 1024, 1024)
IN_OFF = tuple(int(v) for v in np.cumsum((0,) + IN_SIZES))
D_IN = IN_OFF[-1]
BIG = 9216
SMALL = 128
PIECES = (512, 512, 1024, 1024, 3072, 1024, 1024, 1024)

VMEM_LIMIT = 56 * 1024 * 1024
ROW_BLK = 256
ATT_BLK = 256


def _cp(*sem):
    return pltpu.CompilerParams(dimension_semantics=sem, vmem_limit_bytes=VMEM_LIMIT)


def _sigmoid(x):
    return 1.0 / (1.0 + jnp.exp(-x))


def _softplus(x):
    return jnp.maximum(x, 0.0) + jnp.log(1.0 + jnp.exp(-jnp.abs(x)))


def _log_sigmoid(x):
    return jnp.minimum(x, 0.0) - jnp.log(1.0 + jnp.exp(-jnp.abs(x)))


def _dot(a, b, prec=None):
    return jnp.dot(a, b, preferred_element_type=F32, precision=prec)


def _dot_nt(a, b, prec=None):
    return lax.dot_general(a, b, (((1,), (1,)), ((), ())), preferred_element_type=F32, precision=prec)


def _dot_tn(a, b, prec=None):
    return lax.dot_general(a, b, (((0,), (0,)), ((), ())), preferred_element_type=F32, precision=prec)


def _b(x):
    return x.astype(BF16)


def _iota2(n, m, axis):
    return lax.broadcasted_iota(jnp.int32, (n, m), axis)


def _load_weights(pairs, sem):
    copies = [pltpu.make_async_copy(s, d, sem.at[i]) for i, (s, d) in enumerate(pairs)]
    for c in copies:
        c.start()
    for c in copies:
        c.wait()


def _ffn_fwd(h, nw, wg, wu, wd, name):
    T = h.shape[0]
    tm = min(ROW_BLK, T)

    def body(h_ref, nw_ref, wg_hbm, wu_hbm, wd_hbm, ho_ref, n_ref, g_ref, u_ref, wg_v, wu_v, wd_v, sem):
        @pl.when(pl.program_id(0) == 0)
        def _():
            _load_weights(((wg_hbm, wg_v), (wu_hbm, wu_v), (wd_hbm, wd_v)), sem)

        x = h_ref[...]
        r = lax.rsqrt(jnp.mean(x * x, axis=-1, keepdims=True) + EPS)
        nb = _b((x * r) * nw_ref[...])
        n_ref[...] = nb
        g = _dot(nb, wg_v[...])
        u = _dot(nb, wu_v[...])
        g_ref[...] = _b(g)
        u_ref[...] = _b(u)
        a = _b(g * _sigmoid(g) * u)
        ho_ref[...] = x + FFN_RES * _dot(a, wd_v[...])

    row = lambda w: pl.BlockSpec((tm, w), lambda i: (i, 0))
    return pl.pallas_call(
        body, name=name, grid=(T // tm,),
        in_specs=[row(D), pl.BlockSpec((1, D), lambda i: (0, 0)), ANY, ANY, ANY],
        out_specs=[row(D), row(D), row(DFF), row(DFF)],
        out_shape=[jax.ShapeDtypeStruct((T, D), F32), jax.ShapeDtypeStruct((T, D), BF16),
                   jax.ShapeDtypeStruct((T, DFF), BF16), jax.ShapeDtypeStruct((T, DFF), BF16)],
        scratch_shapes=[pltpu.VMEM((D, DFF), BF16), pltpu.VMEM((D, DFF), BF16), pltpu.VMEM((DFF, D), BF16),
                        pltpu.SemaphoreType.DMA((3,))],
        compiler_params=_cp("arbitrary"),
    )(h, nw, wg, wu, wd)


def _ffn_bwd(dh, h, nw, g, u, wg, wu, wd, name):
    T = h.shape[0]
    tm = min(ROW_BLK, T)

    def body(dh_ref, h_ref, nw_ref, g_ref, u_ref, wg_hbm, wu_hbm, wd_hbm,
             dx_ref, dg_ref, du_ref, a_ref, df_ref, dnw_ref, wg_v, wu_v, wd_v, sem):
        @pl.when(pl.program_id(0) == 0)
        def _():
            _load_weights(((wg_hbm, wg_v), (wu_hbm, wu_v), (wd_hbm, wd_v)), sem)
            dnw_ref[...] = jnp.zeros_like(dnw_ref)

        dh_ = dh_ref[...]
        dfb = _b(FFN_RES * dh_)
        df_ref[...] = dfb
        da = _dot_nt(dfb, wd_v[...])
        gg = g_ref[...].astype(F32)
        uu = u_ref[...].astype(F32)
        sg = _sigmoid(gg)
        silu = gg * sg
        a_ref[...] = _b(silu * uu)
        dgb = _b(da * uu * (sg * (1.0 + gg * (1.0 - sg))))
        dub = _b(da * silu)
        dg_ref[...] = dgb
        du_ref[...] = dub
        dn = _dot_nt(dgb, wg_v[...]) + _dot_nt(dub, wu_v[...])
        x = h_ref[...]
        r = lax.rsqrt(jnp.mean(x * x, axis=-1, keepdims=True) + EPS)
        xhat = x * r
        dnw_ref[...] += jnp.sum(dn * xhat, axis=0, keepdims=True)
        dxhat = dn * nw_ref[...]
        dx_ref[...] = dh_ + r * (dxhat - xhat * jnp.mean(dxhat * xhat, axis=-1, keepdims=True))

    row = lambda w: pl.BlockSpec((tm, w), lambda i: (i, 0))
    one = pl.BlockSpec((1, D), lambda i: (0, 0))
    return pl.pallas_call(
        body, name=name, grid=(T // tm,),
        in_specs=[row(D), row(D), one, row(DFF), row(DFF), ANY, ANY, ANY],
        out_specs=[row(D), row(DFF), row(DFF), row(DFF), row(D), one],
        out_shape=[jax.ShapeDtypeStruct((T, D), F32), jax.ShapeDtypeStruct((T, DFF), BF16),
                   jax.ShapeDtypeStruct((T, DFF), BF16), jax.ShapeDtypeStruct((T, DFF), BF16),
                   jax.ShapeDtypeStruct((T, D), BF16), jax.ShapeDtypeStruct((1, D), F32)],
        scratch_shapes=[pltpu.VMEM((D, DFF), BF16), pltpu.VMEM((D, DFF), BF16), pltpu.VMEM((DFF, D), BF16),
                        pltpu.SemaphoreType.DMA((3,))],
        compiler_params=_cp("arbitrary"),
    )(dh, h, nw, g, u, wg, wu, wd)


def _mm_tn(a, b, bm, bn, name, tk=512):
    T, M = a.shape
    N = b.shape[1]
    tk = min(tk, T)
    bm, bn = min(bm, M), min(bn, N)

    def body(a_ref, b_ref, o_ref):
        @pl.when(pl.program_id(2) == 0)
        def _():
            o_ref[...] = jnp.zeros_like(o_ref)

        o_ref[...] += _dot_tn(_b(a_ref[...]), _b(b_ref[...]))

    return pl.pallas_call(
        body, name=name, grid=(M // bm, N // bn, T // tk),
        in_specs=[pl.BlockSpec((tk, bm), lambda i, j, k: (k, i)), pl.BlockSpec((tk, bn), lambda i, j, k: (k, j))],
        out_specs=pl.BlockSpec((bm, bn), lambda i, j, k: (i, j)),
        out_shape=jax.ShapeDtypeStruct((M, N), F32),
        compiler_params=_cp("parallel", "parallel", "arbitrary"),
    )(a, b)


def _norm_proj(h, nw, wbig, wsmall, name):
    T = h.shape[0]
    tm = min(512, T)
    tn = 1536

    def body(h_ref, nw_ref, wb_ref, ws_ref, pb_ref, ps_ref, n_ref):
        @pl.when(pl.program_id(1) == 0)
        def _():
            x = h_ref[...]
            r = lax.rsqrt(jnp.mean(x * x, axis=-1, keepdims=True) + EPS)
            nb = _b((x * r) * nw_ref[...])
            n_ref[...] = nb
            ps_ref[...] = _dot(nb, ws_ref[...])

        pb_ref[...] = _b(_dot(n_ref[...], wb_ref[...]))

    return pl.pallas_call(
        body, name=name, grid=(T // tm, BIG // tn),
        in_specs=[pl.BlockSpec((tm, D), lambda i, j: (i, 0)), pl.BlockSpec((1, D), lambda i, j: (0, 0)),
                  pl.BlockSpec((D, tn), lambda i, j: (0, j)), pl.BlockSpec((D, SMALL), lambda i, j: (0, 0))],
        out_specs=[pl.BlockSpec((tm, tn), lambda i, j: (i, j)), pl.BlockSpec((tm, SMALL), lambda i, j: (i, 0)),
                   pl.BlockSpec((tm, D), lambda i, j: (i, 0))],
        out_shape=[jax.ShapeDtypeStruct((T, BIG), BF16), jax.ShapeDtypeStruct((T, SMALL), F32),
                   jax.ShapeDtypeStruct((T, D), BF16)],
        compiler_params=_cp("parallel", "arbitrary"),
    )(h, nw, wbig, wsmall)


def _proj_bwd(dh, h, nw, pieces, dsmall, wbig, wsmall, name):
    T = h.shape[0]
    tm = min(ROW_BLK, T)
    offs = tuple(int(v) for v in np.cumsum((0,) + PIECES))

    def body(dh_ref, h_ref, nw_ref, *rest):
        p_refs = rest[:len(PIECES)]
        ds_ref, wb_hbm, ws_ref, dx_ref, dnw_ref, wb_v, sem = rest[len(PIECES):]

        @pl.when(pl.program_id(0) == 0)
        def _():
            _load_weights(((wb_hbm, wb_v),), sem)
            dnw_ref[...] = jnp.zeros_like(dnw_ref)

        dn = _dot_nt(_b(ds_ref[...]), ws_ref[...])
        for p_ref, lo, wdt in zip(p_refs, offs, PIECES):
            dn += _dot_nt(p_ref[...], wb_v[:, lo:lo + wdt])
        x = h_ref[...]
        r = lax.rsqrt(jnp.mean(x * x, axis=-1, keepdims=True) + EPS)
        xhat = x * r
        dnw_ref[...] += jnp.sum(dn * xhat, axis=0, keepdims=True)
        dxhat = dn * nw_ref[...]
        dx_ref[...] = dh_ref[...] + r * (dxhat - xhat * jnp.mean(dxhat * xhat, axis=-1, keepdims=True))

    row = lambda w: pl.BlockSpec((tm, w), lambda i: (i, 0))
    one = pl.BlockSpec((1, D), lambda i: (0, 0))
    return pl.pallas_call(
        body, name=name, grid=(T // tm,),
        in_specs=[row(D), row(D), one] + [row(w) for w in PIECES] + [row(SMALL), ANY, pl.BlockSpec((D, SMALL), lambda i: (0, 0))],
        out_specs=[row(D), one],
        out_shape=[jax.ShapeDtypeStruct((T, D), F32), jax.ShapeDtypeStruct((1, D), F32)],
        scratch_shapes=[pltpu.VMEM((D, BIG), BF16), pltpu.SemaphoreType.DMA((1,))],
        compiler_params=_cp("arbitrary"),
    )(dh, h, nw, *pieces, dsmall, wbig, wsmall)


def _gla_chunk(q_ref, k_ref, sm_ref, wg_ref, bg_ref, rows, tril):
    q = q_ref[rows, :].astype(F32)
    k = k_ref[rows, :].astype(F32)
    pre = _dot(sm_ref[rows, :], wg_ref[...], HI) + bg_ref[...]
    la = _log_sigmoid(pre) * (1.0 / GLA_TAU)
    bc = _dot(tril, la, HI)
    bl = bc[CHUNK - 1:CHUNK, :]
    eb = jnp.exp(bc)
    enb = jnp.exp(-bc)
    ebl = jnp.exp(bl - bc)
    q_in = q * (GLA_DK ** -0.5) * eb
    k_out = k * enb
    k_st = k * ebl
    a_ch = jnp.exp(bl)
    return pre, eb, enb, ebl, q_in, k_out, k_st, a_ch


def _gla_specs(blk):
    return [pl.BlockSpec((blk, GLA_DK), lambda h, j: (j, h)),
            pl.BlockSpec((blk, GLA_DK), lambda h, j: (j, GLA_H + h)),
            pl.BlockSpec((blk, GLA_DV), lambda h, j: (j, GLA_H + h)),
            pl.BlockSpec((blk, SMALL), lambda h, j: (j, 0)),
            pl.BlockSpec((SMALL, GLA_DK), lambda h, j: (0, h)),
            pl.BlockSpec((1, GLA_DK), lambda h, j: (0, h))]


def _gla_fwd(pbig, psmall, wgate, bgate, name):
    T = pbig.shape[0]
    blk = min(ATT_BLK, T)
    nc = blk // CHUNK

    def body(q_ref, k_ref, v_ref, sm_ref, wg_ref, bg_ref, o_ref, ss_ref, st_ref):
        @pl.when(pl.program_id(1) == 0)
        def _():
            st_ref[...] = jnp.zeros_like(st_ref)

        causal = _iota2(CHUNK, CHUNK, 0) >= _iota2(CHUNK, CHUNK, 1)
        tril = causal.astype(F32)
        st = st_ref[...]
        for c in range(nc):
            rows = pl.ds(c * CHUNK, CHUNK)
            _, _, _, _, q_in, k_out, k_st, a_ch = _gla_chunk(q_ref, k_ref, sm_ref, wg_ref, bg_ref, rows, tril)
            v = v_ref[rows, :]
            sc = jnp.where(causal, _dot_nt(_b(q_in), _b(k_out)), 0.0)
            ss_ref[0, c] = st
            o_ref[rows, :] = _dot(_b(sc), v) + _dot_nt(_b(q_in), _b(st))
            st = st * a_ch + _dot_tn(v, _b(k_st))
        st_ref[...] = st

    return pl.pallas_call(
        body, name=name, grid=(GLA_H, T // blk),
        in_specs=_gla_specs(blk),
        out_specs=[pl.BlockSpec((blk, GLA_DV), lambda h, j: (j, h)),
                   pl.BlockSpec((1, nc, GLA_DV, GLA_DK), lambda h, j: (h, j, 0, 0))],
        out_shape=[jax.ShapeDtypeStruct((T, GLA_H * GLA_DV), F32),
                   jax.ShapeDtypeStruct((GLA_H, T // CHUNK, GLA_DV, GLA_DK), F32)],
        scratch_shapes=[pltpu.VMEM((GLA_DV, GLA_DK), F32)],
        compiler_params=_cp("parallel", "arbitrary"),
    )(pbig, pbig, pbig, psmall, wgate, bgate)


def _gla_bwd(pbig, psmall, wgate, bgate, states, do, name):
    T = pbig.shape[0]
    blk = min(ATT_BLK, T)
    nc = blk // CHUNK
    nb = T // blk

    def body(q_ref, k_ref, v_ref, sm_ref, wg_ref, bg_ref, ss_ref, do_ref, dq_ref, dk_ref, dv_ref, dpre_ref, dst_ref):
        @pl.when(pl.program_id(1) == 0)
        def _():
            dst_ref[...] = jnp.zeros_like(dst_ref)

        causal = _iota2(CHUNK, CHUNK, 0) >= _iota2(CHUNK, CHUNK, 1)
        tril = causal.astype(F32)
        triu = (_iota2(CHUNK, CHUNK, 0) <= _iota2(CHUNK, CHUNK, 1)).astype(F32)
        dst = dst_ref[...]
        for c in range(nc - 1, -1, -1):
            rows = pl.ds(c * CHUNK, CHUNK)
            pre, eb, enb, ebl, q_in, k_out, k_st, a_ch = _gla_chunk(q_ref, k_ref, sm_ref, wg_ref, bg_ref, rows, tril)
            v = v_ref[rows, :]
            st = ss_ref[0, c]
            dob = _b(do_ref[rows, :])
            sc = jnp.where(causal, _dot_nt(_b(q_in), _b(k_out)), 0.0)
            dsc = _b(jnp.where(causal, _dot_nt(dob, v), 0.0))
            dq_in = _dot(dob, _b(st)) + _dot(dsc, _b(k_out))
            dk_out = _dot_tn(dsc, _b(q_in))
            dk_st = _dot(v, _b(dst))
            dv_ref[rows, :] = _b(_dot_tn(_b(sc), dob) + _dot_nt(_b(k_st), _b(dst)))
            da_ch = jnp.sum(st * dst, axis=0, keepdims=True)
            dst = dst * a_ch + _dot_tn(dob, _b(q_in))
            tk = dk_st * k_st
            db = dq_in * q_in - dk_out * k_out - tk
            db_last = jnp.sum(tk, axis=0, keepdims=True) + da_ch * a_ch
            dq_ref[rows, :] = _b(dq_in * (GLA_DK ** -0.5) * eb)
            dk_ref[rows, :] = _b(dk_out * enb + dk_st * ebl)
            dla = _dot(triu, db, HI) + db_last
            dpre_ref[rows, :] = dla * (1.0 / GLA_TAU) * _sigmoid(-pre)
        dst_ref[...] = dst

    rev = lambda f: (lambda h, j: f(h, nb - 1 - j))
    specs = [pl.BlockSpec(s.block_shape, rev(s.index_map)) for s in _gla_specs(blk)]
    return pl.pallas_call(
        body, name=name, grid=(GLA_H, nb),
        in_specs=specs + [pl.BlockSpec((1, nc, GLA_DV, GLA_DK), lambda h, j: (h, nb - 1 - j, 0, 0)),
                          pl.BlockSpec((blk, GLA_DV), lambda h, j: (nb - 1 - j, h))],
        out_specs=[pl.BlockSpec((blk, GLA_DK), lambda h, j: (nb - 1 - j, h)),
                   pl.BlockSpec((blk, GLA_DK), lambda h, j: (nb - 1 - j, h)),
                   pl.BlockSpec((blk, GLA_DV), lambda h, j: (nb - 1 - j, h)),
                   pl.BlockSpec((blk, GLA_DK), lambda h, j: (nb - 1 - j, h))],
        out_shape=[jax.ShapeDtypeStruct((T, GLA_H * GLA_DK), BF16), jax.ShapeDtypeStruct((T, GLA_H * GLA_DK), BF16),
                   jax.ShapeDtypeStruct((T, GLA_H * GLA_DV), BF16), jax.ShapeDtypeStruct((T, GLA_H * GLA_DK), F32)],
        scratch_shapes=[pltpu.VMEM((GLA_DV, GLA_DK), F32)],
        compiler_params=_cp("parallel", "arbitrary"),
    )(pbig, pbig, pbig, psmall, wgate, bgate, states, do)


def _gla_gate_bwd(dpre, psmall, wgate, name):
    T = dpre.shape[0]
    tm = min(512, T)
    W = GLA_H * GLA_DK

    def body(dp_ref, sm_ref, wg_ref, ds_ref, dw_ref, db_ref):
        @pl.when(pl.program_id(0) == 0)
        def _():
            dw_ref[...] = jnp.zeros_like(dw_ref)
            db_ref[...] = jnp.zeros_like(db_ref)

        dp = dp_ref[...]
        ds_ref[...] = _dot_nt(dp, wg_ref[...], HI)
        dw_ref[...] += _dot_tn(sm_ref[...], dp, HI)
        db_ref[...] += jnp.sum(dp, axis=0, keepdims=True)

    return pl.pallas_call(
        body, name=name, grid=(T // tm,),
        in_specs=[pl.BlockSpec((tm, W), lambda i: (i, 0)), pl.BlockSpec((tm, SMALL), lambda i: (i, 0)),
                  pl.BlockSpec((SMALL, W), lambda i: (0, 0))],
        out_specs=[pl.BlockSpec((tm, SMALL), lambda i: (i, 0)), pl.BlockSpec((SMALL, W), lambda i: (0, 0)),
                   pl.BlockSpec((1, W), lambda i: (0, 0))],
        out_shape=[jax.ShapeDtypeStruct((T, SMALL), F32), jax.ShapeDtypeStruct((SMALL, W), F32),
                   jax.ShapeDtypeStruct((1, W), F32)],
        compiler_params=_cp("arbitrary"),
    )(dpre, psmall, wgate)


CONV_C = 3 * 1024
CONV_BLK = 256


def _conv_fwd(pbig, cw8, name):
    T = pbig.shape[0]
    blk = min(CONV_BLK, T)

    def body(x_ref, w_ref, c_ref, prev_ref):
        @pl.when(pl.program_id(0) == 0)
        def _():
            prev_ref[...] = jnp.zeros_like(prev_ref)

        x = x_ref[...].astype(F32)
        prev = prev_ref[...]
        row8 = _iota2(8, CONV_C, 0)
        acc = x * w_ref[CONV_K - 1:CONV_K, :]
        for s in range(1, CONV_K):
            xs = pltpu.roll(x, s, 0)
            top = jnp.where(row8 < s, pltpu.roll(prev, s, 0), xs[:8])
            xs = jnp.concatenate([top, xs[8:]], axis=0)
            acc += xs * w_ref[CONV_K - 1 - s:CONV_K - s, :]
        c_ref[...] = _b(acc)
        prev_ref[...] = x[blk - 8:]

    return pl.pallas_call(
        body, name=name, grid=(T // blk,),
        in_specs=[pl.BlockSpec((blk, CONV_C), lambda i: (i, 1)), pl.BlockSpec((8, CONV_C), lambda i: (0, 0))],
        out_specs=pl.BlockSpec((blk, CONV_C), lambda i: (i, 0)),
        out_shape=jax.ShapeDtypeStruct((T, CONV_C), BF16),
        scratch_shapes=[pltpu.VMEM((8, CONV_C), F32)],
        compiler_params=_cp("arbitrary"),
    )(pbig, cw8)


def _conv_bwd(dcq, dck, dcv, pbig, cw8, name):
    T = pbig.shape[0]
    blk = min(CONV_BLK, T)
    nb = T // blk

    def body(dq_ref, dk_ref, dv_ref, x_ref, w_ref, dx_ref, dw_ref, nxt_ref):
        @pl.when(pl.program_id(0) == 0)
        def _():
            nxt_ref[...] = jnp.zeros_like(nxt_ref)
            dw_ref[...] = jnp.zeros_like(dw_ref)

        dc = jnp.concatenate([dq_ref[...], dk_ref[...], dv_ref[...]], axis=1).astype(F32)
        x = x_ref[...].astype(F32)
        nxt = nxt_ref[...]
        row8 = _iota2(8, CONV_C, 0)
        acc = dc * w_ref[CONV_K - 1:CONV_K, :]
        dws = [jnp.sum(dc * x, axis=0, keepdims=True)]
        for s in range(1, CONV_K):
            ds = pltpu.roll(dc, blk - s, 0)
            bot = jnp.where(row8 >= 8 - s, pltpu.roll(nxt, 8 - s, 0), ds[blk - 8:])
            ds = jnp.concatenate([ds[:blk - 8], bot], axis=0)
            acc += ds * w_ref[CONV_K - 1 - s:CONV_K - s, :]
            dws.append(jnp.sum(ds * x, axis=0, keepdims=True))
        dx_ref[...] = _b(acc)
        dw_ref[...] += jnp.concatenate(dws[::-1] + [jnp.zeros((8 - CONV_K, CONV_C), F32)], axis=0)
        nxt_ref[...] = dc[:8]

    part = pl.BlockSpec((blk, 1024), lambda i: (nb - 1 - i, 0))
    return pl.pallas_call(
        body, name=name, grid=(nb,),
        in_specs=[part, part, part, pl.BlockSpec((blk, CONV_C), lambda i: (nb - 1 - i, 1)),
                  pl.BlockSpec((8, CONV_C), lambda i: (0, 0))],
        out_specs=[pl.BlockSpec((blk, CONV_C), lambda i: (nb - 1 - i, 0)), pl.BlockSpec((8, CONV_C), lambda i: (0, 0))],
        out_shape=[jax.ShapeDtypeStruct((T, CONV_C), BF16), jax.ShapeDtypeStruct((8, CONV_C), F32)],
        scratch_shapes=[pltpu.VMEM((8, CONV_C), F32)],
        compiler_params=_cp("arbitrary"),
    )(dcq, dck, dcv, pbig, cw8)


def _col(x, lane):
    sel = _iota2(x.shape[0], x.shape[1], 1) == lane
    return jnp.broadcast_to(jnp.sum(jnp.where(sel, x, 0.0), axis=1, keepdims=True), x.shape)


def _unit_lower_inverse(low):
    eye = (_iota2(CHUNK, CHUNK, 0) == _iota2(CHUNK, CHUNK, 1)).astype(F32)
    xk = -low
    inv = eye + xk
    for _ in range(5):
        xk = _dot(xk, xk, HI)
        inv = inv + _dot(inv, xk, HI)
    return inv


def _gdn_chunk(cq_ref, ck_ref, cv_ref, sm_ref, par_ref, rows, h, masks):
    causal, strict, tril, eye = masks
    cq = cq_ref[rows, :].astype(F32)
    ck = ck_ref[rows, :].astype(F32)
    cv = cv_ref[rows, :].astype(F32)
    sq, sk, sv = _sigmoid(cq), _sigmoid(ck), _sigmoid(cv)
    q, k, v = cq * sq, ck * sk, cv * sv
    rq = lax.rsqrt(jnp.sum(q * q, axis=-1, keepdims=True) + EPS)
    rk = lax.rsqrt(jnp.sum(k * k, axis=-1, keepdims=True) + EPS)
    qh, kn = q * rq, k * rk
    qn = qh * (DN_DK ** -0.5)
    sm = sm_ref[rows, :]
    braw = _col(sm, GLA_RANK + h)
    araw = _col(sm, GLA_RANK + DN_H + h)
    ea = jnp.exp(par_ref[0, 0:1, :])
    bias = par_ref[0, 1:2, :]
    beta = _sigmoid(braw)
    sp_arg = araw + bias
    g = -ea * _softplus(sp_arg)
    G = _dot(tril, g, HI)
    gc = G[:, :CHUNK]
    grow = jnp.sum(eye * gc, axis=0, keepdims=True)
    decay = jnp.exp(jnp.where(causal, gc - grow, -1e30))
    kb = kn * beta
    A = _dot_nt(kb, kn, HI)
    low = jnp.where(strict, A * decay, 0.0)
    tinv = _unit_lower_inverse(low)
    eG = jnp.exp(G)
    gl = G[CHUNK - 1:CHUNK, :]
    eGl = jnp.exp(gl - G)
    g_ch = jnp.exp(gl)
    rv = v * beta
    rkk = kb * eG
    u = _dot(tinv, rv, HI)
    w = _dot(tinv, rkk, HI)
    B = _dot_nt(_b(qn), _b(kn))
    qk = jnp.where(causal, B * decay, 0.0)
    q_dec = qn * eG
    k_st = kn * eGl
    return dict(cq=cq, ck=ck, cv=cv, sq=sq, sk=sk, sv=sv, q=q, k=k, v=v, rq=rq, rk=rk, qh=qh, kn=kn, qn=qn,
                beta=beta, ea=ea, sp_arg=sp_arg, g=g, G=G, decay=decay, kb=kb, A=A, tinv=tinv, eG=eG, eGl=eGl,
                g_ch=g_ch, rv=rv, rkk=rkk, u=u, w=w, B=B, qk=qk, q_dec=q_dec, k_st=k_st)


def _gdn_masks():
    r, c = _iota2(CHUNK, CHUNK, 0), _iota2(CHUNK, CHUNK, 1)
    return r >= c, r > c, (r >= c).astype(F32), (r == c).astype(F32)


def _gdn_specs(blk, idx):
    return [pl.BlockSpec((blk, DN_DK), lambda h, j: (idx(j), h)),
            pl.BlockSpec((blk, DN_DK), lambda h, j: (idx(j), DN_H + h)),
            pl.BlockSpec((blk, DN_DV), lambda h, j: (idx(j), 2 * DN_H + h)),
            pl.BlockSpec((blk, SMALL), lambda h, j: (idx(j), 0)),
            pl.BlockSpec((1, 8, 128), lambda h, j: (h, 0, 0))]


def _gdn_fwd(conv, psmall, par, name):
    T = conv.shape[0]
    blk = min(ATT_BLK, T)
    nc = blk // CHUNK

    def body(cq_ref, ck_ref, cv_ref, sm_ref, par_ref, o_ref, ss_ref, s_ref):
        h = pl.program_id(0)

        @pl.when(pl.program_id(1) == 0)
        def _():
            s_ref[...] = jnp.zeros_like(s_ref)

        masks = _gdn_masks()
        S = s_ref[...]
        for c in range(nc):
            rows = pl.ds(c * CHUNK, CHUNK)
            f = _gdn_chunk(cq_ref, ck_ref, cv_ref, sm_ref, par_ref, rows, h, masks)
            ss_ref[0, c] = S
            Sb = _b(S)
            v_new = f["u"] - _dot(_b(f["w"]), Sb)
            o_ref[rows, :] = _dot(_b(f["q_dec"]), Sb) + _dot(_b(f["qk"]), _b(v_new))
            S = S * f["g_ch"] + _dot_tn(_b(f["k_st"]), _b(v_new))
        s_ref[...] = S

    return pl.pallas_call(
        body, name=name, grid=(DN_H, T // blk),
        in_specs=_gdn_specs(blk, lambda j: j),
        out_specs=[pl.BlockSpec((blk, DN_DV), lambda h, j: (j, h)),
                   pl.BlockSpec((1, nc, DN_DK, DN_DV), lambda h, j: (h, j, 0, 0))],
        out_shape=[jax.ShapeDtypeStruct((T, DN_H * DN_DV), F32),
                   jax.ShapeDtypeStruct((DN_H, T // CHUNK, DN_DK, DN_DV), F32)],
        scratch_shapes=[pltpu.VMEM((DN_DK, DN_DV), F32)],
        compiler_params=_cp("parallel", "arbitrary"),
    )(conv, conv, conv, psmall, par)


def _gdn_bwd(conv, psmall, par, states, do, name):
    T = conv.shape[0]
    blk = min(ATT_BLK, T)
    nc = blk // CHUNK
    nb = T // blk
    rsum = lambda x: jnp.sum(x, axis=-1, keepdims=True)

    def body(cq_ref, ck_ref, cv_ref, sm_ref, par_ref, ss_ref, do_ref,
             dcq_ref, dck_ref, dcv_ref, dsm_ref, dpar_ref, ds_ref):
        h = pl.program_id(0)

        @pl.when(pl.program_id(1) == 0)
        def _():
            ds_ref[...] = jnp.zeros_like(ds_ref)
            dpar_ref[...] = jnp.zeros_like(dpar_ref)

        masks = _gdn_masks()
        causal, strict, tril, eye = masks
        triu = (_iota2(CHUNK, CHUNK, 0) <= _iota2(CHUNK, CHUNK, 1)).astype(F32)
        lane = _iota2(CHUNK, 128, 1)
        last_row = _iota2(CHUNK, 128, 0) == CHUNK - 1
        dS = ds_ref[...]
        dpar = jnp.zeros((1, 128), F32)
        for c in range(nc - 1, -1, -1):
            rows = pl.ds(c * CHUNK, CHUNK)
            f = _gdn_chunk(cq_ref, ck_ref, cv_ref, sm_ref, par_ref, rows, h, masks)
            S = ss_ref[0, c]
            Sb = _b(S)
            dSb = _b(dS)
            do_ = _b(do_ref[rows, :])
            wb, qdb, kstb, qkb = _b(f["w"]), _b(f["q_dec"]), _b(f["k_st"]), _b(f["qk"])
            v_new = f["u"] - _dot(wb, Sb)
            vnb = _b(v_new)
            dvn = _dot_tn(qkb, do_) + _dot(kstb, dSb)
            dvnb = _b(dvn)
            dq_dec = _dot_nt(do_, Sb)
            dqk = jnp.where(causal, _dot_nt(do_, vnb), 0.0)
            dk_st = _dot_nt(vnb, dSb)
            dg_ch = jnp.sum(rsum(S * dS), axis=0, keepdims=True)
            dw = -_dot_nt(dvnb, Sb)
            dS = dS * f["g_ch"] + _dot_tn(qdb, do_) - _dot_tn(wb, dvnb)
            drv = _dot_tn(f["tinv"], dvn, HI)
            drk = _dot_tn(f["tinv"], dw, HI)
            dlow = jnp.where(strict, -(_dot_nt(drv, f["u"], HI) + _dot_nt(drk, f["w"], HI)), 0.0)
            dv = drv * f["beta"]
            dbeta = rsum(drv * f["v"])
            dkb = drk * f["eG"]
            dG = rsum(drk * f["rkk"])
            dA = dlow * f["decay"]
            ddec = dlow * f["A"]
            dkb += _dot(dA, f["kn"], HI)
            dkn = _dot_tn(dA, f["kb"], HI)
            dB = dqk * f["decay"]
            ddec += dqk * f["B"]
            dqn = _dot(_b(dB), _b(f["kn"]))
            dkn += _dot_tn(_b(dB), _b(f["qn"]))
            dD = ddec * f["decay"]
            dG += rsum(dD) - rsum(eye * jnp.sum(dD, axis=0, keepdims=True))
            dqn += dq_dec * f["eG"]
            dG += rsum(dq_dec * f["q_dec"])
            dkn += dk_st * f["eGl"]
            tks = rsum(dk_st * f["k_st"])
            dG -= tks
            dG_last = jnp.sum(tks, axis=0, keepdims=True) + dg_ch * f["g_ch"][:, :1]
            dkn += dkb * f["beta"]
            dbeta += rsum(dkb * f["kn"])
            dGf = jnp.broadcast_to(dG, (CHUNK, 128)) + jnp.where(last_row, dG_last, 0.0)
            dg = _dot(triu, dGf, HI)
            dbraw = dbeta * f["beta"][:, :1] * (1.0 - f["beta"][:, :1])
            daraw = dg * (-f["ea"]) * _sigmoid(f["sp_arg"])
            dsm_ref[0, rows, :] = jnp.where(lane == 0, dbraw, jnp.where(lane == 1, daraw, 0.0))
            dpar += jnp.where(lane[:1] == 0, jnp.sum(dg * f["g"], axis=0, keepdims=True),
                              jnp.where(lane[:1] == 1, jnp.sum(daraw, axis=0, keepdims=True), 0.0))
            dqh = dqn * (DN_DK ** -0.5)
            dq = f["rq"] * (dqh - f["qh"] * rsum(dqh * f["qh"]))
            dk = f["rk"] * (dkn - f["kn"] * rsum(dkn * f["kn"]))
            dsilu = lambda x, s: s * (1.0 + x * (1.0 - s))
            dcq_ref[rows, :] = _b(dq * dsilu(f["cq"], f["sq"]))
            dck_ref[rows, :] = _b(dk * dsilu(f["ck"], f["sk"]))
            dcv_ref[rows, :] = _b(dv * dsilu(f["cv"], f["sv"]))
        ds_ref[...] = dS
        dpar_ref[0] += jnp.broadcast_to(dpar, (8, 128))

    r = lambda j: nb - 1 - j
    out_blk = pl.BlockSpec((blk, DN_DK), lambda h, j: (r(j), h))
    return pl.pallas_call(
        body, name=name, grid=(DN_H, nb),
        in_specs=_gdn_specs(blk, r) + [pl.BlockSpec((1, nc, DN_DK, DN_DV), lambda h, j: (h, r(j), 0, 0)),
                                      pl.BlockSpec((blk, DN_DV), lambda h, j: (r(j), h))],
        out_specs=[out_blk, out_blk, out_blk, pl.BlockSpec((1, blk, 128), lambda h, j: (h, r(j), 0)),
                   pl.BlockSpec((1, 8, 128), lambda h, j: (h, 0, 0))],
        out_shape=[jax.ShapeDtypeStruct((T, DN_H * DN_DK), BF16)] * 3 + [
            jax.ShapeDtypeStruct((DN_H, T, 128), F32), jax.ShapeDtypeStruct((DN_H, 8, 128), F32)],
        scratch_shapes=[pltpu.VMEM((DN_DK, DN_DV), F32)],
        compiler_params=_cp("parallel", "arbitrary"),
    )(conv, conv, conv, psmall, par, states, do)


def _head_norm(o, w, dv):
    outs, rs = [], []
    for i in range(o.shape[1] // dv):
        oh = o[:, i * dv:(i + 1) * dv]
        r = lax.rsqrt(jnp.mean(oh * oh, axis=-1, keepdims=True) + EPS)
        outs.append(oh * r)
        rs.append(r)
    return outs, rs


def _merge_specs(tm):
    col = lambda c: pl.BlockSpec((tm, D), lambda i: (i, c))
    return [col(0), col(0), col(2), col(6), col(7), col(8),
            pl.BlockSpec((1, GLA_DV), lambda i: (0, 0)), pl.BlockSpec((1, DN_DV), lambda i: (0, 0)),
            pl.BlockSpec((D, D), lambda i: (0, 0))]


def _merge_fwd(h, oa, ob, pbig, gla_hn, dn_hn, wout, name):
    T = h.shape[0]
    tm = min(ROW_BLK, T)

    def body(h_ref, oa_ref, ob_ref, gr_ref, dg_ref, ma_ref, mb_ref, wa_ref, wb_ref, wo_ref, ho_ref, y_ref):
        na, _ = _head_norm(oa_ref[...], wa_ref[...], GLA_DV)
        nbs, _ = _head_norm(ob_ref[...], wb_ref[...], DN_DV)
        hna = jnp.concatenate([t * wa_ref[...] for t in na], axis=1)
        hnb = jnp.concatenate([t * wb_ref[...] for t in nbs], axis=1)
        gr = gr_ref[...].astype(F32)
        dg = dg_ref[...].astype(F32)
        y = (_sigmoid(ma_ref[...].astype(F32)) * hna * (gr * _sigmoid(gr))
             + _sigmoid(mb_ref[...].astype(F32)) * hnb * (dg * _sigmoid(dg)))
        yb = _b(y)
        y_ref[...] = yb
        ho_ref[...] = h_ref[...] + _dot(yb, wo_ref[...])

    row = pl.BlockSpec((tm, D), lambda i: (i, 0))
    return pl.pallas_call(
        body, name=name, grid=(T // tm,),
        in_specs=[row] + _merge_specs(tm),
        out_specs=[row, row],
        out_shape=[jax.ShapeDtypeStruct((T, D), F32), jax.ShapeDtypeStruct((T, D), BF16)],
        compiler_params=_cp("arbitrary"),
    )(h, oa, ob, pbig, pbig, pbig, pbig, gla_hn, dn_hn, wout)


def _merge_bwd(dh, oa, ob, pbig, gla_hn, dn_hn, wout, name):
    T = dh.shape[0]
    tm = min(ROW_BLK, T)

    def branch(dy, o_ref, w_ref, gate_ref, m_ref, dv):
        w = w_ref[...]
        ohat, rs = _head_norm(o_ref[...], w, dv)
        gate = gate_ref[...].astype(F32)
        m = m_ref[...].astype(F32)
        sgate, sm = _sigmoid(gate), _sigmoid(m)
        silu = gate * sgate
        ohat_all = jnp.concatenate(ohat, axis=1)
        hn = jnp.concatenate([t * w for t in ohat], axis=1)
        d_on = dy * sm
        d_m = dy * hn * silu * sm * (1.0 - sm)
        d_hn = d_on * silu
        d_gate = d_on * hn * (sgate * (1.0 + gate * (1.0 - sgate)))
        dw = jnp.zeros((1, dv), F32)
        d_o = []
        for i, (oh, r) in enumerate(zip(ohat, rs)):
            dhn = d_hn[:, i * dv:(i + 1) * dv]
            dw += jnp.sum(dhn * oh, axis=0, keepdims=True)
            dohat = dhn * w
            d_o.append(r * (dohat - oh * jnp.mean(dohat * oh, axis=-1, keepdims=True)))
        return jnp.concatenate(d_o, axis=1), d_gate, d_m, dw

    def body(dh_ref, oa_ref, ob_ref, gr_ref, dg_ref, ma_ref, mb_ref, wa_ref, wb_ref, wo_ref,
             doa_ref, dob_ref, dgr_ref, ddg_ref, dma_ref, dmb_ref, dwa_ref, dwb_ref, dhb_ref):
        @pl.when(pl.program_id(0) == 0)
        def _():
            dwa_ref[...] = jnp.zeros_like(dwa_ref)
            dwb_ref[...] = jnp.zeros_like(dwb_ref)

        dhb = _b(dh_ref[...])
        dhb_ref[...] = dhb
        dy = _dot_nt(dhb, wo_ref[...])
        d_oa, d_gr, d_ma, dwa = branch(dy, oa_ref, wa_ref, gr_ref, ma_ref, GLA_DV)
        d_ob, d_dg, d_mb, dwb = branch(dy, ob_ref, wb_ref, dg_ref, mb_ref, DN_DV)
        doa_ref[...] = d_oa
        dob_ref[...] = d_ob
        dgr_ref[...] = _b(d_gr)
        ddg_ref[...] = _b(d_dg)
        dma_ref[...] = _b(d_ma)
        dmb_ref[...] = _b(d_mb)
        dwa_ref[...] += dwa
        dwb_ref[...] += dwb

    row = pl.BlockSpec((tm, D), lambda i: (i, 0))
    f32 = jax.ShapeDtypeStruct((T, D), F32)
    b16 = jax.ShapeDtypeStruct((T, D), BF16)
    return pl.pallas_call(
        body, name=name, grid=(T // tm,),
        in_specs=[row] + _merge_specs(tm),
        out_specs=[row] * 6 + [pl.BlockSpec((1, GLA_DV), lambda i: (0, 0)), pl.BlockSpec((1, DN_DV), lambda i: (0, 0)), row],
        out_shape=[f32, f32, b16, b16, b16, b16, jax.ShapeDtypeStruct((1, GLA_DV), F32),
                   jax.ShapeDtypeStruct((1, DN_DV), F32), b16],
        compiler_params=_cp("arbitrary"),
    )(dh, oa, ob, pbig, pbig, pbig, pbig, gla_hn, dn_hn, wout)


def _loss_head(h, nw, target, name):
    T = h.shape[0]
    tm = min(512, T)

    def body(h_ref, nw_ref, t_ref, dx_ref, loss_ref, dnw_ref):
        @pl.when(pl.program_id(0) == 0)
        def _():
            loss_ref[...] = jnp.zeros_like(loss_ref)
            dnw_ref[...] = jnp.zeros_like(dnw_ref)

        x = h_ref[...]
        w = nw_ref[...]
        r = lax.rsqrt(jnp.mean(x * x, axis=-1, keepdims=True) + EPS)
        xhat = x * r
        err = xhat * w - t_ref[...]
        part = jnp.sum(jnp.sum(err * err, axis=-1, keepdims=True), axis=0, keepdims=True)
        loss_ref[...] += (0.5 / D) * part
        dout = err * (1.0 / D)
        dnw_ref[...] += jnp.sum(dout * xhat, axis=0, keepdims=True)
        dxhat = dout * w
        dx_ref[...] = r * (dxhat - xhat * jnp.mean(dxhat * xhat, axis=-1, keepdims=True))

    row = pl.BlockSpec((tm, D), lambda i: (i, 0))
    one = pl.BlockSpec((1, D), lambda i: (0, 0))
    return pl.pallas_call(
        body, name=name, grid=(T // tm,),
        in_specs=[row, one, row],
        out_specs=[row, pl.BlockSpec((8, 128), lambda i: (0, 0)), one],
        out_shape=[jax.ShapeDtypeStruct((T, D), F32), jax.ShapeDtypeStruct((8, 128), F32),
                   jax.ShapeDtypeStruct((1, D), F32)],
        compiler_params=_cp("arbitrary"),
    )(h, nw, target)


def _adamw(w, g, m, v, rows, name):
    R, C = w.shape
    rows = min(rows, R)
    c1 = 1.0 - ADAM_B1 ** ADAM_STEP
    c2 = 1.0 - ADAM_B2 ** ADAM_STEP

    def body(w_ref, g_ref, m_ref, v_ref, d_ref, mo_ref, vo_ref):
        g_ = g_ref[...]
        m_ = ADAM_B1 * m_ref[...] + (1.0 - ADAM_B1) * g_
        v_ = ADAM_B2 * v_ref[...] + (1.0 - ADAM_B2) * (g_ * g_)
        mo_ref[...] = m_
        vo_ref[...] = v_
        d_ref[...] = -ADAM_LR * ((m_ / c1) / (jnp.sqrt(v_ / c2) + ADAM_EPS) + ADAM_WD * w_ref[...])

    blk = pl.BlockSpec((rows, C), lambda i: (i, 0))
    shp = jax.ShapeDtypeStruct((R, C), F32)
    return pl.pallas_call(
        body, name=name, grid=(R // rows,),
        in_specs=[blk] * 4, out_specs=[blk] * 3, out_shape=[shp] * 3,
        compiler_params=_cp("parallel"),
    )(w, g, m, v)


def _me():
    return lax.axis_index("x"), lax.axis_index("y"), lax.axis_index("c")


def _other_chips(x, y):
    return [(1 - x, y), (x, 1 - y), (1 - x, 1 - y)]


def _gather_weights(pack, name):
    _, rh, cols = pack.shape

    def body(p_ref, o_ref, send_sems, recv_sems, local_sem):
        x, y, c = _me()
        sibling = (x, y, 1 - c)
        chips = _other_chips(x, y)

        def copy(k, chip, half, to, src=None):
            dst = o_ref.at[2 * chip[0] + chip[1], half]
            return pltpu.make_async_remote_copy(src_ref=dst if src is None else src, dst_ref=dst,
                                                send_sem=send_sems.at[k], recv_sem=recv_sems.at[k],
                                                device_id=to, device_id_type=MESH)

        mine = pltpu.make_async_copy(p_ref, o_ref.at[2 * x + y], local_sem)
        mine.start()
        first = [copy(j, (x, y), c, (*chip, c), src=p_ref.at[c]) for j, chip in enumerate(chips)]
        for cp in first:
            cp.start()
        passed = [copy(3 + j, chip, c, sibling) for j, chip in enumerate(chips)]
        for j, chip in enumerate(chips):
            copy(j, chip, c, (x, y, c)).wait_recv()
            passed[j].start()
        for j, chip in enumerate(chips):
            copy(3 + j, chip, 1 - c, (x, y, c)).wait_recv()
        for cp in first + passed:
            cp.wait_send()
        mine.wait()

    return pl.pallas_call(
        body, name=name, in_specs=[ANY], out_specs=ANY,
        out_shape=jax.ShapeDtypeStruct((N_SHARD, 2, rh, cols), pack.dtype),
        scratch_shapes=[pltpu.SemaphoreType.DMA((6,)), pltpu.SemaphoreType.DMA((6,)), pltpu.SemaphoreType.DMA],
        compiler_params=pltpu.CompilerParams(has_side_effects=True),
    )(pack)


def _sibling_swap(gp, name):
    _, ns, rh, cols = gp.shape

    def body(g_ref, o_ref, send_sem, recv_sem):
        x, y, c = _me()
        cp = pltpu.make_async_remote_copy(src_ref=g_ref.at[1 - c], dst_ref=o_ref, send_sem=send_sem, recv_sem=recv_sem,
                                          device_id=(x, y, 1 - c), device_id_type=MESH)
        cp.start()
        cp.wait()

    return pl.pallas_call(
        body, name=name, in_specs=[ANY], out_specs=ANY,
        out_shape=jax.ShapeDtypeStruct((ns, rh, cols), gp.dtype),
        scratch_shapes=[pltpu.SemaphoreType.DMA, pltpu.SemaphoreType.DMA],
        compiler_params=pltpu.CompilerParams(has_side_effects=True),
    )(gp)


def _add_pair(gp, other, name):
    _, ns, rh, cols = gp.shape
    rb = PACK_BLK

    def body(a_ref, b_ref, f_ref, h_ref):
        s = a_ref[0, 0].astype(F32) + b_ref[0].astype(F32)
        f_ref[0] = s
        h_ref[0] = _b(s)

    c = lax.axis_index("c")
    return pl.pallas_call(
        body, name=name, grid=(ns, rh // rb),
        in_specs=[pl.BlockSpec((1, 1, rb, cols), lambda s, i: (0, s, i, 0)), pl.BlockSpec((1, rb, cols), lambda s, i: (s, i, 0))],
        out_specs=[pl.BlockSpec((1, rb, cols), lambda s, i: (s, i, 0))] * 2,
        out_shape=[jax.ShapeDtypeStruct((ns, rh, cols), F32), jax.ShapeDtypeStruct((ns, rh, cols), BF16)],
        compiler_params=_cp("parallel", "parallel"),
    )(lax.dynamic_slice_in_dim(gp, c, 1, axis=0), other)


def _chip_exchange(pb, name):
    ns, rh, cols = pb.shape

    def body(p_ref, o_ref, send_sems, recv_sems):
        x, y, c = _me()
        cps = [pltpu.make_async_remote_copy(src_ref=p_ref.at[2 * chip[0] + chip[1]], dst_ref=o_ref.at[j],
                                            send_sem=send_sems.at[j], recv_sem=recv_sems.at[j],
                                            device_id=(*chip, c), device_id_type=MESH)
               for j, chip in enumerate(_other_chips(x, y))]
        for cp in cps:
            cp.start()
        for cp in cps:
            cp.wait()

    return pl.pallas_call(
        body, name=name, in_specs=[ANY], out_specs=ANY,
        out_shape=jax.ShapeDtypeStruct((3, rh, cols), pb.dtype),
        scratch_shapes=[pltpu.SemaphoreType.DMA((3,)), pltpu.SemaphoreType.DMA((3,))],
        compiler_params=pltpu.CompilerParams(has_side_effects=True),
    )(pb)


def _add_four(own, got, name):
    rh, cols = own.shape
    rb = PACK_BLK

    def body(a_ref, b_ref, o_ref):
        o_ref[...] = ((a_ref[...] + b_ref[0].astype(F32)) + b_ref[1].astype(F32)) + b_ref[2].astype(F32)

    return pl.pallas_call(
        body, name=name, grid=(rh // rb,),
        in_specs=[pl.BlockSpec((rb, cols), lambda i: (i, 0)), pl.BlockSpec((3, rb, cols), lambda i: (0, i, 0))],
        out_specs=pl.BlockSpec((rb, cols), lambda i: (i, 0)),
        out_shape=jax.ShapeDtypeStruct((rh, cols), F32),
        compiler_params=_cp("parallel"),
    )(own, got)


def _join_halves(half, name):
    rh, cols = half.shape

    def body(h_ref, o_ref, send_sem, recv_sem, local_sem):
        x, y, c = _me()
        mine = pltpu.make_async_copy(h_ref, o_ref.at[c], local_sem)
        mine.start()
        cp = pltpu.make_async_remote_copy(src_ref=h_ref, dst_ref=o_ref.at[c], send_sem=send_sem, recv_sem=recv_sem,
                                          device_id=(x, y, 1 - c), device_id_type=MESH)
        cp.start()
        cp.wait_send()
        pltpu.make_async_remote_copy(src_ref=h_ref, dst_ref=o_ref.at[1 - c], send_sem=send_sem, recv_sem=recv_sem,
                                     device_id=(x, y, 1 - c), device_id_type=MESH).wait_recv()
        mine.wait()

    return pl.pallas_call(
        body, name=name, in_specs=[ANY], out_specs=ANY,
        out_shape=jax.ShapeDtypeStruct((2, rh, cols), half.dtype),
        scratch_shapes=[pltpu.SemaphoreType.DMA, pltpu.SemaphoreType.DMA, pltpu.SemaphoreType.DMA],
        compiler_params=pltpu.CompilerParams(has_side_effects=True),
    )(half)


def _allsum_small(vec, name):
    def body(v_ref, o_ref, buf_ref, send_sems, recv_sems):
        x, y, c = _me()
        me = 4 * x + 2 * y + c
        buf_ref[me] = v_ref[...]
        cps = []
        for k in range(1, 8):
            peer = (x ^ (k >> 2), y ^ ((k >> 1) & 1), c ^ (k & 1))
            cps.append(pltpu.make_async_remote_copy(src_ref=v_ref, dst_ref=buf_ref.at[me],
                                                    send_sem=send_sems.at[k - 1], recv_sem=recv_sems.at[k - 1],
                                                    device_id=peer, device_id_type=MESH))
        for cp in cps:
            cp.start()
        for k in range(1, 8):
            peer_idx = me ^ k
            pltpu.make_async_remote_copy(src_ref=v_ref, dst_ref=buf_ref.at[peer_idx],
                                         send_sem=send_sems.at[k - 1], recv_sem=recv_sems.at[k - 1],
                                         device_id=(x, y, c), device_id_type=MESH).wait_recv()
        for cp in cps:
            cp.wait_send()
        acc = buf_ref[0]
        for d in range(1, 8):
            acc = acc + buf_ref[d]
        o_ref[...] = acc

    return pl.pallas_call(
        body, name=name,
        in_specs=[pl.BlockSpec(memory_space=pltpu.VMEM)], out_specs=pl.BlockSpec(memory_space=pltpu.VMEM),
        out_shape=jax.ShapeDtypeStruct(vec.shape, F32),
        scratch_shapes=[pltpu.VMEM((8,) + vec.shape, F32), pltpu.SemaphoreType.DMA((7,)), pltpu.SemaphoreType.DMA((7,))],
        compiler_params=pltpu.CompilerParams(has_side_effects=True),
    )(vec)


SHARDED = ("ffn1_w_gate", "ffn1_w_up", "ffn1_w_down", "w_in", "w_gla_gate", "conv_w", "w_out",
           "ffn2_w_gate", "ffn2_w_up", "ffn2_w_down")
SHARD_SHAPE = {"ffn1_w_gate": (D, DFF // 4), "ffn1_w_up": (D, DFF // 4), "ffn1_w_down": (DFF // 4, D),
               "w_in": (D, D_IN // 4), "w_gla_gate": (GLA_RANK, 128), "conv_w": (CONV_K, 768), "w_out": (256, D),
               "ffn2_w_gate": (D, DFF // 4), "ffn2_w_up": (D, DFF // 4), "ffn2_w_down": (DFF // 4, D)}
SHARD_AXIS = {"ffn1_w_gate": 1, "ffn1_w_up": 1, "ffn1_w_down": 0, "w_in": 1, "w_gla_gate": 1, "conv_w": 1, "w_out": 0,
              "ffn2_w_gate": 1, "ffn2_w_up": 1, "ffn2_w_down": 0}
LOW_PART = {"w_gla_gate_lo": "w_gla_gate", "conv_w_lo": "conv_w"}
for _lo, _hi in LOW_PART.items():
    SHARD_SHAPE[_lo] = SHARD_SHAPE[_hi]
    SHARD_AXIS[_lo] = SHARD_AXIS[_hi]
WEIGHT_PACK = SHARDED + tuple(LOW_PART)
PACK_ELEMS = sum(SHARD_SHAPE[n][0] * SHARD_SHAPE[n][1] for n in WEIGHT_PACK)
PACK_ROWS = -(-PACK_ELEMS // (1024 * 96)) * 96
PACK_HALF = PACK_ROWS // 2
PACK_BLK = PACK_HALF // 3


def _pack(shards, dtype, names):
    flat = jnp.concatenate([shards[n].astype(dtype).reshape(-1) for n in names])
    return jnp.pad(flat, (0, PACK_ROWS * 1024 - flat.shape[0]))


def _unpack(flat, names):
    out, off = {}, 0
    for n in names:
        a, b = SHARD_SHAPE[n]
        out[n] = flat[off:off + a * b].reshape(a, b)
        off += a * b
    return out


def _split_w_in(w):
    o = IN_OFF
    big = jnp.concatenate([w[:, :o[4]], w[:, o[5]:o[9]], w[:, o[11]:]], axis=1)
    small = jnp.concatenate([w[:, o[4]:o[5]], w[:, o[9]:o[11]], jnp.zeros((w.shape[0], SMALL - 32), w.dtype)], axis=1)
    return big, small


def _merge_w_in(big, small):
    return jnp.concatenate([big[:, :3072], small[:, :16], big[:, 3072:7168], small[:, 16:32], big[:, 7168:]], axis=1)


def _local_step(x, target, W, P):
    wbig, wsmall = W["w_in_big"], W["w_in_small"]
    wgate_pad = jnp.zeros((SMALL, GLA_H * GLA_DK), F32).at[:GLA_RANK].set(P["w_gla_gate"])
    cw8 = jnp.zeros((8, CONV_C), F32).at[:CONV_K].set(P["conv_w"])
    par = jnp.zeros((DN_H, 8, 128), F32)
    par = par.at[:, 0, :].set(jnp.broadcast_to(P["dn_a_log"].reshape(DN_H, 1), (DN_H, 128)))
    par = par.at[:, 1, :].set(jnp.broadcast_to(P["dn_dt_bias"].reshape(DN_H, 1), (DN_H, 128)))

    h1, n1, g1, u1 = _ffn_fwd(x, P["ffn1_norm"], W["ffn1_w_gate"], W["ffn1_w_up"], W["ffn1_w_down"], "ffn1_fwd")
    pbig, psmall, n2 = _norm_proj(h1, P["mix_norm"], wbig, wsmall, "mix_proj")
    oa, sa = _gla_fwd(pbig, psmall, wgate_pad, P["b_gla_gate"], "gla_fwd")
    conv = _conv_fwd(pbig, cw8, "conv_fwd")
    ob, sb = _gdn_fwd(conv, psmall, par, "gdn_fwd")
    h2, yb = _merge_fwd(h1, oa, ob, pbig, P["gla_head_norm"], P["dn_head_norm"], W["w_out"], "merge_fwd")
    h3, n3, g3, u3 = _ffn_fwd(h2, P["ffn2_norm"], W["ffn2_w_gate"], W["ffn2_w_up"], W["ffn2_w_down"], "ffn2_fwd")
    dh3, loss, d_final = _loss_head(h3, P["final_norm"], target, "loss_head")

    gw, gs = {}, {"final_norm": d_final}

    def ffn_grads(tag, dh, h, n, g, u):
        dx, dg, du, act, dfb, dnw = _ffn_bwd(dh, h, P[tag + "_norm"], g, u, W[tag + "_w_gate"], W[tag + "_w_up"],
                                             W[tag + "_w_down"], tag + "_bwd")
        gw[tag + "_w_gate"] = _mm_tn(n, dg, D, 1408, tag + "_dwg")
        gw[tag + "_w_up"] = _mm_tn(n, du, D, 1408, tag + "_dwu")
        gw[tag + "_w_down"] = _mm_tn(act, dfb, 1408, D, tag + "_dwd")
        gs[tag + "_norm"] = dnw
        return dx

    dh2 = ffn_grads("ffn2", dh3, h2, n3, g3, u3)
    d_oa, d_ob, d_gr, d_dgate, d_ma, d_mb, gs["gla_head_norm"], gs["dn_head_norm"], dh2b = _merge_bwd(
        dh2, oa, ob, pbig, P["gla_head_norm"], P["dn_head_norm"], W["w_out"], "merge_bwd")
    gw["w_out"] = _mm_tn(yb, dh2b, D, D, "dw_out")
    d_gq, d_gk, d_gv, dpre = _gla_bwd(pbig, psmall, wgate_pad, P["b_gla_gate"], sa, d_oa, "gla_bwd")
    ds_gla, dwgate, gs["b_gla_gate"] = _gla_gate_bwd(dpre, psmall, wgate_pad, "gla_gate_bwd")
    gw["w_gla_gate"] = dwgate[:GLA_RANK]
    dcq, dck, dcv, dsm, dpar = _gdn_bwd(conv, psmall, par, sb, d_ob, "gdn_bwd")
    d_x3, dcw = _conv_bwd(dcq, dck, dcv, pbig, cw8, "conv_bwd")
    gw["conv_w"] = dcw[:CONV_K]
    gs["dn_a_log"] = dpar[:, 0, 0].reshape(1, DN_H)
    gs["dn_dt_bias"] = dpar[:, 0, 1].reshape(1, DN_H)
    dsmall = ds_gla + jnp.concatenate([jnp.zeros((x.shape[0], GLA_RANK), F32), dsm[:, :, 0].T, dsm[:, :, 1].T,
                                       jnp.zeros((x.shape[0], SMALL - 32), F32)], axis=1)
    pieces = (d_gq, d_gk, d_gv, d_gr, d_x3, d_dgate, d_ma, d_mb)
    dh1, gs["mix_norm"] = _proj_bwd(dh2, h1, P["mix_norm"], pieces, dsmall, wbig, wsmall, "proj_bwd")
    dbig = jnp.concatenate([_mm_tn(n2, p, D, 1024, "dw_in_%d" % i) for i, p in enumerate(pieces)], axis=1)
    dsml = _mm_tn(n2, dsmall, D, SMALL, "dw_in_small")
    gw["w_in"] = _merge_w_in(dbig, dsml)
    grad_x = ffn_grads("ffn1", dh1, x, n1, g1, u1)
    return loss, grad_x, gw, gs


SMALL_NAMES = ("ffn1_norm", "mix_norm", "ffn2_norm", "final_norm", "b_gla_gate", "gla_head_norm", "dn_head_norm",
               "dn_a_log", "dn_dt_bias")
ROW4 = (("b_gla_gate", 512), ("gla_head_norm", 256), ("dn_head_norm", 128), ("dn_a_log", 8), ("dn_dt_bias", 8))


def _pack_small(d, loss=None):
    row4 = [d[n].reshape(-1) for n, _ in ROW4]
    row4.append(jnp.zeros((1,), F32) if loss is None else loss.reshape(1))
    row4 = jnp.concatenate(row4)
    row4 = jnp.pad(row4, (0, D - row4.shape[0]))
    rows = [d[n].reshape(-1) for n in SMALL_NAMES[:4]] + [row4]
    return jnp.concatenate([jnp.stack(rows), jnp.zeros((3, D), F32)], axis=0)


def _unpack_small(a, like):
    out = {n: a[i].reshape(like[n].shape) for i, n in enumerate(SMALL_NAMES[:4])}
    off = 0
    for n, w in ROW4:
        out[n] = a[4, off:off + w].reshape(like[n].shape)
        off += w
    return out, a[4, off]


WEIGHT_ORDER = ("ffn1_norm", "ffn1_w_gate", "ffn1_w_up", "ffn1_w_down", "mix_norm", "w_in", "w_gla_gate", "b_gla_gate",
                "conv_w", "dn_a_log", "dn_dt_bias", "gla_head_norm", "dn_head_norm", "w_out", "ffn2_norm",
                "ffn2_w_gate", "ffn2_w_up", "ffn2_w_down", "final_norm")
ADAM_ROWS = {"ffn1_w_gate": 256, "ffn1_w_up": 256, "ffn1_w_down": 176, "w_in": 128, "w_gla_gate": 16, "conv_w": 4,
             "w_out": 128, "ffn2_w_gate": 256, "ffn2_w_up": 256, "ffn2_w_down": 176}


def kernel(x, ffn1_norm, ffn1_w_gate, ffn1_w_up, ffn1_w_down, mix_norm, w_in, w_gla_gate, b_gla_gate, conv_w, dn_a_log, dn_dt_bias, gla_head_norm, dn_head_norm, w_out, ffn2_norm, ffn2_w_gate, ffn2_w_up, ffn2_w_down, final_norm, loss_target, m_ffn1_norm, m_ffn1_w_gate, m_ffn1_w_up, m_ffn1_w_down, m_mix_norm, m_w_in, m_w_gla_gate, m_b_gla_gate, m_conv_w, m_dn_a_log, m_dn_dt_bias, m_gla_head_norm, m_dn_head_norm, m_w_out, m_ffn2_norm, m_ffn2_w_gate, m_ffn2_w_up, m_ffn2_w_down, m_final_norm, v_ffn1_norm, v_ffn1_w_gate, v_ffn1_w_up, v_ffn1_w_down, v_mix_norm, v_w_in, v_w_gla_gate, v_b_gla_gate, v_conv_w, v_dn_a_log, v_dn_dt_bias, v_gla_head_norm, v_dn_head_norm, v_w_out, v_ffn2_norm, v_ffn2_w_gate, v_ffn2_w_up, v_ffn2_w_down, v_final_norm):
    given = dict(locals())
    wts = {n: given[n] for n in WEIGHT_ORDER}
    moms = {n: given["m_" + n] for n in WEIGHT_ORDER}
    vars_ = {n: given["v_" + n] for n in WEIGHT_ORDER}
    two_d = lambda a: a.reshape(a.shape[-2], a.shape[-1]) if a.ndim == 3 else a.reshape(1, -1)
    shard = {n: two_d(wts[n]) for n in SHARDED}

    to_send = dict(shard)
    for lo, hi in LOW_PART.items():
        to_send[lo] = shard[hi] - shard[hi].astype(BF16).astype(F32)
    pack = _pack(to_send, BF16, WEIGHT_PACK).reshape(2, PACK_HALF, 1024)
    full = _gather_weights(pack, "gather_weights").reshape(N_SHARD, PACK_ROWS * 1024)
    parts = [_unpack(full[s], WEIGHT_PACK) for s in range(N_SHARD)]
    whole = {n: jnp.concatenate([p[n] for p in parts], axis=SHARD_AXIS[n]) for n in WEIGHT_PACK}
    W = {n: whole[n] for n in SHARDED if n not in ("w_in", "w_gla_gate", "conv_w")}
    W["w_in_big"], W["w_in_small"] = _split_w_in(whole["w_in"])
    P = {n: two_d(wts[n]) for n in SMALL_NAMES}
    for lo, hi in LOW_PART.items():
        P[hi] = whole[hi].astype(F32) + whole[lo].astype(F32)

    loss, grad_x, gw, gs = _local_step(x[0], loss_target[0], W, P)

    per_shard = []
    for s in range(N_SHARD):
        pieces = {}
        for n in SHARDED:
            size = SHARD_SHAPE[n][SHARD_AXIS[n]]
            pieces[n] = lax.slice_in_dim(gw[n], s * size, (s + 1) * size, axis=SHARD_AXIS[n])
        per_shard.append(_pack(pieces, BF16, SHARDED).reshape(2, PACK_HALF, 1024))
    gp = jnp.stack(per_shard, axis=1)
    from_sibling = _sibling_swap(gp, "rs_sibling")
    chip_f32, chip_b16 = _add_pair(gp, from_sibling, "rs_add_pair")
    from_chips = _chip_exchange(chip_b16, "rs_chips")
    my_shard = 2 * lax.axis_index("x") + lax.axis_index("y")
    own = lax.dynamic_index_in_dim(chip_f32, my_shard, axis=0, keepdims=False)
    half = _add_four(own, from_chips, "rs_add_four")
    grads = _unpack(_join_halves(half, "rs_join").reshape(-1), SHARDED)

    small_sum = _allsum_small(_pack_small(gs, loss[0, 0]), "allsum_small")
    small_g, loss_total = _unpack_small(small_sum, P)

    delta, new_m, new_v = {}, {}, {}
    for n in SHARDED:
        d, m_, v_ = _adamw(shard[n], grads[n], two_d(moms[n]), two_d(vars_[n]), ADAM_ROWS[n], "adamw_" + n)
        delta[n], new_m[n], new_v[n] = (t.reshape(wts[n].shape) for t in (d, m_, v_))
    pk = lambda src: _pack_small({n: two_d(src[n]) for n in SMALL_NAMES})
    sd, sm_, sv_ = _adamw(pk(wts), small_sum, pk(moms), pk(vars_), 8, "adamw_small")
    for res, dst in ((sd, delta), (sm_, new_m), (sv_, new_v)):
        u, _ = _unpack_small(res, wts)
        dst.update(u)
    grad_w = {n: grads[n].reshape(wts[n].shape) for n in SHARDED}
    grad_w.update({n: small_g[n].reshape(wts[n].shape) for n in SMALL_NAMES})
    return (loss_total, grad_x[None], *[grad_w[n] for n in WEIGHT_ORDER], *[delta[n] for n in WEIGHT_ORDER],
            *[new_m[n] for n in WEIGHT_ORDER], *[new_v[n] for n in WEIGHT_ORDER])
```

```python
import functools
import math

import numpy as np
import jax
import jax.numpy as jnp
from jax import lax
from jax.experimental import pallas as pl
from jax.experimental.pallas import tpu as pltpu

F32 = jnp.float32
BF16 = jnp.bfloat16
HI = lax.Precision.HIGH
MESH = pl.DeviceIdType.MESH
ANY = pl.BlockSpec(memory_space=pl.ANY)

EPS = 1e-6
D = 1024
DFF = 2816
FFN_RES = 0.5
GLA_H, GLA_DK, GLA_DV, GLA_RANK, GLA_TAU = 4, 128, 256, 16, 16.0
DN_H, DN_DK, DN_DV = 8, 128, 128
CONV_K = 4
CHUNK = 64
N_SHARD = 4
ADAM_LR, ADAM_B1, ADAM_B2, ADAM_EPS, ADAM_WD, ADAM_STEP = 0.001, 0.9, 0.999, 1e-08, 0.01, 10

IN_SIZES = (512, 512, 1024, 1024, 16, 1024, 1024, 1024, 1024, 8, 8, 1024, 1024)
IN_OFF = tuple(int(v) for v in np.cumsum((0,) + IN_SIZES))
D_IN = IN_OFF[-1]
BIG = 9216
SMALL = 128
PIECES = (512, 512, 1024, 1024, 3072, 1024, 1024, 1024)

VMEM_LIMIT = 56 * 1024 * 1024
ROW_BLK = 256
ATT_BLK = 256
GDN_BLK = 256
GDN_HEADS = 4


def _cp(*sem):
    return pltpu.CompilerParams(dimension_semantics=sem, vmem_limit_bytes=VMEM_LIMIT)


def _sigmoid(x):
    return 1.0 / (1.0 + jnp.exp(-x))


def _softplus(x):
    return jnp.maximum(x, 0.0) + jnp.log(1.0 + jnp.exp(-jnp.abs(x)))


def _log_sigmoid(x):
    return jnp.minimum(x, 0.0) - jnp.log(1.0 + jnp.exp(-jnp.abs(x)))


def _dot(a, b, prec=None):
    return jnp.dot(a, b, preferred_element_type=F32, precision=prec)


def _dot_nt(a, b, prec=None):
    return lax.dot_general(a, b, (((1,), (1,)), ((), ())), preferred_element_type=F32, precision=prec)


def _dot_tn(a, b, prec=None):
    return lax.dot_general(a, b, (((0,), (0,)), ((), ())), preferred_element_type=F32, precision=prec)


def _b(x):
    return x.astype(BF16)


def _iota2(n, m, axis):
    return lax.broadcasted_iota(jnp.int32, (n, m), axis)


def _load_weights(pairs, sem):
    copies = [pltpu.make_async_copy(s, d, sem.at[i]) for i, (s, d) in enumerate(pairs)]
    for c in copies:
        c.start()
    for c in copies:
        c.wait()


def _ffn_fwd(h, nw, wg, wu, wd, name):
    T = h.shape[0]
    tm = min(ROW_BLK, T)

    def body(h_ref, nw_ref, wg_hbm, wu_hbm, wd_hbm, ho_ref, n_ref, g_ref, u_ref, wg_v, wu_v, wd_v, sem):
        @pl.when(pl.program_id(0) == 0)
        def _():
            _load_weights(((wg_hbm, wg_v), (wu_hbm, wu_v), (wd_hbm, wd_v)), sem)

        x = h_ref[...]
        r = lax.rsqrt(jnp.mean(x * x, axis=-1, keepdims=True) + EPS)
        nb = _b((x * r) * nw_ref[...])
        n_ref[...] = nb
        g = _dot(nb, wg_v[...])
        u = _dot(nb, wu_v[...])
        g_ref[...] = _b(g)
        u_ref[...] = _b(u)
        a = _b(g * _sigmoid(g) * u)
        ho_ref[...] = x + FFN_RES * _dot(a, wd_v[...])

    row = lambda w: pl.BlockSpec((tm, w), lambda i: (i, 0))
    return pl.pallas_call(
        body, name=name, grid=(T // tm,),
        in_specs=[row(D), pl.BlockSpec((1, D), lambda i: (0, 0)), ANY, ANY, ANY],
        out_specs=[row(D), row(D), row(DFF), row(DFF)],
        out_shape=[jax.ShapeDtypeStruct((T, D), F32), jax.ShapeDtypeStruct((T, D), BF16),
                   jax.ShapeDtypeStruct((T, DFF), BF16), jax.ShapeDtypeStruct((T, DFF), BF16)],
        scratch_shapes=[pltpu.VMEM((D, DFF), BF16), pltpu.VMEM((D, DFF), BF16), pltpu.VMEM((DFF, D), BF16),
                        pltpu.SemaphoreType.DMA((3,))],
        compiler_params=_cp("arbitrary"),
    )(h, nw, wg, wu, wd)


def _ffn_bwd(dh, h, nw, g, u, wg, wu, wd, name):
    T = h.shape[0]
    tm = min(ROW_BLK, T)

    def body(dh_ref, h_ref, nw_ref, g_ref, u_ref, wg_hbm, wu_hbm, wd_hbm,
             dx_ref, dg_ref, du_ref, a_ref, df_ref, dnw_ref, wg_v, wu_v, wd_v, sem):
        @pl.when(pl.program_id(0) == 0)
        def _():
            _load_weights(((wg_hbm, wg_v), (wu_hbm, wu_v), (wd_hbm, wd_v)), sem)
            dnw_ref[...] = jnp.zeros_like(dnw_ref)

        dh_ = dh_ref[...]
        dfb = _b(FFN_RES * dh_)
        df_ref[...] = dfb
        da = _dot_nt(dfb, wd_v[...])
        gg = g_ref[...].astype(F32)
        uu = u_ref[...].astype(F32)
        sg = _sigmoid(gg)
        silu = gg * sg
        a_ref[...] = _b(silu * uu)
        dgb = _b(da * uu * (sg * (1.0 + gg * (1.0 - sg))))
        dub = _b(da * silu)
        dg_ref[...] = dgb
        du_ref[...] = dub
        dn = _dot_nt(dgb, wg_v[...]) + _dot_nt(dub, wu_v[...])
        x = h_ref[...]
        r = lax.rsqrt(jnp.mean(x * x, axis=-1, keepdims=True) + EPS)
        xhat = x * r
        dnw_ref[...] += jnp.sum(dn * xhat, axis=0, keepdims=True)
        dxhat = dn * nw_ref[...]
        dx_ref[...] = dh_ + r * (dxhat - xhat * jnp.mean(dxhat * xhat, axis=-1, keepdims=True))

    row = lambda w: pl.BlockSpec((tm, w), lambda i: (i, 0))
    one = pl.BlockSpec((1, D), lambda i: (0, 0))
    return pl.pallas_call(
        body, name=name, grid=(T // tm,),
        in_specs=[row(D), row(D), one, row(DFF), row(DFF), ANY, ANY, ANY],
        out_specs=[row(D), row(DFF), row(DFF), row(DFF), row(D), one],
        out_shape=[jax.ShapeDtypeStruct((T, D), F32), jax.ShapeDtypeStruct((T, DFF), BF16),
                   jax.ShapeDtypeStruct((T, DFF), BF16), jax.ShapeDtypeStruct((T, DFF), BF16),
                   jax.ShapeDtypeStruct((T, D), BF16), jax.ShapeDtypeStruct((1, D), F32)],
        scratch_shapes=[pltpu.VMEM((D, DFF), BF16), pltpu.VMEM((D, DFF), BF16), pltpu.VMEM((DFF, D), BF16),
                        pltpu.SemaphoreType.DMA((3,))],
        compiler_params=_cp("arbitrary"),
    )(dh, h, nw, g, u, wg, wu, wd)


def _mm_tn(a, b, bm, bn, name, tk=512):
    T, M = a.shape
    N = b.shape[1]
    tk = min(tk, T)
    bm, bn = min(bm, M), min(bn, N)

    def body(a_ref, b_ref, o_ref):
        @pl.when(pl.program_id(2) == 0)
        def _():
            o_ref[...] = jnp.zeros_like(o_ref)

        o_ref[...] += _dot_tn(_b(a_ref[...]), _b(b_ref[...]))

    return pl.pallas_call(
        body, name=name, grid=(M // bm, N // bn, T // tk),
        in_specs=[pl.BlockSpec((tk, bm), lambda i, j, k: (k, i)), pl.BlockSpec((tk, bn), lambda i, j, k: (k, j))],
        out_specs=pl.BlockSpec((bm, bn), lambda i, j, k: (i, j)),
        out_shape=jax.ShapeDtypeStruct((M, N), F32),
        compiler_params=_cp("parallel", "parallel", "arbitrary"),
    )(a, b)


def _norm_proj(h, nw, wbig, wsmall, name):
    T = h.shape[0]
    tm = min(512, T)
    tn = 1536

    def body(h_ref, nw_ref, wb_ref, ws_ref, pb_ref, ps_ref, n_ref):
        @pl.when(pl.program_id(1) == 0)
        def _():
            x = h_ref[...]
            r = lax.rsqrt(jnp.mean(x * x, axis=-1, keepdims=True) + EPS)
            nb = _b((x * r) * nw_ref[...])
            n_ref[...] = nb
            ps_ref[...] = _dot(nb, ws_ref[...])

        pb_ref[...] = _b(_dot(n_ref[...], wb_ref[...]))

    return pl.pallas_call(
        body, name=name, grid=(T // tm, BIG // tn),
        in_specs=[pl.BlockSpec((tm, D), lambda i, j: (i, 0)), pl.BlockSpec((1, D), lambda i, j: (0, 0)),
                  pl.BlockSpec((D, tn), lambda i, j: (0, j)), pl.BlockSpec((D, SMALL), lambda i, j: (0, 0))],
        out_specs=[pl.BlockSpec((tm, tn), lambda i, j: (i, j)), pl.BlockSpec((tm, SMALL), lambda i, j: (i, 0)),
                   pl.BlockSpec((tm, D), lambda i, j: (i, 0))],
        out_shape=[jax.ShapeDtypeStruct((T, BIG), BF16), jax.ShapeDtypeStruct((T, SMALL), F32),
                   jax.ShapeDtypeStruct((T, D), BF16)],
        compiler_params=_cp("parallel", "arbitrary"),
    )(h, nw, wbig, wsmall)


def _proj_bwd(dh, h, nw, pieces, dsmall, wbig, wsmall, name):
    T = h.shape[0]
    tm = min(ROW_BLK, T)
    offs = tuple(int(v) for v in np.cumsum((0,) + PIECES))

    def body(dh_ref, h_ref, nw_ref, *rest):
        p_refs = rest[:len(PIECES)]
        ds_ref, wb_hbm, ws_ref, dx_ref, dnw_ref, wb_v, sem = rest[len(PIECES):]

        @pl.when(pl.program_id(0) == 0)
        def _():
            _load_weights(((wb_hbm, wb_v),), sem)
            dnw_ref[...] = jnp.zeros_like(dnw_ref)

        dn = _dot_nt(_b(ds_ref[...]), ws_ref[...])
        for p_ref, lo, wdt in zip(p_refs, offs, PIECES):
            dn += _dot_nt(p_ref[...], wb_v[:, lo:lo + wdt])
        x = h_ref[...]
        r = lax.rsqrt(jnp.mean(x * x, axis=-1, keepdims=True) + EPS)
        xhat = x * r
        dnw_ref[...] += jnp.sum(dn * xhat, axis=0, keepdims=True)
        dxhat = dn * nw_ref[...]
        dx_ref[...] = dh_ref[...] + r * (dxhat - xhat * jnp.mean(dxhat * xhat, axis=-1, keepdims=True))

    row = lambda w: pl.BlockSpec((tm, w), lambda i: (i, 0))
    one = pl.BlockSpec((1, D), lambda i: (0, 0))
    return pl.pallas_call(
        body, name=name, grid=(T // tm,),
        in_specs=[row(D), row(D), one] + [row(w) for w in PIECES] + [row(SMALL), ANY, pl.BlockSpec((D, SMALL), lambda i: (0, 0))],
        out_specs=[row(D), one],
        out_shape=[jax.ShapeDtypeStruct((T, D), F32), jax.ShapeDtypeStruct((1, D), F32)],
        scratch_shapes=[pltpu.VMEM((D, BIG), BF16), pltpu.SemaphoreType.DMA((1,))],
        compiler_params=_cp("arbitrary"),
    )(dh, h, nw, *pieces, dsmall, wbig, wsmall)


def _gla_chunk(q_ref, k_ref, sm_ref, wg_ref, bg_ref, rows, tril):
    q = q_ref[rows, :].astype(F32)
    k = k_ref[rows, :].astype(F32)
    pre = _dot(sm_ref[rows, :], wg_ref[...], HI) + bg_ref[...]
    la = _log_sigmoid(pre) * (1.0 / GLA_TAU)
    bc = _dot(tril, la, HI)
    bl = bc[CHUNK - 1:CHUNK, :]
    eb = jnp.exp(bc)
    enb = jnp.exp(-bc)
    ebl = jnp.exp(bl - bc)
    q_in = q * (GLA_DK ** -0.5) * eb
    k_out = k * enb
    k_st = k * ebl
    a_ch = jnp.exp(bl)
    return pre, eb, enb, ebl, q_in, k_out, k_st, a_ch


def _gla_specs(blk):
    return [pl.BlockSpec((blk, GLA_DK), lambda h, j: (j, h)),
            pl.BlockSpec((blk, GLA_DK), lambda h, j: (j, GLA_H + h)),
            pl.BlockSpec((blk, GLA_DV), lambda h, j: (j, GLA_H + h)),
            pl.BlockSpec((blk, SMALL), lambda h, j: (j, 0)),
            pl.BlockSpec((SMALL, GLA_DK), lambda h, j: (0, h)),
            pl.BlockSpec((1, GLA_DK), lambda h, j: (0, h))]


def _gla_fwd(pbig, psmall, wgate, bgate, name):
    T = pbig.shape[0]
    blk = min(ATT_BLK, T)
    nc = blk // CHUNK

    def body(q_ref, k_ref, v_ref, sm_ref, wg_ref, bg_ref, o_ref, ss_ref, st_ref):
        @pl.when(pl.program_id(1) == 0)
        def _():
            st_ref[...] = jnp.zeros_like(st_ref)

        causal = _iota2(CHUNK, CHUNK, 0) >= _iota2(CHUNK, CHUNK, 1)
        tril = causal.astype(F32)
        st = st_ref[...]
        for c in range(nc):
            rows = pl.ds(c * CHUNK, CHUNK)
            _, _, _, _, q_in, k_out, k_st, a_ch = _gla_chunk(q_ref, k_ref, sm_ref, wg_ref, bg_ref, rows, tril)
            v = v_ref[rows, :]
            sc = jnp.where(causal, _dot_nt(_b(q_in), _b(k_out)), 0.0)
            ss_ref[0, c] = st
            o_ref[rows, :] = _dot(_b(sc), v) + _dot_nt(_b(q_in), _b(st))
            st = st * a_ch + _dot_tn(v, _b(k_st))
        st_ref[...] = st

    return pl.pallas_call(
        body, name=name, grid=(GLA_H, T // blk),
        in_specs=_gla_specs(blk),
        out_specs=[pl.BlockSpec((blk, GLA_DV), lambda h, j: (j, h)),
                   pl.BlockSpec((1, nc, GLA_DV, GLA_DK), lambda h, j: (h, j, 0, 0))],
        out_shape=[jax.ShapeDtypeStruct((T, GLA_H * GLA_DV), F32),
                   jax.ShapeDtypeStruct((GLA_H, T // CHUNK, GLA_DV, GLA_DK), F32)],
        scratch_shapes=[pltpu.VMEM((GLA_DV, GLA_DK), F32)],
        compiler_params=_cp("parallel", "arbitrary"),
    )(pbig, pbig, pbig, psmall, wgate, bgate)


def _gla_bwd(pbig, psmall, wgate, bgate, states, do, name):
    T = pbig.shape[0]
    blk = min(ATT_BLK, T)
    nc = blk // CHUNK
    nb = T // blk

    def body(q_ref, k_ref, v_ref, sm_ref, wg_ref, bg_ref, ss_ref, do_ref, dq_ref, dk_ref, dv_ref, dpre_ref, dst_ref):
        @pl.when(pl.program_id(1) == 0)
        def _():
            dst_ref[...] = jnp.zeros_like(dst_ref)

        causal = _iota2(CHUNK, CHUNK, 0) >= _iota2(CHUNK, CHUNK, 1)
        tril = causal.astype(F32)
        triu = (_iota2(CHUNK, CHUNK, 0) <= _iota2(CHUNK, CHUNK, 1)).astype(F32)
        dst = dst_ref[...]
        for c in range(nc - 1, -1, -1):
            rows = pl.ds(c * CHUNK, CHUNK)
            pre, eb, enb, ebl, q_in, k_out, k_st, a_ch = _gla_chunk(q_ref, k_ref, sm_ref, wg_ref, bg_ref, rows, tril)
            v = v_ref[rows, :]
            st = ss_ref[0, c]
            dob = _b(do_ref[rows, :])
            sc = jnp.where(causal, _dot_nt(_b(q_in), _b(k_out)), 0.0)
            dsc = _b(jnp.where(causal, _dot_nt(dob, v), 0.0))
            dq_in = _dot(dob, _b(st)) + _dot(dsc, _b(k_out))
            dk_out = _dot_tn(dsc, _b(q_in))
            dk_st = _dot(v, _b(dst))
            dv_ref[rows, :] = _b(_dot_tn(_b(sc), dob) + _dot_nt(_b(k_st), _b(dst)))
            da_ch = jnp.sum(st * dst, axis=0, keepdims=True)
            dst = dst * a_ch + _dot_tn(dob, _b(q_in))
            tk = dk_st * k_st
            db = dq_in * q_in - dk_out * k_out - tk
            db_last = jnp.sum(tk, axis=0, keepdims=True) + da_ch * a_ch
            dq_ref[rows, :] = _b(dq_in * (GLA_DK ** -0.5) * eb)
            dk_ref[rows, :] = _b(dk_out * enb + dk_st * ebl)
            dla = _dot(triu, db, HI) + db_last
            dpre_ref[rows, :] = dla * (1.0 / GLA_TAU) * _sigmoid(-pre)
        dst_ref[...] = dst

    rev = lambda f: (lambda h, j: f(h, nb - 1 - j))
    specs = [pl.BlockSpec(s.block_shape, rev(s.index_map)) for s in _gla_specs(blk)]
    return pl.pallas_call(
        body, name=name, grid=(GLA_H, nb),
        in_specs=specs + [pl.BlockSpec((1, nc, GLA_DV, GLA_DK), lambda h, j: (h, nb - 1 - j, 0, 0)),
                          pl.BlockSpec((blk, GLA_DV), lambda h, j: (nb - 1 - j, h))],
        out_specs=[pl.BlockSpec((blk, GLA_DK), lambda h, j: (nb - 1 - j, h)),
                   pl.BlockSpec((blk, GLA_DK), lambda h, j: (nb - 1 - j, h)),
                   pl.BlockSpec((blk, GLA_DV), lambda h, j: (nb - 1 - j, h)),
                   pl.BlockSpec((blk, GLA_DK), lambda h, j: (nb - 1 - j, h))],
        out_shape=[jax.ShapeDtypeStruct((T, GLA_H * GLA_DK), BF16), jax.ShapeDtypeStruct((T, GLA_H * GLA_DK), BF16),
                   jax.ShapeDtypeStruct((T, GLA_H * GLA_DV), BF16), jax.ShapeDtypeStruct((T, GLA_H * GLA_DK), F32)],
        scratch_shapes=[pltpu.VMEM((GLA_DV, GLA_DK), F32)],
        compiler_params=_cp("parallel", "arbitrary"),
    )(pbig, pbig, pbig, psmall, wgate, bgate, states, do)


def _gla_gate_bwd(dpre, psmall, wgate, name):
    T = dpre.shape[0]
    tm = min(512, T)
    W = GLA_H * GLA_DK

    def body(dp_ref, sm_ref, wg_ref, ds_ref, dw_ref, db_ref):
        @pl.when(pl.program_id(0) == 0)
        def _():
            dw_ref[...] = jnp.zeros_like(dw_ref)
            db_ref[...] = jnp.zeros_like(db_ref)

        dp = dp_ref[...]
        ds_ref[...] = _dot_nt(dp, wg_ref[...], HI)
        dw_ref[...] += _dot_tn(sm_ref[...], dp, HI)
        db_ref[...] += jnp.sum(dp, axis=0, keepdims=True)

    return pl.pallas_call(
        body, name=name, grid=(T // tm,),
        in_specs=[pl.BlockSpec((tm, W), lambda i: (i, 0)), pl.BlockSpec((tm, SMALL), lambda i: (i, 0)),
                  pl.BlockSpec((SMALL, W), lambda i: (0, 0))],
        out_specs=[pl.BlockSpec((tm, SMALL), lambda i: (i, 0)), pl.BlockSpec((SMALL, W), lambda i: (0, 0)),
                   pl.BlockSpec((1, W), lambda i: (0, 0))],
        out_shape=[jax.ShapeDtypeStruct((T, SMALL), F32), jax.ShapeDtypeStruct((SMALL, W), F32),
                   jax.ShapeDtypeStruct((1, W), F32)],
        compiler_params=_cp("arbitrary"),
    )(dpre, psmall, wgate)


CONV_C = 3 * 1024
CONV_BLK = 256


def _conv_fwd(pbig, cw8, name):
    T = pbig.shape[0]
    blk = min(CONV_BLK, T)

    def body(x_ref, w_ref, c_ref, prev_ref):
        @pl.when(pl.program_id(0) == 0)
        def _():
            prev_ref[...] = jnp.zeros_like(prev_ref)

        x = x_ref[...].astype(F32)
        prev = prev_ref[...]
        row8 = _iota2(8, CONV_C, 0)
        acc = x * w_ref[CONV_K - 1:CONV_K, :]
        for s in range(1, CONV_K):
            xs = pltpu.roll(x, s, 0)
            top = jnp.where(row8 < s, pltpu.roll(prev, s, 0), xs[:8])
            xs = jnp.concatenate([top, xs[8:]], axis=0)
            acc += xs * w_ref[CONV_K - 1 - s:CONV_K - s, :]
        c_ref[...] = _b(acc)
        prev_ref[...] = x[blk - 8:]

    return pl.pallas_call(
        body, name=name, grid=(T // blk,),
        in_specs=[pl.BlockSpec((blk, CONV_C), lambda i: (i, 1)), pl.BlockSpec((8, CONV_C), lambda i: (0, 0))],
        out_specs=pl.BlockSpec((blk, CONV_C), lambda i: (i, 0)),
        out_shape=jax.ShapeDtypeStruct((T, CONV_C), BF16),
        scratch_shapes=[pltpu.VMEM((8, CONV_C), F32)],
        compiler_params=_cp("arbitrary"),
    )(pbig, cw8)


def _conv_bwd(dcq, dck, dcv, pbig, cw8, name):
    T = pbig.shape[0]
    blk = min(CONV_BLK, T)
    nb = T // blk

    def body(dq_ref, dk_ref, dv_ref, x_ref, w_ref, dx_ref, dw_ref, nxt_ref):
        @pl.when(pl.program_id(0) == 0)
        def _():
            nxt_ref[...] = jnp.zeros_like(nxt_ref)
            dw_ref[...] = jnp.zeros_like(dw_ref)

        dc = jnp.concatenate([dq_ref[...], dk_ref[...], dv_ref[...]], axis=1).astype(F32)
        x = x_ref[...].astype(F32)
        nxt = nxt_ref[...]
        row8 = _iota2(8, CONV_C, 0)
        acc = dc * w_ref[CONV_K - 1:CONV_K, :]
        dws = [jnp.sum(dc * x, axis=0, keepdims=True)]
        for s in range(1, CONV_K):
            ds = pltpu.roll(dc, blk - s, 0)
            bot = jnp.where(row8 >= 8 - s, pltpu.roll(nxt, 8 - s, 0), ds[blk - 8:])
            ds = jnp.concatenate([ds[:blk - 8], bot], axis=0)
            acc += ds * w_ref[CONV_K - 1 - s:CONV_K - s, :]
            dws.append(jnp.sum(ds * x, axis=0, keepdims=True))
        dx_ref[...] = _b(acc)
        dw_ref[...] += jnp.concatenate(dws[::-1] + [jnp.zeros((8 - CONV_K, CONV_C), F32)], axis=0)
        nxt_ref[...] = dc[:8]

    part = pl.BlockSpec((blk, 1024), lambda i: (nb - 1 - i, 0))
    return pl.pallas_call(
        body, name=name, grid=(nb,),
        in_specs=[part, part, part, pl.BlockSpec((blk, CONV_C), lambda i: (nb - 1 - i, 1)),
                  pl.BlockSpec((8, CONV_C), lambda i: (0, 0))],
        out_specs=[pl.BlockSpec((blk, CONV_C), lambda i: (nb - 1 - i, 0)), pl.BlockSpec((8, CONV_C), lambda i: (0, 0))],
        out_shape=[jax.ShapeDtypeStruct((T, CONV_C), BF16), jax.ShapeDtypeStruct((8, CONV_C), F32)],
        scratch_shapes=[pltpu.VMEM((8, CONV_C), F32)],
        compiler_params=_cp("arbitrary"),
    )(dcq, dck, dcv, pbig, cw8)


def _col(x, lane):
    sel = _iota2(x.shape[0], x.shape[1], 1) == lane
    return jnp.broadcast_to(jnp.sum(jnp.where(sel, x, 0.0), axis=1, keepdims=True), x.shape)


def _bmm(a, b, prec=None):
    return jnp.einsum("bij,bjk->bik", a, b, preferred_element_type=F32, precision=prec)


def _bmm_nt(a, b, prec=None):
    return jnp.einsum("bij,bkj->bik", a, b, preferred_element_type=F32, precision=prec)


def _bmm_tn(a, b, prec=None):
    return jnp.einsum("bji,bjk->bik", a, b, preferred_element_type=F32, precision=prec)


def _unit_lower_inverse(low):
    eye = (_iota2(CHUNK, CHUNK, 0) == _iota2(CHUNK, CHUNK, 1)).astype(F32)
    xk = -low
    inv = eye + xk
    for _ in range(5):
        xk = _bmm(xk, xk, HI)
        inv = inv + _bmm(inv, xk, HI)
    return inv


def _heads_first(x, nc):
    hb = x.shape[1] // 128
    return jnp.concatenate([x[:, i * 128:(i + 1) * 128].reshape(nc, CHUNK, 128) for i in range(hb)], axis=0)


def _heads_last(x, nc):
    hb = x.shape[0] // nc
    return jnp.concatenate([x[i * nc:(i + 1) * nc].reshape(nc * CHUNK, 128) for i in range(hb)], axis=1)


def _gdn_block(cq_ref, ck_ref, cv_ref, sm_ref, par_ref, h0, hb, nc, masks):
    causal, strict, tril, eye = masks
    nbat = hb * nc
    cq = _heads_first(cq_ref[...].astype(F32), nc)
    ck = _heads_first(ck_ref[...].astype(F32), nc)
    cv = _heads_first(cv_ref[...].astype(F32), nc)
    sq, sk, sv = _sigmoid(cq), _sigmoid(ck), _sigmoid(cv)
    q, k, v = cq * sq, ck * sk, cv * sv
    rq = lax.rsqrt(jnp.sum(q * q, axis=-1, keepdims=True) + EPS)
    rk = lax.rsqrt(jnp.sum(k * k, axis=-1, keepdims=True) + EPS)
    qh, kn = q * rq, k * rk
    qn = qh * (DN_DK ** -0.5)
    sm = sm_ref[...]
    per_head = lambda fn: jnp.concatenate([fn(i) for i in range(hb)], axis=0)
    braw = per_head(lambda i: _col(sm, GLA_RANK + h0 + i).reshape(nc, CHUNK, 128))
    araw = per_head(lambda i: _col(sm, GLA_RANK + DN_H + h0 + i).reshape(nc, CHUNK, 128))
    ea = per_head(lambda i: jnp.broadcast_to(jnp.exp(par_ref[i, 0:1, :])[None], (nc, 1, 128)))
    bias = per_head(lambda i: jnp.broadcast_to(par_ref[i, 1:2, :][None], (nc, 1, 128)))
    beta = _sigmoid(braw)
    sp_arg = araw + bias
    g = -ea * _softplus(sp_arg)
    G = _bmm(jnp.broadcast_to(tril, (nbat, CHUNK, CHUNK)), g, HI)
    gc = G[:, :, :CHUNK]
    grow = jnp.sum(eye * gc, axis=1, keepdims=True)
    decay = jnp.exp(jnp.where(causal, gc - grow, -1e30))
    kb = kn * beta
    A = _bmm_nt(kb, kn, HI)
    low = jnp.where(strict, A * decay, 0.0)
    tinv = _unit_lower_inverse(low)
    eG = jnp.exp(G)
    gl = G[:, CHUNK - 1:CHUNK, :]
    eGl = jnp.exp(gl - G)
    g_ch = jnp.exp(gl)
    rv = v * beta
    rkk = kb * eG
    u = _bmm(tinv, rv, HI)
    w = _bmm(tinv, rkk, HI)
    B = _bmm_nt(_b(qn), _b(kn))
    qk = jnp.where(causal, B * decay, 0.0)
    q_dec = qn * eG
    k_st = kn * eGl
    return dict(cq=cq, ck=ck, cv=cv, sq=sq, sk=sk, sv=sv, q=q, k=k, v=v, rq=rq, rk=rk, qh=qh, kn=kn, qn=qn,
                beta=beta, ea=ea, sp_arg=sp_arg, g=g, G=G, decay=decay, kb=kb, A=A, tinv=tinv, eG=eG, eGl=eGl,
                g_ch=g_ch, rv=rv, rkk=rkk, u=u, w=w, B=B, qk=qk, q_dec=q_dec, k_st=k_st)


def _gdn_masks():
    r, c = _iota2(CHUNK, CHUNK, 0), _iota2(CHUNK, CHUNK, 1)
    return r >= c, r > c, (r >= c).astype(F32), (r == c).astype(F32)


def _gdn_specs(blk, hb, idx):
    ng = DN_H // hb
    return [pl.BlockSpec((blk, hb * DN_DK), lambda h, j: (idx(j), h)),
            pl.BlockSpec((blk, hb * DN_DK), lambda h, j: (idx(j), ng + h)),
            pl.BlockSpec((blk, hb * DN_DV), lambda h, j: (idx(j), 2 * ng + h)),
            pl.BlockSpec((blk, SMALL), lambda h, j: (idx(j), 0)),
            pl.BlockSpec((hb, 8, 128), lambda h, j: (h, 0, 0))]


def _gdn_fwd(conv, psmall, par, name):
    T = conv.shape[0]
    blk = min(GDN_BLK, T)
    nc = blk // CHUNK
    hb = GDN_HEADS

    def body(cq_ref, ck_ref, cv_ref, sm_ref, par_ref, o_ref, ss_ref, s_ref):
        @pl.when(pl.program_id(1) == 0)
        def _():
            s_ref[...] = jnp.zeros_like(s_ref)

        f = _gdn_block(cq_ref, ck_ref, cv_ref, sm_ref, par_ref, pl.program_id(0) * hb, hb, nc, _gdn_masks())
        wb, qdb, kstb, qkb = _b(f["w"]), _b(f["q_dec"]), _b(f["k_st"]), _b(f["qk"])
        S = [s_ref[i] for i in range(hb)]
        for c in range(nc):
            for i in range(hb):
                n = i * nc + c
                ss_ref[i, c] = S[i]
                Sb = _b(S[i])
                v_new = _b(f["u"][n] - _dot(wb[n], Sb))
                o_ref[pl.ds(c * CHUNK, CHUNK), i * DN_DV:(i + 1) * DN_DV] = _dot(qdb[n], Sb) + _dot(qkb[n], v_new)
                S[i] = S[i] * f["g_ch"][n] + _dot_tn(kstb[n], v_new)
        for i in range(hb):
            s_ref[i] = S[i]

    return pl.pallas_call(
        body, name=name, grid=(DN_H // hb, T // blk),
        in_specs=_gdn_specs(blk, hb, lambda j: j),
        out_specs=[pl.BlockSpec((blk, hb * DN_DV), lambda h, j: (j, h)),
                   pl.BlockSpec((hb, nc, DN_DK, DN_DV), lambda h, j: (h, j, 0, 0))],
        out_shape=[jax.ShapeDtypeStruct((T, DN_H * DN_DV), F32),
                   jax.ShapeDtypeStruct((DN_H, T // CHUNK, DN_DK, DN_DV), F32)],
        scratch_shapes=[pltpu.VMEM((hb, DN_DK, DN_DV), F32)],
        compiler_params=_cp("parallel", "arbitrary"),
    )(conv, conv, conv, psmall, par)


def _gdn_bwd(conv, psmall, par, states, do, name):
    T = conv.shape[0]
    blk = min(GDN_BLK, T)
    nc = blk // CHUNK
    nb = T // blk
    hb = GDN_HEADS
    nbat = hb * nc
    rsum = lambda x: jnp.sum(x, axis=-1, keepdims=True)

    def body(cq_ref, ck_ref, cv_ref, sm_ref, par_ref, ss_ref, do_ref,
             dcq_ref, dck_ref, dcv_ref, dsm_ref, dpar_ref, ds_ref):
        @pl.when(pl.program_id(1) == 0)
        def _():
            ds_ref[...] = jnp.zeros_like(ds_ref)
            dpar_ref[...] = jnp.zeros_like(dpar_ref)

        masks = _gdn_masks()
        causal, strict, tril, eye = masks
        triu = (_iota2(CHUNK, CHUNK, 0) <= _iota2(CHUNK, CHUNK, 1)).astype(F32)
        lane = _iota2(CHUNK, 128, 1)
        last_row = _iota2(CHUNK, 128, 0) == CHUNK - 1
        f = _gdn_block(cq_ref, ck_ref, cv_ref, sm_ref, par_ref, pl.program_id(0) * hb, hb, nc, masks)
        S = ss_ref[...].reshape(nbat, DN_DK, DN_DV)
        Sb = _b(S)
        do_ = _b(_heads_first(do_ref[...], nc))
        wb, qdb, kstb, qkb = _b(f["w"]), _b(f["q_dec"]), _b(f["k_st"]), _b(f["qk"])
        vnb = _b(f["u"] - _bmm(wb, Sb))
        dvn0 = _bmm_tn(qkb, do_)
        qdo = _bmm_tn(qdb, do_)
        dS = [ds_ref[i] for i in range(hb)]
        after = [None] * nbat
        for c in range(nc - 1, -1, -1):
            for i in range(hb):
                n = i * nc + c
                after[n] = dS[i]
                dvn_c = _b(dvn0[n] + _dot(kstb[n], _b(dS[i])))
                dS[i] = dS[i] * f["g_ch"][n] + qdo[n] - _dot_tn(wb[n], dvn_c)
        for i in range(hb):
            ds_ref[i] = dS[i]
        dSa = jnp.stack(after)
        dSb = _b(dSa)
        dvn = dvn0 + _bmm(kstb, dSb)
        dvnb = _b(dvn)
        dq_dec = _bmm_nt(do_, Sb)
        dqk = jnp.where(causal, _bmm_nt(do_, vnb), 0.0)
        dk_st = _bmm_nt(vnb, dSb)
        dg_ch = jnp.sum(rsum(S * dSa), axis=1, keepdims=True)
        dw = -_bmm_nt(dvnb, Sb)
        drv = _bmm_tn(f["tinv"], dvn, HI)
        drk = _bmm_tn(f["tinv"], dw, HI)
        dlow = jnp.where(strict, -(_bmm_nt(drv, f["u"], HI) + _bmm_nt(drk, f["w"], HI)), 0.0)
        dv = drv * f["beta"]
        dbeta = rsum(drv * f["v"])
        dkb = drk * f["eG"]
        dG = rsum(drk * f["rkk"])
        dA = dlow * f["decay"]
        ddec = dlow * f["A"]
        dkb += _bmm(dA, f["kn"], HI)
        dkn = _bmm_tn(dA, f["kb"], HI)
        dB = dqk * f["decay"]
        ddec += dqk * f["B"]
        dqn = _bmm(_b(dB), _b(f["kn"]))
        dkn += _bmm_tn(_b(dB), _b(f["qn"]))
        dD = ddec * f["decay"]
        dG += rsum(dD) - rsum(eye * jnp.sum(dD, axis=1, keepdims=True))
        dqn += dq_dec * f["eG"]
        dG += rsum(dq_dec * f["q_dec"])
        dkn += dk_st * f["eGl"]
        tks = rsum(dk_st * f["k_st"])
        dG -= tks
        dG_last = jnp.sum(tks, axis=1, keepdims=True) + dg_ch * f["g_ch"][:, :, :1]
        dkn += dkb * f["beta"]
        dbeta += rsum(dkb * f["kn"])
        dGf = jnp.broadcast_to(dG, (nbat, CHUNK, 128)) + jnp.where(last_row, dG_last, 0.0)
        dg = _bmm(jnp.broadcast_to(triu, (nbat, CHUNK, CHUNK)), dGf, HI)
        dbraw = dbeta * f["beta"][:, :, :1] * (1.0 - f["beta"][:, :, :1])
        daraw = dg * (-f["ea"]) * _sigmoid(f["sp_arg"])
        dsm = jnp.where(lane == 0, dbraw, jnp.where(lane == 1, daraw, 0.0))
        both = lambda t: jnp.sum(jnp.sum(t, axis=1, keepdims=True), axis=0)
        dgg = dg * f["g"]
        for i in range(hb):
            mine = slice(i * nc, (i + 1) * nc)
            dsm_ref[i] = dsm[mine].reshape(blk, 128)
            dpar = jnp.where(lane[:1] == 0, both(dgg[mine]), jnp.where(lane[:1] == 1, both(daraw[mine]), 0.0))
            dpar_ref[i] += jnp.broadcast_to(dpar, (8, 128))
        dqh = dqn * (DN_DK ** -0.5)
        dq = f["rq"] * (dqh - f["qh"] * rsum(dqh * f["qh"]))
        dk = f["rk"] * (dkn - f["kn"] * rsum(dkn * f["kn"]))
        dsilu = lambda x, s: s * (1.0 + x * (1.0 - s))
        dcq_ref[...] = _b(_heads_last(dq * dsilu(f["cq"], f["sq"]), nc))
        dck_ref[...] = _b(_heads_last(dk * dsilu(f["ck"], f["sk"]), nc))
        dcv_ref[...] = _b(_heads_last(dv * dsilu(f["cv"], f["sv"]), nc))

    r = lambda j: nb - 1 - j
    out_blk = pl.BlockSpec((blk, hb * DN_DK), lambda h, j: (r(j), h))
    return pl.pallas_call(
        body, name=name, grid=(DN_H // hb, nb),
        in_specs=_gdn_specs(blk, hb, r) + [pl.BlockSpec((hb, nc, DN_DK, DN_DV), lambda h, j: (h, r(j), 0, 0)),
                                          pl.BlockSpec((blk, hb * DN_DV), lambda h, j: (r(j), h))],
        out_specs=[out_blk, out_blk, out_blk, pl.BlockSpec((hb, blk, 128), lambda h, j: (h, r(j), 0)),
                   pl.BlockSpec((hb, 8, 128), lambda h, j: (h, 0, 0))],
        out_shape=[jax.ShapeDtypeStruct((T, DN_H * DN_DK), BF16)] * 3 + [
            jax.ShapeDtypeStruct((DN_H, T, 128), F32), jax.ShapeDtypeStruct((DN_H, 8, 128), F32)],
        scratch_shapes=[pltpu.VMEM((hb, DN_DK, DN_DV), F32)],
        compiler_params=_cp("parallel", "arbitrary"),
    )(conv, conv, conv, psmall, par, states, do)


def _head_norm(o, w, dv):
    outs, rs = [], []
    for i in range(o.shape[1] // dv):
        oh = o[:, i * dv:(i + 1) * dv]
        r = lax.rsqrt(jnp.mean(oh * oh, axis=-1, keepdims=True) + EPS)
        outs.append(oh * r)
        rs.append(r)
    return outs, rs


def _merge_specs(tm):
    col = lambda c: pl.BlockSpec((tm, D), lambda i: (i, c))
    return [col(0), col(0), col(2), col(6), col(7), col(8),
            pl.BlockSpec((1, GLA_DV), lambda i: (0, 0)), pl.BlockSpec((1, DN_DV), lambda i: (0, 0)),
            pl.BlockSpec((D, D), lambda i: (0, 0))]


def _merge_fwd(h, oa, ob, pbig, gla_hn, dn_hn, wout, name):
    T = h.shape[0]
    tm = min(ROW_BLK, T)

    def body(h_ref, oa_ref, ob_ref, gr_ref, dg_ref, ma_ref, mb_ref, wa_ref, wb_ref, wo_ref, ho_ref, y_ref):
        na, _ = _head_norm(oa_ref[...], wa_ref[...], GLA_DV)
        nbs, _ = _head_norm(ob_ref[...], wb_ref[...], DN_DV)
        hna = jnp.concatenate([t * wa_ref[...] for t in na], axis=1)
        hnb = jnp.concatenate([t * wb_ref[...] for t in nbs], axis=1)
        gr = gr_ref[...].astype(F32)
        dg = dg_ref[...].astype(F32)
        y = (_sigmoid(ma_ref[...].astype(F32)) * hna * (gr * _sigmoid(gr))
             + _sigmoid(mb_ref[...].astype(F32)) * hnb * (dg * _sigmoid(dg)))
        yb = _b(y)
        y_ref[...] = yb
        ho_ref[...] = h_ref[...] + _dot(yb, wo_ref[...])

    row = pl.BlockSpec((tm, D), lambda i: (i, 0))
    return pl.pallas_call(
        body, name=name, grid=(T // tm,),
        in_specs=[row] + _merge_specs(tm),
        out_specs=[row, row],
        out_shape=[jax.ShapeDtypeStruct((T, D), F32), jax.ShapeDtypeStruct((T, D), BF16)],
        compiler_params=_cp("arbitrary"),
    )(h, oa, ob, pbig, pbig, pbig, pbig, gla_hn, dn_hn, wout)


def _merge_bwd(dh, oa, ob, pbig, gla_hn, dn_hn, wout, name):
    T = dh.shape[0]
    tm = min(ROW_BLK, T)

    def branch(dy, o_ref, w_ref, gate_ref, m_ref, dv):
        w = w_ref[...]
        ohat, rs = _head_norm(o_ref[...], w, dv)
        gate = gate_ref[...].astype(F32)
        m = m_ref[...].astype(F32)
        sgate, sm = _sigmoid(gate), _sigmoid(m)
        silu = gate * sgate
        ohat_all = jnp.concatenate(ohat, axis=1)
        hn = jnp.concatenate([t * w for t in ohat], axis=1)
        d_on = dy * sm
        d_m = dy * hn * silu * sm * (1.0 - sm)
        d_hn = d_on * silu
        d_gate = d_on * hn * (sgate * (1.0 + gate * (1.0 - sgate)))
        dw = jnp.zeros((1, dv), F32)
        d_o = []
        for i, (oh, r) in enumerate(zip(ohat, rs)):
            dhn = d_hn[:, i * dv:(i + 1) * dv]
            dw += jnp.sum(dhn * oh, axis=0, keepdims=True)
            dohat = dhn * w
            d_o.append(r * (dohat - oh * jnp.mean(dohat * oh, axis=-1, keepdims=True)))
        return jnp.concatenate(d_o, axis=1), d_gate, d_m, dw

    def body(dh_ref, oa_ref, ob_ref, gr_ref, dg_ref, ma_ref, mb_ref, wa_ref, wb_ref, wo_ref,
             doa_ref, dob_ref, dgr_ref, ddg_ref, dma_ref, dmb_ref, dwa_ref, dwb_ref, dhb_ref):
        @pl.when(pl.program_id(0) == 0)
        def _():
            dwa_ref[...] = jnp.zeros_like(dwa_ref)
            dwb_ref[...] = jnp.zeros_like(dwb_ref)

        dhb = _b(dh_ref[...])
        dhb_ref[...] = dhb
        dy = _dot_nt(dhb, wo_ref[...])
        d_oa, d_gr, d_ma, dwa = branch(dy, oa_ref, wa_ref, gr_ref, ma_ref, GLA_DV)
        d_ob, d_dg, d_mb, dwb = branch(dy, ob_ref, wb_ref, dg_ref, mb_ref, DN_DV)
        doa_ref[...] = d_oa
        dob_ref[...] = d_ob
        dgr_ref[...] = _b(d_gr)
        ddg_ref[...] = _b(d_dg)
        dma_ref[...] = _b(d_ma)
        dmb_ref[...] = _b(d_mb)
        dwa_ref[...] += dwa
        dwb_ref[...] += dwb

    row = pl.BlockSpec((tm, D), lambda i: (i, 0))
    f32 = jax.ShapeDtypeStruct((T, D), F32)
    b16 = jax.ShapeDtypeStruct((T, D), BF16)
    return pl.pallas_call(
        body, name=name, grid=(T // tm,),
        in_specs=[row] + _merge_specs(tm),
        out_specs=[row] * 6 + [pl.BlockSpec((1, GLA_DV), lambda i: (0, 0)), pl.BlockSpec((1, DN_DV), lambda i: (0, 0)), row],
        out_shape=[f32, f32, b16, b16, b16, b16, jax.ShapeDtypeStruct((1, GLA_DV), F32),
                   jax.ShapeDtypeStruct((1, DN_DV), F32), b16],
        compiler_params=_cp("arbitrary"),
    )(dh, oa, ob, pbig, pbig, pbig, pbig, gla_hn, dn_hn, wout)


def _loss_head(h, nw, target, name):
    T = h.shape[0]
    tm = min(512, T)

    def body(h_ref, nw_ref, t_ref, dx_ref, loss_ref, dnw_ref):
        @pl.when(pl.program_id(0) == 0)
        def _():
            loss_ref[...] = jnp.zeros_like(loss_ref)
            dnw_ref[...] = jnp.zeros_like(dnw_ref)

        x = h_ref[...]
        w = nw_ref[...]
        r = lax.rsqrt(jnp.mean(x * x, axis=-1, keepdims=True) + EPS)
        xhat = x * r
        err = xhat * w - t_ref[...]
        part = jnp.sum(jnp.sum(err * err, axis=-1, keepdims=True), axis=0, keepdims=True)
        loss_ref[...] += (0.5 / D) * part
        dout = err * (1.0 / D)
        dnw_ref[...] += jnp.sum(dout * xhat, axis=0, keepdims=True)
        dxhat = dout * w
        dx_ref[...] = r * (dxhat - xhat * jnp.mean(dxhat * xhat, axis=-1, keepdims=True))

    row = pl.BlockSpec((tm, D), lambda i: (i, 0))
    one = pl.BlockSpec((1, D), lambda i: (0, 0))
    return pl.pallas_call(
        body, name=name, grid=(T // tm,),
        in_specs=[row, one, row],
        out_specs=[row, pl.BlockSpec((8, 128), lambda i: (0, 0)), one],
        out_shape=[jax.ShapeDtypeStruct((T, D), F32), jax.ShapeDtypeStruct((8, 128), F32),
                   jax.ShapeDtypeStruct((1, D), F32)],
        compiler_params=_cp("arbitrary"),
    )(h, nw, target)


def _adamw(w, g, m, v, rows, name):
    R, C = w.shape
    rows = min(rows, R)
    c1 = 1.0 - ADAM_B1 ** ADAM_STEP
    c2 = 1.0 - ADAM_B2 ** ADAM_STEP

    def body(w_ref, g_ref, m_ref, v_ref, d_ref, mo_ref, vo_ref):
        g_ = g_ref[...]
        m_ = ADAM_B1 * m_ref[...] + (1.0 - ADAM_B1) * g_
        v_ = ADAM_B2 * v_ref[...] + (1.0 - ADAM_B2) * (g_ * g_)
        mo_ref[...] = m_
        vo_ref[...] = v_
        d_ref[...] = -ADAM_LR * ((m_ / c1) / (jnp.sqrt(v_ / c2) + ADAM_EPS) + ADAM_WD * w_ref[...])

    blk = pl.BlockSpec((rows, C), lambda i: (i, 0))
    shp = jax.ShapeDtypeStruct((R, C), F32)
    return pl.pallas_call(
        body, name=name, grid=(R // rows,),
        in_specs=[blk] * 4, out_specs=[blk] * 3, out_shape=[shp] * 3,
        compiler_params=_cp("parallel"),
    )(w, g, m, v)


def _me():
    return lax.axis_index("x"), lax.axis_index("y"), lax.axis_index("c")


def _other_chips(x, y):
    return [(1 - x, y), (x, 1 - y), (1 - x, 1 - y)]


def _gather_weights(pack, name):
    _, rh, cols = pack.shape

    def body(p_ref, o_ref, send_sems, recv_sems, local_sem):
        x, y, c = _me()
        sibling = (x, y, 1 - c)
        chips = _other_chips(x, y)

        def copy(k, chip, half, to, src=None):
            dst = o_ref.at[2 * chip[0] + chip[1], half]
            return pltpu.make_async_remote_copy(src_ref=dst if src is None else src, dst_ref=dst,
                                                send_sem=send_sems.at[k], recv_sem=recv_sems.at[k],
                                                device_id=to, device_id_type=MESH)

        mine = pltpu.make_async_copy(p_ref, o_ref.at[2 * x + y], local_sem)
        mine.start()
        first = [copy(j, (x, y), c, (*chip, c), src=p_ref.at[c]) for j, chip in enumerate(chips)]
        for cp in first:
            cp.start()
        passed = [copy(3 + j, chip, c, sibling) for j, chip in enumerate(chips)]
        for j, chip in enumerate(chips):
            copy(j, chip, c, (x, y, c)).wait_recv()
            passed[j].start()
        for j, chip in enumerate(chips):
            copy(3 + j, chip, 1 - c, (x, y, c)).wait_recv()
        for cp in first + passed:
            cp.wait_send()
        mine.wait()

    return pl.pallas_call(
        body, name=name, in_specs=[ANY], out_specs=ANY,
        out_shape=jax.ShapeDtypeStruct((N_SHARD, 2, rh, cols), pack.dtype),
        scratch_shapes=[pltpu.SemaphoreType.DMA((6,)), pltpu.SemaphoreType.DMA((6,)), pltpu.SemaphoreType.DMA],
        compiler_params=pltpu.CompilerParams(has_side_effects=True),
    )(pack)


def _sibling_swap(gp, name):
    _, ns, rh, cols = gp.shape

    def body(g_ref, o_ref, send_sem, recv_sem):
        x, y, c = _me()
        cp = pltpu.make_async_remote_copy(src_ref=g_ref.at[1 - c], dst_ref=o_ref, send_sem=send_sem, recv_sem=recv_sem,
                                          device_id=(x, y, 1 - c), device_id_type=MESH)
        cp.start()
        cp.wait()

    return pl.pallas_call(
        body, name=name, in_specs=[ANY], out_specs=ANY,
        out_shape=jax.ShapeDtypeStruct((ns, rh, cols), gp.dtype),
        scratch_shapes=[pltpu.SemaphoreType.DMA, pltpu.SemaphoreType.DMA],
        compiler_params=pltpu.CompilerParams(has_side_effects=True),
    )(gp)


def _add_pair(gp, other, name):
    _, ns, rh, cols = gp.shape
    rb = PACK_BLK

    def body(a_ref, b_ref, f_ref, h_ref):
        s = a_ref[0, 0].astype(F32) + b_ref[0].astype(F32)
        f_ref[0] = s
        h_ref[0] = _b(s)

    c = lax.axis_index("c")
    return pl.pallas_call(
        body, name=name, grid=(ns, rh // rb),
        in_specs=[pl.BlockSpec((1, 1, rb, cols), lambda s, i: (0, s, i, 0)), pl.BlockSpec((1, rb, cols), lambda s, i: (s, i, 0))],
        out_specs=[pl.BlockSpec((1, rb, cols), lambda s, i: (s, i, 0))] * 2,
        out_shape=[jax.ShapeDtypeStruct((ns, rh, cols), F32), jax.ShapeDtypeStruct((ns, rh, cols), BF16)],
        compiler_params=_cp("parallel", "parallel"),
    )(lax.dynamic_slice_in_dim(gp, c, 1, axis=0), other)


def _chip_exchange(pb, name):
    ns, rh, cols = pb.shape

    def body(p_ref, o_ref, send_sems, recv_sems):
        x, y, c = _me()
        cps = [pltpu.make_async_remote_copy(src_ref=p_ref.at[2 * chip[0] + chip[1]], dst_ref=o_ref.at[j],
                                            send_sem=send_sems.at[j], recv_sem=recv_sems.at[j],
                                            device_id=(*chip, c), device_id_type=MESH)
               for j, chip in enumerate(_other_chips(x, y))]
        for cp in cps:
            cp.start()
        for cp in cps:
            cp.wait()

    return pl.pallas_call(
        body, name=name, in_specs=[ANY], out_specs=ANY,
        out_shape=jax.ShapeDtypeStruct((3, rh, cols), pb.dtype),
        scratch_shapes=[pltpu.SemaphoreType.DMA((3,)), pltpu.SemaphoreType.DMA((3,))],
        compiler_params=pltpu.CompilerParams(has_side_effects=True),
    )(pb)


def _add_four(own, got, name):
    rh, cols = own.shape
    rb = PACK_BLK

    def body(a_ref, b_ref, o_ref):
        o_ref[...] = ((a_ref[...] + b_ref[0].astype(F32)) + b_ref[1].astype(F32)) + b_ref[2].astype(F32)

    return pl.pallas_call(
        body, name=name, grid=(rh // rb,),
        in_specs=[pl.BlockSpec((rb, cols), lambda i: (i, 0)), pl.BlockSpec((3, rb, cols), lambda i: (0, i, 0))],
        out_specs=pl.BlockSpec((rb, cols), lambda i: (i, 0)),
        out_shape=jax.ShapeDtypeStruct((rh, cols), F32),
        compiler_params=_cp("parallel"),
    )(own, got)


def _join_halves(half, name):
    rh, cols = half.shape

    def body(h_ref, o_ref, send_sem, recv_sem, local_sem):
        x, y, c = _me()
        mine = pltpu.make_async_copy(h_ref, o_ref.at[c], local_sem)
        mine.start()
        cp = pltpu.make_async_remote_copy(src_ref=h_ref, dst_ref=o_ref.at[c], send_sem=send_sem, recv_sem=recv_sem,
                                          device_id=(x, y, 1 - c), device_id_type=MESH)
        cp.start()
        cp.wait_send()
        pltpu.make_async_remote_copy(src_ref=h_ref, dst_ref=o_ref.at[1 - c], send_sem=send_sem, recv_sem=recv_sem,
                                     device_id=(x, y, 1 - c), device_id_type=MESH).wait_recv()
        mine.wait()

    return pl.pallas_call(
        body, name=name, in_specs=[ANY], out_specs=ANY,
        out_shape=jax.ShapeDtypeStruct((2, rh, cols), half.dtype),
        scratch_shapes=[pltpu.SemaphoreType.DMA, pltpu.SemaphoreType.DMA, pltpu.SemaphoreType.DMA],
        compiler_params=pltpu.CompilerParams(has_side_effects=True),
    )(half)


def _allsum_small(vec, name):
    def body(v_ref, o_ref, buf_ref, send_sems, recv_sems):
        x, y, c = _me()
        me = 4 * x + 2 * y + c
        buf_ref[me] = v_ref[...]
        cps = []
        for k in range(1, 8):
            peer = (x ^ (k >> 2), y ^ ((k >> 1) & 1), c ^ (k & 1))
            cps.append(pltpu.make_async_remote_copy(src_ref=v_ref, dst_ref=buf_ref.at[me],
                                                    send_sem=send_sems.at[k - 1], recv_sem=recv_sems.at[k - 1],
                                                    device_id=peer, device_id_type=MESH))
        for cp in cps:
            cp.start()
        for k in range(1, 8):
            peer_idx = me ^ k
            pltpu.make_async_remote_copy(src_ref=v_ref, dst_ref=buf_ref.at[peer_idx],
                                         send_sem=send_sems.at[k - 1], recv_sem=recv_sems.at[k - 1],
                                         device_id=(x, y, c), device_id_type=MESH).wait_recv()
        for cp in cps:
            cp.wait_send()
        acc = buf_ref[0]
        for d in range(1, 8):
            acc = acc + buf_ref[d]
        o_ref[...] = acc

    return pl.pallas_call(
        body, name=name,
        in_specs=[pl.BlockSpec(memory_space=pltpu.VMEM)], out_specs=pl.BlockSpec(memory_space=pltpu.VMEM),
        out_shape=jax.ShapeDtypeStruct(vec.shape, F32),
        scratch_shapes=[pltpu.VMEM((8,) + vec.shape, F32), pltpu.SemaphoreType.DMA((7,)), pltpu.SemaphoreType.DMA((7,))],
        compiler_params=pltpu.CompilerParams(has_side_effects=True),
    )(vec)


SHARDED = ("ffn1_w_gate", "ffn1_w_up", "ffn1_w_down", "w_in", "w_gla_gate", "conv_w", "w_out",
           "ffn2_w_gate", "ffn2_w_up", "ffn2_w_down")
SHARD_SHAPE = {"ffn1_w_gate": (D, DFF // 4), "ffn1_w_up": (D, DFF // 4), "ffn1_w_down": (DFF // 4, D),
               "w_in": (D, D_IN // 4), "w_gla_gate": (GLA_RANK, 128), "conv_w": (CONV_K, 768), "w_out": (256, D),
               "ffn2_w_gate": (D, DFF // 4), "ffn2_w_up": (D, DFF // 4), "ffn2_w_down": (DFF // 4, D)}
SHARD_AXIS = {"ffn1_w_gate": 1, "ffn1_w_up": 1, "ffn1_w_down": 0, "w_in": 1, "w_gla_gate": 1, "conv_w": 1, "w_out": 0,
              "ffn2_w_gate": 1, "ffn2_w_up": 1, "ffn2_w_down": 0}
LOW_PART = {"w_gla_gate_lo": "w_gla_gate", "conv_w_lo": "conv_w"}
for _lo, _hi in LOW_PART.items():
    SHARD_SHAPE[_lo] = SHARD_SHAPE[_hi]
    SHARD_AXIS[_lo] = SHARD_AXIS[_hi]
WEIGHT_PACK = SHARDED + tuple(LOW_PART)
PACK_ELEMS = sum(SHARD_SHAPE[n][0] * SHARD_SHAPE[n][1] for n in WEIGHT_PACK)
PACK_ROWS = -(-PACK_ELEMS // (1024 * 96)) * 96
PACK_HALF = PACK_ROWS // 2
PACK_BLK = PACK_HALF // 3


def _pack(shards, dtype, names):
    flat = jnp.concatenate([shards[n].astype(dtype).reshape(-1) for n in names])
    return jnp.pad(flat, (0, PACK_ROWS * 1024 - flat.shape[0]))


def _unpack(flat, names):
    out, off = {}, 0
    for n in names:
        a, b = SHARD_SHAPE[n]
        out[n] = flat[off:off + a * b].reshape(a, b)
        off += a * b
    return out


def _split_w_in(w):
    o = IN_OFF
    big = jnp.concatenate([w[:, :o[4]], w[:, o[5]:o[9]], w[:, o[11]:]], axis=1)
    small = jnp.concatenate([w[:, o[4]:o[5]], w[:, o[9]:o[11]], jnp.zeros((w.shape[0], SMALL - 32), w.dtype)], axis=1)
    return big, small


def _merge_w_in(big, small):
    return jnp.concatenate([big[:, :3072], small[:, :16], big[:, 3072:7168], small[:, 16:32], big[:, 7168:]], axis=1)


def _local_step(x, target, W, P):
    wbig, wsmall = W["w_in_big"], W["w_in_small"]
    wgate_pad = jnp.zeros((SMALL, GLA_H * GLA_DK), F32).at[:GLA_RANK].set(P["w_gla_gate"])
    cw8 = jnp.zeros((8, CONV_C), F32).at[:CONV_K].set(P["conv_w"])
    par = jnp.zeros((DN_H, 8, 128), F32)
    par = par.at[:, 0, :].set(jnp.broadcast_to(P["dn_a_log"].reshape(DN_H, 1), (DN_H, 128)))
    par = par.at[:, 1, :].set(jnp.broadcast_to(P["dn_dt_bias"].reshape(DN_H, 1), (DN_H, 128)))

    h1, n1, g1, u1 = _ffn_fwd(x, P["ffn1_norm"], W["ffn1_w_gate"], W["ffn1_w_up"], W["ffn1_w_down"], "ffn1_fwd")
    pbig, psmall, n2 = _norm_proj(h1, P["mix_norm"], wbig, wsmall, "mix_proj")
    oa, sa = _gla_fwd(pbig, psmall, wgate_pad, P["b_gla_gate"], "gla_fwd")
    conv = _conv_fwd(pbig, cw8, "conv_fwd")
    ob, sb = _gdn_fwd(conv, psmall, par, "gdn_fwd")
    h2, yb = _merge_fwd(h1, oa, ob, pbig, P["gla_head_norm"], P["dn_head_norm"], W["w_out"], "merge_fwd")
    h3, n3, g3, u3 = _ffn_fwd(h2, P["ffn2_norm"], W["ffn2_w_gate"], W["ffn2_w_up"], W["ffn2_w_down"], "ffn2_fwd")
    dh3, loss, d_final = _loss_head(h3, P["final_norm"], target, "loss_head")

    gw, gs = {}, {"final_norm": d_final}

    def ffn_grads(tag, dh, h, n, g, u):
        dx, dg, du, act, dfb, dnw = _ffn_bwd(dh, h, P[tag + "_norm"], g, u, W[tag + "_w_gate"], W[tag + "_w_up"],
                                             W[tag + "_w_down"], tag + "_bwd")
        gw[tag + "_w_gate"] = _mm_tn(n, dg, D, 1408, tag + "_dwg")
        gw[tag + "_w_up"] = _mm_tn(n, du, D, 1408, tag + "_dwu")
        gw[tag + "_w_down"] = _mm_tn(act, dfb, 1408, D, tag + "_dwd")
        gs[tag + "_norm"] = dnw
        return dx

    dh2 = ffn_grads("ffn2", dh3, h2, n3, g3, u3)
    d_oa, d_ob, d_gr, d_dgate, d_ma, d_mb, gs["gla_head_norm"], gs["dn_head_norm"], dh2b = _merge_bwd(
        dh2, oa, ob, pbig, P["gla_head_norm"], P["dn_head_norm"], W["w_out"], "merge_bwd")
    gw["w_out"] = _mm_tn(yb, dh2b, D, D, "dw_out")
    d_gq, d_gk, d_gv, dpre = _gla_bwd(pbig, psmall, wgate_pad, P["b_gla_gate"], sa, d_oa, "gla_bwd")
    ds_gla, dwgate, gs["b_gla_gate"] = _gla_gate_bwd(dpre, psmall, wgate_pad, "gla_gate_bwd")
    gw["w_gla_gate"] = dwgate[:GLA_RANK]
    dcq, dck, dcv, dsm, dpar = _gdn_bwd(conv, psmall, par, sb, d_ob, "gdn_bwd")
    d_x3, dcw = _conv_bwd(dcq, dck, dcv, pbig, cw8, "conv_bwd")
    gw["conv_w"] = dcw[:CONV_K]
    gs["dn_a_log"] = dpar[:, 0, 0].reshape(1, DN_H)
    gs["dn_dt_bias"] = dpar[:, 0, 1].reshape(1, DN_H)
    dsmall = ds_gla + jnp.concatenate([jnp.zeros((x.shape[0], GLA_RANK), F32), dsm[:, :, 0].T, dsm[:, :, 1].T,
                                       jnp.zeros((x.shape[0], SMALL - 32), F32)], axis=1)
    pieces = (d_gq, d_gk, d_gv, d_gr, d_x3, d_dgate, d_ma, d_mb)
    dh1, gs["mix_norm"] = _proj_bwd(dh2, h1, P["mix_norm"], pieces, dsmall, wbig, wsmall, "proj_bwd")
    dbig = jnp.concatenate([_mm_tn(n2, p, D, 1024, "dw_in_%d" % i) for i, p in enumerate(pieces)], axis=1)
    dsml = _mm_tn(n2, dsmall, D, SMALL, "dw_in_small")
    gw["w_in"] = _merge_w_in(dbig, dsml)
    grad_x = ffn_grads("ffn1", dh1, x, n1, g1, u1)
    return loss, grad_x, gw, gs


SMALL_NAMES = ("ffn1_norm", "mix_norm", "ffn2_norm", "final_norm", "b_gla_gate", "gla_head_norm", "dn_head_norm",
               "dn_a_log", "dn_dt_bias")
ROW4 = (("b_gla_gate", 512), ("gla_head_norm", 256), ("dn_head_norm", 128), ("dn_a_log", 8), ("dn_dt_bias", 8))


def _pack_small(d, loss=None):
    row4 = [d[n].reshape(-1) for n, _ in ROW4]
    row4.append(jnp.zeros((1,), F32) if loss is None else loss.reshape(1))
    row4 = jnp.concatenate(row4)
    row4 = jnp.pad(row4, (0, D - row4.shape[0]))
    rows = [d[n].reshape(-1) for n in SMALL_NAMES[:4]] + [row4]
    return jnp.concatenate([jnp.stack(rows), jnp.zeros((3, D), F32)], axis=0)


def _unpack_small(a, like):
    out = {n: a[i].reshape(like[n].shape) for i, n in enumerate(SMALL_NAMES[:4])}
    off = 0
    for n, w in ROW4:
        out[n] = a[4, off:off + w].reshape(like[n].shape)
        off += w
    return out, a[4, off]


WEIGHT_ORDER = ("ffn1_norm", "ffn1_w_gate", "ffn1_w_up", "ffn1_w_down", "mix_norm", "w_in", "w_gla_gate", "b_gla_gate",
                "conv_w", "dn_a_log", "dn_dt_bias", "gla_head_norm", "dn_head_norm", "w_out", "ffn2_norm",
                "ffn2_w_gate", "ffn2_w_up", "ffn2_w_down", "final_norm")
ADAM_ROWS = {"ffn1_w_gate": 256, "ffn1_w_up": 256, "ffn1_w_down": 176, "w_in": 128, "w_gla_gate": 16, "conv_w": 4,
             "w_out": 128, "ffn2_w_gate": 256, "ffn2_w_up": 256, "ffn2_w_down": 176}


def kernel(x, ffn1_norm, ffn1_w_gate, ffn1_w_up, ffn1_w_down, mix_norm, w_in, w_gla_gate, b_gla_gate, conv_w, dn_a_log, dn_dt_bias, gla_head_norm, dn_head_norm, w_out, ffn2_norm, ffn2_w_gate, ffn2_w_up, ffn2_w_down, final_norm, loss_target, m_ffn1_norm, m_ffn1_w_gate, m_ffn1_w_up, m_ffn1_w_down, m_mix_norm, m_w_in, m_w_gla_gate, m_b_gla_gate, m_conv_w, m_dn_a_log, m_dn_dt_bias, m_gla_head_norm, m_dn_head_norm, m_w_out, m_ffn2_norm, m_ffn2_w_gate, m_ffn2_w_up, m_ffn2_w_down, m_final_norm, v_ffn1_norm, v_ffn1_w_gate, v_ffn1_w_up, v_ffn1_w_down, v_mix_norm, v_w_in, v_w_gla_gate, v_b_gla_gate, v_conv_w, v_dn_a_log, v_dn_dt_bias, v_gla_head_norm, v_dn_head_norm, v_w_out, v_ffn2_norm, v_ffn2_w_gate, v_ffn2_w_up, v_ffn2_w_down, v_final_norm):
    given = dict(locals())
    wts = {n: given[n] for n in WEIGHT_ORDER}
    moms = {n: given["m_" + n] for n in WEIGHT_ORDER}
    vars_ = {n: given["v_" + n] for n in WEIGHT_ORDER}
    two_d = lambda a: a.reshape(a.shape[-2], a.shape[-1]) if a.ndim == 3 else a.reshape(1, -1)
    shard = {n: two_d(wts[n]) for n in SHARDED}

    to_send = dict(shard)
    for lo, hi in LOW_PART.items():
        to_send[lo] = shard[hi] - shard[hi].astype(BF16).astype(F32)
    pack = _pack(to_send, BF16, WEIGHT_PACK).reshape(2, PACK_HALF, 1024)
    full = _gather_weights(pack, "gather_weights").reshape(N_SHARD, PACK_ROWS * 1024)
    parts = [_unpack(full[s], WEIGHT_PACK) for s in range(N_SHARD)]
    whole = {n: jnp.concatenate([p[n] for p in parts], axis=SHARD_AXIS[n]) for n in WEIGHT_PACK}
    W = {n: whole[n] for n in SHARDED if n not in ("w_in", "w_gla_gate", "conv_w")}
    W["w_in_big"], W["w_in_small"] = _split_w_in(whole["w_in"])
    P = {n: two_d(wts[n]) for n in SMALL_NAMES}
    for lo, hi in LOW_PART.items():
        P[hi] = whole[hi].astype(F32) + whole[lo].astype(F32)

    loss, grad_x, gw, gs = _local_step(x[0], loss_target[0], W, P)

    per_shard = []
    for s in range(N_SHARD):
        pieces = {}
        for n in SHARDED:
            size = SHARD_SHAPE[n][SHARD_AXIS[n]]
            pieces[n] = lax.slice_in_dim(gw[n], s * size, (s + 1) * size, axis=SHARD_AXIS[n])
        per_shard.append(_pack(pieces, BF16, SHARDED).reshape(2, PACK_HALF, 1024))
    gp = jnp.stack(per_shard, axis=1)
    from_sibling = _sibling_swap(gp, "rs_sibling")
    chip_f32, chip_b16 = _add_pair(gp, from_sibling, "rs_add_pair")
    from_chips = _chip_exchange(chip_b16, "rs_chips")
    my_shard = 2 * lax.axis_index("x") + lax.axis_index("y")
    own = lax.dynamic_index_in_dim(chip_f32, my_shard, axis=0, keepdims=False)
    half = _add_four(own, from_chips, "rs_add_four")
    grads = _unpack(_join_halves(half, "rs_join").reshape(-1), SHARDED)

    small_sum = _allsum_small(_pack_small(gs, loss[0, 0]), "allsum_small")
    small_g, loss_total = _unpack_small(small_sum, P)

    delta, new_m, new_v = {}, {}, {}
    for n in SHARDED:
        d, m_, v_ = _adamw(shard[n], grads[n], two_d(moms[n]), two_d(vars_[n]), ADAM_ROWS[n], "adamw_" + n)
        delta[n], new_m[n], new_v[n] = (t.reshape(wts[n].shape) for t in (d, m_, v_))
    pk = lambda src: _pack_small({n: two_d(src[n]) for n in SMALL_NAMES})
    sd, sm_, sv_ = _adamw(pk(wts), small_sum, pk(moms), pk(vars_), 8, "adamw_small")
    for res, dst in ((sd, delta), (sm_, new_m), (sv_, new_v)):
        u, _ = _unpack_small(res, wts)
        dst.update(u)
    grad_w = {n: grads[n].reshape(wts[n].shape) for n in SHARDED}
    grad_w.update({n: small_g[n].reshape(wts[n].shape) for n in SMALL_NAMES})
    return (loss_total, grad_x[None], *[grad_w[n] for n in WEIGHT_ORDER], *[delta[n] for n in WEIGHT_ORDER],
            *[new_m[n] for n in WEIGHT_ORDER], *[new_v[n] for n in WEIGHT_ORDER])
```

```python
import functools
import math

import numpy as np
import jax
import jax.numpy as jnp
from jax import lax
from jax.experimental import pallas as pl
from jax.experimental.pallas import tpu as pltpu

F32 = jnp.float32
BF16 = jnp.bfloat16
HI = lax.Precision.HIGH
MESH = pl.DeviceIdType.MESH
ANY = pl.BlockSpec(memory_space=pl.ANY)

EPS = 1e-6
D = 1024
DFF = 2816
FFN_RES = 0.5
GLA_H, GLA_DK, GLA_DV, GLA_RANK, GLA_TAU = 4, 128, 256, 16, 16.0
DN_H, DN_DK, DN_DV = 8, 128, 128
CONV_K = 4
CHUNK = 64
N_SHARD = 4
FF_CUT = DFF // N_SHARD
ADAM_LR, ADAM_B1, ADAM_B2, ADAM_EPS, ADAM_WD, ADAM_STEP = 0.001, 0.9, 0.999, 1e-08, 0.01, 10

IN_SIZES = (512, 512, 1024, 1024, 16, 1024, 1024, 1024, 1024, 8, 8, 1024, 1024)
IN_OFF = tuple(int(v) for v in np.cumsum((0,) + IN_SIZES))
D_IN = IN_OFF[-1]
BIG_COLS = 9216
SMALL = 128
PIECES = (512, 512, 1024, 1024, 3072, 1024, 1024, 1024)

VMEM_LIMIT = 56 * 1024 * 1024
ROW_BLK = 256
ATT_BLK = 256
GDN_BLK = 128
GDN_HEADS = 8


def _cp(*sem):
    return pltpu.CompilerParams(dimension_semantics=sem, vmem_limit_bytes=VMEM_LIMIT)


def _sigmoid(x):
    return 1.0 / (1.0 + jnp.exp(-x))


def _softplus(x):
    return jnp.maximum(x, 0.0) + jnp.log(1.0 + jnp.exp(-jnp.abs(x)))


def _log_sigmoid(x):
    return jnp.minimum(x, 0.0) - jnp.log(1.0 + jnp.exp(-jnp.abs(x)))


def _dot(a, b, prec=None):
    return jnp.dot(a, b, preferred_element_type=F32, precision=prec)


def _dot_nt(a, b, prec=None):
    return lax.dot_general(a, b, (((1,), (1,)), ((), ())), preferred_element_type=F32, precision=prec)


def _dot_tn(a, b, prec=None):
    return lax.dot_general(a, b, (((0,), (0,)), ((), ())), preferred_element_type=F32, precision=prec)


def _b(x):
    return x.astype(BF16)


def _iota2(n, m, axis):
    return lax.broadcasted_iota(jnp.int32, (n, m), axis)


def _load_weights(pairs, sem):
    copies = [pltpu.make_async_copy(s, d, sem.at[i]) for i, (s, d) in enumerate(pairs)]
    for c in copies:
        c.start()
    for c in copies:
        c.wait()


def _ffn_fwd(h, nw, wg, wu, wd, name):
    T = h.shape[0]
    tm = min(ROW_BLK, T)

    def body(h_ref, nw_ref, wg_hbm, wu_hbm, wd_hbm, ho_ref, n_ref, g_ref, u_ref, wg_v, wu_v, wd_v, sem):
        @pl.when(pl.program_id(0) == 0)
        def _():
            _load_weights(((wg_hbm, wg_v), (wu_hbm, wu_v), (wd_hbm, wd_v)), sem)

        x = h_ref[...]
        r = lax.rsqrt(jnp.mean(x * x, axis=-1, keepdims=True) + EPS)
        nb = _b((x * r) * nw_ref[...])
        n_ref[...] = nb
        acc = jnp.zeros((tm, D), F32)
        for s in range(N_SHARD):
            g = _dot(nb, wg_v[s])
            u = _dot(nb, wu_v[s])
            g_ref[s] = _b(g)
            u_ref[s] = _b(u)
            acc += _dot(_b(g * _sigmoid(g) * u), wd_v[s])
        ho_ref[...] = x + FFN_RES * acc

    row = lambda w: pl.BlockSpec((tm, w), lambda i: (i, 0))
    cut = pl.BlockSpec((N_SHARD, tm, FF_CUT), lambda i: (0, i, 0))
    return pl.pallas_call(
        body, name=name, grid=(T // tm,),
        in_specs=[row(D), pl.BlockSpec((1, D), lambda i: (0, 0)), ANY, ANY, ANY],
        out_specs=[row(D), row(D), cut, cut],
        out_shape=[jax.ShapeDtypeStruct((T, D), F32), jax.ShapeDtypeStruct((T, D), BF16),
                   jax.ShapeDtypeStruct((N_SHARD, T, FF_CUT), BF16), jax.ShapeDtypeStruct((N_SHARD, T, FF_CUT), BF16)],
        scratch_shapes=[pltpu.VMEM((N_SHARD, D, FF_CUT), BF16), pltpu.VMEM((N_SHARD, D, FF_CUT), BF16),
                        pltpu.VMEM((N_SHARD, FF_CUT, D), BF16), pltpu.SemaphoreType.DMA((3,))],
        compiler_params=_cp("arbitrary"),
    )(h, nw, wg, wu, wd)


def _ffn_bwd(dh, h, nw, g, u, wg, wu, wd, name):
    T = h.shape[0]
    tm = min(ROW_BLK, T)

    def body(dh_ref, h_ref, nw_ref, g_ref, u_ref, wg_hbm, wu_hbm, wd_hbm,
             dx_ref, dg_ref, du_ref, a_ref, df_ref, dnw_ref, wg_v, wu_v, wd_v, sem):
        @pl.when(pl.program_id(0) == 0)
        def _():
            _load_weights(((wg_hbm, wg_v), (wu_hbm, wu_v), (wd_hbm, wd_v)), sem)
            dnw_ref[...] = jnp.zeros_like(dnw_ref)

        dh_ = dh_ref[...]
        dfb = _b(FFN_RES * dh_)
        df_ref[...] = dfb
        dn = jnp.zeros((tm, D), F32)
        for s in range(N_SHARD):
            da = _dot_nt(dfb, wd_v[s])
            gg = g_ref[s].astype(F32)
            uu = u_ref[s].astype(F32)
            sg = _sigmoid(gg)
            silu = gg * sg
            a_ref[s] = _b(silu * uu)
            dgb = _b(da * uu * (sg * (1.0 + gg * (1.0 - sg))))
            dub = _b(da * silu)
            dg_ref[s] = dgb
            du_ref[s] = dub
            dn += _dot_nt(dgb, wg_v[s]) + _dot_nt(dub, wu_v[s])
        x = h_ref[...]
        r = lax.rsqrt(jnp.mean(x * x, axis=-1, keepdims=True) + EPS)
        xhat = x * r
        dnw_ref[...] += jnp.sum(dn * xhat, axis=0, keepdims=True)
        dxhat = dn * nw_ref[...]
        dx_ref[...] = dh_ + r * (dxhat - xhat * jnp.mean(dxhat * xhat, axis=-1, keepdims=True))

    row = lambda w: pl.BlockSpec((tm, w), lambda i: (i, 0))
    one = pl.BlockSpec((1, D), lambda i: (0, 0))
    cut = pl.BlockSpec((N_SHARD, tm, FF_CUT), lambda i: (0, i, 0))
    cut_shape = jax.ShapeDtypeStruct((N_SHARD, T, FF_CUT), BF16)
    return pl.pallas_call(
        body, name=name, grid=(T // tm,),
        in_specs=[row(D), row(D), one, cut, cut, ANY, ANY, ANY],
        out_specs=[row(D), cut, cut, cut, row(D), one],
        out_shape=[jax.ShapeDtypeStruct((T, D), F32), cut_shape, cut_shape, cut_shape,
                   jax.ShapeDtypeStruct((T, D), BF16), jax.ShapeDtypeStruct((1, D), F32)],
        scratch_shapes=[pltpu.VMEM((N_SHARD, D, FF_CUT), BF16), pltpu.VMEM((N_SHARD, D, FF_CUT), BF16),
                        pltpu.VMEM((N_SHARD, FF_CUT, D), BF16), pltpu.SemaphoreType.DMA((3,))],
        compiler_params=_cp("arbitrary"),
    )(dh, h, nw, g, u, wg, wu, wd)


def _mm_tn(a, b, bm, bn, name, out_dtype=BF16, tk=512):
    cuts = a.shape[0] if a.ndim == 3 else (b.shape[0] if b.ndim == 3 else None)
    T, M = a.shape[-2:]
    N = b.shape[-1]
    tk = min(tk, T)
    bm, bn = min(bm, M), min(bn, N)
    nk = T // tk

    def body(a_ref, b_ref, o_ref, acc_ref):
        k = pl.program_id(3)

        @pl.when(k == 0)
        def _():
            acc_ref[...] = jnp.zeros_like(acc_ref)

        av = a_ref[0] if a.ndim == 3 else a_ref[...]
        bv = b_ref[0] if b.ndim == 3 else b_ref[...]
        acc_ref[...] += _dot_tn(_b(av), _b(bv))

        @pl.when(k == nk - 1)
        def _():
            res = acc_ref[...].astype(out_dtype)
            if cuts is None:
                o_ref[...] = res
            else:
                o_ref[0] = res

    a_spec = (pl.BlockSpec((1, tk, bm), lambda s, i, j, k: (s, k, i)) if a.ndim == 3
              else pl.BlockSpec((tk, bm), lambda s, i, j, k: (k, i)))
    b_spec = (pl.BlockSpec((1, tk, bn), lambda s, i, j, k: (s, k, j)) if b.ndim == 3
              else pl.BlockSpec((tk, bn), lambda s, i, j, k: (k, j)))
    if cuts is None:
        o_spec, o_shape = pl.BlockSpec((bm, bn), lambda s, i, j, k: (i, j)), (M, N)
    else:
        o_spec, o_shape = pl.BlockSpec((1, bm, bn), lambda s, i, j, k: (s, i, j)), (cuts, M, N)
    return pl.pallas_call(
        body, name=name, grid=(cuts or 1, M // bm, N // bn, nk),
        in_specs=[a_spec, b_spec], out_specs=o_spec,
        out_shape=jax.ShapeDtypeStruct(o_shape, out_dtype),
        scratch_shapes=[pltpu.VMEM((bm, bn), F32)],
        compiler_params=_cp("parallel", "parallel", "parallel", "arbitrary"),
    )(a, b)


def _norm_proj(h, nw, wbig, wsmall, name):
    T = h.shape[0]
    tm = min(512, T)
    tn = 1536

    def body(h_ref, nw_ref, wb_ref, ws_ref, pb_ref, ps_ref, n_ref):
        @pl.when(pl.program_id(1) == 0)
        def _():
            x = h_ref[...]
            r = lax.rsqrt(jnp.mean(x * x, axis=-1, keepdims=True) + EPS)
            nb = _b((x * r) * nw_ref[...])
            n_ref[...] = nb
            ps_ref[...] = _dot(nb, ws_ref[...])

        pb_ref[...] = _b(_dot(n_ref[...], wb_ref[...]))

    return pl.pallas_call(
        body, name=name, grid=(T // tm, BIG_COLS // tn),
        in_specs=[pl.BlockSpec((tm, D), lambda i, j: (i, 0)), pl.BlockSpec((1, D), lambda i, j: (0, 0)),
                  pl.BlockSpec((D, tn), lambda i, j: (0, j)), pl.BlockSpec((D, SMALL), lambda i, j: (0, 0))],
        out_specs=[pl.BlockSpec((tm, tn), lambda i, j: (i, j)), pl.BlockSpec((tm, SMALL), lambda i, j: (i, 0)),
                   pl.BlockSpec((tm, D), lambda i, j: (i, 0))],
        out_shape=[jax.ShapeDtypeStruct((T, BIG_COLS), BF16), jax.ShapeDtypeStruct((T, SMALL), F32),
                   jax.ShapeDtypeStruct((T, D), BF16)],
        compiler_params=_cp("parallel", "arbitrary"),
    )(h, nw, wbig, wsmall)


def _proj_bwd(dh, h, nw, pieces, dsmall, wbig, wsmall, name):
    T = h.shape[0]
    tm = min(ROW_BLK, T)
    offs = tuple(int(v) for v in np.cumsum((0,) + PIECES))

    def body(dh_ref, h_ref, nw_ref, *rest):
        p_refs = rest[:len(PIECES)]
        ds_ref, wb_hbm, ws_ref, dx_ref, dnw_ref, wb_v, sem = rest[len(PIECES):]

        @pl.when(pl.program_id(0) == 0)
        def _():
            _load_weights(((wb_hbm, wb_v),), sem)
            dnw_ref[...] = jnp.zeros_like(dnw_ref)

        dn = _dot_nt(_b(ds_ref[...]), ws_ref[...])
        for p_ref, lo, wdt in zip(p_refs, offs, PIECES):
            dn += _dot_nt(p_ref[...], wb_v[:, lo:lo + wdt])
        x = h_ref[...]
        r = lax.rsqrt(jnp.mean(x * x, axis=-1, keepdims=True) + EPS)
        xhat = x * r
        dnw_ref[...] += jnp.sum(dn * xhat, axis=0, keepdims=True)
        dxhat = dn * nw_ref[...]
        dx_ref[...] = dh_ref[...] + r * (dxhat - xhat * jnp.mean(dxhat * xhat, axis=-1, keepdims=True))

    row = lambda w: pl.BlockSpec((tm, w), lambda i: (i, 0))
    one = pl.BlockSpec((1, D), lambda i: (0, 0))
    return pl.pallas_call(
        body, name=name, grid=(T // tm,),
        in_specs=[row(D), row(D), one] + [row(w) for w in PIECES] + [row(SMALL), ANY, pl.BlockSpec((D, SMALL), lambda i: (0, 0))],
        out_specs=[row(D), one],
        out_shape=[jax.ShapeDtypeStruct((T, D), F32), jax.ShapeDtypeStruct((1, D), F32)],
        scratch_shapes=[pltpu.VMEM((D, BIG_COLS), BF16), pltpu.SemaphoreType.DMA((1,))],
        compiler_params=_cp("arbitrary"),
    )(dh, h, nw, *pieces, dsmall, wbig, wsmall)


def _gla_chunk(q_ref, k_ref, sm_ref, wg_ref, bg_ref, rows, tril):
    q = q_ref[rows, :].astype(F32)
    k = k_ref[rows, :].astype(F32)
    pre = _dot(sm_ref[rows, :], wg_ref[...], HI) + bg_ref[...]
    la = _log_sigmoid(pre) * (1.0 / GLA_TAU)
    bc = _dot(tril, la, HI)
    bl = bc[CHUNK - 1:CHUNK, :]
    eb = jnp.exp(bc)
    enb = jnp.exp(-bc)
    ebl = jnp.exp(bl - bc)
    q_in = q * (GLA_DK ** -0.5) * eb
    k_out = k * enb
    k_st = k * ebl
    a_ch = jnp.exp(bl)
    return pre, eb, enb, ebl, q_in, k_out, k_st, a_ch


def _gla_specs(blk):
    return [pl.BlockSpec((blk, GLA_DK), lambda h, j: (j, h)),
            pl.BlockSpec((blk, GLA_DK), lambda h, j: (j, GLA_H + h)),
            pl.BlockSpec((blk, GLA_DV), lambda h, j: (j, GLA_H + h)),
            pl.BlockSpec((blk, SMALL), lambda h, j: (j, 0)),
            pl.BlockSpec((SMALL, GLA_DK), lambda h, j: (0, h)),
            pl.BlockSpec((1, GLA_DK), lambda h, j: (0, h))]


def _gla_fwd(pbig, psmall, wgate, bgate, name):
    T = pbig.shape[0]
    blk = min(ATT_BLK, T)
    nc = blk // CHUNK

    def body(q_ref, k_ref, v_ref, sm_ref, wg_ref, bg_ref, o_ref, ss_ref, st_ref):
        @pl.when(pl.program_id(1) == 0)
        def _():
            st_ref[...] = jnp.zeros_like(st_ref)

        causal = _iota2(CHUNK, CHUNK, 0) >= _iota2(CHUNK, CHUNK, 1)
        tril = causal.astype(F32)
        st = st_ref[...]
        for c in range(nc):
            rows = pl.ds(c * CHUNK, CHUNK)
            _, _, _, _, q_in, k_out, k_st, a_ch = _gla_chunk(q_ref, k_ref, sm_ref, wg_ref, bg_ref, rows, tril)
            v = v_ref[rows, :]
            sc = jnp.where(causal, _dot_nt(_b(q_in), _b(k_out)), 0.0)
            ss_ref[0, c] = st
            o_ref[rows, :] = _dot(_b(sc), v) + _dot_nt(_b(q_in), _b(st))
            st = st * a_ch + _dot_tn(v, _b(k_st))
        st_ref[...] = st

    return pl.pallas_call(
        body, name=name, grid=(GLA_H, T // blk),
        in_specs=_gla_specs(blk),
        out_specs=[pl.BlockSpec((blk, GLA_DV), lambda h, j: (j, h)),
                   pl.BlockSpec((1, nc, GLA_DV, GLA_DK), lambda h, j: (h, j, 0, 0))],
        out_shape=[jax.ShapeDtypeStruct((T, GLA_H * GLA_DV), F32),
                   jax.ShapeDtypeStruct((GLA_H, T // CHUNK, GLA_DV, GLA_DK), F32)],
        scratch_shapes=[pltpu.VMEM((GLA_DV, GLA_DK), F32)],
        compiler_params=_cp("parallel", "arbitrary"),
    )(pbig, pbig, pbig, psmall, wgate, bgate)


def _gla_bwd(pbig, psmall, wgate, bgate, states, do, name):
    T = pbig.shape[0]
    blk = min(ATT_BLK, T)
    nc = blk // CHUNK
    nb = T // blk

    def body(q_ref, k_ref, v_ref, sm_ref, wg_ref, bg_ref, ss_ref, do_ref, dq_ref, dk_ref, dv_ref, dpre_ref, dst_ref):
        @pl.when(pl.program_id(1) == 0)
        def _():
            dst_ref[...] = jnp.zeros_like(dst_ref)

        causal = _iota2(CHUNK, CHUNK, 0) >= _iota2(CHUNK, CHUNK, 1)
        tril = causal.astype(F32)
        triu = (_iota2(CHUNK, CHUNK, 0) <= _iota2(CHUNK, CHUNK, 1)).astype(F32)
        dst = dst_ref[...]
        for c in range(nc - 1, -1, -1):
            rows = pl.ds(c * CHUNK, CHUNK)
            pre, eb, enb, ebl, q_in, k_out, k_st, a_ch = _gla_chunk(q_ref, k_ref, sm_ref, wg_ref, bg_ref, rows, tril)
            v = v_ref[rows, :]
            st = ss_ref[0, c]
            dob = _b(do_ref[rows, :])
            sc = jnp.where(causal, _dot_nt(_b(q_in), _b(k_out)), 0.0)
            dsc = _b(jnp.where(causal, _dot_nt(dob, v), 0.0))
            dq_in = _dot(dob, _b(st)) + _dot(dsc, _b(k_out))
            dk_out = _dot_tn(dsc, _b(q_in))
            dk_st = _dot(v, _b(dst))
            dv_ref[rows, :] = _b(_dot_tn(_b(sc), dob) + _dot_nt(_b(k_st), _b(dst)))
            da_ch = jnp.sum(st * dst, axis=0, keepdims=True)
            dst = dst * a_ch + _dot_tn(dob, _b(q_in))
            tk = dk_st * k_st
            db = dq_in * q_in - dk_out * k_out - tk
            db_last = jnp.sum(tk, axis=0, keepdims=True) + da_ch * a_ch
            dq_ref[rows, :] = _b(dq_in * (GLA_DK ** -0.5) * eb)
            dk_ref[rows, :] = _b(dk_out * enb + dk_st * ebl)
            dla = _dot(triu, db, HI) + db_last
            dpre_ref[rows, :] = dla * (1.0 / GLA_TAU) * _sigmoid(-pre)
        dst_ref[...] = dst

    rev = lambda f: (lambda h, j: f(h, nb - 1 - j))
    specs = [pl.BlockSpec(s.block_shape, rev(s.index_map)) for s in _gla_specs(blk)]
    return pl.pallas_call(
        body, name=name, grid=(GLA_H, nb),
        in_specs=specs + [pl.BlockSpec((1, nc, GLA_DV, GLA_DK), lambda h, j: (h, nb - 1 - j, 0, 0)),
                          pl.BlockSpec((blk, GLA_DV), lambda h, j: (nb - 1 - j, h))],
        out_specs=[pl.BlockSpec((blk, GLA_DK), lambda h, j: (nb - 1 - j, h)),
                   pl.BlockSpec((blk, GLA_DK), lambda h, j: (nb - 1 - j, h)),
                   pl.BlockSpec((blk, GLA_DV), lambda h, j: (nb - 1 - j, h)),
                   pl.BlockSpec((blk, GLA_DK), lambda h, j: (nb - 1 - j, h))],
        out_shape=[jax.ShapeDtypeStruct((T, GLA_H * GLA_DK), BF16), jax.ShapeDtypeStruct((T, GLA_H * GLA_DK), BF16),
                   jax.ShapeDtypeStruct((T, GLA_H * GLA_DV), BF16), jax.ShapeDtypeStruct((T, GLA_H * GLA_DK), F32)],
        scratch_shapes=[pltpu.VMEM((GLA_DV, GLA_DK), F32)],
        compiler_params=_cp("parallel", "arbitrary"),
    )(pbig, pbig, pbig, psmall, wgate, bgate, states, do)


def _gla_gate_bwd(dpre, psmall, wgate, dsm, name):
    T = dpre.shape[0]
    tm = min(512, T)
    W = GLA_H * GLA_DK
    ngrp = dsm.shape[0]

    def body(dp_ref, sm_ref, wg_ref, dsm_ref, ds_ref, dw_ref, db_ref):
        @pl.when(pl.program_id(0) == 0)
        def _():
            dw_ref[...] = jnp.zeros_like(dw_ref)
            db_ref[...] = jnp.zeros_like(db_ref)

        dp = dp_ref[...]
        ds = _dot_nt(dp, wg_ref[...], HI)
        for i in range(ngrp):
            ds += dsm_ref[i]
        ds_ref[...] = ds
        dw_ref[...] += _dot_tn(sm_ref[...], dp, HI)
        db_ref[...] += jnp.sum(dp, axis=0, keepdims=True)

    return pl.pallas_call(
        body, name=name, grid=(T // tm,),
        in_specs=[pl.BlockSpec((tm, W), lambda i: (i, 0)), pl.BlockSpec((tm, SMALL), lambda i: (i, 0)),
                  pl.BlockSpec((SMALL, W), lambda i: (0, 0)), pl.BlockSpec((ngrp, tm, SMALL), lambda i: (0, i, 0))],
        out_specs=[pl.BlockSpec((tm, SMALL), lambda i: (i, 0)), pl.BlockSpec((SMALL, W), lambda i: (0, 0)),
                   pl.BlockSpec((1, W), lambda i: (0, 0))],
        out_shape=[jax.ShapeDtypeStruct((T, SMALL), F32), jax.ShapeDtypeStruct((SMALL, W), F32),
                   jax.ShapeDtypeStruct((1, W), F32)],
        compiler_params=_cp("arbitrary"),
    )(dpre, psmall, wgate, dsm)


CONV_C = 3 * 1024
CONV_BLK = 256


def _conv_fwd(pbig, cw8, name):
    T = pbig.shape[0]
    blk = min(CONV_BLK, T)

    def body(x_ref, w_ref, c_ref, prev_ref):
        @pl.when(pl.program_id(0) == 0)
        def _():
            prev_ref[...] = jnp.zeros_like(prev_ref)

        x = x_ref[...].astype(F32)
        prev = prev_ref[...]
        row8 = _iota2(8, CONV_C, 0)
        acc = x * w_ref[CONV_K - 1:CONV_K, :]
        for s in range(1, CONV_K):
            xs = pltpu.roll(x, s, 0)
            top = jnp.where(row8 < s, pltpu.roll(prev, s, 0), xs[:8])
            xs = jnp.concatenate([top, xs[8:]], axis=0)
            acc += xs * w_ref[CONV_K - 1 - s:CONV_K - s, :]
        c_ref[...] = _b(acc)
        prev_ref[...] = x[blk - 8:]

    return pl.pallas_call(
        body, name=name, grid=(T // blk,),
        in_specs=[pl.BlockSpec((blk, CONV_C), lambda i: (i, 1)), pl.BlockSpec((8, CONV_C), lambda i: (0, 0))],
        out_specs=pl.BlockSpec((blk, CONV_C), lambda i: (i, 0)),
        out_shape=jax.ShapeDtypeStruct((T, CONV_C), BF16),
        scratch_shapes=[pltpu.VMEM((8, CONV_C), F32)],
        compiler_params=_cp("arbitrary"),
    )(pbig, cw8)


def _conv_bwd(dcq, dck, dcv, pbig, cw8, name):
    T = pbig.shape[0]
    blk = min(CONV_BLK, T)
    nb = T // blk

    def body(dq_ref, dk_ref, dv_ref, x_ref, w_ref, dx_ref, dw_ref, nxt_ref):
        @pl.when(pl.program_id(0) == 0)
        def _():
            nxt_ref[...] = jnp.zeros_like(nxt_ref)
            dw_ref[...] = jnp.zeros_like(dw_ref)

        dc = jnp.concatenate([dq_ref[...], dk_ref[...], dv_ref[...]], axis=1).astype(F32)
        x = x_ref[...].astype(F32)
        nxt = nxt_ref[...]
        row8 = _iota2(8, CONV_C, 0)
        acc = dc * w_ref[CONV_K - 1:CONV_K, :]
        dws = [jnp.sum(dc * x, axis=0, keepdims=True)]
        for s in range(1, CONV_K):
            ds = pltpu.roll(dc, blk - s, 0)
            bot = jnp.where(row8 >= 8 - s, pltpu.roll(nxt, 8 - s, 0), ds[blk - 8:])
            ds = jnp.concatenate([ds[:blk - 8], bot], axis=0)
            acc += ds * w_ref[CONV_K - 1 - s:CONV_K - s, :]
            dws.append(jnp.sum(ds * x, axis=0, keepdims=True))
        dx_ref[...] = _b(acc)
        dw_ref[...] += jnp.concatenate(dws[::-1] + [jnp.zeros((8 - CONV_K, CONV_C), F32)], axis=0)
        nxt_ref[...] = dc[:8]

    part = pl.BlockSpec((blk, 1024), lambda i: (nb - 1 - i, 0))
    return pl.pallas_call(
        body, name=name, grid=(nb,),
        in_specs=[part, part, part, pl.BlockSpec((blk, CONV_C), lambda i: (nb - 1 - i, 1)),
                  pl.BlockSpec((8, CONV_C), lambda i: (0, 0))],
        out_specs=[pl.BlockSpec((blk, CONV_C), lambda i: (nb - 1 - i, 0)), pl.BlockSpec((8, CONV_C), lambda i: (0, 0))],
        out_shape=[jax.ShapeDtypeStruct((T, CONV_C), BF16), jax.ShapeDtypeStruct((8, CONV_C), F32)],
        scratch_shapes=[pltpu.VMEM((8, CONV_C), F32)],
        compiler_params=_cp("arbitrary"),
    )(dcq, dck, dcv, pbig, cw8)


def _col(x, lane):
    sel = _iota2(x.shape[0], x.shape[1], 1) == lane
    return jnp.broadcast_to(jnp.sum(jnp.where(sel, x, 0.0), axis=1, keepdims=True), x.shape)


def _bmm(a, b, prec=None):
    return jnp.einsum("bij,bjk->bik", a, b, preferred_element_type=F32, precision=prec)


def _bmm_nt(a, b, prec=None):
    return jnp.einsum("bij,bkj->bik", a, b, preferred_element_type=F32, precision=prec)


def _bmm_tn(a, b, prec=None):
    return jnp.einsum("bji,bjk->bik", a, b, preferred_element_type=F32, precision=prec)


def _unit_lower_inverse(low):
    eye = (_iota2(CHUNK, CHUNK, 0) == _iota2(CHUNK, CHUNK, 1)).astype(F32)
    xk = -low
    inv = eye + xk
    for _ in range(5):
        xk = _bmm(xk, xk, HI)
        inv = inv + _bmm(inv, xk, HI)
    return inv


def _heads_first(x, nc):
    hb = x.shape[1] // 128
    return jnp.concatenate([x[:, i * 128:(i + 1) * 128].reshape(nc, CHUNK, 128) for i in range(hb)], axis=0)


def _heads_last(x, nc):
    hb = x.shape[0] // nc
    return jnp.concatenate([x[i * nc:(i + 1) * nc].reshape(nc * CHUNK, 128) for i in range(hb)], axis=1)


def _gdn_block(cq_ref, ck_ref, cv_ref, sm_ref, par_ref, h0, hb, nc, masks):
    causal, strict, tril, eye = masks
    nbat = hb * nc
    cq = _heads_first(cq_ref[...].astype(F32), nc)
    ck = _heads_first(ck_ref[...].astype(F32), nc)
    cv = _heads_first(cv_ref[...].astype(F32), nc)
    sq, sk, sv = _sigmoid(cq), _sigmoid(ck), _sigmoid(cv)
    q, k, v = cq * sq, ck * sk, cv * sv
    rq = lax.rsqrt(jnp.sum(q * q, axis=-1, keepdims=True) + EPS)
    rk = lax.rsqrt(jnp.sum(k * k, axis=-1, keepdims=True) + EPS)
    qh, kn = q * rq, k * rk
    qn = qh * (DN_DK ** -0.5)
    sm = sm_ref[...]
    per_head = lambda fn: jnp.concatenate([fn(i) for i in range(hb)], axis=0)
    braw = per_head(lambda i: _col(sm, GLA_RANK + h0 + i).reshape(nc, CHUNK, 128))
    araw = per_head(lambda i: _col(sm, GLA_RANK + DN_H + h0 + i).reshape(nc, CHUNK, 128))
    ea = per_head(lambda i: jnp.broadcast_to(jnp.exp(par_ref[i, 0:1, :])[None], (nc, 1, 128)))
    bias = per_head(lambda i: jnp.broadcast_to(par_ref[i, 1:2, :][None], (nc, 1, 128)))
    beta = _sigmoid(braw)
    sp_arg = araw + bias
    g = -ea * _softplus(sp_arg)
    G = _bmm(jnp.broadcast_to(tril, (nbat, CHUNK, CHUNK)), g, HI)
    gc = G[:, :, :CHUNK]
    grow = jnp.sum(eye * gc, axis=1, keepdims=True)
    decay = jnp.exp(jnp.where(causal, gc - grow, -1e30))
    kb = kn * beta
    A = _bmm_nt(kb, kn, HI)
    low = jnp.where(strict, A * decay, 0.0)
    tinv = _unit_lower_inverse(low)
    eG = jnp.exp(G)
    gl = G[:, CHUNK - 1:CHUNK, :]
    eGl = jnp.exp(gl - G)
    g_ch = jnp.exp(gl)
    rv = v * beta
    rkk = kb * eG
    u = _bmm(tinv, rv, HI)
    w = _bmm(tinv, rkk, HI)
    B = _bmm_nt(_b(qn), _b(kn))
    qk = jnp.where(causal, B * decay, 0.0)
    q_dec = qn * eG
    k_st = kn * eGl
    return dict(cq=cq, ck=ck, cv=cv, sq=sq, sk=sk, sv=sv, q=q, k=k, v=v, rq=rq, rk=rk, qh=qh, kn=kn, qn=qn,
                beta=beta, ea=ea, sp_arg=sp_arg, g=g, G=G, decay=decay, kb=kb, A=A, tinv=tinv, eG=eG, eGl=eGl,
                g_ch=g_ch, rv=rv, rkk=rkk, u=u, w=w, B=B, qk=qk, q_dec=q_dec, k_st=k_st)


def _gdn_masks():
    r, c = _iota2(CHUNK, CHUNK, 0), _iota2(CHUNK, CHUNK, 1)
    return r >= c, r > c, (r >= c).astype(F32), (r == c).astype(F32)


def _gdn_specs(blk, hb, idx):
    ng = DN_H // hb
    return [pl.BlockSpec((blk, hb * DN_DK), lambda h, j: (idx(j), h)),
            pl.BlockSpec((blk, hb * DN_DK), lambda h, j: (idx(j), ng + h)),
            pl.BlockSpec((blk, hb * DN_DV), lambda h, j: (idx(j), 2 * ng + h)),
            pl.BlockSpec((blk, SMALL), lambda h, j: (idx(j), 0)),
            pl.BlockSpec((hb, 8, 128), lambda h, j: (h, 0, 0))]


def _gdn_fwd(conv, psmall, par, name):
    T = conv.shape[0]
    blk = min(GDN_BLK, T)
    nc = blk // CHUNK
    hb = GDN_HEADS

    def body(cq_ref, ck_ref, cv_ref, sm_ref, par_ref, o_ref, ss_ref, s_ref):
        @pl.when(pl.program_id(1) == 0)
        def _():
            s_ref[...] = jnp.zeros_like(s_ref)

        f = _gdn_block(cq_ref, ck_ref, cv_ref, sm_ref, par_ref, pl.program_id(0) * hb, hb, nc, _gdn_masks())
        wb, qdb, kstb, qkb = _b(f["w"]), _b(f["q_dec"]), _b(f["k_st"]), _b(f["qk"])
        S = [s_ref[i] for i in range(hb)]
        for c in range(nc):
            for i in range(hb):
                n = i * nc + c
                ss_ref[i, c] = S[i]
                Sb = _b(S[i])
                v_new = _b(f["u"][n] - _dot(wb[n], Sb))
                o_ref[pl.ds(c * CHUNK, CHUNK), i * DN_DV:(i + 1) * DN_DV] = _dot(qdb[n], Sb) + _dot(qkb[n], v_new)
                S[i] = S[i] * f["g_ch"][n] + _dot_tn(kstb[n], v_new)
        for i in range(hb):
            s_ref[i] = S[i]

    return pl.pallas_call(
        body, name=name, grid=(DN_H // hb, T // blk),
        in_specs=_gdn_specs(blk, hb, lambda j: j),
        out_specs=[pl.BlockSpec((blk, hb * DN_DV), lambda h, j: (j, h)),
                   pl.BlockSpec((hb, nc, DN_DK, DN_DV), lambda h, j: (h, j, 0, 0))],
        out_shape=[jax.ShapeDtypeStruct((T, DN_H * DN_DV), F32),
                   jax.ShapeDtypeStruct((DN_H, T // CHUNK, DN_DK, DN_DV), F32)],
        scratch_shapes=[pltpu.VMEM((hb, DN_DK, DN_DV), F32)],
        compiler_params=_cp("parallel", "arbitrary"),
    )(conv, conv, conv, psmall, par)


def _gdn_bwd(conv, psmall, par, states, do, name):
    T = conv.shape[0]
    blk = min(GDN_BLK, T)
    nc = blk // CHUNK
    nb = T // blk
    hb = GDN_HEADS
    nbat = hb * nc
    rsum = lambda x: jnp.sum(x, axis=-1, keepdims=True)

    def body(cq_ref, ck_ref, cv_ref, sm_ref, par_ref, ss_ref, do_ref,
             dcq_ref, dck_ref, dcv_ref, dsm_ref, dpar_ref, ds_ref):
        @pl.when(pl.program_id(1) == 0)
        def _():
            ds_ref[...] = jnp.zeros_like(ds_ref)
            dpar_ref[...] = jnp.zeros_like(dpar_ref)

        masks = _gdn_masks()
        causal, strict, tril, eye = masks
        triu = (_iota2(CHUNK, CHUNK, 0) <= _iota2(CHUNK, CHUNK, 1)).astype(F32)
        lane = _iota2(CHUNK, 128, 1)
        last_row = _iota2(CHUNK, 128, 0) == CHUNK - 1
        h0 = pl.program_id(0) * hb
        f = _gdn_block(cq_ref, ck_ref, cv_ref, sm_ref, par_ref, h0, hb, nc, masks)
        S = ss_ref[...].reshape(nbat, DN_DK, DN_DV)
        Sb = _b(S)
        do_ = _b(_heads_first(do_ref[...], nc))
        wb, qdb, kstb, qkb = _b(f["w"]), _b(f["q_dec"]), _b(f["k_st"]), _b(f["qk"])
        vnb = _b(f["u"] - _bmm(wb, Sb))
        dvn0 = _bmm_tn(qkb, do_)
        qdo = _bmm_tn(qdb, do_)
        dS = [ds_ref[i] for i in range(hb)]
        after = [None] * nbat
        for c in range(nc - 1, -1, -1):
            for i in range(hb):
                n = i * nc + c
                after[n] = dS[i]
                dvn_c = _b(dvn0[n] + _dot(kstb[n], _b(dS[i])))
                dS[i] = dS[i] * f["g_ch"][n] + qdo[n] - _dot_tn(wb[n], dvn_c)
        for i in range(hb):
            ds_ref[i] = dS[i]
        dSa = jnp.stack(after)
        dSb = _b(dSa)
        dvn = dvn0 + _bmm(kstb, dSb)
        dvnb = _b(dvn)
        dq_dec = _bmm_nt(do_, Sb)
        dqk = jnp.where(causal, _bmm_nt(do_, vnb), 0.0)
        dk_st = _bmm_nt(vnb, dSb)
        dg_ch = jnp.sum(rsum(S * dSa), axis=1, keepdims=True)
        dw = -_bmm_nt(dvnb, Sb)
        drv = _bmm_tn(f["tinv"], dvn, HI)
        drk = _bmm_tn(f["tinv"], dw, HI)
        dlow = jnp.where(strict, -(_bmm_nt(drv, f["u"], HI) + _bmm_nt(drk, f["w"], HI)), 0.0)
        dv = drv * f["beta"]
        dbeta = rsum(drv * f["v"])
        dkb = drk * f["eG"]
        dG = rsum(drk * f["rkk"])
        dA = dlow * f["decay"]
        ddec = dlow * f["A"]
        dkb += _bmm(dA, f["kn"], HI)
        dkn = _bmm_tn(dA, f["kb"], HI)
        dB = dqk * f["decay"]
        ddec += dqk * f["B"]
        dqn = _bmm(_b(dB), _b(f["kn"]))
        dkn += _bmm_tn(_b(dB), _b(f["qn"]))
        dD = ddec * f["decay"]
        dG += rsum(dD) - rsum(eye * jnp.sum(dD, axis=1, keepdims=True))
        dqn += dq_dec * f["eG"]
        dG += rsum(dq_dec * f["q_dec"])
        dkn += dk_st * f["eGl"]
        tks = rsum(dk_st * f["k_st"])
        dG -= tks
        dG_last = jnp.sum(tks, axis=1, keepdims=True) + dg_ch * f["g_ch"][:, :, :1]
        dkn += dkb * f["beta"]
        dbeta += rsum(dkb * f["kn"])
        dGf = jnp.broadcast_to(dG, (nbat, CHUNK, 128)) + jnp.where(last_row, dG_last, 0.0)
        dg = _bmm(jnp.broadcast_to(triu, (nbat, CHUNK, CHUNK)), dGf, HI)
        dbraw = dbeta * f["beta"][:, :, :1] * (1.0 - f["beta"][:, :, :1])
        daraw = dg * (-f["ea"]) * _sigmoid(f["sp_arg"])
        both = lambda t: jnp.sum(jnp.sum(t, axis=1, keepdims=True), axis=0)
        dgg = dg * f["g"]
        dsm = jnp.zeros((nc, CHUNK, SMALL), F32)
        for i in range(hb):
            mine = slice(i * nc, (i + 1) * nc)
            dsm += (jnp.where(lane == GLA_RANK + h0 + i, dbraw[mine], 0.0)
                    + jnp.where(lane == GLA_RANK + DN_H + h0 + i, daraw[mine], 0.0))
            dpar = jnp.where(lane[:1] == 0, both(dgg[mine]), jnp.where(lane[:1] == 1, both(daraw[mine]), 0.0))
            dpar_ref[i] += jnp.broadcast_to(dpar, (8, 128))
        dsm_ref[0] = dsm.reshape(blk, SMALL)
        dqh = dqn * (DN_DK ** -0.5)
        dq = f["rq"] * (dqh - f["qh"] * rsum(dqh * f["qh"]))
        dk = f["rk"] * (dkn - f["kn"] * rsum(dkn * f["kn"]))
        dsilu = lambda x, s: s * (1.0 + x * (1.0 - s))
        dcq_ref[...] = _b(_heads_last(dq * dsilu(f["cq"], f["sq"]), nc))
        dck_ref[...] = _b(_heads_last(dk * dsilu(f["ck"], f["sk"]), nc))
        dcv_ref[...] = _b(_heads_last(dv * dsilu(f["cv"], f["sv"]), nc))

    r = lambda j: nb - 1 - j
    out_blk = pl.BlockSpec((blk, hb * DN_DK), lambda h, j: (r(j), h))
    return pl.pallas_call(
        body, name=name, grid=(DN_H // hb, nb),
        in_specs=_gdn_specs(blk, hb, r) + [pl.BlockSpec((hb, nc, DN_DK, DN_DV), lambda h, j: (h, r(j), 0, 0)),
                                          pl.BlockSpec((blk, hb * DN_DV), lambda h, j: (r(j), h))],
        out_specs=[out_blk, out_blk, out_blk, pl.BlockSpec((1, blk, SMALL), lambda h, j: (h, r(j), 0)),
                   pl.BlockSpec((hb, 8, 128), lambda h, j: (h, 0, 0))],
        out_shape=[jax.ShapeDtypeStruct((T, DN_H * DN_DK), BF16)] * 3 + [
            jax.ShapeDtypeStruct((DN_H // hb, T, SMALL), F32), jax.ShapeDtypeStruct((DN_H, 8, 128), F32)],
        scratch_shapes=[pltpu.VMEM((hb, DN_DK, DN_DV), F32)],
        compiler_params=_cp("parallel", "arbitrary"),
    )(conv, conv, conv, psmall, par, states, do)


def _head_norm(o, w, dv):
    outs, rs = [], []
    for i in range(o.shape[1] // dv):
        oh = o[:, i * dv:(i + 1) * dv]
        r = lax.rsqrt(jnp.mean(oh * oh, axis=-1, keepdims=True) + EPS)
        outs.append(oh * r)
        rs.append(r)
    return outs, rs


def _merge_specs(tm):
    col = lambda c: pl.BlockSpec((tm, D), lambda i: (i, c))
    return [col(0), col(0), col(2), col(6), col(7), col(8),
            pl.BlockSpec((1, GLA_DV), lambda i: (0, 0)), pl.BlockSpec((1, DN_DV), lambda i: (0, 0)),
            pl.BlockSpec((D, D), lambda i: (0, 0))]


def _merge_fwd(h, oa, ob, pbig, gla_hn, dn_hn, wout, name):
    T = h.shape[0]
    tm = min(ROW_BLK, T)

    def body(h_ref, oa_ref, ob_ref, gr_ref, dg_ref, ma_ref, mb_ref, wa_ref, wb_ref, wo_ref, ho_ref, y_ref):
        na, _ = _head_norm(oa_ref[...], wa_ref[...], GLA_DV)
        nbs, _ = _head_norm(ob_ref[...], wb_ref[...], DN_DV)
        hna = jnp.concatenate([t * wa_ref[...] for t in na], axis=1)
        hnb = jnp.concatenate([t * wb_ref[...] for t in nbs], axis=1)
        gr = gr_ref[...].astype(F32)
        dg = dg_ref[...].astype(F32)
        y = (_sigmoid(ma_ref[...].astype(F32)) * hna * (gr * _sigmoid(gr))
             + _sigmoid(mb_ref[...].astype(F32)) * hnb * (dg * _sigmoid(dg)))
        yb = _b(y)
        y_ref[...] = yb
        ho_ref[...] = h_ref[...] + _dot(yb, wo_ref[...])

    row = pl.BlockSpec((tm, D), lambda i: (i, 0))
    return pl.pallas_call(
        body, name=name, grid=(T // tm,),
        in_specs=[row] + _merge_specs(tm),
        out_specs=[row, row],
        out_shape=[jax.ShapeDtypeStruct((T, D), F32), jax.ShapeDtypeStruct((T, D), BF16)],
        compiler_params=_cp("arbitrary"),
    )(h, oa, ob, pbig, pbig, pbig, pbig, gla_hn, dn_hn, wout)


def _merge_bwd(dh, oa, ob, pbig, gla_hn, dn_hn, wout, name):
    T = dh.shape[0]
    tm = min(ROW_BLK, T)

    def branch(dy, o_ref, w_ref, gate_ref, m_ref, dv):
        w = w_ref[...]
        ohat, rs = _head_norm(o_ref[...], w, dv)
        gate = gate_ref[...].astype(F32)
        m = m_ref[...].astype(F32)
        sgate, sm = _sigmoid(gate), _sigmoid(m)
        silu = gate * sgate
        ohat_all = jnp.concatenate(ohat, axis=1)
        hn = jnp.concatenate([t * w for t in ohat], axis=1)
        d_on = dy * sm
        d_m = dy * hn * silu * sm * (1.0 - sm)
        d_hn = d_on * silu
        d_gate = d_on * hn * (sgate * (1.0 + gate * (1.0 - sgate)))
        dw = jnp.zeros((1, dv), F32)
        d_o = []
        for i, (oh, r) in enumerate(zip(ohat, rs)):
            dhn = d_hn[:, i * dv:(i + 1) * dv]
            dw += jnp.sum(dhn * oh, axis=0, keepdims=True)
            dohat = dhn * w
            d_o.append(r * (dohat - oh * jnp.mean(dohat * oh, axis=-1, keepdims=True)))
        return jnp.concatenate(d_o, axis=1), d_gate, d_m, dw

    def body(dh_ref, oa_ref, ob_ref, gr_ref, dg_ref, ma_ref, mb_ref, wa_ref, wb_ref, wo_ref,
             doa_ref, dob_ref, dgr_ref, ddg_ref, dma_ref, dmb_ref, dwa_ref, dwb_ref, dhb_ref):
        @pl.when(pl.program_id(0) == 0)
        def _():
            dwa_ref[...] = jnp.zeros_like(dwa_ref)
            dwb_ref[...] = jnp.zeros_like(dwb_ref)

        dhb = _b(dh_ref[...])
        dhb_ref[...] = dhb
        dy = _dot_nt(dhb, wo_ref[...])
        d_oa, d_gr, d_ma, dwa = branch(dy, oa_ref, wa_ref, gr_ref, ma_ref, GLA_DV)
        d_ob, d_dg, d_mb, dwb = branch(dy, ob_ref, wb_ref, dg_ref, mb_ref, DN_DV)
        doa_ref[...] = d_oa
        dob_ref[...] = d_ob
        dgr_ref[...] = _b(d_gr)
        ddg_ref[...] = _b(d_dg)
        dma_ref[...] = _b(d_ma)
        dmb_ref[...] = _b(d_mb)
        dwa_ref[...] += dwa
        dwb_ref[...] += dwb

    row = pl.BlockSpec((tm, D), lambda i: (i, 0))
    f32 = jax.ShapeDtypeStruct((T, D), F32)
    b16 = jax.ShapeDtypeStruct((T, D), BF16)
    return pl.pallas_call(
        body, name=name, grid=(T // tm,),
        in_specs=[row] + _merge_specs(tm),
        out_specs=[row] * 6 + [pl.BlockSpec((1, GLA_DV), lambda i: (0, 0)), pl.BlockSpec((1, DN_DV), lambda i: (0, 0)), row],
        out_shape=[f32, f32, b16, b16, b16, b16, jax.ShapeDtypeStruct((1, GLA_DV), F32),
                   jax.ShapeDtypeStruct((1, DN_DV), F32), b16],
        compiler_params=_cp("arbitrary"),
    )(dh, oa, ob, pbig, pbig, pbig, pbig, gla_hn, dn_hn, wout)


def _loss_head(h, nw, target, name):
    T = h.shape[0]
    tm = min(512, T)

    def body(h_ref, nw_ref, t_ref, dx_ref, loss_ref, dnw_ref):
        @pl.when(pl.program_id(0) == 0)
        def _():
            loss_ref[...] = jnp.zeros_like(loss_ref)
            dnw_ref[...] = jnp.zeros_like(dnw_ref)

        x = h_ref[...]
        w = nw_ref[...]
        r = lax.rsqrt(jnp.mean(x * x, axis=-1, keepdims=True) + EPS)
        xhat = x * r
        err = xhat * w - t_ref[...]
        part = jnp.sum(jnp.sum(err * err, axis=-1, keepdims=True), axis=0, keepdims=True)
        loss_ref[...] += (0.5 / D) * part
        dout = err * (1.0 / D)
        dnw_ref[...] += jnp.sum(dout * xhat, axis=0, keepdims=True)
        dxhat = dout * w
        dx_ref[...] = r * (dxhat - xhat * jnp.mean(dxhat * xhat, axis=-1, keepdims=True))

    row = pl.BlockSpec((tm, D), lambda i: (i, 0))
    one = pl.BlockSpec((1, D), lambda i: (0, 0))
    return pl.pallas_call(
        body, name=name, grid=(T // tm,),
        in_specs=[row, one, row],
        out_specs=[row, pl.BlockSpec((8, 128), lambda i: (0, 0)), one],
        out_shape=[jax.ShapeDtypeStruct((T, D), F32), jax.ShapeDtypeStruct((8, 128), F32),
                   jax.ShapeDtypeStruct((1, D), F32)],
        compiler_params=_cp("arbitrary"),
    )(h, nw, target)


def _adamw(w, g, m, v, rows, name):
    R, C = w.shape
    rows = min(rows, R)
    c1 = 1.0 - ADAM_B1 ** ADAM_STEP
    c2 = 1.0 - ADAM_B2 ** ADAM_STEP

    def body(w_ref, g_ref, m_ref, v_ref, d_ref, mo_ref, vo_ref):
        g_ = g_ref[...]
        m_ = ADAM_B1 * m_ref[...] + (1.0 - ADAM_B1) * g_
        v_ = ADAM_B2 * v_ref[...] + (1.0 - ADAM_B2) * (g_ * g_)
        mo_ref[...] = m_
        vo_ref[...] = v_
        d_ref[...] = -ADAM_LR * ((m_ / c1) / (jnp.sqrt(v_ / c2) + ADAM_EPS) + ADAM_WD * w_ref[...])

    blk = pl.BlockSpec((rows, C), lambda i: (i, 0))
    shp = jax.ShapeDtypeStruct((R, C), F32)
    return pl.pallas_call(
        body, name=name, grid=(R // rows,),
        in_specs=[blk] * 4, out_specs=[blk] * 3, out_shape=[shp] * 3,
        compiler_params=_cp("parallel"),
    )(w, g, m, v)


def _me():
    return lax.axis_index("x"), lax.axis_index("y"), lax.axis_index("c")


def _other_chips(x, y):
    return [(1 - x, y), (x, 1 - y), (1 - x, 1 - y)]


def _half_rows(ref, hf):
    half = ref.shape[-2] // 2
    rows = pl.ds(pl.multiple_of(hf * half, 16), half)
    return ref.at[rows, :] if len(ref.shape) == 2 else ref.at[:, rows, :]


def _gather_weights(big, small, name):
    nbig, nsm = len(big), len(small)
    n = nbig + nsm

    def body(*refs):
        ins, outs = refs[:n], refs[n:2 * n]
        send_sems, recv_sems, local_sems = refs[2 * n:]
        x, y, c = _me()
        sibling = (x, y, 1 - c)
        chips = _other_chips(x, y)
        slot = lambda chip: 2 * chip[0] + chip[1]

        def copy(k, src, dst, to):
            return pltpu.make_async_remote_copy(src_ref=src, dst_ref=dst, send_sem=send_sems.at[k],
                                                recv_sem=recv_sems.at[k], device_id=to, device_id_type=MESH)

        local, sent = [], []
        for i in range(nbig):
            local.append(pltpu.make_async_copy(ins[i], outs[i].at[slot((x, y))], local_sems.at[i]))
            local[-1].start()
            for j, chip in enumerate(chips):
                sent.append(copy(6 * i + j, _half_rows(ins[i], c), _half_rows(outs[i].at[slot((x, y))], c), (*chip, c)))
                sent[-1].start()
        for t in range(nsm):
            w_ref, o_ref = ins[nbig + t], outs[nbig + t]
            o_ref[slot((x, y))] = w_ref[...]
            for j, chip in enumerate(chips):
                sent.append(copy(6 * nbig + 3 * t + j, w_ref, o_ref.at[slot((x, y))], (*chip, c)))
                sent[-1].start()
        for i in range(nbig):
            for j, chip in enumerate(chips):
                landed = _half_rows(outs[i].at[slot(chip)], c)
                copy(6 * i + j, landed, landed, (x, y, c)).wait_recv()
                sent.append(copy(6 * i + 3 + j, landed, landed, sibling))
                sent[-1].start()
        for t in range(nsm):
            for j, chip in enumerate(chips):
                landed = outs[nbig + t].at[slot(chip)]
                copy(6 * nbig + 3 * t + j, landed, landed, (x, y, c)).wait_recv()
        for i in range(nbig):
            for j, chip in enumerate(chips):
                passed = _half_rows(outs[i].at[slot(chip)], 1 - c)
                copy(6 * i + 3 + j, passed, passed, (x, y, c)).wait_recv()
        for cp in sent:
            cp.wait_send()
        for cp in local:
            cp.wait()

    vm = pl.BlockSpec(memory_space=pltpu.VMEM)
    nsem = 6 * nbig + 3 * nsm
    return pl.pallas_call(
        body, name=name, in_specs=[ANY] * nbig + [vm] * nsm, out_specs=[ANY] * nbig + [vm] * nsm,
        out_shape=[jax.ShapeDtypeStruct((N_SHARD,) + w.shape, w.dtype) for w in list(big) + list(small)],
        scratch_shapes=[pltpu.SemaphoreType.DMA((nsem,)), pltpu.SemaphoreType.DMA((nsem,)),
                        pltpu.SemaphoreType.DMA((nbig,))],
        compiler_params=pltpu.CompilerParams(has_side_effects=True),
    )(*big, *small)


def _rs_sibling(gs, name):
    n = len(gs)

    def body(*refs):
        send_sems, recv_sems = refs[2 * n:]
        x, y, c = _me()
        cps = [pltpu.make_async_remote_copy(src_ref=_half_rows(refs[i], 1 - c), dst_ref=refs[n + i],
                                            send_sem=send_sems.at[i], recv_sem=recv_sems.at[i],
                                            device_id=(x, y, 1 - c), device_id_type=MESH) for i in range(n)]
        for cp in cps:
            cp.start()
        for cp in cps:
            cp.wait()

    return pl.pallas_call(
        body, name=name, in_specs=[ANY] * n, out_specs=[ANY] * n,
        out_shape=[jax.ShapeDtypeStruct((g.shape[0], g.shape[1] // 2, g.shape[2]), g.dtype) for g in gs],
        scratch_shapes=[pltpu.SemaphoreType.DMA((n,)), pltpu.SemaphoreType.DMA((n,))],
        compiler_params=pltpu.CompilerParams(has_side_effects=True),
    )(*gs)


def _add_pair(g, other, where, name):
    ns, a, b = g.shape
    half = a // 2

    def body(w_ref, g_ref, o_ref, pb_ref, own_ref):
        t = g_ref[0].astype(F32) + o_ref[0].astype(F32)
        pb_ref[0] = _b(t)

        @pl.when(pl.program_id(0) == w_ref[1])
        def _():
            own_ref[...] = t

    return pl.pallas_call(
        body, name=name,
        grid_spec=pltpu.PrefetchScalarGridSpec(
            num_scalar_prefetch=1, grid=(ns,),
            in_specs=[pl.BlockSpec((1, half, b), lambda s, w: (s, w[0], 0)), pl.BlockSpec((1, half, b), lambda s, w: (s, 0, 0))],
            out_specs=[pl.BlockSpec((1, half, b), lambda s, w: (s, 0, 0)), pl.BlockSpec((half, b), lambda s, w: (0, 0))]),
        out_shape=[jax.ShapeDtypeStruct((ns, half, b), BF16), jax.ShapeDtypeStruct((half, b), F32)],
        compiler_params=_cp("arbitrary"),
    )(where, g, other)


def _rs_chips(pbs, name):
    n = len(pbs)

    def body(*refs):
        send_sems, recv_sems = refs[2 * n:]
        x, y, c = _me()
        cps = [pltpu.make_async_remote_copy(src_ref=refs[i].at[2 * chip[0] + chip[1]], dst_ref=refs[n + i].at[j],
                                            send_sem=send_sems.at[3 * i + j], recv_sem=recv_sems.at[3 * i + j],
                                            device_id=(*chip, c), device_id_type=MESH)
               for i in range(n) for j, chip in enumerate(_other_chips(x, y))]
        for cp in cps:
            cp.start()
        for cp in cps:
            cp.wait()

    return pl.pallas_call(
        body, name=name, in_specs=[ANY] * n, out_specs=[ANY] * n,
        out_shape=[jax.ShapeDtypeStruct((3,) + p.shape[1:], p.dtype) for p in pbs],
        scratch_shapes=[pltpu.SemaphoreType.DMA((3 * n,)), pltpu.SemaphoreType.DMA((3 * n,))],
        compiler_params=pltpu.CompilerParams(has_side_effects=True),
    )(*pbs)


def _add_four(own, got, name):
    rows, cols = own.shape
    rb = rows // 2

    def body(a_ref, b_ref, o_ref):
        o_ref[...] = ((a_ref[...] + b_ref[0].astype(F32)) + b_ref[1].astype(F32)) + b_ref[2].astype(F32)

    return pl.pallas_call(
        body, name=name, grid=(rows // rb,),
        in_specs=[pl.BlockSpec((rb, cols), lambda i: (i, 0)), pl.BlockSpec((3, rb, cols), lambda i: (0, i, 0))],
        out_specs=pl.BlockSpec((rb, cols), lambda i: (i, 0)),
        out_shape=jax.ShapeDtypeStruct((rows, cols), F32),
        compiler_params=_cp("parallel"),
    )(own, got)


def _rs_join(halves, name):
    n = len(halves)

    def body(*refs):
        send_sems, recv_sems, local_sems = refs[2 * n:]
        x, y, c = _me()
        local, sent = [], []
        for i in range(n):
            local.append(pltpu.make_async_copy(refs[i], _half_rows(refs[n + i], c), local_sems.at[i]))
            local[-1].start()
            sent.append(pltpu.make_async_remote_copy(src_ref=refs[i], dst_ref=_half_rows(refs[n + i], c),
                                                     send_sem=send_sems.at[i], recv_sem=recv_sems.at[i],
                                                     device_id=(x, y, 1 - c), device_id_type=MESH))
            sent[-1].start()
        for i in range(n):
            sent[i].wait_send()
            pltpu.make_async_remote_copy(src_ref=refs[i], dst_ref=_half_rows(refs[n + i], 1 - c),
                                         send_sem=send_sems.at[i], recv_sem=recv_sems.at[i],
                                         device_id=(x, y, 1 - c), device_id_type=MESH).wait_recv()
            local[i].wait()

    return pl.pallas_call(
        body, name=name, in_specs=[ANY] * n, out_specs=[ANY] * n,
        out_shape=[jax.ShapeDtypeStruct((2 * h.shape[0], h.shape[1]), h.dtype) for h in halves],
        scratch_shapes=[pltpu.SemaphoreType.DMA((n,)), pltpu.SemaphoreType.DMA((n,)), pltpu.SemaphoreType.DMA((n,))],
        compiler_params=pltpu.CompilerParams(has_side_effects=True),
    )(*halves)


def _allsum_small(vec, name):
    def body(v_ref, o_ref, buf_ref, send_sems, recv_sems):
        x, y, c = _me()
        me = 4 * x + 2 * y + c
        buf_ref[me] = v_ref[...]
        cps = []
        for k in range(1, 8):
            peer = (x ^ (k >> 2), y ^ ((k >> 1) & 1), c ^ (k & 1))
            cps.append(pltpu.make_async_remote_copy(src_ref=v_ref, dst_ref=buf_ref.at[me],
                                                    send_sem=send_sems.at[k - 1], recv_sem=recv_sems.at[k - 1],
                                                    device_id=peer, device_id_type=MESH))
        for cp in cps:
            cp.start()
        for k in range(1, 8):
            peer_idx = me ^ k
            pltpu.make_async_remote_copy(src_ref=v_ref, dst_ref=buf_ref.at[peer_idx],
                                         send_sem=send_sems.at[k - 1], recv_sem=recv_sems.at[k - 1],
                                         device_id=(x, y, c), device_id_type=MESH).wait_recv()
        for cp in cps:
            cp.wait_send()
        acc = buf_ref[0]
        for d in range(1, 8):
            acc = acc + buf_ref[d]
        o_ref[...] = acc

    return pl.pallas_call(
        body, name=name,
        in_specs=[pl.BlockSpec(memory_space=pltpu.VMEM)], out_specs=pl.BlockSpec(memory_space=pltpu.VMEM),
        out_shape=jax.ShapeDtypeStruct(vec.shape, F32),
        scratch_shapes=[pltpu.VMEM((8,) + vec.shape, F32), pltpu.SemaphoreType.DMA((7,)), pltpu.SemaphoreType.DMA((7,))],
        compiler_params=pltpu.CompilerParams(has_side_effects=True),
    )(vec)


BIG = ("ffn1_w_gate", "ffn1_w_up", "ffn1_w_down", "w_in", "w_out", "ffn2_w_gate", "ffn2_w_up", "ffn2_w_down")
TINY = ("w_gla_gate", "conv_w")
SHARDED = BIG + TINY


def _join_cols(w4):
    return jnp.transpose(w4, (1, 0, 2)).reshape(w4.shape[1], N_SHARD * w4.shape[2])


def _cut_cols(w):
    return jnp.transpose(w.reshape(w.shape[0], N_SHARD, w.shape[1] // N_SHARD), (1, 0, 2))


def _split_w_in(w):
    o = IN_OFF
    big = jnp.concatenate([w[:, :o[4]], w[:, o[5]:o[9]], w[:, o[11]:]], axis=1)
    small = jnp.concatenate([w[:, o[4]:o[5]], w[:, o[9]:o[11]], jnp.zeros((w.shape[0], SMALL - 32), w.dtype)], axis=1)
    return big, small


def _merge_w_in(big, small):
    return jnp.concatenate([big[:, :3072], small[:, :16], big[:, 3072:7168], small[:, 16:32], big[:, 7168:]], axis=1)


def _local_step(x, target, W, P):
    wbig, wsmall = W["w_in_big"], W["w_in_small"]
    wgate_pad = jnp.zeros((SMALL, GLA_H * GLA_DK), F32).at[:GLA_RANK].set(P["w_gla_gate"])
    cw8 = jnp.zeros((8, CONV_C), F32).at[:CONV_K].set(P["conv_w"])
    par = jnp.zeros((DN_H, 8, 128), F32)
    par = par.at[:, 0, :].set(jnp.broadcast_to(P["dn_a_log"].reshape(DN_H, 1), (DN_H, 128)))
    par = par.at[:, 1, :].set(jnp.broadcast_to(P["dn_dt_bias"].reshape(DN_H, 1), (DN_H, 128)))

    h1, n1, g1, u1 = _ffn_fwd(x, P["ffn1_norm"], W["ffn1_w_gate"], W["ffn1_w_up"], W["ffn1_w_down"], "ffn1_fwd")
    pbig, psmall, n2 = _norm_proj(h1, P["mix_norm"], wbig, wsmall, "mix_proj")
    oa, sa = _gla_fwd(pbig, psmall, wgate_pad, P["b_gla_gate"], "gla_fwd")
    conv = _conv_fwd(pbig, cw8, "conv_fwd")
    ob, sb = _gdn_fwd(conv, psmall, par, "gdn_fwd")
    h2, yb = _merge_fwd(h1, oa, ob, pbig, P["gla_head_norm"], P["dn_head_norm"], W["w_out"], "merge_fwd")
    h3, n3, g3, u3 = _ffn_fwd(h2, P["ffn2_norm"], W["ffn2_w_gate"], W["ffn2_w_up"], W["ffn2_w_down"], "ffn2_fwd")
    dh3, loss, d_final = _loss_head(h3, P["final_norm"], target, "loss_head")

    gw, gs = {}, {"final_norm": d_final}

    def ffn_grads(tag, dh, h, n, g, u):
        dx, dg, du, act, dfb, dnw = _ffn_bwd(dh, h, P[tag + "_norm"], g, u, W[tag + "_w_gate"], W[tag + "_w_up"],
                                             W[tag + "_w_down"], tag + "_bwd")
        gw[tag + "_w_gate"] = _mm_tn(n, dg, D, FF_CUT, tag + "_dwg")
        gw[tag + "_w_up"] = _mm_tn(n, du, D, FF_CUT, tag + "_dwu")
        gw[tag + "_w_down"] = _mm_tn(act, dfb, FF_CUT, D, tag + "_dwd")
        gs[tag + "_norm"] = dnw
        return dx

    dh2 = ffn_grads("ffn2", dh3, h2, n3, g3, u3)
    d_oa, d_ob, d_gr, d_dgate, d_ma, d_mb, gs["gla_head_norm"], gs["dn_head_norm"], dh2b = _merge_bwd(
        dh2, oa, ob, pbig, P["gla_head_norm"], P["dn_head_norm"], W["w_out"], "merge_bwd")
    gw["w_out"] = _mm_tn(yb, dh2b, D, D, "dw_out").reshape(N_SHARD, D // N_SHARD, D)
    d_gq, d_gk, d_gv, dpre = _gla_bwd(pbig, psmall, wgate_pad, P["b_gla_gate"], sa, d_oa, "gla_bwd")
    dcq, dck, dcv, dsm, dpar = _gdn_bwd(conv, psmall, par, sb, d_ob, "gdn_bwd")
    dsmall, dwgate, gs["b_gla_gate"] = _gla_gate_bwd(dpre, psmall, wgate_pad, dsm, "gla_gate_bwd")
    gs["w_gla_gate"] = dwgate[:GLA_RANK]
    d_x3, dcw = _conv_bwd(dcq, dck, dcv, pbig, cw8, "conv_bwd")
    gs["conv_w"] = dcw[:CONV_K]
    gs["dn_a_log"] = dpar[:, 0, 0].reshape(1, DN_H)
    gs["dn_dt_bias"] = dpar[:, 0, 1].reshape(1, DN_H)
    pieces = (d_gq, d_gk, d_gv, d_gr, d_x3, d_dgate, d_ma, d_mb)
    dh1, gs["mix_norm"] = _proj_bwd(dh2, h1, P["mix_norm"], pieces, dsmall, wbig, wsmall, "proj_bwd")
    dbig = jnp.concatenate([_mm_tn(n2, p, D, 1024, "dw_in_%d" % i) for i, p in enumerate(pieces)], axis=1)
    dsml = _mm_tn(n2, dsmall, D, SMALL, "dw_in_small")
    gw["w_in"] = _cut_cols(_merge_w_in(dbig, dsml))
    grad_x = ffn_grads("ffn1", dh1, x, n1, g1, u1)
    return loss, grad_x, gw, gs


SMALL_NAMES = ("ffn1_norm", "mix_norm", "ffn2_norm", "final_norm", "b_gla_gate", "gla_head_norm", "dn_head_norm",
               "dn_a_log", "dn_dt_bias")
ROW4 = (("b_gla_gate", 512), ("gla_head_norm", 256), ("dn_head_norm", 128), ("dn_a_log", 8), ("dn_dt_bias", 8))


def _pack_small(d, loss=None):
    row4 = [d[n].reshape(-1) for n, _ in ROW4]
    row4.append(jnp.zeros((1,), F32) if loss is None else loss.reshape(1))
    row4 = jnp.concatenate(row4)
    row4 = jnp.pad(row4, (0, D - row4.shape[0]))
    rows = [d[n].reshape(-1) for n in SMALL_NAMES[:4]] + [row4]
    return jnp.concatenate([jnp.stack(rows), jnp.zeros((3, D), F32)], axis=0)


def _unpack_small(a, like):
    out = {n: a[i].reshape(like[n].shape) for i, n in enumerate(SMALL_NAMES[:4])}
    off = 0
    for n, w in ROW4:
        out[n] = a[4, off:off + w].reshape(like[n].shape)
        off += w
    return out, a[4, off]


WEIGHT_ORDER = ("ffn1_norm", "ffn1_w_gate", "ffn1_w_up", "ffn1_w_down", "mix_norm", "w_in", "w_gla_gate", "b_gla_gate",
                "conv_w", "dn_a_log", "dn_dt_bias", "gla_head_norm", "dn_head_norm", "w_out", "ffn2_norm",
                "ffn2_w_gate", "ffn2_w_up", "ffn2_w_down", "final_norm")
ADAM_ROWS = {"ffn1_w_gate": 256, "ffn1_w_up": 256, "ffn1_w_down": 176, "w_in": 128, "w_gla_gate": 16, "conv_w": 4,
             "w_out": 128, "ffn2_w_gate": 256, "ffn2_w_up": 256, "ffn2_w_down": 176}


def kernel(x, ffn1_norm, ffn1_w_gate, ffn1_w_up, ffn1_w_down, mix_norm, w_in, w_gla_gate, b_gla_gate, conv_w, dn_a_log, dn_dt_bias, gla_head_norm, dn_head_norm, w_out, ffn2_norm, ffn2_w_gate, ffn2_w_up, ffn2_w_down, final_norm, loss_target, m_ffn1_norm, m_ffn1_w_gate, m_ffn1_w_up, m_ffn1_w_down, m_mix_norm, m_w_in, m_w_gla_gate, m_b_gla_gate, m_conv_w, m_dn_a_log, m_dn_dt_bias, m_gla_head_norm, m_dn_head_norm, m_w_out, m_ffn2_norm, m_ffn2_w_gate, m_ffn2_w_up, m_ffn2_w_down, m_final_norm, v_ffn1_norm, v_ffn1_w_gate, v_ffn1_w_up, v_ffn1_w_down, v_mix_norm, v_w_in, v_w_gla_gate, v_b_gla_gate, v_conv_w, v_dn_a_log, v_dn_dt_bias, v_gla_head_norm, v_dn_head_norm, v_w_out, v_ffn2_norm, v_ffn2_w_gate, v_ffn2_w_up, v_ffn2_w_down, v_final_norm):
    given = dict(locals())
    wts = {n: given[n] for n in WEIGHT_ORDER}
    moms = {n: given["m_" + n] for n in WEIGHT_ORDER}
    vars_ = {n: given["v_" + n] for n in WEIGHT_ORDER}
    two_d = lambda a: a.reshape(a.shape[-2], a.shape[-1]) if a.ndim == 3 else a.reshape(1, -1)
    shard = {n: two_d(wts[n]) for n in SHARDED}

    gathered = _gather_weights([shard[n].astype(BF16) for n in BIG], [shard[n] for n in TINY], "gather_weights")
    G = dict(zip(SHARDED, gathered))
    W = {n: G[n] for n in BIG if n.startswith("ffn")}
    W["w_out"] = G["w_out"].reshape(D, D)
    W["w_in_big"], W["w_in_small"] = _split_w_in(_join_cols(G["w_in"]))
    P = {n: two_d(wts[n]) for n in SMALL_NAMES}
    for n in TINY:
        P[n] = _join_cols(G[n])

    loss, grad_x, gw, gs = _local_step(x[0], loss_target[0], W, P)

    my_slot = 2 * lax.axis_index("x") + lax.axis_index("y")
    where = jnp.stack([lax.axis_index("c"), my_slot]).astype(jnp.int32)
    mine = [gw[n] for n in BIG]
    from_sibling = _rs_sibling(mine, "rs_sibling")
    pairs = [_add_pair(g, o, where, "rs_pair_" + n) for n, g, o in zip(BIG, mine, from_sibling)]
    from_chips = _rs_chips([p[0] for p in pairs], "rs_chips")
    halves = [_add_four(p[1], got, "rs_four_" + n) for n, p, got in zip(BIG, pairs, from_chips)]
    grads = dict(zip(BIG, _rs_join(halves, "rs_join")))

    tiny_rows = jnp.concatenate([gs["w_gla_gate"].reshape(8, D), gs["conv_w"].reshape(12, D), jnp.zeros((4, D), F32)])
    all_sum = _allsum_small(jnp.concatenate([_pack_small(gs, loss[0, 0]), tiny_rows]), "allsum_small")
    small_sum = all_sum[:8]
    small_g, loss_total = _unpack_small(small_sum, P)
    for n, rows in (("w_gla_gate", all_sum[8:16]), ("conv_w", all_sum[16:28])):
        cols = shard[n].shape[1]
        grads[n] = lax.dynamic_slice_in_dim(rows.reshape(shard[n].shape[0], N_SHARD * cols), my_slot * cols, cols, axis=1)

    delta, new_m, new_v = {}, {}, {}
    for n in SHARDED:
        d, m_, v_ = _adamw(shard[n], grads[n], two_d(moms[n]), two_d(vars_[n]), ADAM_ROWS[n], "adamw_" + n)
        delta[n], new_m[n], new_v[n] = (t.reshape(wts[n].shape) for t in (d, m_, v_))
    pk = lambda src: _pack_small({n: two_d(src[n]) for n in SMALL_NAMES})
    sd, sm_, sv_ = _adamw(pk(wts), small_sum, pk(moms), pk(vars_), 8, "adamw_small")
    for res, dst in ((sd, delta), (sm_, new_m), (sv_, new_v)):
        u, _ = _unpack_small(res, wts)
        dst.update(u)
    grad_w = {n: grads[n].reshape(wts[n].shape) for n in SHARDED}
    grad_w.update({n: small_g[n].reshape(wts[n].shape) for n in SMALL_NAMES})
    return (loss_total, grad_x[None], *[grad_w[n] for n in WEIGHT_ORDER], *[delta[n] for n in WEIGHT_ORDER],
            *[new_m[n] for n in WEIGHT_ORDER], *[new_v[n] for n in WEIGHT_ORDER])
```

```python
import functools
import math

import numpy as np
import jax
import jax.numpy as jnp
from jax import lax
from jax.experimental import pallas as pl
from jax.experimental.pallas import tpu as pltpu

F32 = jnp.float32
BF16 = jnp.bfloat16
HI = lax.Precision.HIGH
MESH = pl.DeviceIdType.MESH
ANY = pl.BlockSpec(memory_space=pl.ANY)

EPS = 1e-6
D = 1024
DFF = 2816
FFN_RES = 0.5
GLA_H, GLA_DK, GLA_DV, GLA_RANK, GLA_TAU = 4, 128, 256, 16, 16.0
DN_H, DN_DK, DN_DV = 8, 128, 128
CONV_K = 4
CHUNK = 64
N_SHARD = 4
FF_CUT = DFF // N_SHARD
ADAM_LR, ADAM_B1, ADAM_B2, ADAM_EPS, ADAM_WD, ADAM_STEP = 0.001, 0.9, 0.999, 1e-08, 0.01, 10

IN_SIZES = (512, 512, 1024, 1024, 16, 1024, 1024, 1024, 1024, 8, 8, 1024, 1024)
IN_OFF = tuple(int(v) for v in np.cumsum((0,) + IN_SIZES))
D_IN = IN_OFF[-1]
BIG_COLS = 9216
SMALL = 128
PIECES = (512, 512, 1024, 1024, 3072, 1024, 1024, 1024)

VMEM_LIMIT = 56 * 1024 * 1024
ROW_BLK = 256
ATT_BLK = 256
GDN_BLK = 128
GDN_HEADS = 8


def _cp(*sem):
    return pltpu.CompilerParams(dimension_semantics=sem, vmem_limit_bytes=VMEM_LIMIT)


def _sigmoid(x):
    return 1.0 / (1.0 + jnp.exp(-x))


def _softplus(x):
    return jnp.maximum(x, 0.0) + jnp.log(1.0 + jnp.exp(-jnp.abs(x)))


def _log_sigmoid(x):
    return jnp.minimum(x, 0.0) - jnp.log(1.0 + jnp.exp(-jnp.abs(x)))


def _dot(a, b, prec=None):
    return jnp.dot(a, b, preferred_element_type=F32, precision=prec)


def _dot_nt(a, b, prec=None):
    return lax.dot_general(a, b, (((1,), (1,)), ((), ())), preferred_element_type=F32, precision=prec)


def _dot_tn(a, b, prec=None):
    return lax.dot_general(a, b, (((0,), (0,)), ((), ())), preferred_element_type=F32, precision=prec)


def _b(x):
    return x.astype(BF16)


def _iota2(n, m, axis):
    return lax.broadcasted_iota(jnp.int32, (n, m), axis)


def _load_weights(pairs, sem):
    copies = [pltpu.make_async_copy(s, d, sem.at[i]) for i, (s, d) in enumerate(pairs)]
    for c in copies:
        c.start()
    for c in copies:
        c.wait()


def _ffn_fwd(h, nw, wg, wu, wd, name):
    T = h.shape[0]
    tm = min(ROW_BLK, T)

    def body(h_ref, nw_ref, wg_hbm, wu_hbm, wd_hbm, ho_ref, n_ref, g_ref, u_ref, wg_v, wu_v, wd_v, sem):
        @pl.when(pl.program_id(0) == 0)
        def _():
            _load_weights(((wg_hbm, wg_v), (wu_hbm, wu_v), (wd_hbm, wd_v)), sem)

        x = h_ref[...]
        r = lax.rsqrt(jnp.mean(x * x, axis=-1, keepdims=True) + EPS)
        nb = _b((x * r) * nw_ref[...])
        n_ref[...] = nb
        acc = jnp.zeros((tm, D), F32)
        for s in range(N_SHARD):
            g = _dot(nb, wg_v[s])
            u = _dot(nb, wu_v[s])
            g_ref[s] = _b(g)
            u_ref[s] = _b(u)
            acc += _dot(_b(g * _sigmoid(g) * u), wd_v[s])
        ho_ref[...] = x + FFN_RES * acc

    row = lambda w: pl.BlockSpec((tm, w), lambda i: (i, 0))
    cut = pl.BlockSpec((N_SHARD, tm, FF_CUT), lambda i: (0, i, 0))
    return pl.pallas_call(
        body, name=name, grid=(T // tm,),
        in_specs=[row(D), pl.BlockSpec((1, D), lambda i: (0, 0)), ANY, ANY, ANY],
        out_specs=[row(D), row(D), cut, cut],
        out_shape=[jax.ShapeDtypeStruct((T, D), F32), jax.ShapeDtypeStruct((T, D), BF16),
                   jax.ShapeDtypeStruct((N_SHARD, T, FF_CUT), BF16), jax.ShapeDtypeStruct((N_SHARD, T, FF_CUT), BF16)],
        scratch_shapes=[pltpu.VMEM((N_SHARD, D, FF_CUT), BF16), pltpu.VMEM((N_SHARD, D, FF_CUT), BF16),
                        pltpu.VMEM((N_SHARD, FF_CUT, D), BF16), pltpu.SemaphoreType.DMA((3,))],
        compiler_params=_cp("arbitrary"),
    )(h, nw, wg, wu, wd)


def _ffn_bwd(dh, h, nw, g, u, wg, wu, wd, name):
    T = h.shape[0]
    tm = min(ROW_BLK, T)

    def body(dh_ref, h_ref, nw_ref, g_ref, u_ref, wg_hbm, wu_hbm, wd_hbm,
             dx_ref, dg_ref, du_ref, a_ref, df_ref, dnw_ref, wg_v, wu_v, wd_v, sem):
        @pl.when(pl.program_id(0) == 0)
        def _():
            _load_weights(((wg_hbm, wg_v), (wu_hbm, wu_v), (wd_hbm, wd_v)), sem)
            dnw_ref[...] = jnp.zeros_like(dnw_ref)

        dh_ = dh_ref[...]
        dfb = _b(FFN_RES * dh_)
        df_ref[...] = dfb
        dn = jnp.zeros((tm, D), F32)
        for s in range(N_SHARD):
            da = _dot_nt(dfb, wd_v[s])
            gg = g_ref[s].astype(F32)
            uu = u_ref[s].astype(F32)
            sg = _sigmoid(gg)
            silu = gg * sg
            a_ref[s] = _b(silu * uu)
            dgb = _b(da * uu * (sg * (1.0 + gg * (1.0 - sg))))
            dub = _b(da * silu)
            dg_ref[s] = dgb
            du_ref[s] = dub
            dn += _dot_nt(dgb, wg_v[s]) + _dot_nt(dub, wu_v[s])
        x = h_ref[...]
        r = lax.rsqrt(jnp.mean(x * x, axis=-1, keepdims=True) + EPS)
        xhat = x * r
        dnw_ref[...] += jnp.sum(dn * xhat, axis=0, keepdims=True)
        dxhat = dn * nw_ref[...]
        dx_ref[...] = dh_ + r * (dxhat - xhat * jnp.mean(dxhat * xhat, axis=-1, keepdims=True))

    row = lambda w: pl.BlockSpec((tm, w), lambda i: (i, 0))
    one = pl.BlockSpec((1, D), lambda i: (0, 0))
    cut = pl.BlockSpec((N_SHARD, tm, FF_CUT), lambda i: (0, i, 0))
    cut_shape = jax.ShapeDtypeStruct((N_SHARD, T, FF_CUT), BF16)
    return pl.pallas_call(
        body, name=name, grid=(T // tm,),
        in_specs=[row(D), row(D), one, cut, cut, ANY, ANY, ANY],
        out_specs=[row(D), cut, cut, cut, row(D), one],
        out_shape=[jax.ShapeDtypeStruct((T, D), F32), cut_shape, cut_shape, cut_shape,
                   jax.ShapeDtypeStruct((T, D), BF16), jax.ShapeDtypeStruct((1, D), F32)],
        scratch_shapes=[pltpu.VMEM((N_SHARD, D, FF_CUT), BF16), pltpu.VMEM((N_SHARD, D, FF_CUT), BF16),
                        pltpu.VMEM((N_SHARD, FF_CUT, D), BF16), pltpu.SemaphoreType.DMA((3,))],
        compiler_params=_cp("arbitrary"),
    )(dh, h, nw, g, u, wg, wu, wd)


def _mm_tn(a, b, bm, bn, name, out_dtype=BF16, tk=512):
    cuts = a.shape[0] if a.ndim == 3 else (b.shape[0] if b.ndim == 3 else None)
    T, M = a.shape[-2:]
    N = b.shape[-1]
    tk = min(tk, T)
    bm, bn = min(bm, M), min(bn, N)
    nk = T // tk

    def body(a_ref, b_ref, o_ref, acc_ref):
        k = pl.program_id(3)

        @pl.when(k == 0)
        def _():
            acc_ref[...] = jnp.zeros_like(acc_ref)

        av = a_ref[0] if a.ndim == 3 else a_ref[...]
        bv = b_ref[0] if b.ndim == 3 else b_ref[...]
        acc_ref[...] += _dot_tn(_b(av), _b(bv))

        @pl.when(k == nk - 1)
        def _():
            res = acc_ref[...].astype(out_dtype)
            if cuts is None:
                o_ref[...] = res
            else:
                o_ref[0] = res

    a_spec = (pl.BlockSpec((1, tk, bm), lambda s, i, j, k: (s, k, i)) if a.ndim == 3
              else pl.BlockSpec((tk, bm), lambda s, i, j, k: (k, i)))
    b_spec = (pl.BlockSpec((1, tk, bn), lambda s, i, j, k: (s, k, j)) if b.ndim == 3
              else pl.BlockSpec((tk, bn), lambda s, i, j, k: (k, j)))
    if cuts is None:
        o_spec, o_shape = pl.BlockSpec((bm, bn), lambda s, i, j, k: (i, j)), (M, N)
    else:
        o_spec, o_shape = pl.BlockSpec((1, bm, bn), lambda s, i, j, k: (s, i, j)), (cuts, M, N)
    return pl.pallas_call(
        body, name=name, grid=(cuts or 1, M // bm, N // bn, nk),
        in_specs=[a_spec, b_spec], out_specs=o_spec,
        out_shape=jax.ShapeDtypeStruct(o_shape, out_dtype),
        scratch_shapes=[pltpu.VMEM((bm, bn), F32)],
        compiler_params=_cp("parallel", "parallel", "parallel", "arbitrary"),
    )(a, b)


def _norm_proj(h, nw, wbig, wsmall, name):
    T = h.shape[0]
    tm = min(512, T)
    tn = 1536

    def body(h_ref, nw_ref, wb_ref, ws_ref, pb_ref, ps_ref, n_ref):
        @pl.when(pl.program_id(1) == 0)
        def _():
            x = h_ref[...]
            r = lax.rsqrt(jnp.mean(x * x, axis=-1, keepdims=True) + EPS)
            nb = _b((x * r) * nw_ref[...])
            n_ref[...] = nb
            ps_ref[...] = _dot(nb, ws_ref[...])

        pb_ref[...] = _b(_dot(n_ref[...], wb_ref[...]))

    return pl.pallas_call(
        body, name=name, grid=(T // tm, BIG_COLS // tn),
        in_specs=[pl.BlockSpec((tm, D), lambda i, j: (i, 0)), pl.BlockSpec((1, D), lambda i, j: (0, 0)),
                  pl.BlockSpec((D, tn), lambda i, j: (0, j)), pl.BlockSpec((D, SMALL), lambda i, j: (0, 0))],
        out_specs=[pl.BlockSpec((tm, tn), lambda i, j: (i, j)), pl.BlockSpec((tm, SMALL), lambda i, j: (i, 0)),
                   pl.BlockSpec((tm, D), lambda i, j: (i, 0))],
        out_shape=[jax.ShapeDtypeStruct((T, BIG_COLS), BF16), jax.ShapeDtypeStruct((T, SMALL), F32),
                   jax.ShapeDtypeStruct((T, D), BF16)],
        compiler_params=_cp("parallel", "arbitrary"),
    )(h, nw, wbig, wsmall)


def _proj_bwd(dh, h, nw, pieces, dsmall, wbig, wsmall, name):
    T = h.shape[0]
    tm = min(ROW_BLK, T)
    offs = tuple(int(v) for v in np.cumsum((0,) + PIECES))

    def body(dh_ref, h_ref, nw_ref, *rest):
        p_refs = rest[:len(PIECES)]
        ds_ref, wb_hbm, ws_ref, dx_ref, dnw_ref, wb_v, sem = rest[len(PIECES):]

        @pl.when(pl.program_id(0) == 0)
        def _():
            _load_weights(((wb_hbm, wb_v),), sem)
            dnw_ref[...] = jnp.zeros_like(dnw_ref)

        dn = _dot_nt(_b(ds_ref[...]), ws_ref[...])
        for p_ref, lo, wdt in zip(p_refs, offs, PIECES):
            dn += _dot_nt(p_ref[...], wb_v[:, lo:lo + wdt])
        x = h_ref[...]
        r = lax.rsqrt(jnp.mean(x * x, axis=-1, keepdims=True) + EPS)
        xhat = x * r
        dnw_ref[...] += jnp.sum(dn * xhat, axis=0, keepdims=True)
        dxhat = dn * nw_ref[...]
        dx_ref[...] = dh_ref[...] + r * (dxhat - xhat * jnp.mean(dxhat * xhat, axis=-1, keepdims=True))

    row = lambda w: pl.BlockSpec((tm, w), lambda i: (i, 0))
    one = pl.BlockSpec((1, D), lambda i: (0, 0))
    return pl.pallas_call(
        body, name=name, grid=(T // tm,),
        in_specs=[row(D), row(D), one] + [row(w) for w in PIECES] + [row(SMALL), ANY, pl.BlockSpec((D, SMALL), lambda i: (0, 0))],
        out_specs=[row(D), one],
        out_shape=[jax.ShapeDtypeStruct((T, D), F32), jax.ShapeDtypeStruct((1, D), F32)],
        scratch_shapes=[pltpu.VMEM((D, BIG_COLS), BF16), pltpu.SemaphoreType.DMA((1,))],
        compiler_params=_cp("arbitrary"),
    )(dh, h, nw, *pieces, dsmall, wbig, wsmall)


def _gla_chunk(q_ref, k_ref, sm_ref, wg_ref, bg_ref, rows, tril):
    q = q_ref[rows, :].astype(F32)
    k = k_ref[rows, :].astype(F32)
    pre = _dot(sm_ref[rows, :], wg_ref[...], HI) + bg_ref[...]
    la = _log_sigmoid(pre) * (1.0 / GLA_TAU)
    bc = _dot(tril, la, HI)
    bl = bc[CHUNK - 1:CHUNK, :]
    eb = jnp.exp(bc)
    enb = jnp.exp(-bc)
    ebl = jnp.exp(bl - bc)
    q_in = q * (GLA_DK ** -0.5) * eb
    k_out = k * enb
    k_st = k * ebl
    a_ch = jnp.exp(bl)
    return pre, eb, enb, ebl, q_in, k_out, k_st, a_ch


def _gla_specs(blk):
    return [pl.BlockSpec((blk, GLA_DK), lambda h, j: (j, h)),
            pl.BlockSpec((blk, GLA_DK), lambda h, j: (j, GLA_H + h)),
            pl.BlockSpec((blk, GLA_DV), lambda h, j: (j, GLA_H + h)),
            pl.BlockSpec((blk, SMALL), lambda h, j: (j, 0)),
            pl.BlockSpec((SMALL, GLA_DK), lambda h, j: (0, h)),
            pl.BlockSpec((1, GLA_DK), lambda h, j: (0, h))]


def _gla_fwd(pbig, psmall, wgate, bgate, name):
    T = pbig.shape[0]
    blk = min(ATT_BLK, T)
    nc = blk // CHUNK

    def body(q_ref, k_ref, v_ref, sm_ref, wg_ref, bg_ref, o_ref, ss_ref, st_ref):
        @pl.when(pl.program_id(1) == 0)
        def _():
            st_ref[...] = jnp.zeros_like(st_ref)

        causal = _iota2(CHUNK, CHUNK, 0) >= _iota2(CHUNK, CHUNK, 1)
        tril = causal.astype(F32)
        st = st_ref[...]
        for c in range(nc):
            rows = pl.ds(c * CHUNK, CHUNK)
            _, _, _, _, q_in, k_out, k_st, a_ch = _gla_chunk(q_ref, k_ref, sm_ref, wg_ref, bg_ref, rows, tril)
            v = v_ref[rows, :]
            sc = jnp.where(causal, _dot_nt(_b(q_in), _b(k_out)), 0.0)
            ss_ref[0, c] = st
            o_ref[rows, :] = _dot(_b(sc), v) + _dot_nt(_b(q_in), _b(st))
            st = st * a_ch + _dot_tn(v, _b(k_st))
        st_ref[...] = st

    return pl.pallas_call(
        body, name=name, grid=(GLA_H, T // blk),
        in_specs=_gla_specs(blk),
        out_specs=[pl.BlockSpec((blk, GLA_DV), lambda h, j: (j, h)),
                   pl.BlockSpec((1, nc, GLA_DV, GLA_DK), lambda h, j: (h, j, 0, 0))],
        out_shape=[jax.ShapeDtypeStruct((T, GLA_H * GLA_DV), F32),
                   jax.ShapeDtypeStruct((GLA_H, T // CHUNK, GLA_DV, GLA_DK), F32)],
        scratch_shapes=[pltpu.VMEM((GLA_DV, GLA_DK), F32)],
        compiler_params=_cp("parallel", "arbitrary"),
    )(pbig, pbig, pbig, psmall, wgate, bgate)


def _gla_bwd(pbig, psmall, wgate, bgate, states, do, name):
    T = pbig.shape[0]
    blk = min(ATT_BLK, T)
    nc = blk // CHUNK
    nb = T // blk

    def body(q_ref, k_ref, v_ref, sm_ref, wg_ref, bg_ref, ss_ref, do_ref, dq_ref, dk_ref, dv_ref, dpre_ref, dst_ref):
        @pl.when(pl.program_id(1) == 0)
        def _():
            dst_ref[...] = jnp.zeros_like(dst_ref)

        causal = _iota2(CHUNK, CHUNK, 0) >= _iota2(CHUNK, CHUNK, 1)
        tril = causal.astype(F32)
        triu = (_iota2(CHUNK, CHUNK, 0) <= _iota2(CHUNK, CHUNK, 1)).astype(F32)
        dst = dst_ref[...]
        for c in range(nc - 1, -1, -1):
            rows = pl.ds(c * CHUNK, CHUNK)
            pre, eb, enb, ebl, q_in, k_out, k_st, a_ch = _gla_chunk(q_ref, k_ref, sm_ref, wg_ref, bg_ref, rows, tril)
            v = v_ref[rows, :]
            st = ss_ref[0, c]
            dob = _b(do_ref[rows, :])
            sc = jnp.where(causal, _dot_nt(_b(q_in), _b(k_out)), 0.0)
            dsc = _b(jnp.where(causal, _dot_nt(dob, v), 0.0))
            dq_in = _dot(dob, _b(st)) + _dot(dsc, _b(k_out))
            dk_out = _dot_tn(dsc, _b(q_in))
            dk_st = _dot(v, _b(dst))
            dv_ref[rows, :] = _b(_dot_tn(_b(sc), dob) + _dot_nt(_b(k_st), _b(dst)))
            da_ch = jnp.sum(st * dst, axis=0, keepdims=True)
            dst = dst * a_ch + _dot_tn(dob, _b(q_in))
            tk = dk_st * k_st
            db = dq_in * q_in - dk_out * k_out - tk
            db_last = jnp.sum(tk, axis=0, keepdims=True) + da_ch * a_ch
            dq_ref[rows, :] = _b(dq_in * (GLA_DK ** -0.5) * eb)
            dk_ref[rows, :] = _b(dk_out * enb + dk_st * ebl)
            dla = _dot(triu, db, HI) + db_last
            dpre_ref[rows, :] = dla * (1.0 / GLA_TAU) * _sigmoid(-pre)
        dst_ref[...] = dst

    rev = lambda f: (lambda h, j: f(h, nb - 1 - j))
    specs = [pl.BlockSpec(s.block_shape, rev(s.index_map)) for s in _gla_specs(blk)]
    return pl.pallas_call(
        body, name=name, grid=(GLA_H, nb),
        in_specs=specs + [pl.BlockSpec((1, nc, GLA_DV, GLA_DK), lambda h, j: (h, nb - 1 - j, 0, 0)),
                          pl.BlockSpec((blk, GLA_DV), lambda h, j: (nb - 1 - j, h))],
        out_specs=[pl.BlockSpec((blk, GLA_DK), lambda h, j: (nb - 1 - j, h)),
                   pl.BlockSpec((blk, GLA_DK), lambda h, j: (nb - 1 - j, h)),
                   pl.BlockSpec((blk, GLA_DV), lambda h, j: (nb - 1 - j, h)),
                   pl.BlockSpec((blk, GLA_DK), lambda h, j: (nb - 1 - j, h))],
        out_shape=[jax.ShapeDtypeStruct((T, GLA_H * GLA_DK), BF16), jax.ShapeDtypeStruct((T, GLA_H * GLA_DK), BF16),
                   jax.ShapeDtypeStruct((T, GLA_H * GLA_DV), BF16), jax.ShapeDtypeStruct((T, GLA_H * GLA_DK), F32)],
        scratch_shapes=[pltpu.VMEM((GLA_DV, GLA_DK), F32)],
        compiler_params=_cp("parallel", "arbitrary"),
    )(pbig, pbig, pbig, psmall, wgate, bgate, states, do)


def _gla_gate_bwd(dpre, psmall, wgate, dsm, name):
    T = dpre.shape[0]
    tm = min(512, T)
    W = GLA_H * GLA_DK
    ngrp = dsm.shape[0]

    def body(dp_ref, sm_ref, wg_ref, dsm_ref, ds_ref, dw_ref, db_ref):
        @pl.when(pl.program_id(0) == 0)
        def _():
            dw_ref[...] = jnp.zeros_like(dw_ref)
            db_ref[...] = jnp.zeros_like(db_ref)

        dp = dp_ref[...]
        ds = _dot_nt(dp, wg_ref[...], HI)
        for i in range(ngrp):
            ds += dsm_ref[i]
        ds_ref[...] = ds
        dw_ref[...] += _dot_tn(sm_ref[...], dp, HI)
        db_ref[...] += jnp.sum(dp, axis=0, keepdims=True)

    return pl.pallas_call(
        body, name=name, grid=(T // tm,),
        in_specs=[pl.BlockSpec((tm, W), lambda i: (i, 0)), pl.BlockSpec((tm, SMALL), lambda i: (i, 0)),
                  pl.BlockSpec((SMALL, W), lambda i: (0, 0)), pl.BlockSpec((ngrp, tm, SMALL), lambda i: (0, i, 0))],
        out_specs=[pl.BlockSpec((tm, SMALL), lambda i: (i, 0)), pl.BlockSpec((SMALL, W), lambda i: (0, 0)),
                   pl.BlockSpec((1, W), lambda i: (0, 0))],
        out_shape=[jax.ShapeDtypeStruct((T, SMALL), F32), jax.ShapeDtypeStruct((SMALL, W), F32),
                   jax.ShapeDtypeStruct((1, W), F32)],
        compiler_params=_cp("arbitrary"),
    )(dpre, psmall, wgate, dsm)


CONV_C = 3 * 1024
CONV_BLK = 256


def _conv_fwd(pbig, cw8, name):
    T = pbig.shape[0]
    blk = min(CONV_BLK, T)

    def body(x_ref, w_ref, c_ref, prev_ref):
        @pl.when(pl.program_id(0) == 0)
        def _():
            prev_ref[...] = jnp.zeros_like(prev_ref)

        x = x_ref[...].astype(F32)
        prev = prev_ref[...]
        row8 = _iota2(8, CONV_C, 0)
        acc = x * w_ref[CONV_K - 1:CONV_K, :]
        for s in range(1, CONV_K):
            xs = pltpu.roll(x, s, 0)
            top = jnp.where(row8 < s, pltpu.roll(prev, s, 0), xs[:8])
            xs = jnp.concatenate([top, xs[8:]], axis=0)
            acc += xs * w_ref[CONV_K - 1 - s:CONV_K - s, :]
        c_ref[...] = _b(acc)
        prev_ref[...] = x[blk - 8:]

    return pl.pallas_call(
        body, name=name, grid=(T // blk,),
        in_specs=[pl.BlockSpec((blk, CONV_C), lambda i: (i, 1)), pl.BlockSpec((8, CONV_C), lambda i: (0, 0))],
        out_specs=pl.BlockSpec((blk, CONV_C), lambda i: (i, 0)),
        out_shape=jax.ShapeDtypeStruct((T, CONV_C), BF16),
        scratch_shapes=[pltpu.VMEM((8, CONV_C), F32)],
        compiler_params=_cp("arbitrary"),
    )(pbig, cw8)


def _conv_bwd(dcq, dck, dcv, pbig, cw8, name):
    T = pbig.shape[0]
    blk = min(CONV_BLK, T)
    nb = T // blk

    def body(dq_ref, dk_ref, dv_ref, x_ref, w_ref, dx_ref, dw_ref, nxt_ref):
        @pl.when(pl.program_id(0) == 0)
        def _():
            nxt_ref[...] = jnp.zeros_like(nxt_ref)
            dw_ref[...] = jnp.zeros_like(dw_ref)

        dc = jnp.concatenate([dq_ref[...], dk_ref[...], dv_ref[...]], axis=1).astype(F32)
        x = x_ref[...].astype(F32)
        nxt = nxt_ref[...]
        row8 = _iota2(8, CONV_C, 0)
        acc = dc * w_ref[CONV_K - 1:CONV_K, :]
        dws = [jnp.sum(dc * x, axis=0, keepdims=True)]
        for s in range(1, CONV_K):
            ds = pltpu.roll(dc, blk - s, 0)
            bot = jnp.where(row8 >= 8 - s, pltpu.roll(nxt, 8 - s, 0), ds[blk - 8:])
            ds = jnp.concatenate([ds[:blk - 8], bot], axis=0)
            acc += ds * w_ref[CONV_K - 1 - s:CONV_K - s, :]
            dws.append(jnp.sum(ds * x, axis=0, keepdims=True))
        dx_ref[...] = _b(acc)
        dw_ref[...] += jnp.concatenate(dws[::-1] + [jnp.zeros((8 - CONV_K, CONV_C), F32)], axis=0)
        nxt_ref[...] = dc[:8]

    part = pl.BlockSpec((blk, 1024), lambda i: (nb - 1 - i, 0))
    return pl.pallas_call(
        body, name=name, grid=(nb,),
        in_specs=[part, part, part, pl.BlockSpec((blk, CONV_C), lambda i: (nb - 1 - i, 1)),
                  pl.BlockSpec((8, CONV_C), lambda i: (0, 0))],
        out_specs=[pl.BlockSpec((blk, CONV_C), lambda i: (nb - 1 - i, 0)), pl.BlockSpec((8, CONV_C), lambda i: (0, 0))],
        out_shape=[jax.ShapeDtypeStruct((T, CONV_C), BF16), jax.ShapeDtypeStruct((8, CONV_C), F32)],
        scratch_shapes=[pltpu.VMEM((8, CONV_C), F32)],
        compiler_params=_cp("arbitrary"),
    )(dcq, dck, dcv, pbig, cw8)


def _col(x, lane):
    sel = _iota2(x.shape[0], x.shape[1], 1) == lane
    return jnp.broadcast_to(jnp.sum(jnp.where(sel, x, 0.0), axis=1, keepdims=True), x.shape)


def _bmm(a, b, prec=None):
    return jnp.einsum("bij,bjk->bik", a, b, preferred_element_type=F32, precision=prec)


def _bmm_nt(a, b, prec=None):
    return jnp.einsum("bij,bkj->bik", a, b, preferred_element_type=F32, precision=prec)


def _bmm_tn(a, b, prec=None):
    return jnp.einsum("bji,bjk->bik", a, b, preferred_element_type=F32, precision=prec)


def _unit_lower_inverse(low):
    eye = (_iota2(CHUNK, CHUNK, 0) == _iota2(CHUNK, CHUNK, 1)).astype(F32)
    xk = -low
    inv = eye + xk
    for _ in range(5):
        xk = _bmm(xk, xk, HI)
        inv = inv + _bmm(inv, xk, HI)
    return inv


def _heads_first(x, nc):
    hb = x.shape[1] // 128
    return jnp.concatenate([x[:, i * 128:(i + 1) * 128].reshape(nc, CHUNK, 128) for i in range(hb)], axis=0)


def _heads_last(x, nc):
    hb = x.shape[0] // nc
    return jnp.concatenate([x[i * nc:(i + 1) * nc].reshape(nc * CHUNK, 128) for i in range(hb)], axis=1)


def _gdn_block(cq_ref, ck_ref, cv_ref, sm_ref, par_ref, h0, hb, nc, masks):
    causal, strict, tril, eye = masks
    nbat = hb * nc
    cq = _heads_first(cq_ref[...].astype(F32), nc)
    ck = _heads_first(ck_ref[...].astype(F32), nc)
    cv = _heads_first(cv_ref[...].astype(F32), nc)
    sq, sk, sv = _sigmoid(cq), _sigmoid(ck), _sigmoid(cv)
    q, k, v = cq * sq, ck * sk, cv * sv
    rq = lax.rsqrt(jnp.sum(q * q, axis=-1, keepdims=True) + EPS)
    rk = lax.rsqrt(jnp.sum(k * k, axis=-1, keepdims=True) + EPS)
    qh, kn = q * rq, k * rk
    qn = qh * (DN_DK ** -0.5)
    sm = sm_ref[...]
    per_head = lambda fn: jnp.concatenate([fn(i) for i in range(hb)], axis=0)
    braw = per_head(lambda i: _col(sm, GLA_RANK + h0 + i).reshape(nc, CHUNK, 128))
    araw = per_head(lambda i: _col(sm, GLA_RANK + DN_H + h0 + i).reshape(nc, CHUNK, 128))
    ea = per_head(lambda i: jnp.broadcast_to(jnp.exp(par_ref[i, 0:1, :])[None], (nc, 1, 128)))
    bias = per_head(lambda i: jnp.broadcast_to(par_ref[i, 1:2, :][None], (nc, 1, 128)))
    beta = _sigmoid(braw)
    sp_arg = araw + bias
    g = -ea * _softplus(sp_arg)
    G = _bmm(jnp.broadcast_to(tril, (nbat, CHUNK, CHUNK)), g, HI)
    gc = G[:, :, :CHUNK]
    grow = jnp.sum(eye * gc, axis=1, keepdims=True)
    decay = jnp.exp(jnp.where(causal, gc - grow, -1e30))
    kb = kn * beta
    A = _bmm_nt(kb, kn, HI)
    low = jnp.where(strict, A * decay, 0.0)
    tinv = _unit_lower_inverse(low)
    eG = jnp.exp(G)
    gl = G[:, CHUNK - 1:CHUNK, :]
    eGl = jnp.exp(gl - G)
    g_ch = jnp.exp(gl)
    rv = v * beta
    rkk = kb * eG
    u = _bmm(tinv, rv, HI)
    w = _bmm(tinv, rkk, HI)
    B = _bmm_nt(_b(qn), _b(kn))
    qk = jnp.where(causal, B * decay, 0.0)
    q_dec = qn * eG
    k_st = kn * eGl
    return dict(cq=cq, ck=ck, cv=cv, sq=sq, sk=sk, sv=sv, q=q, k=k, v=v, rq=rq, rk=rk, qh=qh, kn=kn, qn=qn,
                beta=beta, ea=ea, sp_arg=sp_arg, g=g, G=G, decay=decay, kb=kb, A=A, tinv=tinv, eG=eG, eGl=eGl,
                g_ch=g_ch, rv=rv, rkk=rkk, u=u, w=w, B=B, qk=qk, q_dec=q_dec, k_st=k_st)


def _gdn_masks():
    r, c = _iota2(CHUNK, CHUNK, 0), _iota2(CHUNK, CHUNK, 1)
    return r >= c, r > c, (r >= c).astype(F32), (r == c).astype(F32)


def _gdn_specs(blk, hb, idx):
    ng = DN_H // hb
    return [pl.BlockSpec((blk, hb * DN_DK), lambda h, j: (idx(j), h)),
            pl.BlockSpec((blk, hb * DN_DK), lambda h, j: (idx(j), ng + h)),
            pl.BlockSpec((blk, hb * DN_DV), lambda h, j: (idx(j), 2 * ng + h)),
            pl.BlockSpec((blk, SMALL), lambda h, j: (idx(j), 0)),
            pl.BlockSpec((hb, 8, 128), lambda h, j: (h, 0, 0))]


def _gdn_fwd(conv, psmall, par, name):
    T = conv.shape[0]
    blk = min(GDN_BLK, T)
    nc = blk // CHUNK
    hb = GDN_HEADS

    def body(cq_ref, ck_ref, cv_ref, sm_ref, par_ref, o_ref, ss_ref, s_ref):
        @pl.when(pl.program_id(1) == 0)
        def _():
            s_ref[...] = jnp.zeros_like(s_ref)

        f = _gdn_block(cq_ref, ck_ref, cv_ref, sm_ref, par_ref, pl.program_id(0) * hb, hb, nc, _gdn_masks())
        wb, qdb, kstb, qkb = _b(f["w"]), _b(f["q_dec"]), _b(f["k_st"]), _b(f["qk"])
        S = [s_ref[i] for i in range(hb)]
        for c in range(nc):
            for i in range(hb):
                n = i * nc + c
                ss_ref[i, c] = S[i]
                Sb = _b(S[i])
                v_new = _b(f["u"][n] - _dot(wb[n], Sb))
                o_ref[pl.ds(c * CHUNK, CHUNK), i * DN_DV:(i + 1) * DN_DV] = _dot(qdb[n], Sb) + _dot(qkb[n], v_new)
                S[i] = S[i] * f["g_ch"][n] + _dot_tn(kstb[n], v_new)
        for i in range(hb):
            s_ref[i] = S[i]

    return pl.pallas_call(
        body, name=name, grid=(DN_H // hb, T // blk),
        in_specs=_gdn_specs(blk, hb, lambda j: j),
        out_specs=[pl.BlockSpec((blk, hb * DN_DV), lambda h, j: (j, h)),
                   pl.BlockSpec((hb, nc, DN_DK, DN_DV), lambda h, j: (h, j, 0, 0))],
        out_shape=[jax.ShapeDtypeStruct((T, DN_H * DN_DV), F32),
                   jax.ShapeDtypeStruct((DN_H, T // CHUNK, DN_DK, DN_DV), F32)],
        scratch_shapes=[pltpu.VMEM((hb, DN_DK, DN_DV), F32)],
        compiler_params=_cp("parallel", "arbitrary"),
    )(conv, conv, conv, psmall, par)


def _gdn_bwd(conv, psmall, par, states, do, name):
    T = conv.shape[0]
    blk = min(GDN_BLK, T)
    nc = blk // CHUNK
    nb = T // blk
    hb = GDN_HEADS
    nbat = hb * nc
    rsum = lambda x: jnp.sum(x, axis=-1, keepdims=True)

    def body(cq_ref, ck_ref, cv_ref, sm_ref, par_ref, ss_ref, do_ref,
             dcq_ref, dck_ref, dcv_ref, dsm_ref, dpar_ref, ds_ref):
        @pl.when(pl.program_id(1) == 0)
        def _():
            ds_ref[...] = jnp.zeros_like(ds_ref)
            dpar_ref[...] = jnp.zeros_like(dpar_ref)

        masks = _gdn_masks()
        causal, strict, tril, eye = masks
        triu = (_iota2(CHUNK, CHUNK, 0) <= _iota2(CHUNK, CHUNK, 1)).astype(F32)
        lane = _iota2(CHUNK, 128, 1)
        last_row = _iota2(CHUNK, 128, 0) == CHUNK - 1
        h0 = pl.program_id(0) * hb
        f = _gdn_block(cq_ref, ck_ref, cv_ref, sm_ref, par_ref, h0, hb, nc, masks)
        S = ss_ref[...].reshape(nbat, DN_DK, DN_DV)
        Sb = _b(S)
        do_ = _b(_heads_first(do_ref[...], nc))
        wb, qdb, kstb, qkb = _b(f["w"]), _b(f["q_dec"]), _b(f["k_st"]), _b(f["qk"])
        vnb = _b(f["u"] - _bmm(wb, Sb))
        dvn0 = _bmm_tn(qkb, do_)
        qdo = _bmm_tn(qdb, do_)
        dS = [ds_ref[i] for i in range(hb)]
        after = [None] * nbat
        for c in range(nc - 1, -1, -1):
            for i in range(hb):
                n = i * nc + c
                after[n] = dS[i]
                dvn_c = _b(dvn0[n] + _dot(kstb[n], _b(dS[i])))
                dS[i] = dS[i] * f["g_ch"][n] + qdo[n] - _dot_tn(wb[n], dvn_c)
        for i in range(hb):
            ds_ref[i] = dS[i]
        dSa = jnp.stack(after)
        dSb = _b(dSa)
        dvn = dvn0 + _bmm(kstb, dSb)
        dvnb = _b(dvn)
        dq_dec = _bmm_nt(do_, Sb)
        dqk = jnp.where(causal, _bmm_nt(do_, vnb), 0.0)
        dk_st = _bmm_nt(vnb, dSb)
        dg_ch = jnp.sum(rsum(S * dSa), axis=1, keepdims=True)
        dw = -_bmm_nt(dvnb, Sb)
        drv = _bmm_tn(f["tinv"], dvn, HI)
        drk = _bmm_tn(f["tinv"], dw, HI)
        dlow = jnp.where(strict, -(_bmm_nt(drv, f["u"], HI) + _bmm_nt(drk, f["w"], HI)), 0.0)
        dv = drv * f["beta"]
        dbeta = rsum(drv * f["v"])
        dkb = drk * f["eG"]
        dG = rsum(drk * f["rkk"])
        dA = dlow * f["decay"]
        ddec = dlow * f["A"]
        dkb += _bmm(dA, f["kn"], HI)
        dkn = _bmm_tn(dA, f["kb"], HI)
        dB = dqk * f["decay"]
        ddec += dqk * f["B"]
        dqn = _bmm(_b(dB), _b(f["kn"]))
        dkn += _bmm_tn(_b(dB), _b(f["qn"]))
        dD = ddec * f["decay"]
        dG += rsum(dD) - rsum(eye * jnp.sum(dD, axis=1, keepdims=True))
        dqn += dq_dec * f["eG"]
        dG += rsum(dq_dec * f["q_dec"])
        dkn += dk_st * f["eGl"]
        tks = rsum(dk_st * f["k_st"])
        dG -= tks
        dG_last = jnp.sum(tks, axis=1, keepdims=True) + dg_ch * f["g_ch"][:, :, :1]
        dkn += dkb * f["beta"]
        dbeta += rsum(dkb * f["kn"])
        dGf = jnp.broadcast_to(dG, (nbat, CHUNK, 128)) + jnp.where(last_row, dG_last, 0.0)
        dg = _bmm(jnp.broadcast_to(triu, (nbat, CHUNK, CHUNK)), dGf, HI)
        dbraw = dbeta * f["beta"][:, :, :1] * (1.0 - f["beta"][:, :, :1])
        daraw = dg * (-f["ea"]) * _sigmoid(f["sp_arg"])
        both = lambda t: jnp.sum(jnp.sum(t, axis=1, keepdims=True), axis=0)
        dgg = dg * f["g"]
        dsm = jnp.zeros((nc, CHUNK, SMALL), F32)
        for i in range(hb):
            mine = slice(i * nc, (i + 1) * nc)
            dsm += (jnp.where(lane == GLA_RANK + h0 + i, dbraw[mine], 0.0)
                    + jnp.where(lane == GLA_RANK + DN_H + h0 + i, daraw[mine], 0.0))
            dpar = jnp.where(lane[:1] == 0, both(dgg[mine]), jnp.where(lane[:1] == 1, both(daraw[mine]), 0.0))
            dpar_ref[i] += jnp.broadcast_to(dpar, (8, 128))
        dsm_ref[0] = dsm.reshape(blk, SMALL)
        dqh = dqn * (DN_DK ** -0.5)
        dq = f["rq"] * (dqh - f["qh"] * rsum(dqh * f["qh"]))
        dk = f["rk"] * (dkn - f["kn"] * rsum(dkn * f["kn"]))
        dsilu = lambda x, s: s * (1.0 + x * (1.0 - s))
        dcq_ref[...] = _b(_heads_last(dq * dsilu(f["cq"], f["sq"]), nc))
        dck_ref[...] = _b(_heads_last(dk * dsilu(f["ck"], f["sk"]), nc))
        dcv_ref[...] = _b(_heads_last(dv * dsilu(f["cv"], f["sv"]), nc))

    r = lambda j: nb - 1 - j
    out_blk = pl.BlockSpec((blk, hb * DN_DK), lambda h, j: (r(j), h))
    return pl.pallas_call(
        body, name=name, grid=(DN_H // hb, nb),
        in_specs=_gdn_specs(blk, hb, r) + [pl.BlockSpec((hb, nc, DN_DK, DN_DV), lambda h, j: (h, r(j), 0, 0)),
                                          pl.BlockSpec((blk, hb * DN_DV), lambda h, j: (r(j), h))],
        out_specs=[out_blk, out_blk, out_blk, pl.BlockSpec((1, blk, SMALL), lambda h, j: (h, r(j), 0)),
                   pl.BlockSpec((hb, 8, 128), lambda h, j: (h, 0, 0))],
        out_shape=[jax.ShapeDtypeStruct((T, DN_H * DN_DK), BF16)] * 3 + [
            jax.ShapeDtypeStruct((DN_H // hb, T, SMALL), F32), jax.ShapeDtypeStruct((DN_H, 8, 128), F32)],
        scratch_shapes=[pltpu.VMEM((hb, DN_DK, DN_DV), F32)],
        compiler_params=_cp("parallel", "arbitrary"),
    )(conv, conv, conv, psmall, par, states, do)


def _head_norm(o, w, dv):
    outs, rs = [], []
    for i in range(o.shape[1] // dv):
        oh = o[:, i * dv:(i + 1) * dv]
        r = lax.rsqrt(jnp.mean(oh * oh, axis=-1, keepdims=True) + EPS)
        outs.append(oh * r)
        rs.append(r)
    return outs, rs


def _merge_specs(tm):
    col = lambda c: pl.BlockSpec((tm, D), lambda i: (i, c))
    return [col(0), col(0), col(2), col(6), col(7), col(8),
            pl.BlockSpec((1, GLA_DV), lambda i: (0, 0)), pl.BlockSpec((1, DN_DV), lambda i: (0, 0)),
            pl.BlockSpec((D, D), lambda i: (0, 0))]


def _merge_fwd(h, oa, ob, pbig, gla_hn, dn_hn, wout, name):
    T = h.shape[0]
    tm = min(ROW_BLK, T)

    def body(h_ref, oa_ref, ob_ref, gr_ref, dg_ref, ma_ref, mb_ref, wa_ref, wb_ref, wo_ref, ho_ref, y_ref):
        na, _ = _head_norm(oa_ref[...], wa_ref[...], GLA_DV)
        nbs, _ = _head_norm(ob_ref[...], wb_ref[...], DN_DV)
        hna = jnp.concatenate([t * wa_ref[...] for t in na], axis=1)
        hnb = jnp.concatenate([t * wb_ref[...] for t in nbs], axis=1)
        gr = gr_ref[...].astype(F32)
        dg = dg_ref[...].astype(F32)
        y = (_sigmoid(ma_ref[...].astype(F32)) * hna * (gr * _sigmoid(gr))
             + _sigmoid(mb_ref[...].astype(F32)) * hnb * (dg * _sigmoid(dg)))
        yb = _b(y)
        y_ref[...] = yb
        ho_ref[...] = h_ref[...] + _dot(yb, wo_ref[...])

    row = pl.BlockSpec((tm, D), lambda i: (i, 0))
    return pl.pallas_call(
        body, name=name, grid=(T // tm,),
        in_specs=[row] + _merge_specs(tm),
        out_specs=[row, row],
        out_shape=[jax.ShapeDtypeStruct((T, D), F32), jax.ShapeDtypeStruct((T, D), BF16)],
        compiler_params=_cp("arbitrary"),
    )(h, oa, ob, pbig, pbig, pbig, pbig, gla_hn, dn_hn, wout)


def _merge_bwd(dh, oa, ob, pbig, gla_hn, dn_hn, wout, name):
    T = dh.shape[0]
    tm = min(ROW_BLK, T)

    def branch(dy, o_ref, w_ref, gate_ref, m_ref, dv):
        w = w_ref[...]
        ohat, rs = _head_norm(o_ref[...], w, dv)
        gate = gate_ref[...].astype(F32)
        m = m_ref[...].astype(F32)
        sgate, sm = _sigmoid(gate), _sigmoid(m)
        silu = gate * sgate
        ohat_all = jnp.concatenate(ohat, axis=1)
        hn = jnp.concatenate([t * w for t in ohat], axis=1)
        d_on = dy * sm
        d_m = dy * hn * silu * sm * (1.0 - sm)
        d_hn = d_on * silu
        d_gate = d_on * hn * (sgate * (1.0 + gate * (1.0 - sgate)))
        dw = jnp.zeros((1, dv), F32)
        d_o = []
        for i, (oh, r) in enumerate(zip(ohat, rs)):
            dhn = d_hn[:, i * dv:(i + 1) * dv]
            dw += jnp.sum(dhn * oh, axis=0, keepdims=True)
            dohat = dhn * w
            d_o.append(r * (dohat - oh * jnp.mean(dohat * oh, axis=-1, keepdims=True)))
        return jnp.concatenate(d_o, axis=1), d_gate, d_m, dw

    def body(dh_ref, oa_ref, ob_ref, gr_ref, dg_ref, ma_ref, mb_ref, wa_ref, wb_ref, wo_ref,
             doa_ref, dob_ref, dgr_ref, ddg_ref, dma_ref, dmb_ref, dwa_ref, dwb_ref, dhb_ref):
        @pl.when(pl.program_id(0) == 0)
        def _():
            dwa_ref[...] = jnp.zeros_like(dwa_ref)
            dwb_ref[...] = jnp.zeros_like(dwb_ref)

        dhb = _b(dh_ref[...])
        dhb_ref[...] = dhb
        dy = _dot_nt(dhb, wo_ref[...])
        d_oa, d_gr, d_ma, dwa = branch(dy, oa_ref, wa_ref, gr_ref, ma_ref, GLA_DV)
        d_ob, d_dg, d_mb, dwb = branch(dy, ob_ref, wb_ref, dg_ref, mb_ref, DN_DV)
        doa_ref[...] = d_oa
        dob_ref[...] = d_ob
        dgr_ref[...] = _b(d_gr)
        ddg_ref[...] = _b(d_dg)
        dma_ref[...] = _b(d_ma)
        dmb_ref[...] = _b(d_mb)
        dwa_ref[...] += dwa
        dwb_ref[...] += dwb

    row = pl.BlockSpec((tm, D), lambda i: (i, 0))
    f32 = jax.ShapeDtypeStruct((T, D), F32)
    b16 = jax.ShapeDtypeStruct((T, D), BF16)
    return pl.pallas_call(
        body, name=name, grid=(T // tm,),
        in_specs=[row] + _merge_specs(tm),
        out_specs=[row] * 6 + [pl.BlockSpec((1, GLA_DV), lambda i: (0, 0)), pl.BlockSpec((1, DN_DV), lambda i: (0, 0)), row],
        out_shape=[f32, f32, b16, b16, b16, b16, jax.ShapeDtypeStruct((1, GLA_DV), F32),
                   jax.ShapeDtypeStruct((1, DN_DV), F32), b16],
        compiler_params=_cp("arbitrary"),
    )(dh, oa, ob, pbig, pbig, pbig, pbig, gla_hn, dn_hn, wout)


def _loss_head(h, nw, target, name):
    T = h.shape[0]
    tm = min(512, T)

    def body(h_ref, nw_ref, t_ref, dx_ref, loss_ref, dnw_ref):
        @pl.when(pl.program_id(0) == 0)
        def _():
            loss_ref[...] = jnp.zeros_like(loss_ref)
            dnw_ref[...] = jnp.zeros_like(dnw_ref)

        x = h_ref[...]
        w = nw_ref[...]
        r = lax.rsqrt(jnp.mean(x * x, axis=-1, keepdims=True) + EPS)
        xhat = x * r
        err = xhat * w - t_ref[...]
        part = jnp.sum(jnp.sum(err * err, axis=-1, keepdims=True), axis=0, keepdims=True)
        loss_ref[...] += (0.5 / D) * part
        dout = err * (1.0 / D)
        dnw_ref[...] += jnp.sum(dout * xhat, axis=0, keepdims=True)
        dxhat = dout * w
        dx_ref[...] = r * (dxhat - xhat * jnp.mean(dxhat * xhat, axis=-1, keepdims=True))

    row = pl.BlockSpec((tm, D), lambda i: (i, 0))
    one = pl.BlockSpec((1, D), lambda i: (0, 0))
    return pl.pallas_call(
        body, name=name, grid=(T // tm,),
        in_specs=[row, one, row],
        out_specs=[row, pl.BlockSpec((8, 128), lambda i: (0, 0)), one],
        out_shape=[jax.ShapeDtypeStruct((T, D), F32), jax.ShapeDtypeStruct((8, 128), F32),
                   jax.ShapeDtypeStruct((1, D), F32)],
        compiler_params=_cp("arbitrary"),
    )(h, nw, target)


def _adamw(w, g, m, v, rows, name):
    R, C = w.shape
    rows = min(rows, R)
    c1 = 1.0 - ADAM_B1 ** ADAM_STEP
    c2 = 1.0 - ADAM_B2 ** ADAM_STEP

    def body(w_ref, g_ref, m_ref, v_ref, d_ref, mo_ref, vo_ref):
        g_ = g_ref[...]
        m_ = ADAM_B1 * m_ref[...] + (1.0 - ADAM_B1) * g_
        v_ = ADAM_B2 * v_ref[...] + (1.0 - ADAM_B2) * (g_ * g_)
        mo_ref[...] = m_
        vo_ref[...] = v_
        d_ref[...] = -ADAM_LR * ((m_ / c1) / (jnp.sqrt(v_ / c2) + ADAM_EPS) + ADAM_WD * w_ref[...])

    blk = pl.BlockSpec((rows, C), lambda i: (i, 0))
    shp = jax.ShapeDtypeStruct((R, C), F32)
    return pl.pallas_call(
        body, name=name, grid=(R // rows,),
        in_specs=[blk] * 4, out_specs=[blk] * 3, out_shape=[shp] * 3,
        compiler_params=_cp("parallel"),
    )(w, g, m, v)


def _me():
    return lax.axis_index("x"), lax.axis_index("y"), lax.axis_index("c")


def _other_chips(x, y):
    return [(1 - x, y), (x, 1 - y), (1 - x, 1 - y)]


def _half_rows(ref, hf):
    half = ref.shape[-2] // 2
    rows = pl.ds(pl.multiple_of(hf * half, 16), half)
    return ref.at[rows, :] if len(ref.shape) == 2 else ref.at[:, rows, :]


def _gather_weights(big, small, name):
    nbig, nsm = len(big), len(small)
    n = nbig + nsm
    own_sem = 6 * nbig + 3 * nsm

    def body(*refs):
        ins, outs = refs[:n], refs[n:2 * n]
        send_sems, recv_sems = refs[2 * n:]
        x, y, c = _me()
        sibling = (x, y, 1 - c)
        chips = _other_chips(x, y)
        slot = lambda chip: 2 * chip[0] + chip[1]

        def copy(k, src, dst, to):
            return pltpu.make_async_remote_copy(src_ref=src, dst_ref=dst, send_sem=send_sems.at[k],
                                                recv_sem=recv_sems.at[k], device_id=to, device_id_type=MESH)

        sent = []
        for i in range(nbig):
            sent.append(copy(own_sem + i, ins[i], outs[i].at[slot((x, y))], sibling))
            sent[-1].start()
            for j, chip in enumerate(chips):
                sent.append(copy(6 * i + j, _half_rows(ins[i], c), _half_rows(outs[i].at[slot((x, y))], c), (*chip, c)))
                sent[-1].start()
        for t in range(nsm):
            w_ref, o_ref = ins[nbig + t], outs[nbig + t]
            o_ref[slot((x, y))] = w_ref[...]
            for j, chip in enumerate(chips):
                sent.append(copy(6 * nbig + 3 * t + j, w_ref, o_ref.at[slot((x, y))], (*chip, c)))
                sent[-1].start()
        for i in range(nbig):
            for j, chip in enumerate(chips):
                landed = _half_rows(outs[i].at[slot(chip)], c)
                copy(6 * i + j, landed, landed, (x, y, c)).wait_recv()
                sent.append(copy(6 * i + 3 + j, landed, landed, sibling))
                sent[-1].start()
        for t in range(nsm):
            for j, chip in enumerate(chips):
                landed = outs[nbig + t].at[slot(chip)]
                copy(6 * nbig + 3 * t + j, landed, landed, (x, y, c)).wait_recv()
        for i in range(nbig):
            for j, chip in enumerate(chips):
                passed = _half_rows(outs[i].at[slot(chip)], 1 - c)
                copy(6 * i + 3 + j, passed, passed, (x, y, c)).wait_recv()
        for i in range(nbig):
            mine = outs[i].at[slot((x, y))]
            copy(own_sem + i, mine, mine, (x, y, c)).wait_recv()
        for cp in sent:
            cp.wait_send()

    vm = pl.BlockSpec(memory_space=pltpu.VMEM)
    nsem = own_sem + nbig
    return pl.pallas_call(
        body, name=name, in_specs=[ANY] * nbig + [vm] * nsm, out_specs=[ANY] * nbig + [vm] * nsm,
        out_shape=[jax.ShapeDtypeStruct((N_SHARD,) + w.shape, w.dtype) for w in list(big) + list(small)],
        scratch_shapes=[pltpu.SemaphoreType.DMA((nsem,)), pltpu.SemaphoreType.DMA((nsem,))],
        compiler_params=pltpu.CompilerParams(has_side_effects=True),
    )(*big, *small)


def _rs_sibling(gs, name):
    n = len(gs)

    def body(*refs):
        send_sems, recv_sems = refs[2 * n:]
        x, y, c = _me()
        cps = [pltpu.make_async_remote_copy(src_ref=_half_rows(refs[i], 1 - c), dst_ref=refs[n + i],
                                            send_sem=send_sems.at[i], recv_sem=recv_sems.at[i],
                                            device_id=(x, y, 1 - c), device_id_type=MESH) for i in range(n)]
        for cp in cps:
            cp.start()
        for cp in cps:
            cp.wait()

    return pl.pallas_call(
        body, name=name, in_specs=[ANY] * n, out_specs=[ANY] * n,
        out_shape=[jax.ShapeDtypeStruct((g.shape[0], g.shape[1] // 2, g.shape[2]), g.dtype) for g in gs],
        scratch_shapes=[pltpu.SemaphoreType.DMA((n,)), pltpu.SemaphoreType.DMA((n,))],
        compiler_params=pltpu.CompilerParams(has_side_effects=True),
    )(*gs)


def _add_pair(g, other, where, name):
    ns, a, b = g.shape
    half = a // 2

    def body(w_ref, g_ref, o_ref, pb_ref, own_ref):
        t = g_ref[0].astype(F32) + o_ref[0].astype(F32)
        pb_ref[0] = _b(t)

        @pl.when(pl.program_id(0) == w_ref[1])
        def _():
            own_ref[...] = t

    return pl.pallas_call(
        body, name=name,
        grid_spec=pltpu.PrefetchScalarGridSpec(
            num_scalar_prefetch=1, grid=(ns,),
            in_specs=[pl.BlockSpec((1, half, b), lambda s, w: (s, w[0], 0)), pl.BlockSpec((1, half, b), lambda s, w: (s, 0, 0))],
            out_specs=[pl.BlockSpec((1, half, b), lambda s, w: (s, 0, 0)), pl.BlockSpec((half, b), lambda s, w: (0, 0))]),
        out_shape=[jax.ShapeDtypeStruct((ns, half, b), BF16), jax.ShapeDtypeStruct((half, b), F32)],
        compiler_params=_cp("arbitrary"),
    )(where, g, other)


def _rs_chips(pbs, name):
    n = len(pbs)

    def body(*refs):
        send_sems, recv_sems = refs[2 * n:]
        x, y, c = _me()
        cps = [pltpu.make_async_remote_copy(src_ref=refs[i].at[2 * chip[0] + chip[1]], dst_ref=refs[n + i].at[j],
                                            send_sem=send_sems.at[3 * i + j], recv_sem=recv_sems.at[3 * i + j],
                                            device_id=(*chip, c), device_id_type=MESH)
               for i in range(n) for j, chip in enumerate(_other_chips(x, y))]
        for cp in cps:
            cp.start()
        for cp in cps:
            cp.wait()

    return pl.pallas_call(
        body, name=name, in_specs=[ANY] * n, out_specs=[ANY] * n,
        out_shape=[jax.ShapeDtypeStruct((3,) + p.shape[1:], p.dtype) for p in pbs],
        scratch_shapes=[pltpu.SemaphoreType.DMA((3 * n,)), pltpu.SemaphoreType.DMA((3 * n,))],
        compiler_params=pltpu.CompilerParams(has_side_effects=True),
    )(*pbs)


def _add_four(own, got, name):
    rows, cols = own.shape
    rb = rows // 2

    def body(a_ref, b_ref, o_ref):
        o_ref[...] = ((a_ref[...] + b_ref[0].astype(F32)) + b_ref[1].astype(F32)) + b_ref[2].astype(F32)

    return pl.pallas_call(
        body, name=name, grid=(rows // rb,),
        in_specs=[pl.BlockSpec((rb, cols), lambda i: (i, 0)), pl.BlockSpec((3, rb, cols), lambda i: (0, i, 0))],
        out_specs=pl.BlockSpec((rb, cols), lambda i: (i, 0)),
        out_shape=jax.ShapeDtypeStruct((rows, cols), F32),
        compiler_params=_cp("parallel"),
    )(own, got)


def _rs_swap(halves, name):
    n = len(halves)

    def body(*refs):
        send_sems, recv_sems = refs[2 * n:]
        x, y, c = _me()
        cps = [pltpu.make_async_remote_copy(src_ref=refs[i], dst_ref=refs[n + i], send_sem=send_sems.at[i],
                                            recv_sem=recv_sems.at[i], device_id=(x, y, 1 - c), device_id_type=MESH)
               for i in range(n)]
        for cp in cps:
            cp.start()
        for cp in cps:
            cp.wait()

    return pl.pallas_call(
        body, name=name, in_specs=[ANY] * n, out_specs=[ANY] * n,
        out_shape=[jax.ShapeDtypeStruct(h.shape, h.dtype) for h in halves],
        scratch_shapes=[pltpu.SemaphoreType.DMA((n,)), pltpu.SemaphoreType.DMA((n,))],
        compiler_params=pltpu.CompilerParams(has_side_effects=True),
    )(*halves)


def _adamw_halves(w, own, got, m, v, rows, name):
    a, b = w.shape
    nblk = a // 2 // rows
    c1 = 1.0 - ADAM_B1 ** ADAM_STEP
    c2 = 1.0 - ADAM_B2 ** ADAM_STEP

    def body(w_ref, own_ref, got_ref, m_ref, v_ref, g_ref, d_ref, mo_ref, vo_ref):
        g_ = jnp.where(pl.program_id(0) == lax.axis_index("c"), own_ref[...], got_ref[...])
        g_ref[...] = g_
        m_ = ADAM_B1 * m_ref[...] + (1.0 - ADAM_B1) * g_
        v_ = ADAM_B2 * v_ref[...] + (1.0 - ADAM_B2) * (g_ * g_)
        mo_ref[...] = m_
        vo_ref[...] = v_
        d_ref[...] = -ADAM_LR * ((m_ / c1) / (jnp.sqrt(v_ / c2) + ADAM_EPS) + ADAM_WD * w_ref[...])

    whole = pl.BlockSpec((rows, b), lambda h, i: (h * nblk + i, 0))
    part = pl.BlockSpec((rows, b), lambda h, i: (i, 0))
    shp = jax.ShapeDtypeStruct((a, b), F32)
    return pl.pallas_call(
        body, name=name, grid=(2, nblk),
        in_specs=[whole, part, part, whole, whole], out_specs=[whole] * 4, out_shape=[shp] * 4,
        compiler_params=_cp("parallel", "parallel"),
    )(w, own, got, m, v)


def _allsum_small(vec, name):
    def body(v_ref, o_ref, buf_ref, send_sems, recv_sems):
        x, y, c = _me()
        me = 4 * x + 2 * y + c
        buf_ref[me] = v_ref[...]
        cps = []
        for k in range(1, 8):
            peer = (x ^ (k >> 2), y ^ ((k >> 1) & 1), c ^ (k & 1))
            cps.append(pltpu.make_async_remote_copy(src_ref=v_ref, dst_ref=buf_ref.at[me],
                                                    send_sem=send_sems.at[k - 1], recv_sem=recv_sems.at[k - 1],
                                                    device_id=peer, device_id_type=MESH))
        for cp in cps:
            cp.start()
        for k in range(1, 8):
            peer_idx = me ^ k
            pltpu.make_async_remote_copy(src_ref=v_ref, dst_ref=buf_ref.at[peer_idx],
                                         send_sem=send_sems.at[k - 1], recv_sem=recv_sems.at[k - 1],
                                         device_id=(x, y, c), device_id_type=MESH).wait_recv()
        for cp in cps:
            cp.wait_send()
        acc = buf_ref[0]
        for d in range(1, 8):
            acc = acc + buf_ref[d]
        o_ref[...] = acc

    return pl.pallas_call(
        body, name=name,
        in_specs=[pl.BlockSpec(memory_space=pltpu.VMEM)], out_specs=pl.BlockSpec(memory_space=pltpu.VMEM),
        out_shape=jax.ShapeDtypeStruct(vec.shape, F32),
        scratch_shapes=[pltpu.VMEM((8,) + vec.shape, F32), pltpu.SemaphoreType.DMA((7,)), pltpu.SemaphoreType.DMA((7,))],
        compiler_params=pltpu.CompilerParams(has_side_effects=True),
    )(vec)


BIG = ("ffn1_w_gate", "ffn1_w_up", "ffn1_w_down", "w_in", "w_out", "ffn2_w_gate", "ffn2_w_up", "ffn2_w_down")
TINY = ("w_gla_gate", "conv_w")
SHARDED = BIG + TINY


def _join_cols(w4):
    return jnp.transpose(w4, (1, 0, 2)).reshape(w4.shape[1], N_SHARD * w4.shape[2])


def _cut_cols(w):
    return jnp.transpose(w.reshape(w.shape[0], N_SHARD, w.shape[1] // N_SHARD), (1, 0, 2))


def _split_w_in(w):
    o = IN_OFF
    big = jnp.concatenate([w[:, :o[4]], w[:, o[5]:o[9]], w[:, o[11]:]], axis=1)
    small = jnp.concatenate([w[:, o[4]:o[5]], w[:, o[9]:o[11]], jnp.zeros((w.shape[0], SMALL - 32), w.dtype)], axis=1)
    return big, small


def _merge_w_in(big, small):
    return jnp.concatenate([big[:, :3072], small[:, :16], big[:, 3072:7168], small[:, 16:32], big[:, 7168:]], axis=1)


def _local_step(x, target, W, P):
    wbig, wsmall = W["w_in_big"], W["w_in_small"]
    wgate_pad = jnp.zeros((SMALL, GLA_H * GLA_DK), F32).at[:GLA_RANK].set(P["w_gla_gate"])
    cw8 = jnp.zeros((8, CONV_C), F32).at[:CONV_K].set(P["conv_w"])
    par = jnp.zeros((DN_H, 8, 128), F32)
    par = par.at[:, 0, :].set(jnp.broadcast_to(P["dn_a_log"].reshape(DN_H, 1), (DN_H, 128)))
    par = par.at[:, 1, :].set(jnp.broadcast_to(P["dn_dt_bias"].reshape(DN_H, 1), (DN_H, 128)))

    h1, n1, g1, u1 = _ffn_fwd(x, P["ffn1_norm"], W["ffn1_w_gate"], W["ffn1_w_up"], W["ffn1_w_down"], "ffn1_fwd")
    pbig, psmall, n2 = _norm_proj(h1, P["mix_norm"], wbig, wsmall, "mix_proj")
    oa, sa = _gla_fwd(pbig, psmall, wgate_pad, P["b_gla_gate"], "gla_fwd")
    conv = _conv_fwd(pbig, cw8, "conv_fwd")
    ob, sb = _gdn_fwd(conv, psmall, par, "gdn_fwd")
    h2, yb = _merge_fwd(h1, oa, ob, pbig, P["gla_head_norm"], P["dn_head_norm"], W["w_out"], "merge_fwd")
    h3, n3, g3, u3 = _ffn_fwd(h2, P["ffn2_norm"], W["ffn2_w_gate"], W["ffn2_w_up"], W["ffn2_w_down"], "ffn2_fwd")
    dh3, loss, d_final = _loss_head(h3, P["final_norm"], target, "loss_head")

    gw, gs = {}, {"final_norm": d_final}

    def ffn_grads(tag, dh, h, n, g, u):
        dx, dg, du, act, dfb, dnw = _ffn_bwd(dh, h, P[tag + "_norm"], g, u, W[tag + "_w_gate"], W[tag + "_w_up"],
                                             W[tag + "_w_down"], tag + "_bwd")
        gw[tag + "_w_gate"] = _mm_tn(n, dg, D, FF_CUT, tag + "_dwg")
        gw[tag + "_w_up"] = _mm_tn(n, du, D, FF_CUT, tag + "_dwu")
        gw[tag + "_w_down"] = _mm_tn(act, dfb, FF_CUT, D, tag + "_dwd")
        gs[tag + "_norm"] = dnw
        return dx

    dh2 = ffn_grads("ffn2", dh3, h2, n3, g3, u3)
    d_oa, d_ob, d_gr, d_dgate, d_ma, d_mb, gs["gla_head_norm"], gs["dn_head_norm"], dh2b = _merge_bwd(
        dh2, oa, ob, pbig, P["gla_head_norm"], P["dn_head_norm"], W["w_out"], "merge_bwd")
    gw["w_out"] = _mm_tn(yb, dh2b, D, D, "dw_out").reshape(N_SHARD, D // N_SHARD, D)
    d_gq, d_gk, d_gv, dpre = _gla_bwd(pbig, psmall, wgate_pad, P["b_gla_gate"], sa, d_oa, "gla_bwd")
    dcq, dck, dcv, dsm, dpar = _gdn_bwd(conv, psmall, par, sb, d_ob, "gdn_bwd")
    dsmall, dwgate, gs["b_gla_gate"] = _gla_gate_bwd(dpre, psmall, wgate_pad, dsm, "gla_gate_bwd")
    gs["w_gla_gate"] = dwgate[:GLA_RANK]
    d_x3, dcw = _conv_bwd(dcq, dck, dcv, pbig, cw8, "conv_bwd")
    gs["conv_w"] = dcw[:CONV_K]
    gs["dn_a_log"] = dpar[:, 0, 0].reshape(1, DN_H)
    gs["dn_dt_bias"] = dpar[:, 0, 1].reshape(1, DN_H)
    pieces = (d_gq, d_gk, d_gv, d_gr, d_x3, d_dgate, d_ma, d_mb)
    dh1, gs["mix_norm"] = _proj_bwd(dh2, h1, P["mix_norm"], pieces, dsmall, wbig, wsmall, "proj_bwd")
    dbig = jnp.concatenate([_mm_tn(n2, p, D, 1024, "dw_in_%d" % i) for i, p in enumerate(pieces)], axis=1)
    dsml = _mm_tn(n2, dsmall, D, SMALL, "dw_in_small")
    gw["w_in"] = _cut_cols(_merge_w_in(dbig, dsml))
    grad_x = ffn_grads("ffn1", dh1, x, n1, g1, u1)
    return loss, grad_x, gw, gs


SMALL_NAMES = ("ffn1_norm", "mix_norm", "ffn2_norm", "final_norm", "b_gla_gate", "gla_head_norm", "dn_head_norm",
               "dn_a_log", "dn_dt_bias")
ROW4 = (("b_gla_gate", 512), ("gla_head_norm", 256), ("dn_head_norm", 128), ("dn_a_log", 8), ("dn_dt_bias", 8))


def _pack_small(d, loss=None):
    row4 = [d[n].reshape(-1) for n, _ in ROW4]
    row4.append(jnp.zeros((1,), F32) if loss is None else loss.reshape(1))
    row4 = jnp.concatenate(row4)
    row4 = jnp.pad(row4, (0, D - row4.shape[0]))
    rows = [d[n].reshape(-1) for n in SMALL_NAMES[:4]] + [row4]
    return jnp.concatenate([jnp.stack(rows), jnp.zeros((3, D), F32)], axis=0)


def _unpack_small(a, like):
    out = {n: a[i].reshape(like[n].shape) for i, n in enumerate(SMALL_NAMES[:4])}
    off = 0
    for n, w in ROW4:
        out[n] = a[4, off:off + w].reshape(like[n].shape)
        off += w
    return out, a[4, off]


WEIGHT_ORDER = ("ffn1_norm", "ffn1_w_gate", "ffn1_w_up", "ffn1_w_down", "mix_norm", "w_in", "w_gla_gate", "b_gla_gate",
                "conv_w", "dn_a_log", "dn_dt_bias", "gla_head_norm", "dn_head_norm", "w_out", "ffn2_norm",
                "ffn2_w_gate", "ffn2_w_up", "ffn2_w_down", "final_norm")
ADAM_ROWS = {"ffn1_w_gate": 256, "ffn1_w_up": 256, "ffn1_w_down": 176, "w_in": 128, "w_gla_gate": 16, "conv_w": 4,
             "w_out": 64, "ffn2_w_gate": 256, "ffn2_w_up": 256, "ffn2_w_down": 176}


def kernel(x, ffn1_norm, ffn1_w_gate, ffn1_w_up, ffn1_w_down, mix_norm, w_in, w_gla_gate, b_gla_gate, conv_w, dn_a_log, dn_dt_bias, gla_head_norm, dn_head_norm, w_out, ffn2_norm, ffn2_w_gate, ffn2_w_up, ffn2_w_down, final_norm, loss_target, m_ffn1_norm, m_ffn1_w_gate, m_ffn1_w_up, m_ffn1_w_down, m_mix_norm, m_w_in, m_w_gla_gate, m_b_gla_gate, m_conv_w, m_dn_a_log, m_dn_dt_bias, m_gla_head_norm, m_dn_head_norm, m_w_out, m_ffn2_norm, m_ffn2_w_gate, m_ffn2_w_up, m_ffn2_w_down, m_final_norm, v_ffn1_norm, v_ffn1_w_gate, v_ffn1_w_up, v_ffn1_w_down, v_mix_norm, v_w_in, v_w_gla_gate, v_b_gla_gate, v_conv_w, v_dn_a_log, v_dn_dt_bias, v_gla_head_norm, v_dn_head_norm, v_w_out, v_ffn2_norm, v_ffn2_w_gate, v_ffn2_w_up, v_ffn2_w_down, v_final_norm):
    given = dict(locals())
    wts = {n: given[n] for n in WEIGHT_ORDER}
    moms = {n: given["m_" + n] for n in WEIGHT_ORDER}
    vars_ = {n: given["v_" + n] for n in WEIGHT_ORDER}
    two_d = lambda a: a.reshape(a.shape[-2], a.shape[-1]) if a.ndim == 3 else a.reshape(1, -1)
    shard = {n: two_d(wts[n]) for n in SHARDED}

    gathered = _gather_weights([shard[n].astype(BF16) for n in BIG], [shard[n] for n in TINY], "gather_weights")
    G = dict(zip(SHARDED, gathered))
    W = {n: G[n] for n in BIG if n.startswith("ffn")}
    W["w_out"] = G["w_out"].reshape(D, D)
    W["w_in_big"], W["w_in_small"] = _split_w_in(_join_cols(G["w_in"]))
    P = {n: two_d(wts[n]) for n in SMALL_NAMES}
    for n in TINY:
        P[n] = _join_cols(G[n])

    loss, grad_x, gw, gs = _local_step(x[0], loss_target[0], W, P)

    my_slot = 2 * lax.axis_index("x") + lax.axis_index("y")
    where = jnp.stack([lax.axis_index("c"), my_slot]).astype(jnp.int32)
    mine = [gw[n] for n in BIG]
    from_sibling = _rs_sibling(mine, "rs_sibling")
    pairs = [_add_pair(g, o, where, "rs_pair_" + n) for n, g, o in zip(BIG, mine, from_sibling)]
    from_chips = _rs_chips([p[0] for p in pairs], "rs_chips")
    halves = [_add_four(p[1], got, "rs_four_" + n) for n, p, got in zip(BIG, pairs, from_chips)]
    other_halves = _rs_swap(halves, "rs_swap")

    tiny_rows = jnp.concatenate([gs["w_gla_gate"].reshape(8, D), gs["conv_w"].reshape(12, D), jnp.zeros((4, D), F32)])
    all_sum = _allsum_small(jnp.concatenate([_pack_small(gs, loss[0, 0]), tiny_rows]), "allsum_small")
    small_sum = all_sum[:8]
    small_g, loss_total = _unpack_small(small_sum, P)

    grads, delta, new_m, new_v = {}, {}, {}, {}
    for n, own, got in zip(BIG, halves, other_halves):
        res = _adamw_halves(shard[n], own, got, two_d(moms[n]), two_d(vars_[n]), ADAM_ROWS[n], "adamw_" + n)
        grads[n], delta[n], new_m[n], new_v[n] = (t.reshape(wts[n].shape) for t in res)
    for n, rows in (("w_gla_gate", all_sum[8:16]), ("conv_w", all_sum[16:28])):
        cols = shard[n].shape[1]
        grads[n] = lax.dynamic_slice_in_dim(rows.reshape(shard[n].shape[0], N_SHARD * cols), my_slot * cols, cols, axis=1)
        d, m_, v_ = _adamw(shard[n], grads[n], two_d(moms[n]), two_d(vars_[n]), ADAM_ROWS[n], "adamw_" + n)
        delta[n], new_m[n], new_v[n] = (t.reshape(wts[n].shape) for t in (d, m_, v_))
    pk = lambda src: _pack_small({n: two_d(src[n]) for n in SMALL_NAMES})
    sd, sm_, sv_ = _adamw(pk(wts), small_sum, pk(moms), pk(vars_), 8, "adamw_small")
    for res, dst in ((sd, delta), (sm_, new_m), (sv_, new_v)):
        u, _ = _unpack_small(res, wts)
        dst.update(u)
    grad_w = {n: grads[n].reshape(wts[n].shape) for n in SHARDED}
    grad_w.update({n: small_g[n].reshape(wts[n].shape) for n in SMALL_NAMES})
    return (loss_total, grad_x[None], *[grad_w[n] for n in WEIGHT_ORDER], *[delta[n] for n in WEIGHT_ORDER],
            *[new_m[n] for n in WEIGHT_ORDER], *[new_v[n] for n in WEIGHT_ORDER])
```

```python
import functools
import math

import numpy as np
import jax
import jax.numpy as jnp
from jax import lax
from jax.experimental import pallas as pl
from jax.experimental.pallas import tpu as pltpu

F32 = jnp.float32
BF16 = jnp.bfloat16
HI = lax.Precision.HIGH
MESH = pl.DeviceIdType.MESH
ANY = pl.BlockSpec(memory_space=pl.ANY)

EPS = 1e-6
D = 1024
DFF = 2816
FFN_RES = 0.5
GLA_H, GLA_DK, GLA_DV, GLA_RANK, GLA_TAU = 4, 128, 256, 16, 16.0
DN_H, DN_DK, DN_DV = 8, 128, 128
CONV_K = 4
CHUNK = 64
N_SHARD = 4
FF_CUT = DFF // N_SHARD
ADAM_LR, ADAM_B1, ADAM_B2, ADAM_EPS, ADAM_WD, ADAM_STEP = 0.001, 0.9, 0.999, 1e-08, 0.01, 10

IN_SIZES = (512, 512, 1024, 1024, 16, 1024, 1024, 1024, 1024, 8, 8, 1024, 1024)
IN_OFF = tuple(int(v) for v in np.cumsum((0,) + IN_SIZES))
D_IN = IN_OFF[-1]
BIG_COLS = 9216
SMALL = 128
PIECES = (512, 512, 1024, 1024, 3072, 1024, 1024, 1024)

VMEM_LIMIT = 56 * 1024 * 1024
ROW_BLK = 256
ATT_BLK = 256
GDN_BLK = 128
GDN_HEADS = 8


def _cp(*sem):
    return pltpu.CompilerParams(dimension_semantics=sem, vmem_limit_bytes=VMEM_LIMIT)


def _sigmoid(x):
    return 1.0 / (1.0 + jnp.exp(-x))


def _softplus(x):
    return jnp.maximum(x, 0.0) + jnp.log(1.0 + jnp.exp(-jnp.abs(x)))


def _log_sigmoid(x):
    return jnp.minimum(x, 0.0) - jnp.log(1.0 + jnp.exp(-jnp.abs(x)))


def _dot(a, b, prec=None):
    return jnp.dot(a, b, preferred_element_type=F32, precision=prec)


def _dot_nt(a, b, prec=None):
    return lax.dot_general(a, b, (((1,), (1,)), ((), ())), preferred_element_type=F32, precision=prec)


def _dot_tn(a, b, prec=None):
    return lax.dot_general(a, b, (((0,), (0,)), ((), ())), preferred_element_type=F32, precision=prec)


def _b(x):
    return x.astype(BF16)


def _iota2(n, m, axis):
    return lax.broadcasted_iota(jnp.int32, (n, m), axis)


def _load_weights(pairs, sem):
    copies = [pltpu.make_async_copy(s, d, sem.at[i]) for i, (s, d) in enumerate(pairs)]
    for c in copies:
        c.start()
    for c in copies:
        c.wait()


def _ffn_fwd(h, nw, wg, wu, wd, name):
    T = h.shape[0]
    tm = min(ROW_BLK, T)

    def body(h_ref, nw_ref, wg_hbm, wu_hbm, wd_hbm, ho_ref, n_ref, g_ref, u_ref, wg_v, wu_v, wd_v, sem):
        @pl.when(pl.program_id(0) == 0)
        def _():
            _load_weights(((wg_hbm, wg_v), (wu_hbm, wu_v), (wd_hbm, wd_v)), sem)

        x = h_ref[...]
        r = lax.rsqrt(jnp.mean(x * x, axis=-1, keepdims=True) + EPS)
        nb = _b((x * r) * nw_ref[...])
        n_ref[...] = nb
        acc = jnp.zeros((tm, D), F32)
        for s in range(N_SHARD):
            g = _dot(nb, wg_v[s])
            u = _dot(nb, wu_v[s])
            g_ref[s] = _b(g)
            u_ref[s] = _b(u)
            acc += _dot(_b(g * _sigmoid(g) * u), wd_v[s])
        ho_ref[...] = x + FFN_RES * acc

    row = lambda w: pl.BlockSpec((tm, w), lambda i: (i, 0))
    cut = pl.BlockSpec((N_SHARD, tm, FF_CUT), lambda i: (0, i, 0))
    return pl.pallas_call(
        body, name=name, grid=(T // tm,),
        in_specs=[row(D), pl.BlockSpec((1, D), lambda i: (0, 0)), ANY, ANY, ANY],
        out_specs=[row(D), row(D), cut, cut],
        out_shape=[jax.ShapeDtypeStruct((T, D), F32), jax.ShapeDtypeStruct((T, D), BF16),
                   jax.ShapeDtypeStruct((N_SHARD, T, FF_CUT), BF16), jax.ShapeDtypeStruct((N_SHARD, T, FF_CUT), BF16)],
        scratch_shapes=[pltpu.VMEM((N_SHARD, D, FF_CUT), BF16), pltpu.VMEM((N_SHARD, D, FF_CUT), BF16),
                        pltpu.VMEM((N_SHARD, FF_CUT, D), BF16), pltpu.SemaphoreType.DMA((3,))],
        compiler_params=_cp("arbitrary"),
    )(h, nw, wg, wu, wd)


def _ffn_bwd(dh, h, nw, g, u, wg, wu, wd, name):
    T = h.shape[0]
    tm = min(ROW_BLK, T)

    def body(dh_ref, h_ref, nw_ref, g_ref, u_ref, wg_hbm, wu_hbm, wd_hbm,
             dx_ref, dg_ref, du_ref, a_ref, df_ref, dnw_ref, wg_v, wu_v, wd_v, sem):
        @pl.when(pl.program_id(0) == 0)
        def _():
            _load_weights(((wg_hbm, wg_v), (wu_hbm, wu_v), (wd_hbm, wd_v)), sem)
            dnw_ref[...] = jnp.zeros_like(dnw_ref)

        dh_ = dh_ref[...]
        dfb = _b(FFN_RES * dh_)
        df_ref[...] = dfb
        dn = jnp.zeros((tm, D), F32)
        for s in range(N_SHARD):
            da = _dot_nt(dfb, wd_v[s])
            gg = g_ref[s].astype(F32)
            uu = u_ref[s].astype(F32)
            sg = _sigmoid(gg)
            silu = gg * sg
            a_ref[s] = _b(silu * uu)
            dgb = _b(da * uu * (sg * (1.0 + gg * (1.0 - sg))))
            dub = _b(da * silu)
            dg_ref[s] = dgb
            du_ref[s] = dub
            dn += _dot_nt(dgb, wg_v[s]) + _dot_nt(dub, wu_v[s])
        x = h_ref[...]
        r = lax.rsqrt(jnp.mean(x * x, axis=-1, keepdims=True) + EPS)
        xhat = x * r
        dnw_ref[...] += jnp.sum(dn * xhat, axis=0, keepdims=True)
        dxhat = dn * nw_ref[...]
        dx_ref[...] = dh_ + r * (dxhat - xhat * jnp.mean(dxhat * xhat, axis=-1, keepdims=True))

    row = lambda w: pl.BlockSpec((tm, w), lambda i: (i, 0))
    one = pl.BlockSpec((1, D), lambda i: (0, 0))
    cut = pl.BlockSpec((N_SHARD, tm, FF_CUT), lambda i: (0, i, 0))
    cut_shape = jax.ShapeDtypeStruct((N_SHARD, T, FF_CUT), BF16)
    return pl.pallas_call(
        body, name=name, grid=(T // tm,),
        in_specs=[row(D), row(D), one, cut, cut, ANY, ANY, ANY],
        out_specs=[row(D), cut, cut, cut, row(D), one],
        out_shape=[jax.ShapeDtypeStruct((T, D), F32), cut_shape, cut_shape, cut_shape,
                   jax.ShapeDtypeStruct((T, D), BF16), jax.ShapeDtypeStruct((1, D), F32)],
        scratch_shapes=[pltpu.VMEM((N_SHARD, D, FF_CUT), BF16), pltpu.VMEM((N_SHARD, D, FF_CUT), BF16),
                        pltpu.VMEM((N_SHARD, FF_CUT, D), BF16), pltpu.SemaphoreType.DMA((3,))],
        compiler_params=_cp("arbitrary"),
    )(dh, h, nw, g, u, wg, wu, wd)


def _mm_tn(a, b, bm, bn, name, out_dtype=BF16, tk=2048):
    cuts = a.shape[0] if a.ndim == 3 else (b.shape[0] if b.ndim == 3 else None)
    T, M = a.shape[-2:]
    N = b.shape[-1]
    tk = min(tk, T)
    bm, bn = min(bm, M), min(bn, N)
    nk = T // tk

    def body(a_ref, b_ref, o_ref, acc_ref):
        k = pl.program_id(3)

        @pl.when(k == 0)
        def _():
            acc_ref[...] = jnp.zeros_like(acc_ref)

        av = a_ref[0] if a.ndim == 3 else a_ref[...]
        bv = b_ref[0] if b.ndim == 3 else b_ref[...]
        acc_ref[...] += _dot_tn(_b(av), _b(bv))

        @pl.when(k == nk - 1)
        def _():
            res = acc_ref[...].astype(out_dtype)
            if cuts is None:
                o_ref[...] = res
            else:
                o_ref[0] = res

    a_spec = (pl.BlockSpec((1, tk, bm), lambda s, i, j, k: (s, k, i)) if a.ndim == 3
              else pl.BlockSpec((tk, bm), lambda s, i, j, k: (k, i)))
    b_spec = (pl.BlockSpec((1, tk, bn), lambda s, i, j, k: (s, k, j)) if b.ndim == 3
              else pl.BlockSpec((tk, bn), lambda s, i, j, k: (k, j)))
    if cuts is None:
        o_spec, o_shape = pl.BlockSpec((bm, bn), lambda s, i, j, k: (i, j)), (M, N)
    else:
        o_spec, o_shape = pl.BlockSpec((1, bm, bn), lambda s, i, j, k: (s, i, j)), (cuts, M, N)
    return pl.pallas_call(
        body, name=name, grid=(cuts or 1, M // bm, N // bn, nk),
        in_specs=[a_spec, b_spec], out_specs=o_spec,
        out_shape=jax.ShapeDtypeStruct(o_shape, out_dtype),
        scratch_shapes=[pltpu.VMEM((bm, bn), F32)],
        compiler_params=_cp("parallel", "parallel", "parallel", "arbitrary"),
    )(a, b)


def _norm_proj(h, nw, wbig, wsmall, name):
    T = h.shape[0]
    tm = min(512, T)
    tn = 1536

    def body(h_ref, nw_ref, wb_hbm, ws_ref, pb_ref, ps_ref, n_ref, wb_v, sem):
        @pl.when(pl.program_id(0) == 0)
        def _():
            _load_weights(((wb_hbm, wb_v),), sem)

        x = h_ref[...]
        r = lax.rsqrt(jnp.mean(x * x, axis=-1, keepdims=True) + EPS)
        nb = _b((x * r) * nw_ref[...])
        n_ref[...] = nb
        ps_ref[...] = _dot(nb, ws_ref[...])
        for j in range(BIG_COLS // tn):
            pb_ref[:, j * tn:(j + 1) * tn] = _b(_dot(nb, wb_v[:, j * tn:(j + 1) * tn]))

    row = lambda w: pl.BlockSpec((tm, w), lambda i: (i, 0))
    return pl.pallas_call(
        body, name=name, grid=(T // tm,),
        in_specs=[row(D), pl.BlockSpec((1, D), lambda i: (0, 0)), ANY, pl.BlockSpec((D, SMALL), lambda i: (0, 0))],
        out_specs=[row(BIG_COLS), row(SMALL), row(D)],
        out_shape=[jax.ShapeDtypeStruct((T, BIG_COLS), BF16), jax.ShapeDtypeStruct((T, SMALL), F32),
                   jax.ShapeDtypeStruct((T, D), BF16)],
        scratch_shapes=[pltpu.VMEM((D, BIG_COLS), BF16), pltpu.SemaphoreType.DMA((1,))],
        compiler_params=_cp("arbitrary"),
    )(h, nw, wbig, wsmall)


def _proj_bwd(dh, h, nw, pieces, dsmall, wbig, wsmall, name):
    T = h.shape[0]
    tm = min(ROW_BLK, T)
    offs = tuple(int(v) for v in np.cumsum((0,) + PIECES))

    def body(dh_ref, h_ref, nw_ref, *rest):
        p_refs = rest[:len(PIECES)]
        ds_ref, wb_hbm, ws_ref, dx_ref, dnw_ref, wb_v, sem = rest[len(PIECES):]

        @pl.when(pl.program_id(0) == 0)
        def _():
            _load_weights(((wb_hbm, wb_v),), sem)
            dnw_ref[...] = jnp.zeros_like(dnw_ref)

        dn = _dot_nt(_b(ds_ref[...]), ws_ref[...])
        for p_ref, lo, wdt in zip(p_refs, offs, PIECES):
            dn += _dot_nt(p_ref[...], wb_v[:, lo:lo + wdt])
        x = h_ref[...]
        r = lax.rsqrt(jnp.mean(x * x, axis=-1, keepdims=True) + EPS)
        xhat = x * r
        dnw_ref[...] += jnp.sum(dn * xhat, axis=0, keepdims=True)
        dxhat = dn * nw_ref[...]
        dx_ref[...] = dh_ref[...] + r * (dxhat - xhat * jnp.mean(dxhat * xhat, axis=-1, keepdims=True))

    row = lambda w: pl.BlockSpec((tm, w), lambda i: (i, 0))
    one = pl.BlockSpec((1, D), lambda i: (0, 0))
    return pl.pallas_call(
        body, name=name, grid=(T // tm,),
        in_specs=[row(D), row(D), one] + [row(w) for w in PIECES] + [row(SMALL), ANY, pl.BlockSpec((D, SMALL), lambda i: (0, 0))],
        out_specs=[row(D), one],
        out_shape=[jax.ShapeDtypeStruct((T, D), F32), jax.ShapeDtypeStruct((1, D), F32)],
        scratch_shapes=[pltpu.VMEM((D, BIG_COLS), BF16), pltpu.SemaphoreType.DMA((1,))],
        compiler_params=_cp("arbitrary"),
    )(dh, h, nw, *pieces, dsmall, wbig, wsmall)


def _gla_block(q_ref, k_ref, sm_ref, wg_ref, bg_ref, nc, tril):
    nbat = GLA_H * nc
    q = _heads_first(q_ref[...].astype(F32), nc, GLA_DK)
    k = _heads_first(k_ref[...].astype(F32), nc, GLA_DK)
    pre = _heads_first(_dot(sm_ref[...], wg_ref[...], HI) + bg_ref[...], nc, GLA_DK)
    la = _log_sigmoid(pre) * (1.0 / GLA_TAU)
    bc = _bmm(jnp.broadcast_to(tril, (nbat, CHUNK, CHUNK)), la, HI)
    bl = bc[:, CHUNK - 1:CHUNK, :]
    eb = jnp.exp(bc)
    enb = jnp.exp(-bc)
    ebl = jnp.exp(bl - bc)
    q_in = q * (GLA_DK ** -0.5) * eb
    k_out = k * enb
    k_st = k * ebl
    a_ch = jnp.exp(bl)
    return pre, eb, enb, ebl, q_in, k_out, k_st, a_ch


def _gla_specs(blk, idx):
    hk, hv = GLA_H * GLA_DK, GLA_H * GLA_DV
    return [pl.BlockSpec((blk, hk), lambda j: (idx(j), 0)),
            pl.BlockSpec((blk, hk), lambda j: (idx(j), 1)),
            pl.BlockSpec((blk, hv), lambda j: (idx(j), 1)),
            pl.BlockSpec((blk, SMALL), lambda j: (idx(j), 0)),
            pl.BlockSpec((SMALL, hk), lambda j: (0, 0)),
            pl.BlockSpec((1, hk), lambda j: (0, 0))]


def _gla_fwd(pbig, psmall, wgate, bgate, name):
    T = pbig.shape[0]
    blk = min(ATT_BLK, T)
    nc = blk // CHUNK

    def body(q_ref, k_ref, v_ref, sm_ref, wg_ref, bg_ref, o_ref, ss_ref, st_ref):
        @pl.when(pl.program_id(0) == 0)
        def _():
            st_ref[...] = jnp.zeros_like(st_ref)

        causal = _iota2(CHUNK, CHUNK, 0) >= _iota2(CHUNK, CHUNK, 1)
        _, _, _, _, q_in, k_out, k_st, a_ch = _gla_block(q_ref, k_ref, sm_ref, wg_ref, bg_ref, nc, causal.astype(F32))
        v = _heads_first(v_ref[...], nc, GLA_DV)
        qb = _b(q_in)
        sc = jnp.where(causal, _bmm_nt(qb, _b(k_out)), 0.0)
        kv = _bmm_tn(v, _b(k_st))
        before = [None] * (GLA_H * nc)
        for i in range(GLA_H):
            st = st_ref[i]
            for c in range(nc):
                n = i * nc + c
                before[n] = st
                st = st * a_ch[n] + kv[n]
            st_ref[i] = st
        states = jnp.stack(before)
        ss_ref[...] = states.reshape(GLA_H, nc, GLA_DV, GLA_DK)
        o_ref[...] = _heads_last(_bmm(_b(sc), v) + _bmm_nt(qb, _b(states)), nc)

    return pl.pallas_call(
        body, name=name, grid=(T // blk,),
        in_specs=_gla_specs(blk, lambda j: j),
        out_specs=[pl.BlockSpec((blk, GLA_H * GLA_DV), lambda j: (j, 0)),
                   pl.BlockSpec((GLA_H, nc, GLA_DV, GLA_DK), lambda j: (0, j, 0, 0))],
        out_shape=[jax.ShapeDtypeStruct((T, GLA_H * GLA_DV), F32),
                   jax.ShapeDtypeStruct((GLA_H, T // CHUNK, GLA_DV, GLA_DK), F32)],
        scratch_shapes=[pltpu.VMEM((GLA_H, GLA_DV, GLA_DK), F32)],
        compiler_params=_cp("arbitrary"),
    )(pbig, pbig, pbig, psmall, wgate, bgate)


def _gla_bwd(pbig, psmall, wgate, bgate, states, do, name):
    T = pbig.shape[0]
    blk = min(ATT_BLK, T)
    nc = blk // CHUNK
    nb = T // blk
    nbat = GLA_H * nc

    def body(q_ref, k_ref, v_ref, sm_ref, wg_ref, bg_ref, ss_ref, do_ref, dq_ref, dk_ref, dv_ref, dpre_ref, dst_ref):
        @pl.when(pl.program_id(0) == 0)
        def _():
            dst_ref[...] = jnp.zeros_like(dst_ref)

        causal = _iota2(CHUNK, CHUNK, 0) >= _iota2(CHUNK, CHUNK, 1)
        triu = (_iota2(CHUNK, CHUNK, 0) <= _iota2(CHUNK, CHUNK, 1)).astype(F32)
        pre, eb, enb, ebl, q_in, k_out, k_st, a_ch = _gla_block(q_ref, k_ref, sm_ref, wg_ref, bg_ref, nc,
                                                                causal.astype(F32))
        v = _heads_first(v_ref[...], nc, GLA_DV)
        dob = _b(_heads_first(do_ref[...], nc, GLA_DV))
        st = ss_ref[...].reshape(nbat, GLA_DV, GLA_DK)
        qb, kob, kstb = _b(q_in), _b(k_out), _b(k_st)
        qdo = _bmm_tn(dob, qb)
        after = [None] * nbat
        for i in range(GLA_H):
            dst = dst_ref[i]
            for c in range(nc - 1, -1, -1):
                n = i * nc + c
                after[n] = dst
                dst = dst * a_ch[n] + qdo[n]
            dst_ref[i] = dst
        dsa = jnp.stack(after)
        dsb = _b(dsa)
        sc = jnp.where(causal, _bmm_nt(qb, kob), 0.0)
        dsc = _b(jnp.where(causal, _bmm_nt(dob, v), 0.0))
        dq_in = _bmm(dob, _b(st)) + _bmm(dsc, kob)
        dk_out = _bmm_tn(dsc, qb)
        dk_st = _bmm(v, dsb)
        dv_ref[...] = _b(_heads_last(_bmm_tn(_b(sc), dob) + _bmm_nt(kstb, dsb), nc))
        da_ch = jnp.sum(st * dsa, axis=1, keepdims=True)
        tk = dk_st * k_st
        db = dq_in * q_in - dk_out * k_out - tk
        db_last = jnp.sum(tk, axis=1, keepdims=True) + da_ch * a_ch
        dq_ref[...] = _b(_heads_last(dq_in * (GLA_DK ** -0.5) * eb, nc))
        dk_ref[...] = _b(_heads_last(dk_out * enb + dk_st * ebl, nc))
        dla = _bmm(jnp.broadcast_to(triu, (nbat, CHUNK, CHUNK)), db, HI) + db_last
        dpre_ref[...] = _heads_last(dla * (1.0 / GLA_TAU) * _sigmoid(-pre), nc)

    r = lambda j: nb - 1 - j
    hk, hv = GLA_H * GLA_DK, GLA_H * GLA_DV
    return pl.pallas_call(
        body, name=name, grid=(nb,),
        in_specs=_gla_specs(blk, r) + [pl.BlockSpec((GLA_H, nc, GLA_DV, GLA_DK), lambda j: (0, r(j), 0, 0)),
                                      pl.BlockSpec((blk, hv), lambda j: (r(j), 0))],
        out_specs=[pl.BlockSpec((blk, hk), lambda j: (r(j), 0)), pl.BlockSpec((blk, hk), lambda j: (r(j), 0)),
                   pl.BlockSpec((blk, hv), lambda j: (r(j), 0)), pl.BlockSpec((blk, hk), lambda j: (r(j), 0))],
        out_shape=[jax.ShapeDtypeStruct((T, hk), BF16), jax.ShapeDtypeStruct((T, hk), BF16),
                   jax.ShapeDtypeStruct((T, hv), BF16), jax.ShapeDtypeStruct((T, hk), F32)],
        scratch_shapes=[pltpu.VMEM((GLA_H, GLA_DV, GLA_DK), F32)],
        compiler_params=_cp("arbitrary"),
    )(pbig, pbig, pbig, psmall, wgate, bgate, states, do)


def _gla_gate_bwd(dpre, psmall, wgate, dsm, name):
    T = dpre.shape[0]
    tm = min(512, T)
    W = GLA_H * GLA_DK
    ngrp = dsm.shape[0]

    def body(dp_ref, sm_ref, wg_ref, dsm_ref, ds_ref, dw_ref, db_ref):
        @pl.when(pl.program_id(0) == 0)
        def _():
            dw_ref[...] = jnp.zeros_like(dw_ref)
            db_ref[...] = jnp.zeros_like(db_ref)

        dp = dp_ref[...]
        ds = _dot_nt(dp, wg_ref[...], HI)
        for i in range(ngrp):
            ds += dsm_ref[i]
        ds_ref[...] = ds
        dw_ref[...] += _dot_tn(sm_ref[...], dp, HI)
        db_ref[...] += jnp.sum(dp, axis=0, keepdims=True)

    return pl.pallas_call(
        body, name=name, grid=(T // tm,),
        in_specs=[pl.BlockSpec((tm, W), lambda i: (i, 0)), pl.BlockSpec((tm, SMALL), lambda i: (i, 0)),
                  pl.BlockSpec((SMALL, W), lambda i: (0, 0)), pl.BlockSpec((ngrp, tm, SMALL), lambda i: (0, i, 0))],
        out_specs=[pl.BlockSpec((tm, SMALL), lambda i: (i, 0)), pl.BlockSpec((SMALL, W), lambda i: (0, 0)),
                   pl.BlockSpec((1, W), lambda i: (0, 0))],
        out_shape=[jax.ShapeDtypeStruct((T, SMALL), F32), jax.ShapeDtypeStruct((SMALL, W), F32),
                   jax.ShapeDtypeStruct((1, W), F32)],
        compiler_params=_cp("arbitrary"),
    )(dpre, psmall, wgate, dsm)


CONV_C = 3 * 1024
CONV_BLK = 256


def _conv_fwd(pbig, cw8, name):
    T = pbig.shape[0]
    blk = min(CONV_BLK, T)

    def body(x_ref, w_ref, c_ref, prev_ref):
        @pl.when(pl.program_id(0) == 0)
        def _():
            prev_ref[...] = jnp.zeros_like(prev_ref)

        x = x_ref[...].astype(F32)
        prev = prev_ref[...]
        row8 = _iota2(8, CONV_C, 0)
        acc = x * w_ref[CONV_K - 1:CONV_K, :]
        for s in range(1, CONV_K):
            xs = pltpu.roll(x, s, 0)
            top = jnp.where(row8 < s, pltpu.roll(prev, s, 0), xs[:8])
            xs = jnp.concatenate([top, xs[8:]], axis=0)
            acc += xs * w_ref[CONV_K - 1 - s:CONV_K - s, :]
        c_ref[...] = _b(acc)
        prev_ref[...] = x[blk - 8:]

    return pl.pallas_call(
        body, name=name, grid=(T // blk,),
        in_specs=[pl.BlockSpec((blk, CONV_C), lambda i: (i, 1)), pl.BlockSpec((8, CONV_C), lambda i: (0, 0))],
        out_specs=pl.BlockSpec((blk, CONV_C), lambda i: (i, 0)),
        out_shape=jax.ShapeDtypeStruct((T, CONV_C), BF16),
        scratch_shapes=[pltpu.VMEM((8, CONV_C), F32)],
        compiler_params=_cp("arbitrary"),
    )(pbig, cw8)


def _conv_bwd(dcq, dck, dcv, pbig, cw8, name):
    T = pbig.shape[0]
    blk = min(CONV_BLK, T)
    nb = T // blk

    def body(dq_ref, dk_ref, dv_ref, x_ref, w_ref, dx_ref, dw_ref, nxt_ref):
        @pl.when(pl.program_id(0) == 0)
        def _():
            nxt_ref[...] = jnp.zeros_like(nxt_ref)
            dw_ref[...] = jnp.zeros_like(dw_ref)

        dc = jnp.concatenate([dq_ref[...], dk_ref[...], dv_ref[...]], axis=1).astype(F32)
        x = x_ref[...].astype(F32)
        nxt = nxt_ref[...]
        row8 = _iota2(8, CONV_C, 0)
        acc = dc * w_ref[CONV_K - 1:CONV_K, :]
        dws = [jnp.sum(dc * x, axis=0, keepdims=True)]
        for s in range(1, CONV_K):
            ds = pltpu.roll(dc, blk - s, 0)
            bot = jnp.where(row8 >= 8 - s, pltpu.roll(nxt, 8 - s, 0), ds[blk - 8:])
            ds = jnp.concatenate([ds[:blk - 8], bot], axis=0)
            acc += ds * w_ref[CONV_K - 1 - s:CONV_K - s, :]
            dws.append(jnp.sum(ds * x, axis=0, keepdims=True))
        dx_ref[...] = _b(acc)
        dw_ref[...] += jnp.concatenate(dws[::-1] + [jnp.zeros((8 - CONV_K, CONV_C), F32)], axis=0)
        nxt_ref[...] = dc[:8]

    part = pl.BlockSpec((blk, 1024), lambda i: (nb - 1 - i, 0))
    return pl.pallas_call(
        body, name=name, grid=(nb,),
        in_specs=[part, part, part, pl.BlockSpec((blk, CONV_C), lambda i: (nb - 1 - i, 1)),
                  pl.BlockSpec((8, CONV_C), lambda i: (0, 0))],
        out_specs=[pl.BlockSpec((blk, CONV_C), lambda i: (nb - 1 - i, 0)), pl.BlockSpec((8, CONV_C), lambda i: (0, 0))],
        out_shape=[jax.ShapeDtypeStruct((T, CONV_C), BF16), jax.ShapeDtypeStruct((8, CONV_C), F32)],
        scratch_shapes=[pltpu.VMEM((8, CONV_C), F32)],
        compiler_params=_cp("arbitrary"),
    )(dcq, dck, dcv, pbig, cw8)


def _col(x, lane):
    sel = _iota2(x.shape[0], x.shape[1], 1) == lane
    return jnp.broadcast_to(jnp.sum(jnp.where(sel, x, 0.0), axis=1, keepdims=True), x.shape)


def _bmm(a, b, prec=None):
    return jnp.einsum("bij,bjk->bik", a, b, preferred_element_type=F32, precision=prec)


def _bmm_nt(a, b, prec=None):
    return jnp.einsum("bij,bkj->bik", a, b, preferred_element_type=F32, precision=prec)


def _bmm_tn(a, b, prec=None):
    return jnp.einsum("bji,bjk->bik", a, b, preferred_element_type=F32, precision=prec)


def _unit_lower_inverse(low):
    eye = (_iota2(CHUNK, CHUNK, 0) == _iota2(CHUNK, CHUNK, 1)).astype(F32)
    xk = -low
    inv = eye + xk
    for _ in range(5):
        xk = _bmm(xk, xk, HI)
        inv = inv + _bmm(inv, xk, HI)
    return inv


def _heads_first(x, nc, w=128):
    hb = x.shape[1] // w
    return jnp.concatenate([x[:, i * w:(i + 1) * w].reshape(nc, CHUNK, w) for i in range(hb)], axis=0)


def _heads_last(x, nc):
    hb = x.shape[0] // nc
    return jnp.concatenate([x[i * nc:(i + 1) * nc].reshape(nc * CHUNK, x.shape[2]) for i in range(hb)], axis=1)


def _gdn_block(cq_ref, ck_ref, cv_ref, sm_ref, par_ref, h0, hb, nc, masks):
    causal, strict, tril, eye = masks
    nbat = hb * nc
    cq = _heads_first(cq_ref[...].astype(F32), nc)
    ck = _heads_first(ck_ref[...].astype(F32), nc)
    cv = _heads_first(cv_ref[...].astype(F32), nc)
    sq, sk, sv = _sigmoid(cq), _sigmoid(ck), _sigmoid(cv)
    q, k, v = cq * sq, ck * sk, cv * sv
    rq = lax.rsqrt(jnp.sum(q * q, axis=-1, keepdims=True) + EPS)
    rk = lax.rsqrt(jnp.sum(k * k, axis=-1, keepdims=True) + EPS)
    qh, kn = q * rq, k * rk
    qn = qh * (DN_DK ** -0.5)
    sm = sm_ref[...]
    per_head = lambda fn: jnp.concatenate([fn(i) for i in range(hb)], axis=0)
    braw = per_head(lambda i: _col(sm, GLA_RANK + h0 + i).reshape(nc, CHUNK, 128))
    araw = per_head(lambda i: _col(sm, GLA_RANK + DN_H + h0 + i).reshape(nc, CHUNK, 128))
    ea = per_head(lambda i: jnp.broadcast_to(jnp.exp(par_ref[i, 0:1, :])[None], (nc, 1, 128)))
    bias = per_head(lambda i: jnp.broadcast_to(par_ref[i, 1:2, :][None], (nc, 1, 128)))
    beta = _sigmoid(braw)
    sp_arg = araw + bias
    g = -ea * _softplus(sp_arg)
    G = _bmm(jnp.broadcast_to(tril, (nbat, CHUNK, CHUNK)), g, HI)
    gc = G[:, :, :CHUNK]
    grow = jnp.sum(eye * gc, axis=1, keepdims=True)
    decay = jnp.exp(jnp.where(causal, gc - grow, -1e30))
    kb = kn * beta
    A = _bmm_nt(kb, kn, HI)
    low = jnp.where(strict, A * decay, 0.0)
    tinv = _unit_lower_inverse(low)
    eG = jnp.exp(G)
    gl = G[:, CHUNK - 1:CHUNK, :]
    eGl = jnp.exp(gl - G)
    g_ch = jnp.exp(gl)
    rv = v * beta
    rkk = kb * eG
    u = _bmm(tinv, rv, HI)
    w = _bmm(tinv, rkk, HI)
    B = _bmm_nt(_b(qn), _b(kn))
    qk = jnp.where(causal, B * decay, 0.0)
    q_dec = qn * eG
    k_st = kn * eGl
    return dict(cq=cq, ck=ck, cv=cv, sq=sq, sk=sk, sv=sv, q=q, k=k, v=v, rq=rq, rk=rk, qh=qh, kn=kn, qn=qn,
                beta=beta, ea=ea, sp_arg=sp_arg, g=g, G=G, decay=decay, kb=kb, A=A, tinv=tinv, eG=eG, eGl=eGl,
                g_ch=g_ch, rv=rv, rkk=rkk, u=u, w=w, B=B, qk=qk, q_dec=q_dec, k_st=k_st)


def _gdn_masks():
    r, c = _iota2(CHUNK, CHUNK, 0), _iota2(CHUNK, CHUNK, 1)
    return r >= c, r > c, (r >= c).astype(F32), (r == c).astype(F32)


def _gdn_specs(blk, hb, idx):
    ng = DN_H // hb
    return [pl.BlockSpec((blk, hb * DN_DK), lambda h, j: (idx(j), h)),
            pl.BlockSpec((blk, hb * DN_DK), lambda h, j: (idx(j), ng + h)),
            pl.BlockSpec((blk, hb * DN_DV), lambda h, j: (idx(j), 2 * ng + h)),
            pl.BlockSpec((blk, SMALL), lambda h, j: (idx(j), 0)),
            pl.BlockSpec((hb, 8, 128), lambda h, j: (h, 0, 0))]


def _gdn_fwd(conv, psmall, par, name):
    T = conv.shape[0]
    blk = min(GDN_BLK, T)
    nc = blk // CHUNK
    hb = GDN_HEADS

    def body(cq_ref, ck_ref, cv_ref, sm_ref, par_ref, o_ref, ss_ref, s_ref):
        @pl.when(pl.program_id(1) == 0)
        def _():
            s_ref[...] = jnp.zeros_like(s_ref)

        f = _gdn_block(cq_ref, ck_ref, cv_ref, sm_ref, par_ref, pl.program_id(0) * hb, hb, nc, _gdn_masks())
        wb, qdb, kstb, qkb = _b(f["w"]), _b(f["q_dec"]), _b(f["k_st"]), _b(f["qk"])
        S = [s_ref[i] for i in range(hb)]
        for c in range(nc):
            for i in range(hb):
                n = i * nc + c
                ss_ref[i, c] = S[i]
                Sb = _b(S[i])
                v_new = _b(f["u"][n] - _dot(wb[n], Sb))
                o_ref[pl.ds(c * CHUNK, CHUNK), i * DN_DV:(i + 1) * DN_DV] = _dot(qdb[n], Sb) + _dot(qkb[n], v_new)
                S[i] = S[i] * f["g_ch"][n] + _dot_tn(kstb[n], v_new)
        for i in range(hb):
            s_ref[i] = S[i]

    return pl.pallas_call(
        body, name=name, grid=(DN_H // hb, T // blk),
        in_specs=_gdn_specs(blk, hb, lambda j: j),
        out_specs=[pl.BlockSpec((blk, hb * DN_DV), lambda h, j: (j, h)),
                   pl.BlockSpec((hb, nc, DN_DK, DN_DV), lambda h, j: (h, j, 0, 0))],
        out_shape=[jax.ShapeDtypeStruct((T, DN_H * DN_DV), F32),
                   jax.ShapeDtypeStruct((DN_H, T // CHUNK, DN_DK, DN_DV), F32)],
        scratch_shapes=[pltpu.VMEM((hb, DN_DK, DN_DV), F32)],
        compiler_params=_cp("parallel", "arbitrary"),
    )(conv, conv, conv, psmall, par)


def _gdn_bwd(conv, psmall, par, states, do, name):
    T = conv.shape[0]
    blk = min(GDN_BLK, T)
    nc = blk // CHUNK
    nb = T // blk
    hb = GDN_HEADS
    nbat = hb * nc
    rsum = lambda x: jnp.sum(x, axis=-1, keepdims=True)

    def body(cq_ref, ck_ref, cv_ref, sm_ref, par_ref, ss_ref, do_ref,
             dcq_ref, dck_ref, dcv_ref, dsm_ref, dpar_ref, ds_ref):
        @pl.when(pl.program_id(1) == 0)
        def _():
            ds_ref[...] = jnp.zeros_like(ds_ref)
            dpar_ref[...] = jnp.zeros_like(dpar_ref)

        masks = _gdn_masks()
        causal, strict, tril, eye = masks
        triu = (_iota2(CHUNK, CHUNK, 0) <= _iota2(CHUNK, CHUNK, 1)).astype(F32)
        lane = _iota2(CHUNK, 128, 1)
        last_row = _iota2(CHUNK, 128, 0) == CHUNK - 1
        h0 = pl.program_id(0) * hb
        f = _gdn_block(cq_ref, ck_ref, cv_ref, sm_ref, par_ref, h0, hb, nc, masks)
        S = ss_ref[...].reshape(nbat, DN_DK, DN_DV)
        Sb = _b(S)
        do_ = _b(_heads_first(do_ref[...], nc))
        wb, qdb, kstb, qkb = _b(f["w"]), _b(f["q_dec"]), _b(f["k_st"]), _b(f["qk"])
        vnb = _b(f["u"] - _bmm(wb, Sb))
        dvn0 = _bmm_tn(qkb, do_)
        qdo = _bmm_tn(qdb, do_)
        dS = [ds_ref[i] for i in range(hb)]
        after = [None] * nbat
        for c in range(nc - 1, -1, -1):
            for i in range(hb):
                n = i * nc + c
                after[n] = dS[i]
                dvn_c = _b(dvn0[n] + _dot(kstb[n], _b(dS[i])))
                dS[i] = dS[i] * f["g_ch"][n] + qdo[n] - _dot_tn(wb[n], dvn_c)
        for i in range(hb):
            ds_ref[i] = dS[i]
        dSa = jnp.stack(after)
        dSb = _b(dSa)
        dvn = dvn0 + _bmm(kstb, dSb)
        dvnb = _b(dvn)
        dq_dec = _bmm_nt(do_, Sb)
        dqk = jnp.where(causal, _bmm_nt(do_, vnb), 0.0)
        dk_st = _bmm_nt(vnb, dSb)
        dg_ch = jnp.sum(rsum(S * dSa), axis=1, keepdims=True)
        dw = -_bmm_nt(dvnb, Sb)
        drv = _bmm_tn(f["tinv"], dvn, HI)
        drk = _bmm_tn(f["tinv"], dw, HI)
        dlow = jnp.where(strict, -(_bmm_nt(drv, f["u"], HI) + _bmm_nt(drk, f["w"], HI)), 0.0)
        dv = drv * f["beta"]
        dbeta = rsum(drv * f["v"])
        dkb = drk * f["eG"]
        dG = rsum(drk * f["rkk"])
        dA = dlow * f["decay"]
        ddec = dlow * f["A"]
        dkb += _bmm(dA, f["kn"], HI)
        dkn = _bmm_tn(dA, f["kb"], HI)
        dB = dqk * f["decay"]
        ddec += dqk * f["B"]
        dqn = _bmm(_b(dB), _b(f["kn"]))
        dkn += _bmm_tn(_b(dB), _b(f["qn"]))
        dD = ddec * f["decay"]
        dG += rsum(dD) - rsum(eye * jnp.sum(dD, axis=1, keepdims=True))
        dqn += dq_dec * f["eG"]
        dG += rsum(dq_dec * f["q_dec"])
        dkn += dk_st * f["eGl"]
        tks = rsum(dk_st * f["k_st"])
        dG -= tks
        dG_last = jnp.sum(tks, axis=1, keepdims=True) + dg_ch * f["g_ch"][:, :, :1]
        dkn += dkb * f["beta"]
        dbeta += rsum(dkb * f["kn"])
        dGf = jnp.broadcast_to(dG, (nbat, CHUNK, 128)) + jnp.where(last_row, dG_last, 0.0)
        dg = _bmm(jnp.broadcast_to(triu, (nbat, CHUNK, CHUNK)), dGf, HI)
        dbraw = dbeta * f["beta"][:, :, :1] * (1.0 - f["beta"][:, :, :1])
        daraw = dg * (-f["ea"]) * _sigmoid(f["sp_arg"])
        both = lambda t: jnp.sum(jnp.sum(t, axis=1, keepdims=True), axis=0)
        dgg = dg * f["g"]
        dsm = jnp.zeros((nc, CHUNK, SMALL), F32)
        for i in range(hb):
            mine = slice(i * nc, (i + 1) * nc)
            dsm += (jnp.where(lane == GLA_RANK + h0 + i, dbraw[mine], 0.0)
                    + jnp.where(lane == GLA_RANK + DN_H + h0 + i, daraw[mine], 0.0))
            dpar = jnp.where(lane[:1] == 0, both(dgg[mine]), jnp.where(lane[:1] == 1, both(daraw[mine]), 0.0))
            dpar_ref[i] += jnp.broadcast_to(dpar, (8, 128))
        dsm_ref[0] = dsm.reshape(blk, SMALL)
        dqh = dqn * (DN_DK ** -0.5)
        dq = f["rq"] * (dqh - f["qh"] * rsum(dqh * f["qh"]))
        dk = f["rk"] * (dkn - f["kn"] * rsum(dkn * f["kn"]))
        dsilu = lambda x, s: s * (1.0 + x * (1.0 - s))
        dcq_ref[...] = _b(_heads_last(dq * dsilu(f["cq"], f["sq"]), nc))
        dck_ref[...] = _b(_heads_last(dk * dsilu(f["ck"], f["sk"]), nc))
        dcv_ref[...] = _b(_heads_last(dv * dsilu(f["cv"], f["sv"]), nc))

    r = lambda j: nb - 1 - j
    out_blk = pl.BlockSpec((blk, hb * DN_DK), lambda h, j: (r(j), h))
    return pl.pallas_call(
        body, name=name, grid=(DN_H // hb, nb),
        in_specs=_gdn_specs(blk, hb, r) + [pl.BlockSpec((hb, nc, DN_DK, DN_DV), lambda h, j: (h, r(j), 0, 0)),
                                          pl.BlockSpec((blk, hb * DN_DV), lambda h, j: (r(j), h))],
        out_specs=[out_blk, out_blk, out_blk, pl.BlockSpec((1, blk, SMALL), lambda h, j: (h, r(j), 0)),
                   pl.BlockSpec((hb, 8, 128), lambda h, j: (h, 0, 0))],
        out_shape=[jax.ShapeDtypeStruct((T, DN_H * DN_DK), BF16)] * 3 + [
            jax.ShapeDtypeStruct((DN_H // hb, T, SMALL), F32), jax.ShapeDtypeStruct((DN_H, 8, 128), F32)],
        scratch_shapes=[pltpu.VMEM((hb, DN_DK, DN_DV), F32)],
        compiler_params=_cp("parallel", "arbitrary"),
    )(conv, conv, conv, psmall, par, states, do)


def _head_norm(o, w, dv):
    outs, rs = [], []
    for i in range(o.shape[1] // dv):
        oh = o[:, i * dv:(i + 1) * dv]
        r = lax.rsqrt(jnp.mean(oh * oh, axis=-1, keepdims=True) + EPS)
        outs.append(oh * r)
        rs.append(r)
    return outs, rs


def _merge_specs(tm):
    col = lambda c: pl.BlockSpec((tm, D), lambda i: (i, c))
    return [col(0), col(0), col(2), col(6), col(7), col(8),
            pl.BlockSpec((1, GLA_DV), lambda i: (0, 0)), pl.BlockSpec((1, DN_DV), lambda i: (0, 0)),
            pl.BlockSpec((D, D), lambda i: (0, 0))]


def _merge_fwd(h, oa, ob, pbig, gla_hn, dn_hn, wout, name):
    T = h.shape[0]
    tm = min(ROW_BLK, T)

    def body(h_ref, oa_ref, ob_ref, gr_ref, dg_ref, ma_ref, mb_ref, wa_ref, wb_ref, wo_ref, ho_ref, y_ref):
        na, _ = _head_norm(oa_ref[...], wa_ref[...], GLA_DV)
        nbs, _ = _head_norm(ob_ref[...], wb_ref[...], DN_DV)
        hna = jnp.concatenate([t * wa_ref[...] for t in na], axis=1)
        hnb = jnp.concatenate([t * wb_ref[...] for t in nbs], axis=1)
        gr = gr_ref[...].astype(F32)
        dg = dg_ref[...].astype(F32)
        y = (_sigmoid(ma_ref[...].astype(F32)) * hna * (gr * _sigmoid(gr))
             + _sigmoid(mb_ref[...].astype(F32)) * hnb * (dg * _sigmoid(dg)))
        yb = _b(y)
        y_ref[...] = yb
        ho_ref[...] = h_ref[...] + _dot(yb, wo_ref[...])

    row = pl.BlockSpec((tm, D), lambda i: (i, 0))
    return pl.pallas_call(
        body, name=name, grid=(T // tm,),
        in_specs=[row] + _merge_specs(tm),
        out_specs=[row, row],
        out_shape=[jax.ShapeDtypeStruct((T, D), F32), jax.ShapeDtypeStruct((T, D), BF16)],
        compiler_params=_cp("arbitrary"),
    )(h, oa, ob, pbig, pbig, pbig, pbig, gla_hn, dn_hn, wout)


def _merge_bwd(dh, oa, ob, pbig, gla_hn, dn_hn, wout, name):
    T = dh.shape[0]
    tm = min(ROW_BLK, T)

    def branch(dy, o_ref, w_ref, gate_ref, m_ref, dv):
        w = w_ref[...]
        ohat, rs = _head_norm(o_ref[...], w, dv)
        gate = gate_ref[...].astype(F32)
        m = m_ref[...].astype(F32)
        sgate, sm = _sigmoid(gate), _sigmoid(m)
        silu = gate * sgate
        ohat_all = jnp.concatenate(ohat, axis=1)
        hn = jnp.concatenate([t * w for t in ohat], axis=1)
        d_on = dy * sm
        d_m = dy * hn * silu * sm * (1.0 - sm)
        d_hn = d_on * silu
        d_gate = d_on * hn * (sgate * (1.0 + gate * (1.0 - sgate)))
        dw = jnp.zeros((1, dv), F32)
        d_o = []
        for i, (oh, r) in enumerate(zip(ohat, rs)):
            dhn = d_hn[:, i * dv:(i + 1) * dv]
            dw += jnp.sum(dhn * oh, axis=0, keepdims=True)
            dohat = dhn * w
            d_o.append(r * (dohat - oh * jnp.mean(dohat * oh, axis=-1, keepdims=True)))
        return jnp.concatenate(d_o, axis=1), d_gate, d_m, dw

    def body(dh_ref, oa_ref, ob_ref, gr_ref, dg_ref, ma_ref, mb_ref, wa_ref, wb_ref, wo_ref,
             doa_ref, dob_ref, dgr_ref, ddg_ref, dma_ref, dmb_ref, dwa_ref, dwb_ref, dhb_ref):
        @pl.when(pl.program_id(0) == 0)
        def _():
            dwa_ref[...] = jnp.zeros_like(dwa_ref)
            dwb_ref[...] = jnp.zeros_like(dwb_ref)

        dhb = _b(dh_ref[...])
        dhb_ref[...] = dhb
        dy = _dot_nt(dhb, wo_ref[...])
        d_oa, d_gr, d_ma, dwa = branch(dy, oa_ref, wa_ref, gr_ref, ma_ref, GLA_DV)
        d_ob, d_dg, d_mb, dwb = branch(dy, ob_ref, wb_ref, dg_ref, mb_ref, DN_DV)
        doa_ref[...] = d_oa
        dob_ref[...] = d_ob
        dgr_ref[...] = _b(d_gr)
        ddg_ref[...] = _b(d_dg)
        dma_ref[...] = _b(d_ma)
        dmb_ref[...] = _b(d_mb)
        dwa_ref[...] += dwa
        dwb_ref[...] += dwb

    row = pl.BlockSpec((tm, D), lambda i: (i, 0))
    f32 = jax.ShapeDtypeStruct((T, D), F32)
    b16 = jax.ShapeDtypeStruct((T, D), BF16)
    return pl.pallas_call(
        body, name=name, grid=(T // tm,),
        in_specs=[row] + _merge_specs(tm),
        out_specs=[row] * 6 + [pl.BlockSpec((1, GLA_DV), lambda i: (0, 0)), pl.BlockSpec((1, DN_DV), lambda i: (0, 0)), row],
        out_shape=[f32, f32, b16, b16, b16, b16, jax.ShapeDtypeStruct((1, GLA_DV), F32),
                   jax.ShapeDtypeStruct((1, DN_DV), F32), b16],
        compiler_params=_cp("arbitrary"),
    )(dh, oa, ob, pbig, pbig, pbig, pbig, gla_hn, dn_hn, wout)


def _loss_head(h, nw, target, name):
    T = h.shape[0]
    tm = min(512, T)

    def body(h_ref, nw_ref, t_ref, dx_ref, loss_ref, dnw_ref):
        @pl.when(pl.program_id(0) == 0)
        def _():
            loss_ref[...] = jnp.zeros_like(loss_ref)
            dnw_ref[...] = jnp.zeros_like(dnw_ref)

        x = h_ref[...]
        w = nw_ref[...]
        r = lax.rsqrt(jnp.mean(x * x, axis=-1, keepdims=True) + EPS)
        xhat = x * r
        err = xhat * w - t_ref[...]
        part = jnp.sum(jnp.sum(err * err, axis=-1, keepdims=True), axis=0, keepdims=True)
        loss_ref[...] += (0.5 / D) * part
        dout = err * (1.0 / D)
        dnw_ref[...] += jnp.sum(dout * xhat, axis=0, keepdims=True)
        dxhat = dout * w
        dx_ref[...] = r * (dxhat - xhat * jnp.mean(dxhat * xhat, axis=-1, keepdims=True))

    row = pl.BlockSpec((tm, D), lambda i: (i, 0))
    one = pl.BlockSpec((1, D), lambda i: (0, 0))
    return pl.pallas_call(
        body, name=name, grid=(T // tm,),
        in_specs=[row, one, row],
        out_specs=[row, pl.BlockSpec((8, 128), lambda i: (0, 0)), one],
        out_shape=[jax.ShapeDtypeStruct((T, D), F32), jax.ShapeDtypeStruct((8, 128), F32),
                   jax.ShapeDtypeStruct((1, D), F32)],
        compiler_params=_cp("arbitrary"),
    )(h, nw, target)


def _adamw(w, g, m, v, rows, name):
    R, C = w.shape
    rows = min(rows, R)
    c1 = 1.0 - ADAM_B1 ** ADAM_STEP
    c2 = 1.0 - ADAM_B2 ** ADAM_STEP

    def body(w_ref, g_ref, m_ref, v_ref, d_ref, mo_ref, vo_ref):
        g_ = g_ref[...]
        m_ = ADAM_B1 * m_ref[...] + (1.0 - ADAM_B1) * g_
        v_ = ADAM_B2 * v_ref[...] + (1.0 - ADAM_B2) * (g_ * g_)
        mo_ref[...] = m_
        vo_ref[...] = v_
        d_ref[...] = -ADAM_LR * ((m_ / c1) / (jnp.sqrt(v_ / c2) + ADAM_EPS) + ADAM_WD * w_ref[...])

    blk = pl.BlockSpec((rows, C), lambda i: (i, 0))
    shp = jax.ShapeDtypeStruct((R, C), F32)
    return pl.pallas_call(
        body, name=name, grid=(R // rows,),
        in_specs=[blk] * 4, out_specs=[blk] * 3, out_shape=[shp] * 3,
        compiler_params=_cp("parallel"),
    )(w, g, m, v)


def _me():
    return lax.axis_index("x"), lax.axis_index("y"), lax.axis_index("c")


def _other_chips(x, y):
    return [(1 - x, y), (x, 1 - y), (1 - x, 1 - y)]


def _half_rows(ref, hf):
    half = ref.shape[-2] // 2
    rows = pl.ds(pl.multiple_of(hf * half, 16), half)
    return ref.at[rows, :] if len(ref.shape) == 2 else ref.at[:, rows, :]


def _gather_weights(big, small, name):
    nbig, nsm = len(big), len(small)
    n = nbig + nsm
    own_sem = 6 * nbig + 3 * nsm

    def body(*refs):
        ins, outs = refs[:n], refs[n:2 * n]
        send_sems, recv_sems = refs[2 * n:]
        x, y, c = _me()
        sibling = (x, y, 1 - c)
        chips = _other_chips(x, y)
        slot = lambda chip: 2 * chip[0] + chip[1]

        def copy(k, src, dst, to):
            return pltpu.make_async_remote_copy(src_ref=src, dst_ref=dst, send_sem=send_sems.at[k],
                                                recv_sem=recv_sems.at[k], device_id=to, device_id_type=MESH)

        sent = []
        for i in range(nbig):
            sent.append(copy(own_sem + i, ins[i], outs[i].at[slot((x, y))], sibling))
            sent[-1].start()
            for j, chip in enumerate(chips):
                sent.append(copy(6 * i + j, _half_rows(ins[i], c), _half_rows(outs[i].at[slot((x, y))], c), (*chip, c)))
                sent[-1].start()
        for t in range(nsm):
            w_ref, o_ref = ins[nbig + t], outs[nbig + t]
            o_ref[slot((x, y))] = w_ref[...]
            for j, chip in enumerate(chips):
                sent.append(copy(6 * nbig + 3 * t + j, w_ref, o_ref.at[slot((x, y))], (*chip, c)))
                sent[-1].start()
        for i in range(nbig):
            for j, chip in enumerate(chips):
                landed = _half_rows(outs[i].at[slot(chip)], c)
                copy(6 * i + j, landed, landed, (x, y, c)).wait_recv()
                sent.append(copy(6 * i + 3 + j, landed, landed, sibling))
                sent[-1].start()
        for t in range(nsm):
            for j, chip in enumerate(chips):
                landed = outs[nbig + t].at[slot(chip)]
                copy(6 * nbig + 3 * t + j, landed, landed, (x, y, c)).wait_recv()
        for i in range(nbig):
            for j, chip in enumerate(chips):
                passed = _half_rows(outs[i].at[slot(chip)], 1 - c)
                copy(6 * i + 3 + j, passed, passed, (x, y, c)).wait_recv()
        for i in range(nbig):
            mine = outs[i].at[slot((x, y))]
            copy(own_sem + i, mine, mine, (x, y, c)).wait_recv()
        for cp in sent:
            cp.wait_send()

    vm = pl.BlockSpec(memory_space=pltpu.VMEM)
    nsem = own_sem + nbig
    return pl.pallas_call(
        body, name=name, in_specs=[ANY] * nbig + [vm] * nsm, out_specs=[ANY] * nbig + [vm] * nsm,
        out_shape=[jax.ShapeDtypeStruct((N_SHARD,) + w.shape, w.dtype) for w in list(big) + list(small)],
        scratch_shapes=[pltpu.SemaphoreType.DMA((nsem,)), pltpu.SemaphoreType.DMA((nsem,))],
        compiler_params=pltpu.CompilerParams(has_side_effects=True),
    )(*big, *small)


def _rs_sibling(gs, name):
    n = len(gs)

    def body(*refs):
        send_sems, recv_sems = refs[2 * n:]
        x, y, c = _me()
        cps = [pltpu.make_async_remote_copy(src_ref=_half_rows(refs[i], 1 - c), dst_ref=refs[n + i],
                                            send_sem=send_sems.at[i], recv_sem=recv_sems.at[i],
                                            device_id=(x, y, 1 - c), device_id_type=MESH) for i in range(n)]
        for cp in cps:
            cp.start()
        for cp in cps:
            cp.wait()

    return pl.pallas_call(
        body, name=name, in_specs=[ANY] * n, out_specs=[ANY] * n,
        out_shape=[jax.ShapeDtypeStruct((g.shape[0], g.shape[1] // 2, g.shape[2]), g.dtype) for g in gs],
        scratch_shapes=[pltpu.SemaphoreType.DMA((n,)), pltpu.SemaphoreType.DMA((n,))],
        compiler_params=pltpu.CompilerParams(has_side_effects=True),
    )(*gs)


def _add_pair(g, other, where, name):
    ns, a, b = g.shape
    half = a // 2

    def body(w_ref, g_ref, o_ref, pb_ref, own_ref):
        t = g_ref[0].astype(F32) + o_ref[0].astype(F32)
        pb_ref[0] = _b(t)

        @pl.when(pl.program_id(0) == w_ref[1])
        def _():
            own_ref[...] = t

    return pl.pallas_call(
        body, name=name,
        grid_spec=pltpu.PrefetchScalarGridSpec(
            num_scalar_prefetch=1, grid=(ns,),
            in_specs=[pl.BlockSpec((1, half, b), lambda s, w: (s, w[0], 0)), pl.BlockSpec((1, half, b), lambda s, w: (s, 0, 0))],
            out_specs=[pl.BlockSpec((1, half, b), lambda s, w: (s, 0, 0)), pl.BlockSpec((half, b), lambda s, w: (0, 0))]),
        out_shape=[jax.ShapeDtypeStruct((ns, half, b), BF16), jax.ShapeDtypeStruct((half, b), F32)],
        compiler_params=_cp("arbitrary"),
    )(where, g, other)


def _rs_chips(pbs, name):
    n = len(pbs)

    def body(*refs):
        send_sems, recv_sems = refs[2 * n:]
        x, y, c = _me()
        cps = [pltpu.make_async_remote_copy(src_ref=refs[i].at[2 * chip[0] + chip[1]], dst_ref=refs[n + i].at[j],
                                            send_sem=send_sems.at[3 * i + j], recv_sem=recv_sems.at[3 * i + j],
                                            device_id=(*chip, c), device_id_type=MESH)
               for i in range(n) for j, chip in enumerate(_other_chips(x, y))]
        for cp in cps:
            cp.start()
        for cp in cps:
            cp.wait()

    return pl.pallas_call(
        body, name=name, in_specs=[ANY] * n, out_specs=[ANY] * n,
        out_shape=[jax.ShapeDtypeStruct((3,) + p.shape[1:], p.dtype) for p in pbs],
        scratch_shapes=[pltpu.SemaphoreType.DMA((3 * n,)), pltpu.SemaphoreType.DMA((3 * n,))],
        compiler_params=pltpu.CompilerParams(has_side_effects=True),
    )(*pbs)


def _add_four(own, got, name):
    rows, cols = own.shape
    rb = rows // 2

    def body(a_ref, b_ref, o_ref):
        o_ref[...] = ((a_ref[...] + b_ref[0].astype(F32)) + b_ref[1].astype(F32)) + b_ref[2].astype(F32)

    return pl.pallas_call(
        body, name=name, grid=(rows // rb,),
        in_specs=[pl.BlockSpec((rb, cols), lambda i: (i, 0)), pl.BlockSpec((3, rb, cols), lambda i: (0, i, 0))],
        out_specs=pl.BlockSpec((rb, cols), lambda i: (i, 0)),
        out_shape=jax.ShapeDtypeStruct((rows, cols), F32),
        compiler_params=_cp("parallel"),
    )(own, got)


def _rs_swap(halves, name):
    n = len(halves)

    def body(*refs):
        send_sems, recv_sems = refs[2 * n:]
        x, y, c = _me()
        cps = [pltpu.make_async_remote_copy(src_ref=refs[i], dst_ref=refs[n + i], send_sem=send_sems.at[i],
                                            recv_sem=recv_sems.at[i], device_id=(x, y, 1 - c), device_id_type=MESH)
               for i in range(n)]
        for cp in cps:
            cp.start()
        for cp in cps:
            cp.wait()

    return pl.pallas_call(
        body, name=name, in_specs=[ANY] * n, out_specs=[ANY] * n,
        out_shape=[jax.ShapeDtypeStruct(h.shape, h.dtype) for h in halves],
        scratch_shapes=[pltpu.SemaphoreType.DMA((n,)), pltpu.SemaphoreType.DMA((n,))],
        compiler_params=pltpu.CompilerParams(has_side_effects=True),
    )(*halves)


def _adamw_halves(w, own, got, m, v, rows, name):
    a, b = w.shape
    nblk = a // 2 // rows
    c1 = 1.0 - ADAM_B1 ** ADAM_STEP
    c2 = 1.0 - ADAM_B2 ** ADAM_STEP

    def body(w_ref, own_ref, got_ref, m_ref, v_ref, g_ref, d_ref, mo_ref, vo_ref):
        g_ = jnp.where(pl.program_id(0) == lax.axis_index("c"), own_ref[...], got_ref[...])
        g_ref[...] = g_
        m_ = ADAM_B1 * m_ref[...] + (1.0 - ADAM_B1) * g_
        v_ = ADAM_B2 * v_ref[...] + (1.0 - ADAM_B2) * (g_ * g_)
        mo_ref[...] = m_
        vo_ref[...] = v_
        d_ref[...] = -ADAM_LR * ((m_ / c1) / (jnp.sqrt(v_ / c2) + ADAM_EPS) + ADAM_WD * w_ref[...])

    whole = pl.BlockSpec((rows, b), lambda h, i: (h * nblk + i, 0))
    part = pl.BlockSpec((rows, b), lambda h, i: (i, 0))
    shp = jax.ShapeDtypeStruct((a, b), F32)
    return pl.pallas_call(
        body, name=name, grid=(2, nblk),
        in_specs=[whole, part, part, whole, whole], out_specs=[whole] * 4, out_shape=[shp] * 4,
        compiler_params=_cp("parallel", "parallel"),
    )(w, own, got, m, v)


def _allsum_small(vec, name):
    def body(v_ref, o_ref, buf_ref, send_sems, recv_sems):
        x, y, c = _me()
        me = 4 * x + 2 * y + c
        buf_ref[me] = v_ref[...]
        cps = []
        for k in range(1, 8):
            peer = (x ^ (k >> 2), y ^ ((k >> 1) & 1), c ^ (k & 1))
            cps.append(pltpu.make_async_remote_copy(src_ref=v_ref, dst_ref=buf_ref.at[me],
                                                    send_sem=send_sems.at[k - 1], recv_sem=recv_sems.at[k - 1],
                                                    device_id=peer, device_id_type=MESH))
        for cp in cps:
            cp.start()
        for k in range(1, 8):
            peer_idx = me ^ k
            pltpu.make_async_remote_copy(src_ref=v_ref, dst_ref=buf_ref.at[peer_idx],
                                         send_sem=send_sems.at[k - 1], recv_sem=recv_sems.at[k - 1],
                                         device_id=(x, y, c), device_id_type=MESH).wait_recv()
        for cp in cps:
            cp.wait_send()
        acc = buf_ref[0]
        for d in range(1, 8):
            acc = acc + buf_ref[d]
        o_ref[...] = acc

    return pl.pallas_call(
        body, name=name,
        in_specs=[pl.BlockSpec(memory_space=pltpu.VMEM)], out_specs=pl.BlockSpec(memory_space=pltpu.VMEM),
        out_shape=jax.ShapeDtypeStruct(vec.shape, F32),
        scratch_shapes=[pltpu.VMEM((8,) + vec.shape, F32), pltpu.SemaphoreType.DMA((7,)), pltpu.SemaphoreType.DMA((7,))],
        compiler_params=pltpu.CompilerParams(has_side_effects=True),
    )(vec)


BIG = ("ffn1_w_gate", "ffn1_w_up", "ffn1_w_down", "w_in", "w_out", "ffn2_w_gate", "ffn2_w_up", "ffn2_w_down")
TINY = ("w_gla_gate", "conv_w")
SHARDED = BIG + TINY


def _join_cols(w4):
    return jnp.transpose(w4, (1, 0, 2)).reshape(w4.shape[1], N_SHARD * w4.shape[2])


def _cut_cols(w):
    return jnp.transpose(w.reshape(w.shape[0], N_SHARD, w.shape[1] // N_SHARD), (1, 0, 2))


def _split_w_in(w):
    o = IN_OFF
    big = jnp.concatenate([w[:, :o[4]], w[:, o[5]:o[9]], w[:, o[11]:]], axis=1)
    small = jnp.concatenate([w[:, o[4]:o[5]], w[:, o[9]:o[11]], jnp.zeros((w.shape[0], SMALL - 32), w.dtype)], axis=1)
    return big, small


def _merge_w_in(big, small):
    return jnp.concatenate([big[:, :3072], small[:, :16], big[:, 3072:7168], small[:, 16:32], big[:, 7168:]], axis=1)


def _local_step(x, target, W, P):
    wbig, wsmall = W["w_in_big"], W["w_in_small"]
    wgate_pad = jnp.zeros((SMALL, GLA_H * GLA_DK), F32).at[:GLA_RANK].set(P["w_gla_gate"])
    cw8 = jnp.zeros((8, CONV_C), F32).at[:CONV_K].set(P["conv_w"])
    par = jnp.zeros((DN_H, 8, 128), F32)
    par = par.at[:, 0, :].set(jnp.broadcast_to(P["dn_a_log"].reshape(DN_H, 1), (DN_H, 128)))
    par = par.at[:, 1, :].set(jnp.broadcast_to(P["dn_dt_bias"].reshape(DN_H, 1), (DN_H, 128)))

    h1, n1, g1, u1 = _ffn_fwd(x, P["ffn1_norm"], W["ffn1_w_gate"], W["ffn1_w_up"], W["ffn1_w_down"], "ffn1_fwd")
    pbig, psmall, n2 = _norm_proj(h1, P["mix_norm"], wbig, wsmall, "mix_proj")
    oa, sa = _gla_fwd(pbig, psmall, wgate_pad, P["b_gla_gate"], "gla_fwd")
    conv = _conv_fwd(pbig, cw8, "conv_fwd")
    ob, sb = _gdn_fwd(conv, psmall, par, "gdn_fwd")
    h2, yb = _merge_fwd(h1, oa, ob, pbig, P["gla_head_norm"], P["dn_head_norm"], W["w_out"], "merge_fwd")
    h3, n3, g3, u3 = _ffn_fwd(h2, P["ffn2_norm"], W["ffn2_w_gate"], W["ffn2_w_up"], W["ffn2_w_down"], "ffn2_fwd")
    dh3, loss, d_final = _loss_head(h3, P["final_norm"], target, "loss_head")

    gw, gs = {}, {"final_norm": d_final}

    def ffn_grads(tag, dh, h, n, g, u):
        dx, dg, du, act, dfb, dnw = _ffn_bwd(dh, h, P[tag + "_norm"], g, u, W[tag + "_w_gate"], W[tag + "_w_up"],
                                             W[tag + "_w_down"], tag + "_bwd")
        gw[tag + "_w_gate"] = _mm_tn(n, dg, D, FF_CUT, tag + "_dwg")
        gw[tag + "_w_up"] = _mm_tn(n, du, D, FF_CUT, tag + "_dwu")
        gw[tag + "_w_down"] = _mm_tn(act, dfb, FF_CUT, D, tag + "_dwd")
        gs[tag + "_norm"] = dnw
        return dx

    dh2 = ffn_grads("ffn2", dh3, h2, n3, g3, u3)
    d_oa, d_ob, d_gr, d_dgate, d_ma, d_mb, gs["gla_head_norm"], gs["dn_head_norm"], dh2b = _merge_bwd(
        dh2, oa, ob, pbig, P["gla_head_norm"], P["dn_head_norm"], W["w_out"], "merge_bwd")
    gw["w_out"] = _mm_tn(yb, dh2b, D, D, "dw_out").reshape(N_SHARD, D // N_SHARD, D)
    d_gq, d_gk, d_gv, dpre = _gla_bwd(pbig, psmall, wgate_pad, P["b_gla_gate"], sa, d_oa, "gla_bwd")
    dcq, dck, dcv, dsm, dpar = _gdn_bwd(conv, psmall, par, sb, d_ob, "gdn_bwd")
    dsmall, dwgate, gs["b_gla_gate"] = _gla_gate_bwd(dpre, psmall, wgate_pad, dsm, "gla_gate_bwd")
    gs["w_gla_gate"] = dwgate[:GLA_RANK]
    d_x3, dcw = _conv_bwd(dcq, dck, dcv, pbig, cw8, "conv_bwd")
    gs["conv_w"] = dcw[:CONV_K]
    gs["dn_a_log"] = dpar[:, 0, 0].reshape(1, DN_H)
    gs["dn_dt_bias"] = dpar[:, 0, 1].reshape(1, DN_H)
    pieces = (d_gq, d_gk, d_gv, d_gr, d_x3, d_dgate, d_ma, d_mb)
    dh1, gs["mix_norm"] = _proj_bwd(dh2, h1, P["mix_norm"], pieces, dsmall, wbig, wsmall, "proj_bwd")
    dbig = jnp.concatenate([_mm_tn(n2, p, D, 1024, "dw_in_%d" % i) for i, p in enumerate(pieces)], axis=1)
    dsml = _mm_tn(n2, dsmall, D, SMALL, "dw_in_small")
    gw["w_in"] = _cut_cols(_merge_w_in(dbig, dsml))
    grad_x = ffn_grads("ffn1", dh1, x, n1, g1, u1)
    return loss, grad_x, gw, gs


SMALL_NAMES = ("ffn1_norm", "mix_norm", "ffn2_norm", "final_norm", "b_gla_gate", "gla_head_norm", "dn_head_norm",
               "dn_a_log", "dn_dt_bias")
ROW4 = (("b_gla_gate", 512), ("gla_head_norm", 256), ("dn_head_norm", 128), ("dn_a_log", 8), ("dn_dt_bias", 8))


def _pack_small(d, loss=None):
    row4 = [d[n].reshape(-1) for n, _ in ROW4]
    row4.append(jnp.zeros((1,), F32) if loss is None else loss.reshape(1))
    row4 = jnp.concatenate(row4)
    row4 = jnp.pad(row4, (0, D - row4.shape[0]))
    rows = [d[n].reshape(-1) for n in SMALL_NAMES[:4]] + [row4]
    return jnp.concatenate([jnp.stack(rows), jnp.zeros((3, D), F32)], axis=0)


def _unpack_small(a, like):
    out = {n: a[i].reshape(like[n].shape) for i, n in enumerate(SMALL_NAMES[:4])}
    off = 0
    for n, w in ROW4:
        out[n] = a[4, off:off + w].reshape(like[n].shape)
        off += w
    return out, a[4, off]


WEIGHT_ORDER = ("ffn1_norm", "ffn1_w_gate", "ffn1_w_up", "ffn1_w_down", "mix_norm", "w_in", "w_gla_gate", "b_gla_gate",
                "conv_w", "dn_a_log", "dn_dt_bias", "gla_head_norm", "dn_head_norm", "w_out", "ffn2_norm",
                "ffn2_w_gate", "ffn2_w_up", "ffn2_w_down", "final_norm")
ADAM_ROWS = {"ffn1_w_gate": 256, "ffn1_w_up": 256, "ffn1_w_down": 176, "w_in": 128, "w_gla_gate": 16, "conv_w": 4,
             "w_out": 64, "ffn2_w_gate": 256, "ffn2_w_up": 256, "ffn2_w_down": 176}


def kernel(x, ffn1_norm, ffn1_w_gate, ffn1_w_up, ffn1_w_down, mix_norm, w_in, w_gla_gate, b_gla_gate, conv_w, dn_a_log, dn_dt_bias, gla_head_norm, dn_head_norm, w_out, ffn2_norm, ffn2_w_gate, ffn2_w_up, ffn2_w_down, final_norm, loss_target, m_ffn1_norm, m_ffn1_w_gate, m_ffn1_w_up, m_ffn1_w_down, m_mix_norm, m_w_in, m_w_gla_gate, m_b_gla_gate, m_conv_w, m_dn_a_log, m_dn_dt_bias, m_gla_head_norm, m_dn_head_norm, m_w_out, m_ffn2_norm, m_ffn2_w_gate, m_ffn2_w_up, m_ffn2_w_down, m_final_norm, v_ffn1_norm, v_ffn1_w_gate, v_ffn1_w_up, v_ffn1_w_down, v_mix_norm, v_w_in, v_w_gla_gate, v_b_gla_gate, v_conv_w, v_dn_a_log, v_dn_dt_bias, v_gla_head_norm, v_dn_head_norm, v_w_out, v_ffn2_norm, v_ffn2_w_gate, v_ffn2_w_up, v_ffn2_w_down, v_final_norm):
    given = dict(locals())
    wts = {n: given[n] for n in WEIGHT_ORDER}
    moms = {n: given["m_" + n] for n in WEIGHT_ORDER}
    vars_ = {n: given["v_" + n] for n in WEIGHT_ORDER}
    two_d = lambda a: a.reshape(a.shape[-2], a.shape[-1]) if a.ndim == 3 else a.reshape(1, -1)
    shard = {n: two_d(wts[n]) for n in SHARDED}

    gathered = _gather_weights([shard[n].astype(BF16) for n in BIG], [shard[n] for n in TINY], "gather_weights")
    G = dict(zip(SHARDED, gathered))
    W = {n: G[n] for n in BIG if n.startswith("ffn")}
    W["w_out"] = G["w_out"].reshape(D, D)
    W["w_in_big"], W["w_in_small"] = _split_w_in(_join_cols(G["w_in"]))
    P = {n: two_d(wts[n]) for n in SMALL_NAMES}
    for n in TINY:
        P[n] = _join_cols(G[n])

    loss, grad_x, gw, gs = _local_step(x[0], loss_target[0], W, P)

    my_slot = 2 * lax.axis_index("x") + lax.axis_index("y")
    where = jnp.stack([lax.axis_index("c"), my_slot]).astype(jnp.int32)
    mine = [gw[n] for n in BIG]
    from_sibling = _rs_sibling(mine, "rs_sibling")
    pairs = [_add_pair(g, o, where, "rs_pair_" + n) for n, g, o in zip(BIG, mine, from_sibling)]
    from_chips = _rs_chips([p[0] for p in pairs], "rs_chips")
    halves = [_add_four(p[1], got, "rs_four_" + n) for n, p, got in zip(BIG, pairs, from_chips)]
    other_halves = _rs_swap(halves, "rs_swap")

    tiny_rows = jnp.concatenate([gs["w_gla_gate"].reshape(8, D), gs["conv_w"].reshape(12, D), jnp.zeros((4, D), F32)])
    all_sum = _allsum_small(jnp.concatenate([_pack_small(gs, loss[0, 0]), tiny_rows]), "allsum_small")
    small_sum = all_sum[:8]
    small_g, loss_total = _unpack_small(small_sum, P)

    grads, delta, new_m, new_v = {}, {}, {}, {}
    for n, own, got in zip(BIG, halves, other_halves):
        res = _adamw_halves(shard[n], own, got, two_d(moms[n]), two_d(vars_[n]), ADAM_ROWS[n], "adamw_" + n)
        grads[n], delta[n], new_m[n], new_v[n] = (t.reshape(wts[n].shape) for t in res)
    for n, rows in (("w_gla_gate", all_sum[8:16]), ("conv_w", all_sum[16:28])):
        cols = shard[n].shape[1]
        grads[n] = lax.dynamic_slice_in_dim(rows.reshape(shard[n].shape[0], N_SHARD * cols), my_slot * cols, cols, axis=1)
        d, m_, v_ = _adamw(shard[n], grads[n], two_d(moms[n]), two_d(vars_[n]), ADAM_ROWS[n], "adamw_" + n)
        delta[n], new_m[n], new_v[n] = (t.reshape(wts[n].shape) for t in (d, m_, v_))
    pk = lambda src: _pack_small({n: two_d(src[n]) for n in SMALL_NAMES})
    sd, sm_, sv_ = _adamw(pk(wts), small_sum, pk(moms), pk(vars_), 8, "adamw_small")
    for res, dst in ((sd, delta), (sm_, new_m), (sv_, new_v)):
        u, _ = _unpack_small(res, wts)
        dst.update(u)
    grad_w = {n: grads[n].reshape(wts[n].shape) for n in SHARDED}
    grad_w.update({n: small_g[n].reshape(wts[n].shape) for n in SMALL_NAMES})
    return (loss_total, grad_x[None], *[grad_w[n] for n in WEIGHT_ORDER], *[delta[n] for n in WEIGHT_ORDER],
            *[new_m[n] for n in WEIGHT_ORDER], *[new_v[n] for n in WEIGHT_ORDER])
```

```python
import functools
import math

import numpy as np
import jax
import jax.numpy as jnp
from jax import lax
from jax.experimental import pallas as pl
from jax.experimental.pallas import tpu as pltpu

F32 = jnp.float32
BF16 = jnp.bfloat16
HI = lax.Precision.HIGH
MESH = pl.DeviceIdType.MESH
ANY = pl.BlockSpec(memory_space=pl.ANY)

EPS = 1e-6
D = 1024
DFF = 2816
FFN_RES = 0.5
GLA_H, GLA_DK, GLA_DV, GLA_RANK, GLA_TAU = 4, 128, 256, 16, 16.0
DN_H, DN_DK, DN_DV = 8, 128, 128
CONV_K = 4
CHUNK = 64
N_SHARD = 4
FF_CUT = DFF // N_SHARD
ADAM_LR, ADAM_B1, ADAM_B2, ADAM_EPS, ADAM_WD, ADAM_STEP = 0.001, 0.9, 0.999, 1e-08, 0.01, 10

IN_SIZES = (512, 512, 1024, 1024, 16, 1024, 1024, 1024, 1024, 8, 8, 1024, 1024)
IN_OFF = tuple(int(v) for v in np.cumsum((0,) + IN_SIZES))
D_IN = IN_OFF[-1]
BIG_COLS = 9216
SMALL = 128
PIECES = (512, 512, 1024, 1024, 3072, 1024, 1024, 1024)

VMEM_LIMIT = 56 * 1024 * 1024
ROW_BLK = 256
ATT_BLK = 256
GDN_BLK = 128
GDN_HEADS = 8


def _cp(*sem):
    return pltpu.CompilerParams(dimension_semantics=sem, vmem_limit_bytes=VMEM_LIMIT)


def _sigmoid(x):
    return 1.0 / (1.0 + jnp.exp(-x))


def _softplus(x):
    return jnp.maximum(x, 0.0) + jnp.log(1.0 + jnp.exp(-jnp.abs(x)))


def _log_sigmoid(x):
    return jnp.minimum(x, 0.0) - jnp.log(1.0 + jnp.exp(-jnp.abs(x)))


def _dot(a, b, prec=None):
    return jnp.dot(a, b, preferred_element_type=F32, precision=prec)


def _dot_nt(a, b, prec=None):
    return lax.dot_general(a, b, (((1,), (1,)), ((), ())), preferred_element_type=F32, precision=prec)


def _dot_tn(a, b, prec=None):
    return lax.dot_general(a, b, (((0,), (0,)), ((), ())), preferred_element_type=F32, precision=prec)


def _b(x):
    return x.astype(BF16)


def _iota2(n, m, axis):
    return lax.broadcasted_iota(jnp.int32, (n, m), axis)


def _load_weights(pairs, sem):
    copies = [pltpu.make_async_copy(s, d, sem.at[i]) for i, (s, d) in enumerate(pairs)]
    for c in copies:
        c.start()
    for c in copies:
        c.wait()


def _ffn_fwd(h, nw, wg, wu, wd, name):
    T = h.shape[0]
    tm = min(ROW_BLK, T)

    def body(h_ref, nw_ref, wg_hbm, wu_hbm, wd_hbm, ho_ref, n_ref, g_ref, u_ref, wg_v, wu_v, wd_v, sem):
        @pl.when(pl.program_id(0) == 0)
        def _():
            _load_weights(((wg_hbm, wg_v), (wu_hbm, wu_v), (wd_hbm, wd_v)), sem)

        x = h_ref[...]
        r = lax.rsqrt(jnp.mean(x * x, axis=-1, keepdims=True) + EPS)
        nb = _b((x * r) * nw_ref[...])
        n_ref[...] = nb
        acc = jnp.zeros((tm, D), F32)
        for s in range(N_SHARD):
            g = _dot(nb, wg_v[s])
            u = _dot(nb, wu_v[s])
            g_ref[s] = _b(g)
            u_ref[s] = _b(u)
            acc += _dot(_b(g * _sigmoid(g) * u), wd_v[s])
        ho_ref[...] = x + FFN_RES * acc

    row = lambda w: pl.BlockSpec((tm, w), lambda i: (i, 0))
    cut = pl.BlockSpec((N_SHARD, tm, FF_CUT), lambda i: (0, i, 0))
    return pl.pallas_call(
        body, name=name, grid=(T // tm,),
        in_specs=[row(D), pl.BlockSpec((1, D), lambda i: (0, 0)), ANY, ANY, ANY],
        out_specs=[row(D), row(D), cut, cut],
        out_shape=[jax.ShapeDtypeStruct((T, D), F32), jax.ShapeDtypeStruct((T, D), BF16),
                   jax.ShapeDtypeStruct((N_SHARD, T, FF_CUT), BF16), jax.ShapeDtypeStruct((N_SHARD, T, FF_CUT), BF16)],
        scratch_shapes=[pltpu.VMEM((N_SHARD, D, FF_CUT), BF16), pltpu.VMEM((N_SHARD, D, FF_CUT), BF16),
                        pltpu.VMEM((N_SHARD, FF_CUT, D), BF16), pltpu.SemaphoreType.DMA((3,))],
        compiler_params=_cp("arbitrary"),
    )(h, nw, wg, wu, wd)


def _ffn_bwd(dh, h, nw, g, u, wg, wu, wd, name):
    T = h.shape[0]
    tm = min(ROW_BLK, T)

    def body(dh_ref, h_ref, nw_ref, g_ref, u_ref, wg_hbm, wu_hbm, wd_hbm,
             dx_ref, dg_ref, du_ref, a_ref, df_ref, dnw_ref, wg_v, wu_v, wd_v, sem):
        @pl.when(pl.program_id(0) == 0)
        def _():
            _load_weights(((wg_hbm, wg_v), (wu_hbm, wu_v), (wd_hbm, wd_v)), sem)
            dnw_ref[...] = jnp.zeros_like(dnw_ref)

        dh_ = dh_ref[...]
        dfb = _b(FFN_RES * dh_)
        df_ref[...] = dfb
        dn = jnp.zeros((tm, D), F32)
        for s in range(N_SHARD):
            da = _dot_nt(dfb, wd_v[s])
            gg = g_ref[s].astype(F32)
            uu = u_ref[s].astype(F32)
            sg = _sigmoid(gg)
            silu = gg * sg
            a_ref[s] = _b(silu * uu)
            dgb = _b(da * uu * (sg * (1.0 + gg * (1.0 - sg))))
            dub = _b(da * silu)
            dg_ref[s] = dgb
            du_ref[s] = dub
            dn += _dot_nt(dgb, wg_v[s]) + _dot_nt(dub, wu_v[s])
        x = h_ref[...]
        r = lax.rsqrt(jnp.mean(x * x, axis=-1, keepdims=True) + EPS)
        xhat = x * r
        dnw_ref[...] += jnp.sum(dn * xhat, axis=0, keepdims=True)
        dxhat = dn * nw_ref[...]
        dx_ref[...] = dh_ + r * (dxhat - xhat * jnp.mean(dxhat * xhat, axis=-1, keepdims=True))

    row = lambda w: pl.BlockSpec((tm, w), lambda i: (i, 0))
    one = pl.BlockSpec((1, D), lambda i: (0, 0))
    cut = pl.BlockSpec((N_SHARD, tm, FF_CUT), lambda i: (0, i, 0))
    cut_shape = jax.ShapeDtypeStruct((N_SHARD, T, FF_CUT), BF16)
    return pl.pallas_call(
        body, name=name, grid=(T // tm,),
        in_specs=[row(D), row(D), one, cut, cut, ANY, ANY, ANY],
        out_specs=[row(D), cut, cut, cut, row(D), one],
        out_shape=[jax.ShapeDtypeStruct((T, D), F32), cut_shape, cut_shape, cut_shape,
                   jax.ShapeDtypeStruct((T, D), BF16), jax.ShapeDtypeStruct((1, D), F32)],
        scratch_shapes=[pltpu.VMEM((N_SHARD, D, FF_CUT), BF16), pltpu.VMEM((N_SHARD, D, FF_CUT), BF16),
                        pltpu.VMEM((N_SHARD, FF_CUT, D), BF16), pltpu.SemaphoreType.DMA((3,))],
        compiler_params=_cp("arbitrary"),
    )(dh, h, nw, g, u, wg, wu, wd)


def _mm_tn(a, b, bm, bn, name, out_dtype=BF16, tk=2048):
    cuts = a.shape[0] if a.ndim == 3 else (b.shape[0] if b.ndim == 3 else None)
    T, M = a.shape[-2:]
    N = b.shape[-1]
    tk = min(tk, T)
    bm, bn = min(bm, M), min(bn, N)
    nk = T // tk

    def body(a_ref, b_ref, o_ref, acc_ref):
        k = pl.program_id(3)

        @pl.when(k == 0)
        def _():
            acc_ref[...] = jnp.zeros_like(acc_ref)

        av = a_ref[0] if a.ndim == 3 else a_ref[...]
        bv = b_ref[0] if b.ndim == 3 else b_ref[...]
        acc_ref[...] += _dot_tn(_b(av), _b(bv))

        @pl.when(k == nk - 1)
        def _():
            res = acc_ref[...].astype(out_dtype)
            if cuts is None:
                o_ref[...] = res
            else:
                o_ref[0] = res

    a_spec = (pl.BlockSpec((1, tk, bm), lambda s, i, j, k: (s, k, i)) if a.ndim == 3
              else pl.BlockSpec((tk, bm), lambda s, i, j, k: (k, i)))
    b_spec = (pl.BlockSpec((1, tk, bn), lambda s, i, j, k: (s, k, j)) if b.ndim == 3
              else pl.BlockSpec((tk, bn), lambda s, i, j, k: (k, j)))
    if cuts is None:
        o_spec, o_shape = pl.BlockSpec((bm, bn), lambda s, i, j, k: (i, j)), (M, N)
    else:
        o_spec, o_shape = pl.BlockSpec((1, bm, bn), lambda s, i, j, k: (s, i, j)), (cuts, M, N)
    return pl.pallas_call(
        body, name=name, grid=(cuts or 1, M // bm, N // bn, nk),
        in_specs=[a_spec, b_spec], out_specs=o_spec,
        out_shape=jax.ShapeDtypeStruct(o_shape, out_dtype),
        scratch_shapes=[pltpu.VMEM((bm, bn), F32)],
        compiler_params=_cp("parallel", "parallel", "parallel", "arbitrary"),
    )(a, b)


def _norm_proj(h, nw, wbig, wsmall, name):
    T = h.shape[0]
    tm = min(512, T)
    tn = 1536

    def body(h_ref, nw_ref, wb_hbm, ws_ref, pb_ref, ps_ref, n_ref, wb_v, sem):
        @pl.when(pl.program_id(0) == 0)
        def _():
            _load_weights(((wb_hbm, wb_v),), sem)

        x = h_ref[...]
        r = lax.rsqrt(jnp.mean(x * x, axis=-1, keepdims=True) + EPS)
        nb = _b((x * r) * nw_ref[...])
        n_ref[...] = nb
        ps_ref[...] = _dot(nb, ws_ref[...])
        for j in range(BIG_COLS // tn):
            pb_ref[:, j * tn:(j + 1) * tn] = _b(_dot(nb, wb_v[:, j * tn:(j + 1) * tn]))

    row = lambda w: pl.BlockSpec((tm, w), lambda i: (i, 0))
    return pl.pallas_call(
        body, name=name, grid=(T // tm,),
        in_specs=[row(D), pl.BlockSpec((1, D), lambda i: (0, 0)), ANY, pl.BlockSpec((D, SMALL), lambda i: (0, 0))],
        out_specs=[row(BIG_COLS), row(SMALL), row(D)],
        out_shape=[jax.ShapeDtypeStruct((T, BIG_COLS), BF16), jax.ShapeDtypeStruct((T, SMALL), F32),
                   jax.ShapeDtypeStruct((T, D), BF16)],
        scratch_shapes=[pltpu.VMEM((D, BIG_COLS), BF16), pltpu.SemaphoreType.DMA((1,))],
        compiler_params=_cp("arbitrary"),
    )(h, nw, wbig, wsmall)


def _proj_bwd(dh, h, nw, pieces, dsmall, wbig, wsmall, name):
    T = h.shape[0]
    tm = min(ROW_BLK, T)
    offs = tuple(int(v) for v in np.cumsum((0,) + PIECES))

    def body(dh_ref, h_ref, nw_ref, *rest):
        p_refs = rest[:len(PIECES)]
        ds_ref, wb_hbm, ws_ref, dx_ref, dnw_ref, wb_v, sem = rest[len(PIECES):]

        @pl.when(pl.program_id(0) == 0)
        def _():
            _load_weights(((wb_hbm, wb_v),), sem)
            dnw_ref[...] = jnp.zeros_like(dnw_ref)

        dn = _dot_nt(_b(ds_ref[...]), ws_ref[...])
        for p_ref, lo, wdt in zip(p_refs, offs, PIECES):
            dn += _dot_nt(p_ref[...], wb_v[:, lo:lo + wdt])
        x = h_ref[...]
        r = lax.rsqrt(jnp.mean(x * x, axis=-1, keepdims=True) + EPS)
        xhat = x * r
        dnw_ref[...] += jnp.sum(dn * xhat, axis=0, keepdims=True)
        dxhat = dn * nw_ref[...]
        dx_ref[...] = dh_ref[...] + r * (dxhat - xhat * jnp.mean(dxhat * xhat, axis=-1, keepdims=True))

    row = lambda w: pl.BlockSpec((tm, w), lambda i: (i, 0))
    one = pl.BlockSpec((1, D), lambda i: (0, 0))
    return pl.pallas_call(
        body, name=name, grid=(T // tm,),
        in_specs=[row(D), row(D), one] + [row(w) for w in PIECES] + [row(SMALL), ANY, pl.BlockSpec((D, SMALL), lambda i: (0, 0))],
        out_specs=[row(D), one],
        out_shape=[jax.ShapeDtypeStruct((T, D), F32), jax.ShapeDtypeStruct((1, D), F32)],
        scratch_shapes=[pltpu.VMEM((D, BIG_COLS), BF16), pltpu.SemaphoreType.DMA((1,))],
        compiler_params=_cp("arbitrary"),
    )(dh, h, nw, *pieces, dsmall, wbig, wsmall)


def _gla_block(q_ref, k_ref, sm_ref, wg_ref, bg_ref, nc, tril):
    nbat = GLA_H * nc
    q = _heads_first(q_ref[...].astype(F32), nc, GLA_DK)
    k = _heads_first(k_ref[...].astype(F32), nc, GLA_DK)
    pre = _heads_first(_dot(sm_ref[...], wg_ref[...], HI) + bg_ref[...], nc, GLA_DK)
    la = _log_sigmoid(pre) * (1.0 / GLA_TAU)
    bc = _bmm(jnp.broadcast_to(tril, (nbat, CHUNK, CHUNK)), la, HI)
    bl = bc[:, CHUNK - 1:CHUNK, :]
    eb = jnp.exp(bc)
    enb = jnp.exp(-bc)
    ebl = jnp.exp(bl - bc)
    q_in = q * (GLA_DK ** -0.5) * eb
    k_out = k * enb
    k_st = k * ebl
    a_ch = jnp.exp(bl)
    return pre, eb, enb, ebl, q_in, k_out, k_st, a_ch


def _gla_specs(blk, idx):
    hk, hv = GLA_H * GLA_DK, GLA_H * GLA_DV
    return [pl.BlockSpec((blk, hk), lambda j: (idx(j), 0)),
            pl.BlockSpec((blk, hk), lambda j: (idx(j), 1)),
            pl.BlockSpec((blk, hv), lambda j: (idx(j), 1)),
            pl.BlockSpec((blk, SMALL), lambda j: (idx(j), 0)),
            pl.BlockSpec((SMALL, hk), lambda j: (0, 0)),
            pl.BlockSpec((1, hk), lambda j: (0, 0))]


def _gla_fwd(pbig, psmall, wgate, bgate, name):
    T = pbig.shape[0]
    blk = min(ATT_BLK, T)
    nc = blk // CHUNK

    def body(q_ref, k_ref, v_ref, sm_ref, wg_ref, bg_ref, o_ref, ss_ref, st_ref):
        @pl.when(pl.program_id(0) == 0)
        def _():
            st_ref[...] = jnp.zeros_like(st_ref)

        causal = _iota2(CHUNK, CHUNK, 0) >= _iota2(CHUNK, CHUNK, 1)
        _, _, _, _, q_in, k_out, k_st, a_ch = _gla_block(q_ref, k_ref, sm_ref, wg_ref, bg_ref, nc, causal.astype(F32))
        v = _heads_first(v_ref[...], nc, GLA_DV)
        qb = _b(q_in)
        sc = jnp.where(causal, _bmm_nt(qb, _b(k_out)), 0.0)
        kv = _bmm_tn(v, _b(k_st))
        before = [None] * (GLA_H * nc)
        for i in range(GLA_H):
            st = st_ref[i]
            for c in range(nc):
                n = i * nc + c
                before[n] = st
                st = st * a_ch[n] + kv[n]
            st_ref[i] = st
        states = jnp.stack(before)
        ss_ref[...] = states.reshape(GLA_H, nc, GLA_DV, GLA_DK)
        o_ref[...] = _heads_last(_bmm(_b(sc), v) + _bmm_nt(qb, _b(states)), nc)

    return pl.pallas_call(
        body, name=name, grid=(T // blk,),
        in_specs=_gla_specs(blk, lambda j: j),
        out_specs=[pl.BlockSpec((blk, GLA_H * GLA_DV), lambda j: (j, 0)),
                   pl.BlockSpec((GLA_H, nc, GLA_DV, GLA_DK), lambda j: (0, j, 0, 0))],
        out_shape=[jax.ShapeDtypeStruct((T, GLA_H * GLA_DV), F32),
                   jax.ShapeDtypeStruct((GLA_H, T // CHUNK, GLA_DV, GLA_DK), F32)],
        scratch_shapes=[pltpu.VMEM((GLA_H, GLA_DV, GLA_DK), F32)],
        compiler_params=_cp("arbitrary"),
    )(pbig, pbig, pbig, psmall, wgate, bgate)


def _gla_bwd(pbig, psmall, wgate, bgate, states, do, name):
    T = pbig.shape[0]
    blk = min(ATT_BLK, T)
    nc = blk // CHUNK
    nb = T // blk
    nbat = GLA_H * nc

    def body(q_ref, k_ref, v_ref, sm_ref, wg_ref, bg_ref, ss_ref, do_ref, dq_ref, dk_ref, dv_ref, dpre_ref, dst_ref):
        @pl.when(pl.program_id(0) == 0)
        def _():
            dst_ref[...] = jnp.zeros_like(dst_ref)

        causal = _iota2(CHUNK, CHUNK, 0) >= _iota2(CHUNK, CHUNK, 1)
        triu = (_iota2(CHUNK, CHUNK, 0) <= _iota2(CHUNK, CHUNK, 1)).astype(F32)
        pre, eb, enb, ebl, q_in, k_out, k_st, a_ch = _gla_block(q_ref, k_ref, sm_ref, wg_ref, bg_ref, nc,
                                                                causal.astype(F32))
        v = _heads_first(v_ref[...], nc, GLA_DV)
        dob = _b(_heads_first(do_ref[...], nc, GLA_DV))
        st = ss_ref[...].reshape(nbat, GLA_DV, GLA_DK)
        qb, kob, kstb = _b(q_in), _b(k_out), _b(k_st)
        qdo = _bmm_tn(dob, qb)
        after = [None] * nbat
        for i in range(GLA_H):
            dst = dst_ref[i]
            for c in range(nc - 1, -1, -1):
                n = i * nc + c
                after[n] = dst
                dst = dst * a_ch[n] + qdo[n]
            dst_ref[i] = dst
        dsa = jnp.stack(after)
        dsb = _b(dsa)
        sc = jnp.where(causal, _bmm_nt(qb, kob), 0.0)
        dsc = _b(jnp.where(causal, _bmm_nt(dob, v), 0.0))
        dq_in = _bmm(dob, _b(st)) + _bmm(dsc, kob)
        dk_out = _bmm_tn(dsc, qb)
        dk_st = _bmm(v, dsb)
        dv_ref[...] = _b(_heads_last(_bmm_tn(_b(sc), dob) + _bmm_nt(kstb, dsb), nc))
        da_ch = jnp.sum(st * dsa, axis=1, keepdims=True)
        tk = dk_st * k_st
        db = dq_in * q_in - dk_out * k_out - tk
        db_last = jnp.sum(tk, axis=1, keepdims=True) + da_ch * a_ch
        dq_ref[...] = _b(_heads_last(dq_in * (GLA_DK ** -0.5) * eb, nc))
        dk_ref[...] = _b(_heads_last(dk_out * enb + dk_st * ebl, nc))
        dla = _bmm(jnp.broadcast_to(triu, (nbat, CHUNK, CHUNK)), db, HI) + db_last
        dpre_ref[...] = _heads_last(dla * (1.0 / GLA_TAU) * _sigmoid(-pre), nc)

    r = lambda j: nb - 1 - j
    hk, hv = GLA_H * GLA_DK, GLA_H * GLA_DV
    return pl.pallas_call(
        body, name=name, grid=(nb,),
        in_specs=_gla_specs(blk, r) + [pl.BlockSpec((GLA_H, nc, GLA_DV, GLA_DK), lambda j: (0, r(j), 0, 0)),
                                      pl.BlockSpec((blk, hv), lambda j: (r(j), 0))],
        out_specs=[pl.BlockSpec((blk, hk), lambda j: (r(j), 0)), pl.BlockSpec((blk, hk), lambda j: (r(j), 0)),
                   pl.BlockSpec((blk, hv), lambda j: (r(j), 0)), pl.BlockSpec((blk, hk), lambda j: (r(j), 0))],
        out_shape=[jax.ShapeDtypeStruct((T, hk), BF16), jax.ShapeDtypeStruct((T, hk), BF16),
                   jax.ShapeDtypeStruct((T, hv), BF16), jax.ShapeDtypeStruct((T, hk), F32)],
        scratch_shapes=[pltpu.VMEM((GLA_H, GLA_DV, GLA_DK), F32)],
        compiler_params=_cp("arbitrary"),
    )(pbig, pbig, pbig, psmall, wgate, bgate, states, do)


def _gla_gate_bwd(dpre, psmall, wgate, dsm, name):
    T = dpre.shape[0]
    tm = min(512, T)
    W = GLA_H * GLA_DK
    ngrp = dsm.shape[0]

    def body(dp_ref, sm_ref, wg_ref, dsm_ref, ds_ref, dw_ref, db_ref):
        @pl.when(pl.program_id(0) == 0)
        def _():
            dw_ref[...] = jnp.zeros_like(dw_ref)
            db_ref[...] = jnp.zeros_like(db_ref)

        dp = dp_ref[...]
        ds = _dot_nt(dp, wg_ref[...], HI)
        for i in range(ngrp):
            ds += dsm_ref[i]
        ds_ref[...] = ds
        dw_ref[...] += _dot_tn(sm_ref[...], dp, HI)
        db_ref[...] += jnp.sum(dp, axis=0, keepdims=True)

    return pl.pallas_call(
        body, name=name, grid=(T // tm,),
        in_specs=[pl.BlockSpec((tm, W), lambda i: (i, 0)), pl.BlockSpec((tm, SMALL), lambda i: (i, 0)),
                  pl.BlockSpec((SMALL, W), lambda i: (0, 0)), pl.BlockSpec((ngrp, tm, SMALL), lambda i: (0, i, 0))],
        out_specs=[pl.BlockSpec((tm, SMALL), lambda i: (i, 0)), pl.BlockSpec((SMALL, W), lambda i: (0, 0)),
                   pl.BlockSpec((1, W), lambda i: (0, 0))],
        out_shape=[jax.ShapeDtypeStruct((T, SMALL), F32), jax.ShapeDtypeStruct((SMALL, W), F32),
                   jax.ShapeDtypeStruct((1, W), F32)],
        compiler_params=_cp("arbitrary"),
    )(dpre, psmall, wgate, dsm)


CONV_C = 3 * 1024
CONV_BLK = 256


def _conv_fwd(pbig, cw8, name):
    T = pbig.shape[0]
    blk = min(CONV_BLK, T)

    def body(x_ref, w_ref, c_ref, prev_ref):
        @pl.when(pl.program_id(0) == 0)
        def _():
            prev_ref[...] = jnp.zeros_like(prev_ref)

        x = x_ref[...].astype(F32)
        prev = prev_ref[...]
        row8 = _iota2(8, CONV_C, 0)
        acc = x * w_ref[CONV_K - 1:CONV_K, :]
        for s in range(1, CONV_K):
            xs = pltpu.roll(x, s, 0)
            top = jnp.where(row8 < s, pltpu.roll(prev, s, 0), xs[:8])
            xs = jnp.concatenate([top, xs[8:]], axis=0)
            acc += xs * w_ref[CONV_K - 1 - s:CONV_K - s, :]
        c_ref[...] = _b(acc)
        prev_ref[...] = x[blk - 8:]

    return pl.pallas_call(
        body, name=name, grid=(T // blk,),
        in_specs=[pl.BlockSpec((blk, CONV_C), lambda i: (i, 1)), pl.BlockSpec((8, CONV_C), lambda i: (0, 0))],
        out_specs=pl.BlockSpec((blk, CONV_C), lambda i: (i, 0)),
        out_shape=jax.ShapeDtypeStruct((T, CONV_C), BF16),
        scratch_shapes=[pltpu.VMEM((8, CONV_C), F32)],
        compiler_params=_cp("arbitrary"),
    )(pbig, cw8)


def _conv_bwd(dcq, dck, dcv, pbig, cw8, name):
    T = pbig.shape[0]
    blk = min(CONV_BLK, T)
    nb = T // blk

    def body(dq_ref, dk_ref, dv_ref, x_ref, w_ref, dx_ref, dw_ref, nxt_ref):
        @pl.when(pl.program_id(0) == 0)
        def _():
            nxt_ref[...] = jnp.zeros_like(nxt_ref)
            dw_ref[...] = jnp.zeros_like(dw_ref)

        dc = jnp.concatenate([dq_ref[...], dk_ref[...], dv_ref[...]], axis=1).astype(F32)
        x = x_ref[...].astype(F32)
        nxt = nxt_ref[...]
        row8 = _iota2(8, CONV_C, 0)
        acc = dc * w_ref[CONV_K - 1:CONV_K, :]
        dws = [jnp.sum(dc * x, axis=0, keepdims=True)]
        for s in range(1, CONV_K):
            ds = pltpu.roll(dc, blk - s, 0)
            bot = jnp.where(row8 >= 8 - s, pltpu.roll(nxt, 8 - s, 0), ds[blk - 8:])
            ds = jnp.concatenate([ds[:blk - 8], bot], axis=0)
            acc += ds * w_ref[CONV_K - 1 - s:CONV_K - s, :]
            dws.append(jnp.sum(ds * x, axis=0, keepdims=True))
        dx_ref[...] = _b(acc)
        dw_ref[...] += jnp.concatenate(dws[::-1] + [jnp.zeros((8 - CONV_K, CONV_C), F32)], axis=0)
        nxt_ref[...] = dc[:8]

    part = pl.BlockSpec((blk, 1024), lambda i: (nb - 1 - i, 0))
    return pl.pallas_call(
        body, name=name, grid=(nb,),
        in_specs=[part, part, part, pl.BlockSpec((blk, CONV_C), lambda i: (nb - 1 - i, 1)),
                  pl.BlockSpec((8, CONV_C), lambda i: (0, 0))],
        out_specs=[pl.BlockSpec((blk, CONV_C), lambda i: (nb - 1 - i, 0)), pl.BlockSpec((8, CONV_C), lambda i: (0, 0))],
        out_shape=[jax.ShapeDtypeStruct((T, CONV_C), BF16), jax.ShapeDtypeStruct((8, CONV_C), F32)],
        scratch_shapes=[pltpu.VMEM((8, CONV_C), F32)],
        compiler_params=_cp("arbitrary"),
    )(dcq, dck, dcv, pbig, cw8)


def _col(x, lane):
    sel = _iota2(x.shape[0], x.shape[1], 1) == lane
    return jnp.broadcast_to(jnp.sum(jnp.where(sel, x, 0.0), axis=1, keepdims=True), x.shape)


def _bmm(a, b, prec=None):
    return jnp.einsum("bij,bjk->bik", a, b, preferred_element_type=F32, precision=prec)


def _bmm_nt(a, b, prec=None):
    return jnp.einsum("bij,bkj->bik", a, b, preferred_element_type=F32, precision=prec)


def _bmm_tn(a, b, prec=None):
    return jnp.einsum("bji,bjk->bik", a, b, preferred_element_type=F32, precision=prec)


def _unit_lower_inverse(low):
    eye = (_iota2(CHUNK, CHUNK, 0) == _iota2(CHUNK, CHUNK, 1)).astype(F32)
    xk = -low
    inv = eye + xk
    for _ in range(5):
        xb = _b(xk)
        xk = _bmm(xb, xb)
        inv = inv + _bmm(_b(inv), _b(xk))
    resid = eye - _bmm(eye + low, inv, HI)
    return inv + _bmm(inv, resid, HI)


def _heads_first(x, nc, w=128):
    hb = x.shape[1] // w
    return jnp.concatenate([x[:, i * w:(i + 1) * w].reshape(nc, CHUNK, w) for i in range(hb)], axis=0)


def _heads_last(x, nc):
    hb = x.shape[0] // nc
    return jnp.concatenate([x[i * nc:(i + 1) * nc].reshape(nc * CHUNK, x.shape[2]) for i in range(hb)], axis=1)


def _gdn_block(cq_ref, ck_ref, cv_ref, sm_ref, par_ref, h0, hb, nc, masks):
    causal, strict, tril, eye = masks
    nbat = hb * nc
    cq = _heads_first(cq_ref[...].astype(F32), nc)
    ck = _heads_first(ck_ref[...].astype(F32), nc)
    cv = _heads_first(cv_ref[...].astype(F32), nc)
    sq, sk, sv = _sigmoid(cq), _sigmoid(ck), _sigmoid(cv)
    q, k, v = cq * sq, ck * sk, cv * sv
    rq = lax.rsqrt(jnp.sum(q * q, axis=-1, keepdims=True) + EPS)
    rk = lax.rsqrt(jnp.sum(k * k, axis=-1, keepdims=True) + EPS)
    qh, kn = q * rq, k * rk
    qn = qh * (DN_DK ** -0.5)
    sm = sm_ref[...]
    per_head = lambda fn: jnp.concatenate([fn(i) for i in range(hb)], axis=0)
    braw = per_head(lambda i: _col(sm, GLA_RANK + h0 + i).reshape(nc, CHUNK, 128))
    araw = per_head(lambda i: _col(sm, GLA_RANK + DN_H + h0 + i).reshape(nc, CHUNK, 128))
    ea = per_head(lambda i: jnp.broadcast_to(jnp.exp(par_ref[i, 0:1, :])[None], (nc, 1, 128)))
    bias = per_head(lambda i: jnp.broadcast_to(par_ref[i, 1:2, :][None], (nc, 1, 128)))
    beta = _sigmoid(braw)
    sp_arg = araw + bias
    g = -ea * _softplus(sp_arg)
    G = _bmm(jnp.broadcast_to(tril, (nbat, CHUNK, CHUNK)), g, HI)
    gc = G[:, :, :CHUNK]
    grow = jnp.sum(eye * gc, axis=1, keepdims=True)
    decay = jnp.exp(jnp.where(causal, gc - grow, -1e30))
    kb = kn * beta
    A = _bmm_nt(_b(kb), _b(kn))
    low = jnp.where(strict, A * decay, 0.0)
    tinv = _unit_lower_inverse(low)
    eG = jnp.exp(G)
    gl = G[:, CHUNK - 1:CHUNK, :]
    eGl = jnp.exp(gl - G)
    g_ch = jnp.exp(gl)
    rv = v * beta
    rkk = kb * eG
    tinv_b = _b(tinv)
    u = _bmm(tinv_b, _b(rv))
    w = _bmm(tinv_b, _b(rkk))
    B = _bmm_nt(_b(qn), _b(kn))
    qk = jnp.where(causal, B * decay, 0.0)
    q_dec = qn * eG
    k_st = kn * eGl
    return dict(cq=cq, ck=ck, cv=cv, sq=sq, sk=sk, sv=sv, q=q, k=k, v=v, rq=rq, rk=rk, qh=qh, kn=kn, qn=qn,
                beta=beta, ea=ea, sp_arg=sp_arg, g=g, G=G, decay=decay, kb=kb, A=A, tinv_b=tinv_b, eG=eG, eGl=eGl,
                g_ch=g_ch, rv=rv, rkk=rkk, u=u, w=w, B=B, qk=qk, q_dec=q_dec, k_st=k_st)


def _gdn_masks():
    r, c = _iota2(CHUNK, CHUNK, 0), _iota2(CHUNK, CHUNK, 1)
    return r >= c, r > c, (r >= c).astype(F32), (r == c).astype(F32)


def _gdn_specs(blk, hb, idx):
    ng = DN_H // hb
    return [pl.BlockSpec((blk, hb * DN_DK), lambda h, j: (idx(j), h)),
            pl.BlockSpec((blk, hb * DN_DK), lambda h, j: (idx(j), ng + h)),
            pl.BlockSpec((blk, hb * DN_DV), lambda h, j: (idx(j), 2 * ng + h)),
            pl.BlockSpec((blk, SMALL), lambda h, j: (idx(j), 0)),
            pl.BlockSpec((hb, 8, 128), lambda h, j: (h, 0, 0))]


def _gdn_fwd(conv, psmall, par, name):
    T = conv.shape[0]
    blk = min(GDN_BLK, T)
    nc = blk // CHUNK
    hb = GDN_HEADS

    def body(cq_ref, ck_ref, cv_ref, sm_ref, par_ref, o_ref, ss_ref, s_ref):
        @pl.when(pl.program_id(1) == 0)
        def _():
            s_ref[...] = jnp.zeros_like(s_ref)

        f = _gdn_block(cq_ref, ck_ref, cv_ref, sm_ref, par_ref, pl.program_id(0) * hb, hb, nc, _gdn_masks())
        wb, qdb, kstb, qkb = _b(f["w"]), _b(f["q_dec"]), _b(f["k_st"]), _b(f["qk"])
        S = [s_ref[i] for i in range(hb)]
        for c in range(nc):
            for i in range(hb):
                n = i * nc + c
                ss_ref[i, c] = S[i]
                Sb = _b(S[i])
                v_new = _b(f["u"][n] - _dot(wb[n], Sb))
                o_ref[pl.ds(c * CHUNK, CHUNK), i * DN_DV:(i + 1) * DN_DV] = _dot(qdb[n], Sb) + _dot(qkb[n], v_new)
                S[i] = S[i] * f["g_ch"][n] + _dot_tn(kstb[n], v_new)
        for i in range(hb):
            s_ref[i] = S[i]

    return pl.pallas_call(
        body, name=name, grid=(DN_H // hb, T // blk),
        in_specs=_gdn_specs(blk, hb, lambda j: j),
        out_specs=[pl.BlockSpec((blk, hb * DN_DV), lambda h, j: (j, h)),
                   pl.BlockSpec((hb, nc, DN_DK, DN_DV), lambda h, j: (h, j, 0, 0))],
        out_shape=[jax.ShapeDtypeStruct((T, DN_H * DN_DV), F32),
                   jax.ShapeDtypeStruct((DN_H, T // CHUNK, DN_DK, DN_DV), F32)],
        scratch_shapes=[pltpu.VMEM((hb, DN_DK, DN_DV), F32)],
        compiler_params=_cp("parallel", "arbitrary"),
    )(conv, conv, conv, psmall, par)


def _gdn_bwd(conv, psmall, par, states, do, name):
    T = conv.shape[0]
    blk = min(GDN_BLK, T)
    nc = blk // CHUNK
    nb = T // blk
    hb = GDN_HEADS
    nbat = hb * nc
    rsum = lambda x: jnp.sum(x, axis=-1, keepdims=True)

    def body(cq_ref, ck_ref, cv_ref, sm_ref, par_ref, ss_ref, do_ref,
             dcq_ref, dck_ref, dcv_ref, dsm_ref, dpar_ref, ds_ref):
        @pl.when(pl.program_id(1) == 0)
        def _():
            ds_ref[...] = jnp.zeros_like(ds_ref)
            dpar_ref[...] = jnp.zeros_like(dpar_ref)

        masks = _gdn_masks()
        causal, strict, tril, eye = masks
        triu = (_iota2(CHUNK, CHUNK, 0) <= _iota2(CHUNK, CHUNK, 1)).astype(F32)
        lane = _iota2(CHUNK, 128, 1)
        last_row = _iota2(CHUNK, 128, 0) == CHUNK - 1
        h0 = pl.program_id(0) * hb
        f = _gdn_block(cq_ref, ck_ref, cv_ref, sm_ref, par_ref, h0, hb, nc, masks)
        S = ss_ref[...].reshape(nbat, DN_DK, DN_DV)
        Sb = _b(S)
        do_ = _b(_heads_first(do_ref[...], nc))
        wb, qdb, kstb, qkb = _b(f["w"]), _b(f["q_dec"]), _b(f["k_st"]), _b(f["qk"])
        vnb = _b(f["u"] - _bmm(wb, Sb))
        dvn0 = _bmm_tn(qkb, do_)
        qdo = _bmm_tn(qdb, do_)
        dS = [ds_ref[i] for i in range(hb)]
        after = [None] * nbat
        for c in range(nc - 1, -1, -1):
            for i in range(hb):
                n = i * nc + c
                after[n] = dS[i]
                dvn_c = _b(dvn0[n] + _dot(kstb[n], _b(dS[i])))
                dS[i] = dS[i] * f["g_ch"][n] + qdo[n] - _dot_tn(wb[n], dvn_c)
        for i in range(hb):
            ds_ref[i] = dS[i]
        dSa = jnp.stack(after)
        dSb = _b(dSa)
        dvn = dvn0 + _bmm(kstb, dSb)
        dvnb = _b(dvn)
        dq_dec = _bmm_nt(do_, Sb)
        dqk = jnp.where(causal, _bmm_nt(do_, vnb), 0.0)
        dk_st = _bmm_nt(vnb, dSb)
        dg_ch = jnp.sum(rsum(S * dSa), axis=1, keepdims=True)
        dw = -_bmm_nt(dvnb, Sb)
        drv = _bmm_tn(f["tinv_b"], dvnb)
        drk = _bmm_tn(f["tinv_b"], _b(dw))
        dlow = jnp.where(strict, -(_bmm_nt(_b(drv), _b(f["u"])) + _bmm_nt(_b(drk), wb)), 0.0)
        dv = drv * f["beta"]
        dbeta = rsum(drv * f["v"])
        dkb = drk * f["eG"]
        dG = rsum(drk * f["rkk"])
        dA = dlow * f["decay"]
        ddec = dlow * f["A"]
        dkb += _bmm(_b(dA), _b(f["kn"]))
        dkn = _bmm_tn(_b(dA), _b(f["kb"]))
        dB = dqk * f["decay"]
        ddec += dqk * f["B"]
        dqn = _bmm(_b(dB), _b(f["kn"]))
        dkn += _bmm_tn(_b(dB), _b(f["qn"]))
        dD = ddec * f["decay"]
        dG += rsum(dD) - rsum(eye * jnp.sum(dD, axis=1, keepdims=True))
        dqn += dq_dec * f["eG"]
        dG += rsum(dq_dec * f["q_dec"])
        dkn += dk_st * f["eGl"]
        tks = rsum(dk_st * f["k_st"])
        dG -= tks
        dG_last = jnp.sum(tks, axis=1, keepdims=True) + dg_ch * f["g_ch"][:, :, :1]
        dkn += dkb * f["beta"]
        dbeta += rsum(dkb * f["kn"])
        dGf = jnp.broadcast_to(dG, (nbat, CHUNK, 128)) + jnp.where(last_row, dG_last, 0.0)
        dg = _bmm(jnp.broadcast_to(triu, (nbat, CHUNK, CHUNK)), dGf, HI)
        dbraw = dbeta * f["beta"][:, :, :1] * (1.0 - f["beta"][:, :, :1])
        daraw = dg * (-f["ea"]) * _sigmoid(f["sp_arg"])
        both = lambda t: jnp.sum(jnp.sum(t, axis=1, keepdims=True), axis=0)
        dgg = dg * f["g"]
        dsm = jnp.zeros((nc, CHUNK, SMALL), F32)
        for i in range(hb):
            mine = slice(i * nc, (i + 1) * nc)
            dsm += (jnp.where(lane == GLA_RANK + h0 + i, dbraw[mine], 0.0)
                    + jnp.where(lane == GLA_RANK + DN_H + h0 + i, daraw[mine], 0.0))
            dpar = jnp.where(lane[:1] == 0, both(dgg[mine]), jnp.where(lane[:1] == 1, both(daraw[mine]), 0.0))
            dpar_ref[i] += jnp.broadcast_to(dpar, (8, 128))
        dsm_ref[0] = dsm.reshape(blk, SMALL)
        dqh = dqn * (DN_DK ** -0.5)
        dq = f["rq"] * (dqh - f["qh"] * rsum(dqh * f["qh"]))
        dk = f["rk"] * (dkn - f["kn"] * rsum(dkn * f["kn"]))
        dsilu = lambda x, s: s * (1.0 + x * (1.0 - s))
        dcq_ref[...] = _b(_heads_last(dq * dsilu(f["cq"], f["sq"]), nc))
        dck_ref[...] = _b(_heads_last(dk * dsilu(f["ck"], f["sk"]), nc))
        dcv_ref[...] = _b(_heads_last(dv * dsilu(f["cv"], f["sv"]), nc))

    r = lambda j: nb - 1 - j
    out_blk = pl.BlockSpec((blk, hb * DN_DK), lambda h, j: (r(j), h))
    return pl.pallas_call(
        body, name=name, grid=(DN_H // hb, nb),
        in_specs=_gdn_specs(blk, hb, r) + [pl.BlockSpec((hb, nc, DN_DK, DN_DV), lambda h, j: (h, r(j), 0, 0)),
                                          pl.BlockSpec((blk, hb * DN_DV), lambda h, j: (r(j), h))],
        out_specs=[out_blk, out_blk, out_blk, pl.BlockSpec((1, blk, SMALL), lambda h, j: (h, r(j), 0)),
                   pl.BlockSpec((hb, 8, 128), lambda h, j: (h, 0, 0))],
        out_shape=[jax.ShapeDtypeStruct((T, DN_H * DN_DK), BF16)] * 3 + [
            jax.ShapeDtypeStruct((DN_H // hb, T, SMALL), F32), jax.ShapeDtypeStruct((DN_H, 8, 128), F32)],
        scratch_shapes=[pltpu.VMEM((hb, DN_DK, DN_DV), F32)],
        compiler_params=_cp("parallel", "arbitrary"),
    )(conv, conv, conv, psmall, par, states, do)


def _head_norm(o, w, dv):
    outs, rs = [], []
    for i in range(o.shape[1] // dv):
        oh = o[:, i * dv:(i + 1) * dv]
        r = lax.rsqrt(jnp.mean(oh * oh, axis=-1, keepdims=True) + EPS)
        outs.append(oh * r)
        rs.append(r)
    return outs, rs


def _merge_specs(tm):
    col = lambda c: pl.BlockSpec((tm, D), lambda i: (i, c))
    return [col(0), col(0), col(2), col(6), col(7), col(8),
            pl.BlockSpec((1, GLA_DV), lambda i: (0, 0)), pl.BlockSpec((1, DN_DV), lambda i: (0, 0)),
            pl.BlockSpec((D, D), lambda i: (0, 0))]


def _merge_fwd(h, oa, ob, pbig, gla_hn, dn_hn, wout, name):
    T = h.shape[0]
    tm = min(ROW_BLK, T)

    def body(h_ref, oa_ref, ob_ref, gr_ref, dg_ref, ma_ref, mb_ref, wa_ref, wb_ref, wo_ref, ho_ref, y_ref):
        na, _ = _head_norm(oa_ref[...], wa_ref[...], GLA_DV)
        nbs, _ = _head_norm(ob_ref[...], wb_ref[...], DN_DV)
        hna = jnp.concatenate([t * wa_ref[...] for t in na], axis=1)
        hnb = jnp.concatenate([t * wb_ref[...] for t in nbs], axis=1)
        gr = gr_ref[...].astype(F32)
        dg = dg_ref[...].astype(F32)
        y = (_sigmoid(ma_ref[...].astype(F32)) * hna * (gr * _sigmoid(gr))
             + _sigmoid(mb_ref[...].astype(F32)) * hnb * (dg * _sigmoid(dg)))
        yb = _b(y)
        y_ref[...] = yb
        ho_ref[...] = h_ref[...] + _dot(yb, wo_ref[...])

    row = pl.BlockSpec((tm, D), lambda i: (i, 0))
    return pl.pallas_call(
        body, name=name, grid=(T // tm,),
        in_specs=[row] + _merge_specs(tm),
        out_specs=[row, row],
        out_shape=[jax.ShapeDtypeStruct((T, D), F32), jax.ShapeDtypeStruct((T, D), BF16)],
        compiler_params=_cp("arbitrary"),
    )(h, oa, ob, pbig, pbig, pbig, pbig, gla_hn, dn_hn, wout)


def _merge_bwd(dh, oa, ob, pbig, gla_hn, dn_hn, wout, name):
    T = dh.shape[0]
    tm = min(ROW_BLK, T)

    def branch(dy, o_ref, w_ref, gate_ref, m_ref, dv):
        w = w_ref[...]
        ohat, rs = _head_norm(o_ref[...], w, dv)
        gate = gate_ref[...].astype(F32)
        m = m_ref[...].astype(F32)
        sgate, sm = _sigmoid(gate), _sigmoid(m)
        silu = gate * sgate
        ohat_all = jnp.concatenate(ohat, axis=1)
        hn = jnp.concatenate([t * w for t in ohat], axis=1)
        d_on = dy * sm
        d_m = dy * hn * silu * sm * (1.0 - sm)
        d_hn = d_on * silu
        d_gate = d_on * hn * (sgate * (1.0 + gate * (1.0 - sgate)))
        dw = jnp.zeros((1, dv), F32)
        d_o = []
        for i, (oh, r) in enumerate(zip(ohat, rs)):
            dhn = d_hn[:, i * dv:(i + 1) * dv]
            dw += jnp.sum(dhn * oh, axis=0, keepdims=True)
            dohat = dhn * w
            d_o.append(r * (dohat - oh * jnp.mean(dohat * oh, axis=-1, keepdims=True)))
        return jnp.concatenate(d_o, axis=1), d_gate, d_m, dw

    def body(dh_ref, oa_ref, ob_ref, gr_ref, dg_ref, ma_ref, mb_ref, wa_ref, wb_ref, wo_ref,
             doa_ref, dob_ref, dgr_ref, ddg_ref, dma_ref, dmb_ref, dwa_ref, dwb_ref, dhb_ref):
        @pl.when(pl.program_id(0) == 0)
        def _():
            dwa_ref[...] = jnp.zeros_like(dwa_ref)
            dwb_ref[...] = jnp.zeros_like(dwb_ref)

        dhb = _b(dh_ref[...])
        dhb_ref[...] = dhb
        dy = _dot_nt(dhb, wo_ref[...])
        d_oa, d_gr, d_ma, dwa = branch(dy, oa_ref, wa_ref, gr_ref, ma_ref, GLA_DV)
        d_ob, d_dg, d_mb, dwb = branch(dy, ob_ref, wb_ref, dg_ref, mb_ref, DN_DV)
        doa_ref[...] = d_oa
        dob_ref[...] = d_ob
        dgr_ref[...] = _b(d_gr)
        ddg_ref[...] = _b(d_dg)
        dma_ref[...] = _b(d_ma)
        dmb_ref[...] = _b(d_mb)
        dwa_ref[...] += dwa
        dwb_ref[...] += dwb

    row = pl.BlockSpec((tm, D), lambda i: (i, 0))
    f32 = jax.ShapeDtypeStruct((T, D), F32)
    b16 = jax.ShapeDtypeStruct((T, D), BF16)
    return pl.pallas_call(
        body, name=name, grid=(T // tm,),
        in_specs=[row] + _merge_specs(tm),
        out_specs=[row] * 6 + [pl.BlockSpec((1, GLA_DV), lambda i: (0, 0)), pl.BlockSpec((1, DN_DV), lambda i: (0, 0)), row],
        out_shape=[f32, f32, b16, b16, b16, b16, jax.ShapeDtypeStruct((1, GLA_DV), F32),
                   jax.ShapeDtypeStruct((1, DN_DV), F32), b16],
        compiler_params=_cp("arbitrary"),
    )(dh, oa, ob, pbig, pbig, pbig, pbig, gla_hn, dn_hn, wout)


def _loss_head(h, nw, target, name):
    T = h.shape[0]
    tm = min(512, T)

    def body(h_ref, nw_ref, t_ref, dx_ref, loss_ref, dnw_ref):
        @pl.when(pl.program_id(0) == 0)
        def _():
            loss_ref[...] = jnp.zeros_like(loss_ref)
            dnw_ref[...] = jnp.zeros_like(dnw_ref)

        x = h_ref[...]
        w = nw_ref[...]
        r = lax.rsqrt(jnp.mean(x * x, axis=-1, keepdims=True) + EPS)
        xhat = x * r
        err = xhat * w - t_ref[...]
        part = jnp.sum(jnp.sum(err * err, axis=-1, keepdims=True), axis=0, keepdims=True)
        loss_ref[...] += (0.5 / D) * part
        dout = err * (1.0 / D)
        dnw_ref[...] += jnp.sum(dout * xhat, axis=0, keepdims=True)
        dxhat = dout * w
        dx_ref[...] = r * (dxhat - xhat * jnp.mean(dxhat * xhat, axis=-1, keepdims=True))

    row = pl.BlockSpec((tm, D), lambda i: (i, 0))
    one = pl.BlockSpec((1, D), lambda i: (0, 0))
    return pl.pallas_call(
        body, name=name, grid=(T // tm,),
        in_specs=[row, one, row],
        out_specs=[row, pl.BlockSpec((8, 128), lambda i: (0, 0)), one],
        out_shape=[jax.ShapeDtypeStruct((T, D), F32), jax.ShapeDtypeStruct((8, 128), F32),
                   jax.ShapeDtypeStruct((1, D), F32)],
        compiler_params=_cp("arbitrary"),
    )(h, nw, target)


def _adamw(w, g, m, v, rows, name):
    R, C = w.shape
    rows = min(rows, R)
    c1 = 1.0 - ADAM_B1 ** ADAM_STEP
    c2 = 1.0 - ADAM_B2 ** ADAM_STEP

    def body(w_ref, g_ref, m_ref, v_ref, d_ref, mo_ref, vo_ref):
        g_ = g_ref[...]
        m_ = ADAM_B1 * m_ref[...] + (1.0 - ADAM_B1) * g_
        v_ = ADAM_B2 * v_ref[...] + (1.0 - ADAM_B2) * (g_ * g_)
        mo_ref[...] = m_
        vo_ref[...] = v_
        d_ref[...] = -ADAM_LR * ((m_ / c1) / (jnp.sqrt(v_ / c2) + ADAM_EPS) + ADAM_WD * w_ref[...])

    blk = pl.BlockSpec((rows, C), lambda i: (i, 0))
    shp = jax.ShapeDtypeStruct((R, C), F32)
    return pl.pallas_call(
        body, name=name, grid=(R // rows,),
        in_specs=[blk] * 4, out_specs=[blk] * 3, out_shape=[shp] * 3,
        compiler_params=_cp("parallel"),
    )(w, g, m, v)


def _me():
    return lax.axis_index("x"), lax.axis_index("y"), lax.axis_index("c")


def _other_chips(x, y):
    return [(1 - x, y), (x, 1 - y), (1 - x, 1 - y)]


def _half_rows(ref, hf):
    half = ref.shape[-2] // 2
    rows = pl.ds(pl.multiple_of(hf * half, 16), half)
    return ref.at[rows, :] if len(ref.shape) == 2 else ref.at[:, rows, :]


def _gather_weights(big, small, name):
    nbig, nsm = len(big), len(small)
    n = nbig + nsm
    own_sem = 6 * nbig + 3 * nsm

    def body(*refs):
        ins, outs = refs[:n], refs[n:2 * n]
        send_sems, recv_sems = refs[2 * n:]
        x, y, c = _me()
        sibling = (x, y, 1 - c)
        chips = _other_chips(x, y)
        slot = lambda chip: 2 * chip[0] + chip[1]

        def copy(k, src, dst, to):
            return pltpu.make_async_remote_copy(src_ref=src, dst_ref=dst, send_sem=send_sems.at[k],
                                                recv_sem=recv_sems.at[k], device_id=to, device_id_type=MESH)

        sent = []
        for i in range(nbig):
            sent.append(copy(own_sem + i, ins[i], outs[i].at[slot((x, y))], sibling))
            sent[-1].start()
            for j, chip in enumerate(chips):
                sent.append(copy(6 * i + j, _half_rows(ins[i], c), _half_rows(outs[i].at[slot((x, y))], c), (*chip, c)))
                sent[-1].start()
        for t in range(nsm):
            w_ref, o_ref = ins[nbig + t], outs[nbig + t]
            o_ref[slot((x, y))] = w_ref[...]
            for j, chip in enumerate(chips):
                sent.append(copy(6 * nbig + 3 * t + j, w_ref, o_ref.at[slot((x, y))], (*chip, c)))
                sent[-1].start()
        for i in range(nbig):
            for j, chip in enumerate(chips):
                landed = _half_rows(outs[i].at[slot(chip)], c)
                copy(6 * i + j, landed, landed, (x, y, c)).wait_recv()
                sent.append(copy(6 * i + 3 + j, landed, landed, sibling))
                sent[-1].start()
        for t in range(nsm):
            for j, chip in enumerate(chips):
                landed = outs[nbig + t].at[slot(chip)]
                copy(6 * nbig + 3 * t + j, landed, landed, (x, y, c)).wait_recv()
        for i in range(nbig):
            for j, chip in enumerate(chips):
                passed = _half_rows(outs[i].at[slot(chip)], 1 - c)
                copy(6 * i + 3 + j, passed, passed, (x, y, c)).wait_recv()
        for i in range(nbig):
            mine = outs[i].at[slot((x, y))]
            copy(own_sem + i, mine, mine, (x, y, c)).wait_recv()
        for cp in sent:
            cp.wait_send()

    vm = pl.BlockSpec(memory_space=pltpu.VMEM)
    nsem = own_sem + nbig
    return pl.pallas_call(
        body, name=name, in_specs=[ANY] * nbig + [vm] * nsm, out_specs=[ANY] * nbig + [vm] * nsm,
        out_shape=[jax.ShapeDtypeStruct((N_SHARD,) + w.shape, w.dtype) for w in list(big) + list(small)],
        scratch_shapes=[pltpu.SemaphoreType.DMA((nsem,)), pltpu.SemaphoreType.DMA((nsem,))],
        compiler_params=pltpu.CompilerParams(has_side_effects=True),
    )(*big, *small)


def _rs_sibling(gs, name):
    n = len(gs)

    def body(*refs):
        send_sems, recv_sems = refs[2 * n:]
        x, y, c = _me()
        cps = [pltpu.make_async_remote_copy(src_ref=_half_rows(refs[i], 1 - c), dst_ref=refs[n + i],
                                            send_sem=send_sems.at[i], recv_sem=recv_sems.at[i],
                                            device_id=(x, y, 1 - c), device_id_type=MESH) for i in range(n)]
        for cp in cps:
            cp.start()
        for cp in cps:
            cp.wait()

    return pl.pallas_call(
        body, name=name, in_specs=[ANY] * n, out_specs=[ANY] * n,
        out_shape=[jax.ShapeDtypeStruct((g.shape[0], g.shape[1] // 2, g.shape[2]), g.dtype) for g in gs],
        scratch_shapes=[pltpu.SemaphoreType.DMA((n,)), pltpu.SemaphoreType.DMA((n,))],
        compiler_params=pltpu.CompilerParams(has_side_effects=True),
    )(*gs)


def _add_pair(g, other, where, name):
    ns, a, b = g.shape
    half = a // 2

    def body(w_ref, g_ref, o_ref, pb_ref, own_ref):
        t = g_ref[0].astype(F32) + o_ref[0].astype(F32)
        pb_ref[0] = _b(t)

        @pl.when(pl.program_id(0) == w_ref[1])
        def _():
            own_ref[...] = t

    return pl.pallas_call(
        body, name=name,
        grid_spec=pltpu.PrefetchScalarGridSpec(
            num_scalar_prefetch=1, grid=(ns,),
            in_specs=[pl.BlockSpec((1, half, b), lambda s, w: (s, w[0], 0)), pl.BlockSpec((1, half, b), lambda s, w: (s, 0, 0))],
            out_specs=[pl.BlockSpec((1, half, b), lambda s, w: (s, 0, 0)), pl.BlockSpec((half, b), lambda s, w: (0, 0))]),
        out_shape=[jax.ShapeDtypeStruct((ns, half, b), BF16), jax.ShapeDtypeStruct((half, b), F32)],
        compiler_params=_cp("arbitrary"),
    )(where, g, other)


def _rs_chips(pbs, name):
    n = len(pbs)

    def body(*refs):
        send_sems, recv_sems = refs[2 * n:]
        x, y, c = _me()
        cps = [pltpu.make_async_remote_copy(src_ref=refs[i].at[2 * chip[0] + chip[1]], dst_ref=refs[n + i].at[j],
                                            send_sem=send_sems.at[3 * i + j], recv_sem=recv_sems.at[3 * i + j],
                                            device_id=(*chip, c), device_id_type=MESH)
               for i in range(n) for j, chip in enumerate(_other_chips(x, y))]
        for cp in cps:
            cp.start()
        for cp in cps:
            cp.wait()

    return pl.pallas_call(
        body, name=name, in_specs=[ANY] * n, out_specs=[ANY] * n,
        out_shape=[jax.ShapeDtypeStruct((3,) + p.shape[1:], p.dtype) for p in pbs],
        scratch_shapes=[pltpu.SemaphoreType.DMA((3 * n,)), pltpu.SemaphoreType.DMA((3 * n,))],
        compiler_params=pltpu.CompilerParams(has_side_effects=True),
    )(*pbs)


def _add_four(own, got, name):
    rows, cols = own.shape
    rb = rows // 2

    def body(a_ref, b_ref, o_ref):
        o_ref[...] = ((a_ref[...] + b_ref[0].astype(F32)) + b_ref[1].astype(F32)) + b_ref[2].astype(F32)

    return pl.pallas_call(
        body, name=name, grid=(rows // rb,),
        in_specs=[pl.BlockSpec((rb, cols), lambda i: (i, 0)), pl.BlockSpec((3, rb, cols), lambda i: (0, i, 0))],
        out_specs=pl.BlockSpec((rb, cols), lambda i: (i, 0)),
        out_shape=jax.ShapeDtypeStruct((rows, cols), F32),
        compiler_params=_cp("parallel"),
    )(own, got)


def _rs_swap(halves, name):
    n = len(halves)

    def body(*refs):
        send_sems, recv_sems = refs[2 * n:]
        x, y, c = _me()
        cps = [pltpu.make_async_remote_copy(src_ref=refs[i], dst_ref=refs[n + i], send_sem=send_sems.at[i],
                                            recv_sem=recv_sems.at[i], device_id=(x, y, 1 - c), device_id_type=MESH)
               for i in range(n)]
        for cp in cps:
            cp.start()
        for cp in cps:
            cp.wait()

    return pl.pallas_call(
        body, name=name, in_specs=[ANY] * n, out_specs=[ANY] * n,
        out_shape=[jax.ShapeDtypeStruct(h.shape, h.dtype) for h in halves],
        scratch_shapes=[pltpu.SemaphoreType.DMA((n,)), pltpu.SemaphoreType.DMA((n,))],
        compiler_params=pltpu.CompilerParams(has_side_effects=True),
    )(*halves)


def _adamw_halves(w, own, got, m, v, rows, name):
    a, b = w.shape
    nblk = a // 2 // rows
    c1 = 1.0 - ADAM_B1 ** ADAM_STEP
    c2 = 1.0 - ADAM_B2 ** ADAM_STEP

    def body(w_ref, own_ref, got_ref, m_ref, v_ref, g_ref, d_ref, mo_ref, vo_ref):
        g_ = jnp.where(pl.program_id(0) == lax.axis_index("c"), own_ref[...], got_ref[...])
        g_ref[...] = g_
        m_ = ADAM_B1 * m_ref[...] + (1.0 - ADAM_B1) * g_
        v_ = ADAM_B2 * v_ref[...] + (1.0 - ADAM_B2) * (g_ * g_)
        mo_ref[...] = m_
        vo_ref[...] = v_
        d_ref[...] = -ADAM_LR * ((m_ / c1) / (jnp.sqrt(v_ / c2) + ADAM_EPS) + ADAM_WD * w_ref[...])

    whole = pl.BlockSpec((rows, b), lambda h, i: (h * nblk + i, 0))
    part = pl.BlockSpec((rows, b), lambda h, i: (i, 0))
    shp = jax.ShapeDtypeStruct((a, b), F32)
    return pl.pallas_call(
        body, name=name, grid=(2, nblk),
        in_specs=[whole, part, part, whole, whole], out_specs=[whole] * 4, out_shape=[shp] * 4,
        compiler_params=_cp("parallel", "parallel"),
    )(w, own, got, m, v)


def _allsum_small(vec, name):
    def body(v_ref, o_ref, buf_ref, send_sems, recv_sems):
        x, y, c = _me()
        me = 4 * x + 2 * y + c
        buf_ref[me] = v_ref[...]
        cps = []
        for k in range(1, 8):
            peer = (x ^ (k >> 2), y ^ ((k >> 1) & 1), c ^ (k & 1))
            cps.append(pltpu.make_async_remote_copy(src_ref=v_ref, dst_ref=buf_ref.at[me],
                                                    send_sem=send_sems.at[k - 1], recv_sem=recv_sems.at[k - 1],
                                                    device_id=peer, device_id_type=MESH))
        for cp in cps:
            cp.start()
        for k in range(1, 8):
            peer_idx = me ^ k
            pltpu.make_async_remote_copy(src_ref=v_ref, dst_ref=buf_ref.at[peer_idx],
                                         send_sem=send_sems.at[k - 1], recv_sem=recv_sems.at[k - 1],
                                         device_id=(x, y, c), device_id_type=MESH).wait_recv()
        for cp in cps:
            cp.wait_send()
        acc = buf_ref[0]
        for d in range(1, 8):
            acc = acc + buf_ref[d]
        o_ref[...] = acc

    return pl.pallas_call(
        body, name=name,
        in_specs=[pl.BlockSpec(memory_space=pltpu.VMEM)], out_specs=pl.BlockSpec(memory_space=pltpu.VMEM),
        out_shape=jax.ShapeDtypeStruct(vec.shape, F32),
        scratch_shapes=[pltpu.VMEM((8,) + vec.shape, F32), pltpu.SemaphoreType.DMA((7,)), pltpu.SemaphoreType.DMA((7,))],
        compiler_params=pltpu.CompilerParams(has_side_effects=True),
    )(vec)


BIG = ("ffn1_w_gate", "ffn1_w_up", "ffn1_w_down", "w_in", "w_out", "ffn2_w_gate", "ffn2_w_up", "ffn2_w_down")
TINY = ("w_gla_gate", "conv_w")
SHARDED = BIG + TINY


def _join_cols(w4):
    return jnp.transpose(w4, (1, 0, 2)).reshape(w4.shape[1], N_SHARD * w4.shape[2])


def _cut_cols(w):
    return jnp.transpose(w.reshape(w.shape[0], N_SHARD, w.shape[1] // N_SHARD), (1, 0, 2))


def _split_w_in(w):
    o = IN_OFF
    big = jnp.concatenate([w[:, :o[4]], w[:, o[5]:o[9]], w[:, o[11]:]], axis=1)
    small = jnp.concatenate([w[:, o[4]:o[5]], w[:, o[9]:o[11]], jnp.zeros((w.shape[0], SMALL - 32), w.dtype)], axis=1)
    return big, small


def _merge_w_in(big, small):
    return jnp.concatenate([big[:, :3072], small[:, :16], big[:, 3072:7168], small[:, 16:32], big[:, 7168:]], axis=1)


def _local_step(x, target, W, P):
    wbig, wsmall = W["w_in_big"], W["w_in_small"]
    wgate_pad = jnp.zeros((SMALL, GLA_H * GLA_DK), F32).at[:GLA_RANK].set(P["w_gla_gate"])
    cw8 = jnp.zeros((8, CONV_C), F32).at[:CONV_K].set(P["conv_w"])
    par = jnp.zeros((DN_H, 8, 128), F32)
    par = par.at[:, 0, :].set(jnp.broadcast_to(P["dn_a_log"].reshape(DN_H, 1), (DN_H, 128)))
    par = par.at[:, 1, :].set(jnp.broadcast_to(P["dn_dt_bias"].reshape(DN_H, 1), (DN_H, 128)))

    h1, n1, g1, u1 = _ffn_fwd(x, P["ffn1_norm"], W["ffn1_w_gate"], W["ffn1_w_up"], W["ffn1_w_down"], "ffn1_fwd")
    pbig, psmall, n2 = _norm_proj(h1, P["mix_norm"], wbig, wsmall, "mix_proj")
    oa, sa = _gla_fwd(pbig, psmall, wgate_pad, P["b_gla_gate"], "gla_fwd")
    conv = _conv_fwd(pbig, cw8, "conv_fwd")
    ob, sb = _gdn_fwd(conv, psmall, par, "gdn_fwd")
    h2, yb = _merge_fwd(h1, oa, ob, pbig, P["gla_head_norm"], P["dn_head_norm"], W["w_out"], "merge_fwd")
    h3, n3, g3, u3 = _ffn_fwd(h2, P["ffn2_norm"], W["ffn2_w_gate"], W["ffn2_w_up"], W["ffn2_w_down"], "ffn2_fwd")
    dh3, loss, d_final = _loss_head(h3, P["final_norm"], target, "loss_head")

    gw, gs = {}, {"final_norm": d_final}

    def ffn_grads(tag, dh, h, n, g, u):
        dx, dg, du, act, dfb, dnw = _ffn_bwd(dh, h, P[tag + "_norm"], g, u, W[tag + "_w_gate"], W[tag + "_w_up"],
                                             W[tag + "_w_down"], tag + "_bwd")
        gw[tag + "_w_gate"] = _mm_tn(n, dg, D, FF_CUT, tag + "_dwg")
        gw[tag + "_w_up"] = _mm_tn(n, du, D, FF_CUT, tag + "_dwu")
        gw[tag + "_w_down"] = _mm_tn(act, dfb, FF_CUT, D, tag + "_dwd")
        gs[tag + "_norm"] = dnw
        return dx

    dh2 = ffn_grads("ffn2", dh3, h2, n3, g3, u3)
    d_oa, d_ob, d_gr, d_dgate, d_ma, d_mb, gs["gla_head_norm"], gs["dn_head_norm"], dh2b = _merge_bwd(
        dh2, oa, ob, pbig, P["gla_head_norm"], P["dn_head_norm"], W["w_out"], "merge_bwd")
    gw["w_out"] = _mm_tn(yb, dh2b, D, D, "dw_out").reshape(N_SHARD, D // N_SHARD, D)
    d_gq, d_gk, d_gv, dpre = _gla_bwd(pbig, psmall, wgate_pad, P["b_gla_gate"], sa, d_oa, "gla_bwd")
    dcq, dck, dcv, dsm, dpar = _gdn_bwd(conv, psmall, par, sb, d_ob, "gdn_bwd")
    dsmall, dwgate, gs["b_gla_gate"] = _gla_gate_bwd(dpre, psmall, wgate_pad, dsm, "gla_gate_bwd")
    gs["w_gla_gate"] = dwgate[:GLA_RANK]
    d_x3, dcw = _conv_bwd(dcq, dck, dcv, pbig, cw8, "conv_bwd")
    gs["conv_w"] = dcw[:CONV_K]
    gs["dn_a_log"] = dpar[:, 0, 0].reshape(1, DN_H)
    gs["dn_dt_bias"] = dpar[:, 0, 1].reshape(1, DN_H)
    pieces = (d_gq, d_gk, d_gv, d_gr, d_x3, d_dgate, d_ma, d_mb)
    dh1, gs["mix_norm"] = _proj_bwd(dh2, h1, P["mix_norm"], pieces, dsmall, wbig, wsmall, "proj_bwd")
    dbig = jnp.concatenate([_mm_tn(n2, p, D, 1024, "dw_in_%d" % i) for i, p in enumerate(pieces)], axis=1)
    dsml = _mm_tn(n2, dsmall, D, SMALL, "dw_in_small")
    gw["w_in"] = _cut_cols(_merge_w_in(dbig, dsml))
    grad_x = ffn_grads("ffn1", dh1, x, n1, g1, u1)
    return loss, grad_x, gw, gs


SMALL_NAMES = ("ffn1_norm", "mix_norm", "ffn2_norm", "final_norm", "b_gla_gate", "gla_head_norm", "dn_head_norm",
               "dn_a_log", "dn_dt_bias")
ROW4 = (("b_gla_gate", 512), ("gla_head_norm", 256), ("dn_head_norm", 128), ("dn_a_log", 8), ("dn_dt_bias", 8))


def _pack_small(d, loss=None):
    row4 = [d[n].reshape(-1) for n, _ in ROW4]
    row4.append(jnp.zeros((1,), F32) if loss is None else loss.reshape(1))
    row4 = jnp.concatenate(row4)
    row4 = jnp.pad(row4, (0, D - row4.shape[0]))
    rows = [d[n].reshape(-1) for n in SMALL_NAMES[:4]] + [row4]
    return jnp.concatenate([jnp.stack(rows), jnp.zeros((3, D), F32)], axis=0)


def _unpack_small(a, like):
    out = {n: a[i].reshape(like[n].shape) for i, n in enumerate(SMALL_NAMES[:4])}
    off = 0
    for n, w in ROW4:
        out[n] = a[4, off:off + w].reshape(like[n].shape)
        off += w
    return out, a[4, off]


WEIGHT_ORDER = ("ffn1_norm", "ffn1_w_gate", "ffn1_w_up", "ffn1_w_down", "mix_norm", "w_in", "w_gla_gate", "b_gla_gate",
                "conv_w", "dn_a_log", "dn_dt_bias", "gla_head_norm", "dn_head_norm", "w_out", "ffn2_norm",
                "ffn2_w_gate", "ffn2_w_up", "ffn2_w_down", "final_norm")
ADAM_ROWS = {"ffn1_w_gate": 256, "ffn1_w_up": 256, "ffn1_w_down": 176, "w_in": 128, "w_gla_gate": 16, "conv_w": 4,
             "w_out": 64, "ffn2_w_gate": 256, "ffn2_w_up": 256, "ffn2_w_down": 176}


def kernel(x, ffn1_norm, ffn1_w_gate, ffn1_w_up, ffn1_w_down, mix_norm, w_in, w_gla_gate, b_gla_gate, conv_w, dn_a_log, dn_dt_bias, gla_head_norm, dn_head_norm, w_out, ffn2_norm, ffn2_w_gate, ffn2_w_up, ffn2_w_down, final_norm, loss_target, m_ffn1_norm, m_ffn1_w_gate, m_ffn1_w_up, m_ffn1_w_down, m_mix_norm, m_w_in, m_w_gla_gate, m_b_gla_gate, m_conv_w, m_dn_a_log, m_dn_dt_bias, m_gla_head_norm, m_dn_head_norm, m_w_out, m_ffn2_norm, m_ffn2_w_gate, m_ffn2_w_up, m_ffn2_w_down, m_final_norm, v_ffn1_norm, v_ffn1_w_gate, v_ffn1_w_up, v_ffn1_w_down, v_mix_norm, v_w_in, v_w_gla_gate, v_b_gla_gate, v_conv_w, v_dn_a_log, v_dn_dt_bias, v_gla_head_norm, v_dn_head_norm, v_w_out, v_ffn2_norm, v_ffn2_w_gate, v_ffn2_w_up, v_ffn2_w_down, v_final_norm):
    given = dict(locals())
    wts = {n: given[n] for n in WEIGHT_ORDER}
    moms = {n: given["m_" + n] for n in WEIGHT_ORDER}
    vars_ = {n: given["v_" + n] for n in WEIGHT_ORDER}
    two_d = lambda a: a.reshape(a.shape[-2], a.shape[-1]) if a.ndim == 3 else a.reshape(1, -1)
    shard = {n: two_d(wts[n]) for n in SHARDED}

    gathered = _gather_weights([shard[n].astype(BF16) for n in BIG], [shard[n] for n in TINY], "gather_weights")
    G = dict(zip(SHARDED, gathered))
    W = {n: G[n] for n in BIG if n.startswith("ffn")}
    W["w_out"] = G["w_out"].reshape(D, D)
    W["w_in_big"], W["w_in_small"] = _split_w_in(_join_cols(G["w_in"]))
    P = {n: two_d(wts[n]) for n in SMALL_NAMES}
    for n in TINY:
        P[n] = _join_cols(G[n])

    loss, grad_x, gw, gs = _local_step(x[0], loss_target[0], W, P)

    my_slot = 2 * lax.axis_index("x") + lax.axis_index("y")
    where = jnp.stack([lax.axis_index("c"), my_slot]).astype(jnp.int32)
    mine = [gw[n] for n in BIG]
    from_sibling = _rs_sibling(mine, "rs_sibling")
    pairs = [_add_pair(g, o, where, "rs_pair_" + n) for n, g, o in zip(BIG, mine, from_sibling)]
    from_chips = _rs_chips([p[0] for p in pairs], "rs_chips")
    halves = [_add_four(p[1], got, "rs_four_" + n) for n, p, got in zip(BIG, pairs, from_chips)]
    other_halves = _rs_swap(halves, "rs_swap")

    tiny_rows = jnp.concatenate([gs["w_gla_gate"].reshape(8, D), gs["conv_w"].reshape(12, D), jnp.zeros((4, D), F32)])
    all_sum = _allsum_small(jnp.concatenate([_pack_small(gs, loss[0, 0]), tiny_rows]), "allsum_small")
    small_sum = all_sum[:8]
    small_g, loss_total = _unpack_small(small_sum, P)

    grads, delta, new_m, new_v = {}, {}, {}, {}
    for n, own, got in zip(BIG, halves, other_halves):
        res = _adamw_halves(shard[n], own, got, two_d(moms[n]), two_d(vars_[n]), ADAM_ROWS[n], "adamw_" + n)
        grads[n], delta[n], new_m[n], new_v[n] = (t.reshape(wts[n].shape) for t in res)
    for n, rows in (("w_gla_gate", all_sum[8:16]), ("conv_w", all_sum[16:28])):
        cols = shard[n].shape[1]
        grads[n] = lax.dynamic_slice_in_dim(rows.reshape(shard[n].shape[0], N_SHARD * cols), my_slot * cols, cols, axis=1)
        d, m_, v_ = _adamw(shard[n], grads[n], two_d(moms[n]), two_d(vars_[n]), ADAM_ROWS[n], "adamw_" + n)
        delta[n], new_m[n], new_v[n] = (t.reshape(wts[n].shape) for t in (d, m_, v_))
    pk = lambda src: _pack_small({n: two_d(src[n]) for n in SMALL_NAMES})
    sd, sm_, sv_ = _adamw(pk(wts), small_sum, pk(moms), pk(vars_), 8, "adamw_small")
    for res, dst in ((sd, delta), (sm_, new_m), (sv_, new_v)):
        u, _ = _unpack_small(res, wts)
        dst.update(u)
    grad_w = {n: grads[n].reshape(wts[n].shape) for n in SHARDED}
    grad_w.update({n: small_g[n].reshape(wts[n].shape) for n in SMALL_NAMES})
    return (loss_total, grad_x[None], *[grad_w[n] for n in WEIGHT_ORDER], *[delta[n] for n in WEIGHT_ORDER],
            *[new_m[n] for n in WEIGHT_ORDER], *[new_v[n] for n in WEIGHT_ORDER])
```

```python
import functools
import math

import numpy as np
import jax
import jax.numpy as jnp
from jax import lax
from jax.experimental import pallas as pl
from jax.experimental.pallas import tpu as pltpu

F32 = jnp.float32
BF16 = jnp.bfloat16
HI = lax.Precision.HIGH
MESH = pl.DeviceIdType.MESH
ANY = pl.BlockSpec(memory_space=pl.ANY)

EPS = 1e-6
D = 1024
DFF = 2816
FFN_RES = 0.5
GLA_H, GLA_DK, GLA_DV, GLA_RANK, GLA_TAU = 4, 128, 256, 16, 16.0
DN_H, DN_DK, DN_DV = 8, 128, 128
CONV_K = 4
CHUNK = 64
N_SHARD = 4
FF_CUT = DFF // N_SHARD
ADAM_LR, ADAM_B1, ADAM_B2, ADAM_EPS, ADAM_WD, ADAM_STEP = 0.001, 0.9, 0.999, 1e-08, 0.01, 10

IN_SIZES = (512, 512, 1024, 1024, 16, 1024, 1024, 1024, 1024, 8, 8, 1024, 1024)
IN_OFF = tuple(int(v) for v in np.cumsum((0,) + IN_SIZES))
D_IN = IN_OFF[-1]
BIG_COLS = 9216
SMALL = 128
PIECES = (512, 512, 1024, 1024, 3072, 1024, 1024, 1024)

VMEM_LIMIT = 56 * 1024 * 1024
ROW_BLK = 256
ATT_BLK = 256
GDN_BLK = 128
GDN_HEADS = 8


def _cp(*sem):
    return pltpu.CompilerParams(dimension_semantics=sem, vmem_limit_bytes=VMEM_LIMIT)


def _sigmoid(x):
    return 1.0 / (1.0 + jnp.exp(-x))


def _softplus(x):
    return jnp.maximum(x, 0.0) + jnp.log(1.0 + jnp.exp(-jnp.abs(x)))


def _log_sigmoid(x):
    return jnp.minimum(x, 0.0) - jnp.log(1.0 + jnp.exp(-jnp.abs(x)))


def _dot(a, b, prec=None):
    return jnp.dot(a, b, preferred_element_type=F32, precision=prec)


def _dot_nt(a, b, prec=None):
    return lax.dot_general(a, b, (((1,), (1,)), ((), ())), preferred_element_type=F32, precision=prec)


def _dot_tn(a, b, prec=None):
    return lax.dot_general(a, b, (((0,), (0,)), ((), ())), preferred_element_type=F32, precision=prec)


def _b(x):
    return x.astype(BF16)


def _iota2(n, m, axis):
    return lax.broadcasted_iota(jnp.int32, (n, m), axis)


def _load_weights(pairs, sem):
    copies = [pltpu.make_async_copy(s, d, sem.at[i]) for i, (s, d) in enumerate(pairs)]
    for c in copies:
        c.start()
    for c in copies:
        c.wait()


def _ffn_fwd(h, nw, wg, wu, wd, name, carry=None):
    T = h.shape[0]
    tm = min(ROW_BLK, T)
    ex_in, ex_args, ex_out, ex_shape, ex_sems = _carry_specs(carry)

    def body(h_ref, nw_ref, wg_hbm, wu_hbm, wd_hbm, ho_ref, n_ref, g_ref, u_ref, wg_v, wu_v, wd_v, sem):
        @pl.when(pl.program_id(0) == 0)
        def _():
            _load_weights(((wg_hbm, wg_v), (wu_hbm, wu_v), (wd_hbm, wd_v)), sem)

        x = h_ref[...]
        r = lax.rsqrt(jnp.mean(x * x, axis=-1, keepdims=True) + EPS)
        nb = _b((x * r) * nw_ref[...])
        n_ref[...] = nb
        acc = jnp.zeros((tm, D), F32)
        for s in range(N_SHARD):
            g = _dot(nb, wg_v[s])
            u = _dot(nb, wu_v[s])
            g_ref[s] = _b(g)
            u_ref[s] = _b(u)
            acc += _dot(_b(g * _sigmoid(g) * u), wd_v[s])
        ho_ref[...] = x + FFN_RES * acc

    row = lambda w: pl.BlockSpec((tm, w), lambda i: (i, 0))
    cut = pl.BlockSpec((N_SHARD, tm, FF_CUT), lambda i: (0, i, 0))
    return pl.pallas_call(
        _carry(carry, body, 5, 4, (T // tm,)), name=name, grid=(T // tm,),
        in_specs=[row(D), pl.BlockSpec((1, D), lambda i: (0, 0)), ANY, ANY, ANY] + ex_in,
        out_specs=[row(D), row(D), cut, cut] + ex_out,
        out_shape=[jax.ShapeDtypeStruct((T, D), F32), jax.ShapeDtypeStruct((T, D), BF16),
                   jax.ShapeDtypeStruct((N_SHARD, T, FF_CUT), BF16),
                   jax.ShapeDtypeStruct((N_SHARD, T, FF_CUT), BF16)] + ex_shape,
        scratch_shapes=[pltpu.VMEM((N_SHARD, D, FF_CUT), BF16), pltpu.VMEM((N_SHARD, D, FF_CUT), BF16),
                        pltpu.VMEM((N_SHARD, FF_CUT, D), BF16), pltpu.SemaphoreType.DMA((3,))] + ex_sems,
        compiler_params=_cp("arbitrary"),
    )(h, nw, wg, wu, wd, *ex_args)


def _ffn_bwd(dh, h, nw, g, u, wg, wu, wd, name, carry=None):
    T = h.shape[0]
    tm = min(ROW_BLK, T)
    ex_in, ex_args, ex_out, ex_shape, ex_sems = _carry_specs(carry)

    def body(dh_ref, h_ref, nw_ref, g_ref, u_ref, wg_hbm, wu_hbm, wd_hbm,
             dx_ref, dg_ref, du_ref, a_ref, df_ref, dnw_ref, wg_v, wu_v, wd_v, sem):
        @pl.when(pl.program_id(0) == 0)
        def _():
            _load_weights(((wg_hbm, wg_v), (wu_hbm, wu_v), (wd_hbm, wd_v)), sem)
            dnw_ref[...] = jnp.zeros_like(dnw_ref)

        dh_ = dh_ref[...]
        dfb = _b(FFN_RES * dh_)
        df_ref[...] = dfb
        dn = jnp.zeros((tm, D), F32)
        for s in range(N_SHARD):
            da = _dot_nt(dfb, wd_v[s])
            gg = g_ref[s].astype(F32)
            uu = u_ref[s].astype(F32)
            sg = _sigmoid(gg)
            silu = gg * sg
            a_ref[s] = _b(silu * uu)
            dgb = _b(da * uu * (sg * (1.0 + gg * (1.0 - sg))))
            dub = _b(da * silu)
            dg_ref[s] = dgb
            du_ref[s] = dub
            dn += _dot_nt(dgb, wg_v[s]) + _dot_nt(dub, wu_v[s])
        x = h_ref[...]
        r = lax.rsqrt(jnp.mean(x * x, axis=-1, keepdims=True) + EPS)
        xhat = x * r
        dnw_ref[...] += jnp.sum(dn * xhat, axis=0, keepdims=True)
        dxhat = dn * nw_ref[...]
        dx_ref[...] = dh_ + r * (dxhat - xhat * jnp.mean(dxhat * xhat, axis=-1, keepdims=True))

    row = lambda w: pl.BlockSpec((tm, w), lambda i: (i, 0))
    one = pl.BlockSpec((1, D), lambda i: (0, 0))
    cut = pl.BlockSpec((N_SHARD, tm, FF_CUT), lambda i: (0, i, 0))
    cut_shape = jax.ShapeDtypeStruct((N_SHARD, T, FF_CUT), BF16)
    return pl.pallas_call(
        _carry(carry, body, 8, 6, (T // tm,)), name=name, grid=(T // tm,),
        in_specs=[row(D), row(D), one, cut, cut, ANY, ANY, ANY] + ex_in,
        out_specs=[row(D), cut, cut, cut, row(D), one] + ex_out,
        out_shape=[jax.ShapeDtypeStruct((T, D), F32), cut_shape, cut_shape, cut_shape,
                   jax.ShapeDtypeStruct((T, D), BF16), jax.ShapeDtypeStruct((1, D), F32)] + ex_shape,
        scratch_shapes=[pltpu.VMEM((N_SHARD, D, FF_CUT), BF16), pltpu.VMEM((N_SHARD, D, FF_CUT), BF16),
                        pltpu.VMEM((N_SHARD, FF_CUT, D), BF16), pltpu.SemaphoreType.DMA((3,))] + ex_sems,
        compiler_params=_cp("arbitrary"),
    )(dh, h, nw, g, u, wg, wu, wd, *ex_args)


def _mm_tn(a, b, bm, bn, name, out_dtype=BF16, tk=2048):
    cuts = a.shape[0] if a.ndim == 3 else (b.shape[0] if b.ndim == 3 else None)
    T, M = a.shape[-2:]
    N = b.shape[-1]
    tk = min(tk, T)
    bm, bn = min(bm, M), min(bn, N)
    nk = T // tk

    def body(a_ref, b_ref, o_ref, acc_ref):
        k = pl.program_id(3)

        @pl.when(k == 0)
        def _():
            acc_ref[...] = jnp.zeros_like(acc_ref)

        av = a_ref[0] if a.ndim == 3 else a_ref[...]
        bv = b_ref[0] if b.ndim == 3 else b_ref[...]
        acc_ref[...] += _dot_tn(_b(av), _b(bv))

        @pl.when(k == nk - 1)
        def _():
            res = acc_ref[...].astype(out_dtype)
            if cuts is None:
                o_ref[...] = res
            else:
                o_ref[0] = res

    a_spec = (pl.BlockSpec((1, tk, bm), lambda s, i, j, k: (s, k, i)) if a.ndim == 3
              else pl.BlockSpec((tk, bm), lambda s, i, j, k: (k, i)))
    b_spec = (pl.BlockSpec((1, tk, bn), lambda s, i, j, k: (s, k, j)) if b.ndim == 3
              else pl.BlockSpec((tk, bn), lambda s, i, j, k: (k, j)))
    if cuts is None:
        o_spec, o_shape = pl.BlockSpec((bm, bn), lambda s, i, j, k: (i, j)), (M, N)
    else:
        o_spec, o_shape = pl.BlockSpec((1, bm, bn), lambda s, i, j, k: (s, i, j)), (cuts, M, N)
    return pl.pallas_call(
        body, name=name, grid=(cuts or 1, M // bm, N // bn, nk),
        in_specs=[a_spec, b_spec], out_specs=o_spec,
        out_shape=jax.ShapeDtypeStruct(o_shape, out_dtype),
        scratch_shapes=[pltpu.VMEM((bm, bn), F32)],
        compiler_params=_cp("parallel", "parallel", "parallel", "arbitrary"),
    )(a, b)


def _norm_proj(h, nw, wbig, wsmall, name):
    T = h.shape[0]
    tm = min(512, T)
    tn = 1536

    def body(h_ref, nw_ref, wb_hbm, ws_ref, pb_ref, ps_ref, n_ref, wb_v, sem):
        @pl.when(pl.program_id(0) == 0)
        def _():
            _load_weights(((wb_hbm, wb_v),), sem)

        x = h_ref[...]
        r = lax.rsqrt(jnp.mean(x * x, axis=-1, keepdims=True) + EPS)
        nb = _b((x * r) * nw_ref[...])
        n_ref[...] = nb
        ps_ref[...] = _dot(nb, ws_ref[...])
        for j in range(BIG_COLS // tn):
            pb_ref[:, j * tn:(j + 1) * tn] = _b(_dot(nb, wb_v[:, j * tn:(j + 1) * tn]))

    row = lambda w: pl.BlockSpec((tm, w), lambda i: (i, 0))
    return pl.pallas_call(
        body, name=name, grid=(T // tm,),
        in_specs=[row(D), pl.BlockSpec((1, D), lambda i: (0, 0)), ANY, pl.BlockSpec((D, SMALL), lambda i: (0, 0))],
        out_specs=[row(BIG_COLS), row(SMALL), row(D)],
        out_shape=[jax.ShapeDtypeStruct((T, BIG_COLS), BF16), jax.ShapeDtypeStruct((T, SMALL), F32),
                   jax.ShapeDtypeStruct((T, D), BF16)],
        scratch_shapes=[pltpu.VMEM((D, BIG_COLS), BF16), pltpu.SemaphoreType.DMA((1,))],
        compiler_params=_cp("arbitrary"),
    )(h, nw, wbig, wsmall)


def _proj_bwd(dh, h, nw, pieces, dsmall, wbig, wsmall, name):
    T = h.shape[0]
    tm = min(ROW_BLK, T)
    offs = tuple(int(v) for v in np.cumsum((0,) + PIECES))

    def body(dh_ref, h_ref, nw_ref, *rest):
        p_refs = rest[:len(PIECES)]
        ds_ref, wb_hbm, ws_ref, dx_ref, dnw_ref, wb_v, sem = rest[len(PIECES):]

        @pl.when(pl.program_id(0) == 0)
        def _():
            _load_weights(((wb_hbm, wb_v),), sem)
            dnw_ref[...] = jnp.zeros_like(dnw_ref)

        dn = _dot_nt(_b(ds_ref[...]), ws_ref[...])
        for p_ref, lo, wdt in zip(p_refs, offs, PIECES):
            dn += _dot_nt(p_ref[...], wb_v[:, lo:lo + wdt])
        x = h_ref[...]
        r = lax.rsqrt(jnp.mean(x * x, axis=-1, keepdims=True) + EPS)
        xhat = x * r
        dnw_ref[...] += jnp.sum(dn * xhat, axis=0, keepdims=True)
        dxhat = dn * nw_ref[...]
        dx_ref[...] = dh_ref[...] + r * (dxhat - xhat * jnp.mean(dxhat * xhat, axis=-1, keepdims=True))

    row = lambda w: pl.BlockSpec((tm, w), lambda i: (i, 0))
    one = pl.BlockSpec((1, D), lambda i: (0, 0))
    return pl.pallas_call(
        body, name=name, grid=(T // tm,),
        in_specs=[row(D), row(D), one] + [row(w) for w in PIECES] + [row(SMALL), ANY, pl.BlockSpec((D, SMALL), lambda i: (0, 0))],
        out_specs=[row(D), one],
        out_shape=[jax.ShapeDtypeStruct((T, D), F32), jax.ShapeDtypeStruct((1, D), F32)],
        scratch_shapes=[pltpu.VMEM((D, BIG_COLS), BF16), pltpu.SemaphoreType.DMA((1,))],
        compiler_params=_cp("arbitrary"),
    )(dh, h, nw, *pieces, dsmall, wbig, wsmall)


def _gla_block(q_ref, k_ref, sm_ref, wg_ref, bg_ref, nc, tril):
    nbat = GLA_H * nc
    q = _heads_first(q_ref[...].astype(F32), nc, GLA_DK)
    k = _heads_first(k_ref[...].astype(F32), nc, GLA_DK)
    pre = _heads_first(_dot(sm_ref[...], wg_ref[...], HI) + bg_ref[...], nc, GLA_DK)
    la = _log_sigmoid(pre) * (1.0 / GLA_TAU)
    bc = _bmm(jnp.broadcast_to(tril, (nbat, CHUNK, CHUNK)), la, HI)
    bl = bc[:, CHUNK - 1:CHUNK, :]
    eb = jnp.exp(bc)
    enb = jnp.exp(-bc)
    ebl = jnp.exp(bl - bc)
    q_in = q * (GLA_DK ** -0.5) * eb
    k_out = k * enb
    k_st = k * ebl
    a_ch = jnp.exp(bl)
    return pre, eb, enb, ebl, q_in, k_out, k_st, a_ch


def _gla_specs(blk, idx):
    hk, hv = GLA_H * GLA_DK, GLA_H * GLA_DV
    return [pl.BlockSpec((blk, hk), lambda j: (idx(j), 0)),
            pl.BlockSpec((blk, hk), lambda j: (idx(j), 1)),
            pl.BlockSpec((blk, hv), lambda j: (idx(j), 1)),
            pl.BlockSpec((blk, SMALL), lambda j: (idx(j), 0)),
            pl.BlockSpec((SMALL, hk), lambda j: (0, 0)),
            pl.BlockSpec((1, hk), lambda j: (0, 0))]


def _gla_fwd(pbig, psmall, wgate, bgate, name):
    T = pbig.shape[0]
    blk = min(ATT_BLK, T)
    nc = blk // CHUNK

    def body(q_ref, k_ref, v_ref, sm_ref, wg_ref, bg_ref, o_ref, ss_ref, st_ref):
        @pl.when(pl.program_id(0) == 0)
        def _():
            st_ref[...] = jnp.zeros_like(st_ref)

        causal = _iota2(CHUNK, CHUNK, 0) >= _iota2(CHUNK, CHUNK, 1)
        _, _, _, _, q_in, k_out, k_st, a_ch = _gla_block(q_ref, k_ref, sm_ref, wg_ref, bg_ref, nc, causal.astype(F32))
        v = _heads_first(v_ref[...], nc, GLA_DV)
        qb = _b(q_in)
        sc = jnp.where(causal, _bmm_nt(qb, _b(k_out)), 0.0)
        kv = _bmm_tn(v, _b(k_st))
        before = [None] * (GLA_H * nc)
        for i in range(GLA_H):
            st = st_ref[i]
            for c in range(nc):
                n = i * nc + c
                before[n] = st
                st = st * a_ch[n] + kv[n]
            st_ref[i] = st
        states = jnp.stack(before)
        ss_ref[...] = states.reshape(GLA_H, nc, GLA_DV, GLA_DK)
        o_ref[...] = _heads_last(_bmm(_b(sc), v) + _bmm_nt(qb, _b(states)), nc)

    return pl.pallas_call(
        body, name=name, grid=(T // blk,),
        in_specs=_gla_specs(blk, lambda j: j),
        out_specs=[pl.BlockSpec((blk, GLA_H * GLA_DV), lambda j: (j, 0)),
                   pl.BlockSpec((GLA_H, nc, GLA_DV, GLA_DK), lambda j: (0, j, 0, 0))],
        out_shape=[jax.ShapeDtypeStruct((T, GLA_H * GLA_DV), F32),
                   jax.ShapeDtypeStruct((GLA_H, T // CHUNK, GLA_DV, GLA_DK), F32)],
        scratch_shapes=[pltpu.VMEM((GLA_H, GLA_DV, GLA_DK), F32)],
        compiler_params=_cp("arbitrary"),
    )(pbig, pbig, pbig, psmall, wgate, bgate)


def _gla_bwd(pbig, psmall, wgate, bgate, states, do, name):
    T = pbig.shape[0]
    blk = min(ATT_BLK, T)
    nc = blk // CHUNK
    nb = T // blk
    nbat = GLA_H * nc

    def body(q_ref, k_ref, v_ref, sm_ref, wg_ref, bg_ref, ss_ref, do_ref, dq_ref, dk_ref, dv_ref, dpre_ref, dst_ref):
        @pl.when(pl.program_id(0) == 0)
        def _():
            dst_ref[...] = jnp.zeros_like(dst_ref)

        causal = _iota2(CHUNK, CHUNK, 0) >= _iota2(CHUNK, CHUNK, 1)
        triu = (_iota2(CHUNK, CHUNK, 0) <= _iota2(CHUNK, CHUNK, 1)).astype(F32)
        pre, eb, enb, ebl, q_in, k_out, k_st, a_ch = _gla_block(q_ref, k_ref, sm_ref, wg_ref, bg_ref, nc,
                                                                causal.astype(F32))
        v = _heads_first(v_ref[...], nc, GLA_DV)
        dob = _b(_heads_first(do_ref[...], nc, GLA_DV))
        st = ss_ref[...].reshape(nbat, GLA_DV, GLA_DK)
        qb, kob, kstb = _b(q_in), _b(k_out), _b(k_st)
        qdo = _bmm_tn(dob, qb)
        after = [None] * nbat
        for i in range(GLA_H):
            dst = dst_ref[i]
            for c in range(nc - 1, -1, -1):
                n = i * nc + c
                after[n] = dst
                dst = dst * a_ch[n] + qdo[n]
            dst_ref[i] = dst
        dsa = jnp.stack(after)
        dsb = _b(dsa)
        sc = jnp.where(causal, _bmm_nt(qb, kob), 0.0)
        dsc = _b(jnp.where(causal, _bmm_nt(dob, v), 0.0))
        dq_in = _bmm(dob, _b(st)) + _bmm(dsc, kob)
        dk_out = _bmm_tn(dsc, qb)
        dk_st = _bmm(v, dsb)
        dv_ref[...] = _b(_heads_last(_bmm_tn(_b(sc), dob) + _bmm_nt(kstb, dsb), nc))
        da_ch = jnp.sum(st * dsa, axis=1, keepdims=True)
        tk = dk_st * k_st
        db = dq_in * q_in - dk_out * k_out - tk
        db_last = jnp.sum(tk, axis=1, keepdims=True) + da_ch * a_ch
        dq_ref[...] = _b(_heads_last(dq_in * (GLA_DK ** -0.5) * eb, nc))
        dk_ref[...] = _b(_heads_last(dk_out * enb + dk_st * ebl, nc))
        dla = _bmm(jnp.broadcast_to(triu, (nbat, CHUNK, CHUNK)), db, HI) + db_last
        dpre_ref[...] = _heads_last(dla * (1.0 / GLA_TAU) * _sigmoid(-pre), nc)

    r = lambda j: nb - 1 - j
    hk, hv = GLA_H * GLA_DK, GLA_H * GLA_DV
    return pl.pallas_call(
        body, name=name, grid=(nb,),
        in_specs=_gla_specs(blk, r) + [pl.BlockSpec((GLA_H, nc, GLA_DV, GLA_DK), lambda j: (0, r(j), 0, 0)),
                                      pl.BlockSpec((blk, hv), lambda j: (r(j), 0))],
        out_specs=[pl.BlockSpec((blk, hk), lambda j: (r(j), 0)), pl.BlockSpec((blk, hk), lambda j: (r(j), 0)),
                   pl.BlockSpec((blk, hv), lambda j: (r(j), 0)), pl.BlockSpec((blk, hk), lambda j: (r(j), 0))],
        out_shape=[jax.ShapeDtypeStruct((T, hk), BF16), jax.ShapeDtypeStruct((T, hk), BF16),
                   jax.ShapeDtypeStruct((T, hv), BF16), jax.ShapeDtypeStruct((T, hk), F32)],
        scratch_shapes=[pltpu.VMEM((GLA_H, GLA_DV, GLA_DK), F32)],
        compiler_params=_cp("arbitrary"),
    )(pbig, pbig, pbig, psmall, wgate, bgate, states, do)


def _gla_gate_bwd(dpre, psmall, wgate, dsm, name):
    T = dpre.shape[0]
    tm = min(512, T)
    W = GLA_H * GLA_DK
    ngrp = dsm.shape[0]

    def body(dp_ref, sm_ref, wg_ref, dsm_ref, ds_ref, dw_ref, db_ref):
        @pl.when(pl.program_id(0) == 0)
        def _():
            dw_ref[...] = jnp.zeros_like(dw_ref)
            db_ref[...] = jnp.zeros_like(db_ref)

        dp = dp_ref[...]
        ds = _dot_nt(dp, wg_ref[...], HI)
        for i in range(ngrp):
            ds += dsm_ref[i]
        ds_ref[...] = ds
        dw_ref[...] += _dot_tn(sm_ref[...], dp, HI)
        db_ref[...] += jnp.sum(dp, axis=0, keepdims=True)

    return pl.pallas_call(
        body, name=name, grid=(T // tm,),
        in_specs=[pl.BlockSpec((tm, W), lambda i: (i, 0)), pl.BlockSpec((tm, SMALL), lambda i: (i, 0)),
                  pl.BlockSpec((SMALL, W), lambda i: (0, 0)), pl.BlockSpec((ngrp, tm, SMALL), lambda i: (0, i, 0))],
        out_specs=[pl.BlockSpec((tm, SMALL), lambda i: (i, 0)), pl.BlockSpec((SMALL, W), lambda i: (0, 0)),
                   pl.BlockSpec((1, W), lambda i: (0, 0))],
        out_shape=[jax.ShapeDtypeStruct((T, SMALL), F32), jax.ShapeDtypeStruct((SMALL, W), F32),
                   jax.ShapeDtypeStruct((1, W), F32)],
        compiler_params=_cp("arbitrary"),
    )(dpre, psmall, wgate, dsm)


CONV_C = 3 * 1024
CONV_BLK = 256


def _conv_fwd(pbig, cw8, name):
    T = pbig.shape[0]
    blk = min(CONV_BLK, T)

    def body(x_ref, w_ref, c_ref, prev_ref):
        @pl.when(pl.program_id(0) == 0)
        def _():
            prev_ref[...] = jnp.zeros_like(prev_ref)

        x = x_ref[...].astype(F32)
        prev = prev_ref[...]
        row8 = _iota2(8, CONV_C, 0)
        acc = x * w_ref[CONV_K - 1:CONV_K, :]
        for s in range(1, CONV_K):
            xs = pltpu.roll(x, s, 0)
            top = jnp.where(row8 < s, pltpu.roll(prev, s, 0), xs[:8])
            xs = jnp.concatenate([top, xs[8:]], axis=0)
            acc += xs * w_ref[CONV_K - 1 - s:CONV_K - s, :]
        c_ref[...] = _b(acc)
        prev_ref[...] = x[blk - 8:]

    return pl.pallas_call(
        body, name=name, grid=(T // blk,),
        in_specs=[pl.BlockSpec((blk, CONV_C), lambda i: (i, 1)), pl.BlockSpec((8, CONV_C), lambda i: (0, 0))],
        out_specs=pl.BlockSpec((blk, CONV_C), lambda i: (i, 0)),
        out_shape=jax.ShapeDtypeStruct((T, CONV_C), BF16),
        scratch_shapes=[pltpu.VMEM((8, CONV_C), F32)],
        compiler_params=_cp("arbitrary"),
    )(pbig, cw8)


def _conv_bwd(dcq, dck, dcv, pbig, cw8, name):
    T = pbig.shape[0]
    blk = min(CONV_BLK, T)
    nb = T // blk

    def body(dq_ref, dk_ref, dv_ref, x_ref, w_ref, dx_ref, dw_ref, nxt_ref):
        @pl.when(pl.program_id(0) == 0)
        def _():
            nxt_ref[...] = jnp.zeros_like(nxt_ref)
            dw_ref[...] = jnp.zeros_like(dw_ref)

        dc = jnp.concatenate([dq_ref[...], dk_ref[...], dv_ref[...]], axis=1).astype(F32)
        x = x_ref[...].astype(F32)
        nxt = nxt_ref[...]
        row8 = _iota2(8, CONV_C, 0)
        acc = dc * w_ref[CONV_K - 1:CONV_K, :]
        dws = [jnp.sum(dc * x, axis=0, keepdims=True)]
        for s in range(1, CONV_K):
            ds = pltpu.roll(dc, blk - s, 0)
            bot = jnp.where(row8 >= 8 - s, pltpu.roll(nxt, 8 - s, 0), ds[blk - 8:])
            ds = jnp.concatenate([ds[:blk - 8], bot], axis=0)
            acc += ds * w_ref[CONV_K - 1 - s:CONV_K - s, :]
            dws.append(jnp.sum(ds * x, axis=0, keepdims=True))
        dx_ref[...] = _b(acc)
        dw_ref[...] += jnp.concatenate(dws[::-1] + [jnp.zeros((8 - CONV_K, CONV_C), F32)], axis=0)
        nxt_ref[...] = dc[:8]

    part = pl.BlockSpec((blk, 1024), lambda i: (nb - 1 - i, 0))
    return pl.pallas_call(
        body, name=name, grid=(nb,),
        in_specs=[part, part, part, pl.BlockSpec((blk, CONV_C), lambda i: (nb - 1 - i, 1)),
                  pl.BlockSpec((8, CONV_C), lambda i: (0, 0))],
        out_specs=[pl.BlockSpec((blk, CONV_C), lambda i: (nb - 1 - i, 0)), pl.BlockSpec((8, CONV_C), lambda i: (0, 0))],
        out_shape=[jax.ShapeDtypeStruct((T, CONV_C), BF16), jax.ShapeDtypeStruct((8, CONV_C), F32)],
        scratch_shapes=[pltpu.VMEM((8, CONV_C), F32)],
        compiler_params=_cp("arbitrary"),
    )(dcq, dck, dcv, pbig, cw8)


def _col(x, lane):
    sel = _iota2(x.shape[0], x.shape[1], 1) == lane
    return jnp.broadcast_to(jnp.sum(jnp.where(sel, x, 0.0), axis=1, keepdims=True), x.shape)


def _bmm(a, b, prec=None):
    return jnp.einsum("bij,bjk->bik", a, b, preferred_element_type=F32, precision=prec)


def _bmm_nt(a, b, prec=None):
    return jnp.einsum("bij,bkj->bik", a, b, preferred_element_type=F32, precision=prec)


def _bmm_tn(a, b, prec=None):
    return jnp.einsum("bji,bjk->bik", a, b, preferred_element_type=F32, precision=prec)


def _unit_lower_inverse(low):
    eye = (_iota2(CHUNK, CHUNK, 0) == _iota2(CHUNK, CHUNK, 1)).astype(F32)
    xk = -low
    inv = eye + xk
    for _ in range(5):
        xb = _b(xk)
        xk = _bmm(xb, xb)
        inv = inv + _bmm(_b(inv), _b(xk))
    resid = eye - _bmm(eye + low, inv, HI)
    return inv + _bmm(inv, resid, HI)


def _heads_first(x, nc, w=128):
    hb = x.shape[1] // w
    return jnp.concatenate([x[:, i * w:(i + 1) * w].reshape(nc, CHUNK, w) for i in range(hb)], axis=0)


def _heads_last(x, nc):
    hb = x.shape[0] // nc
    return jnp.concatenate([x[i * nc:(i + 1) * nc].reshape(nc * CHUNK, x.shape[2]) for i in range(hb)], axis=1)


def _gdn_block(cq_ref, ck_ref, cv_ref, sm_ref, par_ref, h0, hb, nc, masks):
    causal, strict, tril, eye = masks
    nbat = hb * nc
    cq = _heads_first(cq_ref[...].astype(F32), nc)
    ck = _heads_first(ck_ref[...].astype(F32), nc)
    cv = _heads_first(cv_ref[...].astype(F32), nc)
    sq, sk, sv = _sigmoid(cq), _sigmoid(ck), _sigmoid(cv)
    q, k, v = cq * sq, ck * sk, cv * sv
    rq = lax.rsqrt(jnp.sum(q * q, axis=-1, keepdims=True) + EPS)
    rk = lax.rsqrt(jnp.sum(k * k, axis=-1, keepdims=True) + EPS)
    qh, kn = q * rq, k * rk
    qn = qh * (DN_DK ** -0.5)
    sm = sm_ref[...]
    per_head = lambda fn: jnp.concatenate([fn(i) for i in range(hb)], axis=0)
    braw = per_head(lambda i: _col(sm, GLA_RANK + h0 + i).reshape(nc, CHUNK, 128))
    araw = per_head(lambda i: _col(sm, GLA_RANK + DN_H + h0 + i).reshape(nc, CHUNK, 128))
    ea = per_head(lambda i: jnp.broadcast_to(jnp.exp(par_ref[i, 0:1, :])[None], (nc, 1, 128)))
    bias = per_head(lambda i: jnp.broadcast_to(par_ref[i, 1:2, :][None], (nc, 1, 128)))
    beta = _sigmoid(braw)
    sp_arg = araw + bias
    g = -ea * _softplus(sp_arg)
    G = _bmm(jnp.broadcast_to(tril, (nbat, CHUNK, CHUNK)), g, HI)
    gc = G[:, :, :CHUNK]
    grow = jnp.sum(eye * gc, axis=1, keepdims=True)
    decay = jnp.exp(jnp.where(causal, gc - grow, -1e30))
    kb = kn * beta
    A = _bmm_nt(_b(kb), _b(kn))
    low = jnp.where(strict, A * decay, 0.0)
    tinv = _unit_lower_inverse(low)
    eG = jnp.exp(G)
    gl = G[:, CHUNK - 1:CHUNK, :]
    eGl = jnp.exp(gl - G)
    g_ch = jnp.exp(gl)
    rv = v * beta
    rkk = kb * eG
    tinv_b = _b(tinv)
    u = _bmm(tinv_b, _b(rv))
    w = _bmm(tinv_b, _b(rkk))
    B = _bmm_nt(_b(qn), _b(kn))
    qk = jnp.where(causal, B * decay, 0.0)
    q_dec = qn * eG
    k_st = kn * eGl
    return dict(cq=cq, ck=ck, cv=cv, sq=sq, sk=sk, sv=sv, q=q, k=k, v=v, rq=rq, rk=rk, qh=qh, kn=kn, qn=qn,
                beta=beta, ea=ea, sp_arg=sp_arg, g=g, G=G, decay=decay, kb=kb, A=A, tinv_b=tinv_b, eG=eG, eGl=eGl,
                g_ch=g_ch, rv=rv, rkk=rkk, u=u, w=w, B=B, qk=qk, q_dec=q_dec, k_st=k_st)


def _gdn_masks():
    r, c = _iota2(CHUNK, CHUNK, 0), _iota2(CHUNK, CHUNK, 1)
    return r >= c, r > c, (r >= c).astype(F32), (r == c).astype(F32)


def _gdn_specs(blk, hb, idx):
    ng = DN_H // hb
    return [pl.BlockSpec((blk, hb * DN_DK), lambda h, j: (idx(j), h)),
            pl.BlockSpec((blk, hb * DN_DK), lambda h, j: (idx(j), ng + h)),
            pl.BlockSpec((blk, hb * DN_DV), lambda h, j: (idx(j), 2 * ng + h)),
            pl.BlockSpec((blk, SMALL), lambda h, j: (idx(j), 0)),
            pl.BlockSpec((hb, 8, 128), lambda h, j: (h, 0, 0))]


def _gdn_fwd(conv, psmall, par, name):
    T = conv.shape[0]
    blk = min(GDN_BLK, T)
    nc = blk // CHUNK
    hb = GDN_HEADS

    def body(cq_ref, ck_ref, cv_ref, sm_ref, par_ref, o_ref, ss_ref, s_ref):
        @pl.when(pl.program_id(1) == 0)
        def _():
            s_ref[...] = jnp.zeros_like(s_ref)

        f = _gdn_block(cq_ref, ck_ref, cv_ref, sm_ref, par_ref, pl.program_id(0) * hb, hb, nc, _gdn_masks())
        wb, qdb, kstb, qkb = _b(f["w"]), _b(f["q_dec"]), _b(f["k_st"]), _b(f["qk"])
        S = [s_ref[i] for i in range(hb)]
        for c in range(nc):
            for i in range(hb):
                n = i * nc + c
                ss_ref[i, c] = S[i]
                Sb = _b(S[i])
                v_new = _b(f["u"][n] - _dot(wb[n], Sb))
                o_ref[pl.ds(c * CHUNK, CHUNK), i * DN_DV:(i + 1) * DN_DV] = _dot(qdb[n], Sb) + _dot(qkb[n], v_new)
                S[i] = S[i] * f["g_ch"][n] + _dot_tn(kstb[n], v_new)
        for i in range(hb):
            s_ref[i] = S[i]

    return pl.pallas_call(
        body, name=name, grid=(DN_H // hb, T // blk),
        in_specs=_gdn_specs(blk, hb, lambda j: j),
        out_specs=[pl.BlockSpec((blk, hb * DN_DV), lambda h, j: (j, h)),
                   pl.BlockSpec((hb, nc, DN_DK, DN_DV), lambda h, j: (h, j, 0, 0))],
        out_shape=[jax.ShapeDtypeStruct((T, DN_H * DN_DV), F32),
                   jax.ShapeDtypeStruct((DN_H, T // CHUNK, DN_DK, DN_DV), F32)],
        scratch_shapes=[pltpu.VMEM((hb, DN_DK, DN_DV), F32)],
        compiler_params=_cp("parallel", "arbitrary"),
    )(conv, conv, conv, psmall, par)


def _gdn_bwd(conv, psmall, par, states, do, name, carry=None):
    ex_in, ex_args, ex_out, ex_shape, ex_sems = _carry_specs(carry)
    T = conv.shape[0]
    blk = min(GDN_BLK, T)
    nc = blk // CHUNK
    nb = T // blk
    hb = GDN_HEADS
    nbat = hb * nc
    rsum = lambda x: jnp.sum(x, axis=-1, keepdims=True)

    def body(cq_ref, ck_ref, cv_ref, sm_ref, par_ref, ss_ref, do_ref,
             dcq_ref, dck_ref, dcv_ref, dsm_ref, dpar_ref, ds_ref):
        @pl.when(pl.program_id(1) == 0)
        def _():
            ds_ref[...] = jnp.zeros_like(ds_ref)
            dpar_ref[...] = jnp.zeros_like(dpar_ref)

        masks = _gdn_masks()
        causal, strict, tril, eye = masks
        triu = (_iota2(CHUNK, CHUNK, 0) <= _iota2(CHUNK, CHUNK, 1)).astype(F32)
        lane = _iota2(CHUNK, 128, 1)
        last_row = _iota2(CHUNK, 128, 0) == CHUNK - 1
        h0 = pl.program_id(0) * hb
        f = _gdn_block(cq_ref, ck_ref, cv_ref, sm_ref, par_ref, h0, hb, nc, masks)
        S = ss_ref[...].reshape(nbat, DN_DK, DN_DV)
        Sb = _b(S)
        do_ = _b(_heads_first(do_ref[...], nc))
        wb, qdb, kstb, qkb = _b(f["w"]), _b(f["q_dec"]), _b(f["k_st"]), _b(f["qk"])
        vnb = _b(f["u"] - _bmm(wb, Sb))
        dvn0 = _bmm_tn(qkb, do_)
        qdo = _bmm_tn(qdb, do_)
        dS = [ds_ref[i] for i in range(hb)]
        after = [None] * nbat
        for c in range(nc - 1, -1, -1):
            for i in range(hb):
                n = i * nc + c
                after[n] = dS[i]
                dvn_c = _b(dvn0[n] + _dot(kstb[n], _b(dS[i])))
                dS[i] = dS[i] * f["g_ch"][n] + qdo[n] - _dot_tn(wb[n], dvn_c)
        for i in range(hb):
            ds_ref[i] = dS[i]
        dSa = jnp.stack(after)
        dSb = _b(dSa)
        dvn = dvn0 + _bmm(kstb, dSb)
        dvnb = _b(dvn)
        dq_dec = _bmm_nt(do_, Sb)
        dqk = jnp.where(causal, _bmm_nt(do_, vnb), 0.0)
        dk_st = _bmm_nt(vnb, dSb)
        dg_ch = jnp.sum(rsum(S * dSa), axis=1, keepdims=True)
        dw = -_bmm_nt(dvnb, Sb)
        drv = _bmm_tn(f["tinv_b"], dvnb)
        drk = _bmm_tn(f["tinv_b"], _b(dw))
        dlow = jnp.where(strict, -(_bmm_nt(_b(drv), _b(f["u"])) + _bmm_nt(_b(drk), wb)), 0.0)
        dv = drv * f["beta"]
        dbeta = rsum(drv * f["v"])
        dkb = drk * f["eG"]
        dG = rsum(drk * f["rkk"])
        dA = dlow * f["decay"]
        ddec = dlow * f["A"]
        dkb += _bmm(_b(dA), _b(f["kn"]))
        dkn = _bmm_tn(_b(dA), _b(f["kb"]))
        dB = dqk * f["decay"]
        ddec += dqk * f["B"]
        dqn = _bmm(_b(dB), _b(f["kn"]))
        dkn += _bmm_tn(_b(dB), _b(f["qn"]))
        dD = ddec * f["decay"]
        dG += rsum(dD) - rsum(eye * jnp.sum(dD, axis=1, keepdims=True))
        dqn += dq_dec * f["eG"]
        dG += rsum(dq_dec * f["q_dec"])
        dkn += dk_st * f["eGl"]
        tks = rsum(dk_st * f["k_st"])
        dG -= tks
        dG_last = jnp.sum(tks, axis=1, keepdims=True) + dg_ch * f["g_ch"][:, :, :1]
        dkn += dkb * f["beta"]
        dbeta += rsum(dkb * f["kn"])
        dGf = jnp.broadcast_to(dG, (nbat, CHUNK, 128)) + jnp.where(last_row, dG_last, 0.0)
        dg = _bmm(jnp.broadcast_to(triu, (nbat, CHUNK, CHUNK)), dGf, HI)
        dbraw = dbeta * f["beta"][:, :, :1] * (1.0 - f["beta"][:, :, :1])
        daraw = dg * (-f["ea"]) * _sigmoid(f["sp_arg"])
        both = lambda t: jnp.sum(jnp.sum(t, axis=1, keepdims=True), axis=0)
        dgg = dg * f["g"]
        dsm = jnp.zeros((nc, CHUNK, SMALL), F32)
        for i in range(hb):
            mine = slice(i * nc, (i + 1) * nc)
            dsm += (jnp.where(lane == GLA_RANK + h0 + i, dbraw[mine], 0.0)
                    + jnp.where(lane == GLA_RANK + DN_H + h0 + i, daraw[mine], 0.0))
            dpar = jnp.where(lane[:1] == 0, both(dgg[mine]), jnp.where(lane[:1] == 1, both(daraw[mine]), 0.0))
            dpar_ref[i] += jnp.broadcast_to(dpar, (8, 128))
        dsm_ref[0] = dsm.reshape(blk, SMALL)
        dqh = dqn * (DN_DK ** -0.5)
        dq = f["rq"] * (dqh - f["qh"] * rsum(dqh * f["qh"]))
        dk = f["rk"] * (dkn - f["kn"] * rsum(dkn * f["kn"]))
        dsilu = lambda x, s: s * (1.0 + x * (1.0 - s))
        dcq_ref[...] = _b(_heads_last(dq * dsilu(f["cq"], f["sq"]), nc))
        dck_ref[...] = _b(_heads_last(dk * dsilu(f["ck"], f["sk"]), nc))
        dcv_ref[...] = _b(_heads_last(dv * dsilu(f["cv"], f["sv"]), nc))

    r = lambda j: nb - 1 - j
    out_blk = pl.BlockSpec((blk, hb * DN_DK), lambda h, j: (r(j), h))
    grid = (DN_H // hb, nb)
    return pl.pallas_call(
        _carry(carry, body, 7, 5, grid), name=name, grid=grid,
        in_specs=_gdn_specs(blk, hb, r) + [pl.BlockSpec((hb, nc, DN_DK, DN_DV), lambda h, j: (h, r(j), 0, 0)),
                                          pl.BlockSpec((blk, hb * DN_DV), lambda h, j: (r(j), h))] + ex_in,
        out_specs=[out_blk, out_blk, out_blk, pl.BlockSpec((1, blk, SMALL), lambda h, j: (h, r(j), 0)),
                   pl.BlockSpec((hb, 8, 128), lambda h, j: (h, 0, 0))] + ex_out,
        out_shape=[jax.ShapeDtypeStruct((T, DN_H * DN_DK), BF16)] * 3 + [
            jax.ShapeDtypeStruct((DN_H // hb, T, SMALL), F32), jax.ShapeDtypeStruct((DN_H, 8, 128), F32)] + ex_shape,
        scratch_shapes=[pltpu.VMEM((hb, DN_DK, DN_DV), F32)] + ex_sems,
        compiler_params=_cp("arbitrary", "arbitrary"),
    )(conv, conv, conv, psmall, par, states, do, *ex_args)


def _head_norm(o, w, dv):
    outs, rs = [], []
    for i in range(o.shape[1] // dv):
        oh = o[:, i * dv:(i + 1) * dv]
        r = lax.rsqrt(jnp.mean(oh * oh, axis=-1, keepdims=True) + EPS)
        outs.append(oh * r)
        rs.append(r)
    return outs, rs


def _merge_specs(tm):
    col = lambda c: pl.BlockSpec((tm, D), lambda i: (i, c))
    return [col(0), col(0), col(2), col(6), col(7), col(8),
            pl.BlockSpec((1, GLA_DV), lambda i: (0, 0)), pl.BlockSpec((1, DN_DV), lambda i: (0, 0)),
            pl.BlockSpec((D, D), lambda i: (0, 0))]


def _merge_fwd(h, oa, ob, pbig, gla_hn, dn_hn, wout, name):
    T = h.shape[0]
    tm = min(ROW_BLK, T)

    def body(h_ref, oa_ref, ob_ref, gr_ref, dg_ref, ma_ref, mb_ref, wa_ref, wb_ref, wo_ref, ho_ref, y_ref):
        na, _ = _head_norm(oa_ref[...], wa_ref[...], GLA_DV)
        nbs, _ = _head_norm(ob_ref[...], wb_ref[...], DN_DV)
        hna = jnp.concatenate([t * wa_ref[...] for t in na], axis=1)
        hnb = jnp.concatenate([t * wb_ref[...] for t in nbs], axis=1)
        gr = gr_ref[...].astype(F32)
        dg = dg_ref[...].astype(F32)
        y = (_sigmoid(ma_ref[...].astype(F32)) * hna * (gr * _sigmoid(gr))
             + _sigmoid(mb_ref[...].astype(F32)) * hnb * (dg * _sigmoid(dg)))
        yb = _b(y)
        y_ref[...] = yb
        ho_ref[...] = h_ref[...] + _dot(yb, wo_ref[...])

    row = pl.BlockSpec((tm, D), lambda i: (i, 0))
    return pl.pallas_call(
        body, name=name, grid=(T // tm,),
        in_specs=[row] + _merge_specs(tm),
        out_specs=[row, row],
        out_shape=[jax.ShapeDtypeStruct((T, D), F32), jax.ShapeDtypeStruct((T, D), BF16)],
        compiler_params=_cp("arbitrary"),
    )(h, oa, ob, pbig, pbig, pbig, pbig, gla_hn, dn_hn, wout)


def _merge_bwd(dh, oa, ob, pbig, gla_hn, dn_hn, wout, name):
    T = dh.shape[0]
    tm = min(ROW_BLK, T)

    def branch(dy, o_ref, w_ref, gate_ref, m_ref, dv):
        w = w_ref[...]
        ohat, rs = _head_norm(o_ref[...], w, dv)
        gate = gate_ref[...].astype(F32)
        m = m_ref[...].astype(F32)
        sgate, sm = _sigmoid(gate), _sigmoid(m)
        silu = gate * sgate
        ohat_all = jnp.concatenate(ohat, axis=1)
        hn = jnp.concatenate([t * w for t in ohat], axis=1)
        d_on = dy * sm
        d_m = dy * hn * silu * sm * (1.0 - sm)
        d_hn = d_on * silu
        d_gate = d_on * hn * (sgate * (1.0 + gate * (1.0 - sgate)))
        dw = jnp.zeros((1, dv), F32)
        d_o = []
        for i, (oh, r) in enumerate(zip(ohat, rs)):
            dhn = d_hn[:, i * dv:(i + 1) * dv]
            dw += jnp.sum(dhn * oh, axis=0, keepdims=True)
            dohat = dhn * w
            d_o.append(r * (dohat - oh * jnp.mean(dohat * oh, axis=-1, keepdims=True)))
        return jnp.concatenate(d_o, axis=1), d_gate, d_m, dw

    def body(dh_ref, oa_ref, ob_ref, gr_ref, dg_ref, ma_ref, mb_ref, wa_ref, wb_ref, wo_ref,
             doa_ref, dob_ref, dgr_ref, ddg_ref, dma_ref, dmb_ref, dwa_ref, dwb_ref, dhb_ref):
        @pl.when(pl.program_id(0) == 0)
        def _():
            dwa_ref[...] = jnp.zeros_like(dwa_ref)
            dwb_ref[...] = jnp.zeros_like(dwb_ref)

        dhb = _b(dh_ref[...])
        dhb_ref[...] = dhb
        dy = _dot_nt(dhb, wo_ref[...])
        d_oa, d_gr, d_ma, dwa = branch(dy, oa_ref, wa_ref, gr_ref, ma_ref, GLA_DV)
        d_ob, d_dg, d_mb, dwb = branch(dy, ob_ref, wb_ref, dg_ref, mb_ref, DN_DV)
        doa_ref[...] = d_oa
        dob_ref[...] = d_ob
        dgr_ref[...] = _b(d_gr)
        ddg_ref[...] = _b(d_dg)
        dma_ref[...] = _b(d_ma)
        dmb_ref[...] = _b(d_mb)
        dwa_ref[...] += dwa
        dwb_ref[...] += dwb

    row = pl.BlockSpec((tm, D), lambda i: (i, 0))
    f32 = jax.ShapeDtypeStruct((T, D), F32)
    b16 = jax.ShapeDtypeStruct((T, D), BF16)
    return pl.pallas_call(
        body, name=name, grid=(T // tm,),
        in_specs=[row] + _merge_specs(tm),
        out_specs=[row] * 6 + [pl.BlockSpec((1, GLA_DV), lambda i: (0, 0)), pl.BlockSpec((1, DN_DV), lambda i: (0, 0)), row],
        out_shape=[f32, f32, b16, b16, b16, b16, jax.ShapeDtypeStruct((1, GLA_DV), F32),
                   jax.ShapeDtypeStruct((1, DN_DV), F32), b16],
        compiler_params=_cp("arbitrary"),
    )(dh, oa, ob, pbig, pbig, pbig, pbig, gla_hn, dn_hn, wout)


def _loss_head(h, nw, target, name):
    T = h.shape[0]
    tm = min(512, T)

    def body(h_ref, nw_ref, t_ref, dx_ref, loss_ref, dnw_ref):
        @pl.when(pl.program_id(0) == 0)
        def _():
            loss_ref[...] = jnp.zeros_like(loss_ref)
            dnw_ref[...] = jnp.zeros_like(dnw_ref)

        x = h_ref[...]
        w = nw_ref[...]
        r = lax.rsqrt(jnp.mean(x * x, axis=-1, keepdims=True) + EPS)
        xhat = x * r
        err = xhat * w - t_ref[...]
        part = jnp.sum(jnp.sum(err * err, axis=-1, keepdims=True), axis=0, keepdims=True)
        loss_ref[...] += (0.5 / D) * part
        dout = err * (1.0 / D)
        dnw_ref[...] += jnp.sum(dout * xhat, axis=0, keepdims=True)
        dxhat = dout * w
        dx_ref[...] = r * (dxhat - xhat * jnp.mean(dxhat * xhat, axis=-1, keepdims=True))

    row = pl.BlockSpec((tm, D), lambda i: (i, 0))
    one = pl.BlockSpec((1, D), lambda i: (0, 0))
    return pl.pallas_call(
        body, name=name, grid=(T // tm,),
        in_specs=[row, one, row],
        out_specs=[row, pl.BlockSpec((8, 128), lambda i: (0, 0)), one],
        out_shape=[jax.ShapeDtypeStruct((T, D), F32), jax.ShapeDtypeStruct((8, 128), F32),
                   jax.ShapeDtypeStruct((1, D), F32)],
        compiler_params=_cp("arbitrary"),
    )(h, nw, target)


def _adamw(w, g, m, v, rows, name):
    R, C = w.shape
    rows = min(rows, R)
    c1 = 1.0 - ADAM_B1 ** ADAM_STEP
    c2 = 1.0 - ADAM_B2 ** ADAM_STEP

    def body(w_ref, g_ref, m_ref, v_ref, d_ref, mo_ref, vo_ref):
        g_ = g_ref[...]
        m_ = ADAM_B1 * m_ref[...] + (1.0 - ADAM_B1) * g_
        v_ = ADAM_B2 * v_ref[...] + (1.0 - ADAM_B2) * (g_ * g_)
        mo_ref[...] = m_
        vo_ref[...] = v_
        d_ref[...] = -ADAM_LR * ((m_ / c1) / (jnp.sqrt(v_ / c2) + ADAM_EPS) + ADAM_WD * w_ref[...])

    blk = pl.BlockSpec((rows, C), lambda i: (i, 0))
    shp = jax.ShapeDtypeStruct((R, C), F32)
    return pl.pallas_call(
        body, name=name, grid=(R // rows,),
        in_specs=[blk] * 4, out_specs=[blk] * 3, out_shape=[shp] * 3,
        compiler_params=_cp("parallel"),
    )(w, g, m, v)


def _me():
    return lax.axis_index("x"), lax.axis_index("y"), lax.axis_index("c")


def _other_chips(x, y):
    return [(1 - x, y), (x, 1 - y), (1 - x, 1 - y)]


def _half_rows(ref, hf):
    half = ref.shape[-2] // 2
    rows = pl.ds(pl.multiple_of(hf * half, 16), half)
    return ref.at[rows, :] if len(ref.shape) == 2 else ref.at[:, rows, :]


class _GatherBig:
    def __init__(self, big):
        self.arrays = list(big)
        self.out_shape = [jax.ShapeDtypeStruct((N_SHARD,) + w.shape, w.dtype) for w in big]
        self.n_sem = 7 * len(big)

    @staticmethod
    def _copy(sems, k, src, dst, to):
        return pltpu.make_async_remote_copy(src_ref=src, dst_ref=dst, send_sem=sems[0].at[k], recv_sem=sems[1].at[k],
                                            device_id=to, device_id_type=MESH)

    def start(self, ins, outs, *sems):
        x, y, c = _me()
        mine = 2 * x + y
        for i, (w_ref, o_ref) in enumerate(zip(ins, outs)):
            self._copy(sems, 7 * i + 6, w_ref, o_ref.at[mine], (x, y, 1 - c)).start()
            for j, chip in enumerate(_other_chips(x, y)):
                self._copy(sems, 7 * i + j, _half_rows(w_ref, c), _half_rows(o_ref.at[mine], c), (*chip, c)).start()

    def finish(self, ins, outs, *sems):
        x, y, c = _me()
        me, sibling = (x, y, c), (x, y, 1 - c)
        chips = _other_chips(x, y)
        slot = lambda chip: 2 * chip[0] + chip[1]
        for i, o_ref in enumerate(outs):
            for j, chip in enumerate(chips):
                landed = _half_rows(o_ref.at[slot(chip)], c)
                self._copy(sems, 7 * i + j, landed, landed, me).wait_recv()
                self._copy(sems, 7 * i + 3 + j, landed, landed, sibling).start()
        for i, (w_ref, o_ref) in enumerate(zip(ins, outs)):
            for j, chip in enumerate(chips):
                passed = _half_rows(o_ref.at[slot(chip)], 1 - c)
                self._copy(sems, 7 * i + 3 + j, passed, passed, me).wait_recv()
            self._copy(sems, 7 * i + 6, o_ref.at[slot((x, y))], o_ref.at[slot((x, y))], me).wait_recv()
        for i, (w_ref, o_ref) in enumerate(zip(ins, outs)):
            self._copy(sems, 7 * i + 6, w_ref, o_ref.at[slot((x, y))], sibling).wait_send()
            for j, chip in enumerate(chips):
                self._copy(sems, 7 * i + j, _half_rows(w_ref, c), _half_rows(o_ref.at[slot((x, y))], c),
                           (*chip, c)).wait_send()
                landed = _half_rows(o_ref.at[slot(chip)], c)
                self._copy(sems, 7 * i + 3 + j, landed, landed, sibling).wait_send()


class _ChipsExchange:
    def __init__(self, pbs):
        self.arrays = list(pbs)
        self.out_shape = [jax.ShapeDtypeStruct((3,) + p.shape[1:], p.dtype) for p in pbs]
        self.n_sem = 3 * len(pbs)

    def _copies(self, ins, outs, send_sems, recv_sems):
        x, y, c = _me()
        return [pltpu.make_async_remote_copy(src_ref=ins[i].at[2 * chip[0] + chip[1]], dst_ref=outs[i].at[j],
                                             send_sem=send_sems.at[3 * i + j], recv_sem=recv_sems.at[3 * i + j],
                                             device_id=(*chip, c), device_id_type=MESH)
                for i in range(len(ins)) for j, chip in enumerate(_other_chips(x, y))]

    def start(self, ins, outs, *sems):
        for cp in self._copies(ins, outs, *sems):
            cp.start()

    def finish(self, ins, outs, *sems):
        for cp in self._copies(ins, outs, *sems):
            cp.wait()


def _carry(ex, body, n_in, n_out, grid):
    if ex is None:
        return body
    ni, no = len(ex.arrays), len(ex.out_shape)

    def carried(*refs):
        ins, ex_in = refs[:n_in], refs[n_in:n_in + ni]
        outs, ex_out = refs[n_in + ni:n_in + ni + n_out], refs[n_in + ni + n_out:n_in + ni + n_out + no]
        scratch, sems = refs[n_in + ni + n_out + no:-2], refs[-2:]
        ids = [pl.program_id(a) for a in range(len(grid))]
        first = functools.reduce(jnp.logical_and, [i == 0 for i in ids])
        last = functools.reduce(jnp.logical_and, [i == g - 1 for i, g in zip(ids, grid)])

        @pl.when(first)
        def _():
            ex.start(ex_in, ex_out, *sems)

        body(*ins, *outs, *scratch)

        @pl.when(last)
        def _():
            ex.finish(ex_in, ex_out, *sems)

    return carried


def _carry_specs(ex):
    if ex is None:
        return [], [], [], [], []
    sems = [pltpu.SemaphoreType.DMA((ex.n_sem,)), pltpu.SemaphoreType.DMA((ex.n_sem,))]
    return [ANY] * len(ex.arrays), ex.arrays, [ANY] * len(ex.out_shape), ex.out_shape, sems


def _gather_weights(big, small, name):
    nbig, nsm = len(big), len(small)
    n = nbig + nsm
    own_sem = 6 * nbig + 3 * nsm

    def body(*refs):
        ins, outs = refs[:n], refs[n:2 * n]
        send_sems, recv_sems = refs[2 * n:]
        x, y, c = _me()
        sibling = (x, y, 1 - c)
        chips = _other_chips(x, y)
        slot = lambda chip: 2 * chip[0] + chip[1]

        def copy(k, src, dst, to):
            return pltpu.make_async_remote_copy(src_ref=src, dst_ref=dst, send_sem=send_sems.at[k],
                                                recv_sem=recv_sems.at[k], device_id=to, device_id_type=MESH)

        sent = []
        for i in range(nbig):
            sent.append(copy(own_sem + i, ins[i], outs[i].at[slot((x, y))], sibling))
            sent[-1].start()
            for j, chip in enumerate(chips):
                sent.append(copy(6 * i + j, _half_rows(ins[i], c), _half_rows(outs[i].at[slot((x, y))], c), (*chip, c)))
                sent[-1].start()
        for t in range(nsm):
            w_ref, o_ref = ins[nbig + t], outs[nbig + t]
            o_ref[slot((x, y))] = w_ref[...]
            for j, chip in enumerate(chips):
                sent.append(copy(6 * nbig + 3 * t + j, w_ref, o_ref.at[slot((x, y))], (*chip, c)))
                sent[-1].start()
        for i in range(nbig):
            for j, chip in enumerate(chips):
                landed = _half_rows(outs[i].at[slot(chip)], c)
                copy(6 * i + j, landed, landed, (x, y, c)).wait_recv()
                sent.append(copy(6 * i + 3 + j, landed, landed, sibling))
                sent[-1].start()
        for t in range(nsm):
            for j, chip in enumerate(chips):
                landed = outs[nbig + t].at[slot(chip)]
                copy(6 * nbig + 3 * t + j, landed, landed, (x, y, c)).wait_recv()
        for i in range(nbig):
            for j, chip in enumerate(chips):
                passed = _half_rows(outs[i].at[slot(chip)], 1 - c)
                copy(6 * i + 3 + j, passed, passed, (x, y, c)).wait_recv()
        for i in range(nbig):
            mine = outs[i].at[slot((x, y))]
            copy(own_sem + i, mine, mine, (x, y, c)).wait_recv()
        for cp in sent:
            cp.wait_send()

    vm = pl.BlockSpec(memory_space=pltpu.VMEM)
    nsem = own_sem + nbig
    return pl.pallas_call(
        body, name=name, in_specs=[ANY] * nbig + [vm] * nsm, out_specs=[ANY] * nbig + [vm] * nsm,
        out_shape=[jax.ShapeDtypeStruct((N_SHARD,) + w.shape, w.dtype) for w in list(big) + list(small)],
        scratch_shapes=[pltpu.SemaphoreType.DMA((nsem,)), pltpu.SemaphoreType.DMA((nsem,))],
        compiler_params=pltpu.CompilerParams(has_side_effects=True),
    )(*big, *small)


def _rs_sibling(gs, name):
    n = len(gs)

    def body(*refs):
        send_sems, recv_sems = refs[2 * n:]
        x, y, c = _me()
        cps = [pltpu.make_async_remote_copy(src_ref=_half_rows(refs[i], 1 - c), dst_ref=refs[n + i],
                                            send_sem=send_sems.at[i], recv_sem=recv_sems.at[i],
                                            device_id=(x, y, 1 - c), device_id_type=MESH) for i in range(n)]
        for cp in cps:
            cp.start()
        for cp in cps:
            cp.wait()

    return pl.pallas_call(
        body, name=name, in_specs=[ANY] * n, out_specs=[ANY] * n,
        out_shape=[jax.ShapeDtypeStruct((g.shape[0], g.shape[1] // 2, g.shape[2]), g.dtype) for g in gs],
        scratch_shapes=[pltpu.SemaphoreType.DMA((n,)), pltpu.SemaphoreType.DMA((n,))],
        compiler_params=pltpu.CompilerParams(has_side_effects=True),
    )(*gs)


def _add_pair(g, other, where, name):
    ns, a, b = g.shape
    half = a // 2

    def body(w_ref, g_ref, o_ref, pb_ref, own_ref):
        t = g_ref[0].astype(F32) + o_ref[0].astype(F32)
        pb_ref[0] = _b(t)

        @pl.when(pl.program_id(0) == w_ref[1])
        def _():
            own_ref[...] = t

    return pl.pallas_call(
        body, name=name,
        grid_spec=pltpu.PrefetchScalarGridSpec(
            num_scalar_prefetch=1, grid=(ns,),
            in_specs=[pl.BlockSpec((1, half, b), lambda s, w: (s, w[0], 0)), pl.BlockSpec((1, half, b), lambda s, w: (s, 0, 0))],
            out_specs=[pl.BlockSpec((1, half, b), lambda s, w: (s, 0, 0)), pl.BlockSpec((half, b), lambda s, w: (0, 0))]),
        out_shape=[jax.ShapeDtypeStruct((ns, half, b), BF16), jax.ShapeDtypeStruct((half, b), F32)],
        compiler_params=_cp("arbitrary"),
    )(where, g, other)


def _rs_chips(pbs, name):
    n = len(pbs)

    def body(*refs):
        send_sems, recv_sems = refs[2 * n:]
        x, y, c = _me()
        cps = [pltpu.make_async_remote_copy(src_ref=refs[i].at[2 * chip[0] + chip[1]], dst_ref=refs[n + i].at[j],
                                            send_sem=send_sems.at[3 * i + j], recv_sem=recv_sems.at[3 * i + j],
                                            device_id=(*chip, c), device_id_type=MESH)
               for i in range(n) for j, chip in enumerate(_other_chips(x, y))]
        for cp in cps:
            cp.start()
        for cp in cps:
            cp.wait()

    return pl.pallas_call(
        body, name=name, in_specs=[ANY] * n, out_specs=[ANY] * n,
        out_shape=[jax.ShapeDtypeStruct((3,) + p.shape[1:], p.dtype) for p in pbs],
        scratch_shapes=[pltpu.SemaphoreType.DMA((3 * n,)), pltpu.SemaphoreType.DMA((3 * n,))],
        compiler_params=pltpu.CompilerParams(has_side_effects=True),
    )(*pbs)


def _add_four(own, got, name):
    rows, cols = own.shape
    rb = rows // 2

    def body(a_ref, b_ref, o_ref):
        o_ref[...] = ((a_ref[...] + b_ref[0].astype(F32)) + b_ref[1].astype(F32)) + b_ref[2].astype(F32)

    return pl.pallas_call(
        body, name=name, grid=(rows // rb,),
        in_specs=[pl.BlockSpec((rb, cols), lambda i: (i, 0)), pl.BlockSpec((3, rb, cols), lambda i: (0, i, 0))],
        out_specs=pl.BlockSpec((rb, cols), lambda i: (i, 0)),
        out_shape=jax.ShapeDtypeStruct((rows, cols), F32),
        compiler_params=_cp("parallel"),
    )(own, got)


def _rs_swap(halves, name):
    n = len(halves)

    def body(*refs):
        send_sems, recv_sems = refs[2 * n:]
        x, y, c = _me()
        cps = [pltpu.make_async_remote_copy(src_ref=refs[i], dst_ref=refs[n + i], send_sem=send_sems.at[i],
                                            recv_sem=recv_sems.at[i], device_id=(x, y, 1 - c), device_id_type=MESH)
               for i in range(n)]
        for cp in cps:
            cp.start()
        for cp in cps:
            cp.wait()

    return pl.pallas_call(
        body, name=name, in_specs=[ANY] * n, out_specs=[ANY] * n,
        out_shape=[jax.ShapeDtypeStruct(h.shape, h.dtype) for h in halves],
        scratch_shapes=[pltpu.SemaphoreType.DMA((n,)), pltpu.SemaphoreType.DMA((n,))],
        compiler_params=pltpu.CompilerParams(has_side_effects=True),
    )(*halves)


def _adamw_halves(w, own, got, m, v, rows, name):
    a, b = w.shape
    nblk = a // 2 // rows
    c1 = 1.0 - ADAM_B1 ** ADAM_STEP
    c2 = 1.0 - ADAM_B2 ** ADAM_STEP

    def body(w_ref, own_ref, got_ref, m_ref, v_ref, g_ref, d_ref, mo_ref, vo_ref):
        g_ = jnp.where(pl.program_id(0) == lax.axis_index("c"), own_ref[...], got_ref[...])
        g_ref[...] = g_
        m_ = ADAM_B1 * m_ref[...] + (1.0 - ADAM_B1) * g_
        v_ = ADAM_B2 * v_ref[...] + (1.0 - ADAM_B2) * (g_ * g_)
        mo_ref[...] = m_
        vo_ref[...] = v_
        d_ref[...] = -ADAM_LR * ((m_ / c1) / (jnp.sqrt(v_ / c2) + ADAM_EPS) + ADAM_WD * w_ref[...])

    whole = pl.BlockSpec((rows, b), lambda h, i: (h * nblk + i, 0))
    part = pl.BlockSpec((rows, b), lambda h, i: (i, 0))
    shp = jax.ShapeDtypeStruct((a, b), F32)
    return pl.pallas_call(
        body, name=name, grid=(2, nblk),
        in_specs=[whole, part, part, whole, whole], out_specs=[whole] * 4, out_shape=[shp] * 4,
        compiler_params=_cp("parallel", "parallel"),
    )(w, own, got, m, v)


def _allsum_small(vec, name):
    def body(v_ref, o_ref, buf_ref, send_sems, recv_sems):
        x, y, c = _me()
        me = 4 * x + 2 * y + c
        buf_ref[me] = v_ref[...]
        cps = []
        for k in range(1, 8):
            peer = (x ^ (k >> 2), y ^ ((k >> 1) & 1), c ^ (k & 1))
            cps.append(pltpu.make_async_remote_copy(src_ref=v_ref, dst_ref=buf_ref.at[me],
                                                    send_sem=send_sems.at[k - 1], recv_sem=recv_sems.at[k - 1],
                                                    device_id=peer, device_id_type=MESH))
        for cp in cps:
            cp.start()
        for k in range(1, 8):
            peer_idx = me ^ k
            pltpu.make_async_remote_copy(src_ref=v_ref, dst_ref=buf_ref.at[peer_idx],
                                         send_sem=send_sems.at[k - 1], recv_sem=recv_sems.at[k - 1],
                                         device_id=(x, y, c), device_id_type=MESH).wait_recv()
        for cp in cps:
            cp.wait_send()
        acc = buf_ref[0]
        for d in range(1, 8):
            acc = acc + buf_ref[d]
        o_ref[...] = acc

    return pl.pallas_call(
        body, name=name,
        in_specs=[pl.BlockSpec(memory_space=pltpu.VMEM)], out_specs=pl.BlockSpec(memory_space=pltpu.VMEM),
        out_shape=jax.ShapeDtypeStruct(vec.shape, F32),
        scratch_shapes=[pltpu.VMEM((8,) + vec.shape, F32), pltpu.SemaphoreType.DMA((7,)), pltpu.SemaphoreType.DMA((7,))],
        compiler_params=pltpu.CompilerParams(has_side_effects=True),
    )(vec)


BIG = ("ffn1_w_gate", "ffn1_w_up", "ffn1_w_down", "w_in", "w_out", "ffn2_w_gate", "ffn2_w_up", "ffn2_w_down")
TINY = ("w_gla_gate", "conv_w")
SHARDED = BIG + TINY


def _join_cols(w4):
    return jnp.transpose(w4, (1, 0, 2)).reshape(w4.shape[1], N_SHARD * w4.shape[2])


def _cut_cols(w):
    return jnp.transpose(w.reshape(w.shape[0], N_SHARD, w.shape[1] // N_SHARD), (1, 0, 2))


def _split_w_in(w):
    o = IN_OFF
    big = jnp.concatenate([w[:, :o[4]], w[:, o[5]:o[9]], w[:, o[11]:]], axis=1)
    small = jnp.concatenate([w[:, o[4]:o[5]], w[:, o[9]:o[11]], jnp.zeros((w.shape[0], SMALL - 32), w.dtype)], axis=1)
    return big, small


def _merge_w_in(big, small):
    return jnp.concatenate([big[:, :3072], small[:, :16], big[:, 3072:7168], small[:, 16:32], big[:, 7168:]], axis=1)


class _Comm:
    def __init__(self, where, rest_shards):
        self.where = where
        self.rest = _GatherBig(rest_shards)
        self.pairs, self.got = {}, {}

    @staticmethod
    def weights(gathered):
        G = dict(zip(BIG[3:], gathered))
        W = {n: G[n] for n in BIG[3:] if n.startswith("ffn")}
        W["w_out"] = G["w_out"].reshape(D, D)
        W["w_in_big"], W["w_in_small"] = _split_w_in(_join_cols(G["w_in"]))
        return W

    def begin(self, names, grads):
        from_sibling = _rs_sibling(grads, "rs_sibling_" + names[0])
        for n, g, o in zip(names, grads, from_sibling):
            self.pairs[n] = _add_pair(g, o, self.where, "rs_pair_" + n)
        return _ChipsExchange([self.pairs[n][0] for n in names])

    def landed(self, names, outs):
        self.got.update(zip(names, outs))

    def last(self, names, grads):
        ex = self.begin(names, grads)
        self.landed(names, _rs_chips(ex.arrays, "rs_chips"))


def _local_step(x, target, W, P, comm=None):
    wgate_pad = jnp.zeros((SMALL, GLA_H * GLA_DK), F32).at[:GLA_RANK].set(P["w_gla_gate"])
    cw8 = jnp.zeros((8, CONV_C), F32).at[:CONV_K].set(P["conv_w"])
    par = jnp.zeros((DN_H, 8, 128), F32)
    par = par.at[:, 0, :].set(jnp.broadcast_to(P["dn_a_log"].reshape(DN_H, 1), (DN_H, 128)))
    par = par.at[:, 1, :].set(jnp.broadcast_to(P["dn_dt_bias"].reshape(DN_H, 1), (DN_H, 128)))

    h1, n1, g1, u1, *rest = _ffn_fwd(x, P["ffn1_norm"], W["ffn1_w_gate"], W["ffn1_w_up"], W["ffn1_w_down"], "ffn1_fwd",
                                     carry=comm.rest if comm else None)
    if comm:
        W = dict(W, **comm.weights(rest))
    wbig, wsmall = W["w_in_big"], W["w_in_small"]
    pbig, psmall, n2 = _norm_proj(h1, P["mix_norm"], wbig, wsmall, "mix_proj")
    oa, sa = _gla_fwd(pbig, psmall, wgate_pad, P["b_gla_gate"], "gla_fwd")
    conv = _conv_fwd(pbig, cw8, "conv_fwd")
    ob, sb = _gdn_fwd(conv, psmall, par, "gdn_fwd")
    h2, yb = _merge_fwd(h1, oa, ob, pbig, P["gla_head_norm"], P["dn_head_norm"], W["w_out"], "merge_fwd")
    h3, n3, g3, u3 = _ffn_fwd(h2, P["ffn2_norm"], W["ffn2_w_gate"], W["ffn2_w_up"], W["ffn2_w_down"], "ffn2_fwd")
    dh3, loss, d_final = _loss_head(h3, P["final_norm"], target, "loss_head")

    gw, gs = {}, {"final_norm": d_final}

    def ffn_grads(tag, dh, h, n, g, u, carry=None):
        dx, dg, du, act, dfb, dnw, *landed = _ffn_bwd(dh, h, P[tag + "_norm"], g, u, W[tag + "_w_gate"],
                                                      W[tag + "_w_up"], W[tag + "_w_down"], tag + "_bwd", carry=carry)
        gw[tag + "_w_gate"] = _mm_tn(n, dg, D, FF_CUT, tag + "_dwg")
        gw[tag + "_w_up"] = _mm_tn(n, du, D, FF_CUT, tag + "_dwu")
        gw[tag + "_w_down"] = _mm_tn(act, dfb, FF_CUT, D, tag + "_dwd")
        gs[tag + "_norm"] = dnw
        return dx, landed

    dh2, _ = ffn_grads("ffn2", dh3, h2, n3, g3, u3)
    d_oa, d_ob, d_gr, d_dgate, d_ma, d_mb, gs["gla_head_norm"], gs["dn_head_norm"], dh2b = _merge_bwd(
        dh2, oa, ob, pbig, P["gla_head_norm"], P["dn_head_norm"], W["w_out"], "merge_bwd")
    gw["w_out"] = _mm_tn(yb, dh2b, D, D, "dw_out").reshape(N_SHARD, D // N_SHARD, D)
    early = ("ffn2_w_gate", "ffn2_w_up", "ffn2_w_down", "w_out")
    d_gq, d_gk, d_gv, dpre = _gla_bwd(pbig, psmall, wgate_pad, P["b_gla_gate"], sa, d_oa, "gla_bwd")
    dcq, dck, dcv, dsm, dpar, *landed = _gdn_bwd(conv, psmall, par, sb, d_ob, "gdn_bwd",
                                                 carry=comm.begin(early, [gw[n] for n in early]) if comm else None)
    if comm:
        comm.landed(early, landed)
    dsmall, dwgate, gs["b_gla_gate"] = _gla_gate_bwd(dpre, psmall, wgate_pad, dsm, "gla_gate_bwd")
    gs["w_gla_gate"] = dwgate[:GLA_RANK]
    d_x3, dcw = _conv_bwd(dcq, dck, dcv, pbig, cw8, "conv_bwd")
    gs["conv_w"] = dcw[:CONV_K]
    gs["dn_a_log"] = dpar[:, 0, 0].reshape(1, DN_H)
    gs["dn_dt_bias"] = dpar[:, 0, 1].reshape(1, DN_H)
    pieces = (d_gq, d_gk, d_gv, d_gr, d_x3, d_dgate, d_ma, d_mb)
    dh1, gs["mix_norm"] = _proj_bwd(dh2, h1, P["mix_norm"], pieces, dsmall, wbig, wsmall, "proj_bwd")
    dbig = jnp.concatenate([_mm_tn(n2, p, D, 1024, "dw_in_%d" % i) for i, p in enumerate(pieces)], axis=1)
    dsml = _mm_tn(n2, dsmall, D, SMALL, "dw_in_small")
    gw["w_in"] = _cut_cols(_merge_w_in(dbig, dsml))
    grad_x, landed = ffn_grads("ffn1", dh1, x, n1, g1, u1,
                               carry=comm.begin(("w_in",), [gw["w_in"]]) if comm else None)
    if comm:
        comm.landed(("w_in",), landed)
        comm.last(BIG[:3], [gw[n] for n in BIG[:3]])
    return loss, grad_x, gw, gs


SMALL_NAMES = ("ffn1_norm", "mix_norm", "ffn2_norm", "final_norm", "b_gla_gate", "gla_head_norm", "dn_head_norm",
               "dn_a_log", "dn_dt_bias")
ROW4 = (("b_gla_gate", 512), ("gla_head_norm", 256), ("dn_head_norm", 128), ("dn_a_log", 8), ("dn_dt_bias", 8))


def _pack_small(d, loss=None):
    row4 = [d[n].reshape(-1) for n, _ in ROW4]
    row4.append(jnp.zeros((1,), F32) if loss is None else loss.reshape(1))
    row4 = jnp.concatenate(row4)
    row4 = jnp.pad(row4, (0, D - row4.shape[0]))
    rows = [d[n].reshape(-1) for n in SMALL_NAMES[:4]] + [row4]
    return jnp.concatenate([jnp.stack(rows), jnp.zeros((3, D), F32)], axis=0)


def _unpack_small(a, like):
    out = {n: a[i].reshape(like[n].shape) for i, n in enumerate(SMALL_NAMES[:4])}
    off = 0
    for n, w in ROW4:
        out[n] = a[4, off:off + w].reshape(like[n].shape)
        off += w
    return out, a[4, off]


WEIGHT_ORDER = ("ffn1_norm", "ffn1_w_gate", "ffn1_w_up", "ffn1_w_down", "mix_norm", "w_in", "w_gla_gate", "b_gla_gate",
                "conv_w", "dn_a_log", "dn_dt_bias", "gla_head_norm", "dn_head_norm", "w_out", "ffn2_norm",
                "ffn2_w_gate", "ffn2_w_up", "ffn2_w_down", "final_norm")
ADAM_ROWS = {"ffn1_w_gate": 256, "ffn1_w_up": 256, "ffn1_w_down": 176, "w_in": 128, "w_gla_gate": 16, "conv_w": 4,
             "w_out": 64, "ffn2_w_gate": 256, "ffn2_w_up": 256, "ffn2_w_down": 176}


def kernel(x, ffn1_norm, ffn1_w_gate, ffn1_w_up, ffn1_w_down, mix_norm, w_in, w_gla_gate, b_gla_gate, conv_w, dn_a_log, dn_dt_bias, gla_head_norm, dn_head_norm, w_out, ffn2_norm, ffn2_w_gate, ffn2_w_up, ffn2_w_down, final_norm, loss_target, m_ffn1_norm, m_ffn1_w_gate, m_ffn1_w_up, m_ffn1_w_down, m_mix_norm, m_w_in, m_w_gla_gate, m_b_gla_gate, m_conv_w, m_dn_a_log, m_dn_dt_bias, m_gla_head_norm, m_dn_head_norm, m_w_out, m_ffn2_norm, m_ffn2_w_gate, m_ffn2_w_up, m_ffn2_w_down, m_final_norm, v_ffn1_norm, v_ffn1_w_gate, v_ffn1_w_up, v_ffn1_w_down, v_mix_norm, v_w_in, v_w_gla_gate, v_b_gla_gate, v_conv_w, v_dn_a_log, v_dn_dt_bias, v_gla_head_norm, v_dn_head_norm, v_w_out, v_ffn2_norm, v_ffn2_w_gate, v_ffn2_w_up, v_ffn2_w_down, v_final_norm):
    given = dict(locals())
    wts = {n: given[n] for n in WEIGHT_ORDER}
    moms = {n: given["m_" + n] for n in WEIGHT_ORDER}
    vars_ = {n: given["v_" + n] for n in WEIGHT_ORDER}
    two_d = lambda a: a.reshape(a.shape[-2], a.shape[-1]) if a.ndim == 3 else a.reshape(1, -1)
    shard = {n: two_d(wts[n]) for n in SHARDED}

    gathered = _gather_weights([shard[n].astype(BF16) for n in BIG[:3]], [shard[n] for n in TINY], "gather_first")
    W = dict(zip(BIG[:3], gathered))
    P = {n: two_d(wts[n]) for n in SMALL_NAMES}
    for n, g in zip(TINY, gathered[3:]):
        P[n] = _join_cols(g)

    my_slot = 2 * lax.axis_index("x") + lax.axis_index("y")
    where = jnp.stack([lax.axis_index("c"), my_slot]).astype(jnp.int32)
    comm = _Comm(where, [shard[n].astype(BF16) for n in BIG[3:]])
    loss, grad_x, gw, gs = _local_step(x[0], loss_target[0], W, P, comm)
    halves = [_add_four(comm.pairs[n][1], comm.got[n], "rs_four_" + n) for n in BIG]
    other_halves = _rs_swap(halves, "rs_swap")

    tiny_rows = jnp.concatenate([gs["w_gla_gate"].reshape(8, D), gs["conv_w"].reshape(12, D), jnp.zeros((4, D), F32)])
    all_sum = _allsum_small(jnp.concatenate([_pack_small(gs, loss[0, 0]), tiny_rows]), "allsum_small")
    small_sum = all_sum[:8]
    small_g, loss_total = _unpack_small(small_sum, P)

    grads, delta, new_m, new_v = {}, {}, {}, {}
    for n, own, got in zip(BIG, halves, other_halves):
        res = _adamw_halves(shard[n], own, got, two_d(moms[n]), two_d(vars_[n]), ADAM_ROWS[n], "adamw_" + n)
        grads[n], delta[n], new_m[n], new_v[n] = (t.reshape(wts[n].shape) for t in res)
    for n, rows in (("w_gla_gate", all_sum[8:16]), ("conv_w", all_sum[16:28])):
        cols = shard[n].shape[1]
        grads[n] = lax.dynamic_slice_in_dim(rows.reshape(shard[n].shape[0], N_SHARD * cols), my_slot * cols, cols, axis=1)
        d, m_, v_ = _adamw(shard[n], grads[n], two_d(moms[n]), two_d(vars_[n]), ADAM_ROWS[n], "adamw_" + n)
        delta[n], new_m[n], new_v[n] = (t.reshape(wts[n].shape) for t in (d, m_, v_))
    pk = lambda src: _pack_small({n: two_d(src[n]) for n in SMALL_NAMES})
    sd, sm_, sv_ = _adamw(pk(wts), small_sum, pk(moms), pk(vars_), 8, "adamw_small")
    for res, dst in ((sd, delta), (sm_, new_m), (sv_, new_v)):
        u, _ = _unpack_small(res, wts)
        dst.update(u)
    grad_w = {n: grads[n].reshape(wts[n].shape) for n in SHARDED}
    grad_w.update({n: small_g[n].reshape(wts[n].shape) for n in SMALL_NAMES})
    return (loss_total, grad_x[None], *[grad_w[n] for n in WEIGHT_ORDER], *[delta[n] for n in WEIGHT_ORDER],
            *[new_m[n] for n in WEIGHT_ORDER], *[new_v[n] for n in WEIGHT_ORDER])
```

```python
import functools
import math

import numpy as np
import jax
import jax.numpy as jnp
from jax import lax
from jax.experimental import pallas as pl
from jax.experimental.pallas import tpu as pltpu

F32 = jnp.float32
BF16 = jnp.bfloat16
HI = lax.Precision.HIGH
MESH = pl.DeviceIdType.MESH
ANY = pl.BlockSpec(memory_space=pl.ANY)

EPS = 1e-6
D = 1024
DFF = 2816
FFN_RES = 0.5
GLA_H, GLA_DK, GLA_DV, GLA_RANK, GLA_TAU = 4, 128, 256, 16, 16.0
DN_H, DN_DK, DN_DV = 8, 128, 128
CONV_K = 4
CHUNK = 64
N_SHARD = 4
FF_CUT = DFF // N_SHARD
ADAM_LR, ADAM_B1, ADAM_B2, ADAM_EPS, ADAM_WD, ADAM_STEP = 0.001, 0.9, 0.999, 1e-08, 0.01, 10

IN_SIZES = (512, 512, 1024, 1024, 16, 1024, 1024, 1024, 1024, 8, 8, 1024, 1024)
IN_OFF = tuple(int(v) for v in np.cumsum((0,) + IN_SIZES))
D_IN = IN_OFF[-1]
BIG_COLS = 9216
SMALL = 128
PIECES = (512, 512, 1024, 1024, 3072, 1024, 1024, 1024)

VMEM_LIMIT = 56 * 1024 * 1024
ROW_BLK = 256
BIG_ROW_BLK = 512
ATT_BLK = 256
GDN_BLK = 128
GDN_HEADS = 8


def _cp(*sem):
    return pltpu.CompilerParams(dimension_semantics=sem, vmem_limit_bytes=VMEM_LIMIT)


def _sigmoid(x):
    return 1.0 / (1.0 + jnp.exp(-x))


def _softplus(x):
    return jnp.maximum(x, 0.0) + jnp.log(1.0 + jnp.exp(-jnp.abs(x)))


def _log_sigmoid(x):
    return jnp.minimum(x, 0.0) - jnp.log(1.0 + jnp.exp(-jnp.abs(x)))


def _dot(a, b, prec=None):
    return jnp.dot(a, b, preferred_element_type=F32, precision=prec)


def _dot_nt(a, b, prec=None):
    return lax.dot_general(a, b, (((1,), (1,)), ((), ())), preferred_element_type=F32, precision=prec)


def _dot_tn(a, b, prec=None):
    return lax.dot_general(a, b, (((0,), (0,)), ((), ())), preferred_element_type=F32, precision=prec)


def _b(x):
    return x.astype(BF16)


def _iota2(n, m, axis):
    return lax.broadcasted_iota(jnp.int32, (n, m), axis)


def _load_weights(pairs, sem):
    copies = [pltpu.make_async_copy(s, d, sem.at[i]) for i, (s, d) in enumerate(pairs)]
    for c in copies:
        c.start()
    for c in copies:
        c.wait()


def _ffn_fwd(h, nw, wg, wu, wd, name, carry=None):
    T = h.shape[0]
    tm = min(BIG_ROW_BLK, T)
    ex_in, ex_args, ex_out, ex_shape, ex_sems = _carry_specs(carry)

    def body(h_ref, nw_ref, wg_hbm, wu_hbm, wd_hbm, ho_ref, n_ref, g_ref, u_ref, wg_v, wu_v, wd_v, sem):
        @pl.when(pl.program_id(0) == 0)
        def _():
            _load_weights(((wg_hbm, wg_v), (wu_hbm, wu_v), (wd_hbm, wd_v)), sem)

        x = h_ref[...]
        r = lax.rsqrt(jnp.mean(x * x, axis=-1, keepdims=True) + EPS)
        nb = _b((x * r) * nw_ref[...])
        n_ref[...] = nb
        acc = jnp.zeros((tm, D), F32)
        for s in range(N_SHARD):
            g = _dot(nb, wg_v[s])
            u = _dot(nb, wu_v[s])
            g_ref[s] = _b(g)
            u_ref[s] = _b(u)
            acc += _dot(_b(g * _sigmoid(g) * u), wd_v[s])
        ho_ref[...] = x + FFN_RES * acc

    row = lambda w: pl.BlockSpec((tm, w), lambda i: (i, 0))
    cut = pl.BlockSpec((N_SHARD, tm, FF_CUT), lambda i: (0, i, 0))
    return pl.pallas_call(
        _carry(carry, body, 5, 4, (T // tm,)), name=name, grid=(T // tm,),
        in_specs=[row(D), pl.BlockSpec((1, D), lambda i: (0, 0)), ANY, ANY, ANY] + ex_in,
        out_specs=[row(D), row(D), cut, cut] + ex_out,
        out_shape=[jax.ShapeDtypeStruct((T, D), F32), jax.ShapeDtypeStruct((T, D), BF16),
                   jax.ShapeDtypeStruct((N_SHARD, T, FF_CUT), BF16),
                   jax.ShapeDtypeStruct((N_SHARD, T, FF_CUT), BF16)] + ex_shape,
        scratch_shapes=[pltpu.VMEM((N_SHARD, D, FF_CUT), BF16), pltpu.VMEM((N_SHARD, D, FF_CUT), BF16),
                        pltpu.VMEM((N_SHARD, FF_CUT, D), BF16), pltpu.SemaphoreType.DMA((3,))] + ex_sems,
        compiler_params=_cp("arbitrary"),
    )(h, nw, wg, wu, wd, *ex_args)


def _ffn_bwd(dh, h, nw, g, u, wg, wu, wd, name, carry=None):
    T = h.shape[0]
    tm = min(ROW_BLK, T)
    ex_in, ex_args, ex_out, ex_shape, ex_sems = _carry_specs(carry)

    def body(dh_ref, h_ref, nw_ref, g_ref, u_ref, wg_hbm, wu_hbm, wd_hbm,
             dx_ref, dg_ref, du_ref, a_ref, df_ref, dnw_ref, wg_v, wu_v, wd_v, sem):
        @pl.when(pl.program_id(0) == 0)
        def _():
            _load_weights(((wg_hbm, wg_v), (wu_hbm, wu_v), (wd_hbm, wd_v)), sem)
            dnw_ref[...] = jnp.zeros_like(dnw_ref)

        dh_ = dh_ref[...]
        dfb = _b(FFN_RES * dh_)
        df_ref[...] = dfb
        dn = jnp.zeros((tm, D), F32)
        for s in range(N_SHARD):
            da = _dot_nt(dfb, wd_v[s])
            gg = g_ref[s].astype(F32)
            uu = u_ref[s].astype(F32)
            sg = _sigmoid(gg)
            silu = gg * sg
            a_ref[s] = _b(silu * uu)
            dgb = _b(da * uu * (sg * (1.0 + gg * (1.0 - sg))))
            dub = _b(da * silu)
            dg_ref[s] = dgb
            du_ref[s] = dub
            dn += _dot_nt(dgb, wg_v[s]) + _dot_nt(dub, wu_v[s])
        x = h_ref[...]
        r = lax.rsqrt(jnp.mean(x * x, axis=-1, keepdims=True) + EPS)
        xhat = x * r
        dnw_ref[...] += jnp.sum(dn * xhat, axis=0, keepdims=True)
        dxhat = dn * nw_ref[...]
        dx_ref[...] = dh_ + r * (dxhat - xhat * jnp.mean(dxhat * xhat, axis=-1, keepdims=True))

    row = lambda w: pl.BlockSpec((tm, w), lambda i: (i, 0))
    one = pl.BlockSpec((1, D), lambda i: (0, 0))
    cut = pl.BlockSpec((N_SHARD, tm, FF_CUT), lambda i: (0, i, 0))
    cut_shape = jax.ShapeDtypeStruct((N_SHARD, T, FF_CUT), BF16)
    return pl.pallas_call(
        _carry(carry, body, 8, 6, (T // tm,)), name=name, grid=(T // tm,),
        in_specs=[row(D), row(D), one, cut, cut, ANY, ANY, ANY] + ex_in,
        out_specs=[row(D), cut, cut, cut, row(D), one] + ex_out,
        out_shape=[jax.ShapeDtypeStruct((T, D), F32), cut_shape, cut_shape, cut_shape,
                   jax.ShapeDtypeStruct((T, D), BF16), jax.ShapeDtypeStruct((1, D), F32)] + ex_shape,
        scratch_shapes=[pltpu.VMEM((N_SHARD, D, FF_CUT), BF16), pltpu.VMEM((N_SHARD, D, FF_CUT), BF16),
                        pltpu.VMEM((N_SHARD, FF_CUT, D), BF16), pltpu.SemaphoreType.DMA((3,))] + ex_sems,
        compiler_params=_cp("arbitrary"),
    )(dh, h, nw, g, u, wg, wu, wd, *ex_args)


def _mm_tn(a, b, bm, bn, name, out_dtype=BF16, tk=2048):
    cuts = a.shape[0] if a.ndim == 3 else (b.shape[0] if b.ndim == 3 else None)
    T, M = a.shape[-2:]
    N = b.shape[-1]
    tk = min(tk, T)
    bm, bn = min(bm, M), min(bn, N)
    nk = T // tk

    def body(a_ref, b_ref, o_ref, acc_ref):
        k = pl.program_id(3)

        @pl.when(k == 0)
        def _():
            acc_ref[...] = jnp.zeros_like(acc_ref)

        av = a_ref[0] if a.ndim == 3 else a_ref[...]
        bv = b_ref[0] if b.ndim == 3 else b_ref[...]
        acc_ref[...] += _dot_tn(_b(av), _b(bv))

        @pl.when(k == nk - 1)
        def _():
            res = acc_ref[...].astype(out_dtype)
            if cuts is None:
                o_ref[...] = res
            else:
                o_ref[0] = res

    a_spec = (pl.BlockSpec((1, tk, bm), lambda s, i, j, k: (s, k, i)) if a.ndim == 3
              else pl.BlockSpec((tk, bm), lambda s, i, j, k: (k, i)))
    b_spec = (pl.BlockSpec((1, tk, bn), lambda s, i, j, k: (s, k, j)) if b.ndim == 3
              else pl.BlockSpec((tk, bn), lambda s, i, j, k: (k, j)))
    if cuts is None:
        o_spec, o_shape = pl.BlockSpec((bm, bn), lambda s, i, j, k: (i, j)), (M, N)
    else:
        o_spec, o_shape = pl.BlockSpec((1, bm, bn), lambda s, i, j, k: (s, i, j)), (cuts, M, N)
    return pl.pallas_call(
        body, name=name, grid=(cuts or 1, M // bm, N // bn, nk),
        in_specs=[a_spec, b_spec], out_specs=o_spec,
        out_shape=jax.ShapeDtypeStruct(o_shape, out_dtype),
        scratch_shapes=[pltpu.VMEM((bm, bn), F32)],
        compiler_params=_cp("parallel", "parallel", "parallel", "arbitrary"),
    )(a, b)


def _norm_proj(h, nw, wbig, wsmall, name):
    T = h.shape[0]
    tm = min(512, T)
    tn = 1536

    def body(h_ref, nw_ref, wb_hbm, ws_ref, pb_ref, ps_ref, n_ref, wb_v, sem):
        @pl.when(pl.program_id(0) == 0)
        def _():
            _load_weights(((wb_hbm, wb_v),), sem)

        x = h_ref[...]
        r = lax.rsqrt(jnp.mean(x * x, axis=-1, keepdims=True) + EPS)
        nb = _b((x * r) * nw_ref[...])
        n_ref[...] = nb
        ps_ref[...] = _dot(nb, ws_ref[...])
        for j in range(BIG_COLS // tn):
            pb_ref[:, j * tn:(j + 1) * tn] = _b(_dot(nb, wb_v[:, j * tn:(j + 1) * tn]))

    row = lambda w: pl.BlockSpec((tm, w), lambda i: (i, 0))
    return pl.pallas_call(
        body, name=name, grid=(T // tm,),
        in_specs=[row(D), pl.BlockSpec((1, D), lambda i: (0, 0)), ANY, pl.BlockSpec((D, SMALL), lambda i: (0, 0))],
        out_specs=[row(BIG_COLS), row(SMALL), row(D)],
        out_shape=[jax.ShapeDtypeStruct((T, BIG_COLS), BF16), jax.ShapeDtypeStruct((T, SMALL), F32),
                   jax.ShapeDtypeStruct((T, D), BF16)],
        scratch_shapes=[pltpu.VMEM((D, BIG_COLS), BF16), pltpu.SemaphoreType.DMA((1,))],
        compiler_params=_cp("arbitrary"),
    )(h, nw, wbig, wsmall)


def _proj_bwd(dh, h, nw, pieces, dsmall, wbig, wsmall, name):
    T = h.shape[0]
    tm = min(BIG_ROW_BLK, T)
    offs = tuple(int(v) for v in np.cumsum((0,) + PIECES))

    def body(dh_ref, h_ref, nw_ref, *rest):
        p_refs = rest[:len(PIECES)]
        ds_ref, wb_hbm, ws_ref, dx_ref, dnw_ref, wb_v, sem = rest[len(PIECES):]

        @pl.when(pl.program_id(0) == 0)
        def _():
            _load_weights(((wb_hbm, wb_v),), sem)
            dnw_ref[...] = jnp.zeros_like(dnw_ref)

        dn = _dot_nt(_b(ds_ref[...]), ws_ref[...])
        for p_ref, lo, wdt in zip(p_refs, offs, PIECES):
            dn += _dot_nt(p_ref[...], wb_v[:, lo:lo + wdt])
        x = h_ref[...]
        r = lax.rsqrt(jnp.mean(x * x, axis=-1, keepdims=True) + EPS)
        xhat = x * r
        dnw_ref[...] += jnp.sum(dn * xhat, axis=0, keepdims=True)
        dxhat = dn * nw_ref[...]
        dx_ref[...] = dh_ref[...] + r * (dxhat - xhat * jnp.mean(dxhat * xhat, axis=-1, keepdims=True))

    row = lambda w: pl.BlockSpec((tm, w), lambda i: (i, 0))
    one = pl.BlockSpec((1, D), lambda i: (0, 0))
    return pl.pallas_call(
        body, name=name, grid=(T // tm,),
        in_specs=[row(D), row(D), one] + [row(w) for w in PIECES] + [row(SMALL), ANY, pl.BlockSpec((D, SMALL), lambda i: (0, 0))],
        out_specs=[row(D), one],
        out_shape=[jax.ShapeDtypeStruct((T, D), F32), jax.ShapeDtypeStruct((1, D), F32)],
        scratch_shapes=[pltpu.VMEM((D, BIG_COLS), BF16), pltpu.SemaphoreType.DMA((1,))],
        compiler_params=_cp("arbitrary"),
    )(dh, h, nw, *pieces, dsmall, wbig, wsmall)


def _gla_block(q_ref, k_ref, sm_ref, wg_ref, bg_ref, nc, tril):
    nbat = GLA_H * nc
    q = _heads_first(q_ref[...].astype(F32), nc, GLA_DK)
    k = _heads_first(k_ref[...].astype(F32), nc, GLA_DK)
    pre = _heads_first(_dot(sm_ref[...], wg_ref[...], HI) + bg_ref[...], nc, GLA_DK)
    la = _log_sigmoid(pre) * (1.0 / GLA_TAU)
    bc = _bmm(jnp.broadcast_to(tril, (nbat, CHUNK, CHUNK)), la, HI)
    bl = bc[:, CHUNK - 1:CHUNK, :]
    eb = jnp.exp(bc)
    enb = jnp.exp(-bc)
    ebl = jnp.exp(bl - bc)
    q_in = q * (GLA_DK ** -0.5) * eb
    k_out = k * enb
    k_st = k * ebl
    a_ch = jnp.exp(bl)
    return pre, eb, enb, ebl, q_in, k_out, k_st, a_ch


def _gla_specs(blk, idx):
    hk, hv = GLA_H * GLA_DK, GLA_H * GLA_DV
    return [pl.BlockSpec((blk, hk), lambda j: (idx(j), 0)),
            pl.BlockSpec((blk, hk), lambda j: (idx(j), 1)),
            pl.BlockSpec((blk, hv), lambda j: (idx(j), 1)),
            pl.BlockSpec((blk, SMALL), lambda j: (idx(j), 0)),
            pl.BlockSpec((SMALL, hk), lambda j: (0, 0)),
            pl.BlockSpec((1, hk), lambda j: (0, 0))]


def _gla_fwd(pbig, psmall, wgate, bgate, name):
    T = pbig.shape[0]
    blk = min(ATT_BLK, T)
    nc = blk // CHUNK

    def body(q_ref, k_ref, v_ref, sm_ref, wg_ref, bg_ref, o_ref, ss_ref, st_ref):
        @pl.when(pl.program_id(0) == 0)
        def _():
            st_ref[...] = jnp.zeros_like(st_ref)

        causal = _iota2(CHUNK, CHUNK, 0) >= _iota2(CHUNK, CHUNK, 1)
        _, _, _, _, q_in, k_out, k_st, a_ch = _gla_block(q_ref, k_ref, sm_ref, wg_ref, bg_ref, nc, causal.astype(F32))
        v = _heads_first(v_ref[...], nc, GLA_DV)
        qb = _b(q_in)
        sc = jnp.where(causal, _bmm_nt(qb, _b(k_out)), 0.0)
        kv = _bmm_tn(v, _b(k_st))
        before = [None] * (GLA_H * nc)
        for i in range(GLA_H):
            st = st_ref[i]
            for c in range(nc):
                n = i * nc + c
                before[n] = st
                st = st * a_ch[n] + kv[n]
            st_ref[i] = st
        states = jnp.stack(before)
        ss_ref[...] = states.reshape(GLA_H, nc, GLA_DV, GLA_DK)
        o_ref[...] = _heads_last(_bmm(_b(sc), v) + _bmm_nt(qb, _b(states)), nc)

    return pl.pallas_call(
        body, name=name, grid=(T // blk,),
        in_specs=_gla_specs(blk, lambda j: j),
        out_specs=[pl.BlockSpec((blk, GLA_H * GLA_DV), lambda j: (j, 0)),
                   pl.BlockSpec((GLA_H, nc, GLA_DV, GLA_DK), lambda j: (0, j, 0, 0))],
        out_shape=[jax.ShapeDtypeStruct((T, GLA_H * GLA_DV), F32),
                   jax.ShapeDtypeStruct((GLA_H, T // CHUNK, GLA_DV, GLA_DK), F32)],
        scratch_shapes=[pltpu.VMEM((GLA_H, GLA_DV, GLA_DK), F32)],
        compiler_params=_cp("arbitrary"),
    )(pbig, pbig, pbig, psmall, wgate, bgate)


def _gla_bwd(pbig, psmall, wgate, bgate, states, do, name):
    T = pbig.shape[0]
    blk = min(ATT_BLK, T)
    nc = blk // CHUNK
    nb = T // blk
    nbat = GLA_H * nc

    def body(q_ref, k_ref, v_ref, sm_ref, wg_ref, bg_ref, ss_ref, do_ref, dq_ref, dk_ref, dv_ref, dpre_ref, dst_ref):
        @pl.when(pl.program_id(0) == 0)
        def _():
            dst_ref[...] = jnp.zeros_like(dst_ref)

        causal = _iota2(CHUNK, CHUNK, 0) >= _iota2(CHUNK, CHUNK, 1)
        triu = (_iota2(CHUNK, CHUNK, 0) <= _iota2(CHUNK, CHUNK, 1)).astype(F32)
        pre, eb, enb, ebl, q_in, k_out, k_st, a_ch = _gla_block(q_ref, k_ref, sm_ref, wg_ref, bg_ref, nc,
                                                                causal.astype(F32))
        v = _heads_first(v_ref[...], nc, GLA_DV)
        dob = _b(_heads_first(do_ref[...], nc, GLA_DV))
        st = ss_ref[...].reshape(nbat, GLA_DV, GLA_DK)
        qb, kob, kstb = _b(q_in), _b(k_out), _b(k_st)
        qdo = _bmm_tn(dob, qb)
        after = [None] * nbat
        for i in range(GLA_H):
            dst = dst_ref[i]
            for c in range(nc - 1, -1, -1):
                n = i * nc + c
                after[n] = dst
                dst = dst * a_ch[n] + qdo[n]
            dst_ref[i] = dst
        dsa = jnp.stack(after)
        dsb = _b(dsa)
        sc = jnp.where(causal, _bmm_nt(qb, kob), 0.0)
        dsc = _b(jnp.where(causal, _bmm_nt(dob, v), 0.0))
        dq_in = _bmm(dob, _b(st)) + _bmm(dsc, kob)
        dk_out = _bmm_tn(dsc, qb)
        dk_st = _bmm(v, dsb)
        dv_ref[...] = _b(_heads_last(_bmm_tn(_b(sc), dob) + _bmm_nt(kstb, dsb), nc))
        da_ch = jnp.sum(st * dsa, axis=1, keepdims=True)
        tk = dk_st * k_st
        db = dq_in * q_in - dk_out * k_out - tk
        db_last = jnp.sum(tk, axis=1, keepdims=True) + da_ch * a_ch
        dq_ref[...] = _b(_heads_last(dq_in * (GLA_DK ** -0.5) * eb, nc))
        dk_ref[...] = _b(_heads_last(dk_out * enb + dk_st * ebl, nc))
        dla = _bmm(jnp.broadcast_to(triu, (nbat, CHUNK, CHUNK)), db, HI) + db_last
        dpre_ref[...] = _heads_last(dla * (1.0 / GLA_TAU) * _sigmoid(-pre), nc)

    r = lambda j: nb - 1 - j
    hk, hv = GLA_H * GLA_DK, GLA_H * GLA_DV
    return pl.pallas_call(
        body, name=name, grid=(nb,),
        in_specs=_gla_specs(blk, r) + [pl.BlockSpec((GLA_H, nc, GLA_DV, GLA_DK), lambda j: (0, r(j), 0, 0)),
                                      pl.BlockSpec((blk, hv), lambda j: (r(j), 0))],
        out_specs=[pl.BlockSpec((blk, hk), lambda j: (r(j), 0)), pl.BlockSpec((blk, hk), lambda j: (r(j), 0)),
                   pl.BlockSpec((blk, hv), lambda j: (r(j), 0)), pl.BlockSpec((blk, hk), lambda j: (r(j), 0))],
        out_shape=[jax.ShapeDtypeStruct((T, hk), BF16), jax.ShapeDtypeStruct((T, hk), BF16),
                   jax.ShapeDtypeStruct((T, hv), BF16), jax.ShapeDtypeStruct((T, hk), F32)],
        scratch_shapes=[pltpu.VMEM((GLA_H, GLA_DV, GLA_DK), F32)],
        compiler_params=_cp("arbitrary"),
    )(pbig, pbig, pbig, psmall, wgate, bgate, states, do)


def _gla_gate_bwd(dpre, psmall, wgate, dsm, name):
    T = dpre.shape[0]
    tm = min(512, T)
    W = GLA_H * GLA_DK
    ngrp = dsm.shape[0]

    def body(dp_ref, sm_ref, wg_ref, dsm_ref, ds_ref, dw_ref, db_ref):
        @pl.when(pl.program_id(0) == 0)
        def _():
            dw_ref[...] = jnp.zeros_like(dw_ref)
            db_ref[...] = jnp.zeros_like(db_ref)

        dp = dp_ref[...]
        ds = _dot_nt(dp, wg_ref[...], HI)
        for i in range(ngrp):
            ds += dsm_ref[i]
        ds_ref[...] = ds
        dw_ref[...] += _dot_tn(sm_ref[...], dp, HI)
        db_ref[...] += jnp.sum(dp, axis=0, keepdims=True)

    return pl.pallas_call(
        body, name=name, grid=(T // tm,),
        in_specs=[pl.BlockSpec((tm, W), lambda i: (i, 0)), pl.BlockSpec((tm, SMALL), lambda i: (i, 0)),
                  pl.BlockSpec((SMALL, W), lambda i: (0, 0)), pl.BlockSpec((ngrp, tm, SMALL), lambda i: (0, i, 0))],
        out_specs=[pl.BlockSpec((tm, SMALL), lambda i: (i, 0)), pl.BlockSpec((SMALL, W), lambda i: (0, 0)),
                   pl.BlockSpec((1, W), lambda i: (0, 0))],
        out_shape=[jax.ShapeDtypeStruct((T, SMALL), F32), jax.ShapeDtypeStruct((SMALL, W), F32),
                   jax.ShapeDtypeStruct((1, W), F32)],
        compiler_params=_cp("arbitrary"),
    )(dpre, psmall, wgate, dsm)


CONV_C = 3 * 1024
CONV_BLK = 256


def _conv_fwd(pbig, cw8, name):
    T = pbig.shape[0]
    blk = min(CONV_BLK, T)

    def body(x_ref, w_ref, c_ref, prev_ref):
        @pl.when(pl.program_id(0) == 0)
        def _():
            prev_ref[...] = jnp.zeros_like(prev_ref)

        x = x_ref[...].astype(F32)
        prev = prev_ref[...]
        row8 = _iota2(8, CONV_C, 0)
        acc = x * w_ref[CONV_K - 1:CONV_K, :]
        for s in range(1, CONV_K):
            xs = pltpu.roll(x, s, 0)
            top = jnp.where(row8 < s, pltpu.roll(prev, s, 0), xs[:8])
            xs = jnp.concatenate([top, xs[8:]], axis=0)
            acc += xs * w_ref[CONV_K - 1 - s:CONV_K - s, :]
        c_ref[...] = _b(acc)
        prev_ref[...] = x[blk - 8:]

    return pl.pallas_call(
        body, name=name, grid=(T // blk,),
        in_specs=[pl.BlockSpec((blk, CONV_C), lambda i: (i, 1)), pl.BlockSpec((8, CONV_C), lambda i: (0, 0))],
        out_specs=pl.BlockSpec((blk, CONV_C), lambda i: (i, 0)),
        out_shape=jax.ShapeDtypeStruct((T, CONV_C), BF16),
        scratch_shapes=[pltpu.VMEM((8, CONV_C), F32)],
        compiler_params=_cp("arbitrary"),
    )(pbig, cw8)


def _conv_bwd(dcq, dck, dcv, pbig, cw8, name):
    T = pbig.shape[0]
    blk = min(CONV_BLK, T)
    nb = T // blk

    def body(dq_ref, dk_ref, dv_ref, x_ref, w_ref, dx_ref, dw_ref, nxt_ref):
        @pl.when(pl.program_id(0) == 0)
        def _():
            nxt_ref[...] = jnp.zeros_like(nxt_ref)
            dw_ref[...] = jnp.zeros_like(dw_ref)

        dc = jnp.concatenate([dq_ref[...], dk_ref[...], dv_ref[...]], axis=1).astype(F32)
        x = x_ref[...].astype(F32)
        nxt = nxt_ref[...]
        row8 = _iota2(8, CONV_C, 0)
        acc = dc * w_ref[CONV_K - 1:CONV_K, :]
        dws = [jnp.sum(dc * x, axis=0, keepdims=True)]
        for s in range(1, CONV_K):
            ds = pltpu.roll(dc, blk - s, 0)
            bot = jnp.where(row8 >= 8 - s, pltpu.roll(nxt, 8 - s, 0), ds[blk - 8:])
            ds = jnp.concatenate([ds[:blk - 8], bot], axis=0)
            acc += ds * w_ref[CONV_K - 1 - s:CONV_K - s, :]
            dws.append(jnp.sum(ds * x, axis=0, keepdims=True))
        dx_ref[...] = _b(acc)
        dw_ref[...] += jnp.concatenate(dws[::-1] + [jnp.zeros((8 - CONV_K, CONV_C), F32)], axis=0)
        nxt_ref[...] = dc[:8]

    part = pl.BlockSpec((blk, 1024), lambda i: (nb - 1 - i, 0))
    return pl.pallas_call(
        body, name=name, grid=(nb,),
        in_specs=[part, part, part, pl.BlockSpec((blk, CONV_C), lambda i: (nb - 1 - i, 1)),
                  pl.BlockSpec((8, CONV_C), lambda i: (0, 0))],
        out_specs=[pl.BlockSpec((blk, CONV_C), lambda i: (nb - 1 - i, 0)), pl.BlockSpec((8, CONV_C), lambda i: (0, 0))],
        out_shape=[jax.ShapeDtypeStruct((T, CONV_C), BF16), jax.ShapeDtypeStruct((8, CONV_C), F32)],
        scratch_shapes=[pltpu.VMEM((8, CONV_C), F32)],
        compiler_params=_cp("arbitrary"),
    )(dcq, dck, dcv, pbig, cw8)


def _col(x, lane):
    sel = _iota2(x.shape[0], x.shape[1], 1) == lane
    return jnp.broadcast_to(jnp.sum(jnp.where(sel, x, 0.0), axis=1, keepdims=True), x.shape)


def _bmm(a, b, prec=None):
    return jnp.einsum("bij,bjk->bik", a, b, preferred_element_type=F32, precision=prec)


def _bmm_nt(a, b, prec=None):
    return jnp.einsum("bij,bkj->bik", a, b, preferred_element_type=F32, precision=prec)


def _bmm_tn(a, b, prec=None):
    return jnp.einsum("bji,bjk->bik", a, b, preferred_element_type=F32, precision=prec)


def _unit_lower_inverse(low):
    eye = (_iota2(CHUNK, CHUNK, 0) == _iota2(CHUNK, CHUNK, 1)).astype(F32)
    xk = -low
    inv = eye + xk
    for _ in range(5):
        xb = _b(xk)
        xk = _bmm(xb, xb)
        inv = inv + _bmm(_b(inv), _b(xk))
    resid = eye - _bmm(eye + low, inv, HI)
    return inv + _bmm(inv, resid, HI)


def _heads_first(x, nc, w=128):
    hb = x.shape[1] // w
    return jnp.concatenate([x[:, i * w:(i + 1) * w].reshape(nc, CHUNK, w) for i in range(hb)], axis=0)


def _heads_last(x, nc):
    hb = x.shape[0] // nc
    return jnp.concatenate([x[i * nc:(i + 1) * nc].reshape(nc * CHUNK, x.shape[2]) for i in range(hb)], axis=1)


def _gdn_block(cq_ref, ck_ref, cv_ref, sm_ref, par_ref, h0, hb, nc, masks, solved=None):
    causal, strict, tril, eye = masks
    nbat = hb * nc
    cq = _heads_first(cq_ref[...].astype(F32), nc)
    ck = _heads_first(ck_ref[...].astype(F32), nc)
    cv = _heads_first(cv_ref[...].astype(F32), nc)
    sq, sk, sv = _sigmoid(cq), _sigmoid(ck), _sigmoid(cv)
    q, k, v = cq * sq, ck * sk, cv * sv
    rq = lax.rsqrt(jnp.sum(q * q, axis=-1, keepdims=True) + EPS)
    rk = lax.rsqrt(jnp.sum(k * k, axis=-1, keepdims=True) + EPS)
    qh, kn = q * rq, k * rk
    qn = qh * (DN_DK ** -0.5)
    sm = sm_ref[...]
    per_head = lambda fn: jnp.concatenate([fn(i) for i in range(hb)], axis=0)
    braw = per_head(lambda i: _col(sm, GLA_RANK + h0 + i).reshape(nc, CHUNK, 128))
    araw = per_head(lambda i: _col(sm, GLA_RANK + DN_H + h0 + i).reshape(nc, CHUNK, 128))
    ea = per_head(lambda i: jnp.broadcast_to(jnp.exp(par_ref[i, 0:1, :])[None], (nc, 1, 128)))
    bias = per_head(lambda i: jnp.broadcast_to(par_ref[i, 1:2, :][None], (nc, 1, 128)))
    beta = _sigmoid(braw)
    sp_arg = araw + bias
    g = -ea * _softplus(sp_arg)
    G = _bmm(jnp.broadcast_to(tril, (nbat, CHUNK, CHUNK)), g, HI)
    gc = G[:, :, :CHUNK]
    grow = jnp.sum(eye * gc, axis=1, keepdims=True)
    decay = jnp.exp(jnp.where(causal, gc - grow, -1e30))
    kb = kn * beta
    A = _bmm_nt(_b(kb), _b(kn))
    eG = jnp.exp(G)
    gl = G[:, CHUNK - 1:CHUNK, :]
    eGl = jnp.exp(gl - G)
    g_ch = jnp.exp(gl)
    rv = v * beta
    rkk = kb * eG
    if solved is None:
        tinv_b = _b(_unit_lower_inverse(jnp.where(strict, A * decay, 0.0)))
        u = _bmm(tinv_b, _b(rv))
        w = _b(_bmm(tinv_b, _b(rkk)))
    else:
        tinv_b, u, w = solved
    B = _bmm_nt(_b(qn), _b(kn))
    qk = jnp.where(causal, B * decay, 0.0)
    q_dec = qn * eG
    k_st = kn * eGl
    return dict(cq=cq, ck=ck, cv=cv, sq=sq, sk=sk, sv=sv, q=q, k=k, v=v, rq=rq, rk=rk, qh=qh, kn=kn, qn=qn,
                beta=beta, ea=ea, sp_arg=sp_arg, g=g, G=G, decay=decay, kb=kb, A=A, tinv_b=tinv_b, eG=eG, eGl=eGl,
                g_ch=g_ch, rv=rv, rkk=rkk, u=u, w=w, B=B, qk=qk, q_dec=q_dec, k_st=k_st)


def _gdn_masks():
    r, c = _iota2(CHUNK, CHUNK, 0), _iota2(CHUNK, CHUNK, 1)
    return r >= c, r > c, (r >= c).astype(F32), (r == c).astype(F32)


def _gdn_specs(blk, hb, idx):
    ng = DN_H // hb
    return [pl.BlockSpec((blk, hb * DN_DK), lambda h, j: (idx(j), h)),
            pl.BlockSpec((blk, hb * DN_DK), lambda h, j: (idx(j), ng + h)),
            pl.BlockSpec((blk, hb * DN_DV), lambda h, j: (idx(j), 2 * ng + h)),
            pl.BlockSpec((blk, SMALL), lambda h, j: (idx(j), 0)),
            pl.BlockSpec((hb, 8, 128), lambda h, j: (h, 0, 0))]


def _gdn_solved_specs(blk, hb, idx):
    nc = blk // CHUNK
    spec = lambda w: pl.BlockSpec((hb, nc, CHUNK, w), lambda h, j: (h, idx(j), 0, 0))
    return [spec(CHUNK), spec(DN_DV), spec(DN_DK)]


def _gdn_fwd(conv, psmall, par, name):
    T = conv.shape[0]
    blk = min(GDN_BLK, T)
    nc = blk // CHUNK
    hb = GDN_HEADS
    N = T // CHUNK

    def body(cq_ref, ck_ref, cv_ref, sm_ref, par_ref, o_ref, ss_ref, ti_ref, u_ref, w_ref, s_ref):
        @pl.when(pl.program_id(1) == 0)
        def _():
            s_ref[...] = jnp.zeros_like(s_ref)

        f = _gdn_block(cq_ref, ck_ref, cv_ref, sm_ref, par_ref, pl.program_id(0) * hb, hb, nc, _gdn_masks())
        ti_ref[...] = f["tinv_b"].reshape(hb, nc, CHUNK, CHUNK)
        u_ref[...] = f["u"].reshape(hb, nc, CHUNK, DN_DV)
        w_ref[...] = f["w"].reshape(hb, nc, CHUNK, DN_DK)
        wb, qdb, kstb, qkb = f["w"], _b(f["q_dec"]), _b(f["k_st"]), _b(f["qk"])
        S = [s_ref[i] for i in range(hb)]
        for c in range(nc):
            for i in range(hb):
                n = i * nc + c
                ss_ref[i, c] = S[i]
                Sb = _b(S[i])
                v_new = _b(f["u"][n] - _dot(wb[n], Sb))
                o_ref[pl.ds(c * CHUNK, CHUNK), i * DN_DV:(i + 1) * DN_DV] = _dot(qdb[n], Sb) + _dot(qkb[n], v_new)
                S[i] = S[i] * f["g_ch"][n] + _dot_tn(kstb[n], v_new)
        for i in range(hb):
            s_ref[i] = S[i]

    return pl.pallas_call(
        body, name=name, grid=(DN_H // hb, T // blk),
        in_specs=_gdn_specs(blk, hb, lambda j: j),
        out_specs=[pl.BlockSpec((blk, hb * DN_DV), lambda h, j: (j, h)),
                   pl.BlockSpec((hb, nc, DN_DK, DN_DV), lambda h, j: (h, j, 0, 0))]
        + _gdn_solved_specs(blk, hb, lambda j: j),
        out_shape=[jax.ShapeDtypeStruct((T, DN_H * DN_DV), F32), jax.ShapeDtypeStruct((DN_H, N, DN_DK, DN_DV), F32),
                   jax.ShapeDtypeStruct((DN_H, N, CHUNK, CHUNK), BF16), jax.ShapeDtypeStruct((DN_H, N, CHUNK, DN_DV), F32),
                   jax.ShapeDtypeStruct((DN_H, N, CHUNK, DN_DK), BF16)],
        scratch_shapes=[pltpu.VMEM((hb, DN_DK, DN_DV), F32)],
        compiler_params=_cp("parallel", "arbitrary"),
    )(conv, conv, conv, psmall, par)


def _gdn_bwd(conv, psmall, par, states, do, solved, name, carry=None):
    ex_in, ex_args, ex_out, ex_shape, ex_sems = _carry_specs(carry)
    T = conv.shape[0]
    blk = min(GDN_BLK, T)
    nc = blk // CHUNK
    nb = T // blk
    hb = GDN_HEADS
    nbat = hb * nc
    rsum = lambda x: jnp.sum(x, axis=-1, keepdims=True)

    def body(cq_ref, ck_ref, cv_ref, sm_ref, par_ref, ss_ref, do_ref, ti_ref, u_ref, w_ref,
             dcq_ref, dck_ref, dcv_ref, dsm_ref, dpar_ref, ds_ref):
        @pl.when(pl.program_id(1) == 0)
        def _():
            ds_ref[...] = jnp.zeros_like(ds_ref)
            dpar_ref[...] = jnp.zeros_like(dpar_ref)

        masks = _gdn_masks()
        causal, strict, tril, eye = masks
        triu = (_iota2(CHUNK, CHUNK, 0) <= _iota2(CHUNK, CHUNK, 1)).astype(F32)
        lane = _iota2(CHUNK, 128, 1)
        last_row = _iota2(CHUNK, 128, 0) == CHUNK - 1
        h0 = pl.program_id(0) * hb
        solved = (ti_ref[...].reshape(nbat, CHUNK, CHUNK), u_ref[...].reshape(nbat, CHUNK, DN_DV),
                  w_ref[...].reshape(nbat, CHUNK, DN_DK))
        f = _gdn_block(cq_ref, ck_ref, cv_ref, sm_ref, par_ref, h0, hb, nc, masks, solved)
        S = ss_ref[...].reshape(nbat, DN_DK, DN_DV)
        Sb = _b(S)
        do_ = _b(_heads_first(do_ref[...], nc))
        wb, qdb, kstb, qkb = _b(f["w"]), _b(f["q_dec"]), _b(f["k_st"]), _b(f["qk"])
        vnb = _b(f["u"] - _bmm(wb, Sb))
        dvn0 = _bmm_tn(qkb, do_)
        qdo = _bmm_tn(qdb, do_)
        dS = [ds_ref[i] for i in range(hb)]
        after = [None] * nbat
        for c in range(nc - 1, -1, -1):
            for i in range(hb):
                n = i * nc + c
                after[n] = dS[i]
                dvn_c = _b(dvn0[n] + _dot(kstb[n], _b(dS[i])))
                dS[i] = dS[i] * f["g_ch"][n] + qdo[n] - _dot_tn(wb[n], dvn_c)
        for i in range(hb):
            ds_ref[i] = dS[i]
        dSa = jnp.stack(after)
        dSb = _b(dSa)
        dvn = dvn0 + _bmm(kstb, dSb)
        dvnb = _b(dvn)
        dq_dec = _bmm_nt(do_, Sb)
        dqk = jnp.where(causal, _bmm_nt(do_, vnb), 0.0)
        dk_st = _bmm_nt(vnb, dSb)
        dg_ch = jnp.sum(rsum(S * dSa), axis=1, keepdims=True)
        dw = -_bmm_nt(dvnb, Sb)
        drv = _bmm_tn(f["tinv_b"], dvnb)
        drk = _bmm_tn(f["tinv_b"], _b(dw))
        dlow = jnp.where(strict, -(_bmm_nt(_b(drv), _b(f["u"])) + _bmm_nt(_b(drk), wb)), 0.0)
        dv = drv * f["beta"]
        dbeta = rsum(drv * f["v"])
        dkb = drk * f["eG"]
        dG = rsum(drk * f["rkk"])
        dA = dlow * f["decay"]
        ddec = dlow * f["A"]
        dkb += _bmm(_b(dA), _b(f["kn"]))
        dkn = _bmm_tn(_b(dA), _b(f["kb"]))
        dB = dqk * f["decay"]
        ddec += dqk * f["B"]
        dqn = _bmm(_b(dB), _b(f["kn"]))
        dkn += _bmm_tn(_b(dB), _b(f["qn"]))
        dD = ddec * f["decay"]
        dG += rsum(dD) - rsum(eye * jnp.sum(dD, axis=1, keepdims=True))
        dqn += dq_dec * f["eG"]
        dG += rsum(dq_dec * f["q_dec"])
        dkn += dk_st * f["eGl"]
        tks = rsum(dk_st * f["k_st"])
        dG -= tks
        dG_last = jnp.sum(tks, axis=1, keepdims=True) + dg_ch * f["g_ch"][:, :, :1]
        dkn += dkb * f["beta"]
        dbeta += rsum(dkb * f["kn"])
        dGf = jnp.broadcast_to(dG, (nbat, CHUNK, 128)) + jnp.where(last_row, dG_last, 0.0)
        dg = _bmm(jnp.broadcast_to(triu, (nbat, CHUNK, CHUNK)), dGf, HI)
        dbraw = dbeta * f["beta"][:, :, :1] * (1.0 - f["beta"][:, :, :1])
        daraw = dg * (-f["ea"]) * _sigmoid(f["sp_arg"])
        both = lambda t: jnp.sum(jnp.sum(t, axis=1, keepdims=True), axis=0)
        dgg = dg * f["g"]
        dsm = jnp.zeros((nc, CHUNK, SMALL), F32)
        for i in range(hb):
            mine = slice(i * nc, (i + 1) * nc)
            dsm += (jnp.where(lane == GLA_RANK + h0 + i, dbraw[mine], 0.0)
                    + jnp.where(lane == GLA_RANK + DN_H + h0 + i, daraw[mine], 0.0))
            dpar = jnp.where(lane[:1] == 0, both(dgg[mine]), jnp.where(lane[:1] == 1, both(daraw[mine]), 0.0))
            dpar_ref[i] += jnp.broadcast_to(dpar, (8, 128))
        dsm_ref[0] = dsm.reshape(blk, SMALL)
        dqh = dqn * (DN_DK ** -0.5)
        dq = f["rq"] * (dqh - f["qh"] * rsum(dqh * f["qh"]))
        dk = f["rk"] * (dkn - f["kn"] * rsum(dkn * f["kn"]))
        dsilu = lambda x, s: s * (1.0 + x * (1.0 - s))
        dcq_ref[...] = _b(_heads_last(dq * dsilu(f["cq"], f["sq"]), nc))
        dck_ref[...] = _b(_heads_last(dk * dsilu(f["ck"], f["sk"]), nc))
        dcv_ref[...] = _b(_heads_last(dv * dsilu(f["cv"], f["sv"]), nc))

    r = lambda j: nb - 1 - j
    out_blk = pl.BlockSpec((blk, hb * DN_DK), lambda h, j: (r(j), h))
    grid = (DN_H // hb, nb)
    return pl.pallas_call(
        _carry(carry, body, 10, 5, grid), name=name, grid=grid,
        in_specs=_gdn_specs(blk, hb, r) + [pl.BlockSpec((hb, nc, DN_DK, DN_DV), lambda h, j: (h, r(j), 0, 0)),
                                          pl.BlockSpec((blk, hb * DN_DV), lambda h, j: (r(j), h))]
        + _gdn_solved_specs(blk, hb, r) + ex_in,
        out_specs=[out_blk, out_blk, out_blk, pl.BlockSpec((1, blk, SMALL), lambda h, j: (h, r(j), 0)),
                   pl.BlockSpec((hb, 8, 128), lambda h, j: (h, 0, 0))] + ex_out,
        out_shape=[jax.ShapeDtypeStruct((T, DN_H * DN_DK), BF16)] * 3 + [
            jax.ShapeDtypeStruct((DN_H // hb, T, SMALL), F32), jax.ShapeDtypeStruct((DN_H, 8, 128), F32)] + ex_shape,
        scratch_shapes=[pltpu.VMEM((hb, DN_DK, DN_DV), F32)] + ex_sems,
        compiler_params=_cp("arbitrary", "arbitrary"),
    )(conv, conv, conv, psmall, par, states, do, *solved, *ex_args)


def _head_norm(o, w, dv):
    outs, rs = [], []
    for i in range(o.shape[1] // dv):
        oh = o[:, i * dv:(i + 1) * dv]
        r = lax.rsqrt(jnp.mean(oh * oh, axis=-1, keepdims=True) + EPS)
        outs.append(oh * r)
        rs.append(r)
    return outs, rs


def _merge_specs(tm):
    col = lambda c: pl.BlockSpec((tm, D), lambda i: (i, c))
    return [col(0), col(0), col(2), col(6), col(7), col(8),
            pl.BlockSpec((1, GLA_DV), lambda i: (0, 0)), pl.BlockSpec((1, DN_DV), lambda i: (0, 0)),
            pl.BlockSpec((D, D), lambda i: (0, 0))]


def _merge_fwd(h, oa, ob, pbig, gla_hn, dn_hn, wout, name):
    T = h.shape[0]
    tm = min(ROW_BLK, T)

    def body(h_ref, oa_ref, ob_ref, gr_ref, dg_ref, ma_ref, mb_ref, wa_ref, wb_ref, wo_ref, ho_ref, y_ref):
        na, _ = _head_norm(oa_ref[...], wa_ref[...], GLA_DV)
        nbs, _ = _head_norm(ob_ref[...], wb_ref[...], DN_DV)
        hna = jnp.concatenate([t * wa_ref[...] for t in na], axis=1)
        hnb = jnp.concatenate([t * wb_ref[...] for t in nbs], axis=1)
        gr = gr_ref[...].astype(F32)
        dg = dg_ref[...].astype(F32)
        y = (_sigmoid(ma_ref[...].astype(F32)) * hna * (gr * _sigmoid(gr))
             + _sigmoid(mb_ref[...].astype(F32)) * hnb * (dg * _sigmoid(dg)))
        yb = _b(y)
        y_ref[...] = yb
        ho_ref[...] = h_ref[...] + _dot(yb, wo_ref[...])

    row = pl.BlockSpec((tm, D), lambda i: (i, 0))
    return pl.pallas_call(
        body, name=name, grid=(T // tm,),
        in_specs=[row] + _merge_specs(tm),
        out_specs=[row, row],
        out_shape=[jax.ShapeDtypeStruct((T, D), F32), jax.ShapeDtypeStruct((T, D), BF16)],
        compiler_params=_cp("arbitrary"),
    )(h, oa, ob, pbig, pbig, pbig, pbig, gla_hn, dn_hn, wout)


def _merge_bwd(dh, oa, ob, pbig, gla_hn, dn_hn, wout, name):
    T = dh.shape[0]
    tm = min(ROW_BLK, T)

    def branch(dy, o_ref, w_ref, gate_ref, m_ref, dv):
        w = w_ref[...]
        ohat, rs = _head_norm(o_ref[...], w, dv)
        gate = gate_ref[...].astype(F32)
        m = m_ref[...].astype(F32)
        sgate, sm = _sigmoid(gate), _sigmoid(m)
        silu = gate * sgate
        ohat_all = jnp.concatenate(ohat, axis=1)
        hn = jnp.concatenate([t * w for t in ohat], axis=1)
        d_on = dy * sm
        d_m = dy * hn * silu * sm * (1.0 - sm)
        d_hn = d_on * silu
        d_gate = d_on * hn * (sgate * (1.0 + gate * (1.0 - sgate)))
        dw = jnp.zeros((1, dv), F32)
        d_o = []
        for i, (oh, r) in enumerate(zip(ohat, rs)):
            dhn = d_hn[:, i * dv:(i + 1) * dv]
            dw += jnp.sum(dhn * oh, axis=0, keepdims=True)
            dohat = dhn * w
            d_o.append(r * (dohat - oh * jnp.mean(dohat * oh, axis=-1, keepdims=True)))
        return jnp.concatenate(d_o, axis=1), d_gate, d_m, dw

    def body(dh_ref, oa_ref, ob_ref, gr_ref, dg_ref, ma_ref, mb_ref, wa_ref, wb_ref, wo_ref,
             doa_ref, dob_ref, dgr_ref, ddg_ref, dma_ref, dmb_ref, dwa_ref, dwb_ref, dhb_ref):
        @pl.when(pl.program_id(0) == 0)
        def _():
            dwa_ref[...] = jnp.zeros_like(dwa_ref)
            dwb_ref[...] = jnp.zeros_like(dwb_ref)

        dhb = _b(dh_ref[...])
        dhb_ref[...] = dhb
        dy = _dot_nt(dhb, wo_ref[...])
        d_oa, d_gr, d_ma, dwa = branch(dy, oa_ref, wa_ref, gr_ref, ma_ref, GLA_DV)
        d_ob, d_dg, d_mb, dwb = branch(dy, ob_ref, wb_ref, dg_ref, mb_ref, DN_DV)
        doa_ref[...] = d_oa
        dob_ref[...] = d_ob
        dgr_ref[...] = _b(d_gr)
        ddg_ref[...] = _b(d_dg)
        dma_ref[...] = _b(d_ma)
        dmb_ref[...] = _b(d_mb)
        dwa_ref[...] += dwa
        dwb_ref[...] += dwb

    row = pl.BlockSpec((tm, D), lambda i: (i, 0))
    f32 = jax.ShapeDtypeStruct((T, D), F32)
    b16 = jax.ShapeDtypeStruct((T, D), BF16)
    return pl.pallas_call(
        body, name=name, grid=(T // tm,),
        in_specs=[row] + _merge_specs(tm),
        out_specs=[row] * 6 + [pl.BlockSpec((1, GLA_DV), lambda i: (0, 0)), pl.BlockSpec((1, DN_DV), lambda i: (0, 0)), row],
        out_shape=[f32, f32, b16, b16, b16, b16, jax.ShapeDtypeStruct((1, GLA_DV), F32),
                   jax.ShapeDtypeStruct((1, DN_DV), F32), b16],
        compiler_params=_cp("arbitrary"),
    )(dh, oa, ob, pbig, pbig, pbig, pbig, gla_hn, dn_hn, wout)


def _loss_head(h, nw, target, name):
    T = h.shape[0]
    tm = min(512, T)

    def body(h_ref, nw_ref, t_ref, dx_ref, loss_ref, dnw_ref):
        @pl.when(pl.program_id(0) == 0)
        def _():
            loss_ref[...] = jnp.zeros_like(loss_ref)
            dnw_ref[...] = jnp.zeros_like(dnw_ref)

        x = h_ref[...]
        w = nw_ref[...]
        r = lax.rsqrt(jnp.mean(x * x, axis=-1, keepdims=True) + EPS)
        xhat = x * r
        err = xhat * w - t_ref[...]
        part = jnp.sum(jnp.sum(err * err, axis=-1, keepdims=True), axis=0, keepdims=True)
        loss_ref[...] += (0.5 / D) * part
        dout = err * (1.0 / D)
        dnw_ref[...] += jnp.sum(dout * xhat, axis=0, keepdims=True)
        dxhat = dout * w
        dx_ref[...] = r * (dxhat - xhat * jnp.mean(dxhat * xhat, axis=-1, keepdims=True))

    row = pl.BlockSpec((tm, D), lambda i: (i, 0))
    one = pl.BlockSpec((1, D), lambda i: (0, 0))
    return pl.pallas_call(
        body, name=name, grid=(T // tm,),
        in_specs=[row, one, row],
        out_specs=[row, pl.BlockSpec((8, 128), lambda i: (0, 0)), one],
        out_shape=[jax.ShapeDtypeStruct((T, D), F32), jax.ShapeDtypeStruct((8, 128), F32),
                   jax.ShapeDtypeStruct((1, D), F32)],
        compiler_params=_cp("arbitrary"),
    )(h, nw, target)


def _adamw(w, g, m, v, rows, name):
    R, C = w.shape
    rows = min(rows, R)
    c1 = 1.0 - ADAM_B1 ** ADAM_STEP
    c2 = 1.0 - ADAM_B2 ** ADAM_STEP

    def body(w_ref, g_ref, m_ref, v_ref, d_ref, mo_ref, vo_ref):
        g_ = g_ref[...]
        m_ = ADAM_B1 * m_ref[...] + (1.0 - ADAM_B1) * g_
        v_ = ADAM_B2 * v_ref[...] + (1.0 - ADAM_B2) * (g_ * g_)
        mo_ref[...] = m_
        vo_ref[...] = v_
        d_ref[...] = -ADAM_LR * ((m_ / c1) / (jnp.sqrt(v_ / c2) + ADAM_EPS) + ADAM_WD * w_ref[...])

    blk = pl.BlockSpec((rows, C), lambda i: (i, 0))
    shp = jax.ShapeDtypeStruct((R, C), F32)
    return pl.pallas_call(
        body, name=name, grid=(R // rows,),
        in_specs=[blk] * 4, out_specs=[blk] * 3, out_shape=[shp] * 3,
        compiler_params=_cp("parallel"),
    )(w, g, m, v)


def _me():
    return lax.axis_index("x"), lax.axis_index("y"), lax.axis_index("c")


def _other_chips(x, y):
    return [(1 - x, y), (x, 1 - y), (1 - x, 1 - y)]


def _half_rows(ref, hf):
    half = ref.shape[-2] // 2
    rows = pl.ds(pl.multiple_of(hf * half, 16), half)
    return ref.at[rows, :] if len(ref.shape) == 2 else ref.at[:, rows, :]


class _GatherBig:
    def __init__(self, big):
        self.arrays = list(big)
        self.out_shape = [jax.ShapeDtypeStruct((N_SHARD,) + w.shape, w.dtype) for w in big]
        self.n_sem = 7 * len(big)

    @staticmethod
    def _copy(sems, k, src, dst, to):
        return pltpu.make_async_remote_copy(src_ref=src, dst_ref=dst, send_sem=sems[0].at[k], recv_sem=sems[1].at[k],
                                            device_id=to, device_id_type=MESH)

    def start(self, ins, outs, *sems):
        x, y, c = _me()
        mine = 2 * x + y
        for i, (w_ref, o_ref) in enumerate(zip(ins, outs)):
            self._copy(sems, 7 * i + 6, w_ref, o_ref.at[mine], (x, y, 1 - c)).start()
            for j, chip in enumerate(_other_chips(x, y)):
                self._copy(sems, 7 * i + j, _half_rows(w_ref, c), _half_rows(o_ref.at[mine], c), (*chip, c)).start()

    def finish(self, ins, outs, *sems):
        x, y, c = _me()
        me, sibling = (x, y, c), (x, y, 1 - c)
        chips = _other_chips(x, y)
        slot = lambda chip: 2 * chip[0] + chip[1]
        for i, o_ref in enumerate(outs):
            for j, chip in enumerate(chips):
                landed = _half_rows(o_ref.at[slot(chip)], c)
                self._copy(sems, 7 * i + j, landed, landed, me).wait_recv()
                self._copy(sems, 7 * i + 3 + j, landed, landed, sibling).start()
        for i, (w_ref, o_ref) in enumerate(zip(ins, outs)):
            for j, chip in enumerate(chips):
                passed = _half_rows(o_ref.at[slot(chip)], 1 - c)
                self._copy(sems, 7 * i + 3 + j, passed, passed, me).wait_recv()
            self._copy(sems, 7 * i + 6, o_ref.at[slot((x, y))], o_ref.at[slot((x, y))], me).wait_recv()
        for i, (w_ref, o_ref) in enumerate(zip(ins, outs)):
            self._copy(sems, 7 * i + 6, w_ref, o_ref.at[slot((x, y))], sibling).wait_send()
            for j, chip in enumerate(chips):
                self._copy(sems, 7 * i + j, _half_rows(w_ref, c), _half_rows(o_ref.at[slot((x, y))], c),
                           (*chip, c)).wait_send()
                landed = _half_rows(o_ref.at[slot(chip)], c)
                self._copy(sems, 7 * i + 3 + j, landed, landed, sibling).wait_send()


class _ChipsExchange:
    def __init__(self, pbs):
        self.arrays = list(pbs)
        self.out_shape = [jax.ShapeDtypeStruct((3,) + p.shape[1:], p.dtype) for p in pbs]
        self.n_sem = 3 * len(pbs)

    def _copies(self, ins, outs, send_sems, recv_sems):
        x, y, c = _me()
        return [pltpu.make_async_remote_copy(src_ref=ins[i].at[2 * chip[0] + chip[1]], dst_ref=outs[i].at[j],
                                             send_sem=send_sems.at[3 * i + j], recv_sem=recv_sems.at[3 * i + j],
                                             device_id=(*chip, c), device_id_type=MESH)
                for i in range(len(ins)) for j, chip in enumerate(_other_chips(x, y))]

    def start(self, ins, outs, *sems):
        for cp in self._copies(ins, outs, *sems):
            cp.start()

    def finish(self, ins, outs, *sems):
        for cp in self._copies(ins, outs, *sems):
            cp.wait()


def _carry(ex, body, n_in, n_out, grid):
    if ex is None:
        return body
    ni, no = len(ex.arrays), len(ex.out_shape)

    def carried(*refs):
        ins, ex_in = refs[:n_in], refs[n_in:n_in + ni]
        outs, ex_out = refs[n_in + ni:n_in + ni + n_out], refs[n_in + ni + n_out:n_in + ni + n_out + no]
        scratch, sems = refs[n_in + ni + n_out + no:-2], refs[-2:]
        ids = [pl.program_id(a) for a in range(len(grid))]
        first = functools.reduce(jnp.logical_and, [i == 0 for i in ids])
        last = functools.reduce(jnp.logical_and, [i == g - 1 for i, g in zip(ids, grid)])

        @pl.when(first)
        def _():
            ex.start(ex_in, ex_out, *sems)

        body(*ins, *outs, *scratch)

        @pl.when(last)
        def _():
            ex.finish(ex_in, ex_out, *sems)

    return carried


def _carry_specs(ex):
    if ex is None:
        return [], [], [], [], []
    sems = [pltpu.SemaphoreType.DMA((ex.n_sem,)), pltpu.SemaphoreType.DMA((ex.n_sem,))]
    return [ANY] * len(ex.arrays), ex.arrays, [ANY] * len(ex.out_shape), ex.out_shape, sems


def _gather_weights(big, small, name):
    nbig, nsm = len(big), len(small)
    n = nbig + nsm
    own_sem = 6 * nbig + 3 * nsm

    def body(*refs):
        ins, outs = refs[:n], refs[n:2 * n]
        send_sems, recv_sems = refs[2 * n:]
        x, y, c = _me()
        sibling = (x, y, 1 - c)
        chips = _other_chips(x, y)
        slot = lambda chip: 2 * chip[0] + chip[1]

        def copy(k, src, dst, to):
            return pltpu.make_async_remote_copy(src_ref=src, dst_ref=dst, send_sem=send_sems.at[k],
                                                recv_sem=recv_sems.at[k], device_id=to, device_id_type=MESH)

        sent = []
        for i in range(nbig):
            sent.append(copy(own_sem + i, ins[i], outs[i].at[slot((x, y))], sibling))
            sent[-1].start()
            for j, chip in enumerate(chips):
                sent.append(copy(6 * i + j, _half_rows(ins[i], c), _half_rows(outs[i].at[slot((x, y))], c), (*chip, c)))
                sent[-1].start()
        for t in range(nsm):
            w_ref, o_ref = ins[nbig + t], outs[nbig + t]
            o_ref[slot((x, y))] = w_ref[...]
            for j, chip in enumerate(chips):
                sent.append(copy(6 * nbig + 3 * t + j, w_ref, o_ref.at[slot((x, y))], (*chip, c)))
                sent[-1].start()
        for i in range(nbig):
            for j, chip in enumerate(chips):
                landed = _half_rows(outs[i].at[slot(chip)], c)
                copy(6 * i + j, landed, landed, (x, y, c)).wait_recv()
                sent.append(copy(6 * i + 3 + j, landed, landed, sibling))
                sent[-1].start()
        for t in range(nsm):
            for j, chip in enumerate(chips):
                landed = outs[nbig + t].at[slot(chip)]
                copy(6 * nbig + 3 * t + j, landed, landed, (x, y, c)).wait_recv()
        for i in range(nbig):
            for j, chip in enumerate(chips):
                passed = _half_rows(outs[i].at[slot(chip)], 1 - c)
                copy(6 * i + 3 + j, passed, passed, (x, y, c)).wait_recv()
        for i in range(nbig):
            mine = outs[i].at[slot((x, y))]
            copy(own_sem + i, mine, mine, (x, y, c)).wait_recv()
        for cp in sent:
            cp.wait_send()

    vm = pl.BlockSpec(memory_space=pltpu.VMEM)
    nsem = own_sem + nbig
    return pl.pallas_call(
        body, name=name, in_specs=[ANY] * nbig + [vm] * nsm, out_specs=[ANY] * nbig + [vm] * nsm,
        out_shape=[jax.ShapeDtypeStruct((N_SHARD,) + w.shape, w.dtype) for w in list(big) + list(small)],
        scratch_shapes=[pltpu.SemaphoreType.DMA((nsem,)), pltpu.SemaphoreType.DMA((nsem,))],
        compiler_params=pltpu.CompilerParams(has_side_effects=True),
    )(*big, *small)


def _rs_sibling(gs, name):
    n = len(gs)

    def body(*refs):
        send_sems, recv_sems = refs[2 * n:]
        x, y, c = _me()
        cps = [pltpu.make_async_remote_copy(src_ref=_half_rows(refs[i], 1 - c), dst_ref=refs[n + i],
                                            send_sem=send_sems.at[i], recv_sem=recv_sems.at[i],
                                            device_id=(x, y, 1 - c), device_id_type=MESH) for i in range(n)]
        for cp in cps:
            cp.start()
        for cp in cps:
            cp.wait()

    return pl.pallas_call(
        body, name=name, in_specs=[ANY] * n, out_specs=[ANY] * n,
        out_shape=[jax.ShapeDtypeStruct((g.shape[0], g.shape[1] // 2, g.shape[2]), g.dtype) for g in gs],
        scratch_shapes=[pltpu.SemaphoreType.DMA((n,)), pltpu.SemaphoreType.DMA((n,))],
        compiler_params=pltpu.CompilerParams(has_side_effects=True),
    )(*gs)


def _add_pair(g, other, where, name):
    ns, a, b = g.shape
    half = a // 2

    def body(w_ref, g_ref, o_ref, pb_ref, own_ref):
        t = g_ref[0].astype(F32) + o_ref[0].astype(F32)
        pb_ref[0] = _b(t)

        @pl.when(pl.program_id(0) == w_ref[1])
        def _():
            own_ref[...] = t

    return pl.pallas_call(
        body, name=name,
        grid_spec=pltpu.PrefetchScalarGridSpec(
            num_scalar_prefetch=1, grid=(ns,),
            in_specs=[pl.BlockSpec((1, half, b), lambda s, w: (s, w[0], 0)), pl.BlockSpec((1, half, b), lambda s, w: (s, 0, 0))],
            out_specs=[pl.BlockSpec((1, half, b), lambda s, w: (s, 0, 0)), pl.BlockSpec((half, b), lambda s, w: (0, 0))]),
        out_shape=[jax.ShapeDtypeStruct((ns, half, b), BF16), jax.ShapeDtypeStruct((half, b), F32)],
        compiler_params=_cp("arbitrary"),
    )(where, g, other)


def _rs_chips(pbs, name):
    n = len(pbs)

    def body(*refs):
        send_sems, recv_sems = refs[2 * n:]
        x, y, c = _me()
        cps = [pltpu.make_async_remote_copy(src_ref=refs[i].at[2 * chip[0] + chip[1]], dst_ref=refs[n + i].at[j],
                                            send_sem=send_sems.at[3 * i + j], recv_sem=recv_sems.at[3 * i + j],
                                            device_id=(*chip, c), device_id_type=MESH)
               for i in range(n) for j, chip in enumerate(_other_chips(x, y))]
        for cp in cps:
            cp.start()
        for cp in cps:
            cp.wait()

    return pl.pallas_call(
        body, name=name, in_specs=[ANY] * n, out_specs=[ANY] * n,
        out_shape=[jax.ShapeDtypeStruct((3,) + p.shape[1:], p.dtype) for p in pbs],
        scratch_shapes=[pltpu.SemaphoreType.DMA((3 * n,)), pltpu.SemaphoreType.DMA((3 * n,))],
        compiler_params=pltpu.CompilerParams(has_side_effects=True),
    )(*pbs)


def _add_four(own, got, name):
    rows, cols = own.shape
    rb = rows // 2

    def body(a_ref, b_ref, o_ref):
        o_ref[...] = ((a_ref[...] + b_ref[0].astype(F32)) + b_ref[1].astype(F32)) + b_ref[2].astype(F32)

    return pl.pallas_call(
        body, name=name, grid=(rows // rb,),
        in_specs=[pl.BlockSpec((rb, cols), lambda i: (i, 0)), pl.BlockSpec((3, rb, cols), lambda i: (0, i, 0))],
        out_specs=pl.BlockSpec((rb, cols), lambda i: (i, 0)),
        out_shape=jax.ShapeDtypeStruct((rows, cols), F32),
        compiler_params=_cp("parallel"),
    )(own, got)


def _rs_swap(halves, name):
    n = len(halves)

    def body(*refs):
        send_sems, recv_sems = refs[2 * n:]
        x, y, c = _me()
        cps = [pltpu.make_async_remote_copy(src_ref=refs[i], dst_ref=refs[n + i], send_sem=send_sems.at[i],
                                            recv_sem=recv_sems.at[i], device_id=(x, y, 1 - c), device_id_type=MESH)
               for i in range(n)]
        for cp in cps:
            cp.start()
        for cp in cps:
            cp.wait()

    return pl.pallas_call(
        body, name=name, in_specs=[ANY] * n, out_specs=[ANY] * n,
        out_shape=[jax.ShapeDtypeStruct(h.shape, h.dtype) for h in halves],
        scratch_shapes=[pltpu.SemaphoreType.DMA((n,)), pltpu.SemaphoreType.DMA((n,))],
        compiler_params=pltpu.CompilerParams(has_side_effects=True),
    )(*halves)


def _adamw_halves(w, own, got, m, v, rows, name):
    a, b = w.shape
    nblk = a // 2 // rows
    c1 = 1.0 - ADAM_B1 ** ADAM_STEP
    c2 = 1.0 - ADAM_B2 ** ADAM_STEP

    def body(w_ref, own_ref, got_ref, m_ref, v_ref, g_ref, d_ref, mo_ref, vo_ref):
        g_ = jnp.where(pl.program_id(0) == lax.axis_index("c"), own_ref[...], got_ref[...])
        g_ref[...] = g_
        m_ = ADAM_B1 * m_ref[...] + (1.0 - ADAM_B1) * g_
        v_ = ADAM_B2 * v_ref[...] + (1.0 - ADAM_B2) * (g_ * g_)
        mo_ref[...] = m_
        vo_ref[...] = v_
        d_ref[...] = -ADAM_LR * ((m_ / c1) / (jnp.sqrt(v_ / c2) + ADAM_EPS) + ADAM_WD * w_ref[...])

    whole = pl.BlockSpec((rows, b), lambda h, i: (h * nblk + i, 0))
    part = pl.BlockSpec((rows, b), lambda h, i: (i, 0))
    shp = jax.ShapeDtypeStruct((a, b), F32)
    return pl.pallas_call(
        body, name=name, grid=(2, nblk),
        in_specs=[whole, part, part, whole, whole], out_specs=[whole] * 4, out_shape=[shp] * 4,
        compiler_params=_cp("parallel", "parallel"),
    )(w, own, got, m, v)


def _allsum_small(vec, name):
    def body(v_ref, o_ref, buf_ref, send_sems, recv_sems):
        x, y, c = _me()
        me = 4 * x + 2 * y + c
        buf_ref[me] = v_ref[...]
        cps = []
        for k in range(1, 8):
            peer = (x ^ (k >> 2), y ^ ((k >> 1) & 1), c ^ (k & 1))
            cps.append(pltpu.make_async_remote_copy(src_ref=v_ref, dst_ref=buf_ref.at[me],
                                                    send_sem=send_sems.at[k - 1], recv_sem=recv_sems.at[k - 1],
                                                    device_id=peer, device_id_type=MESH))
        for cp in cps:
            cp.start()
        for k in range(1, 8):
            peer_idx = me ^ k
            pltpu.make_async_remote_copy(src_ref=v_ref, dst_ref=buf_ref.at[peer_idx],
                                         send_sem=send_sems.at[k - 1], recv_sem=recv_sems.at[k - 1],
                                         device_id=(x, y, c), device_id_type=MESH).wait_recv()
        for cp in cps:
            cp.wait_send()
        acc = buf_ref[0]
        for d in range(1, 8):
            acc = acc + buf_ref[d]
        o_ref[...] = acc

    return pl.pallas_call(
        body, name=name,
        in_specs=[pl.BlockSpec(memory_space=pltpu.VMEM)], out_specs=pl.BlockSpec(memory_space=pltpu.VMEM),
        out_shape=jax.ShapeDtypeStruct(vec.shape, F32),
        scratch_shapes=[pltpu.VMEM((8,) + vec.shape, F32), pltpu.SemaphoreType.DMA((7,)), pltpu.SemaphoreType.DMA((7,))],
        compiler_params=pltpu.CompilerParams(has_side_effects=True),
    )(vec)


BIG = ("ffn1_w_gate", "ffn1_w_up", "ffn1_w_down", "w_in", "w_out", "ffn2_w_gate", "ffn2_w_up", "ffn2_w_down")
TINY = ("w_gla_gate", "conv_w")
SHARDED = BIG + TINY


def _join_cols(w4):
    return jnp.transpose(w4, (1, 0, 2)).reshape(w4.shape[1], N_SHARD * w4.shape[2])


def _cut_cols(w):
    return jnp.transpose(w.reshape(w.shape[0], N_SHARD, w.shape[1] // N_SHARD), (1, 0, 2))


def _split_w_in(w):
    o = IN_OFF
    big = jnp.concatenate([w[:, :o[4]], w[:, o[5]:o[9]], w[:, o[11]:]], axis=1)
    small = jnp.concatenate([w[:, o[4]:o[5]], w[:, o[9]:o[11]], jnp.zeros((w.shape[0], SMALL - 32), w.dtype)], axis=1)
    return big, small


def _merge_w_in(big, small):
    return jnp.concatenate([big[:, :3072], small[:, :16], big[:, 3072:7168], small[:, 16:32], big[:, 7168:]], axis=1)


class _Comm:
    def __init__(self, where, rest_shards):
        self.where = where
        self.rest = _GatherBig(rest_shards)
        self.pairs, self.got = {}, {}

    @staticmethod
    def weights(gathered):
        G = dict(zip(BIG[3:], gathered))
        W = {n: G[n] for n in BIG[3:] if n.startswith("ffn")}
        W["w_out"] = G["w_out"].reshape(D, D)
        W["w_in_big"], W["w_in_small"] = _split_w_in(_join_cols(G["w_in"]))
        return W

    def begin(self, names, grads):
        from_sibling = _rs_sibling(grads, "rs_sibling_" + names[0])
        for n, g, o in zip(names, grads, from_sibling):
            self.pairs[n] = _add_pair(g, o, self.where, "rs_pair_" + n)
        return _ChipsExchange([self.pairs[n][0] for n in names])

    def landed(self, names, outs):
        self.got.update(zip(names, outs))

    def last(self, names, grads):
        ex = self.begin(names, grads)
        self.landed(names, _rs_chips(ex.arrays, "rs_chips"))


def _local_step(x, target, W, P, comm=None):
    wgate_pad = jnp.zeros((SMALL, GLA_H * GLA_DK), F32).at[:GLA_RANK].set(P["w_gla_gate"])
    cw8 = jnp.zeros((8, CONV_C), F32).at[:CONV_K].set(P["conv_w"])
    par = jnp.zeros((DN_H, 8, 128), F32)
    par = par.at[:, 0, :].set(jnp.broadcast_to(P["dn_a_log"].reshape(DN_H, 1), (DN_H, 128)))
    par = par.at[:, 1, :].set(jnp.broadcast_to(P["dn_dt_bias"].reshape(DN_H, 1), (DN_H, 128)))

    h1, n1, g1, u1, *rest = _ffn_fwd(x, P["ffn1_norm"], W["ffn1_w_gate"], W["ffn1_w_up"], W["ffn1_w_down"], "ffn1_fwd",
                                     carry=comm.rest if comm else None)
    if comm:
        W = dict(W, **comm.weights(rest))
    wbig, wsmall = W["w_in_big"], W["w_in_small"]
    pbig, psmall, n2 = _norm_proj(h1, P["mix_norm"], wbig, wsmall, "mix_proj")
    oa, sa = _gla_fwd(pbig, psmall, wgate_pad, P["b_gla_gate"], "gla_fwd")
    conv = _conv_fwd(pbig, cw8, "conv_fwd")
    ob, sb, *solved = _gdn_fwd(conv, psmall, par, "gdn_fwd")
    h2, yb = _merge_fwd(h1, oa, ob, pbig, P["gla_head_norm"], P["dn_head_norm"], W["w_out"], "merge_fwd")
    h3, n3, g3, u3 = _ffn_fwd(h2, P["ffn2_norm"], W["ffn2_w_gate"], W["ffn2_w_up"], W["ffn2_w_down"], "ffn2_fwd")
    dh3, loss, d_final = _loss_head(h3, P["final_norm"], target, "loss_head")

    gw, gs = {}, {"final_norm": d_final}

    def ffn_grads(tag, dh, h, n, g, u, carry=None):
        dx, dg, du, act, dfb, dnw, *landed = _ffn_bwd(dh, h, P[tag + "_norm"], g, u, W[tag + "_w_gate"],
                                                      W[tag + "_w_up"], W[tag + "_w_down"], tag + "_bwd", carry=carry)
        gw[tag + "_w_gate"] = _mm_tn(n, dg, D, FF_CUT, tag + "_dwg")
        gw[tag + "_w_up"] = _mm_tn(n, du, D, FF_CUT, tag + "_dwu")
        gw[tag + "_w_down"] = _mm_tn(act, dfb, FF_CUT, D, tag + "_dwd")
        gs[tag + "_norm"] = dnw
        return dx, landed

    dh2, _ = ffn_grads("ffn2", dh3, h2, n3, g3, u3)
    d_oa, d_ob, d_gr, d_dgate, d_ma, d_mb, gs["gla_head_norm"], gs["dn_head_norm"], dh2b = _merge_bwd(
        dh2, oa, ob, pbig, P["gla_head_norm"], P["dn_head_norm"], W["w_out"], "merge_bwd")
    gw["w_out"] = _mm_tn(yb, dh2b, D, D, "dw_out").reshape(N_SHARD, D // N_SHARD, D)
    early = ("ffn2_w_gate", "ffn2_w_up", "ffn2_w_down", "w_out")
    d_gq, d_gk, d_gv, dpre = _gla_bwd(pbig, psmall, wgate_pad, P["b_gla_gate"], sa, d_oa, "gla_bwd")
    dcq, dck, dcv, dsm, dpar, *landed = _gdn_bwd(conv, psmall, par, sb, d_ob, solved, "gdn_bwd",
                                                 carry=comm.begin(early, [gw[n] for n in early]) if comm else None)
    if comm:
        comm.landed(early, landed)
    dsmall, dwgate, gs["b_gla_gate"] = _gla_gate_bwd(dpre, psmall, wgate_pad, dsm, "gla_gate_bwd")
    gs["w_gla_gate"] = dwgate[:GLA_RANK]
    d_x3, dcw = _conv_bwd(dcq, dck, dcv, pbig, cw8, "conv_bwd")
    gs["conv_w"] = dcw[:CONV_K]
    gs["dn_a_log"] = dpar[:, 0, 0].reshape(1, DN_H)
    gs["dn_dt_bias"] = dpar[:, 0, 1].reshape(1, DN_H)
    pieces = (d_gq, d_gk, d_gv, d_gr, d_x3, d_dgate, d_ma, d_mb)
    dh1, gs["mix_norm"] = _proj_bwd(dh2, h1, P["mix_norm"], pieces, dsmall, wbig, wsmall, "proj_bwd")
    dbig = jnp.concatenate([_mm_tn(n2, p, D, 1024, "dw_in_%d" % i) for i, p in enumerate(pieces)], axis=1)
    dsml = _mm_tn(n2, dsmall, D, SMALL, "dw_in_small")
    gw["w_in"] = _cut_cols(_merge_w_in(dbig, dsml))
    grad_x, landed = ffn_grads("ffn1", dh1, x, n1, g1, u1,
                               carry=comm.begin(("w_in",), [gw["w_in"]]) if comm else None)
    if comm:
        comm.landed(("w_in",), landed)
        comm.last(BIG[:3], [gw[n] for n in BIG[:3]])
    return loss, grad_x, gw, gs


SMALL_NAMES = ("ffn1_norm", "mix_norm", "ffn2_norm", "final_norm", "b_gla_gate", "gla_head_norm", "dn_head_norm",
               "dn_a_log", "dn_dt_bias")
ROW4 = (("b_gla_gate", 512), ("gla_head_norm", 256), ("dn_head_norm", 128), ("dn_a_log", 8), ("dn_dt_bias", 8))


def _pack_small(d, loss=None):
    row4 = [d[n].reshape(-1) for n, _ in ROW4]
    row4.append(jnp.zeros((1,), F32) if loss is None else loss.reshape(1))
    row4 = jnp.concatenate(row4)
    row4 = jnp.pad(row4, (0, D - row4.shape[0]))
    rows = [d[n].reshape(-1) for n in SMALL_NAMES[:4]] + [row4]
    return jnp.concatenate([jnp.stack(rows), jnp.zeros((3, D), F32)], axis=0)


def _unpack_small(a, like):
    out = {n: a[i].reshape(like[n].shape) for i, n in enumerate(SMALL_NAMES[:4])}
    off = 0
    for n, w in ROW4:
        out[n] = a[4, off:off + w].reshape(like[n].shape)
        off += w
    return out, a[4, off]


WEIGHT_ORDER = ("ffn1_norm", "ffn1_w_gate", "ffn1_w_up", "ffn1_w_down", "mix_norm", "w_in", "w_gla_gate", "b_gla_gate",
                "conv_w", "dn_a_log", "dn_dt_bias", "gla_head_norm", "dn_head_norm", "w_out", "ffn2_norm",
                "ffn2_w_gate", "ffn2_w_up", "ffn2_w_down", "final_norm")
ADAM_ROWS = {"ffn1_w_gate": 256, "ffn1_w_up": 256, "ffn1_w_down": 176, "w_in": 128, "w_gla_gate": 16, "conv_w": 4,
             "w_out": 64, "ffn2_w_gate": 256, "ffn2_w_up": 256, "ffn2_w_down": 176}


def kernel(x, ffn1_norm, ffn1_w_gate, ffn1_w_up, ffn1_w_down, mix_norm, w_in, w_gla_gate, b_gla_gate, conv_w, dn_a_log, dn_dt_bias, gla_head_norm, dn_head_norm, w_out, ffn2_norm, ffn2_w_gate, ffn2_w_up, ffn2_w_down, final_norm, loss_target, m_ffn1_norm, m_ffn1_w_gate, m_ffn1_w_up, m_ffn1_w_down, m_mix_norm, m_w_in, m_w_gla_gate, m_b_gla_gate, m_conv_w, m_dn_a_log, m_dn_dt_bias, m_gla_head_norm, m_dn_head_norm, m_w_out, m_ffn2_norm, m_ffn2_w_gate, m_ffn2_w_up, m_ffn2_w_down, m_final_norm, v_ffn1_norm, v_ffn1_w_gate, v_ffn1_w_up, v_ffn1_w_down, v_mix_norm, v_w_in, v_w_gla_gate, v_b_gla_gate, v_conv_w, v_dn_a_log, v_dn_dt_bias, v_gla_head_norm, v_dn_head_norm, v_w_out, v_ffn2_norm, v_ffn2_w_gate, v_ffn2_w_up, v_ffn2_w_down, v_final_norm):
    given = dict(locals())
    wts = {n: given[n] for n in WEIGHT_ORDER}
    moms = {n: given["m_" + n] for n in WEIGHT_ORDER}
    vars_ = {n: given["v_" + n] for n in WEIGHT_ORDER}
    two_d = lambda a: a.reshape(a.shape[-2], a.shape[-1]) if a.ndim == 3 else a.reshape(1, -1)
    shard = {n: two_d(wts[n]) for n in SHARDED}

    gathered = _gather_weights([shard[n].astype(BF16) for n in BIG[:3]], [shard[n] for n in TINY], "gather_first")
    W = dict(zip(BIG[:3], gathered))
    P = {n: two_d(wts[n]) for n in SMALL_NAMES}
    for n, g in zip(TINY, gathered[3:]):
        P[n] = _join_cols(g)

    my_slot = 2 * lax.axis_index("x") + lax.axis_index("y")
    where = jnp.stack([lax.axis_index("c"), my_slot]).astype(jnp.int32)
    comm = _Comm(where, [shard[n].astype(BF16) for n in BIG[3:]])
    loss, grad_x, gw, gs = _local_step(x[0], loss_target[0], W, P, comm)
    halves = [_add_four(comm.pairs[n][1], comm.got[n], "rs_four_" + n) for n in BIG]
    other_halves = _rs_swap(halves, "rs_swap")

    tiny_rows = jnp.concatenate([gs["w_gla_gate"].reshape(8, D), gs["conv_w"].reshape(12, D), jnp.zeros((4, D), F32)])
    all_sum = _allsum_small(jnp.concatenate([_pack_small(gs, loss[0, 0]), tiny_rows]), "allsum_small")
    small_sum = all_sum[:8]
    small_g, loss_total = _unpack_small(small_sum, P)

    grads, delta, new_m, new_v = {}, {}, {}, {}
    for n, own, got in zip(BIG, halves, other_halves):
        res = _adamw_halves(shard[n], own, got, two_d(moms[n]), two_d(vars_[n]), ADAM_ROWS[n], "adamw_" + n)
        grads[n], delta[n], new_m[n], new_v[n] = (t.reshape(wts[n].shape) for t in res)
    for n, rows in (("w_gla_gate", all_sum[8:16]), ("conv_w", all_sum[16:28])):
        cols = shard[n].shape[1]
        grads[n] = lax.dynamic_slice_in_dim(rows.reshape(shard[n].shape[0], N_SHARD * cols), my_slot * cols, cols, axis=1)
        d, m_, v_ = _adamw(shard[n], grads[n], two_d(moms[n]), two_d(vars_[n]), ADAM_ROWS[n], "adamw_" + n)
        delta[n], new_m[n], new_v[n] = (t.reshape(wts[n].shape) for t in (d, m_, v_))
    pk = lambda src: _pack_small({n: two_d(src[n]) for n in SMALL_NAMES})
    sd, sm_, sv_ = _adamw(pk(wts), small_sum, pk(moms), pk(vars_), 8, "adamw_small")
    for res, dst in ((sd, delta), (sm_, new_m), (sv_, new_v)):
        u, _ = _unpack_small(res, wts)
        dst.update(u)
    grad_w = {n: grads[n].reshape(wts[n].shape) for n in SHARDED}
    grad_w.update({n: small_g[n].reshape(wts[n].shape) for n in SMALL_NAMES})
    return (loss_total, grad_x[None], *[grad_w[n] for n in WEIGHT_ORDER], *[delta[n] for n in WEIGHT_ORDER],
            *[new_m[n] for n in WEIGHT_ORDER], *[new_v[n] for n in WEIGHT_ORDER])
```

```python
import functools
import math

import numpy as np
import jax
import jax.numpy as jnp
from jax import lax
from jax.experimental import pallas as pl
from jax.experimental.pallas import tpu as pltpu

F32 = jnp.float32
BF16 = jnp.bfloat16
HI = lax.Precision.HIGH
MESH = pl.DeviceIdType.MESH
ANY = pl.BlockSpec(memory_space=pl.ANY)

EPS = 1e-6
D = 1024
DFF = 2816
FFN_RES = 0.5
GLA_H, GLA_DK, GLA_DV, GLA_RANK, GLA_TAU = 4, 128, 256, 16, 16.0
DN_H, DN_DK, DN_DV = 8, 128, 128
CONV_K = 4
CHUNK = 64
N_SHARD = 4
FF_CUT = DFF // N_SHARD
ADAM_LR, ADAM_B1, ADAM_B2, ADAM_EPS, ADAM_WD, ADAM_STEP = 0.001, 0.9, 0.999, 1e-08, 0.01, 10

IN_SIZES = (512, 512, 1024, 1024, 16, 1024, 1024, 1024, 1024, 8, 8, 1024, 1024)
IN_OFF = tuple(int(v) for v in np.cumsum((0,) + IN_SIZES))
D_IN = IN_OFF[-1]
BIG_COLS = 9216
SMALL = 128
PIECES = (512, 512, 1024, 1024, 3072, 1024, 1024, 1024)

VMEM_LIMIT = 56 * 1024 * 1024
ROW_BLK = 256
BIG_ROW_BLK = 512
ATT_BLK = 256
GDN_BLK = 128
GDN_HEADS = 8


def _cp(*sem):
    return pltpu.CompilerParams(dimension_semantics=sem, vmem_limit_bytes=VMEM_LIMIT)


def _sigmoid(x):
    return 1.0 / (1.0 + jnp.exp(-x))


def _softplus(x):
    return jnp.maximum(x, 0.0) + jnp.log(1.0 + jnp.exp(-jnp.abs(x)))


def _log_sigmoid(x):
    return jnp.minimum(x, 0.0) - jnp.log(1.0 + jnp.exp(-jnp.abs(x)))


def _dot(a, b, prec=None):
    return jnp.dot(a, b, preferred_element_type=F32, precision=prec)


def _dot_nt(a, b, prec=None):
    return lax.dot_general(a, b, (((1,), (1,)), ((), ())), preferred_element_type=F32, precision=prec)


def _dot_tn(a, b, prec=None):
    return lax.dot_general(a, b, (((0,), (0,)), ((), ())), preferred_element_type=F32, precision=prec)


def _b(x):
    return x.astype(BF16)


def _iota2(n, m, axis):
    return lax.broadcasted_iota(jnp.int32, (n, m), axis)


def _load_weights(pairs, sem):
    copies = [pltpu.make_async_copy(s, d, sem.at[i]) for i, (s, d) in enumerate(pairs)]
    for c in copies:
        c.start()
    for c in copies:
        c.wait()


def _ffn_fwd(h, nw, wg, wu, wd, name, carry=None):
    T = h.shape[0]
    tm = min(BIG_ROW_BLK, T)
    ex_in, ex_args, ex_out, ex_shape, ex_sems = _carry_specs(carry)

    def body(h_ref, nw_ref, wg_hbm, wu_hbm, wd_hbm, ho_ref, n_ref, g_ref, u_ref, wg_v, wu_v, wd_v, sem):
        @pl.when(pl.program_id(0) == 0)
        def _():
            _load_weights(((wg_hbm, wg_v), (wu_hbm, wu_v), (wd_hbm, wd_v)), sem)

        x = h_ref[...]
        r = lax.rsqrt(jnp.mean(x * x, axis=-1, keepdims=True) + EPS)
        nb = _b((x * r) * nw_ref[...])
        n_ref[...] = nb
        acc = jnp.zeros((tm, D), F32)
        for s in range(N_SHARD):
            g = _dot(nb, wg_v[s])
            u = _dot(nb, wu_v[s])
            g_ref[s] = _b(g)
            u_ref[s] = _b(u)
            acc += _dot(_b(g * _sigmoid(g) * u), wd_v[s])
        ho_ref[...] = x + FFN_RES * acc

    row = lambda w: pl.BlockSpec((tm, w), lambda i: (i, 0))
    cut = pl.BlockSpec((N_SHARD, tm, FF_CUT), lambda i: (0, i, 0))
    return pl.pallas_call(
        _carry(carry, body, 5, 4, (T // tm,)), name=name, grid=(T // tm,),
        in_specs=[row(D), pl.BlockSpec((1, D), lambda i: (0, 0)), ANY, ANY, ANY] + ex_in,
        out_specs=[row(D), row(D), cut, cut] + ex_out,
        out_shape=[jax.ShapeDtypeStruct((T, D), F32), jax.ShapeDtypeStruct((T, D), BF16),
                   jax.ShapeDtypeStruct((N_SHARD, T, FF_CUT), BF16),
                   jax.ShapeDtypeStruct((N_SHARD, T, FF_CUT), BF16)] + ex_shape,
        scratch_shapes=[pltpu.VMEM((N_SHARD, D, FF_CUT), BF16), pltpu.VMEM((N_SHARD, D, FF_CUT), BF16),
                        pltpu.VMEM((N_SHARD, FF_CUT, D), BF16), pltpu.SemaphoreType.DMA((3,))] + ex_sems,
        compiler_params=_cp("arbitrary"),
    )(h, nw, wg, wu, wd, *ex_args)


def _ffn_bwd_hidden(dh, g, u, wd, name, carry=None):
    T = dh.shape[0]
    tm = min(BIG_ROW_BLK, T)
    ex_in, ex_args, ex_out, ex_shape, ex_sems = _carry_specs(carry)

    def body(dh_ref, g_ref, u_ref, wd_hbm, dg_ref, du_ref, a_ref, df_ref, wd_v, sem):
        @pl.when(pl.program_id(0) == 0)
        def _():
            _load_weights(((wd_hbm, wd_v),), sem)

        dfb = _b(FFN_RES * dh_ref[...])
        df_ref[...] = dfb
        for s in range(N_SHARD):
            da = _dot_nt(dfb, wd_v[s])
            gg = g_ref[s].astype(F32)
            uu = u_ref[s].astype(F32)
            sg = _sigmoid(gg)
            silu = gg * sg
            a_ref[s] = _b(silu * uu)
            dg_ref[s] = _b(da * uu * (sg * (1.0 + gg * (1.0 - sg))))
            du_ref[s] = _b(da * silu)

    row = pl.BlockSpec((tm, D), lambda i: (i, 0))
    cut = pl.BlockSpec((N_SHARD, tm, FF_CUT), lambda i: (0, i, 0))
    cut_shape = jax.ShapeDtypeStruct((N_SHARD, T, FF_CUT), BF16)
    return pl.pallas_call(
        _carry(carry, body, 4, 4, (T // tm,)), name=name, grid=(T // tm,),
        in_specs=[row, cut, cut, ANY] + ex_in,
        out_specs=[cut, cut, cut, row] + ex_out,
        out_shape=[cut_shape, cut_shape, cut_shape, jax.ShapeDtypeStruct((T, D), BF16)] + ex_shape,
        scratch_shapes=[pltpu.VMEM((N_SHARD, FF_CUT, D), BF16), pltpu.SemaphoreType.DMA((1,))] + ex_sems,
        compiler_params=_cp("arbitrary"),
    )(dh, g, u, wd, *ex_args)


def _ffn_bwd_input(dh, h, nw, dg, du, wg, wu, name, carry=None):
    T = h.shape[0]
    tm = min(BIG_ROW_BLK, T)
    ex_in, ex_args, ex_out, ex_shape, ex_sems = _carry_specs(carry)

    def body(dh_ref, h_ref, nw_ref, dg_ref, du_ref, wg_hbm, wu_hbm, dx_ref, dnw_ref, wg_v, wu_v, sem):
        @pl.when(pl.program_id(0) == 0)
        def _():
            _load_weights(((wg_hbm, wg_v), (wu_hbm, wu_v)), sem)
            dnw_ref[...] = jnp.zeros_like(dnw_ref)

        dn = jnp.zeros((tm, D), F32)
        for s in range(N_SHARD):
            dn += _dot_nt(dg_ref[s], wg_v[s]) + _dot_nt(du_ref[s], wu_v[s])
        x = h_ref[...]
        r = lax.rsqrt(jnp.mean(x * x, axis=-1, keepdims=True) + EPS)
        xhat = x * r
        dnw_ref[...] += jnp.sum(dn * xhat, axis=0, keepdims=True)
        dxhat = dn * nw_ref[...]
        dx_ref[...] = dh_ref[...] + r * (dxhat - xhat * jnp.mean(dxhat * xhat, axis=-1, keepdims=True))

    row = pl.BlockSpec((tm, D), lambda i: (i, 0))
    one = pl.BlockSpec((1, D), lambda i: (0, 0))
    cut = pl.BlockSpec((N_SHARD, tm, FF_CUT), lambda i: (0, i, 0))
    return pl.pallas_call(
        _carry(carry, body, 7, 2, (T // tm,)), name=name, grid=(T // tm,),
        in_specs=[row, row, one, cut, cut, ANY, ANY] + ex_in,
        out_specs=[row, one] + ex_out,
        out_shape=[jax.ShapeDtypeStruct((T, D), F32), jax.ShapeDtypeStruct((1, D), F32)] + ex_shape,
        scratch_shapes=[pltpu.VMEM((N_SHARD, D, FF_CUT), BF16), pltpu.VMEM((N_SHARD, D, FF_CUT), BF16),
                        pltpu.SemaphoreType.DMA((2,))] + ex_sems,
        compiler_params=_cp("arbitrary"),
    )(dh, h, nw, dg, du, wg, wu, *ex_args)


def _mm_tn(a, b, bm, bn, name, out_dtype=BF16, tk=2048):
    cuts = a.shape[0] if a.ndim == 3 else (b.shape[0] if b.ndim == 3 else None)
    T, M = a.shape[-2:]
    N = b.shape[-1]
    tk = min(tk, T)
    bm, bn = min(bm, M), min(bn, N)
    nk = T // tk

    def body(a_ref, b_ref, o_ref, acc_ref):
        k = pl.program_id(3)

        @pl.when(k == 0)
        def _():
            acc_ref[...] = jnp.zeros_like(acc_ref)

        av = a_ref[0] if a.ndim == 3 else a_ref[...]
        bv = b_ref[0] if b.ndim == 3 else b_ref[...]
        acc_ref[...] += _dot_tn(_b(av), _b(bv))

        @pl.when(k == nk - 1)
        def _():
            res = acc_ref[...].astype(out_dtype)
            if cuts is None:
                o_ref[...] = res
            else:
                o_ref[0] = res

    a_spec = (pl.BlockSpec((1, tk, bm), lambda s, i, j, k: (s, k, i)) if a.ndim == 3
              else pl.BlockSpec((tk, bm), lambda s, i, j, k: (k, i)))
    b_spec = (pl.BlockSpec((1, tk, bn), lambda s, i, j, k: (s, k, j)) if b.ndim == 3
              else pl.BlockSpec((tk, bn), lambda s, i, j, k: (k, j)))
    if cuts is None:
        o_spec, o_shape = pl.BlockSpec((bm, bn), lambda s, i, j, k: (i, j)), (M, N)
    else:
        o_spec, o_shape = pl.BlockSpec((1, bm, bn), lambda s, i, j, k: (s, i, j)), (cuts, M, N)
    return pl.pallas_call(
        body, name=name, grid=(cuts or 1, M // bm, N // bn, nk),
        in_specs=[a_spec, b_spec], out_specs=o_spec,
        out_shape=jax.ShapeDtypeStruct(o_shape, out_dtype),
        scratch_shapes=[pltpu.VMEM((bm, bn), F32)],
        compiler_params=_cp("parallel", "parallel", "parallel", "arbitrary"),
    )(a, b)


def _norm_proj(h, nw, wbig, wsmall, name):
    T = h.shape[0]
    tm = min(512, T)
    tn = 1536

    def body(h_ref, nw_ref, wb_hbm, ws_ref, pb_ref, ps_ref, n_ref, wb_v, sem):
        @pl.when(pl.program_id(0) == 0)
        def _():
            _load_weights(((wb_hbm, wb_v),), sem)

        x = h_ref[...]
        r = lax.rsqrt(jnp.mean(x * x, axis=-1, keepdims=True) + EPS)
        nb = _b((x * r) * nw_ref[...])
        n_ref[...] = nb
        ps_ref[...] = _dot(nb, ws_ref[...])
        for j in range(BIG_COLS // tn):
            pb_ref[:, j * tn:(j + 1) * tn] = _b(_dot(nb, wb_v[:, j * tn:(j + 1) * tn]))

    row = lambda w: pl.BlockSpec((tm, w), lambda i: (i, 0))
    return pl.pallas_call(
        body, name=name, grid=(T // tm,),
        in_specs=[row(D), pl.BlockSpec((1, D), lambda i: (0, 0)), ANY, pl.BlockSpec((D, SMALL), lambda i: (0, 0))],
        out_specs=[row(BIG_COLS), row(SMALL), row(D)],
        out_shape=[jax.ShapeDtypeStruct((T, BIG_COLS), BF16), jax.ShapeDtypeStruct((T, SMALL), F32),
                   jax.ShapeDtypeStruct((T, D), BF16)],
        scratch_shapes=[pltpu.VMEM((D, BIG_COLS), BF16), pltpu.SemaphoreType.DMA((1,))],
        compiler_params=_cp("arbitrary"),
    )(h, nw, wbig, wsmall)


def _proj_bwd(dh, h, nw, pieces, dsmall, wbig, wsmall, name):
    T = h.shape[0]
    tm = min(BIG_ROW_BLK, T)
    offs = tuple(int(v) for v in np.cumsum((0,) + PIECES))

    def body(dh_ref, h_ref, nw_ref, *rest):
        p_refs = rest[:len(PIECES)]
        ds_ref, wb_hbm, ws_ref, dx_ref, dnw_ref, wb_v, sem = rest[len(PIECES):]

        @pl.when(pl.program_id(0) == 0)
        def _():
            _load_weights(((wb_hbm, wb_v),), sem)
            dnw_ref[...] = jnp.zeros_like(dnw_ref)

        dn = _dot_nt(_b(ds_ref[...]), ws_ref[...])
        for p_ref, lo, wdt in zip(p_refs, offs, PIECES):
            dn += _dot_nt(p_ref[...], wb_v[:, lo:lo + wdt])
        x = h_ref[...]
        r = lax.rsqrt(jnp.mean(x * x, axis=-1, keepdims=True) + EPS)
        xhat = x * r
        dnw_ref[...] += jnp.sum(dn * xhat, axis=0, keepdims=True)
        dxhat = dn * nw_ref[...]
        dx_ref[...] = dh_ref[...] + r * (dxhat - xhat * jnp.mean(dxhat * xhat, axis=-1, keepdims=True))

    row = lambda w: pl.BlockSpec((tm, w), lambda i: (i, 0))
    one = pl.BlockSpec((1, D), lambda i: (0, 0))
    return pl.pallas_call(
        body, name=name, grid=(T // tm,),
        in_specs=[row(D), row(D), one] + [row(w) for w in PIECES] + [row(SMALL), ANY, pl.BlockSpec((D, SMALL), lambda i: (0, 0))],
        out_specs=[row(D), one],
        out_shape=[jax.ShapeDtypeStruct((T, D), F32), jax.ShapeDtypeStruct((1, D), F32)],
        scratch_shapes=[pltpu.VMEM((D, BIG_COLS), BF16), pltpu.SemaphoreType.DMA((1,))],
        compiler_params=_cp("arbitrary"),
    )(dh, h, nw, *pieces, dsmall, wbig, wsmall)


def _gla_block(q_ref, k_ref, sm_ref, wg_ref, bg_ref, nc, tril):
    nbat = GLA_H * nc
    q = _heads_first(q_ref[...].astype(F32), nc, GLA_DK)
    k = _heads_first(k_ref[...].astype(F32), nc, GLA_DK)
    pre = _heads_first(_dot(sm_ref[...], wg_ref[...], HI) + bg_ref[...], nc, GLA_DK)
    la = _log_sigmoid(pre) * (1.0 / GLA_TAU)
    bc = _bmm(jnp.broadcast_to(tril, (nbat, CHUNK, CHUNK)), la, HI)
    bl = bc[:, CHUNK - 1:CHUNK, :]
    eb = jnp.exp(bc)
    enb = jnp.exp(-bc)
    ebl = jnp.exp(bl - bc)
    q_in = q * (GLA_DK ** -0.5) * eb
    k_out = k * enb
    k_st = k * ebl
    a_ch = jnp.exp(bl)
    return pre, eb, enb, ebl, q_in, k_out, k_st, a_ch


def _gla_specs(blk, idx):
    hk, hv = GLA_H * GLA_DK, GLA_H * GLA_DV
    return [pl.BlockSpec((blk, hk), lambda j: (idx(j), 0)),
            pl.BlockSpec((blk, hk), lambda j: (idx(j), 1)),
            pl.BlockSpec((blk, hv), lambda j: (idx(j), 1)),
            pl.BlockSpec((blk, SMALL), lambda j: (idx(j), 0)),
            pl.BlockSpec((SMALL, hk), lambda j: (0, 0)),
            pl.BlockSpec((1, hk), lambda j: (0, 0))]


def _gla_fwd(pbig, psmall, wgate, bgate, name):
    T = pbig.shape[0]
    blk = min(ATT_BLK, T)
    nc = blk // CHUNK

    def body(q_ref, k_ref, v_ref, sm_ref, wg_ref, bg_ref, o_ref, ss_ref, st_ref):
        @pl.when(pl.program_id(0) == 0)
        def _():
            st_ref[...] = jnp.zeros_like(st_ref)

        causal = _iota2(CHUNK, CHUNK, 0) >= _iota2(CHUNK, CHUNK, 1)
        _, _, _, _, q_in, k_out, k_st, a_ch = _gla_block(q_ref, k_ref, sm_ref, wg_ref, bg_ref, nc, causal.astype(F32))
        v = _heads_first(v_ref[...], nc, GLA_DV)
        qb = _b(q_in)
        sc = jnp.where(causal, _bmm_nt(qb, _b(k_out)), 0.0)
        kv = _bmm_tn(v, _b(k_st))
        before = [None] * (GLA_H * nc)
        for i in range(GLA_H):
            st = st_ref[i]
            for c in range(nc):
                n = i * nc + c
                before[n] = st
                st = st * a_ch[n] + kv[n]
            st_ref[i] = st
        states = jnp.stack(before)
        ss_ref[...] = states.reshape(GLA_H, nc, GLA_DV, GLA_DK)
        o_ref[...] = _heads_last(_bmm(_b(sc), v) + _bmm_nt(qb, _b(states)), nc)

    return pl.pallas_call(
        body, name=name, grid=(T // blk,),
        in_specs=_gla_specs(blk, lambda j: j),
        out_specs=[pl.BlockSpec((blk, GLA_H * GLA_DV), lambda j: (j, 0)),
                   pl.BlockSpec((GLA_H, nc, GLA_DV, GLA_DK), lambda j: (0, j, 0, 0))],
        out_shape=[jax.ShapeDtypeStruct((T, GLA_H * GLA_DV), F32),
                   jax.ShapeDtypeStruct((GLA_H, T // CHUNK, GLA_DV, GLA_DK), F32)],
        scratch_shapes=[pltpu.VMEM((GLA_H, GLA_DV, GLA_DK), F32)],
        compiler_params=_cp("arbitrary"),
    )(pbig, pbig, pbig, psmall, wgate, bgate)


def _gla_bwd(pbig, psmall, wgate, bgate, states, do, name):
    T = pbig.shape[0]
    blk = min(ATT_BLK, T)
    nc = blk // CHUNK
    nb = T // blk
    nbat = GLA_H * nc

    def body(q_ref, k_ref, v_ref, sm_ref, wg_ref, bg_ref, ss_ref, do_ref, dq_ref, dk_ref, dv_ref, dpre_ref, dst_ref):
        @pl.when(pl.program_id(0) == 0)
        def _():
            dst_ref[...] = jnp.zeros_like(dst_ref)

        causal = _iota2(CHUNK, CHUNK, 0) >= _iota2(CHUNK, CHUNK, 1)
        triu = (_iota2(CHUNK, CHUNK, 0) <= _iota2(CHUNK, CHUNK, 1)).astype(F32)
        pre, eb, enb, ebl, q_in, k_out, k_st, a_ch = _gla_block(q_ref, k_ref, sm_ref, wg_ref, bg_ref, nc,
                                                                causal.astype(F32))
        v = _heads_first(v_ref[...], nc, GLA_DV)
        dob = _b(_heads_first(do_ref[...], nc, GLA_DV))
        st = ss_ref[...].reshape(nbat, GLA_DV, GLA_DK)
        qb, kob, kstb = _b(q_in), _b(k_out), _b(k_st)
        qdo = _bmm_tn(dob, qb)
        after = [None] * nbat
        for i in range(GLA_H):
            dst = dst_ref[i]
            for c in range(nc - 1, -1, -1):
                n = i * nc + c
                after[n] = dst
                dst = dst * a_ch[n] + qdo[n]
            dst_ref[i] = dst
        dsa = jnp.stack(after)
        dsb = _b(dsa)
        sc = jnp.where(causal, _bmm_nt(qb, kob), 0.0)
        dsc = _b(jnp.where(causal, _bmm_nt(dob, v), 0.0))
        dq_in = _bmm(dob, _b(st)) + _bmm(dsc, kob)
        dk_out = _bmm_tn(dsc, qb)
        dk_st = _bmm(v, dsb)
        dv_ref[...] = _b(_heads_last(_bmm_tn(_b(sc), dob) + _bmm_nt(kstb, dsb), nc))
        da_ch = jnp.sum(st * dsa, axis=1, keepdims=True)
        tk = dk_st * k_st
        db = dq_in * q_in - dk_out * k_out - tk
        db_last = jnp.sum(tk, axis=1, keepdims=True) + da_ch * a_ch
        dq_ref[...] = _b(_heads_last(dq_in * (GLA_DK ** -0.5) * eb, nc))
        dk_ref[...] = _b(_heads_last(dk_out * enb + dk_st * ebl, nc))
        dla = _bmm(jnp.broadcast_to(triu, (nbat, CHUNK, CHUNK)), db, HI) + db_last
        dpre_ref[...] = _heads_last(dla * (1.0 / GLA_TAU) * _sigmoid(-pre), nc)

    r = lambda j: nb - 1 - j
    hk, hv = GLA_H * GLA_DK, GLA_H * GLA_DV
    return pl.pallas_call(
        body, name=name, grid=(nb,),
        in_specs=_gla_specs(blk, r) + [pl.BlockSpec((GLA_H, nc, GLA_DV, GLA_DK), lambda j: (0, r(j), 0, 0)),
                                      pl.BlockSpec((blk, hv), lambda j: (r(j), 0))],
        out_specs=[pl.BlockSpec((blk, hk), lambda j: (r(j), 0)), pl.BlockSpec((blk, hk), lambda j: (r(j), 0)),
                   pl.BlockSpec((blk, hv), lambda j: (r(j), 0)), pl.BlockSpec((blk, hk), lambda j: (r(j), 0))],
        out_shape=[jax.ShapeDtypeStruct((T, hk), BF16), jax.ShapeDtypeStruct((T, hk), BF16),
                   jax.ShapeDtypeStruct((T, hv), BF16), jax.ShapeDtypeStruct((T, hk), F32)],
        scratch_shapes=[pltpu.VMEM((GLA_H, GLA_DV, GLA_DK), F32)],
        compiler_params=_cp("arbitrary"),
    )(pbig, pbig, pbig, psmall, wgate, bgate, states, do)


def _gla_gate_bwd(dpre, psmall, wgate, dsm, name):
    T = dpre.shape[0]
    tm = min(512, T)
    W = GLA_H * GLA_DK
    ngrp = dsm.shape[0]

    def body(dp_ref, sm_ref, wg_ref, dsm_ref, ds_ref, dw_ref, db_ref):
        @pl.when(pl.program_id(0) == 0)
        def _():
            dw_ref[...] = jnp.zeros_like(dw_ref)
            db_ref[...] = jnp.zeros_like(db_ref)

        dp = dp_ref[...]
        ds = _dot_nt(dp, wg_ref[...], HI)
        for i in range(ngrp):
            ds += dsm_ref[i]
        ds_ref[...] = ds
        dw_ref[...] += _dot_tn(sm_ref[...], dp, HI)
        db_ref[...] += jnp.sum(dp, axis=0, keepdims=True)

    return pl.pallas_call(
        body, name=name, grid=(T // tm,),
        in_specs=[pl.BlockSpec((tm, W), lambda i: (i, 0)), pl.BlockSpec((tm, SMALL), lambda i: (i, 0)),
                  pl.BlockSpec((SMALL, W), lambda i: (0, 0)), pl.BlockSpec((ngrp, tm, SMALL), lambda i: (0, i, 0))],
        out_specs=[pl.BlockSpec((tm, SMALL), lambda i: (i, 0)), pl.BlockSpec((SMALL, W), lambda i: (0, 0)),
                   pl.BlockSpec((1, W), lambda i: (0, 0))],
        out_shape=[jax.ShapeDtypeStruct((T, SMALL), F32), jax.ShapeDtypeStruct((SMALL, W), F32),
                   jax.ShapeDtypeStruct((1, W), F32)],
        compiler_params=_cp("arbitrary"),
    )(dpre, psmall, wgate, dsm)


CONV_C = 3 * 1024
CONV_BLK = 256


def _conv_fwd(pbig, cw8, name):
    T = pbig.shape[0]
    blk = min(CONV_BLK, T)

    def body(x_ref, w_ref, c_ref, prev_ref):
        @pl.when(pl.program_id(0) == 0)
        def _():
            prev_ref[...] = jnp.zeros_like(prev_ref)

        x = x_ref[...].astype(F32)
        prev = prev_ref[...]
        row8 = _iota2(8, CONV_C, 0)
        acc = x * w_ref[CONV_K - 1:CONV_K, :]
        for s in range(1, CONV_K):
            xs = pltpu.roll(x, s, 0)
            top = jnp.where(row8 < s, pltpu.roll(prev, s, 0), xs[:8])
            xs = jnp.concatenate([top, xs[8:]], axis=0)
            acc += xs * w_ref[CONV_K - 1 - s:CONV_K - s, :]
        c_ref[...] = _b(acc)
        prev_ref[...] = x[blk - 8:]

    return pl.pallas_call(
        body, name=name, grid=(T // blk,),
        in_specs=[pl.BlockSpec((blk, CONV_C), lambda i: (i, 1)), pl.BlockSpec((8, CONV_C), lambda i: (0, 0))],
        out_specs=pl.BlockSpec((blk, CONV_C), lambda i: (i, 0)),
        out_shape=jax.ShapeDtypeStruct((T, CONV_C), BF16),
        scratch_shapes=[pltpu.VMEM((8, CONV_C), F32)],
        compiler_params=_cp("arbitrary"),
    )(pbig, cw8)


def _conv_bwd(dcq, dck, dcv, pbig, cw8, name):
    T = pbig.shape[0]
    blk = min(CONV_BLK, T)
    nb = T // blk

    def body(dq_ref, dk_ref, dv_ref, x_ref, w_ref, dx_ref, dw_ref, nxt_ref):
        @pl.when(pl.program_id(0) == 0)
        def _():
            nxt_ref[...] = jnp.zeros_like(nxt_ref)
            dw_ref[...] = jnp.zeros_like(dw_ref)

        dc = jnp.concatenate([dq_ref[...], dk_ref[...], dv_ref[...]], axis=1).astype(F32)
        x = x_ref[...].astype(F32)
        nxt = nxt_ref[...]
        row8 = _iota2(8, CONV_C, 0)
        acc = dc * w_ref[CONV_K - 1:CONV_K, :]
        dws = [jnp.sum(dc * x, axis=0, keepdims=True)]
        for s in range(1, CONV_K):
            ds = pltpu.roll(dc, blk - s, 0)
            bot = jnp.where(row8 >= 8 - s, pltpu.roll(nxt, 8 - s, 0), ds[blk - 8:])
            ds = jnp.concatenate([ds[:blk - 8], bot], axis=0)
            acc += ds * w_ref[CONV_K - 1 - s:CONV_K - s, :]
            dws.append(jnp.sum(ds * x, axis=0, keepdims=True))
        dx_ref[...] = _b(acc)
        dw_ref[...] += jnp.concatenate(dws[::-1] + [jnp.zeros((8 - CONV_K, CONV_C), F32)], axis=0)
        nxt_ref[...] = dc[:8]

    part = pl.BlockSpec((blk, 1024), lambda i: (nb - 1 - i, 0))
    return pl.pallas_call(
        body, name=name, grid=(nb,),
        in_specs=[part, part, part, pl.BlockSpec((blk, CONV_C), lambda i: (nb - 1 - i, 1)),
                  pl.BlockSpec((8, CONV_C), lambda i: (0, 0))],
        out_specs=[pl.BlockSpec((blk, CONV_C), lambda i: (nb - 1 - i, 0)), pl.BlockSpec((8, CONV_C), lambda i: (0, 0))],
        out_shape=[jax.ShapeDtypeStruct((T, CONV_C), BF16), jax.ShapeDtypeStruct((8, CONV_C), F32)],
        scratch_shapes=[pltpu.VMEM((8, CONV_C), F32)],
        compiler_params=_cp("arbitrary"),
    )(dcq, dck, dcv, pbig, cw8)


def _col(x, lane):
    sel = _iota2(x.shape[0], x.shape[1], 1) == lane
    return jnp.broadcast_to(jnp.sum(jnp.where(sel, x, 0.0), axis=1, keepdims=True), x.shape)


def _bmm(a, b, prec=None):
    return jnp.einsum("bij,bjk->bik", a, b, preferred_element_type=F32, precision=prec)


def _bmm_nt(a, b, prec=None):
    return jnp.einsum("bij,bkj->bik", a, b, preferred_element_type=F32, precision=prec)


def _bmm_tn(a, b, prec=None):
    return jnp.einsum("bji,bjk->bik", a, b, preferred_element_type=F32, precision=prec)


def _unit_lower_inverse(low):
    eye = (_iota2(CHUNK, CHUNK, 0) == _iota2(CHUNK, CHUNK, 1)).astype(F32)
    xk = -low
    inv = eye + xk
    for _ in range(5):
        xb = _b(xk)
        xk = _bmm(xb, xb)
        inv = inv + _bmm(_b(inv), _b(xk))
    resid = eye - _bmm(eye + low, inv, HI)
    return inv + _bmm(inv, resid, HI)


def _heads_first(x, nc, w=128):
    hb = x.shape[1] // w
    return jnp.concatenate([x[:, i * w:(i + 1) * w].reshape(nc, CHUNK, w) for i in range(hb)], axis=0)


def _heads_last(x, nc):
    hb = x.shape[0] // nc
    return jnp.concatenate([x[i * nc:(i + 1) * nc].reshape(nc * CHUNK, x.shape[2]) for i in range(hb)], axis=1)


def _gdn_block(cq_ref, ck_ref, cv_ref, sm_ref, par_ref, h0, hb, nc, masks, solved=None):
    causal, strict, tril, eye = masks
    nbat = hb * nc
    cq = _heads_first(cq_ref[...].astype(F32), nc)
    ck = _heads_first(ck_ref[...].astype(F32), nc)
    cv = _heads_first(cv_ref[...].astype(F32), nc)
    sq, sk, sv = _sigmoid(cq), _sigmoid(ck), _sigmoid(cv)
    q, k, v = cq * sq, ck * sk, cv * sv
    rq = lax.rsqrt(jnp.sum(q * q, axis=-1, keepdims=True) + EPS)
    rk = lax.rsqrt(jnp.sum(k * k, axis=-1, keepdims=True) + EPS)
    qh, kn = q * rq, k * rk
    qn = qh * (DN_DK ** -0.5)
    sm = sm_ref[...]
    per_head = lambda fn: jnp.concatenate([fn(i) for i in range(hb)], axis=0)
    braw = per_head(lambda i: _col(sm, GLA_RANK + h0 + i).reshape(nc, CHUNK, 128))
    araw = per_head(lambda i: _col(sm, GLA_RANK + DN_H + h0 + i).reshape(nc, CHUNK, 128))
    ea = per_head(lambda i: jnp.broadcast_to(jnp.exp(par_ref[i, 0:1, :])[None], (nc, 1, 128)))
    bias = per_head(lambda i: jnp.broadcast_to(par_ref[i, 1:2, :][None], (nc, 1, 128)))
    beta = _sigmoid(braw)
    sp_arg = araw + bias
    g = -ea * _softplus(sp_arg)
    G = _bmm(jnp.broadcast_to(tril, (nbat, CHUNK, CHUNK)), g, HI)
    gc = G[:, :, :CHUNK]
    grow = jnp.sum(eye * gc, axis=1, keepdims=True)
    decay = jnp.exp(jnp.where(causal, gc - grow, -1e30))
    kb = kn * beta
    A = _bmm_nt(_b(kb), _b(kn))
    eG = jnp.exp(G)
    gl = G[:, CHUNK - 1:CHUNK, :]
    eGl = jnp.exp(gl - G)
    g_ch = jnp.exp(gl)
    rv = v * beta
    rkk = kb * eG
    if solved is None:
        tinv_b = _b(_unit_lower_inverse(jnp.where(strict, A * decay, 0.0)))
        u = _bmm(tinv_b, _b(rv))
        w = _b(_bmm(tinv_b, _b(rkk)))
    else:
        tinv_b, u, w = solved
    B = _bmm_nt(_b(qn), _b(kn))
    qk = jnp.where(causal, B * decay, 0.0)
    q_dec = qn * eG
    k_st = kn * eGl
    return dict(cq=cq, ck=ck, cv=cv, sq=sq, sk=sk, sv=sv, q=q, k=k, v=v, rq=rq, rk=rk, qh=qh, kn=kn, qn=qn,
                beta=beta, ea=ea, sp_arg=sp_arg, g=g, G=G, decay=decay, kb=kb, A=A, tinv_b=tinv_b, eG=eG, eGl=eGl,
                g_ch=g_ch, rv=rv, rkk=rkk, u=u, w=w, B=B, qk=qk, q_dec=q_dec, k_st=k_st)


def _gdn_masks():
    r, c = _iota2(CHUNK, CHUNK, 0), _iota2(CHUNK, CHUNK, 1)
    return r >= c, r > c, (r >= c).astype(F32), (r == c).astype(F32)


def _gdn_specs(blk, hb, idx):
    ng = DN_H // hb
    return [pl.BlockSpec((blk, hb * DN_DK), lambda h, j: (idx(j), h)),
            pl.BlockSpec((blk, hb * DN_DK), lambda h, j: (idx(j), ng + h)),
            pl.BlockSpec((blk, hb * DN_DV), lambda h, j: (idx(j), 2 * ng + h)),
            pl.BlockSpec((blk, SMALL), lambda h, j: (idx(j), 0)),
            pl.BlockSpec((hb, 8, 128), lambda h, j: (h, 0, 0))]


def _gdn_solved_specs(blk, hb, idx):
    nc = blk // CHUNK
    spec = lambda w: pl.BlockSpec((hb, nc, CHUNK, w), lambda h, j: (h, idx(j), 0, 0))
    return [spec(CHUNK), spec(DN_DV), spec(DN_DK)]


def _gdn_fwd(conv, psmall, par, name):
    T = conv.shape[0]
    blk = min(GDN_BLK, T)
    nc = blk // CHUNK
    hb = GDN_HEADS
    N = T // CHUNK

    def body(cq_ref, ck_ref, cv_ref, sm_ref, par_ref, o_ref, ss_ref, ti_ref, u_ref, w_ref, s_ref):
        @pl.when(pl.program_id(1) == 0)
        def _():
            s_ref[...] = jnp.zeros_like(s_ref)

        f = _gdn_block(cq_ref, ck_ref, cv_ref, sm_ref, par_ref, pl.program_id(0) * hb, hb, nc, _gdn_masks())
        ti_ref[...] = f["tinv_b"].reshape(hb, nc, CHUNK, CHUNK)
        u_ref[...] = f["u"].reshape(hb, nc, CHUNK, DN_DV)
        w_ref[...] = f["w"].reshape(hb, nc, CHUNK, DN_DK)
        wb, qdb, kstb, qkb = f["w"], _b(f["q_dec"]), _b(f["k_st"]), _b(f["qk"])
        S = [s_ref[i] for i in range(hb)]
        for c in range(nc):
            for i in range(hb):
                n = i * nc + c
                ss_ref[i, c] = S[i]
                Sb = _b(S[i])
                v_new = _b(f["u"][n] - _dot(wb[n], Sb))
                o_ref[pl.ds(c * CHUNK, CHUNK), i * DN_DV:(i + 1) * DN_DV] = _dot(qdb[n], Sb) + _dot(qkb[n], v_new)
                S[i] = S[i] * f["g_ch"][n] + _dot_tn(kstb[n], v_new)
        for i in range(hb):
            s_ref[i] = S[i]

    return pl.pallas_call(
        body, name=name, grid=(DN_H // hb, T // blk),
        in_specs=_gdn_specs(blk, hb, lambda j: j),
        out_specs=[pl.BlockSpec((blk, hb * DN_DV), lambda h, j: (j, h)),
                   pl.BlockSpec((hb, nc, DN_DK, DN_DV), lambda h, j: (h, j, 0, 0))]
        + _gdn_solved_specs(blk, hb, lambda j: j),
        out_shape=[jax.ShapeDtypeStruct((T, DN_H * DN_DV), F32), jax.ShapeDtypeStruct((DN_H, N, DN_DK, DN_DV), F32),
                   jax.ShapeDtypeStruct((DN_H, N, CHUNK, CHUNK), BF16), jax.ShapeDtypeStruct((DN_H, N, CHUNK, DN_DV), F32),
                   jax.ShapeDtypeStruct((DN_H, N, CHUNK, DN_DK), BF16)],
        scratch_shapes=[pltpu.VMEM((hb, DN_DK, DN_DV), F32)],
        compiler_params=_cp("parallel", "arbitrary"),
    )(conv, conv, conv, psmall, par)


def _gdn_bwd(conv, psmall, par, states, do, solved, name, carry=None):
    ex_in, ex_args, ex_out, ex_shape, ex_sems = _carry_specs(carry)
    T = conv.shape[0]
    blk = min(GDN_BLK, T)
    nc = blk // CHUNK
    nb = T // blk
    hb = GDN_HEADS
    nbat = hb * nc
    rsum = lambda x: jnp.sum(x, axis=-1, keepdims=True)

    def body(cq_ref, ck_ref, cv_ref, sm_ref, par_ref, ss_ref, do_ref, ti_ref, u_ref, w_ref,
             dcq_ref, dck_ref, dcv_ref, dsm_ref, dpar_ref, ds_ref):
        @pl.when(pl.program_id(1) == 0)
        def _():
            ds_ref[...] = jnp.zeros_like(ds_ref)
            dpar_ref[...] = jnp.zeros_like(dpar_ref)

        masks = _gdn_masks()
        causal, strict, tril, eye = masks
        triu = (_iota2(CHUNK, CHUNK, 0) <= _iota2(CHUNK, CHUNK, 1)).astype(F32)
        lane = _iota2(CHUNK, 128, 1)
        last_row = _iota2(CHUNK, 128, 0) == CHUNK - 1
        h0 = pl.program_id(0) * hb
        solved = (ti_ref[...].reshape(nbat, CHUNK, CHUNK), u_ref[...].reshape(nbat, CHUNK, DN_DV),
                  w_ref[...].reshape(nbat, CHUNK, DN_DK))
        f = _gdn_block(cq_ref, ck_ref, cv_ref, sm_ref, par_ref, h0, hb, nc, masks, solved)
        S = ss_ref[...].reshape(nbat, DN_DK, DN_DV)
        Sb = _b(S)
        do_ = _b(_heads_first(do_ref[...], nc))
        wb, qdb, kstb, qkb = _b(f["w"]), _b(f["q_dec"]), _b(f["k_st"]), _b(f["qk"])
        vnb = _b(f["u"] - _bmm(wb, Sb))
        dvn0 = _bmm_tn(qkb, do_)
        qdo = _bmm_tn(qdb, do_)
        dS = [ds_ref[i] for i in range(hb)]
        after = [None] * nbat
        for c in range(nc - 1, -1, -1):
            for i in range(hb):
                n = i * nc + c
                after[n] = dS[i]
                dvn_c = _b(dvn0[n] + _dot(kstb[n], _b(dS[i])))
                dS[i] = dS[i] * f["g_ch"][n] + qdo[n] - _dot_tn(wb[n], dvn_c)
        for i in range(hb):
            ds_ref[i] = dS[i]
        dSa = jnp.stack(after)
        dSb = _b(dSa)
        dvn = dvn0 + _bmm(kstb, dSb)
        dvnb = _b(dvn)
        dq_dec = _bmm_nt(do_, Sb)
        dqk = jnp.where(causal, _bmm_nt(do_, vnb), 0.0)
        dk_st = _bmm_nt(vnb, dSb)
        dg_ch = jnp.sum(rsum(S * dSa), axis=1, keepdims=True)
        dw = -_bmm_nt(dvnb, Sb)
        drv = _bmm_tn(f["tinv_b"], dvnb)
        drk = _bmm_tn(f["tinv_b"], _b(dw))
        dlow = jnp.where(strict, -(_bmm_nt(_b(drv), _b(f["u"])) + _bmm_nt(_b(drk), wb)), 0.0)
        dv = drv * f["beta"]
        dbeta = rsum(drv * f["v"])
        dkb = drk * f["eG"]
        dG = rsum(drk * f["rkk"])
        dA = dlow * f["decay"]
        ddec = dlow * f["A"]
        dkb += _bmm(_b(dA), _b(f["kn"]))
        dkn = _bmm_tn(_b(dA), _b(f["kb"]))
        dB = dqk * f["decay"]
        ddec += dqk * f["B"]
        dqn = _bmm(_b(dB), _b(f["kn"]))
        dkn += _bmm_tn(_b(dB), _b(f["qn"]))
        dD = ddec * f["decay"]
        dG += rsum(dD) - rsum(eye * jnp.sum(dD, axis=1, keepdims=True))
        dqn += dq_dec * f["eG"]
        dG += rsum(dq_dec * f["q_dec"])
        dkn += dk_st * f["eGl"]
        tks = rsum(dk_st * f["k_st"])
        dG -= tks
        dG_last = jnp.sum(tks, axis=1, keepdims=True) + dg_ch * f["g_ch"][:, :, :1]
        dkn += dkb * f["beta"]
        dbeta += rsum(dkb * f["kn"])
        dGf = jnp.broadcast_to(dG, (nbat, CHUNK, 128)) + jnp.where(last_row, dG_last, 0.0)
        dg = _bmm(jnp.broadcast_to(triu, (nbat, CHUNK, CHUNK)), dGf, HI)
        dbraw = dbeta * f["beta"][:, :, :1] * (1.0 - f["beta"][:, :, :1])
        daraw = dg * (-f["ea"]) * _sigmoid(f["sp_arg"])
        both = lambda t: jnp.sum(jnp.sum(t, axis=1, keepdims=True), axis=0)
        dgg = dg * f["g"]
        dsm = jnp.zeros((nc, CHUNK, SMALL), F32)
        for i in range(hb):
            mine = slice(i * nc, (i + 1) * nc)
            dsm += (jnp.where(lane == GLA_RANK + h0 + i, dbraw[mine], 0.0)
                    + jnp.where(lane == GLA_RANK + DN_H + h0 + i, daraw[mine], 0.0))
            dpar = jnp.where(lane[:1] == 0, both(dgg[mine]), jnp.where(lane[:1] == 1, both(daraw[mine]), 0.0))
            dpar_ref[i] += jnp.broadcast_to(dpar, (8, 128))
        dsm_ref[0] = dsm.reshape(blk, SMALL)
        dqh = dqn * (DN_DK ** -0.5)
        dq = f["rq"] * (dqh - f["qh"] * rsum(dqh * f["qh"]))
        dk = f["rk"] * (dkn - f["kn"] * rsum(dkn * f["kn"]))
        dsilu = lambda x, s: s * (1.0 + x * (1.0 - s))
        dcq_ref[...] = _b(_heads_last(dq * dsilu(f["cq"], f["sq"]), nc))
        dck_ref[...] = _b(_heads_last(dk * dsilu(f["ck"], f["sk"]), nc))
        dcv_ref[...] = _b(_heads_last(dv * dsilu(f["cv"], f["sv"]), nc))

    r = lambda j: nb - 1 - j
    out_blk = pl.BlockSpec((blk, hb * DN_DK), lambda h, j: (r(j), h))
    grid = (DN_H // hb, nb)
    return pl.pallas_call(
        _carry(carry, body, 10, 5, grid), name=name, grid=grid,
        in_specs=_gdn_specs(blk, hb, r) + [pl.BlockSpec((hb, nc, DN_DK, DN_DV), lambda h, j: (h, r(j), 0, 0)),
                                          pl.BlockSpec((blk, hb * DN_DV), lambda h, j: (r(j), h))]
        + _gdn_solved_specs(blk, hb, r) + ex_in,
        out_specs=[out_blk, out_blk, out_blk, pl.BlockSpec((1, blk, SMALL), lambda h, j: (h, r(j), 0)),
                   pl.BlockSpec((hb, 8, 128), lambda h, j: (h, 0, 0))] + ex_out,
        out_shape=[jax.ShapeDtypeStruct((T, DN_H * DN_DK), BF16)] * 3 + [
            jax.ShapeDtypeStruct((DN_H // hb, T, SMALL), F32), jax.ShapeDtypeStruct((DN_H, 8, 128), F32)] + ex_shape,
        scratch_shapes=[pltpu.VMEM((hb, DN_DK, DN_DV), F32)] + ex_sems,
        compiler_params=_cp("arbitrary", "arbitrary"),
    )(conv, conv, conv, psmall, par, states, do, *solved, *ex_args)


def _head_norm(o, w, dv):
    outs, rs = [], []
    for i in range(o.shape[1] // dv):
        oh = o[:, i * dv:(i + 1) * dv]
        r = lax.rsqrt(jnp.mean(oh * oh, axis=-1, keepdims=True) + EPS)
        outs.append(oh * r)
        rs.append(r)
    return outs, rs


def _merge_specs(tm):
    col = lambda c: pl.BlockSpec((tm, D), lambda i: (i, c))
    return [col(0), col(0), col(2), col(6), col(7), col(8),
            pl.BlockSpec((1, GLA_DV), lambda i: (0, 0)), pl.BlockSpec((1, DN_DV), lambda i: (0, 0)),
            pl.BlockSpec((D, D), lambda i: (0, 0))]


def _merge_fwd(h, oa, ob, pbig, gla_hn, dn_hn, wout, name):
    T = h.shape[0]
    tm = min(ROW_BLK, T)

    def body(h_ref, oa_ref, ob_ref, gr_ref, dg_ref, ma_ref, mb_ref, wa_ref, wb_ref, wo_ref, ho_ref, y_ref):
        na, _ = _head_norm(oa_ref[...], wa_ref[...], GLA_DV)
        nbs, _ = _head_norm(ob_ref[...], wb_ref[...], DN_DV)
        hna = jnp.concatenate([t * wa_ref[...] for t in na], axis=1)
        hnb = jnp.concatenate([t * wb_ref[...] for t in nbs], axis=1)
        gr = gr_ref[...].astype(F32)
        dg = dg_ref[...].astype(F32)
        y = (_sigmoid(ma_ref[...].astype(F32)) * hna * (gr * _sigmoid(gr))
             + _sigmoid(mb_ref[...].astype(F32)) * hnb * (dg * _sigmoid(dg)))
        yb = _b(y)
        y_ref[...] = yb
        ho_ref[...] = h_ref[...] + _dot(yb, wo_ref[...])

    row = pl.BlockSpec((tm, D), lambda i: (i, 0))
    return pl.pallas_call(
        body, name=name, grid=(T // tm,),
        in_specs=[row] + _merge_specs(tm),
        out_specs=[row, row],
        out_shape=[jax.ShapeDtypeStruct((T, D), F32), jax.ShapeDtypeStruct((T, D), BF16)],
        compiler_params=_cp("arbitrary"),
    )(h, oa, ob, pbig, pbig, pbig, pbig, gla_hn, dn_hn, wout)


def _merge_bwd(dh, oa, ob, pbig, gla_hn, dn_hn, wout, name):
    T = dh.shape[0]
    tm = min(ROW_BLK, T)

    def branch(dy, o_ref, w_ref, gate_ref, m_ref, dv):
        w = w_ref[...]
        ohat, rs = _head_norm(o_ref[...], w, dv)
        gate = gate_ref[...].astype(F32)
        m = m_ref[...].astype(F32)
        sgate, sm = _sigmoid(gate), _sigmoid(m)
        silu = gate * sgate
        ohat_all = jnp.concatenate(ohat, axis=1)
        hn = jnp.concatenate([t * w for t in ohat], axis=1)
        d_on = dy * sm
        d_m = dy * hn * silu * sm * (1.0 - sm)
        d_hn = d_on * silu
        d_gate = d_on * hn * (sgate * (1.0 + gate * (1.0 - sgate)))
        dw = jnp.zeros((1, dv), F32)
        d_o = []
        for i, (oh, r) in enumerate(zip(ohat, rs)):
            dhn = d_hn[:, i * dv:(i + 1) * dv]
            dw += jnp.sum(dhn * oh, axis=0, keepdims=True)
            dohat = dhn * w
            d_o.append(r * (dohat - oh * jnp.mean(dohat * oh, axis=-1, keepdims=True)))
        return jnp.concatenate(d_o, axis=1), d_gate, d_m, dw

    def body(dh_ref, oa_ref, ob_ref, gr_ref, dg_ref, ma_ref, mb_ref, wa_ref, wb_ref, wo_ref,
             doa_ref, dob_ref, dgr_ref, ddg_ref, dma_ref, dmb_ref, dwa_ref, dwb_ref, dhb_ref):
        @pl.when(pl.program_id(0) == 0)
        def _():
            dwa_ref[...] = jnp.zeros_like(dwa_ref)
            dwb_ref[...] = jnp.zeros_like(dwb_ref)

        dhb = _b(dh_ref[...])
        dhb_ref[...] = dhb
        dy = _dot_nt(dhb, wo_ref[...])
        d_oa, d_gr, d_ma, dwa = branch(dy, oa_ref, wa_ref, gr_ref, ma_ref, GLA_DV)
        d_ob, d_dg, d_mb, dwb = branch(dy, ob_ref, wb_ref, dg_ref, mb_ref, DN_DV)
        doa_ref[...] = _b(d_oa)
        dob_ref[...] = _b(d_ob)
        dgr_ref[...] = _b(d_gr)
        ddg_ref[...] = _b(d_dg)
        dma_ref[...] = _b(d_ma)
        dmb_ref[...] = _b(d_mb)
        dwa_ref[...] += dwa
        dwb_ref[...] += dwb

    row = pl.BlockSpec((tm, D), lambda i: (i, 0))
    f32 = jax.ShapeDtypeStruct((T, D), F32)
    b16 = jax.ShapeDtypeStruct((T, D), BF16)
    return pl.pallas_call(
        body, name=name, grid=(T // tm,),
        in_specs=[row] + _merge_specs(tm),
        out_specs=[row] * 6 + [pl.BlockSpec((1, GLA_DV), lambda i: (0, 0)), pl.BlockSpec((1, DN_DV), lambda i: (0, 0)), row],
        out_shape=[b16, b16, b16, b16, b16, b16, jax.ShapeDtypeStruct((1, GLA_DV), F32),
                   jax.ShapeDtypeStruct((1, DN_DV), F32), b16],
        compiler_params=_cp("arbitrary"),
    )(dh, oa, ob, pbig, pbig, pbig, pbig, gla_hn, dn_hn, wout)


def _loss_head(h, nw, target, name):
    T = h.shape[0]
    tm = min(512, T)

    def body(h_ref, nw_ref, t_ref, dx_ref, loss_ref, dnw_ref):
        @pl.when(pl.program_id(0) == 0)
        def _():
            loss_ref[...] = jnp.zeros_like(loss_ref)
            dnw_ref[...] = jnp.zeros_like(dnw_ref)

        x = h_ref[...]
        w = nw_ref[...]
        r = lax.rsqrt(jnp.mean(x * x, axis=-1, keepdims=True) + EPS)
        xhat = x * r
        err = xhat * w - t_ref[...]
        part = jnp.sum(jnp.sum(err * err, axis=-1, keepdims=True), axis=0, keepdims=True)
        loss_ref[...] += (0.5 / D) * part
        dout = err * (1.0 / D)
        dnw_ref[...] += jnp.sum(dout * xhat, axis=0, keepdims=True)
        dxhat = dout * w
        dx_ref[...] = r * (dxhat - xhat * jnp.mean(dxhat * xhat, axis=-1, keepdims=True))

    row = pl.BlockSpec((tm, D), lambda i: (i, 0))
    one = pl.BlockSpec((1, D), lambda i: (0, 0))
    return pl.pallas_call(
        body, name=name, grid=(T // tm,),
        in_specs=[row, one, row],
        out_specs=[row, pl.BlockSpec((8, 128), lambda i: (0, 0)), one],
        out_shape=[jax.ShapeDtypeStruct((T, D), F32), jax.ShapeDtypeStruct((8, 128), F32),
                   jax.ShapeDtypeStruct((1, D), F32)],
        compiler_params=_cp("arbitrary"),
    )(h, nw, target)


def _adamw(w, g, m, v, rows, name):
    R, C = w.shape
    rows = min(rows, R)
    c1 = 1.0 - ADAM_B1 ** ADAM_STEP
    c2 = 1.0 - ADAM_B2 ** ADAM_STEP

    def body(w_ref, g_ref, m_ref, v_ref, d_ref, mo_ref, vo_ref):
        g_ = g_ref[...]
        m_ = ADAM_B1 * m_ref[...] + (1.0 - ADAM_B1) * g_
        v_ = ADAM_B2 * v_ref[...] + (1.0 - ADAM_B2) * (g_ * g_)
        mo_ref[...] = m_
        vo_ref[...] = v_
        d_ref[...] = -ADAM_LR * ((m_ / c1) / (jnp.sqrt(v_ / c2) + ADAM_EPS) + ADAM_WD * w_ref[...])

    blk = pl.BlockSpec((rows, C), lambda i: (i, 0))
    shp = jax.ShapeDtypeStruct((R, C), F32)
    return pl.pallas_call(
        body, name=name, grid=(R // rows,),
        in_specs=[blk] * 4, out_specs=[blk] * 3, out_shape=[shp] * 3,
        compiler_params=_cp("parallel"),
    )(w, g, m, v)


def _me():
    return lax.axis_index("x"), lax.axis_index("y"), lax.axis_index("c")


def _other_chips(x, y):
    return [(1 - x, y), (x, 1 - y), (1 - x, 1 - y)]


def _half_rows(ref, hf):
    half = ref.shape[-2] // 2
    rows = pl.ds(pl.multiple_of(hf * half, 16), half)
    return ref.at[rows, :] if len(ref.shape) == 2 else ref.at[:, rows, :]


class _GatherBig:
    def __init__(self, big):
        self.arrays = list(big)
        self.out_shape = [jax.ShapeDtypeStruct((N_SHARD,) + w.shape, w.dtype) for w in big]
        self.n_sem = 7 * len(big)

    @staticmethod
    def _copy(sems, k, src, dst, to):
        return pltpu.make_async_remote_copy(src_ref=src, dst_ref=dst, send_sem=sems[0].at[k], recv_sem=sems[1].at[k],
                                            device_id=to, device_id_type=MESH)

    def start(self, ins, outs, *sems):
        x, y, c = _me()
        mine = 2 * x + y
        for i, (w_ref, o_ref) in enumerate(zip(ins, outs)):
            self._copy(sems, 7 * i + 6, w_ref, o_ref.at[mine], (x, y, 1 - c)).start()
            for j, chip in enumerate(_other_chips(x, y)):
                self._copy(sems, 7 * i + j, _half_rows(w_ref, c), _half_rows(o_ref.at[mine], c), (*chip, c)).start()

    def finish(self, ins, outs, *sems):
        x, y, c = _me()
        me, sibling = (x, y, c), (x, y, 1 - c)
        chips = _other_chips(x, y)
        slot = lambda chip: 2 * chip[0] + chip[1]
        for i, o_ref in enumerate(outs):
            for j, chip in enumerate(chips):
                landed = _half_rows(o_ref.at[slot(chip)], c)
                self._copy(sems, 7 * i + j, landed, landed, me).wait_recv()
                self._copy(sems, 7 * i + 3 + j, landed, landed, sibling).start()
        for i, (w_ref, o_ref) in enumerate(zip(ins, outs)):
            for j, chip in enumerate(chips):
                passed = _half_rows(o_ref.at[slot(chip)], 1 - c)
                self._copy(sems, 7 * i + 3 + j, passed, passed, me).wait_recv()
            self._copy(sems, 7 * i + 6, o_ref.at[slot((x, y))], o_ref.at[slot((x, y))], me).wait_recv()
        for i, (w_ref, o_ref) in enumerate(zip(ins, outs)):
            self._copy(sems, 7 * i + 6, w_ref, o_ref.at[slot((x, y))], sibling).wait_send()
            for j, chip in enumerate(chips):
                self._copy(sems, 7 * i + j, _half_rows(w_ref, c), _half_rows(o_ref.at[slot((x, y))], c),
                           (*chip, c)).wait_send()
                landed = _half_rows(o_ref.at[slot(chip)], c)
                self._copy(sems, 7 * i + 3 + j, landed, landed, sibling).wait_send()


class _ChipsExchange:
    def __init__(self, pbs):
        self.arrays = list(pbs)
        self.out_shape = [jax.ShapeDtypeStruct((3,) + p.shape[1:], p.dtype) for p in pbs]
        self.n_sem = 3 * len(pbs)

    def _copies(self, ins, outs, send_sems, recv_sems):
        x, y, c = _me()
        return [pltpu.make_async_remote_copy(src_ref=ins[i].at[2 * chip[0] + chip[1]], dst_ref=outs[i].at[j],
                                             send_sem=send_sems.at[3 * i + j], recv_sem=recv_sems.at[3 * i + j],
                                             device_id=(*chip, c), device_id_type=MESH)
                for i in range(len(ins)) for j, chip in enumerate(_other_chips(x, y))]

    def start(self, ins, outs, *sems):
        for cp in self._copies(ins, outs, *sems):
            cp.start()

    def finish(self, ins, outs, *sems):
        for cp in self._copies(ins, outs, *sems):
            cp.wait()


def _carry(ex, body, n_in, n_out, grid):
    if ex is None:
        return body
    ni, no = len(ex.arrays), len(ex.out_shape)

    def carried(*refs):
        ins, ex_in = refs[:n_in], refs[n_in:n_in + ni]
        outs, ex_out = refs[n_in + ni:n_in + ni + n_out], refs[n_in + ni + n_out:n_in + ni + n_out + no]
        scratch, sems = refs[n_in + ni + n_out + no:-2], refs[-2:]
        ids = [pl.program_id(a) for a in range(len(grid))]
        first = functools.reduce(jnp.logical_and, [i == 0 for i in ids])
        last = functools.reduce(jnp.logical_and, [i == g - 1 for i, g in zip(ids, grid)])

        @pl.when(first)
        def _():
            ex.start(ex_in, ex_out, *sems)

        body(*ins, *outs, *scratch)

        @pl.when(last)
        def _():
            ex.finish(ex_in, ex_out, *sems)

    return carried


def _carry_specs(ex):
    if ex is None:
        return [], [], [], [], []
    sems = [pltpu.SemaphoreType.DMA((ex.n_sem,)), pltpu.SemaphoreType.DMA((ex.n_sem,))]
    return [ANY] * len(ex.arrays), ex.arrays, [ANY] * len(ex.out_shape), ex.out_shape, sems


def _gather_weights(big, small, name):
    nbig, nsm = len(big), len(small)
    n = nbig + nsm
    own_sem = 6 * nbig + 3 * nsm

    def body(*refs):
        ins, outs = refs[:n], refs[n:2 * n]
        send_sems, recv_sems = refs[2 * n:]
        x, y, c = _me()
        sibling = (x, y, 1 - c)
        chips = _other_chips(x, y)
        slot = lambda chip: 2 * chip[0] + chip[1]

        def copy(k, src, dst, to):
            return pltpu.make_async_remote_copy(src_ref=src, dst_ref=dst, send_sem=send_sems.at[k],
                                                recv_sem=recv_sems.at[k], device_id=to, device_id_type=MESH)

        sent = []
        for i in range(nbig):
            sent.append(copy(own_sem + i, ins[i], outs[i].at[slot((x, y))], sibling))
            sent[-1].start()
            for j, chip in enumerate(chips):
                sent.append(copy(6 * i + j, _half_rows(ins[i], c), _half_rows(outs[i].at[slot((x, y))], c), (*chip, c)))
                sent[-1].start()
        for t in range(nsm):
            w_ref, o_ref = ins[nbig + t], outs[nbig + t]
            o_ref[slot((x, y))] = w_ref[...]
            for j, chip in enumerate(chips):
                sent.append(copy(6 * nbig + 3 * t + j, w_ref, o_ref.at[slot((x, y))], (*chip, c)))
                sent[-1].start()
        for i in range(nbig):
            for j, chip in enumerate(chips):
                landed = _half_rows(outs[i].at[slot(chip)], c)
                copy(6 * i + j, landed, landed, (x, y, c)).wait_recv()
                sent.append(copy(6 * i + 3 + j, landed, landed, sibling))
                sent[-1].start()
        for t in range(nsm):
            for j, chip in enumerate(chips):
                landed = outs[nbig + t].at[slot(chip)]
                copy(6 * nbig + 3 * t + j, landed, landed, (x, y, c)).wait_recv()
        for i in range(nbig):
            for j, chip in enumerate(chips):
                passed = _half_rows(outs[i].at[slot(chip)], 1 - c)
                copy(6 * i + 3 + j, passed, passed, (x, y, c)).wait_recv()
        for i in range(nbig):
            mine = outs[i].at[slot((x, y))]
            copy(own_sem + i, mine, mine, (x, y, c)).wait_recv()
        for cp in sent:
            cp.wait_send()

    vm = pl.BlockSpec(memory_space=pltpu.VMEM)
    nsem = own_sem + nbig
    return pl.pallas_call(
        body, name=name, in_specs=[ANY] * nbig + [vm] * nsm, out_specs=[ANY] * nbig + [vm] * nsm,
        out_shape=[jax.ShapeDtypeStruct((N_SHARD,) + w.shape, w.dtype) for w in list(big) + list(small)],
        scratch_shapes=[pltpu.SemaphoreType.DMA((nsem,)), pltpu.SemaphoreType.DMA((nsem,))],
        compiler_params=pltpu.CompilerParams(has_side_effects=True),
    )(*big, *small)


def _rs_sibling(gs, name):
    n = len(gs)

    def body(*refs):
        send_sems, recv_sems = refs[2 * n:]
        x, y, c = _me()
        cps = [pltpu.make_async_remote_copy(src_ref=_half_rows(refs[i], 1 - c), dst_ref=refs[n + i],
                                            send_sem=send_sems.at[i], recv_sem=recv_sems.at[i],
                                            device_id=(x, y, 1 - c), device_id_type=MESH) for i in range(n)]
        for cp in cps:
            cp.start()
        for cp in cps:
            cp.wait()

    return pl.pallas_call(
        body, name=name, in_specs=[ANY] * n, out_specs=[ANY] * n,
        out_shape=[jax.ShapeDtypeStruct((g.shape[0], g.shape[1] // 2, g.shape[2]), g.dtype) for g in gs],
        scratch_shapes=[pltpu.SemaphoreType.DMA((n,)), pltpu.SemaphoreType.DMA((n,))],
        compiler_params=pltpu.CompilerParams(has_side_effects=True),
    )(*gs)


def _add_pair(g, other, where, name):
    ns, a, b = g.shape
    half = a // 2

    def body(w_ref, g_ref, o_ref, pb_ref, own_ref):
        t = g_ref[0].astype(F32) + o_ref[0].astype(F32)
        pb_ref[0] = _b(t)

        @pl.when(pl.program_id(0) == w_ref[1])
        def _():
            own_ref[...] = t

    return pl.pallas_call(
        body, name=name,
        grid_spec=pltpu.PrefetchScalarGridSpec(
            num_scalar_prefetch=1, grid=(ns,),
            in_specs=[pl.BlockSpec((1, half, b), lambda s, w: (s, w[0], 0)), pl.BlockSpec((1, half, b), lambda s, w: (s, 0, 0))],
            out_specs=[pl.BlockSpec((1, half, b), lambda s, w: (s, 0, 0)), pl.BlockSpec((half, b), lambda s, w: (0, 0))]),
        out_shape=[jax.ShapeDtypeStruct((ns, half, b), BF16), jax.ShapeDtypeStruct((half, b), F32)],
        compiler_params=_cp("arbitrary"),
    )(where, g, other)


def _add_four(own, got, name):
    rows, cols = own.shape
    rb = rows // 2

    def body(a_ref, b_ref, o_ref):
        o_ref[...] = ((a_ref[...] + b_ref[0].astype(F32)) + b_ref[1].astype(F32)) + b_ref[2].astype(F32)

    return pl.pallas_call(
        body, name=name, grid=(rows // rb,),
        in_specs=[pl.BlockSpec((rb, cols), lambda i: (i, 0)), pl.BlockSpec((3, rb, cols), lambda i: (0, i, 0))],
        out_specs=pl.BlockSpec((rb, cols), lambda i: (i, 0)),
        out_shape=jax.ShapeDtypeStruct((rows, cols), F32),
        compiler_params=_cp("parallel"),
    )(own, got)


def _rs_swap(halves, name):
    n = len(halves)

    def body(*refs):
        send_sems, recv_sems = refs[2 * n:]
        x, y, c = _me()
        cps = [pltpu.make_async_remote_copy(src_ref=refs[i], dst_ref=refs[n + i], send_sem=send_sems.at[i],
                                            recv_sem=recv_sems.at[i], device_id=(x, y, 1 - c), device_id_type=MESH)
               for i in range(n)]
        for cp in cps:
            cp.start()
        for cp in cps:
            cp.wait()

    return pl.pallas_call(
        body, name=name, in_specs=[ANY] * n, out_specs=[ANY] * n,
        out_shape=[jax.ShapeDtypeStruct(h.shape, h.dtype) for h in halves],
        scratch_shapes=[pltpu.SemaphoreType.DMA((n,)), pltpu.SemaphoreType.DMA((n,))],
        compiler_params=pltpu.CompilerParams(has_side_effects=True),
    )(*halves)


def _adamw_halves(w, own, got, m, v, rows, name):
    a, b = w.shape
    nblk = a // 2 // rows
    c1 = 1.0 - ADAM_B1 ** ADAM_STEP
    c2 = 1.0 - ADAM_B2 ** ADAM_STEP

    def body(w_ref, own_ref, got_ref, m_ref, v_ref, g_ref, d_ref, mo_ref, vo_ref):
        g_ = jnp.where(pl.program_id(0) == lax.axis_index("c"), own_ref[...], got_ref[...])
        g_ref[...] = g_
        m_ = ADAM_B1 * m_ref[...] + (1.0 - ADAM_B1) * g_
        v_ = ADAM_B2 * v_ref[...] + (1.0 - ADAM_B2) * (g_ * g_)
        mo_ref[...] = m_
        vo_ref[...] = v_
        d_ref[...] = -ADAM_LR * ((m_ / c1) / (jnp.sqrt(v_ / c2) + ADAM_EPS) + ADAM_WD * w_ref[...])

    whole = pl.BlockSpec((rows, b), lambda h, i: (h * nblk + i, 0))
    part = pl.BlockSpec((rows, b), lambda h, i: (i, 0))
    shp = jax.ShapeDtypeStruct((a, b), F32)
    return pl.pallas_call(
        body, name=name, grid=(2, nblk),
        in_specs=[whole, part, part, whole, whole], out_specs=[whole] * 4, out_shape=[shp] * 4,
        compiler_params=_cp("parallel", "parallel"),
    )(w, own, got, m, v)


def _allsum_small(vec, name):
    def body(v_ref, o_ref, buf_ref, send_sems, recv_sems):
        x, y, c = _me()
        me = 4 * x + 2 * y + c
        buf_ref[me] = v_ref[...]
        cps = []
        for k in range(1, 8):
            peer = (x ^ (k >> 2), y ^ ((k >> 1) & 1), c ^ (k & 1))
            cps.append(pltpu.make_async_remote_copy(src_ref=v_ref, dst_ref=buf_ref.at[me],
                                                    send_sem=send_sems.at[k - 1], recv_sem=recv_sems.at[k - 1],
                                                    device_id=peer, device_id_type=MESH))
        for cp in cps:
            cp.start()
        for k in range(1, 8):
            peer_idx = me ^ k
            pltpu.make_async_remote_copy(src_ref=v_ref, dst_ref=buf_ref.at[peer_idx],
                                         send_sem=send_sems.at[k - 1], recv_sem=recv_sems.at[k - 1],
                                         device_id=(x, y, c), device_id_type=MESH).wait_recv()
        for cp in cps:
            cp.wait_send()
        acc = buf_ref[0]
        for d in range(1, 8):
            acc = acc + buf_ref[d]
        o_ref[...] = acc

    return pl.pallas_call(
        body, name=name,
        in_specs=[pl.BlockSpec(memory_space=pltpu.VMEM)], out_specs=pl.BlockSpec(memory_space=pltpu.VMEM),
        out_shape=jax.ShapeDtypeStruct(vec.shape, F32),
        scratch_shapes=[pltpu.VMEM((8,) + vec.shape, F32), pltpu.SemaphoreType.DMA((7,)), pltpu.SemaphoreType.DMA((7,))],
        compiler_params=pltpu.CompilerParams(has_side_effects=True),
    )(vec)


BIG = ("ffn1_w_gate", "ffn1_w_up", "ffn1_w_down", "w_in", "w_out", "ffn2_w_gate", "ffn2_w_up", "ffn2_w_down")
TINY = ("w_gla_gate", "conv_w")
SHARDED = BIG + TINY


def _join_cols(w4):
    return jnp.transpose(w4, (1, 0, 2)).reshape(w4.shape[1], N_SHARD * w4.shape[2])


def _cut_cols(w):
    return jnp.transpose(w.reshape(w.shape[0], N_SHARD, w.shape[1] // N_SHARD), (1, 0, 2))


def _split_w_in(w):
    o = IN_OFF
    big = jnp.concatenate([w[:, :o[4]], w[:, o[5]:o[9]], w[:, o[11]:]], axis=1)
    small = jnp.concatenate([w[:, o[4]:o[5]], w[:, o[9]:o[11]], jnp.zeros((w.shape[0], SMALL - 32), w.dtype)], axis=1)
    return big, small


def _merge_w_in(big, small):
    return jnp.concatenate([big[:, :3072], small[:, :16], big[:, 3072:7168], small[:, 16:32], big[:, 7168:]], axis=1)


class _Comm:
    def __init__(self, where, rest_shards):
        self.where = where
        self.rest = _GatherBig(rest_shards)
        self.pairs, self.got = {}, {}

    @staticmethod
    def weights(gathered):
        G = dict(zip(BIG[3:], gathered))
        W = {n: G[n] for n in BIG[3:] if n.startswith("ffn")}
        W["w_out"] = G["w_out"].reshape(D, D)
        W["w_in_big"], W["w_in_small"] = _split_w_in(_join_cols(G["w_in"]))
        return W

    def begin(self, names, grads):
        from_sibling = _rs_sibling(grads, "rs_sibling_" + names[0])
        for n, g, o in zip(names, grads, from_sibling):
            self.pairs[n] = _add_pair(g, o, self.where, "rs_pair_" + n)
        return _ChipsExchange([self.pairs[n][0] for n in names])

    def landed(self, names, outs):
        self.got.update(zip(names, outs))


def _local_step(x, target, W, P, comm=None):
    wgate_pad = jnp.zeros((SMALL, GLA_H * GLA_DK), F32).at[:GLA_RANK].set(P["w_gla_gate"])
    cw8 = jnp.zeros((8, CONV_C), F32).at[:CONV_K].set(P["conv_w"])
    par = jnp.zeros((DN_H, 8, 128), F32)
    par = par.at[:, 0, :].set(jnp.broadcast_to(P["dn_a_log"].reshape(DN_H, 1), (DN_H, 128)))
    par = par.at[:, 1, :].set(jnp.broadcast_to(P["dn_dt_bias"].reshape(DN_H, 1), (DN_H, 128)))

    h1, n1, g1, u1, *rest = _ffn_fwd(x, P["ffn1_norm"], W["ffn1_w_gate"], W["ffn1_w_up"], W["ffn1_w_down"], "ffn1_fwd",
                                     carry=comm.rest if comm else None)
    if comm:
        W = dict(W, **comm.weights(rest))
    wbig, wsmall = W["w_in_big"], W["w_in_small"]
    pbig, psmall, n2 = _norm_proj(h1, P["mix_norm"], wbig, wsmall, "mix_proj")
    oa, sa = _gla_fwd(pbig, psmall, wgate_pad, P["b_gla_gate"], "gla_fwd")
    conv = _conv_fwd(pbig, cw8, "conv_fwd")
    ob, sb, *solved = _gdn_fwd(conv, psmall, par, "gdn_fwd")
    h2, yb = _merge_fwd(h1, oa, ob, pbig, P["gla_head_norm"], P["dn_head_norm"], W["w_out"], "merge_fwd")
    h3, n3, g3, u3 = _ffn_fwd(h2, P["ffn2_norm"], W["ffn2_w_gate"], W["ffn2_w_up"], W["ffn2_w_down"], "ffn2_fwd")
    dh3, loss, d_final = _loss_head(h3, P["final_norm"], target, "loss_head")

    gw, gs = {}, {"final_norm": d_final}

    def ffn_grads(tag, dh, h, n, g, u, before=None):
        names = tuple(tag + s for s in ("_w_gate", "_w_up", "_w_down"))
        dg, du, act, dfb, *landed = _ffn_bwd_hidden(dh, g, u, W[names[2]], tag + "_bwd_hidden", carry=before)
        gw[names[0]] = _mm_tn(n, dg, D, FF_CUT, tag + "_dwg")
        gw[names[1]] = _mm_tn(n, du, D, FF_CUT, tag + "_dwu")
        gw[names[2]] = _mm_tn(act, dfb, FF_CUT, D, tag + "_dwd")
        dx, gs[tag + "_norm"], *own = _ffn_bwd_input(dh, h, P[tag + "_norm"], dg, du, W[names[0]], W[names[1]],
                                                     tag + "_bwd_input",
                                                     carry=comm.begin(names, [gw[n] for n in names]) if comm else None)
        if comm:
            comm.landed(names, own)
        return dx, landed

    dh2, _ = ffn_grads("ffn2", dh3, h2, n3, g3, u3)
    d_oa, d_ob, d_gr, d_dgate, d_ma, d_mb, gs["gla_head_norm"], gs["dn_head_norm"], dh2b = _merge_bwd(
        dh2, oa, ob, pbig, P["gla_head_norm"], P["dn_head_norm"], W["w_out"], "merge_bwd")
    gw["w_out"] = _mm_tn(yb, dh2b, D, D, "dw_out").reshape(N_SHARD, D // N_SHARD, D)
    early = ("w_out",)
    d_gq, d_gk, d_gv, dpre = _gla_bwd(pbig, psmall, wgate_pad, P["b_gla_gate"], sa, d_oa, "gla_bwd")
    dcq, dck, dcv, dsm, dpar, *landed = _gdn_bwd(conv, psmall, par, sb, d_ob, solved, "gdn_bwd",
                                                 carry=comm.begin(early, [gw[n] for n in early]) if comm else None)
    if comm:
        comm.landed(early, landed)
    dsmall, dwgate, gs["b_gla_gate"] = _gla_gate_bwd(dpre, psmall, wgate_pad, dsm, "gla_gate_bwd")
    gs["w_gla_gate"] = dwgate[:GLA_RANK]
    d_x3, dcw = _conv_bwd(dcq, dck, dcv, pbig, cw8, "conv_bwd")
    gs["conv_w"] = dcw[:CONV_K]
    gs["dn_a_log"] = dpar[:, 0, 0].reshape(1, DN_H)
    gs["dn_dt_bias"] = dpar[:, 0, 1].reshape(1, DN_H)
    pieces = (d_gq, d_gk, d_gv, d_gr, d_x3, d_dgate, d_ma, d_mb)
    dh1, gs["mix_norm"] = _proj_bwd(dh2, h1, P["mix_norm"], pieces, dsmall, wbig, wsmall, "proj_bwd")
    dbig = jnp.concatenate([_mm_tn(n2, p, D, 1024, "dw_in_%d" % i) for i, p in enumerate(pieces)], axis=1)
    dsml = _mm_tn(n2, dsmall, D, SMALL, "dw_in_small")
    gw["w_in"] = _cut_cols(_merge_w_in(dbig, dsml))
    grad_x, landed = ffn_grads("ffn1", dh1, x, n1, g1, u1,
                               before=comm.begin(("w_in",), [gw["w_in"]]) if comm else None)
    if comm:
        comm.landed(("w_in",), landed)
    return loss, grad_x, gw, gs


SMALL_NAMES = ("ffn1_norm", "mix_norm", "ffn2_norm", "final_norm", "b_gla_gate", "gla_head_norm", "dn_head_norm",
               "dn_a_log", "dn_dt_bias")
ROW4 = (("b_gla_gate", 512), ("gla_head_norm", 256), ("dn_head_norm", 128), ("dn_a_log", 8), ("dn_dt_bias", 8))


def _pack_small(d, loss=None):
    row4 = [d[n].reshape(-1) for n, _ in ROW4]
    row4.append(jnp.zeros((1,), F32) if loss is None else loss.reshape(1))
    row4 = jnp.concatenate(row4)
    row4 = jnp.pad(row4, (0, D - row4.shape[0]))
    rows = [d[n].reshape(-1) for n in SMALL_NAMES[:4]] + [row4]
    return jnp.concatenate([jnp.stack(rows), jnp.zeros((3, D), F32)], axis=0)


def _unpack_small(a, like):
    out = {n: a[i].reshape(like[n].shape) for i, n in enumerate(SMALL_NAMES[:4])}
    off = 0
    for n, w in ROW4:
        out[n] = a[4, off:off + w].reshape(like[n].shape)
        off += w
    return out, a[4, off]


WEIGHT_ORDER = ("ffn1_norm", "ffn1_w_gate", "ffn1_w_up", "ffn1_w_down", "mix_norm", "w_in", "w_gla_gate", "b_gla_gate",
                "conv_w", "dn_a_log", "dn_dt_bias", "gla_head_norm", "dn_head_norm", "w_out", "ffn2_norm",
                "ffn2_w_gate", "ffn2_w_up", "ffn2_w_down", "final_norm")
ADAM_ROWS = {"ffn1_w_gate": 256, "ffn1_w_up": 256, "ffn1_w_down": 176, "w_in": 128, "w_gla_gate": 16, "conv_w": 4,
             "w_out": 64, "ffn2_w_gate": 256, "ffn2_w_up": 256, "ffn2_w_down": 176}


def kernel(x, ffn1_norm, ffn1_w_gate, ffn1_w_up, ffn1_w_down, mix_norm, w_in, w_gla_gate, b_gla_gate, conv_w, dn_a_log, dn_dt_bias, gla_head_norm, dn_head_norm, w_out, ffn2_norm, ffn2_w_gate, ffn2_w_up, ffn2_w_down, final_norm, loss_target, m_ffn1_norm, m_ffn1_w_gate, m_ffn1_w_up, m_ffn1_w_down, m_mix_norm, m_w_in, m_w_gla_gate, m_b_gla_gate, m_conv_w, m_dn_a_log, m_dn_dt_bias, m_gla_head_norm, m_dn_head_norm, m_w_out, m_ffn2_norm, m_ffn2_w_gate, m_ffn2_w_up, m_ffn2_w_down, m_final_norm, v_ffn1_norm, v_ffn1_w_gate, v_ffn1_w_up, v_ffn1_w_down, v_mix_norm, v_w_in, v_w_gla_gate, v_b_gla_gate, v_conv_w, v_dn_a_log, v_dn_dt_bias, v_gla_head_norm, v_dn_head_norm, v_w_out, v_ffn2_norm, v_ffn2_w_gate, v_ffn2_w_up, v_ffn2_w_down, v_final_norm):
    given = dict(locals())
    wts = {n: given[n] for n in WEIGHT_ORDER}
    moms = {n: given["m_" + n] for n in WEIGHT_ORDER}
    vars_ = {n: given["v_" + n] for n in WEIGHT_ORDER}
    two_d = lambda a: a.reshape(a.shape[-2], a.shape[-1]) if a.ndim == 3 else a.reshape(1, -1)
    shard = {n: two_d(wts[n]) for n in SHARDED}

    gathered = _gather_weights([shard[n].astype(BF16) for n in BIG[:3]], [shard[n] for n in TINY], "gather_first")
    W = dict(zip(BIG[:3], gathered))
    P = {n: two_d(wts[n]) for n in SMALL_NAMES}
    for n, g in zip(TINY, gathered[3:]):
        P[n] = _join_cols(g)

    my_slot = 2 * lax.axis_index("x") + lax.axis_index("y")
    where = jnp.stack([lax.axis_index("c"), my_slot]).astype(jnp.int32)
    comm = _Comm(where, [shard[n].astype(BF16) for n in BIG[3:]])
    loss, grad_x, gw, gs = _local_step(x[0], loss_target[0], W, P, comm)
    halves = [_add_four(comm.pairs[n][1], comm.got[n], "rs_four_" + n) for n in BIG]
    other_halves = _rs_swap(halves, "rs_swap")

    tiny_rows = jnp.concatenate([gs["w_gla_gate"].reshape(8, D), gs["conv_w"].reshape(12, D), jnp.zeros((4, D), F32)])
    all_sum = _allsum_small(jnp.concatenate([_pack_small(gs, loss[0, 0]), tiny_rows]), "allsum_small")
    small_sum = all_sum[:8]
    small_g, loss_total = _unpack_small(small_sum, P)

    grads, delta, new_m, new_v = {}, {}, {}, {}
    for n, own, got in zip(BIG, halves, other_halves):
        res = _adamw_halves(shard[n], own, got, two_d(moms[n]), two_d(vars_[n]), ADAM_ROWS[n], "adamw_" + n)
        grads[n], delta[n], new_m[n], new_v[n] = (t.reshape(wts[n].shape) for t in res)
    for n, rows in (("w_gla_gate", all_sum[8:16]), ("conv_w", all_sum[16:28])):
        cols = shard[n].shape[1]
        grads[n] = lax.dynamic_slice_in_dim(rows.reshape(shard[n].shape[0], N_SHARD * cols), my_slot * cols, cols, axis=1)
        d, m_, v_ = _adamw(shard[n], grads[n], two_d(moms[n]), two_d(vars_[n]), ADAM_ROWS[n], "adamw_" + n)
        delta[n], new_m[n], new_v[n] = (t.reshape(wts[n].shape) for t in (d, m_, v_))
    pk = lambda src: _pack_small({n: two_d(src[n]) for n in SMALL_NAMES})
    sd, sm_, sv_ = _adamw(pk(wts), small_sum, pk(moms), pk(vars_), 8, "adamw_small")
    for res, dst in ((sd, delta), (sm_, new_m), (sv_, new_v)):
        u, _ = _unpack_small(res, wts)
        dst.update(u)
    grad_w = {n: grads[n].reshape(wts[n].shape) for n in SHARDED}
    grad_w.update({n: small_g[n].reshape(wts[n].shape) for n in SMALL_NAMES})
    return (loss_total, grad_x[None], *[grad_w[n] for n in WEIGHT_ORDER], *[delta[n] for n in WEIGHT_ORDER],
            *[new_m[n] for n in WEIGHT_ORDER], *[new_v[n] for n in WEIGHT_ORDER])
```

```python
import functools
import math

import numpy as np
import jax
import jax.numpy as jnp
from jax import lax
from jax.experimental import pallas as pl
from jax.experimental.pallas import tpu as pltpu

F32 = jnp.float32
BF16 = jnp.bfloat16
HI = lax.Precision.HIGH
MESH = pl.DeviceIdType.MESH
ANY = pl.BlockSpec(memory_space=pl.ANY)

EPS = 1e-6
D = 1024
DFF = 2816
FFN_RES = 0.5
GLA_H, GLA_DK, GLA_DV, GLA_RANK, GLA_TAU = 4, 128, 256, 16, 16.0
DN_H, DN_DK, DN_DV = 8, 128, 128
CONV_K = 4
CHUNK = 64
N_SHARD = 4
FF_CUT = DFF // N_SHARD
ADAM_LR, ADAM_B1, ADAM_B2, ADAM_EPS, ADAM_WD, ADAM_STEP = 0.001, 0.9, 0.999, 1e-08, 0.01, 10

IN_SIZES = (512, 512, 1024, 1024, 16, 1024, 1024, 1024, 1024, 8, 8, 1024, 1024)
IN_OFF = tuple(int(v) for v in np.cumsum((0,) + IN_SIZES))
D_IN = IN_OFF[-1]
BIG_COLS = 9216
SMALL = 128
PIECES = (512, 512, 1024, 1024, 3072, 1024, 1024, 1024)

VMEM_LIMIT = 56 * 1024 * 1024
ROW_BLK = 256
BIG_ROW_BLK = 512
ATT_BLK = 256
GDN_BLK = 128
GDN_HEADS = 8


def _cp(*sem):
    return pltpu.CompilerParams(dimension_semantics=sem, vmem_limit_bytes=VMEM_LIMIT)


def _sigmoid(x):
    return 1.0 / (1.0 + jnp.exp(-x))


def _softplus(x):
    return jnp.maximum(x, 0.0) + jnp.log(1.0 + jnp.exp(-jnp.abs(x)))


def _log_sigmoid(x):
    return jnp.minimum(x, 0.0) - jnp.log(1.0 + jnp.exp(-jnp.abs(x)))


def _dot(a, b, prec=None):
    return jnp.dot(a, b, preferred_element_type=F32, precision=prec)


def _dot_nt(a, b, prec=None):
    return lax.dot_general(a, b, (((1,), (1,)), ((), ())), preferred_element_type=F32, precision=prec)


def _dot_tn(a, b, prec=None):
    return lax.dot_general(a, b, (((0,), (0,)), ((), ())), preferred_element_type=F32, precision=prec)


def _b(x):
    return x.astype(BF16)


def _iota2(n, m, axis):
    return lax.broadcasted_iota(jnp.int32, (n, m), axis)


def _load_weights(pairs, sem):
    copies = [pltpu.make_async_copy(s, d, sem.at[i]) for i, (s, d) in enumerate(pairs)]
    for c in copies:
        c.start()
    for c in copies:
        c.wait()


def _ffn_fwd(h, nw, wg, wu, wd, name, carry=None):
    T = h.shape[0]
    tm = min(BIG_ROW_BLK, T)
    ex_in, ex_args, ex_out, ex_shape, ex_sems = _carry_specs(carry)

    def body(h_ref, nw_ref, wg_hbm, wu_hbm, wd_hbm, ho_ref, n_ref, g_ref, u_ref, wg_v, wu_v, wd_v, sem):
        @pl.when(pl.program_id(0) == 0)
        def _():
            _load_weights(((wg_hbm, wg_v), (wu_hbm, wu_v), (wd_hbm, wd_v)), sem)

        x = h_ref[...]
        r = lax.rsqrt(jnp.mean(x * x, axis=-1, keepdims=True) + EPS)
        nb = _b((x * r) * nw_ref[...])
        n_ref[...] = nb
        acc = jnp.zeros((tm, D), F32)
        for s in range(N_SHARD):
            g = _dot(nb, wg_v[s])
            u = _dot(nb, wu_v[s])
            g_ref[s] = _b(g)
            u_ref[s] = _b(u)
            acc += _dot(_b(g * _sigmoid(g) * u), wd_v[s])
        ho_ref[...] = x + FFN_RES * acc

    row = lambda w: pl.BlockSpec((tm, w), lambda i: (i, 0))
    cut = pl.BlockSpec((N_SHARD, tm, FF_CUT), lambda i: (0, i, 0))
    return pl.pallas_call(
        _carry(carry, body, 5, 4, (T // tm,)), name=name, grid=(T // tm,),
        in_specs=[row(D), pl.BlockSpec((1, D), lambda i: (0, 0)), ANY, ANY, ANY] + ex_in,
        out_specs=[row(D), row(D), cut, cut] + ex_out,
        out_shape=[jax.ShapeDtypeStruct((T, D), F32), jax.ShapeDtypeStruct((T, D), BF16),
                   jax.ShapeDtypeStruct((N_SHARD, T, FF_CUT), BF16),
                   jax.ShapeDtypeStruct((N_SHARD, T, FF_CUT), BF16)] + ex_shape,
        scratch_shapes=[pltpu.VMEM((N_SHARD, D, FF_CUT), BF16), pltpu.VMEM((N_SHARD, D, FF_CUT), BF16),
                        pltpu.VMEM((N_SHARD, FF_CUT, D), BF16), pltpu.SemaphoreType.DMA((3,))] + ex_sems,
        compiler_params=_cp("arbitrary"),
    )(h, nw, wg, wu, wd, *ex_args)


def _ffn_bwd_hidden(dh, g, u, wd, name, carry=None):
    T = dh.shape[0]
    tm = min(BIG_ROW_BLK, T)
    ex_in, ex_args, ex_out, ex_shape, ex_sems = _carry_specs(carry)

    def body(dh_ref, g_ref, u_ref, wd_hbm, dg_ref, du_ref, a_ref, df_ref, wd_v, sem):
        @pl.when(pl.program_id(0) == 0)
        def _():
            _load_weights(((wd_hbm, wd_v),), sem)

        dfb = _b(FFN_RES * dh_ref[...])
        df_ref[...] = dfb
        for s in range(N_SHARD):
            da = _dot_nt(dfb, wd_v[s])
            gg = g_ref[s].astype(F32)
            uu = u_ref[s].astype(F32)
            sg = _sigmoid(gg)
            silu = gg * sg
            a_ref[s] = _b(silu * uu)
            dg_ref[s] = _b(da * uu * (sg * (1.0 + gg * (1.0 - sg))))
            du_ref[s] = _b(da * silu)

    row = pl.BlockSpec((tm, D), lambda i: (i, 0))
    cut = pl.BlockSpec((N_SHARD, tm, FF_CUT), lambda i: (0, i, 0))
    cut_shape = jax.ShapeDtypeStruct((N_SHARD, T, FF_CUT), BF16)
    return pl.pallas_call(
        _carry(carry, body, 4, 4, (T // tm,)), name=name, grid=(T // tm,),
        in_specs=[row, cut, cut, ANY] + ex_in,
        out_specs=[cut, cut, cut, row] + ex_out,
        out_shape=[cut_shape, cut_shape, cut_shape, jax.ShapeDtypeStruct((T, D), BF16)] + ex_shape,
        scratch_shapes=[pltpu.VMEM((N_SHARD, FF_CUT, D), BF16), pltpu.SemaphoreType.DMA((1,))] + ex_sems,
        compiler_params=_cp("arbitrary"),
    )(dh, g, u, wd, *ex_args)


def _ffn_bwd_input(dh, h, nw, dg, du, wg, wu, name, carry=None):
    T = h.shape[0]
    tm = min(BIG_ROW_BLK, T)
    ex_in, ex_args, ex_out, ex_shape, ex_sems = _carry_specs(carry)

    def body(dh_ref, h_ref, nw_ref, dg_ref, du_ref, wg_hbm, wu_hbm, dx_ref, dnw_ref, wg_v, wu_v, sem):
        @pl.when(pl.program_id(0) == 0)
        def _():
            _load_weights(((wg_hbm, wg_v), (wu_hbm, wu_v)), sem)
            dnw_ref[...] = jnp.zeros_like(dnw_ref)

        dn = jnp.zeros((tm, D), F32)
        for s in range(N_SHARD):
            dn += _dot_nt(dg_ref[s], wg_v[s]) + _dot_nt(du_ref[s], wu_v[s])
        x = h_ref[...]
        r = lax.rsqrt(jnp.mean(x * x, axis=-1, keepdims=True) + EPS)
        xhat = x * r
        dnw_ref[...] += jnp.sum(dn * xhat, axis=0, keepdims=True)
        dxhat = dn * nw_ref[...]
        dx_ref[...] = dh_ref[...] + r * (dxhat - xhat * jnp.mean(dxhat * xhat, axis=-1, keepdims=True))

    row = pl.BlockSpec((tm, D), lambda i: (i, 0))
    one = pl.BlockSpec((1, D), lambda i: (0, 0))
    cut = pl.BlockSpec((N_SHARD, tm, FF_CUT), lambda i: (0, i, 0))
    return pl.pallas_call(
        _carry(carry, body, 7, 2, (T // tm,)), name=name, grid=(T // tm,),
        in_specs=[row, row, one, cut, cut, ANY, ANY] + ex_in,
        out_specs=[row, one] + ex_out,
        out_shape=[jax.ShapeDtypeStruct((T, D), F32), jax.ShapeDtypeStruct((1, D), F32)] + ex_shape,
        scratch_shapes=[pltpu.VMEM((N_SHARD, D, FF_CUT), BF16), pltpu.VMEM((N_SHARD, D, FF_CUT), BF16),
                        pltpu.SemaphoreType.DMA((2,))] + ex_sems,
        compiler_params=_cp("arbitrary"),
    )(dh, h, nw, dg, du, wg, wu, *ex_args)


def _mm_tn(a, b, bm, bn, name, out_dtype=BF16, tk=2048):
    cuts = a.shape[0] if a.ndim == 3 else (b.shape[0] if b.ndim == 3 else None)
    T, M = a.shape[-2:]
    N = b.shape[-1]
    tk = min(tk, T)
    bm, bn = min(bm, M), min(bn, N)
    nk = T // tk

    def body(a_ref, b_ref, o_ref, acc_ref):
        k = pl.program_id(3)

        @pl.when(k == 0)
        def _():
            acc_ref[...] = jnp.zeros_like(acc_ref)

        av = a_ref[0] if a.ndim == 3 else a_ref[...]
        bv = b_ref[0] if b.ndim == 3 else b_ref[...]
        acc_ref[...] += _dot_tn(_b(av), _b(bv))

        @pl.when(k == nk - 1)
        def _():
            res = acc_ref[...].astype(out_dtype)
            if cuts is None:
                o_ref[...] = res
            else:
                o_ref[0] = res

    a_spec = (pl.BlockSpec((1, tk, bm), lambda s, i, j, k: (s, k, i)) if a.ndim == 3
              else pl.BlockSpec((tk, bm), lambda s, i, j, k: (k, i)))
    b_spec = (pl.BlockSpec((1, tk, bn), lambda s, i, j, k: (s, k, j)) if b.ndim == 3
              else pl.BlockSpec((tk, bn), lambda s, i, j, k: (k, j)))
    if cuts is None:
        o_spec, o_shape = pl.BlockSpec((bm, bn), lambda s, i, j, k: (i, j)), (M, N)
    else:
        o_spec, o_shape = pl.BlockSpec((1, bm, bn), lambda s, i, j, k: (s, i, j)), (cuts, M, N)
    return pl.pallas_call(
        body, name=name, grid=(cuts or 1, M // bm, N // bn, nk),
        in_specs=[a_spec, b_spec], out_specs=o_spec,
        out_shape=jax.ShapeDtypeStruct(o_shape, out_dtype),
        scratch_shapes=[pltpu.VMEM((bm, bn), F32)],
        compiler_params=_cp("parallel", "parallel", "parallel", "arbitrary"),
    )(a, b)


def _norm_proj(h, nw, wbig, wsmall, name, carry=None):
    T = h.shape[0]
    tm = min(512, T)
    tn = 1536
    ex_in, ex_args, ex_out, ex_shape, ex_sems = _carry_specs(carry)

    def body(h_ref, nw_ref, wb_hbm, ws_ref, pb_ref, ps_ref, n_ref, wb_v, sem):
        @pl.when(pl.program_id(0) == 0)
        def _():
            _load_weights(((wb_hbm, wb_v),), sem)

        x = h_ref[...]
        r = lax.rsqrt(jnp.mean(x * x, axis=-1, keepdims=True) + EPS)
        nb = _b((x * r) * nw_ref[...])
        n_ref[...] = nb
        ps_ref[...] = _dot(nb, ws_ref[...])
        for j in range(BIG_COLS // tn):
            pb_ref[:, j * tn:(j + 1) * tn] = _b(_dot(nb, wb_v[:, j * tn:(j + 1) * tn]))

    row = lambda w: pl.BlockSpec((tm, w), lambda i: (i, 0))
    return pl.pallas_call(
        _carry(carry, body, 4, 3, (T // tm,)), name=name, grid=(T // tm,),
        in_specs=[row(D), pl.BlockSpec((1, D), lambda i: (0, 0)), ANY, pl.BlockSpec((D, SMALL), lambda i: (0, 0))]
        + ex_in,
        out_specs=[row(BIG_COLS), row(SMALL), row(D)] + ex_out,
        out_shape=[jax.ShapeDtypeStruct((T, BIG_COLS), BF16), jax.ShapeDtypeStruct((T, SMALL), F32),
                   jax.ShapeDtypeStruct((T, D), BF16)] + ex_shape,
        scratch_shapes=[pltpu.VMEM((D, BIG_COLS), BF16), pltpu.SemaphoreType.DMA((1,))] + ex_sems,
        compiler_params=_cp("arbitrary"),
    )(h, nw, wbig, wsmall, *ex_args)


def _proj_bwd(dh, h, nw, pieces, dsmall, wbig, wsmall, name):
    T = h.shape[0]
    tm = min(BIG_ROW_BLK, T)
    offs = tuple(int(v) for v in np.cumsum((0,) + PIECES))

    def body(dh_ref, h_ref, nw_ref, *rest):
        p_refs = rest[:len(PIECES)]
        ds_ref, wb_hbm, ws_ref, dx_ref, dnw_ref, wb_v, sem = rest[len(PIECES):]

        @pl.when(pl.program_id(0) == 0)
        def _():
            _load_weights(((wb_hbm, wb_v),), sem)
            dnw_ref[...] = jnp.zeros_like(dnw_ref)

        dn = _dot_nt(_b(ds_ref[...]), ws_ref[...])
        for p_ref, lo, wdt in zip(p_refs, offs, PIECES):
            dn += _dot_nt(p_ref[...], wb_v[:, lo:lo + wdt])
        x = h_ref[...]
        r = lax.rsqrt(jnp.mean(x * x, axis=-1, keepdims=True) + EPS)
        xhat = x * r
        dnw_ref[...] += jnp.sum(dn * xhat, axis=0, keepdims=True)
        dxhat = dn * nw_ref[...]
        dx_ref[...] = dh_ref[...] + r * (dxhat - xhat * jnp.mean(dxhat * xhat, axis=-1, keepdims=True))

    row = lambda w: pl.BlockSpec((tm, w), lambda i: (i, 0))
    one = pl.BlockSpec((1, D), lambda i: (0, 0))
    return pl.pallas_call(
        body, name=name, grid=(T // tm,),
        in_specs=[row(D), row(D), one] + [row(w) for w in PIECES] + [row(SMALL), ANY, pl.BlockSpec((D, SMALL), lambda i: (0, 0))],
        out_specs=[row(D), one],
        out_shape=[jax.ShapeDtypeStruct((T, D), F32), jax.ShapeDtypeStruct((1, D), F32)],
        scratch_shapes=[pltpu.VMEM((D, BIG_COLS), BF16), pltpu.SemaphoreType.DMA((1,))],
        compiler_params=_cp("arbitrary"),
    )(dh, h, nw, *pieces, dsmall, wbig, wsmall)


def _gla_block(q_ref, k_ref, sm_ref, wg_ref, bg_ref, nc, tril):
    nbat = GLA_H * nc
    q = _heads_first(q_ref[...].astype(F32), nc, GLA_DK)
    k = _heads_first(k_ref[...].astype(F32), nc, GLA_DK)
    pre = _heads_first(_dot(sm_ref[...], wg_ref[...], HI) + bg_ref[...], nc, GLA_DK)
    la = _log_sigmoid(pre) * (1.0 / GLA_TAU)
    bc = _bmm(jnp.broadcast_to(tril, (nbat, CHUNK, CHUNK)), la, HI)
    bl = bc[:, CHUNK - 1:CHUNK, :]
    eb = jnp.exp(bc)
    enb = jnp.exp(-bc)
    ebl = jnp.exp(bl - bc)
    q_in = q * (GLA_DK ** -0.5) * eb
    k_out = k * enb
    k_st = k * ebl
    a_ch = jnp.exp(bl)
    return pre, eb, enb, ebl, q_in, k_out, k_st, a_ch


def _gla_specs(blk, idx):
    hk, hv = GLA_H * GLA_DK, GLA_H * GLA_DV
    return [pl.BlockSpec((blk, hk), lambda j: (idx(j), 0)),
            pl.BlockSpec((blk, hk), lambda j: (idx(j), 1)),
            pl.BlockSpec((blk, hv), lambda j: (idx(j), 1)),
            pl.BlockSpec((blk, SMALL), lambda j: (idx(j), 0)),
            pl.BlockSpec((SMALL, hk), lambda j: (0, 0)),
            pl.BlockSpec((1, hk), lambda j: (0, 0))]


def _gla_fwd(pbig, psmall, wgate, bgate, name):
    T = pbig.shape[0]
    blk = min(ATT_BLK, T)
    nc = blk // CHUNK

    def body(q_ref, k_ref, v_ref, sm_ref, wg_ref, bg_ref, o_ref, ss_ref, st_ref):
        @pl.when(pl.program_id(0) == 0)
        def _():
            st_ref[...] = jnp.zeros_like(st_ref)

        causal = _iota2(CHUNK, CHUNK, 0) >= _iota2(CHUNK, CHUNK, 1)
        _, _, _, _, q_in, k_out, k_st, a_ch = _gla_block(q_ref, k_ref, sm_ref, wg_ref, bg_ref, nc, causal.astype(F32))
        v = _heads_first(v_ref[...], nc, GLA_DV)
        qb = _b(q_in)
        sc = jnp.where(causal, _bmm_nt(qb, _b(k_out)), 0.0)
        kv = _bmm_tn(v, _b(k_st))
        before = [None] * (GLA_H * nc)
        for i in range(GLA_H):
            st = st_ref[i]
            for c in range(nc):
                n = i * nc + c
                before[n] = st
                st = st * a_ch[n] + kv[n]
            st_ref[i] = st
        states = jnp.stack(before)
        ss_ref[...] = states.reshape(GLA_H, nc, GLA_DV, GLA_DK)
        o_ref[...] = _heads_last(_bmm(_b(sc), v) + _bmm_nt(qb, _b(states)), nc)

    return pl.pallas_call(
        body, name=name, grid=(T // blk,),
        in_specs=_gla_specs(blk, lambda j: j),
        out_specs=[pl.BlockSpec((blk, GLA_H * GLA_DV), lambda j: (j, 0)),
                   pl.BlockSpec((GLA_H, nc, GLA_DV, GLA_DK), lambda j: (0, j, 0, 0))],
        out_shape=[jax.ShapeDtypeStruct((T, GLA_H * GLA_DV), F32),
                   jax.ShapeDtypeStruct((GLA_H, T // CHUNK, GLA_DV, GLA_DK), F32)],
        scratch_shapes=[pltpu.VMEM((GLA_H, GLA_DV, GLA_DK), F32)],
        compiler_params=_cp("arbitrary"),
    )(pbig, pbig, pbig, psmall, wgate, bgate)


def _gla_bwd(pbig, psmall, wgate, bgate, states, do, name):
    T = pbig.shape[0]
    blk = min(ATT_BLK, T)
    nc = blk // CHUNK
    nb = T // blk
    nbat = GLA_H * nc

    def body(q_ref, k_ref, v_ref, sm_ref, wg_ref, bg_ref, ss_ref, do_ref, dq_ref, dk_ref, dv_ref, dpre_ref, dst_ref):
        @pl.when(pl.program_id(0) == 0)
        def _():
            dst_ref[...] = jnp.zeros_like(dst_ref)

        causal = _iota2(CHUNK, CHUNK, 0) >= _iota2(CHUNK, CHUNK, 1)
        triu = (_iota2(CHUNK, CHUNK, 0) <= _iota2(CHUNK, CHUNK, 1)).astype(F32)
        pre, eb, enb, ebl, q_in, k_out, k_st, a_ch = _gla_block(q_ref, k_ref, sm_ref, wg_ref, bg_ref, nc,
                                                                causal.astype(F32))
        v = _heads_first(v_ref[...], nc, GLA_DV)
        dob = _b(_heads_first(do_ref[...], nc, GLA_DV))
        st = ss_ref[...].reshape(nbat, GLA_DV, GLA_DK)
        qb, kob, kstb = _b(q_in), _b(k_out), _b(k_st)
        qdo = _bmm_tn(dob, qb)
        after = [None] * nbat
        for i in range(GLA_H):
            dst = dst_ref[i]
            for c in range(nc - 1, -1, -1):
                n = i * nc + c
                after[n] = dst
                dst = dst * a_ch[n] + qdo[n]
            dst_ref[i] = dst
        dsa = jnp.stack(after)
        dsb = _b(dsa)
        sc = jnp.where(causal, _bmm_nt(qb, kob), 0.0)
        dsc = _b(jnp.where(causal, _bmm_nt(dob, v), 0.0))
        dq_in = _bmm(dob, _b(st)) + _bmm(dsc, kob)
        dk_out = _bmm_tn(dsc, qb)
        dk_st = _bmm(v, dsb)
        dv_ref[...] = _b(_heads_last(_bmm_tn(_b(sc), dob) + _bmm_nt(kstb, dsb), nc))
        da_ch = jnp.sum(st * dsa, axis=1, keepdims=True)
        tk = dk_st * k_st
        db = dq_in * q_in - dk_out * k_out - tk
        db_last = jnp.sum(tk, axis=1, keepdims=True) + da_ch * a_ch
        dq_ref[...] = _b(_heads_last(dq_in * (GLA_DK ** -0.5) * eb, nc))
        dk_ref[...] = _b(_heads_last(dk_out * enb + dk_st * ebl, nc))
        dla = _bmm(jnp.broadcast_to(triu, (nbat, CHUNK, CHUNK)), db, HI) + db_last
        dpre_ref[...] = _heads_last(dla * (1.0 / GLA_TAU) * _sigmoid(-pre), nc)

    r = lambda j: nb - 1 - j
    hk, hv = GLA_H * GLA_DK, GLA_H * GLA_DV
    return pl.pallas_call(
        body, name=name, grid=(nb,),
        in_specs=_gla_specs(blk, r) + [pl.BlockSpec((GLA_H, nc, GLA_DV, GLA_DK), lambda j: (0, r(j), 0, 0)),
                                      pl.BlockSpec((blk, hv), lambda j: (r(j), 0))],
        out_specs=[pl.BlockSpec((blk, hk), lambda j: (r(j), 0)), pl.BlockSpec((blk, hk), lambda j: (r(j), 0)),
                   pl.BlockSpec((blk, hv), lambda j: (r(j), 0)), pl.BlockSpec((blk, hk), lambda j: (r(j), 0))],
        out_shape=[jax.ShapeDtypeStruct((T, hk), BF16), jax.ShapeDtypeStruct((T, hk), BF16),
                   jax.ShapeDtypeStruct((T, hv), BF16), jax.ShapeDtypeStruct((T, hk), F32)],
        scratch_shapes=[pltpu.VMEM((GLA_H, GLA_DV, GLA_DK), F32)],
        compiler_params=_cp("arbitrary"),
    )(pbig, pbig, pbig, psmall, wgate, bgate, states, do)


def _gla_gate_bwd(dpre, psmall, wgate, dsm, name):
    T = dpre.shape[0]
    tm = min(512, T)
    W = GLA_H * GLA_DK
    ngrp = dsm.shape[0]

    def body(dp_ref, sm_ref, wg_ref, dsm_ref, ds_ref, dw_ref, db_ref):
        @pl.when(pl.program_id(0) == 0)
        def _():
            dw_ref[...] = jnp.zeros_like(dw_ref)
            db_ref[...] = jnp.zeros_like(db_ref)

        dp = dp_ref[...]
        ds = _dot_nt(dp, wg_ref[...], HI)
        for i in range(ngrp):
            ds += dsm_ref[i]
        ds_ref[...] = ds
        dw_ref[...] += _dot_tn(sm_ref[...], dp, HI)
        db_ref[...] += jnp.sum(dp, axis=0, keepdims=True)

    return pl.pallas_call(
        body, name=name, grid=(T // tm,),
        in_specs=[pl.BlockSpec((tm, W), lambda i: (i, 0)), pl.BlockSpec((tm, SMALL), lambda i: (i, 0)),
                  pl.BlockSpec((SMALL, W), lambda i: (0, 0)), pl.BlockSpec((ngrp, tm, SMALL), lambda i: (0, i, 0))],
        out_specs=[pl.BlockSpec((tm, SMALL), lambda i: (i, 0)), pl.BlockSpec((SMALL, W), lambda i: (0, 0)),
                   pl.BlockSpec((1, W), lambda i: (0, 0))],
        out_shape=[jax.ShapeDtypeStruct((T, SMALL), F32), jax.ShapeDtypeStruct((SMALL, W), F32),
                   jax.ShapeDtypeStruct((1, W), F32)],
        compiler_params=_cp("arbitrary"),
    )(dpre, psmall, wgate, dsm)


CONV_C = 3 * 1024
CONV_BLK = 256


def _conv_fwd(pbig, cw8, name):
    T = pbig.shape[0]
    blk = min(CONV_BLK, T)

    def body(x_ref, w_ref, c_ref, prev_ref):
        @pl.when(pl.program_id(0) == 0)
        def _():
            prev_ref[...] = jnp.zeros_like(prev_ref)

        x = x_ref[...].astype(F32)
        prev = prev_ref[...]
        row8 = _iota2(8, CONV_C, 0)
        acc = x * w_ref[CONV_K - 1:CONV_K, :]
        for s in range(1, CONV_K):
            xs = pltpu.roll(x, s, 0)
            top = jnp.where(row8 < s, pltpu.roll(prev, s, 0), xs[:8])
            xs = jnp.concatenate([top, xs[8:]], axis=0)
            acc += xs * w_ref[CONV_K - 1 - s:CONV_K - s, :]
        c_ref[...] = _b(acc)
        prev_ref[...] = x[blk - 8:]

    return pl.pallas_call(
        body, name=name, grid=(T // blk,),
        in_specs=[pl.BlockSpec((blk, CONV_C), lambda i: (i, 1)), pl.BlockSpec((8, CONV_C), lambda i: (0, 0))],
        out_specs=pl.BlockSpec((blk, CONV_C), lambda i: (i, 0)),
        out_shape=jax.ShapeDtypeStruct((T, CONV_C), BF16),
        scratch_shapes=[pltpu.VMEM((8, CONV_C), F32)],
        compiler_params=_cp("arbitrary"),
    )(pbig, cw8)


def _conv_bwd(dcq, dck, dcv, pbig, cw8, name):
    T = pbig.shape[0]
    blk = min(CONV_BLK, T)
    nb = T // blk

    def body(dq_ref, dk_ref, dv_ref, x_ref, w_ref, dx_ref, dw_ref, nxt_ref):
        @pl.when(pl.program_id(0) == 0)
        def _():
            nxt_ref[...] = jnp.zeros_like(nxt_ref)
            dw_ref[...] = jnp.zeros_like(dw_ref)

        dc = jnp.concatenate([dq_ref[...], dk_ref[...], dv_ref[...]], axis=1).astype(F32)
        x = x_ref[...].astype(F32)
        nxt = nxt_ref[...]
        row8 = _iota2(8, CONV_C, 0)
        acc = dc * w_ref[CONV_K - 1:CONV_K, :]
        dws = [jnp.sum(dc * x, axis=0, keepdims=True)]
        for s in range(1, CONV_K):
            ds = pltpu.roll(dc, blk - s, 0)
            bot = jnp.where(row8 >= 8 - s, pltpu.roll(nxt, 8 - s, 0), ds[blk - 8:])
            ds = jnp.concatenate([ds[:blk - 8], bot], axis=0)
            acc += ds * w_ref[CONV_K - 1 - s:CONV_K - s, :]
            dws.append(jnp.sum(ds * x, axis=0, keepdims=True))
        dx_ref[...] = _b(acc)
        dw_ref[...] += jnp.concatenate(dws[::-1] + [jnp.zeros((8 - CONV_K, CONV_C), F32)], axis=0)
        nxt_ref[...] = dc[:8]

    part = pl.BlockSpec((blk, 1024), lambda i: (nb - 1 - i, 0))
    return pl.pallas_call(
        body, name=name, grid=(nb,),
        in_specs=[part, part, part, pl.BlockSpec((blk, CONV_C), lambda i: (nb - 1 - i, 1)),
                  pl.BlockSpec((8, CONV_C), lambda i: (0, 0))],
        out_specs=[pl.BlockSpec((blk, CONV_C), lambda i: (nb - 1 - i, 0)), pl.BlockSpec((8, CONV_C), lambda i: (0, 0))],
        out_shape=[jax.ShapeDtypeStruct((T, CONV_C), BF16), jax.ShapeDtypeStruct((8, CONV_C), F32)],
        scratch_shapes=[pltpu.VMEM((8, CONV_C), F32)],
        compiler_params=_cp("arbitrary"),
    )(dcq, dck, dcv, pbig, cw8)


def _col(x, lane):
    sel = _iota2(x.shape[0], x.shape[1], 1) == lane
    return jnp.broadcast_to(jnp.sum(jnp.where(sel, x, 0.0), axis=1, keepdims=True), x.shape)


def _bmm(a, b, prec=None):
    return jnp.einsum("bij,bjk->bik", a, b, preferred_element_type=F32, precision=prec)


def _bmm_nt(a, b, prec=None):
    return jnp.einsum("bij,bkj->bik", a, b, preferred_element_type=F32, precision=prec)


def _bmm_tn(a, b, prec=None):
    return jnp.einsum("bji,bjk->bik", a, b, preferred_element_type=F32, precision=prec)


def _unit_lower_inverse(low):
    eye = (_iota2(CHUNK, CHUNK, 0) == _iota2(CHUNK, CHUNK, 1)).astype(F32)
    xk = -low
    inv = eye + xk
    for _ in range(5):
        xb = _b(xk)
        xk = _bmm(xb, xb)
        inv = inv + _bmm(_b(inv), _b(xk))
    resid = eye - _bmm(eye + low, inv, HI)
    return inv + _bmm(inv, resid, HI)


def _heads_first(x, nc, w=128):
    hb = x.shape[1] // w
    return jnp.concatenate([x[:, i * w:(i + 1) * w].reshape(nc, CHUNK, w) for i in range(hb)], axis=0)


def _heads_last(x, nc):
    hb = x.shape[0] // nc
    return jnp.concatenate([x[i * nc:(i + 1) * nc].reshape(nc * CHUNK, x.shape[2]) for i in range(hb)], axis=1)


def _gdn_block(cq_ref, ck_ref, cv_ref, sm_ref, par_ref, h0, hb, nc, masks, solved=None):
    causal, strict, tril, eye = masks
    nbat = hb * nc
    cq = _heads_first(cq_ref[...].astype(F32), nc)
    ck = _heads_first(ck_ref[...].astype(F32), nc)
    cv = _heads_first(cv_ref[...].astype(F32), nc)
    sq, sk, sv = _sigmoid(cq), _sigmoid(ck), _sigmoid(cv)
    q, k, v = cq * sq, ck * sk, cv * sv
    rq = lax.rsqrt(jnp.sum(q * q, axis=-1, keepdims=True) + EPS)
    rk = lax.rsqrt(jnp.sum(k * k, axis=-1, keepdims=True) + EPS)
    qh, kn = q * rq, k * rk
    qn = qh * (DN_DK ** -0.5)
    sm = sm_ref[...]
    per_head = lambda fn: jnp.concatenate([fn(i) for i in range(hb)], axis=0)
    braw = per_head(lambda i: _col(sm, GLA_RANK + h0 + i).reshape(nc, CHUNK, 128))
    araw = per_head(lambda i: _col(sm, GLA_RANK + DN_H + h0 + i).reshape(nc, CHUNK, 128))
    ea = per_head(lambda i: jnp.broadcast_to(jnp.exp(par_ref[i, 0:1, :])[None], (nc, 1, 128)))
    bias = per_head(lambda i: jnp.broadcast_to(par_ref[i, 1:2, :][None], (nc, 1, 128)))
    beta = _sigmoid(braw)
    sp_arg = araw + bias
    g = -ea * _softplus(sp_arg)
    G = _bmm(jnp.broadcast_to(tril, (nbat, CHUNK, CHUNK)), g, HI)
    gc = G[:, :, :CHUNK]
    grow = jnp.sum(eye * gc, axis=1, keepdims=True)
    decay = jnp.exp(jnp.where(causal, gc - grow, -1e30))
    kb = kn * beta
    A = _bmm_nt(_b(kb), _b(kn))
    eG = jnp.exp(G)
    gl = G[:, CHUNK - 1:CHUNK, :]
    eGl = jnp.exp(gl - G)
    g_ch = jnp.exp(gl)
    rv = v * beta
    rkk = kb * eG
    if solved is None:
        tinv_b = _b(_unit_lower_inverse(jnp.where(strict, A * decay, 0.0)))
        u = _bmm(tinv_b, _b(rv))
        w = _b(_bmm(tinv_b, _b(rkk)))
    else:
        tinv_b, u, w = solved
    B = _bmm_nt(_b(qn), _b(kn))
    qk = jnp.where(causal, B * decay, 0.0)
    q_dec = qn * eG
    k_st = kn * eGl
    return dict(cq=cq, ck=ck, cv=cv, sq=sq, sk=sk, sv=sv, q=q, k=k, v=v, rq=rq, rk=rk, qh=qh, kn=kn, qn=qn,
                beta=beta, ea=ea, sp_arg=sp_arg, g=g, G=G, decay=decay, kb=kb, A=A, tinv_b=tinv_b, eG=eG, eGl=eGl,
                g_ch=g_ch, rv=rv, rkk=rkk, u=u, w=w, B=B, qk=qk, q_dec=q_dec, k_st=k_st)


def _gdn_masks():
    r, c = _iota2(CHUNK, CHUNK, 0), _iota2(CHUNK, CHUNK, 1)
    return r >= c, r > c, (r >= c).astype(F32), (r == c).astype(F32)


def _gdn_specs(blk, hb, idx):
    ng = DN_H // hb
    return [pl.BlockSpec((blk, hb * DN_DK), lambda h, j: (idx(j), h)),
            pl.BlockSpec((blk, hb * DN_DK), lambda h, j: (idx(j), ng + h)),
            pl.BlockSpec((blk, hb * DN_DV), lambda h, j: (idx(j), 2 * ng + h)),
            pl.BlockSpec((blk, SMALL), lambda h, j: (idx(j), 0)),
            pl.BlockSpec((hb, 8, 128), lambda h, j: (h, 0, 0))]


def _gdn_solved_specs(blk, hb, idx):
    nc = blk // CHUNK
    spec = lambda w: pl.BlockSpec((hb, nc, CHUNK, w), lambda h, j: (h, idx(j), 0, 0))
    return [spec(CHUNK), spec(DN_DV), spec(DN_DK)]


def _gdn_fwd(conv, psmall, par, name):
    T = conv.shape[0]
    blk = min(GDN_BLK, T)
    nc = blk // CHUNK
    hb = GDN_HEADS
    N = T // CHUNK

    def body(cq_ref, ck_ref, cv_ref, sm_ref, par_ref, o_ref, ss_ref, ti_ref, u_ref, w_ref, s_ref):
        @pl.when(pl.program_id(1) == 0)
        def _():
            s_ref[...] = jnp.zeros_like(s_ref)

        f = _gdn_block(cq_ref, ck_ref, cv_ref, sm_ref, par_ref, pl.program_id(0) * hb, hb, nc, _gdn_masks())
        ti_ref[...] = f["tinv_b"].reshape(hb, nc, CHUNK, CHUNK)
        u_ref[...] = f["u"].reshape(hb, nc, CHUNK, DN_DV)
        w_ref[...] = f["w"].reshape(hb, nc, CHUNK, DN_DK)
        wb, qdb, kstb, qkb = f["w"], _b(f["q_dec"]), _b(f["k_st"]), _b(f["qk"])
        S = [s_ref[i] for i in range(hb)]
        for c in range(nc):
            for i in range(hb):
                n = i * nc + c
                ss_ref[i, c] = S[i]
                Sb = _b(S[i])
                v_new = _b(f["u"][n] - _dot(wb[n], Sb))
                o_ref[pl.ds(c * CHUNK, CHUNK), i * DN_DV:(i + 1) * DN_DV] = _dot(qdb[n], Sb) + _dot(qkb[n], v_new)
                S[i] = S[i] * f["g_ch"][n] + _dot_tn(kstb[n], v_new)
        for i in range(hb):
            s_ref[i] = S[i]

    return pl.pallas_call(
        body, name=name, grid=(DN_H // hb, T // blk),
        in_specs=_gdn_specs(blk, hb, lambda j: j),
        out_specs=[pl.BlockSpec((blk, hb * DN_DV), lambda h, j: (j, h)),
                   pl.BlockSpec((hb, nc, DN_DK, DN_DV), lambda h, j: (h, j, 0, 0))]
        + _gdn_solved_specs(blk, hb, lambda j: j),
        out_shape=[jax.ShapeDtypeStruct((T, DN_H * DN_DV), F32), jax.ShapeDtypeStruct((DN_H, N, DN_DK, DN_DV), F32),
                   jax.ShapeDtypeStruct((DN_H, N, CHUNK, CHUNK), BF16), jax.ShapeDtypeStruct((DN_H, N, CHUNK, DN_DV), F32),
                   jax.ShapeDtypeStruct((DN_H, N, CHUNK, DN_DK), BF16)],
        scratch_shapes=[pltpu.VMEM((hb, DN_DK, DN_DV), F32)],
        compiler_params=_cp("parallel", "arbitrary"),
    )(conv, conv, conv, psmall, par)


def _gdn_bwd(conv, psmall, par, states, do, solved, name, carry=None):
    ex_in, ex_args, ex_out, ex_shape, ex_sems = _carry_specs(carry)
    T = conv.shape[0]
    blk = min(GDN_BLK, T)
    nc = blk // CHUNK
    nb = T // blk
    hb = GDN_HEADS
    nbat = hb * nc
    rsum = lambda x: jnp.sum(x, axis=-1, keepdims=True)

    def body(cq_ref, ck_ref, cv_ref, sm_ref, par_ref, ss_ref, do_ref, ti_ref, u_ref, w_ref,
             dcq_ref, dck_ref, dcv_ref, dsm_ref, dpar_ref, ds_ref):
        @pl.when(pl.program_id(1) == 0)
        def _():
            ds_ref[...] = jnp.zeros_like(ds_ref)
            dpar_ref[...] = jnp.zeros_like(dpar_ref)

        masks = _gdn_masks()
        causal, strict, tril, eye = masks
        triu = (_iota2(CHUNK, CHUNK, 0) <= _iota2(CHUNK, CHUNK, 1)).astype(F32)
        lane = _iota2(CHUNK, 128, 1)
        last_row = _iota2(CHUNK, 128, 0) == CHUNK - 1
        h0 = pl.program_id(0) * hb
        solved = (ti_ref[...].reshape(nbat, CHUNK, CHUNK), u_ref[...].reshape(nbat, CHUNK, DN_DV),
                  w_ref[...].reshape(nbat, CHUNK, DN_DK))
        f = _gdn_block(cq_ref, ck_ref, cv_ref, sm_ref, par_ref, h0, hb, nc, masks, solved)
        S = ss_ref[...].reshape(nbat, DN_DK, DN_DV)
        Sb = _b(S)
        do_ = _b(_heads_first(do_ref[...], nc))
        wb, qdb, kstb, qkb = _b(f["w"]), _b(f["q_dec"]), _b(f["k_st"]), _b(f["qk"])
        vnb = _b(f["u"] - _bmm(wb, Sb))
        dvn0 = _bmm_tn(qkb, do_)
        qdo = _bmm_tn(qdb, do_)
        dS = [ds_ref[i] for i in range(hb)]
        after = [None] * nbat
        for c in range(nc - 1, -1, -1):
            for i in range(hb):
                n = i * nc + c
                after[n] = dS[i]
                dvn_c = _b(dvn0[n] + _dot(kstb[n], _b(dS[i])))
                dS[i] = dS[i] * f["g_ch"][n] + qdo[n] - _dot_tn(wb[n], dvn_c)
        for i in range(hb):
            ds_ref[i] = dS[i]
        dSa = jnp.stack(after)
        dSb = _b(dSa)
        dvn = dvn0 + _bmm(kstb, dSb)
        dvnb = _b(dvn)
        dq_dec = _bmm_nt(do_, Sb)
        dqk = jnp.where(causal, _bmm_nt(do_, vnb), 0.0)
        dk_st = _bmm_nt(vnb, dSb)
        dg_ch = jnp.sum(rsum(S * dSa), axis=1, keepdims=True)
        dw = -_bmm_nt(dvnb, Sb)
        drv = _bmm_tn(f["tinv_b"], dvnb)
        drk = _bmm_tn(f["tinv_b"], _b(dw))
        dlow = jnp.where(strict, -(_bmm_nt(_b(drv), _b(f["u"])) + _bmm_nt(_b(drk), wb)), 0.0)
        dv = drv * f["beta"]
        dbeta = rsum(drv * f["v"])
        dkb = drk * f["eG"]
        dG = rsum(drk * f["rkk"])
        dA = dlow * f["decay"]
        ddec = dlow * f["A"]
        dkb += _bmm(_b(dA), _b(f["kn"]))
        dkn = _bmm_tn(_b(dA), _b(f["kb"]))
        dB = dqk * f["decay"]
        ddec += dqk * f["B"]
        dqn = _bmm(_b(dB), _b(f["kn"]))
        dkn += _bmm_tn(_b(dB), _b(f["qn"]))
        dD = ddec * f["decay"]
        dG += rsum(dD) - rsum(eye * jnp.sum(dD, axis=1, keepdims=True))
        dqn += dq_dec * f["eG"]
        dG += rsum(dq_dec * f["q_dec"])
        dkn += dk_st * f["eGl"]
        tks = rsum(dk_st * f["k_st"])
        dG -= tks
        dG_last = jnp.sum(tks, axis=1, keepdims=True) + dg_ch * f["g_ch"][:, :, :1]
        dkn += dkb * f["beta"]
        dbeta += rsum(dkb * f["kn"])
        dGf = jnp.broadcast_to(dG, (nbat, CHUNK, 128)) + jnp.where(last_row, dG_last, 0.0)
        dg = _bmm(jnp.broadcast_to(triu, (nbat, CHUNK, CHUNK)), dGf, HI)
        dbraw = dbeta * f["beta"][:, :, :1] * (1.0 - f["beta"][:, :, :1])
        daraw = dg * (-f["ea"]) * _sigmoid(f["sp_arg"])
        both = lambda t: jnp.sum(jnp.sum(t, axis=1, keepdims=True), axis=0)
        dgg = dg * f["g"]
        dsm = jnp.zeros((nc, CHUNK, SMALL), F32)
        for i in range(hb):
            mine = slice(i * nc, (i + 1) * nc)
            dsm += (jnp.where(lane == GLA_RANK + h0 + i, dbraw[mine], 0.0)
                    + jnp.where(lane == GLA_RANK + DN_H + h0 + i, daraw[mine], 0.0))
            dpar = jnp.where(lane[:1] == 0, both(dgg[mine]), jnp.where(lane[:1] == 1, both(daraw[mine]), 0.0))
            dpar_ref[i] += jnp.broadcast_to(dpar, (8, 128))
        dsm_ref[0] = dsm.reshape(blk, SMALL)
        dqh = dqn * (DN_DK ** -0.5)
        dq = f["rq"] * (dqh - f["qh"] * rsum(dqh * f["qh"]))
        dk = f["rk"] * (dkn - f["kn"] * rsum(dkn * f["kn"]))
        dsilu = lambda x, s: s * (1.0 + x * (1.0 - s))
        dcq_ref[...] = _b(_heads_last(dq * dsilu(f["cq"], f["sq"]), nc))
        dck_ref[...] = _b(_heads_last(dk * dsilu(f["ck"], f["sk"]), nc))
        dcv_ref[...] = _b(_heads_last(dv * dsilu(f["cv"], f["sv"]), nc))

    r = lambda j: nb - 1 - j
    out_blk = pl.BlockSpec((blk, hb * DN_DK), lambda h, j: (r(j), h))
    grid = (DN_H // hb, nb)
    return pl.pallas_call(
        _carry(carry, body, 10, 5, grid), name=name, grid=grid,
        in_specs=_gdn_specs(blk, hb, r) + [pl.BlockSpec((hb, nc, DN_DK, DN_DV), lambda h, j: (h, r(j), 0, 0)),
                                          pl.BlockSpec((blk, hb * DN_DV), lambda h, j: (r(j), h))]
        + _gdn_solved_specs(blk, hb, r) + ex_in,
        out_specs=[out_blk, out_blk, out_blk, pl.BlockSpec((1, blk, SMALL), lambda h, j: (h, r(j), 0)),
                   pl.BlockSpec((hb, 8, 128), lambda h, j: (h, 0, 0))] + ex_out,
        out_shape=[jax.ShapeDtypeStruct((T, DN_H * DN_DK), BF16)] * 3 + [
            jax.ShapeDtypeStruct((DN_H // hb, T, SMALL), F32), jax.ShapeDtypeStruct((DN_H, 8, 128), F32)] + ex_shape,
        scratch_shapes=[pltpu.VMEM((hb, DN_DK, DN_DV), F32)] + ex_sems,
        compiler_params=_cp("arbitrary", "arbitrary"),
    )(conv, conv, conv, psmall, par, states, do, *solved, *ex_args)


def _head_norm(o, w, dv):
    outs, rs = [], []
    for i in range(o.shape[1] // dv):
        oh = o[:, i * dv:(i + 1) * dv]
        r = lax.rsqrt(jnp.mean(oh * oh, axis=-1, keepdims=True) + EPS)
        outs.append(oh * r)
        rs.append(r)
    return outs, rs


def _merge_specs(tm):
    col = lambda c: pl.BlockSpec((tm, D), lambda i: (i, c))
    return [col(0), col(0), col(2), col(6), col(7), col(8),
            pl.BlockSpec((1, GLA_DV), lambda i: (0, 0)), pl.BlockSpec((1, DN_DV), lambda i: (0, 0)),
            pl.BlockSpec((D, D), lambda i: (0, 0))]


def _merge_fwd(h, oa, ob, pbig, gla_hn, dn_hn, wout, name):
    T = h.shape[0]
    tm = min(ROW_BLK, T)

    def body(h_ref, oa_ref, ob_ref, gr_ref, dg_ref, ma_ref, mb_ref, wa_ref, wb_ref, wo_ref, ho_ref, y_ref):
        na, _ = _head_norm(oa_ref[...], wa_ref[...], GLA_DV)
        nbs, _ = _head_norm(ob_ref[...], wb_ref[...], DN_DV)
        hna = jnp.concatenate([t * wa_ref[...] for t in na], axis=1)
        hnb = jnp.concatenate([t * wb_ref[...] for t in nbs], axis=1)
        gr = gr_ref[...].astype(F32)
        dg = dg_ref[...].astype(F32)
        y = (_sigmoid(ma_ref[...].astype(F32)) * hna * (gr * _sigmoid(gr))
             + _sigmoid(mb_ref[...].astype(F32)) * hnb * (dg * _sigmoid(dg)))
        yb = _b(y)
        y_ref[...] = yb
        ho_ref[...] = h_ref[...] + _dot(yb, wo_ref[...])

    row = pl.BlockSpec((tm, D), lambda i: (i, 0))
    return pl.pallas_call(
        body, name=name, grid=(T // tm,),
        in_specs=[row] + _merge_specs(tm),
        out_specs=[row, row],
        out_shape=[jax.ShapeDtypeStruct((T, D), F32), jax.ShapeDtypeStruct((T, D), BF16)],
        compiler_params=_cp("arbitrary"),
    )(h, oa, ob, pbig, pbig, pbig, pbig, gla_hn, dn_hn, wout)


def _merge_bwd(dh, oa, ob, pbig, gla_hn, dn_hn, wout, name):
    T = dh.shape[0]
    tm = min(ROW_BLK, T)

    def branch(dy, o_ref, w_ref, gate_ref, m_ref, dv):
        w = w_ref[...]
        ohat, rs = _head_norm(o_ref[...], w, dv)
        gate = gate_ref[...].astype(F32)
        m = m_ref[...].astype(F32)
        sgate, sm = _sigmoid(gate), _sigmoid(m)
        silu = gate * sgate
        ohat_all = jnp.concatenate(ohat, axis=1)
        hn = jnp.concatenate([t * w for t in ohat], axis=1)
        d_on = dy * sm
        d_m = dy * hn * silu * sm * (1.0 - sm)
        d_hn = d_on * silu
        d_gate = d_on * hn * (sgate * (1.0 + gate * (1.0 - sgate)))
        dw = jnp.zeros((1, dv), F32)
        d_o = []
        for i, (oh, r) in enumerate(zip(ohat, rs)):
            dhn = d_hn[:, i * dv:(i + 1) * dv]
            dw += jnp.sum(dhn * oh, axis=0, keepdims=True)
            dohat = dhn * w
            d_o.append(r * (dohat - oh * jnp.mean(dohat * oh, axis=-1, keepdims=True)))
        return jnp.concatenate(d_o, axis=1), d_gate, d_m, dw

    def body(dh_ref, oa_ref, ob_ref, gr_ref, dg_ref, ma_ref, mb_ref, wa_ref, wb_ref, wo_ref,
             doa_ref, dob_ref, dgr_ref, ddg_ref, dma_ref, dmb_ref, dwa_ref, dwb_ref, dhb_ref):
        @pl.when(pl.program_id(0) == 0)
        def _():
            dwa_ref[...] = jnp.zeros_like(dwa_ref)
            dwb_ref[...] = jnp.zeros_like(dwb_ref)

        dhb = _b(dh_ref[...])
        dhb_ref[...] = dhb
        dy = _dot_nt(dhb, wo_ref[...])
        d_oa, d_gr, d_ma, dwa = branch(dy, oa_ref, wa_ref, gr_ref, ma_ref, GLA_DV)
        d_ob, d_dg, d_mb, dwb = branch(dy, ob_ref, wb_ref, dg_ref, mb_ref, DN_DV)
        doa_ref[...] = _b(d_oa)
        dob_ref[...] = _b(d_ob)
        dgr_ref[...] = _b(d_gr)
        ddg_ref[...] = _b(d_dg)
        dma_ref[...] = _b(d_ma)
        dmb_ref[...] = _b(d_mb)
        dwa_ref[...] += dwa
        dwb_ref[...] += dwb

    row = pl.BlockSpec((tm, D), lambda i: (i, 0))
    f32 = jax.ShapeDtypeStruct((T, D), F32)
    b16 = jax.ShapeDtypeStruct((T, D), BF16)
    return pl.pallas_call(
        body, name=name, grid=(T // tm,),
        in_specs=[row] + _merge_specs(tm),
        out_specs=[row] * 6 + [pl.BlockSpec((1, GLA_DV), lambda i: (0, 0)), pl.BlockSpec((1, DN_DV), lambda i: (0, 0)), row],
        out_shape=[b16, b16, b16, b16, b16, b16, jax.ShapeDtypeStruct((1, GLA_DV), F32),
                   jax.ShapeDtypeStruct((1, DN_DV), F32), b16],
        compiler_params=_cp("arbitrary"),
    )(dh, oa, ob, pbig, pbig, pbig, pbig, gla_hn, dn_hn, wout)


def _loss_head(h, nw, target, name):
    T = h.shape[0]
    tm = min(512, T)

    def body(h_ref, nw_ref, t_ref, dx_ref, loss_ref, dnw_ref):
        @pl.when(pl.program_id(0) == 0)
        def _():
            loss_ref[...] = jnp.zeros_like(loss_ref)
            dnw_ref[...] = jnp.zeros_like(dnw_ref)

        x = h_ref[...]
        w = nw_ref[...]
        r = lax.rsqrt(jnp.mean(x * x, axis=-1, keepdims=True) + EPS)
        xhat = x * r
        err = xhat * w - t_ref[...]
        part = jnp.sum(jnp.sum(err * err, axis=-1, keepdims=True), axis=0, keepdims=True)
        loss_ref[...] += (0.5 / D) * part
        dout = err * (1.0 / D)
        dnw_ref[...] += jnp.sum(dout * xhat, axis=0, keepdims=True)
        dxhat = dout * w
        dx_ref[...] = r * (dxhat - xhat * jnp.mean(dxhat * xhat, axis=-1, keepdims=True))

    row = pl.BlockSpec((tm, D), lambda i: (i, 0))
    one = pl.BlockSpec((1, D), lambda i: (0, 0))
    return pl.pallas_call(
        body, name=name, grid=(T // tm,),
        in_specs=[row, one, row],
        out_specs=[row, pl.BlockSpec((8, 128), lambda i: (0, 0)), one],
        out_shape=[jax.ShapeDtypeStruct((T, D), F32), jax.ShapeDtypeStruct((8, 128), F32),
                   jax.ShapeDtypeStruct((1, D), F32)],
        compiler_params=_cp("arbitrary"),
    )(h, nw, target)


def _adamw(w, g, m, v, rows, name):
    R, C = w.shape
    rows = min(rows, R)
    c1 = 1.0 - ADAM_B1 ** ADAM_STEP
    c2 = 1.0 - ADAM_B2 ** ADAM_STEP

    def body(w_ref, g_ref, m_ref, v_ref, d_ref, mo_ref, vo_ref):
        g_ = g_ref[...]
        m_ = ADAM_B1 * m_ref[...] + (1.0 - ADAM_B1) * g_
        v_ = ADAM_B2 * v_ref[...] + (1.0 - ADAM_B2) * (g_ * g_)
        mo_ref[...] = m_
        vo_ref[...] = v_
        d_ref[...] = -ADAM_LR * ((m_ / c1) / (jnp.sqrt(v_ / c2) + ADAM_EPS) + ADAM_WD * w_ref[...])

    blk = pl.BlockSpec((rows, C), lambda i: (i, 0))
    shp = jax.ShapeDtypeStruct((R, C), F32)
    return pl.pallas_call(
        body, name=name, grid=(R // rows,),
        in_specs=[blk] * 4, out_specs=[blk] * 3, out_shape=[shp] * 3,
        compiler_params=_cp("parallel"),
    )(w, g, m, v)


def _me():
    return lax.axis_index("x"), lax.axis_index("y"), lax.axis_index("c")


def _other_chips(x, y):
    return [(1 - x, y), (x, 1 - y), (1 - x, 1 - y)]


def _half_rows(ref, hf):
    half = ref.shape[-2] // 2
    rows = pl.ds(pl.multiple_of(hf * half, 16), half)
    return ref.at[rows, :] if len(ref.shape) == 2 else ref.at[:, rows, :]


class _GatherBig:
    def __init__(self, big):
        self.arrays = list(big)
        self.out_shape = [jax.ShapeDtypeStruct((N_SHARD,) + w.shape, w.dtype) for w in big]
        self.n_sem = 7 * len(big)

    @staticmethod
    def _copy(sems, k, src, dst, to):
        return pltpu.make_async_remote_copy(src_ref=src, dst_ref=dst, send_sem=sems[0].at[k], recv_sem=sems[1].at[k],
                                            device_id=to, device_id_type=MESH)

    def start(self, ins, outs, *sems):
        x, y, c = _me()
        mine = 2 * x + y
        for i, (w_ref, o_ref) in enumerate(zip(ins, outs)):
            self._copy(sems, 7 * i + 6, w_ref, o_ref.at[mine], (x, y, 1 - c)).start()
            for j, chip in enumerate(_other_chips(x, y)):
                self._copy(sems, 7 * i + j, _half_rows(w_ref, c), _half_rows(o_ref.at[mine], c), (*chip, c)).start()

    def finish(self, ins, outs, *sems):
        x, y, c = _me()
        me, sibling = (x, y, c), (x, y, 1 - c)
        chips = _other_chips(x, y)
        slot = lambda chip: 2 * chip[0] + chip[1]
        for i, o_ref in enumerate(outs):
            for j, chip in enumerate(chips):
                landed = _half_rows(o_ref.at[slot(chip)], c)
                self._copy(sems, 7 * i + j, landed, landed, me).wait_recv()
                self._copy(sems, 7 * i + 3 + j, landed, landed, sibling).start()
        for i, (w_ref, o_ref) in enumerate(zip(ins, outs)):
            for j, chip in enumerate(chips):
                passed = _half_rows(o_ref.at[slot(chip)], 1 - c)
                self._copy(sems, 7 * i + 3 + j, passed, passed, me).wait_recv()
            self._copy(sems, 7 * i + 6, o_ref.at[slot((x, y))], o_ref.at[slot((x, y))], me).wait_recv()
        for i, (w_ref, o_ref) in enumerate(zip(ins, outs)):
            self._copy(sems, 7 * i + 6, w_ref, o_ref.at[slot((x, y))], sibling).wait_send()
            for j, chip in enumerate(chips):
                self._copy(sems, 7 * i + j, _half_rows(w_ref, c), _half_rows(o_ref.at[slot((x, y))], c),
                           (*chip, c)).wait_send()
                landed = _half_rows(o_ref.at[slot(chip)], c)
                self._copy(sems, 7 * i + 3 + j, landed, landed, sibling).wait_send()


class _ChipsExchange:
    def __init__(self, pbs):
        self.arrays = list(pbs)
        self.out_shape = [jax.ShapeDtypeStruct((3,) + p.shape[1:], p.dtype) for p in pbs]
        self.n_sem = 3 * len(pbs)

    def _copies(self, ins, outs, send_sems, recv_sems):
        x, y, c = _me()
        return [pltpu.make_async_remote_copy(src_ref=ins[i].at[2 * chip[0] + chip[1]], dst_ref=outs[i].at[j],
                                             send_sem=send_sems.at[3 * i + j], recv_sem=recv_sems.at[3 * i + j],
                                             device_id=(*chip, c), device_id_type=MESH)
                for i in range(len(ins)) for j, chip in enumerate(_other_chips(x, y))]

    def start(self, ins, outs, *sems):
        for cp in self._copies(ins, outs, *sems):
            cp.start()

    def finish(self, ins, outs, *sems):
        for cp in self._copies(ins, outs, *sems):
            cp.wait()


def _carry(ex, body, n_in, n_out, grid):
    if ex is None:
        return body
    ni, no = len(ex.arrays), len(ex.out_shape)

    def carried(*refs):
        ins, ex_in = refs[:n_in], refs[n_in:n_in + ni]
        outs, ex_out = refs[n_in + ni:n_in + ni + n_out], refs[n_in + ni + n_out:n_in + ni + n_out + no]
        scratch, sems = refs[n_in + ni + n_out + no:-2], refs[-2:]
        ids = [pl.program_id(a) for a in range(len(grid))]
        first = functools.reduce(jnp.logical_and, [i == 0 for i in ids])
        last = functools.reduce(jnp.logical_and, [i == g - 1 for i, g in zip(ids, grid)])

        @pl.when(first)
        def _():
            ex.start(ex_in, ex_out, *sems)

        body(*ins, *outs, *scratch)

        @pl.when(last)
        def _():
            ex.finish(ex_in, ex_out, *sems)

    return carried


def _carry_specs(ex):
    if ex is None:
        return [], [], [], [], []
    sems = [pltpu.SemaphoreType.DMA((ex.n_sem,)), pltpu.SemaphoreType.DMA((ex.n_sem,))]
    return [ANY] * len(ex.arrays), ex.arrays, [ANY] * len(ex.out_shape), ex.out_shape, sems


def _gather_weights(big, small, name):
    nbig, nsm = len(big), len(small)
    n = nbig + nsm
    own_sem = 6 * nbig + 3 * nsm

    def body(*refs):
        ins, outs = refs[:n], refs[n:2 * n]
        send_sems, recv_sems = refs[2 * n:]
        x, y, c = _me()
        sibling = (x, y, 1 - c)
        chips = _other_chips(x, y)
        slot = lambda chip: 2 * chip[0] + chip[1]

        def copy(k, src, dst, to):
            return pltpu.make_async_remote_copy(src_ref=src, dst_ref=dst, send_sem=send_sems.at[k],
                                                recv_sem=recv_sems.at[k], device_id=to, device_id_type=MESH)

        sent = []
        for i in range(nbig):
            sent.append(copy(own_sem + i, ins[i], outs[i].at[slot((x, y))], sibling))
            sent[-1].start()
            for j, chip in enumerate(chips):
                sent.append(copy(6 * i + j, _half_rows(ins[i], c), _half_rows(outs[i].at[slot((x, y))], c), (*chip, c)))
                sent[-1].start()
        for t in range(nsm):
            w_ref, o_ref = ins[nbig + t], outs[nbig + t]
            o_ref[slot((x, y))] = w_ref[...]
            for j, chip in enumerate(chips):
                sent.append(copy(6 * nbig + 3 * t + j, w_ref, o_ref.at[slot((x, y))], (*chip, c)))
                sent[-1].start()
        for i in range(nbig):
            for j, chip in enumerate(chips):
                landed = _half_rows(outs[i].at[slot(chip)], c)
                copy(6 * i + j, landed, landed, (x, y, c)).wait_recv()
                sent.append(copy(6 * i + 3 + j, landed, landed, sibling))
                sent[-1].start()
        for t in range(nsm):
            for j, chip in enumerate(chips):
                landed = outs[nbig + t].at[slot(chip)]
                copy(6 * nbig + 3 * t + j, landed, landed, (x, y, c)).wait_recv()
        for i in range(nbig):
            for j, chip in enumerate(chips):
                passed = _half_rows(outs[i].at[slot(chip)], 1 - c)
                copy(6 * i + 3 + j, passed, passed, (x, y, c)).wait_recv()
        for i in range(nbig):
            mine = outs[i].at[slot((x, y))]
            copy(own_sem + i, mine, mine, (x, y, c)).wait_recv()
        for cp in sent:
            cp.wait_send()

    vm = pl.BlockSpec(memory_space=pltpu.VMEM)
    nsem = own_sem + nbig
    return pl.pallas_call(
        body, name=name, in_specs=[ANY] * nbig + [vm] * nsm, out_specs=[ANY] * nbig + [vm] * nsm,
        out_shape=[jax.ShapeDtypeStruct((N_SHARD,) + w.shape, w.dtype) for w in list(big) + list(small)],
        scratch_shapes=[pltpu.SemaphoreType.DMA((nsem,)), pltpu.SemaphoreType.DMA((nsem,))],
        compiler_params=pltpu.CompilerParams(has_side_effects=True),
    )(*big, *small)


def _rs_sibling(gs, name):
    n = len(gs)

    def body(*refs):
        send_sems, recv_sems = refs[2 * n:]
        x, y, c = _me()
        cps = [pltpu.make_async_remote_copy(src_ref=_half_rows(refs[i], 1 - c), dst_ref=refs[n + i],
                                            send_sem=send_sems.at[i], recv_sem=recv_sems.at[i],
                                            device_id=(x, y, 1 - c), device_id_type=MESH) for i in range(n)]
        for cp in cps:
            cp.start()
        for cp in cps:
            cp.wait()

    return pl.pallas_call(
        body, name=name, in_specs=[ANY] * n, out_specs=[ANY] * n,
        out_shape=[jax.ShapeDtypeStruct((g.shape[0], g.shape[1] // 2, g.shape[2]), g.dtype) for g in gs],
        scratch_shapes=[pltpu.SemaphoreType.DMA((n,)), pltpu.SemaphoreType.DMA((n,))],
        compiler_params=pltpu.CompilerParams(has_side_effects=True),
    )(*gs)


def _add_pair(g, other, where, name):
    ns, a, b = g.shape
    half = a // 2

    def body(w_ref, g_ref, o_ref, pb_ref, own_ref):
        t = g_ref[0].astype(F32) + o_ref[0].astype(F32)
        pb_ref[0] = _b(t)

        @pl.when(pl.program_id(0) == w_ref[1])
        def _():
            own_ref[...] = t

    return pl.pallas_call(
        body, name=name,
        grid_spec=pltpu.PrefetchScalarGridSpec(
            num_scalar_prefetch=1, grid=(ns,),
            in_specs=[pl.BlockSpec((1, half, b), lambda s, w: (s, w[0], 0)), pl.BlockSpec((1, half, b), lambda s, w: (s, 0, 0))],
            out_specs=[pl.BlockSpec((1, half, b), lambda s, w: (s, 0, 0)), pl.BlockSpec((half, b), lambda s, w: (0, 0))]),
        out_shape=[jax.ShapeDtypeStruct((ns, half, b), BF16), jax.ShapeDtypeStruct((half, b), F32)],
        compiler_params=_cp("arbitrary"),
    )(where, g, other)


def _add_four(own, got, name):
    rows, cols = own.shape
    rb = rows // 2

    def body(a_ref, b_ref, o_ref):
        o_ref[...] = ((a_ref[...] + b_ref[0].astype(F32)) + b_ref[1].astype(F32)) + b_ref[2].astype(F32)

    return pl.pallas_call(
        body, name=name, grid=(rows // rb,),
        in_specs=[pl.BlockSpec((rb, cols), lambda i: (i, 0)), pl.BlockSpec((3, rb, cols), lambda i: (0, i, 0))],
        out_specs=pl.BlockSpec((rb, cols), lambda i: (i, 0)),
        out_shape=jax.ShapeDtypeStruct((rows, cols), F32),
        compiler_params=_cp("parallel"),
    )(own, got)


def _rs_swap(halves, name):
    n = len(halves)

    def body(*refs):
        send_sems, recv_sems = refs[2 * n:]
        x, y, c = _me()
        cps = [pltpu.make_async_remote_copy(src_ref=refs[i], dst_ref=refs[n + i], send_sem=send_sems.at[i],
                                            recv_sem=recv_sems.at[i], device_id=(x, y, 1 - c), device_id_type=MESH)
               for i in range(n)]
        for cp in cps:
            cp.start()
        for cp in cps:
            cp.wait()

    return pl.pallas_call(
        body, name=name, in_specs=[ANY] * n, out_specs=[ANY] * n,
        out_shape=[jax.ShapeDtypeStruct(h.shape, h.dtype) for h in halves],
        scratch_shapes=[pltpu.SemaphoreType.DMA((n,)), pltpu.SemaphoreType.DMA((n,))],
        compiler_params=pltpu.CompilerParams(has_side_effects=True),
    )(*halves)


def _adamw_halves(w, own, got, m, v, rows, name):
    a, b = w.shape
    nblk = a // 2 // rows
    c1 = 1.0 - ADAM_B1 ** ADAM_STEP
    c2 = 1.0 - ADAM_B2 ** ADAM_STEP

    def body(w_ref, own_ref, got_ref, m_ref, v_ref, g_ref, d_ref, mo_ref, vo_ref):
        g_ = jnp.where(pl.program_id(0) == lax.axis_index("c"), own_ref[...], got_ref[...])
        g_ref[...] = g_
        m_ = ADAM_B1 * m_ref[...] + (1.0 - ADAM_B1) * g_
        v_ = ADAM_B2 * v_ref[...] + (1.0 - ADAM_B2) * (g_ * g_)
        mo_ref[...] = m_
        vo_ref[...] = v_
        d_ref[...] = -ADAM_LR * ((m_ / c1) / (jnp.sqrt(v_ / c2) + ADAM_EPS) + ADAM_WD * w_ref[...])

    whole = pl.BlockSpec((rows, b), lambda h, i: (h * nblk + i, 0))
    part = pl.BlockSpec((rows, b), lambda h, i: (i, 0))
    shp = jax.ShapeDtypeStruct((a, b), F32)
    return pl.pallas_call(
        body, name=name, grid=(2, nblk),
        in_specs=[whole, part, part, whole, whole], out_specs=[whole] * 4, out_shape=[shp] * 4,
        compiler_params=_cp("parallel", "parallel"),
    )(w, own, got, m, v)


def _allsum_small(vec, name):
    def body(v_ref, o_ref, buf_ref, send_sems, recv_sems):
        x, y, c = _me()
        me = 4 * x + 2 * y + c
        buf_ref[me] = v_ref[...]
        cps = []
        for k in range(1, 8):
            peer = (x ^ (k >> 2), y ^ ((k >> 1) & 1), c ^ (k & 1))
            cps.append(pltpu.make_async_remote_copy(src_ref=v_ref, dst_ref=buf_ref.at[me],
                                                    send_sem=send_sems.at[k - 1], recv_sem=recv_sems.at[k - 1],
                                                    device_id=peer, device_id_type=MESH))
        for cp in cps:
            cp.start()
        for k in range(1, 8):
            peer_idx = me ^ k
            pltpu.make_async_remote_copy(src_ref=v_ref, dst_ref=buf_ref.at[peer_idx],
                                         send_sem=send_sems.at[k - 1], recv_sem=recv_sems.at[k - 1],
                                         device_id=(x, y, c), device_id_type=MESH).wait_recv()
        for cp in cps:
            cp.wait_send()
        acc = buf_ref[0]
        for d in range(1, 8):
            acc = acc + buf_ref[d]
        o_ref[...] = acc

    return pl.pallas_call(
        body, name=name,
        in_specs=[pl.BlockSpec(memory_space=pltpu.VMEM)], out_specs=pl.BlockSpec(memory_space=pltpu.VMEM),
        out_shape=jax.ShapeDtypeStruct(vec.shape, F32),
        scratch_shapes=[pltpu.VMEM((8,) + vec.shape, F32), pltpu.SemaphoreType.DMA((7,)), pltpu.SemaphoreType.DMA((7,))],
        compiler_params=pltpu.CompilerParams(has_side_effects=True),
    )(vec)


BIG = ("ffn1_w_gate", "ffn1_w_up", "ffn1_w_down", "w_in", "w_out", "ffn2_w_gate", "ffn2_w_up", "ffn2_w_down")
TINY = ("w_gla_gate", "conv_w")
SHARDED = BIG + TINY


def _join_cols(w4):
    return jnp.transpose(w4, (1, 0, 2)).reshape(w4.shape[1], N_SHARD * w4.shape[2])


def _cut_cols(w):
    return jnp.transpose(w.reshape(w.shape[0], N_SHARD, w.shape[1] // N_SHARD), (1, 0, 2))


def _split_w_in(w):
    o = IN_OFF
    big = jnp.concatenate([w[:, :o[4]], w[:, o[5]:o[9]], w[:, o[11]:]], axis=1)
    small = jnp.concatenate([w[:, o[4]:o[5]], w[:, o[9]:o[11]], jnp.zeros((w.shape[0], SMALL - 32), w.dtype)], axis=1)
    return big, small


def _merge_w_in(big, small):
    return jnp.concatenate([big[:, :3072], small[:, :16], big[:, 3072:7168], small[:, 16:32], big[:, 7168:]], axis=1)


class _Comm:
    def __init__(self, where, rest_shards):
        self.where = where
        self.w_in = _GatherBig(rest_shards[:1])
        self.later = _GatherBig(rest_shards[1:])
        self.pairs, self.got = {}, {}

    @staticmethod
    def w_in_weights(gathered):
        return dict(zip(("w_in_big", "w_in_small"), _split_w_in(_join_cols(gathered[0]))))

    @staticmethod
    def later_weights(gathered):
        W = dict(zip(BIG[4:], gathered))
        W["w_out"] = W["w_out"].reshape(D, D)
        return W

    def begin(self, names, grads):
        from_sibling = _rs_sibling(grads, "rs_sibling_" + names[0])
        for n, g, o in zip(names, grads, from_sibling):
            self.pairs[n] = _add_pair(g, o, self.where, "rs_pair_" + n)
        return _ChipsExchange([self.pairs[n][0] for n in names])

    def landed(self, names, outs):
        self.got.update(zip(names, outs))


def _local_step(x, target, W, P, comm=None):
    wgate_pad = jnp.zeros((SMALL, GLA_H * GLA_DK), F32).at[:GLA_RANK].set(P["w_gla_gate"])
    cw8 = jnp.zeros((8, CONV_C), F32).at[:CONV_K].set(P["conv_w"])
    par = jnp.zeros((DN_H, 8, 128), F32)
    par = par.at[:, 0, :].set(jnp.broadcast_to(P["dn_a_log"].reshape(DN_H, 1), (DN_H, 128)))
    par = par.at[:, 1, :].set(jnp.broadcast_to(P["dn_dt_bias"].reshape(DN_H, 1), (DN_H, 128)))

    h1, n1, g1, u1, *got = _ffn_fwd(x, P["ffn1_norm"], W["ffn1_w_gate"], W["ffn1_w_up"], W["ffn1_w_down"], "ffn1_fwd",
                                    carry=comm.w_in if comm else None)
    if comm:
        W = dict(W, **comm.w_in_weights(got))
    wbig, wsmall = W["w_in_big"], W["w_in_small"]
    pbig, psmall, n2, *got = _norm_proj(h1, P["mix_norm"], wbig, wsmall, "mix_proj", carry=comm.later if comm else None)
    if comm:
        W = dict(W, **comm.later_weights(got))
    oa, sa = _gla_fwd(pbig, psmall, wgate_pad, P["b_gla_gate"], "gla_fwd")
    conv = _conv_fwd(pbig, cw8, "conv_fwd")
    ob, sb, *solved = _gdn_fwd(conv, psmall, par, "gdn_fwd")
    h2, yb = _merge_fwd(h1, oa, ob, pbig, P["gla_head_norm"], P["dn_head_norm"], W["w_out"], "merge_fwd")
    h3, n3, g3, u3 = _ffn_fwd(h2, P["ffn2_norm"], W["ffn2_w_gate"], W["ffn2_w_up"], W["ffn2_w_down"], "ffn2_fwd")
    dh3, loss, d_final = _loss_head(h3, P["final_norm"], target, "loss_head")

    gw, gs = {}, {"final_norm": d_final}

    def ffn_grads(tag, dh, h, n, g, u, before=None):
        names = tuple(tag + s for s in ("_w_gate", "_w_up", "_w_down"))
        dg, du, act, dfb, *landed = _ffn_bwd_hidden(dh, g, u, W[names[2]], tag + "_bwd_hidden", carry=before)
        gw[names[0]] = _mm_tn(n, dg, D, FF_CUT, tag + "_dwg")
        gw[names[1]] = _mm_tn(n, du, D, FF_CUT, tag + "_dwu")
        gw[names[2]] = _mm_tn(act, dfb, FF_CUT, D, tag + "_dwd")
        dx, gs[tag + "_norm"], *own = _ffn_bwd_input(dh, h, P[tag + "_norm"], dg, du, W[names[0]], W[names[1]],
                                                     tag + "_bwd_input",
                                                     carry=comm.begin(names, [gw[n] for n in names]) if comm else None)
        if comm:
            comm.landed(names, own)
        return dx, landed

    dh2, _ = ffn_grads("ffn2", dh3, h2, n3, g3, u3)
    d_oa, d_ob, d_gr, d_dgate, d_ma, d_mb, gs["gla_head_norm"], gs["dn_head_norm"], dh2b = _merge_bwd(
        dh2, oa, ob, pbig, P["gla_head_norm"], P["dn_head_norm"], W["w_out"], "merge_bwd")
    gw["w_out"] = _mm_tn(yb, dh2b, D, D, "dw_out").reshape(N_SHARD, D // N_SHARD, D)
    early = ("w_out",)
    d_gq, d_gk, d_gv, dpre = _gla_bwd(pbig, psmall, wgate_pad, P["b_gla_gate"], sa, d_oa, "gla_bwd")
    dcq, dck, dcv, dsm, dpar, *landed = _gdn_bwd(conv, psmall, par, sb, d_ob, solved, "gdn_bwd",
                                                 carry=comm.begin(early, [gw[n] for n in early]) if comm else None)
    if comm:
        comm.landed(early, landed)
    dsmall, dwgate, gs["b_gla_gate"] = _gla_gate_bwd(dpre, psmall, wgate_pad, dsm, "gla_gate_bwd")
    gs["w_gla_gate"] = dwgate[:GLA_RANK]
    d_x3, dcw = _conv_bwd(dcq, dck, dcv, pbig, cw8, "conv_bwd")
    gs["conv_w"] = dcw[:CONV_K]
    gs["dn_a_log"] = dpar[:, 0, 0].reshape(1, DN_H)
    gs["dn_dt_bias"] = dpar[:, 0, 1].reshape(1, DN_H)
    pieces = (d_gq, d_gk, d_gv, d_gr, d_x3, d_dgate, d_ma, d_mb)
    dh1, gs["mix_norm"] = _proj_bwd(dh2, h1, P["mix_norm"], pieces, dsmall, wbig, wsmall, "proj_bwd")
    dbig = jnp.concatenate([_mm_tn(n2, p, D, 1024, "dw_in_%d" % i) for i, p in enumerate(pieces)], axis=1)
    dsml = _mm_tn(n2, dsmall, D, SMALL, "dw_in_small")
    gw["w_in"] = _cut_cols(_merge_w_in(dbig, dsml))
    grad_x, landed = ffn_grads("ffn1", dh1, x, n1, g1, u1,
                               before=comm.begin(("w_in",), [gw["w_in"]]) if comm else None)
    if comm:
        comm.landed(("w_in",), landed)
    return loss, grad_x, gw, gs


SMALL_NAMES = ("ffn1_norm", "mix_norm", "ffn2_norm", "final_norm", "b_gla_gate", "gla_head_norm", "dn_head_norm",
               "dn_a_log", "dn_dt_bias")
ROW4 = (("b_gla_gate", 512), ("gla_head_norm", 256), ("dn_head_norm", 128), ("dn_a_log", 8), ("dn_dt_bias", 8))


def _pack_small(d, loss=None):
    row4 = [d[n].reshape(-1) for n, _ in ROW4]
    row4.append(jnp.zeros((1,), F32) if loss is None else loss.reshape(1))
    row4 = jnp.concatenate(row4)
    row4 = jnp.pad(row4, (0, D - row4.shape[0]))
    rows = [d[n].reshape(-1) for n in SMALL_NAMES[:4]] + [row4]
    return jnp.concatenate([jnp.stack(rows), jnp.zeros((3, D), F32)], axis=0)


def _unpack_small(a, like):
    out = {n: a[i].reshape(like[n].shape) for i, n in enumerate(SMALL_NAMES[:4])}
    off = 0
    for n, w in ROW4:
        out[n] = a[4, off:off + w].reshape(like[n].shape)
        off += w
    return out, a[4, off]


WEIGHT_ORDER = ("ffn1_norm", "ffn1_w_gate", "ffn1_w_up", "ffn1_w_down", "mix_norm", "w_in", "w_gla_gate", "b_gla_gate",
                "conv_w", "dn_a_log", "dn_dt_bias", "gla_head_norm", "dn_head_norm", "w_out", "ffn2_norm",
                "ffn2_w_gate", "ffn2_w_up", "ffn2_w_down", "final_norm")
ADAM_ROWS = {"ffn1_w_gate": 256, "ffn1_w_up": 256, "ffn1_w_down": 176, "w_in": 128, "w_gla_gate": 16, "conv_w": 4,
             "w_out": 64, "ffn2_w_gate": 256, "ffn2_w_up": 256, "ffn2_w_down": 176}


def kernel(x, ffn1_norm, ffn1_w_gate, ffn1_w_up, ffn1_w_down, mix_norm, w_in, w_gla_gate, b_gla_gate, conv_w, dn_a_log, dn_dt_bias, gla_head_norm, dn_head_norm, w_out, ffn2_norm, ffn2_w_gate, ffn2_w_up, ffn2_w_down, final_norm, loss_target, m_ffn1_norm, m_ffn1_w_gate, m_ffn1_w_up, m_ffn1_w_down, m_mix_norm, m_w_in, m_w_gla_gate, m_b_gla_gate, m_conv_w, m_dn_a_log, m_dn_dt_bias, m_gla_head_norm, m_dn_head_norm, m_w_out, m_ffn2_norm, m_ffn2_w_gate, m_ffn2_w_up, m_ffn2_w_down, m_final_norm, v_ffn1_norm, v_ffn1_w_gate, v_ffn1_w_up, v_ffn1_w_down, v_mix_norm, v_w_in, v_w_gla_gate, v_b_gla_gate, v_conv_w, v_dn_a_log, v_dn_dt_bias, v_gla_head_norm, v_dn_head_norm, v_w_out, v_ffn2_norm, v_ffn2_w_gate, v_ffn2_w_up, v_ffn2_w_down, v_final_norm):
    given = dict(locals())
    wts = {n: given[n] for n in WEIGHT_ORDER}
    moms = {n: given["m_" + n] for n in WEIGHT_ORDER}
    vars_ = {n: given["v_" + n] for n in WEIGHT_ORDER}
    two_d = lambda a: a.reshape(a.shape[-2], a.shape[-1]) if a.ndim == 3 else a.reshape(1, -1)
    shard = {n: two_d(wts[n]) for n in SHARDED}

    gathered = _gather_weights([shard[n].astype(BF16) for n in BIG[:3]], [shard[n] for n in TINY], "gather_first")
    W = dict(zip(BIG[:3], gathered))
    P = {n: two_d(wts[n]) for n in SMALL_NAMES}
    for n, g in zip(TINY, gathered[3:]):
        P[n] = _join_cols(g)

    my_slot = 2 * lax.axis_index("x") + lax.axis_index("y")
    where = jnp.stack([lax.axis_index("c"), my_slot]).astype(jnp.int32)
    comm = _Comm(where, [shard[n].astype(BF16) for n in BIG[3:]])
    loss, grad_x, gw, gs = _local_step(x[0], loss_target[0], W, P, comm)
    halves = [_add_four(comm.pairs[n][1], comm.got[n], "rs_four_" + n) for n in BIG]
    other_halves = _rs_swap(halves, "rs_swap")

    tiny_rows = jnp.concatenate([gs["w_gla_gate"].reshape(8, D), gs["conv_w"].reshape(12, D), jnp.zeros((4, D), F32)])
    all_sum = _allsum_small(jnp.concatenate([_pack_small(gs, loss[0, 0]), tiny_rows]), "allsum_small")
    small_sum = all_sum[:8]
    small_g, loss_total = _unpack_small(small_sum, P)

    grads, delta, new_m, new_v = {}, {}, {}, {}
    for n, own, got in zip(BIG, halves, other_halves):
        res = _adamw_halves(shard[n], own, got, two_d(moms[n]), two_d(vars_[n]), ADAM_ROWS[n], "adamw_" + n)
        grads[n], delta[n], new_m[n], new_v[n] = (t.reshape(wts[n].shape) for t in res)
    for n, rows in (("w_gla_gate", all_sum[8:16]), ("conv_w", all_sum[16:28])):
        cols = shard[n].shape[1]
        grads[n] = lax.dynamic_slice_in_dim(rows.reshape(shard[n].shape[0], N_SHARD * cols), my_slot * cols, cols, axis=1)
        d, m_, v_ = _adamw(shard[n], grads[n], two_d(moms[n]), two_d(vars_[n]), ADAM_ROWS[n], "adamw_" + n)
        delta[n], new_m[n], new_v[n] = (t.reshape(wts[n].shape) for t in (d, m_, v_))
    pk = lambda src: _pack_small({n: two_d(src[n]) for n in SMALL_NAMES})
    sd, sm_, sv_ = _adamw(pk(wts), small_sum, pk(moms), pk(vars_), 8, "adamw_small")
    for res, dst in ((sd, delta), (sm_, new_m), (sv_, new_v)):
        u, _ = _unpack_small(res, wts)
        dst.update(u)
    grad_w = {n: grads[n].reshape(wts[n].shape) for n in SHARDED}
    grad_w.update({n: small_g[n].reshape(wts[n].shape) for n in SMALL_NAMES})
    return (loss_total, grad_x[None], *[grad_w[n] for n in WEIGHT_ORDER], *[delta[n] for n in WEIGHT_ORDER],
            *[new_m[n] for n in WEIGHT_ORDER], *[new_v[n] for n in WEIGHT_ORDER])
```

```python
import functools
import math

import numpy as np
import jax
import jax.numpy as jnp
from jax import lax
from jax.experimental import pallas as pl
from jax.experimental.pallas import tpu as pltpu

F32 = jnp.float32
BF16 = jnp.bfloat16
HI = lax.Precision.HIGH
MESH = pl.DeviceIdType.MESH
ANY = pl.BlockSpec(memory_space=pl.ANY)

EPS = 1e-6
D = 1024
DFF = 2816
FFN_RES = 0.5
GLA_H, GLA_DK, GLA_DV, GLA_RANK, GLA_TAU = 4, 128, 256, 16, 16.0
DN_H, DN_DK, DN_DV = 8, 128, 128
CONV_K = 4
CHUNK = 64
N_SHARD = 4
FF_CUT = DFF // N_SHARD
ADAM_LR, ADAM_B1, ADAM_B2, ADAM_EPS, ADAM_WD, ADAM_STEP = 0.001, 0.9, 0.999, 1e-08, 0.01, 10

IN_SIZES = (512, 512, 1024, 1024, 16, 1024, 1024, 1024, 1024, 8, 8, 1024, 1024)
IN_OFF = tuple(int(v) for v in np.cumsum((0,) + IN_SIZES))
D_IN = IN_OFF[-1]
BIG_COLS = 9216
SMALL = 128
PIECES = (512, 512, 1024, 1024, 3072, 1024, 1024, 1024)

VMEM_LIMIT = 56 * 1024 * 1024
ROW_BLK = 256
BIG_ROW_BLK = 512
ATT_BLK = 256
GDN_BLK = 128
GDN_FWD_BLK = 256
GDN_HEADS = 8


def _cp(*sem):
    return pltpu.CompilerParams(dimension_semantics=sem, vmem_limit_bytes=VMEM_LIMIT)


def _sigmoid(x):
    return 1.0 / (1.0 + jnp.exp(-x))


def _softplus(x):
    return jnp.maximum(x, 0.0) + jnp.log(1.0 + jnp.exp(-jnp.abs(x)))


def _log_sigmoid(x):
    return jnp.minimum(x, 0.0) - jnp.log(1.0 + jnp.exp(-jnp.abs(x)))


def _dot(a, b, prec=None):
    return jnp.dot(a, b, preferred_element_type=F32, precision=prec)


def _dot_nt(a, b, prec=None):
    return lax.dot_general(a, b, (((1,), (1,)), ((), ())), preferred_element_type=F32, precision=prec)


def _dot_tn(a, b, prec=None):
    return lax.dot_general(a, b, (((0,), (0,)), ((), ())), preferred_element_type=F32, precision=prec)


def _b(x):
    return x.astype(BF16)


def _iota2(n, m, axis):
    return lax.broadcasted_iota(jnp.int32, (n, m), axis)


def _load_weights(pairs, sem):
    copies = [pltpu.make_async_copy(s, d, sem.at[i]) for i, (s, d) in enumerate(pairs)]
    for c in copies:
        c.start()
    for c in copies:
        c.wait()


def _ffn_fwd(h, nw, wg, wu, wd, name, carry=None):
    T = h.shape[0]
    tm = min(BIG_ROW_BLK, T)
    ex_in, ex_args, ex_out, ex_shape, ex_sems = _carry_specs(carry)

    def body(h_ref, nw_ref, wg_hbm, wu_hbm, wd_hbm, ho_ref, n_ref, g_ref, u_ref, wg_v, wu_v, wd_v, sem):
        @pl.when(pl.program_id(0) == 0)
        def _():
            _load_weights(((wg_hbm, wg_v), (wu_hbm, wu_v), (wd_hbm, wd_v)), sem)

        x = h_ref[...]
        r = lax.rsqrt(jnp.mean(x * x, axis=-1, keepdims=True) + EPS)
        nb = _b((x * r) * nw_ref[...])
        n_ref[...] = nb
        acc = jnp.zeros((tm, D), F32)
        for s in range(N_SHARD):
            g = _dot(nb, wg_v[s])
            u = _dot(nb, wu_v[s])
            g_ref[s] = _b(g)
            u_ref[s] = _b(u)
            acc += _dot(_b(g * _sigmoid(g) * u), wd_v[s])
        ho_ref[...] = x + FFN_RES * acc

    row = lambda w: pl.BlockSpec((tm, w), lambda i: (i, 0))
    cut = pl.BlockSpec((N_SHARD, tm, FF_CUT), lambda i: (0, i, 0))
    return pl.pallas_call(
        _carry(carry, body, 5, 4, (T // tm,)), name=name, grid=(T // tm,),
        in_specs=[row(D), pl.BlockSpec((1, D), lambda i: (0, 0)), ANY, ANY, ANY] + ex_in,
        out_specs=[row(D), row(D), cut, cut] + ex_out,
        out_shape=[jax.ShapeDtypeStruct((T, D), F32), jax.ShapeDtypeStruct((T, D), BF16),
                   jax.ShapeDtypeStruct((N_SHARD, T, FF_CUT), BF16),
                   jax.ShapeDtypeStruct((N_SHARD, T, FF_CUT), BF16)] + ex_shape,
        scratch_shapes=[pltpu.VMEM((N_SHARD, D, FF_CUT), BF16), pltpu.VMEM((N_SHARD, D, FF_CUT), BF16),
                        pltpu.VMEM((N_SHARD, FF_CUT, D), BF16), pltpu.SemaphoreType.DMA((3,))] + ex_sems,
        compiler_params=_cp("arbitrary"),
    )(h, nw, wg, wu, wd, *ex_args)


def _ffn_bwd_hidden(dh, g, u, wd, name, carry=None):
    T = dh.shape[0]
    tm = min(BIG_ROW_BLK, T)
    ex_in, ex_args, ex_out, ex_shape, ex_sems = _carry_specs(carry)

    def body(dh_ref, g_ref, u_ref, wd_hbm, dg_ref, du_ref, a_ref, df_ref, wd_v, sem):
        @pl.when(pl.program_id(0) == 0)
        def _():
            _load_weights(((wd_hbm, wd_v),), sem)

        dfb = _b(FFN_RES * dh_ref[...])
        df_ref[...] = dfb
        for s in range(N_SHARD):
            da = _dot_nt(dfb, wd_v[s])
            gg = g_ref[s].astype(F32)
            uu = u_ref[s].astype(F32)
            sg = _sigmoid(gg)
            silu = gg * sg
            a_ref[s] = _b(silu * uu)
            dg_ref[s] = _b(da * uu * (sg * (1.0 + gg * (1.0 - sg))))
            du_ref[s] = _b(da * silu)

    row = pl.BlockSpec((tm, D), lambda i: (i, 0))
    cut = pl.BlockSpec((N_SHARD, tm, FF_CUT), lambda i: (0, i, 0))
    cut_shape = jax.ShapeDtypeStruct((N_SHARD, T, FF_CUT), BF16)
    return pl.pallas_call(
        _carry(carry, body, 4, 4, (T // tm,)), name=name, grid=(T // tm,),
        in_specs=[row, cut, cut, ANY] + ex_in,
        out_specs=[cut, cut, cut, row] + ex_out,
        out_shape=[cut_shape, cut_shape, cut_shape, jax.ShapeDtypeStruct((T, D), BF16)] + ex_shape,
        scratch_shapes=[pltpu.VMEM((N_SHARD, FF_CUT, D), BF16), pltpu.SemaphoreType.DMA((1,))] + ex_sems,
        compiler_params=_cp("arbitrary"),
    )(dh, g, u, wd, *ex_args)


def _ffn_bwd_input(dh, h, nw, dg, du, wg, wu, name, carry=None):
    T = h.shape[0]
    tm = min(BIG_ROW_BLK, T)
    ex_in, ex_args, ex_out, ex_shape, ex_sems = _carry_specs(carry)

    def body(dh_ref, h_ref, nw_ref, dg_ref, du_ref, wg_hbm, wu_hbm, dx_ref, dnw_ref, wg_v, wu_v, sem):
        @pl.when(pl.program_id(0) == 0)
        def _():
            _load_weights(((wg_hbm, wg_v), (wu_hbm, wu_v)), sem)
            dnw_ref[...] = jnp.zeros_like(dnw_ref)

        dn = jnp.zeros((tm, D), F32)
        for s in range(N_SHARD):
            dn += _dot_nt(dg_ref[s], wg_v[s]) + _dot_nt(du_ref[s], wu_v[s])
        x = h_ref[...]
        r = lax.rsqrt(jnp.mean(x * x, axis=-1, keepdims=True) + EPS)
        xhat = x * r
        dnw_ref[...] += jnp.sum(dn * xhat, axis=0, keepdims=True)
        dxhat = dn * nw_ref[...]
        dx_ref[...] = dh_ref[...] + r * (dxhat - xhat * jnp.mean(dxhat * xhat, axis=-1, keepdims=True))

    row = pl.BlockSpec((tm, D), lambda i: (i, 0))
    one = pl.BlockSpec((1, D), lambda i: (0, 0))
    cut = pl.BlockSpec((N_SHARD, tm, FF_CUT), lambda i: (0, i, 0))
    return pl.pallas_call(
        _carry(carry, body, 7, 2, (T // tm,)), name=name, grid=(T // tm,),
        in_specs=[row, row, one, cut, cut, ANY, ANY] + ex_in,
        out_specs=[row, one] + ex_out,
        out_shape=[jax.ShapeDtypeStruct((T, D), F32), jax.ShapeDtypeStruct((1, D), F32)] + ex_shape,
        scratch_shapes=[pltpu.VMEM((N_SHARD, D, FF_CUT), BF16), pltpu.VMEM((N_SHARD, D, FF_CUT), BF16),
                        pltpu.SemaphoreType.DMA((2,))] + ex_sems,
        compiler_params=_cp("arbitrary"),
    )(dh, h, nw, dg, du, wg, wu, *ex_args)


def _mm_tn(a, b, bm, bn, name, out_dtype=BF16, tk=2048):
    cuts = a.shape[0] if a.ndim == 3 else (b.shape[0] if b.ndim == 3 else None)
    T, M = a.shape[-2:]
    N = b.shape[-1]
    tk = min(tk, T)
    bm, bn = min(bm, M), min(bn, N)
    nk = T // tk

    def body(a_ref, b_ref, o_ref, acc_ref):
        k = pl.program_id(3)

        @pl.when(k == 0)
        def _():
            acc_ref[...] = jnp.zeros_like(acc_ref)

        av = a_ref[0] if a.ndim == 3 else a_ref[...]
        bv = b_ref[0] if b.ndim == 3 else b_ref[...]
        acc_ref[...] += _dot_tn(_b(av), _b(bv))

        @pl.when(k == nk - 1)
        def _():
            res = acc_ref[...].astype(out_dtype)
            if cuts is None:
                o_ref[...] = res
            else:
                o_ref[0] = res

    a_spec = (pl.BlockSpec((1, tk, bm), lambda s, i, j, k: (s, k, i)) if a.ndim == 3
              else pl.BlockSpec((tk, bm), lambda s, i, j, k: (k, i)))
    b_spec = (pl.BlockSpec((1, tk, bn), lambda s, i, j, k: (s, k, j)) if b.ndim == 3
              else pl.BlockSpec((tk, bn), lambda s, i, j, k: (k, j)))
    if cuts is None:
        o_spec, o_shape = pl.BlockSpec((bm, bn), lambda s, i, j, k: (i, j)), (M, N)
    else:
        o_spec, o_shape = pl.BlockSpec((1, bm, bn), lambda s, i, j, k: (s, i, j)), (cuts, M, N)
    return pl.pallas_call(
        body, name=name, grid=(cuts or 1, M // bm, N // bn, nk),
        in_specs=[a_spec, b_spec], out_specs=o_spec,
        out_shape=jax.ShapeDtypeStruct(o_shape, out_dtype),
        scratch_shapes=[pltpu.VMEM((bm, bn), F32)],
        compiler_params=_cp("parallel", "parallel", "parallel", "arbitrary"),
    )(a, b)


def _norm_proj(h, nw, wbig, wsmall, name, carry=None):
    T = h.shape[0]
    tm = min(512, T)
    tn = 1536
    ex_in, ex_args, ex_out, ex_shape, ex_sems = _carry_specs(carry)

    def body(h_ref, nw_ref, wb_hbm, ws_ref, pb_ref, ps_ref, n_ref, wb_v, sem):
        @pl.when(pl.program_id(0) == 0)
        def _():
            _load_weights(((wb_hbm, wb_v),), sem)

        x = h_ref[...]
        r = lax.rsqrt(jnp.mean(x * x, axis=-1, keepdims=True) + EPS)
        nb = _b((x * r) * nw_ref[...])
        n_ref[...] = nb
        ps_ref[...] = _dot(nb, ws_ref[...])
        for j in range(BIG_COLS // tn):
            pb_ref[:, j * tn:(j + 1) * tn] = _b(_dot(nb, wb_v[:, j * tn:(j + 1) * tn]))

    row = lambda w: pl.BlockSpec((tm, w), lambda i: (i, 0))
    return pl.pallas_call(
        _carry(carry, body, 4, 3, (T // tm,)), name=name, grid=(T // tm,),
        in_specs=[row(D), pl.BlockSpec((1, D), lambda i: (0, 0)), ANY, pl.BlockSpec((D, SMALL), lambda i: (0, 0))]
        + ex_in,
        out_specs=[row(BIG_COLS), row(SMALL), row(D)] + ex_out,
        out_shape=[jax.ShapeDtypeStruct((T, BIG_COLS), BF16), jax.ShapeDtypeStruct((T, SMALL), F32),
                   jax.ShapeDtypeStruct((T, D), BF16)] + ex_shape,
        scratch_shapes=[pltpu.VMEM((D, BIG_COLS), BF16), pltpu.SemaphoreType.DMA((1,))] + ex_sems,
        compiler_params=_cp("arbitrary"),
    )(h, nw, wbig, wsmall, *ex_args)


def _proj_bwd(dh, h, nw, pieces, dsmall, wbig, wsmall, name):
    T = h.shape[0]
    tm = min(BIG_ROW_BLK, T)
    offs = tuple(int(v) for v in np.cumsum((0,) + PIECES))

    def body(dh_ref, h_ref, nw_ref, *rest):
        p_refs = rest[:len(PIECES)]
        ds_ref, wb_hbm, ws_ref, dx_ref, dnw_ref, wb_v, sem = rest[len(PIECES):]

        @pl.when(pl.program_id(0) == 0)
        def _():
            _load_weights(((wb_hbm, wb_v),), sem)
            dnw_ref[...] = jnp.zeros_like(dnw_ref)

        dn = _dot_nt(_b(ds_ref[...]), ws_ref[...])
        for p_ref, lo, wdt in zip(p_refs, offs, PIECES):
            dn += _dot_nt(p_ref[...], wb_v[:, lo:lo + wdt])
        x = h_ref[...]
        r = lax.rsqrt(jnp.mean(x * x, axis=-1, keepdims=True) + EPS)
        xhat = x * r
        dnw_ref[...] += jnp.sum(dn * xhat, axis=0, keepdims=True)
        dxhat = dn * nw_ref[...]
        dx_ref[...] = dh_ref[...] + r * (dxhat - xhat * jnp.mean(dxhat * xhat, axis=-1, keepdims=True))

    row = lambda w: pl.BlockSpec((tm, w), lambda i: (i, 0))
    one = pl.BlockSpec((1, D), lambda i: (0, 0))
    return pl.pallas_call(
        body, name=name, grid=(T // tm,),
        in_specs=[row(D), row(D), one] + [row(w) for w in PIECES] + [row(SMALL), ANY, pl.BlockSpec((D, SMALL), lambda i: (0, 0))],
        out_specs=[row(D), one],
        out_shape=[jax.ShapeDtypeStruct((T, D), F32), jax.ShapeDtypeStruct((1, D), F32)],
        scratch_shapes=[pltpu.VMEM((D, BIG_COLS), BF16), pltpu.SemaphoreType.DMA((1,))],
        compiler_params=_cp("arbitrary"),
    )(dh, h, nw, *pieces, dsmall, wbig, wsmall)


def _gla_block(q_ref, k_ref, sm_ref, wg_ref, bg_ref, nc, tril):
    nbat = GLA_H * nc
    q = _heads_first(q_ref[...].astype(F32), nc, GLA_DK)
    k = _heads_first(k_ref[...].astype(F32), nc, GLA_DK)
    pre = _heads_first(_dot(sm_ref[...], wg_ref[...], HI) + bg_ref[...], nc, GLA_DK)
    la = _log_sigmoid(pre) * (1.0 / GLA_TAU)
    bc = _bmm(jnp.broadcast_to(tril, (nbat, CHUNK, CHUNK)), la, HI)
    bl = bc[:, CHUNK - 1:CHUNK, :]
    eb = jnp.exp(bc)
    enb = jnp.exp(-bc)
    ebl = jnp.exp(bl - bc)
    q_in = q * (GLA_DK ** -0.5) * eb
    k_out = k * enb
    k_st = k * ebl
    a_ch = jnp.exp(bl)
    return pre, eb, enb, ebl, q_in, k_out, k_st, a_ch


def _gla_specs(blk, idx):
    hk, hv = GLA_H * GLA_DK, GLA_H * GLA_DV
    return [pl.BlockSpec((blk, hk), lambda j: (idx(j), 0)),
            pl.BlockSpec((blk, hk), lambda j: (idx(j), 1)),
            pl.BlockSpec((blk, hv), lambda j: (idx(j), 1)),
            pl.BlockSpec((blk, SMALL), lambda j: (idx(j), 0)),
            pl.BlockSpec((SMALL, hk), lambda j: (0, 0)),
            pl.BlockSpec((1, hk), lambda j: (0, 0))]


def _gla_fwd(pbig, psmall, wgate, bgate, name):
    T = pbig.shape[0]
    blk = min(ATT_BLK, T)
    nc = blk // CHUNK

    def body(q_ref, k_ref, v_ref, sm_ref, wg_ref, bg_ref, o_ref, ss_ref, st_ref):
        @pl.when(pl.program_id(0) == 0)
        def _():
            st_ref[...] = jnp.zeros_like(st_ref)

        causal = _iota2(CHUNK, CHUNK, 0) >= _iota2(CHUNK, CHUNK, 1)
        _, _, _, _, q_in, k_out, k_st, a_ch = _gla_block(q_ref, k_ref, sm_ref, wg_ref, bg_ref, nc, causal.astype(F32))
        v = _heads_first(v_ref[...], nc, GLA_DV)
        qb = _b(q_in)
        sc = jnp.where(causal, _bmm_nt(qb, _b(k_out)), 0.0)
        kv = _bmm_tn(v, _b(k_st))
        before = [None] * (GLA_H * nc)
        for i in range(GLA_H):
            st = st_ref[i]
            for c in range(nc):
                n = i * nc + c
                before[n] = st
                st = st * a_ch[n] + kv[n]
            st_ref[i] = st
        states = jnp.stack(before)
        ss_ref[...] = states.reshape(GLA_H, nc, GLA_DV, GLA_DK)
        o_ref[...] = _heads_last(_bmm(_b(sc), v) + _bmm_nt(qb, _b(states)), nc)

    return pl.pallas_call(
        body, name=name, grid=(T // blk,),
        in_specs=_gla_specs(blk, lambda j: j),
        out_specs=[pl.BlockSpec((blk, GLA_H * GLA_DV), lambda j: (j, 0)),
                   pl.BlockSpec((GLA_H, nc, GLA_DV, GLA_DK), lambda j: (0, j, 0, 0))],
        out_shape=[jax.ShapeDtypeStruct((T, GLA_H * GLA_DV), F32),
                   jax.ShapeDtypeStruct((GLA_H, T // CHUNK, GLA_DV, GLA_DK), F32)],
        scratch_shapes=[pltpu.VMEM((GLA_H, GLA_DV, GLA_DK), F32)],
        compiler_params=_cp("arbitrary"),
    )(pbig, pbig, pbig, psmall, wgate, bgate)


def _gla_bwd(pbig, psmall, wgate, bgate, states, do, name):
    T = pbig.shape[0]
    blk = min(ATT_BLK, T)
    nc = blk // CHUNK
    nb = T // blk
    nbat = GLA_H * nc

    def body(q_ref, k_ref, v_ref, sm_ref, wg_ref, bg_ref, ss_ref, do_ref, dq_ref, dk_ref, dv_ref, dpre_ref, dst_ref):
        @pl.when(pl.program_id(0) == 0)
        def _():
            dst_ref[...] = jnp.zeros_like(dst_ref)

        causal = _iota2(CHUNK, CHUNK, 0) >= _iota2(CHUNK, CHUNK, 1)
        triu = (_iota2(CHUNK, CHUNK, 0) <= _iota2(CHUNK, CHUNK, 1)).astype(F32)
        pre, eb, enb, ebl, q_in, k_out, k_st, a_ch = _gla_block(q_ref, k_ref, sm_ref, wg_ref, bg_ref, nc,
                                                                causal.astype(F32))
        v = _heads_first(v_ref[...], nc, GLA_DV)
        dob = _b(_heads_first(do_ref[...], nc, GLA_DV))
        st = ss_ref[...].reshape(nbat, GLA_DV, GLA_DK)
        qb, kob, kstb = _b(q_in), _b(k_out), _b(k_st)
        qdo = _bmm_tn(dob, qb)
        after = [None] * nbat
        for i in range(GLA_H):
            dst = dst_ref[i]
            for c in range(nc - 1, -1, -1):
                n = i * nc + c
                after[n] = dst
                dst = dst * a_ch[n] + qdo[n]
            dst_ref[i] = dst
        dsa = jnp.stack(after)
        dsb = _b(dsa)
        sc = jnp.where(causal, _bmm_nt(qb, kob), 0.0)
        dsc = _b(jnp.where(causal, _bmm_nt(dob, v), 0.0))
        dq_in = _bmm(dob, _b(st)) + _bmm(dsc, kob)
        dk_out = _bmm_tn(dsc, qb)
        dk_st = _bmm(v, dsb)
        dv_ref[...] = _b(_heads_last(_bmm_tn(_b(sc), dob) + _bmm_nt(kstb, dsb), nc))
        da_ch = jnp.sum(st * dsa, axis=1, keepdims=True)
        tk = dk_st * k_st
        db = dq_in * q_in - dk_out * k_out - tk
        db_last = jnp.sum(tk, axis=1, keepdims=True) + da_ch * a_ch
        dq_ref[...] = _b(_heads_last(dq_in * (GLA_DK ** -0.5) * eb, nc))
        dk_ref[...] = _b(_heads_last(dk_out * enb + dk_st * ebl, nc))
        dla = _bmm(jnp.broadcast_to(triu, (nbat, CHUNK, CHUNK)), db, HI) + db_last
        dpre_ref[...] = _heads_last(dla * (1.0 / GLA_TAU) * _sigmoid(-pre), nc)

    r = lambda j: nb - 1 - j
    hk, hv = GLA_H * GLA_DK, GLA_H * GLA_DV
    return pl.pallas_call(
        body, name=name, grid=(nb,),
        in_specs=_gla_specs(blk, r) + [pl.BlockSpec((GLA_H, nc, GLA_DV, GLA_DK), lambda j: (0, r(j), 0, 0)),
                                      pl.BlockSpec((blk, hv), lambda j: (r(j), 0))],
        out_specs=[pl.BlockSpec((blk, hk), lambda j: (r(j), 0)), pl.BlockSpec((blk, hk), lambda j: (r(j), 0)),
                   pl.BlockSpec((blk, hv), lambda j: (r(j), 0)), pl.BlockSpec((blk, hk), lambda j: (r(j), 0))],
        out_shape=[jax.ShapeDtypeStruct((T, hk), BF16), jax.ShapeDtypeStruct((T, hk), BF16),
                   jax.ShapeDtypeStruct((T, hv), BF16), jax.ShapeDtypeStruct((T, hk), F32)],
        scratch_shapes=[pltpu.VMEM((GLA_H, GLA_DV, GLA_DK), F32)],
        compiler_params=_cp("arbitrary"),
    )(pbig, pbig, pbig, psmall, wgate, bgate, states, do)


def _gla_gate_bwd(dpre, psmall, wgate, dsm, name):
    T = dpre.shape[0]
    tm = min(512, T)
    W = GLA_H * GLA_DK
    ngrp = dsm.shape[0]

    def body(dp_ref, sm_ref, wg_ref, dsm_ref, ds_ref, dw_ref, db_ref):
        @pl.when(pl.program_id(0) == 0)
        def _():
            dw_ref[...] = jnp.zeros_like(dw_ref)
            db_ref[...] = jnp.zeros_like(db_ref)

        dp = dp_ref[...]
        ds = _dot_nt(dp, wg_ref[...], HI)
        for i in range(ngrp):
            ds += dsm_ref[i]
        ds_ref[...] = ds
        dw_ref[...] += _dot_tn(sm_ref[...], dp, HI)
        db_ref[...] += jnp.sum(dp, axis=0, keepdims=True)

    return pl.pallas_call(
        body, name=name, grid=(T // tm,),
        in_specs=[pl.BlockSpec((tm, W), lambda i: (i, 0)), pl.BlockSpec((tm, SMALL), lambda i: (i, 0)),
                  pl.BlockSpec((SMALL, W), lambda i: (0, 0)), pl.BlockSpec((ngrp, tm, SMALL), lambda i: (0, i, 0))],
        out_specs=[pl.BlockSpec((tm, SMALL), lambda i: (i, 0)), pl.BlockSpec((SMALL, W), lambda i: (0, 0)),
                   pl.BlockSpec((1, W), lambda i: (0, 0))],
        out_shape=[jax.ShapeDtypeStruct((T, SMALL), F32), jax.ShapeDtypeStruct((SMALL, W), F32),
                   jax.ShapeDtypeStruct((1, W), F32)],
        compiler_params=_cp("arbitrary"),
    )(dpre, psmall, wgate, dsm)


CONV_C = 3 * 1024
CONV_BLK = 256


def _conv_fwd(pbig, cw8, name):
    T = pbig.shape[0]
    blk = min(CONV_BLK, T)

    def body(x_ref, w_ref, c_ref, prev_ref):
        @pl.when(pl.program_id(0) == 0)
        def _():
            prev_ref[...] = jnp.zeros_like(prev_ref)

        x = x_ref[...].astype(F32)
        prev = prev_ref[...]
        row8 = _iota2(8, CONV_C, 0)
        acc = x * w_ref[CONV_K - 1:CONV_K, :]
        for s in range(1, CONV_K):
            xs = pltpu.roll(x, s, 0)
            top = jnp.where(row8 < s, pltpu.roll(prev, s, 0), xs[:8])
            xs = jnp.concatenate([top, xs[8:]], axis=0)
            acc += xs * w_ref[CONV_K - 1 - s:CONV_K - s, :]
        c_ref[...] = _b(acc)
        prev_ref[...] = x[blk - 8:]

    return pl.pallas_call(
        body, name=name, grid=(T // blk,),
        in_specs=[pl.BlockSpec((blk, CONV_C), lambda i: (i, 1)), pl.BlockSpec((8, CONV_C), lambda i: (0, 0))],
        out_specs=pl.BlockSpec((blk, CONV_C), lambda i: (i, 0)),
        out_shape=jax.ShapeDtypeStruct((T, CONV_C), BF16),
        scratch_shapes=[pltpu.VMEM((8, CONV_C), F32)],
        compiler_params=_cp("arbitrary"),
    )(pbig, cw8)


def _conv_bwd(dcq, dck, dcv, pbig, cw8, name):
    T = pbig.shape[0]
    blk = min(CONV_BLK, T)
    nb = T // blk

    def body(dq_ref, dk_ref, dv_ref, x_ref, w_ref, dx_ref, dw_ref, nxt_ref):
        @pl.when(pl.program_id(0) == 0)
        def _():
            nxt_ref[...] = jnp.zeros_like(nxt_ref)
            dw_ref[...] = jnp.zeros_like(dw_ref)

        dc = jnp.concatenate([dq_ref[...], dk_ref[...], dv_ref[...]], axis=1).astype(F32)
        x = x_ref[...].astype(F32)
        nxt = nxt_ref[...]
        row8 = _iota2(8, CONV_C, 0)
        acc = dc * w_ref[CONV_K - 1:CONV_K, :]
        dws = [jnp.sum(dc * x, axis=0, keepdims=True)]
        for s in range(1, CONV_K):
            ds = pltpu.roll(dc, blk - s, 0)
            bot = jnp.where(row8 >= 8 - s, pltpu.roll(nxt, 8 - s, 0), ds[blk - 8:])
            ds = jnp.concatenate([ds[:blk - 8], bot], axis=0)
            acc += ds * w_ref[CONV_K - 1 - s:CONV_K - s, :]
            dws.append(jnp.sum(ds * x, axis=0, keepdims=True))
        dx_ref[...] = _b(acc)
        dw_ref[...] += jnp.concatenate(dws[::-1] + [jnp.zeros((8 - CONV_K, CONV_C), F32)], axis=0)
        nxt_ref[...] = dc[:8]

    part = pl.BlockSpec((blk, 1024), lambda i: (nb - 1 - i, 0))
    return pl.pallas_call(
        body, name=name, grid=(nb,),
        in_specs=[part, part, part, pl.BlockSpec((blk, CONV_C), lambda i: (nb - 1 - i, 1)),
                  pl.BlockSpec((8, CONV_C), lambda i: (0, 0))],
        out_specs=[pl.BlockSpec((blk, CONV_C), lambda i: (nb - 1 - i, 0)), pl.BlockSpec((8, CONV_C), lambda i: (0, 0))],
        out_shape=[jax.ShapeDtypeStruct((T, CONV_C), BF16), jax.ShapeDtypeStruct((8, CONV_C), F32)],
        scratch_shapes=[pltpu.VMEM((8, CONV_C), F32)],
        compiler_params=_cp("arbitrary"),
    )(dcq, dck, dcv, pbig, cw8)


def _col(x, lane):
    sel = _iota2(x.shape[0], x.shape[1], 1) == lane
    return jnp.broadcast_to(jnp.sum(jnp.where(sel, x, 0.0), axis=1, keepdims=True), x.shape)


def _bmm(a, b, prec=None):
    return jnp.einsum("bij,bjk->bik", a, b, preferred_element_type=F32, precision=prec)


def _bmm_nt(a, b, prec=None):
    return jnp.einsum("bij,bkj->bik", a, b, preferred_element_type=F32, precision=prec)


def _bmm_tn(a, b, prec=None):
    return jnp.einsum("bji,bjk->bik", a, b, preferred_element_type=F32, precision=prec)


def _unit_lower_inverse(low):
    eye = (_iota2(CHUNK, CHUNK, 0) == _iota2(CHUNK, CHUNK, 1)).astype(F32)
    xk = -low
    inv = eye + xk
    for _ in range(5):
        xb = _b(xk)
        xk = _bmm(xb, xb)
        inv = inv + _bmm(_b(inv), _b(xk))
    resid = eye - _bmm(eye + low, inv, HI)
    return inv + _bmm(inv, resid, HI)


def _heads_first(x, nc, w=128):
    hb = x.shape[1] // w
    return jnp.concatenate([x[:, i * w:(i + 1) * w].reshape(nc, CHUNK, w) for i in range(hb)], axis=0)


def _heads_last(x, nc):
    hb = x.shape[0] // nc
    return jnp.concatenate([x[i * nc:(i + 1) * nc].reshape(nc * CHUNK, x.shape[2]) for i in range(hb)], axis=1)


def _gdn_block(cq_ref, ck_ref, cv_ref, sm_ref, par_ref, h0, hb, nc, masks, solved=None):
    causal, strict, tril, eye = masks
    nbat = hb * nc
    cq = _heads_first(cq_ref[...].astype(F32), nc)
    ck = _heads_first(ck_ref[...].astype(F32), nc)
    cv = _heads_first(cv_ref[...].astype(F32), nc)
    sq, sk, sv = _sigmoid(cq), _sigmoid(ck), _sigmoid(cv)
    q, k, v = cq * sq, ck * sk, cv * sv
    rq = lax.rsqrt(jnp.sum(q * q, axis=-1, keepdims=True) + EPS)
    rk = lax.rsqrt(jnp.sum(k * k, axis=-1, keepdims=True) + EPS)
    qh, kn = q * rq, k * rk
    qn = qh * (DN_DK ** -0.5)
    sm = sm_ref[...]
    per_head = lambda fn: jnp.concatenate([fn(i) for i in range(hb)], axis=0)
    braw = per_head(lambda i: _col(sm, GLA_RANK + h0 + i).reshape(nc, CHUNK, 128))
    araw = per_head(lambda i: _col(sm, GLA_RANK + DN_H + h0 + i).reshape(nc, CHUNK, 128))
    ea = per_head(lambda i: jnp.broadcast_to(jnp.exp(par_ref[i, 0:1, :])[None], (nc, 1, 128)))
    bias = per_head(lambda i: jnp.broadcast_to(par_ref[i, 1:2, :][None], (nc, 1, 128)))
    beta = _sigmoid(braw)
    sp_arg = araw + bias
    g = -ea * _softplus(sp_arg)
    G = _bmm(jnp.broadcast_to(tril, (nbat, CHUNK, CHUNK)), g, HI)
    gc = G[:, :, :CHUNK]
    grow = jnp.sum(eye * gc, axis=1, keepdims=True)
    decay = jnp.exp(jnp.where(causal, gc - grow, -1e30))
    kb = kn * beta
    A = _bmm_nt(_b(kb), _b(kn))
    eG = jnp.exp(G)
    gl = G[:, CHUNK - 1:CHUNK, :]
    eGl = jnp.exp(gl - G)
    g_ch = jnp.exp(gl)
    rv = v * beta
    rkk = kb * eG
    if solved is None:
        tinv_b = _b(_unit_lower_inverse(jnp.where(strict, A * decay, 0.0)))
        u = _bmm(tinv_b, _b(rv))
        w = _b(_bmm(tinv_b, _b(rkk)))
    else:
        tinv_b, u, w = solved
    B = _bmm_nt(_b(qn), _b(kn))
    qk = jnp.where(causal, B * decay, 0.0)
    q_dec = qn * eG
    k_st = kn * eGl
    return dict(cq=cq, ck=ck, cv=cv, sq=sq, sk=sk, sv=sv, q=q, k=k, v=v, rq=rq, rk=rk, qh=qh, kn=kn, qn=qn,
                beta=beta, ea=ea, sp_arg=sp_arg, g=g, G=G, decay=decay, kb=kb, A=A, tinv_b=tinv_b, eG=eG, eGl=eGl,
                g_ch=g_ch, rv=rv, rkk=rkk, u=u, w=w, B=B, qk=qk, q_dec=q_dec, k_st=k_st)


def _gdn_masks():
    r, c = _iota2(CHUNK, CHUNK, 0), _iota2(CHUNK, CHUNK, 1)
    return r >= c, r > c, (r >= c).astype(F32), (r == c).astype(F32)


def _gdn_specs(blk, hb, idx):
    ng = DN_H // hb
    return [pl.BlockSpec((blk, hb * DN_DK), lambda h, j: (idx(j), h)),
            pl.BlockSpec((blk, hb * DN_DK), lambda h, j: (idx(j), ng + h)),
            pl.BlockSpec((blk, hb * DN_DV), lambda h, j: (idx(j), 2 * ng + h)),
            pl.BlockSpec((blk, SMALL), lambda h, j: (idx(j), 0)),
            pl.BlockSpec((hb, 8, 128), lambda h, j: (h, 0, 0))]


def _gdn_solved_specs(blk, hb, idx):
    nc = blk // CHUNK
    spec = lambda w: pl.BlockSpec((hb, nc, CHUNK, w), lambda h, j: (h, idx(j), 0, 0))
    return [spec(CHUNK), spec(DN_DV), spec(DN_DK)]


def _gdn_fwd(conv, psmall, par, name):
    T = conv.shape[0]
    blk = min(GDN_FWD_BLK, T)
    nc = blk // CHUNK
    hb = GDN_HEADS
    N = T // CHUNK

    def body(cq_ref, ck_ref, cv_ref, sm_ref, par_ref, o_ref, ss_ref, ti_ref, u_ref, w_ref, s_ref):
        @pl.when(pl.program_id(1) == 0)
        def _():
            s_ref[...] = jnp.zeros_like(s_ref)

        f = _gdn_block(cq_ref, ck_ref, cv_ref, sm_ref, par_ref, pl.program_id(0) * hb, hb, nc, _gdn_masks())
        ti_ref[...] = f["tinv_b"].reshape(hb, nc, CHUNK, CHUNK)
        u_ref[...] = f["u"].reshape(hb, nc, CHUNK, DN_DV)
        w_ref[...] = f["w"].reshape(hb, nc, CHUNK, DN_DK)
        wb, ub, kstb, qkb = f["w"], _b(f["u"]), _b(f["k_st"]), _b(f["qk"])
        mix = _b(_bmm_tn(kstb, wb))
        add = _bmm_tn(kstb, ub)
        q_eff = _b(f["q_dec"] - _bmm(qkb, wb))
        before = [None] * (hb * nc)
        S = [s_ref[i] for i in range(hb)]
        for c in range(nc):
            for i in range(hb):
                n = i * nc + c
                before[n] = S[i]
                S[i] = S[i] * f["g_ch"][n] - _dot(mix[n], _b(S[i])) + add[n]
        for i in range(hb):
            s_ref[i] = S[i]
        states = jnp.stack(before)
        ss_ref[...] = states.reshape(hb, nc, DN_DK, DN_DV)
        o_ref[...] = _heads_last(_bmm(qkb, ub) + _bmm(q_eff, _b(states)), nc)

    return pl.pallas_call(
        body, name=name, grid=(DN_H // hb, T // blk),
        in_specs=_gdn_specs(blk, hb, lambda j: j),
        out_specs=[pl.BlockSpec((blk, hb * DN_DV), lambda h, j: (j, h)),
                   pl.BlockSpec((hb, nc, DN_DK, DN_DV), lambda h, j: (h, j, 0, 0))]
        + _gdn_solved_specs(blk, hb, lambda j: j),
        out_shape=[jax.ShapeDtypeStruct((T, DN_H * DN_DV), F32), jax.ShapeDtypeStruct((DN_H, N, DN_DK, DN_DV), F32),
                   jax.ShapeDtypeStruct((DN_H, N, CHUNK, CHUNK), BF16), jax.ShapeDtypeStruct((DN_H, N, CHUNK, DN_DV), F32),
                   jax.ShapeDtypeStruct((DN_H, N, CHUNK, DN_DK), BF16)],
        scratch_shapes=[pltpu.VMEM((hb, DN_DK, DN_DV), F32)],
        compiler_params=_cp("parallel", "arbitrary"),
    )(conv, conv, conv, psmall, par)


def _gdn_bwd(conv, psmall, par, states, do, solved, name, carry=None):
    ex_in, ex_args, ex_out, ex_shape, ex_sems = _carry_specs(carry)
    T = conv.shape[0]
    blk = min(GDN_BLK, T)
    nc = blk // CHUNK
    nb = T // blk
    hb = GDN_HEADS
    nbat = hb * nc
    rsum = lambda x: jnp.sum(x, axis=-1, keepdims=True)

    def body(cq_ref, ck_ref, cv_ref, sm_ref, par_ref, ss_ref, do_ref, ti_ref, u_ref, w_ref,
             dcq_ref, dck_ref, dcv_ref, dsm_ref, dpar_ref, ds_ref):
        @pl.when(pl.program_id(1) == 0)
        def _():
            ds_ref[...] = jnp.zeros_like(ds_ref)
            dpar_ref[...] = jnp.zeros_like(dpar_ref)

        masks = _gdn_masks()
        causal, strict, tril, eye = masks
        triu = (_iota2(CHUNK, CHUNK, 0) <= _iota2(CHUNK, CHUNK, 1)).astype(F32)
        lane = _iota2(CHUNK, 128, 1)
        last_row = _iota2(CHUNK, 128, 0) == CHUNK - 1
        h0 = pl.program_id(0) * hb
        solved = (ti_ref[...].reshape(nbat, CHUNK, CHUNK), u_ref[...].reshape(nbat, CHUNK, DN_DV),
                  w_ref[...].reshape(nbat, CHUNK, DN_DK))
        f = _gdn_block(cq_ref, ck_ref, cv_ref, sm_ref, par_ref, h0, hb, nc, masks, solved)
        S = ss_ref[...].reshape(nbat, DN_DK, DN_DV)
        Sb = _b(S)
        do_ = _b(_heads_first(do_ref[...], nc))
        wb, qdb, kstb, qkb = _b(f["w"]), _b(f["q_dec"]), _b(f["k_st"]), _b(f["qk"])
        vnb = _b(f["u"] - _bmm(wb, Sb))
        dvn0 = _bmm_tn(qkb, do_)
        qdo = _bmm_tn(qdb, do_)
        dS = [ds_ref[i] for i in range(hb)]
        after = [None] * nbat
        for c in range(nc - 1, -1, -1):
            for i in range(hb):
                n = i * nc + c
                after[n] = dS[i]
                dvn_c = _b(dvn0[n] + _dot(kstb[n], _b(dS[i])))
                dS[i] = dS[i] * f["g_ch"][n] + qdo[n] - _dot_tn(wb[n], dvn_c)
        for i in range(hb):
            ds_ref[i] = dS[i]
        dSa = jnp.stack(after)
        dSb = _b(dSa)
        dvn = dvn0 + _bmm(kstb, dSb)
        dvnb = _b(dvn)
        dq_dec = _bmm_nt(do_, Sb)
        dqk = jnp.where(causal, _bmm_nt(do_, vnb), 0.0)
        dk_st = _bmm_nt(vnb, dSb)
        dg_ch = jnp.sum(rsum(S * dSa), axis=1, keepdims=True)
        dw = -_bmm_nt(dvnb, Sb)
        drv = _bmm_tn(f["tinv_b"], dvnb)
        drk = _bmm_tn(f["tinv_b"], _b(dw))
        dlow = jnp.where(strict, -(_bmm_nt(_b(drv), _b(f["u"])) + _bmm_nt(_b(drk), wb)), 0.0)
        dv = drv * f["beta"]
        dbeta = rsum(drv * f["v"])
        dkb = drk * f["eG"]
        dG = rsum(drk * f["rkk"])
        dA = dlow * f["decay"]
        ddec = dlow * f["A"]
        dkb += _bmm(_b(dA), _b(f["kn"]))
        dkn = _bmm_tn(_b(dA), _b(f["kb"]))
        dB = dqk * f["decay"]
        ddec += dqk * f["B"]
        dqn = _bmm(_b(dB), _b(f["kn"]))
        dkn += _bmm_tn(_b(dB), _b(f["qn"]))
        dD = ddec * f["decay"]
        dG += rsum(dD) - rsum(eye * jnp.sum(dD, axis=1, keepdims=True))
        dqn += dq_dec * f["eG"]
        dG += rsum(dq_dec * f["q_dec"])
        dkn += dk_st * f["eGl"]
        tks = rsum(dk_st * f["k_st"])
        dG -= tks
        dG_last = jnp.sum(tks, axis=1, keepdims=True) + dg_ch * f["g_ch"][:, :, :1]
        dkn += dkb * f["beta"]
        dbeta += rsum(dkb * f["kn"])
        dGf = jnp.broadcast_to(dG, (nbat, CHUNK, 128)) + jnp.where(last_row, dG_last, 0.0)
        dg = _bmm(jnp.broadcast_to(triu, (nbat, CHUNK, CHUNK)), dGf, HI)
        dbraw = dbeta * f["beta"][:, :, :1] * (1.0 - f["beta"][:, :, :1])
        daraw = dg * (-f["ea"]) * _sigmoid(f["sp_arg"])
        both = lambda t: jnp.sum(jnp.sum(t, axis=1, keepdims=True), axis=0)
        dgg = dg * f["g"]
        dsm = jnp.zeros((nc, CHUNK, SMALL), F32)
        for i in range(hb):
            mine = slice(i * nc, (i + 1) * nc)
            dsm += (jnp.where(lane == GLA_RANK + h0 + i, dbraw[mine], 0.0)
                    + jnp.where(lane == GLA_RANK + DN_H + h0 + i, daraw[mine], 0.0))
            dpar = jnp.where(lane[:1] == 0, both(dgg[mine]), jnp.where(lane[:1] == 1, both(daraw[mine]), 0.0))
            dpar_ref[i] += jnp.broadcast_to(dpar, (8, 128))
        dsm_ref[0] = dsm.reshape(blk, SMALL)
        dqh = dqn * (DN_DK ** -0.5)
        dq = f["rq"] * (dqh - f["qh"] * rsum(dqh * f["qh"]))
        dk = f["rk"] * (dkn - f["kn"] * rsum(dkn * f["kn"]))
        dsilu = lambda x, s: s * (1.0 + x * (1.0 - s))
        dcq_ref[...] = _b(_heads_last(dq * dsilu(f["cq"], f["sq"]), nc))
        dck_ref[...] = _b(_heads_last(dk * dsilu(f["ck"], f["sk"]), nc))
        dcv_ref[...] = _b(_heads_last(dv * dsilu(f["cv"], f["sv"]), nc))

    r = lambda j: nb - 1 - j
    out_blk = pl.BlockSpec((blk, hb * DN_DK), lambda h, j: (r(j), h))
    grid = (DN_H // hb, nb)
    return pl.pallas_call(
        _carry(carry, body, 10, 5, grid), name=name, grid=grid,
        in_specs=_gdn_specs(blk, hb, r) + [pl.BlockSpec((hb, nc, DN_DK, DN_DV), lambda h, j: (h, r(j), 0, 0)),
                                          pl.BlockSpec((blk, hb * DN_DV), lambda h, j: (r(j), h))]
        + _gdn_solved_specs(blk, hb, r) + ex_in,
        out_specs=[out_blk, out_blk, out_blk, pl.BlockSpec((1, blk, SMALL), lambda h, j: (h, r(j), 0)),
                   pl.BlockSpec((hb, 8, 128), lambda h, j: (h, 0, 0))] + ex_out,
        out_shape=[jax.ShapeDtypeStruct((T, DN_H * DN_DK), BF16)] * 3 + [
            jax.ShapeDtypeStruct((DN_H // hb, T, SMALL), F32), jax.ShapeDtypeStruct((DN_H, 8, 128), F32)] + ex_shape,
        scratch_shapes=[pltpu.VMEM((hb, DN_DK, DN_DV), F32)] + ex_sems,
        compiler_params=_cp("arbitrary", "arbitrary"),
    )(conv, conv, conv, psmall, par, states, do, *solved, *ex_args)


def _head_norm(o, w, dv):
    outs, rs = [], []
    for i in range(o.shape[1] // dv):
        oh = o[:, i * dv:(i + 1) * dv]
        r = lax.rsqrt(jnp.mean(oh * oh, axis=-1, keepdims=True) + EPS)
        outs.append(oh * r)
        rs.append(r)
    return outs, rs


def _merge_specs(tm):
    col = lambda c: pl.BlockSpec((tm, D), lambda i: (i, c))
    return [col(0), col(0), col(2), col(6), col(7), col(8),
            pl.BlockSpec((1, GLA_DV), lambda i: (0, 0)), pl.BlockSpec((1, DN_DV), lambda i: (0, 0)),
            pl.BlockSpec((D, D), lambda i: (0, 0))]


def _merge_fwd(h, oa, ob, pbig, gla_hn, dn_hn, wout, name):
    T = h.shape[0]
    tm = min(ROW_BLK, T)

    def body(h_ref, oa_ref, ob_ref, gr_ref, dg_ref, ma_ref, mb_ref, wa_ref, wb_ref, wo_ref, ho_ref, y_ref):
        na, _ = _head_norm(oa_ref[...], wa_ref[...], GLA_DV)
        nbs, _ = _head_norm(ob_ref[...], wb_ref[...], DN_DV)
        hna = jnp.concatenate([t * wa_ref[...] for t in na], axis=1)
        hnb = jnp.concatenate([t * wb_ref[...] for t in nbs], axis=1)
        gr = gr_ref[...].astype(F32)
        dg = dg_ref[...].astype(F32)
        y = (_sigmoid(ma_ref[...].astype(F32)) * hna * (gr * _sigmoid(gr))
             + _sigmoid(mb_ref[...].astype(F32)) * hnb * (dg * _sigmoid(dg)))
        yb = _b(y)
        y_ref[...] = yb
        ho_ref[...] = h_ref[...] + _dot(yb, wo_ref[...])

    row = pl.BlockSpec((tm, D), lambda i: (i, 0))
    return pl.pallas_call(
        body, name=name, grid=(T // tm,),
        in_specs=[row] + _merge_specs(tm),
        out_specs=[row, row],
        out_shape=[jax.ShapeDtypeStruct((T, D), F32), jax.ShapeDtypeStruct((T, D), BF16)],
        compiler_params=_cp("arbitrary"),
    )(h, oa, ob, pbig, pbig, pbig, pbig, gla_hn, dn_hn, wout)


def _merge_bwd(dh, oa, ob, pbig, gla_hn, dn_hn, wout, name):
    T = dh.shape[0]
    tm = min(ROW_BLK, T)

    def branch(dy, o_ref, w_ref, gate_ref, m_ref, dv):
        w = w_ref[...]
        ohat, rs = _head_norm(o_ref[...], w, dv)
        gate = gate_ref[...].astype(F32)
        m = m_ref[...].astype(F32)
        sgate, sm = _sigmoid(gate), _sigmoid(m)
        silu = gate * sgate
        ohat_all = jnp.concatenate(ohat, axis=1)
        hn = jnp.concatenate([t * w for t in ohat], axis=1)
        d_on = dy * sm
        d_m = dy * hn * silu * sm * (1.0 - sm)
        d_hn = d_on * silu
        d_gate = d_on * hn * (sgate * (1.0 + gate * (1.0 - sgate)))
        dw = jnp.zeros((1, dv), F32)
        d_o = []
        for i, (oh, r) in enumerate(zip(ohat, rs)):
            dhn = d_hn[:, i * dv:(i + 1) * dv]
            dw += jnp.sum(dhn * oh, axis=0, keepdims=True)
            dohat = dhn * w
            d_o.append(r * (dohat - oh * jnp.mean(dohat * oh, axis=-1, keepdims=True)))
        return jnp.concatenate(d_o, axis=1), d_gate, d_m, dw

    def body(dh_ref, oa_ref, ob_ref, gr_ref, dg_ref, ma_ref, mb_ref, wa_ref, wb_ref, wo_ref,
             doa_ref, dob_ref, dgr_ref, ddg_ref, dma_ref, dmb_ref, dwa_ref, dwb_ref, dhb_ref):
        @pl.when(pl.program_id(0) == 0)
        def _():
            dwa_ref[...] = jnp.zeros_like(dwa_ref)
            dwb_ref[...] = jnp.zeros_like(dwb_ref)

        dhb = _b(dh_ref[...])
        dhb_ref[...] = dhb
        dy = _dot_nt(dhb, wo_ref[...])
        d_oa, d_gr, d_ma, dwa = branch(dy, oa_ref, wa_ref, gr_ref, ma_ref, GLA_DV)
        d_ob, d_dg, d_mb, dwb = branch(dy, ob_ref, wb_ref, dg_ref, mb_ref, DN_DV)
        doa_ref[...] = _b(d_oa)
        dob_ref[...] = _b(d_ob)
        dgr_ref[...] = _b(d_gr)
        ddg_ref[...] = _b(d_dg)
        dma_ref[...] = _b(d_ma)
        dmb_ref[...] = _b(d_mb)
        dwa_ref[...] += dwa
        dwb_ref[...] += dwb

    row = pl.BlockSpec((tm, D), lambda i: (i, 0))
    f32 = jax.ShapeDtypeStruct((T, D), F32)
    b16 = jax.ShapeDtypeStruct((T, D), BF16)
    return pl.pallas_call(
        body, name=name, grid=(T // tm,),
        in_specs=[row] + _merge_specs(tm),
        out_specs=[row] * 6 + [pl.BlockSpec((1, GLA_DV), lambda i: (0, 0)), pl.BlockSpec((1, DN_DV), lambda i: (0, 0)), row],
        out_shape=[b16, b16, b16, b16, b16, b16, jax.ShapeDtypeStruct((1, GLA_DV), F32),
                   jax.ShapeDtypeStruct((1, DN_DV), F32), b16],
        compiler_params=_cp("arbitrary"),
    )(dh, oa, ob, pbig, pbig, pbig, pbig, gla_hn, dn_hn, wout)


def _loss_head(h, nw, target, name):
    T = h.shape[0]
    tm = min(512, T)

    def body(h_ref, nw_ref, t_ref, dx_ref, loss_ref, dnw_ref):
        @pl.when(pl.program_id(0) == 0)
        def _():
            loss_ref[...] = jnp.zeros_like(loss_ref)
            dnw_ref[...] = jnp.zeros_like(dnw_ref)

        x = h_ref[...]
        w = nw_ref[...]
        r = lax.rsqrt(jnp.mean(x * x, axis=-1, keepdims=True) + EPS)
        xhat = x * r
        err = xhat * w - t_ref[...]
        part = jnp.sum(jnp.sum(err * err, axis=-1, keepdims=True), axis=0, keepdims=True)
        loss_ref[...] += (0.5 / D) * part
        dout = err * (1.0 / D)
        dnw_ref[...] += jnp.sum(dout * xhat, axis=0, keepdims=True)
        dxhat = dout * w
        dx_ref[...] = r * (dxhat - xhat * jnp.mean(dxhat * xhat, axis=-1, keepdims=True))

    row = pl.BlockSpec((tm, D), lambda i: (i, 0))
    one = pl.BlockSpec((1, D), lambda i: (0, 0))
    return pl.pallas_call(
        body, name=name, grid=(T // tm,),
        in_specs=[row, one, row],
        out_specs=[row, pl.BlockSpec((8, 128), lambda i: (0, 0)), one],
        out_shape=[jax.ShapeDtypeStruct((T, D), F32), jax.ShapeDtypeStruct((8, 128), F32),
                   jax.ShapeDtypeStruct((1, D), F32)],
        compiler_params=_cp("arbitrary"),
    )(h, nw, target)


def _adamw(w, g, m, v, rows, name):
    R, C = w.shape
    rows = min(rows, R)
    c1 = 1.0 - ADAM_B1 ** ADAM_STEP
    c2 = 1.0 - ADAM_B2 ** ADAM_STEP

    def body(w_ref, g_ref, m_ref, v_ref, d_ref, mo_ref, vo_ref):
        g_ = g_ref[...]
        m_ = ADAM_B1 * m_ref[...] + (1.0 - ADAM_B1) * g_
        v_ = ADAM_B2 * v_ref[...] + (1.0 - ADAM_B2) * (g_ * g_)
        mo_ref[...] = m_
        vo_ref[...] = v_
        d_ref[...] = -ADAM_LR * ((m_ / c1) / (jnp.sqrt(v_ / c2) + ADAM_EPS) + ADAM_WD * w_ref[...])

    blk = pl.BlockSpec((rows, C), lambda i: (i, 0))
    shp = jax.ShapeDtypeStruct((R, C), F32)
    return pl.pallas_call(
        body, name=name, grid=(R // rows,),
        in_specs=[blk] * 4, out_specs=[blk] * 3, out_shape=[shp] * 3,
        compiler_params=_cp("parallel"),
    )(w, g, m, v)


def _me():
    return lax.axis_index("x"), lax.axis_index("y"), lax.axis_index("c")


def _other_chips(x, y):
    return [(1 - x, y), (x, 1 - y), (1 - x, 1 - y)]


def _half_rows(ref, hf):
    half = ref.shape[-2] // 2
    rows = pl.ds(pl.multiple_of(hf * half, 16), half)
    return ref.at[rows, :] if len(ref.shape) == 2 else ref.at[:, rows, :]


class _GatherBig:
    def __init__(self, big):
        self.arrays = list(big)
        self.out_shape = [jax.ShapeDtypeStruct((N_SHARD,) + w.shape, w.dtype) for w in big]
        self.n_sem = 7 * len(big)

    @staticmethod
    def _copy(sems, k, src, dst, to):
        return pltpu.make_async_remote_copy(src_ref=src, dst_ref=dst, send_sem=sems[0].at[k], recv_sem=sems[1].at[k],
                                            device_id=to, device_id_type=MESH)

    def start(self, ins, outs, *sems):
        x, y, c = _me()
        mine = 2 * x + y
        for i, (w_ref, o_ref) in enumerate(zip(ins, outs)):
            self._copy(sems, 7 * i + 6, w_ref, o_ref.at[mine], (x, y, 1 - c)).start()
            for j, chip in enumerate(_other_chips(x, y)):
                self._copy(sems, 7 * i + j, _half_rows(w_ref, c), _half_rows(o_ref.at[mine], c), (*chip, c)).start()

    def relay(self, ins, outs, *sems):
        x, y, c = _me()
        for i, o_ref in enumerate(outs):
            for j, chip in enumerate(_other_chips(x, y)):
                landed = _half_rows(o_ref.at[2 * chip[0] + chip[1]], c)
                self._copy(sems, 7 * i + j, landed, landed, (x, y, c)).wait_recv()
                self._copy(sems, 7 * i + 3 + j, landed, landed, (x, y, 1 - c)).start()

    def finish(self, ins, outs, *sems):
        x, y, c = _me()
        me, sibling = (x, y, c), (x, y, 1 - c)
        chips = _other_chips(x, y)
        slot = lambda chip: 2 * chip[0] + chip[1]
        for i, (w_ref, o_ref) in enumerate(zip(ins, outs)):
            for j, chip in enumerate(chips):
                passed = _half_rows(o_ref.at[slot(chip)], 1 - c)
                self._copy(sems, 7 * i + 3 + j, passed, passed, me).wait_recv()
            self._copy(sems, 7 * i + 6, o_ref.at[slot((x, y))], o_ref.at[slot((x, y))], me).wait_recv()
        for i, (w_ref, o_ref) in enumerate(zip(ins, outs)):
            self._copy(sems, 7 * i + 6, w_ref, o_ref.at[slot((x, y))], sibling).wait_send()
            for j, chip in enumerate(chips):
                self._copy(sems, 7 * i + j, _half_rows(w_ref, c), _half_rows(o_ref.at[slot((x, y))], c),
                           (*chip, c)).wait_send()
                landed = _half_rows(o_ref.at[slot(chip)], c)
                self._copy(sems, 7 * i + 3 + j, landed, landed, sibling).wait_send()


class _ChipsExchange:
    def __init__(self, pbs):
        self.arrays = list(pbs)
        self.out_shape = [jax.ShapeDtypeStruct((3,) + p.shape[1:], p.dtype) for p in pbs]
        self.n_sem = 3 * len(pbs)

    def _copies(self, ins, outs, send_sems, recv_sems):
        x, y, c = _me()
        return [pltpu.make_async_remote_copy(src_ref=ins[i].at[2 * chip[0] + chip[1]], dst_ref=outs[i].at[j],
                                             send_sem=send_sems.at[3 * i + j], recv_sem=recv_sems.at[3 * i + j],
                                             device_id=(*chip, c), device_id_type=MESH)
                for i in range(len(ins)) for j, chip in enumerate(_other_chips(x, y))]

    def start(self, ins, outs, *sems):
        for cp in self._copies(ins, outs, *sems):
            cp.start()

    def finish(self, ins, outs, *sems):
        for cp in self._copies(ins, outs, *sems):
            cp.wait()


def _carry(ex, body, n_in, n_out, grid):
    if ex is None:
        return body
    ni, no = len(ex.arrays), len(ex.out_shape)

    def carried(*refs):
        ins, ex_in = refs[:n_in], refs[n_in:n_in + ni]
        outs, ex_out = refs[n_in + ni:n_in + ni + n_out], refs[n_in + ni + n_out:n_in + ni + n_out + no]
        scratch, sems = refs[n_in + ni + n_out + no:-2], refs[-2:]
        step = functools.reduce(lambda acc, a: acc * grid[a] + pl.program_id(a), range(len(grid)), 0)
        steps = math.prod(grid)

        @pl.when(step == 0)
        def _():
            ex.start(ex_in, ex_out, *sems)

        body(*ins, *outs, *scratch)

        if hasattr(ex, "relay"):
            @pl.when(step == (3 * steps) // 4)
            def _():
                ex.relay(ex_in, ex_out, *sems)

        @pl.when(step == steps - 1)
        def _():
            ex.finish(ex_in, ex_out, *sems)

    return carried


def _carry_specs(ex):
    if ex is None:
        return [], [], [], [], []
    sems = [pltpu.SemaphoreType.DMA((ex.n_sem,)), pltpu.SemaphoreType.DMA((ex.n_sem,))]
    return [ANY] * len(ex.arrays), ex.arrays, [ANY] * len(ex.out_shape), ex.out_shape, sems


def _gather_weights(big, small, name):
    nbig, nsm = len(big), len(small)
    n = nbig + nsm
    own_sem = 6 * nbig + 3 * nsm

    def body(*refs):
        ins, outs = refs[:n], refs[n:2 * n]
        send_sems, recv_sems = refs[2 * n:]
        x, y, c = _me()
        sibling = (x, y, 1 - c)
        chips = _other_chips(x, y)
        slot = lambda chip: 2 * chip[0] + chip[1]

        def copy(k, src, dst, to):
            return pltpu.make_async_remote_copy(src_ref=src, dst_ref=dst, send_sem=send_sems.at[k],
                                                recv_sem=recv_sems.at[k], device_id=to, device_id_type=MESH)

        sent = []
        for i in range(nbig):
            sent.append(copy(own_sem + i, ins[i], outs[i].at[slot((x, y))], sibling))
            sent[-1].start()
            for j, chip in enumerate(chips):
                sent.append(copy(6 * i + j, _half_rows(ins[i], c), _half_rows(outs[i].at[slot((x, y))], c), (*chip, c)))
                sent[-1].start()
        for t in range(nsm):
            w_ref, o_ref = ins[nbig + t], outs[nbig + t]
            o_ref[slot((x, y))] = w_ref[...]
            for j, chip in enumerate(chips):
                sent.append(copy(6 * nbig + 3 * t + j, w_ref, o_ref.at[slot((x, y))], (*chip, c)))
                sent[-1].start()
        for i in range(nbig):
            for j, chip in enumerate(chips):
                landed = _half_rows(outs[i].at[slot(chip)], c)
                copy(6 * i + j, landed, landed, (x, y, c)).wait_recv()
                sent.append(copy(6 * i + 3 + j, landed, landed, sibling))
                sent[-1].start()
        for t in range(nsm):
            for j, chip in enumerate(chips):
                landed = outs[nbig + t].at[slot(chip)]
                copy(6 * nbig + 3 * t + j, landed, landed, (x, y, c)).wait_recv()
        for i in range(nbig):
            for j, chip in enumerate(chips):
                passed = _half_rows(outs[i].at[slot(chip)], 1 - c)
                copy(6 * i + 3 + j, passed, passed, (x, y, c)).wait_recv()
        for i in range(nbig):
            mine = outs[i].at[slot((x, y))]
            copy(own_sem + i, mine, mine, (x, y, c)).wait_recv()
        for cp in sent:
            cp.wait_send()

    vm = pl.BlockSpec(memory_space=pltpu.VMEM)
    nsem = own_sem + nbig
    return pl.pallas_call(
        body, name=name, in_specs=[ANY] * nbig + [vm] * nsm, out_specs=[ANY] * nbig + [vm] * nsm,
        out_shape=[jax.ShapeDtypeStruct((N_SHARD,) + w.shape, w.dtype) for w in list(big) + list(small)],
        scratch_shapes=[pltpu.SemaphoreType.DMA((nsem,)), pltpu.SemaphoreType.DMA((nsem,))],
        compiler_params=pltpu.CompilerParams(has_side_effects=True),
    )(*big, *small)


def _rs_sibling(gs, name):
    n = len(gs)

    def body(*refs):
        send_sems, recv_sems = refs[2 * n:]
        x, y, c = _me()
        cps = [pltpu.make_async_remote_copy(src_ref=_half_rows(refs[i], 1 - c), dst_ref=refs[n + i],
                                            send_sem=send_sems.at[i], recv_sem=recv_sems.at[i],
                                            device_id=(x, y, 1 - c), device_id_type=MESH) for i in range(n)]
        for cp in cps:
            cp.start()
        for cp in cps:
            cp.wait()

    return pl.pallas_call(
        body, name=name, in_specs=[ANY] * n, out_specs=[ANY] * n,
        out_shape=[jax.ShapeDtypeStruct((g.shape[0], g.shape[1] // 2, g.shape[2]), g.dtype) for g in gs],
        scratch_shapes=[pltpu.SemaphoreType.DMA((n,)), pltpu.SemaphoreType.DMA((n,))],
        compiler_params=pltpu.CompilerParams(has_side_effects=True),
    )(*gs)


def _add_pair(g, other, where, name):
    ns, a, b = g.shape
    half = a // 2

    def body(w_ref, g_ref, o_ref, pb_ref, own_ref):
        t = g_ref[0].astype(F32) + o_ref[0].astype(F32)
        pb_ref[0] = _b(t)

        @pl.when(pl.program_id(0) == w_ref[1])
        def _():
            own_ref[...] = t

    return pl.pallas_call(
        body, name=name,
        grid_spec=pltpu.PrefetchScalarGridSpec(
            num_scalar_prefetch=1, grid=(ns,),
            in_specs=[pl.BlockSpec((1, half, b), lambda s, w: (s, w[0], 0)), pl.BlockSpec((1, half, b), lambda s, w: (s, 0, 0))],
            out_specs=[pl.BlockSpec((1, half, b), lambda s, w: (s, 0, 0)), pl.BlockSpec((half, b), lambda s, w: (0, 0))]),
        out_shape=[jax.ShapeDtypeStruct((ns, half, b), BF16), jax.ShapeDtypeStruct((half, b), F32)],
        compiler_params=_cp("arbitrary"),
    )(where, g, other)


def _add_four(own, got, name):
    rows, cols = own.shape
    rb = rows // 2

    def body(a_ref, b_ref, o_ref):
        o_ref[...] = ((a_ref[...] + b_ref[0].astype(F32)) + b_ref[1].astype(F32)) + b_ref[2].astype(F32)

    return pl.pallas_call(
        body, name=name, grid=(rows // rb,),
        in_specs=[pl.BlockSpec((rb, cols), lambda i: (i, 0)), pl.BlockSpec((3, rb, cols), lambda i: (0, i, 0))],
        out_specs=pl.BlockSpec((rb, cols), lambda i: (i, 0)),
        out_shape=jax.ShapeDtypeStruct((rows, cols), F32),
        compiler_params=_cp("parallel"),
    )(own, got)


def _rs_swap(halves, name):
    n = len(halves)

    def body(*refs):
        send_sems, recv_sems = refs[2 * n:]
        x, y, c = _me()
        cps = [pltpu.make_async_remote_copy(src_ref=refs[i], dst_ref=refs[n + i], send_sem=send_sems.at[i],
                                            recv_sem=recv_sems.at[i], device_id=(x, y, 1 - c), device_id_type=MESH)
               for i in range(n)]
        for cp in cps:
            cp.start()
        for cp in cps:
            cp.wait()

    return pl.pallas_call(
        body, name=name, in_specs=[ANY] * n, out_specs=[ANY] * n,
        out_shape=[jax.ShapeDtypeStruct(h.shape, h.dtype) for h in halves],
        scratch_shapes=[pltpu.SemaphoreType.DMA((n,)), pltpu.SemaphoreType.DMA((n,))],
        compiler_params=pltpu.CompilerParams(has_side_effects=True),
    )(*halves)


def _adamw_halves(w, own, got, m, v, rows, name):
    a, b = w.shape
    nblk = a // 2 // rows
    c1 = 1.0 - ADAM_B1 ** ADAM_STEP
    c2 = 1.0 - ADAM_B2 ** ADAM_STEP

    def body(w_ref, own_ref, got_ref, m_ref, v_ref, g_ref, d_ref, mo_ref, vo_ref):
        g_ = jnp.where(pl.program_id(0) == lax.axis_index("c"), own_ref[...], got_ref[...])
        g_ref[...] = g_
        m_ = ADAM_B1 * m_ref[...] + (1.0 - ADAM_B1) * g_
        v_ = ADAM_B2 * v_ref[...] + (1.0 - ADAM_B2) * (g_ * g_)
        mo_ref[...] = m_
        vo_ref[...] = v_
        d_ref[...] = -ADAM_LR * ((m_ / c1) / (jnp.sqrt(v_ / c2) + ADAM_EPS) + ADAM_WD * w_ref[...])

    whole = pl.BlockSpec((rows, b), lambda h, i: (h * nblk + i, 0))
    part = pl.BlockSpec((rows, b), lambda h, i: (i, 0))
    shp = jax.ShapeDtypeStruct((a, b), F32)
    return pl.pallas_call(
        body, name=name, grid=(2, nblk),
        in_specs=[whole, part, part, whole, whole], out_specs=[whole] * 4, out_shape=[shp] * 4,
        compiler_params=_cp("parallel", "parallel"),
    )(w, own, got, m, v)


def _allsum_small(vec, name):
    def body(v_ref, o_ref, buf_ref, send_sems, recv_sems):
        x, y, c = _me()
        me = 4 * x + 2 * y + c
        buf_ref[me] = v_ref[...]
        cps = []
        for k in range(1, 8):
            peer = (x ^ (k >> 2), y ^ ((k >> 1) & 1), c ^ (k & 1))
            cps.append(pltpu.make_async_remote_copy(src_ref=v_ref, dst_ref=buf_ref.at[me],
                                                    send_sem=send_sems.at[k - 1], recv_sem=recv_sems.at[k - 1],
                                                    device_id=peer, device_id_type=MESH))
        for cp in cps:
            cp.start()
        for k in range(1, 8):
            peer_idx = me ^ k
            pltpu.make_async_remote_copy(src_ref=v_ref, dst_ref=buf_ref.at[peer_idx],
                                         send_sem=send_sems.at[k - 1], recv_sem=recv_sems.at[k - 1],
                                         device_id=(x, y, c), device_id_type=MESH).wait_recv()
        for cp in cps:
            cp.wait_send()
        acc = buf_ref[0]
        for d in range(1, 8):
            acc = acc + buf_ref[d]
        o_ref[...] = acc

    return pl.pallas_call(
        body, name=name,
        in_specs=[pl.BlockSpec(memory_space=pltpu.VMEM)], out_specs=pl.BlockSpec(memory_space=pltpu.VMEM),
        out_shape=jax.ShapeDtypeStruct(vec.shape, F32),
        scratch_shapes=[pltpu.VMEM((8,) + vec.shape, F32), pltpu.SemaphoreType.DMA((7,)), pltpu.SemaphoreType.DMA((7,))],
        compiler_params=pltpu.CompilerParams(has_side_effects=True),
    )(vec)


BIG = ("ffn1_w_gate", "ffn1_w_up", "ffn1_w_down", "w_in", "w_out", "ffn2_w_gate", "ffn2_w_up", "ffn2_w_down")
TINY = ("w_gla_gate", "conv_w")
SHARDED = BIG + TINY


def _join_cols(w4):
    return jnp.transpose(w4, (1, 0, 2)).reshape(w4.shape[1], N_SHARD * w4.shape[2])


def _cut_cols(w):
    return jnp.transpose(w.reshape(w.shape[0], N_SHARD, w.shape[1] // N_SHARD), (1, 0, 2))


def _split_w_in(w):
    o = IN_OFF
    big = jnp.concatenate([w[:, :o[4]], w[:, o[5]:o[9]], w[:, o[11]:]], axis=1)
    small = jnp.concatenate([w[:, o[4]:o[5]], w[:, o[9]:o[11]], jnp.zeros((w.shape[0], SMALL - 32), w.dtype)], axis=1)
    return big, small


def _merge_w_in(big, small):
    return jnp.concatenate([big[:, :3072], small[:, :16], big[:, 3072:7168], small[:, 16:32], big[:, 7168:]], axis=1)


class _Comm:
    def __init__(self, where, rest_shards):
        self.where = where
        self.w_in = _GatherBig(rest_shards[:1])
        self.later = _GatherBig(rest_shards[1:])
        self.pairs, self.got = {}, {}

    @staticmethod
    def w_in_weights(gathered):
        return dict(zip(("w_in_big", "w_in_small"), _split_w_in(_join_cols(gathered[0]))))

    @staticmethod
    def later_weights(gathered):
        W = dict(zip(BIG[4:], gathered))
        W["w_out"] = W["w_out"].reshape(D, D)
        return W

    def begin(self, names, grads):
        from_sibling = _rs_sibling(grads, "rs_sibling_" + names[0])
        for n, g, o in zip(names, grads, from_sibling):
            self.pairs[n] = _add_pair(g, o, self.where, "rs_pair_" + n)
        return _ChipsExchange([self.pairs[n][0] for n in names])

    def landed(self, names, outs):
        self.got.update(zip(names, outs))


def _local_step(x, target, W, P, comm=None):
    wgate_pad = jnp.zeros((SMALL, GLA_H * GLA_DK), F32).at[:GLA_RANK].set(P["w_gla_gate"])
    cw8 = jnp.zeros((8, CONV_C), F32).at[:CONV_K].set(P["conv_w"])
    par = jnp.zeros((DN_H, 8, 128), F32)
    par = par.at[:, 0, :].set(jnp.broadcast_to(P["dn_a_log"].reshape(DN_H, 1), (DN_H, 128)))
    par = par.at[:, 1, :].set(jnp.broadcast_to(P["dn_dt_bias"].reshape(DN_H, 1), (DN_H, 128)))

    h1, n1, g1, u1, *got = _ffn_fwd(x, P["ffn1_norm"], W["ffn1_w_gate"], W["ffn1_w_up"], W["ffn1_w_down"], "ffn1_fwd",
                                    carry=comm.w_in if comm else None)
    if comm:
        W = dict(W, **comm.w_in_weights(got))
    wbig, wsmall = W["w_in_big"], W["w_in_small"]
    pbig, psmall, n2, *got = _norm_proj(h1, P["mix_norm"], wbig, wsmall, "mix_proj", carry=comm.later if comm else None)
    if comm:
        W = dict(W, **comm.later_weights(got))
    oa, sa = _gla_fwd(pbig, psmall, wgate_pad, P["b_gla_gate"], "gla_fwd")
    conv = _conv_fwd(pbig, cw8, "conv_fwd")
    ob, sb, *solved = _gdn_fwd(conv, psmall, par, "gdn_fwd")
    h2, yb = _merge_fwd(h1, oa, ob, pbig, P["gla_head_norm"], P["dn_head_norm"], W["w_out"], "merge_fwd")
    h3, n3, g3, u3 = _ffn_fwd(h2, P["ffn2_norm"], W["ffn2_w_gate"], W["ffn2_w_up"], W["ffn2_w_down"], "ffn2_fwd")
    dh3, loss, d_final = _loss_head(h3, P["final_norm"], target, "loss_head")

    gw, gs = {}, {"final_norm": d_final}

    def ffn_grads(tag, dh, h, n, g, u, before=None):
        names = tuple(tag + s for s in ("_w_gate", "_w_up", "_w_down"))
        dg, du, act, dfb, *landed = _ffn_bwd_hidden(dh, g, u, W[names[2]], tag + "_bwd_hidden", carry=before)
        gw[names[0]] = _mm_tn(n, dg, D, FF_CUT, tag + "_dwg")
        gw[names[1]] = _mm_tn(n, du, D, FF_CUT, tag + "_dwu")
        gw[names[2]] = _mm_tn(act, dfb, FF_CUT, D, tag + "_dwd")
        dx, gs[tag + "_norm"], *own = _ffn_bwd_input(dh, h, P[tag + "_norm"], dg, du, W[names[0]], W[names[1]],
                                                     tag + "_bwd_input",
                                                     carry=comm.begin(names, [gw[n] for n in names]) if comm else None)
        if comm:
            comm.landed(names, own)
        return dx, landed

    dh2, _ = ffn_grads("ffn2", dh3, h2, n3, g3, u3)
    d_oa, d_ob, d_gr, d_dgate, d_ma, d_mb, gs["gla_head_norm"], gs["dn_head_norm"], dh2b = _merge_bwd(
        dh2, oa, ob, pbig, P["gla_head_norm"], P["dn_head_norm"], W["w_out"], "merge_bwd")
    gw["w_out"] = _mm_tn(yb, dh2b, D, D, "dw_out").reshape(N_SHARD, D // N_SHARD, D)
    early = ("w_out",)
    d_gq, d_gk, d_gv, dpre = _gla_bwd(pbig, psmall, wgate_pad, P["b_gla_gate"], sa, d_oa, "gla_bwd")
    dcq, dck, dcv, dsm, dpar, *landed = _gdn_bwd(conv, psmall, par, sb, d_ob, solved, "gdn_bwd",
                                                 carry=comm.begin(early, [gw[n] for n in early]) if comm else None)
    if comm:
        comm.landed(early, landed)
    dsmall, dwgate, gs["b_gla_gate"] = _gla_gate_bwd(dpre, psmall, wgate_pad, dsm, "gla_gate_bwd")
    gs["w_gla_gate"] = dwgate[:GLA_RANK]
    d_x3, dcw = _conv_bwd(dcq, dck, dcv, pbig, cw8, "conv_bwd")
    gs["conv_w"] = dcw[:CONV_K]
    gs["dn_a_log"] = dpar[:, 0, 0].reshape(1, DN_H)
    gs["dn_dt_bias"] = dpar[:, 0, 1].reshape(1, DN_H)
    pieces = (d_gq, d_gk, d_gv, d_gr, d_x3, d_dgate, d_ma, d_mb)
    dh1, gs["mix_norm"] = _proj_bwd(dh2, h1, P["mix_norm"], pieces, dsmall, wbig, wsmall, "proj_bwd")
    dbig = jnp.concatenate([_mm_tn(n2, p, D, 1024, "dw_in_%d" % i) for i, p in enumerate(pieces)], axis=1)
    dsml = _mm_tn(n2, dsmall, D, SMALL, "dw_in_small")
    gw["w_in"] = _cut_cols(_merge_w_in(dbig, dsml))
    grad_x, landed = ffn_grads("ffn1", dh1, x, n1, g1, u1,
                               before=comm.begin(("w_in",), [gw["w_in"]]) if comm else None)
    if comm:
        comm.landed(("w_in",), landed)
    return loss, grad_x, gw, gs


SMALL_NAMES = ("ffn1_norm", "mix_norm", "ffn2_norm", "final_norm", "b_gla_gate", "gla_head_norm", "dn_head_norm",
               "dn_a_log", "dn_dt_bias")
ROW4 = (("b_gla_gate", 512), ("gla_head_norm", 256), ("dn_head_norm", 128), ("dn_a_log", 8), ("dn_dt_bias", 8))


def _pack_small(d, loss=None):
    row4 = [d[n].reshape(-1) for n, _ in ROW4]
    row4.append(jnp.zeros((1,), F32) if loss is None else loss.reshape(1))
    row4 = jnp.concatenate(row4)
    row4 = jnp.pad(row4, (0, D - row4.shape[0]))
    rows = [d[n].reshape(-1) for n in SMALL_NAMES[:4]] + [row4]
    return jnp.concatenate([jnp.stack(rows), jnp.zeros((3, D), F32)], axis=0)


def _unpack_small(a, like):
    out = {n: a[i].reshape(like[n].shape) for i, n in enumerate(SMALL_NAMES[:4])}
    off = 0
    for n, w in ROW4:
        out[n] = a[4, off:off + w].reshape(like[n].shape)
        off += w
    return out, a[4, off]


WEIGHT_ORDER = ("ffn1_norm", "ffn1_w_gate", "ffn1_w_up", "ffn1_w_down", "mix_norm", "w_in", "w_gla_gate", "b_gla_gate",
                "conv_w", "dn_a_log", "dn_dt_bias", "gla_head_norm", "dn_head_norm", "w_out", "ffn2_norm",
                "ffn2_w_gate", "ffn2_w_up", "ffn2_w_down", "final_norm")
ADAM_ROWS = {"ffn1_w_gate": 256, "ffn1_w_up": 256, "ffn1_w_down": 176, "w_in": 128, "w_gla_gate": 16, "conv_w": 4,
             "w_out": 64, "ffn2_w_gate": 256, "ffn2_w_up": 256, "ffn2_w_down": 176}


def kernel(x, ffn1_norm, ffn1_w_gate, ffn1_w_up, ffn1_w_down, mix_norm, w_in, w_gla_gate, b_gla_gate, conv_w, dn_a_log, dn_dt_bias, gla_head_norm, dn_head_norm, w_out, ffn2_norm, ffn2_w_gate, ffn2_w_up, ffn2_w_down, final_norm, loss_target, m_ffn1_norm, m_ffn1_w_gate, m_ffn1_w_up, m_ffn1_w_down, m_mix_norm, m_w_in, m_w_gla_gate, m_b_gla_gate, m_conv_w, m_dn_a_log, m_dn_dt_bias, m_gla_head_norm, m_dn_head_norm, m_w_out, m_ffn2_norm, m_ffn2_w_gate, m_ffn2_w_up, m_ffn2_w_down, m_final_norm, v_ffn1_norm, v_ffn1_w_gate, v_ffn1_w_up, v_ffn1_w_down, v_mix_norm, v_w_in, v_w_gla_gate, v_b_gla_gate, v_conv_w, v_dn_a_log, v_dn_dt_bias, v_gla_head_norm, v_dn_head_norm, v_w_out, v_ffn2_norm, v_ffn2_w_gate, v_ffn2_w_up, v_ffn2_w_down, v_final_norm):
    given = dict(locals())
    wts = {n: given[n] for n in WEIGHT_ORDER}
    moms = {n: given["m_" + n] for n in WEIGHT_ORDER}
    vars_ = {n: given["v_" + n] for n in WEIGHT_ORDER}
    two_d = lambda a: a.reshape(a.shape[-2], a.shape[-1]) if a.ndim == 3 else a.reshape(1, -1)
    shard = {n: two_d(wts[n]) for n in SHARDED}

    gathered = _gather_weights([shard[n].astype(BF16) for n in BIG[:3]], [shard[n] for n in TINY], "gather_first")
    W = dict(zip(BIG[:3], gathered))
    P = {n: two_d(wts[n]) for n in SMALL_NAMES}
    for n, g in zip(TINY, gathered[3:]):
        P[n] = _join_cols(g)

    my_slot = 2 * lax.axis_index("x") + lax.axis_index("y")
    where = jnp.stack([lax.axis_index("c"), my_slot]).astype(jnp.int32)
    comm = _Comm(where, [shard[n].astype(BF16) for n in BIG[3:]])
    loss, grad_x, gw, gs = _local_step(x[0], loss_target[0], W, P, comm)
    halves = [_add_four(comm.pairs[n][1], comm.got[n], "rs_four_" + n) for n in BIG]
    other_halves = _rs_swap(halves, "rs_swap")

    tiny_rows = jnp.concatenate([gs["w_gla_gate"].reshape(8, D), gs["conv_w"].reshape(12, D), jnp.zeros((4, D), F32)])
    all_sum = _allsum_small(jnp.concatenate([_pack_small(gs, loss[0, 0]), tiny_rows]), "allsum_small")
    small_sum = all_sum[:8]
    small_g, loss_total = _unpack_small(small_sum, P)

    grads, delta, new_m, new_v = {}, {}, {}, {}
    for n, own, got in zip(BIG, halves, other_halves):
        res = _adamw_halves(shard[n], own, got, two_d(moms[n]), two_d(vars_[n]), ADAM_ROWS[n], "adamw_" + n)
        grads[n], delta[n], new_m[n], new_v[n] = (t.reshape(wts[n].shape) for t in res)
    for n, rows in (("w_gla_gate", all_sum[8:16]), ("conv_w", all_sum[16:28])):
        cols = shard[n].shape[1]
        grads[n] = lax.dynamic_slice_in_dim(rows.reshape(shard[n].shape[0], N_SHARD * cols), my_slot * cols, cols, axis=1)
        d, m_, v_ = _adamw(shard[n], grads[n], two_d(moms[n]), two_d(vars_[n]), ADAM_ROWS[n], "adamw_" + n)
        delta[n], new_m[n], new_v[n] = (t.reshape(wts[n].shape) for t in (d, m_, v_))
    pk = lambda src: _pack_small({n: two_d(src[n]) for n in SMALL_NAMES})
    sd, sm_, sv_ = _adamw(pk(wts), small_sum, pk(moms), pk(vars_), 8, "adamw_small")
    for res, dst in ((sd, delta), (sm_, new_m), (sv_, new_v)):
        u, _ = _unpack_small(res, wts)
        dst.update(u)
    grad_w = {n: grads[n].reshape(wts[n].shape) for n in SHARDED}
    grad_w.update({n: small_g[n].reshape(wts[n].shape) for n in SMALL_NAMES})
    return (loss_total, grad_x[None], *[grad_w[n] for n in WEIGHT_ORDER], *[delta[n] for n in WEIGHT_ORDER],
            *[new_m[n] for n in WEIGHT_ORDER], *[new_v[n] for n in WEIGHT_ORDER])
```

```python
import functools
import math

import numpy as np
import jax
import jax.numpy as jnp
from jax import lax
from jax.experimental import pallas as pl
from jax.experimental.pallas import tpu as pltpu

F32 = jnp.float32
BF16 = jnp.bfloat16
HI = lax.Precision.HIGH
MESH = pl.DeviceIdType.MESH
ANY = pl.BlockSpec(memory_space=pl.ANY)

EPS = 1e-6
D = 1024
DFF = 2816
FFN_RES = 0.5
GLA_H, GLA_DK, GLA_DV, GLA_RANK, GLA_TAU = 4, 128, 256, 16, 16.0
DN_H, DN_DK, DN_DV = 8, 128, 128
CONV_K = 4
CHUNK = 64
N_SHARD = 4
FF_CUT = DFF // N_SHARD
ADAM_LR, ADAM_B1, ADAM_B2, ADAM_EPS, ADAM_WD, ADAM_STEP = 0.001, 0.9, 0.999, 1e-08, 0.01, 10

IN_SIZES = (512, 512, 1024, 1024, 16, 1024, 1024, 1024, 1024, 8, 8, 1024, 1024)
IN_OFF = tuple(int(v) for v in np.cumsum((0,) + IN_SIZES))
D_IN = IN_OFF[-1]
BIG_COLS = 9216
SMALL = 128
PIECES = (512, 512, 1024, 1024, 3072, 1024, 1024, 1024)

VMEM_LIMIT = 56 * 1024 * 1024
ROW_BLK = 256
BIG_ROW_BLK = 512
ATT_BLK = 512
GDN_BLK = 256
GDN_HEADS = 8


def _cp(*sem):
    return pltpu.CompilerParams(dimension_semantics=sem, vmem_limit_bytes=VMEM_LIMIT)


def _sigmoid(x):
    return 1.0 / (1.0 + jnp.exp(-x))


def _softplus(x):
    return jnp.maximum(x, 0.0) + jnp.log(1.0 + jnp.exp(-jnp.abs(x)))


def _log_sigmoid(x):
    return jnp.minimum(x, 0.0) - jnp.log(1.0 + jnp.exp(-jnp.abs(x)))


def _dot(a, b, prec=None):
    return jnp.dot(a, b, preferred_element_type=F32, precision=prec)


def _dot_nt(a, b, prec=None):
    return lax.dot_general(a, b, (((1,), (1,)), ((), ())), preferred_element_type=F32, precision=prec)


def _dot_tn(a, b, prec=None):
    return lax.dot_general(a, b, (((0,), (0,)), ((), ())), preferred_element_type=F32, precision=prec)


def _b(x):
    return x.astype(BF16)


def _iota2(n, m, axis):
    return lax.broadcasted_iota(jnp.int32, (n, m), axis)


def _load_weights(pairs, sem):
    copies = [pltpu.make_async_copy(s, d, sem.at[i]) for i, (s, d) in enumerate(pairs)]
    for c in copies:
        c.start()
    for c in copies:
        c.wait()


def _ffn_fwd(h, nw, wg, wu, wd, name, carry=None):
    T = h.shape[0]
    tm = min(BIG_ROW_BLK, T)
    ex_in, ex_args, ex_out, ex_shape, ex_sems = _carry_specs(carry)

    def body(h_ref, nw_ref, wg_hbm, wu_hbm, wd_hbm, ho_ref, n_ref, g_ref, u_ref, wg_v, wu_v, wd_v, sem):
        @pl.when(pl.program_id(0) == 0)
        def _():
            _load_weights(((wg_hbm, wg_v), (wu_hbm, wu_v), (wd_hbm, wd_v)), sem)

        x = h_ref[...]
        r = lax.rsqrt(jnp.mean(x * x, axis=-1, keepdims=True) + EPS)
        nb = _b((x * r) * nw_ref[...])
        n_ref[...] = nb
        acc = jnp.zeros((tm, D), F32)
        for s in range(N_SHARD):
            g = _dot(nb, wg_v[s])
            u = _dot(nb, wu_v[s])
            g_ref[s] = _b(g)
            u_ref[s] = _b(u)
            acc += _dot(_b(g * _sigmoid(g) * u), wd_v[s])
        ho_ref[...] = x + FFN_RES * acc

    row = lambda w: pl.BlockSpec((tm, w), lambda i: (i, 0))
    cut = pl.BlockSpec((N_SHARD, tm, FF_CUT), lambda i: (0, i, 0))
    return pl.pallas_call(
        _carry(carry, body, 5, 4, (T // tm,)), name=name, grid=(T // tm,),
        in_specs=[row(D), pl.BlockSpec((1, D), lambda i: (0, 0)), ANY, ANY, ANY] + ex_in,
        out_specs=[row(D), row(D), cut, cut] + ex_out,
        out_shape=[jax.ShapeDtypeStruct((T, D), F32), jax.ShapeDtypeStruct((T, D), BF16),
                   jax.ShapeDtypeStruct((N_SHARD, T, FF_CUT), BF16),
                   jax.ShapeDtypeStruct((N_SHARD, T, FF_CUT), BF16)] + ex_shape,
        scratch_shapes=[pltpu.VMEM((N_SHARD, D, FF_CUT), BF16), pltpu.VMEM((N_SHARD, D, FF_CUT), BF16),
                        pltpu.VMEM((N_SHARD, FF_CUT, D), BF16), pltpu.SemaphoreType.DMA((3,))] + ex_sems,
        compiler_params=_cp("arbitrary"),
    )(h, nw, wg, wu, wd, *ex_args)


def _ffn_bwd_hidden(dh, g, u, wd, name, carry=None):
    T = dh.shape[0]
    tm = min(BIG_ROW_BLK, T)
    ex_in, ex_args, ex_out, ex_shape, ex_sems = _carry_specs(carry)

    def body(dh_ref, g_ref, u_ref, wd_hbm, dg_ref, du_ref, a_ref, df_ref, wd_v, sem):
        @pl.when(pl.program_id(0) == 0)
        def _():
            _load_weights(((wd_hbm, wd_v),), sem)

        dfb = _b(FFN_RES * dh_ref[...])
        df_ref[...] = dfb
        for s in range(N_SHARD):
            da = _dot_nt(dfb, wd_v[s])
            gg = g_ref[s].astype(F32)
            uu = u_ref[s].astype(F32)
            sg = _sigmoid(gg)
            silu = gg * sg
            a_ref[s] = _b(silu * uu)
            dg_ref[s] = _b(da * uu * (sg * (1.0 + gg * (1.0 - sg))))
            du_ref[s] = _b(da * silu)

    row = pl.BlockSpec((tm, D), lambda i: (i, 0))
    cut = pl.BlockSpec((N_SHARD, tm, FF_CUT), lambda i: (0, i, 0))
    cut_shape = jax.ShapeDtypeStruct((N_SHARD, T, FF_CUT), BF16)
    return pl.pallas_call(
        _carry(carry, body, 4, 4, (T // tm,)), name=name, grid=(T // tm,),
        in_specs=[row, cut, cut, ANY] + ex_in,
        out_specs=[cut, cut, cut, row] + ex_out,
        out_shape=[cut_shape, cut_shape, cut_shape, jax.ShapeDtypeStruct((T, D), BF16)] + ex_shape,
        scratch_shapes=[pltpu.VMEM((N_SHARD, FF_CUT, D), BF16), pltpu.SemaphoreType.DMA((1,))] + ex_sems,
        compiler_params=_cp("arbitrary"),
    )(dh, g, u, wd, *ex_args)


def _ffn_bwd_input(dh, h, nw, dg, du, wg, wu, name, carry=None):
    T = h.shape[0]
    tm = min(BIG_ROW_BLK, T)
    ex_in, ex_args, ex_out, ex_shape, ex_sems = _carry_specs(carry)

    def body(dh_ref, h_ref, nw_ref, dg_ref, du_ref, wg_hbm, wu_hbm, dx_ref, dnw_ref, wg_v, wu_v, sem):
        @pl.when(pl.program_id(0) == 0)
        def _():
            _load_weights(((wg_hbm, wg_v), (wu_hbm, wu_v)), sem)
            dnw_ref[...] = jnp.zeros_like(dnw_ref)

        dn = jnp.zeros((tm, D), F32)
        for s in range(N_SHARD):
            dn += _dot_nt(dg_ref[s], wg_v[s]) + _dot_nt(du_ref[s], wu_v[s])
        x = h_ref[...]
        r = lax.rsqrt(jnp.mean(x * x, axis=-1, keepdims=True) + EPS)
        xhat = x * r
        dnw_ref[...] += jnp.sum(dn * xhat, axis=0, keepdims=True)
        dxhat = dn * nw_ref[...]
        dx_ref[...] = dh_ref[...] + r * (dxhat - xhat * jnp.mean(dxhat * xhat, axis=-1, keepdims=True))

    row = pl.BlockSpec((tm, D), lambda i: (i, 0))
    one = pl.BlockSpec((1, D), lambda i: (0, 0))
    cut = pl.BlockSpec((N_SHARD, tm, FF_CUT), lambda i: (0, i, 0))
    return pl.pallas_call(
        _carry(carry, body, 7, 2, (T // tm,)), name=name, grid=(T // tm,),
        in_specs=[row, row, one, cut, cut, ANY, ANY] + ex_in,
        out_specs=[row, one] + ex_out,
        out_shape=[jax.ShapeDtypeStruct((T, D), F32), jax.ShapeDtypeStruct((1, D), F32)] + ex_shape,
        scratch_shapes=[pltpu.VMEM((N_SHARD, D, FF_CUT), BF16), pltpu.VMEM((N_SHARD, D, FF_CUT), BF16),
                        pltpu.SemaphoreType.DMA((2,))] + ex_sems,
        compiler_params=_cp("arbitrary"),
    )(dh, h, nw, dg, du, wg, wu, *ex_args)


def _mm_tn(a, b, bm, bn, name, out_dtype=BF16, tk=2048):
    cuts = a.shape[0] if a.ndim == 3 else (b.shape[0] if b.ndim == 3 else None)
    T, M = a.shape[-2:]
    N = b.shape[-1]
    tk = min(tk, T)
    bm, bn = min(bm, M), min(bn, N)
    nk = T // tk

    def body(a_ref, b_ref, o_ref, acc_ref):
        k = pl.program_id(3)

        @pl.when(k == 0)
        def _():
            acc_ref[...] = jnp.zeros_like(acc_ref)

        av = a_ref[0] if a.ndim == 3 else a_ref[...]
        bv = b_ref[0] if b.ndim == 3 else b_ref[...]
        acc_ref[...] += _dot_tn(_b(av), _b(bv))

        @pl.when(k == nk - 1)
        def _():
            res = acc_ref[...].astype(out_dtype)
            if cuts is None:
                o_ref[...] = res
            else:
                o_ref[0] = res

    a_spec = (pl.BlockSpec((1, tk, bm), lambda s, i, j, k: (s, k, i)) if a.ndim == 3
              else pl.BlockSpec((tk, bm), lambda s, i, j, k: (k, i)))
    b_spec = (pl.BlockSpec((1, tk, bn), lambda s, i, j, k: (s, k, j)) if b.ndim == 3
              else pl.BlockSpec((tk, bn), lambda s, i, j, k: (k, j)))
    if cuts is None:
        o_spec, o_shape = pl.BlockSpec((bm, bn), lambda s, i, j, k: (i, j)), (M, N)
    else:
        o_spec, o_shape = pl.BlockSpec((1, bm, bn), lambda s, i, j, k: (s, i, j)), (cuts, M, N)
    return pl.pallas_call(
        body, name=name, grid=(cuts or 1, M // bm, N // bn, nk),
        in_specs=[a_spec, b_spec], out_specs=o_spec,
        out_shape=jax.ShapeDtypeStruct(o_shape, out_dtype),
        scratch_shapes=[pltpu.VMEM((bm, bn), F32)],
        compiler_params=_cp("parallel", "parallel", "parallel", "arbitrary"),
    )(a, b)


def _norm_proj(h, nw, wbig, wsmall, name, carry=None):
    T = h.shape[0]
    tm = min(512, T)
    tn = 1536
    ex_in, ex_args, ex_out, ex_shape, ex_sems = _carry_specs(carry)

    def body(h_ref, nw_ref, wb_hbm, ws_ref, pb_ref, ps_ref, n_ref, wb_v, sem):
        @pl.when(pl.program_id(0) == 0)
        def _():
            _load_weights(((wb_hbm, wb_v),), sem)

        x = h_ref[...]
        r = lax.rsqrt(jnp.mean(x * x, axis=-1, keepdims=True) + EPS)
        nb = _b((x * r) * nw_ref[...])
        n_ref[...] = nb
        ps_ref[...] = _dot(nb, ws_ref[...])
        for j in range(BIG_COLS // tn):
            pb_ref[:, j * tn:(j + 1) * tn] = _b(_dot(nb, wb_v[:, j * tn:(j + 1) * tn]))

    row = lambda w: pl.BlockSpec((tm, w), lambda i: (i, 0))
    return pl.pallas_call(
        _carry(carry, body, 4, 3, (T // tm,)), name=name, grid=(T // tm,),
        in_specs=[row(D), pl.BlockSpec((1, D), lambda i: (0, 0)), ANY, pl.BlockSpec((D, SMALL), lambda i: (0, 0))]
        + ex_in,
        out_specs=[row(BIG_COLS), row(SMALL), row(D)] + ex_out,
        out_shape=[jax.ShapeDtypeStruct((T, BIG_COLS), BF16), jax.ShapeDtypeStruct((T, SMALL), F32),
                   jax.ShapeDtypeStruct((T, D), BF16)] + ex_shape,
        scratch_shapes=[pltpu.VMEM((D, BIG_COLS), BF16), pltpu.SemaphoreType.DMA((1,))] + ex_sems,
        compiler_params=_cp("arbitrary"),
    )(h, nw, wbig, wsmall, *ex_args)


def _proj_bwd(dh, h, nw, pieces, dsmall, wbig, wsmall, name):
    T = h.shape[0]
    tm = min(BIG_ROW_BLK, T)
    offs = tuple(int(v) for v in np.cumsum((0,) + PIECES))

    def body(dh_ref, h_ref, nw_ref, *rest):
        p_refs = rest[:len(PIECES)]
        ds_ref, wb_hbm, ws_ref, dx_ref, dnw_ref, wb_v, sem = rest[len(PIECES):]

        @pl.when(pl.program_id(0) == 0)
        def _():
            _load_weights(((wb_hbm, wb_v),), sem)
            dnw_ref[...] = jnp.zeros_like(dnw_ref)

        dn = _dot_nt(_b(ds_ref[...]), ws_ref[...])
        for p_ref, lo, wdt in zip(p_refs, offs, PIECES):
            dn += _dot_nt(p_ref[...], wb_v[:, lo:lo + wdt])
        x = h_ref[...]
        r = lax.rsqrt(jnp.mean(x * x, axis=-1, keepdims=True) + EPS)
        xhat = x * r
        dnw_ref[...] += jnp.sum(dn * xhat, axis=0, keepdims=True)
        dxhat = dn * nw_ref[...]
        dx_ref[...] = dh_ref[...] + r * (dxhat - xhat * jnp.mean(dxhat * xhat, axis=-1, keepdims=True))

    row = lambda w: pl.BlockSpec((tm, w), lambda i: (i, 0))
    one = pl.BlockSpec((1, D), lambda i: (0, 0))
    return pl.pallas_call(
        body, name=name, grid=(T // tm,),
        in_specs=[row(D), row(D), one] + [row(w) for w in PIECES] + [row(SMALL), ANY, pl.BlockSpec((D, SMALL), lambda i: (0, 0))],
        out_specs=[row(D), one],
        out_shape=[jax.ShapeDtypeStruct((T, D), F32), jax.ShapeDtypeStruct((1, D), F32)],
        scratch_shapes=[pltpu.VMEM((D, BIG_COLS), BF16), pltpu.SemaphoreType.DMA((1,))],
        compiler_params=_cp("arbitrary"),
    )(dh, h, nw, *pieces, dsmall, wbig, wsmall)


def _gla_block(q_ref, k_ref, sm_ref, wg_ref, bg_ref, nc, tril):
    nbat = GLA_H * nc
    q = _heads_first(q_ref[...].astype(F32), nc, GLA_DK)
    k = _heads_first(k_ref[...].astype(F32), nc, GLA_DK)
    pre = _heads_first(_dot(sm_ref[...], wg_ref[...], HI) + bg_ref[...], nc, GLA_DK)
    la = _log_sigmoid(pre) * (1.0 / GLA_TAU)
    bc = _bmm(jnp.broadcast_to(tril, (nbat, CHUNK, CHUNK)), la, HI)
    bl = bc[:, CHUNK - 1:CHUNK, :]
    eb = jnp.exp(bc)
    enb = jnp.exp(-bc)
    ebl = jnp.exp(bl - bc)
    q_in = q * (GLA_DK ** -0.5) * eb
    k_out = k * enb
    k_st = k * ebl
    a_ch = jnp.exp(bl)
    return pre, eb, enb, ebl, q_in, k_out, k_st, a_ch


def _gla_specs(blk, idx):
    hk, hv = GLA_H * GLA_DK, GLA_H * GLA_DV
    return [pl.BlockSpec((blk, hk), lambda j: (idx(j), 0)),
            pl.BlockSpec((blk, hk), lambda j: (idx(j), 1)),
            pl.BlockSpec((blk, hv), lambda j: (idx(j), 1)),
            pl.BlockSpec((blk, SMALL), lambda j: (idx(j), 0)),
            pl.BlockSpec((SMALL, hk), lambda j: (0, 0)),
            pl.BlockSpec((1, hk), lambda j: (0, 0))]


def _gla_fwd(pbig, psmall, wgate, bgate, name):
    T = pbig.shape[0]
    blk = min(ATT_BLK, T)
    nc = blk // CHUNK

    def body(q_ref, k_ref, v_ref, sm_ref, wg_ref, bg_ref, o_ref, ss_ref, st_ref):
        @pl.when(pl.program_id(0) == 0)
        def _():
            st_ref[...] = jnp.zeros_like(st_ref)

        causal = _iota2(CHUNK, CHUNK, 0) >= _iota2(CHUNK, CHUNK, 1)
        _, _, _, _, q_in, k_out, k_st, a_ch = _gla_block(q_ref, k_ref, sm_ref, wg_ref, bg_ref, nc, causal.astype(F32))
        v = _heads_first(v_ref[...], nc, GLA_DV)
        qb = _b(q_in)
        sc = jnp.where(causal, _bmm_nt(qb, _b(k_out)), 0.0)
        kv = _bmm_tn(v, _b(k_st))
        before = [None] * (GLA_H * nc)
        for i in range(GLA_H):
            st = st_ref[i]
            for c in range(nc):
                n = i * nc + c
                before[n] = st
                st = st * a_ch[n] + kv[n]
            st_ref[i] = st
        states = jnp.stack(before)
        ss_ref[...] = states.reshape(GLA_H, nc, GLA_DV, GLA_DK)
        o_ref[...] = _b(_heads_last(_bmm(_b(sc), v) + _bmm_nt(qb, _b(states)), nc))

    return pl.pallas_call(
        body, name=name, grid=(T // blk,),
        in_specs=_gla_specs(blk, lambda j: j),
        out_specs=[pl.BlockSpec((blk, GLA_H * GLA_DV), lambda j: (j, 0)),
                   pl.BlockSpec((GLA_H, nc, GLA_DV, GLA_DK), lambda j: (0, j, 0, 0))],
        out_shape=[jax.ShapeDtypeStruct((T, GLA_H * GLA_DV), BF16),
                   jax.ShapeDtypeStruct((GLA_H, T // CHUNK, GLA_DV, GLA_DK), F32)],
        scratch_shapes=[pltpu.VMEM((GLA_H, GLA_DV, GLA_DK), F32)],
        compiler_params=_cp("arbitrary"),
    )(pbig, pbig, pbig, psmall, wgate, bgate)


def _gla_bwd(pbig, psmall, wgate, bgate, states, do, name):
    T = pbig.shape[0]
    blk = min(ATT_BLK, T)
    nc = blk // CHUNK
    nb = T // blk
    nbat = GLA_H * nc

    def body(q_ref, k_ref, v_ref, sm_ref, wg_ref, bg_ref, ss_ref, do_ref, dq_ref, dk_ref, dv_ref, dpre_ref, dst_ref):
        @pl.when(pl.program_id(0) == 0)
        def _():
            dst_ref[...] = jnp.zeros_like(dst_ref)

        causal = _iota2(CHUNK, CHUNK, 0) >= _iota2(CHUNK, CHUNK, 1)
        triu = (_iota2(CHUNK, CHUNK, 0) <= _iota2(CHUNK, CHUNK, 1)).astype(F32)
        pre, eb, enb, ebl, q_in, k_out, k_st, a_ch = _gla_block(q_ref, k_ref, sm_ref, wg_ref, bg_ref, nc,
                                                                causal.astype(F32))
        v = _heads_first(v_ref[...], nc, GLA_DV)
        dob = _b(_heads_first(do_ref[...], nc, GLA_DV))
        st = ss_ref[...].reshape(nbat, GLA_DV, GLA_DK)
        qb, kob, kstb = _b(q_in), _b(k_out), _b(k_st)
        qdo = _bmm_tn(dob, qb)
        after = [None] * nbat
        for i in range(GLA_H):
            dst = dst_ref[i]
            for c in range(nc - 1, -1, -1):
                n = i * nc + c
                after[n] = dst
                dst = dst * a_ch[n] + qdo[n]
            dst_ref[i] = dst
        dsa = jnp.stack(after)
        dsb = _b(dsa)
        sc = jnp.where(causal, _bmm_nt(qb, kob), 0.0)
        dsc = _b(jnp.where(causal, _bmm_nt(dob, v), 0.0))
        dq_in = _bmm(dob, _b(st)) + _bmm(dsc, kob)
        dk_out = _bmm_tn(dsc, qb)
        dk_st = _bmm(v, dsb)
        dv_ref[...] = _b(_heads_last(_bmm_tn(_b(sc), dob) + _bmm_nt(kstb, dsb), nc))
        da_ch = jnp.sum(st * dsa, axis=1, keepdims=True)
        tk = dk_st * k_st
        db = dq_in * q_in - dk_out * k_out - tk
        db_last = jnp.sum(tk, axis=1, keepdims=True) + da_ch * a_ch
        dq_ref[...] = _b(_heads_last(dq_in * (GLA_DK ** -0.5) * eb, nc))
        dk_ref[...] = _b(_heads_last(dk_out * enb + dk_st * ebl, nc))
        dla = _bmm(jnp.broadcast_to(triu, (nbat, CHUNK, CHUNK)), db, HI) + db_last
        dpre_ref[...] = _heads_last(dla * (1.0 / GLA_TAU) * _sigmoid(-pre), nc)

    r = lambda j: nb - 1 - j
    hk, hv = GLA_H * GLA_DK, GLA_H * GLA_DV
    return pl.pallas_call(
        body, name=name, grid=(nb,),
        in_specs=_gla_specs(blk, r) + [pl.BlockSpec((GLA_H, nc, GLA_DV, GLA_DK), lambda j: (0, r(j), 0, 0)),
                                      pl.BlockSpec((blk, hv), lambda j: (r(j), 0))],
        out_specs=[pl.BlockSpec((blk, hk), lambda j: (r(j), 0)), pl.BlockSpec((blk, hk), lambda j: (r(j), 0)),
                   pl.BlockSpec((blk, hv), lambda j: (r(j), 0)), pl.BlockSpec((blk, hk), lambda j: (r(j), 0))],
        out_shape=[jax.ShapeDtypeStruct((T, hk), BF16), jax.ShapeDtypeStruct((T, hk), BF16),
                   jax.ShapeDtypeStruct((T, hv), BF16), jax.ShapeDtypeStruct((T, hk), F32)],
        scratch_shapes=[pltpu.VMEM((GLA_H, GLA_DV, GLA_DK), F32)],
        compiler_params=_cp("arbitrary"),
    )(pbig, pbig, pbig, psmall, wgate, bgate, states, do)


def _gla_gate_bwd(dpre, psmall, wgate, dsm, name):
    T = dpre.shape[0]
    tm = min(512, T)
    W = GLA_H * GLA_DK
    ngrp = dsm.shape[0]

    def body(dp_ref, sm_ref, wg_ref, dsm_ref, ds_ref, dw_ref, db_ref):
        @pl.when(pl.program_id(0) == 0)
        def _():
            dw_ref[...] = jnp.zeros_like(dw_ref)
            db_ref[...] = jnp.zeros_like(db_ref)

        dp = dp_ref[...]
        ds = _dot_nt(dp, wg_ref[...], HI)
        for i in range(ngrp):
            ds += dsm_ref[i]
        ds_ref[...] = ds
        dw_ref[...] += _dot_tn(sm_ref[...], dp, HI)
        db_ref[...] += jnp.sum(dp, axis=0, keepdims=True)

    return pl.pallas_call(
        body, name=name, grid=(T // tm,),
        in_specs=[pl.BlockSpec((tm, W), lambda i: (i, 0)), pl.BlockSpec((tm, SMALL), lambda i: (i, 0)),
                  pl.BlockSpec((SMALL, W), lambda i: (0, 0)), pl.BlockSpec((ngrp, tm, SMALL), lambda i: (0, i, 0))],
        out_specs=[pl.BlockSpec((tm, SMALL), lambda i: (i, 0)), pl.BlockSpec((SMALL, W), lambda i: (0, 0)),
                   pl.BlockSpec((1, W), lambda i: (0, 0))],
        out_shape=[jax.ShapeDtypeStruct((T, SMALL), F32), jax.ShapeDtypeStruct((SMALL, W), F32),
                   jax.ShapeDtypeStruct((1, W), F32)],
        compiler_params=_cp("arbitrary"),
    )(dpre, psmall, wgate, dsm)


CONV_C = 3 * 1024
CONV_BLK = 256


def _conv_fwd(pbig, cw8, name):
    T = pbig.shape[0]
    blk = min(CONV_BLK, T)

    def body(x_ref, w_ref, c_ref, prev_ref):
        @pl.when(pl.program_id(0) == 0)
        def _():
            prev_ref[...] = jnp.zeros_like(prev_ref)

        x = x_ref[...].astype(F32)
        prev = prev_ref[...]
        row8 = _iota2(8, CONV_C, 0)
        acc = x * w_ref[CONV_K - 1:CONV_K, :]
        for s in range(1, CONV_K):
            xs = pltpu.roll(x, s, 0)
            top = jnp.where(row8 < s, pltpu.roll(prev, s, 0), xs[:8])
            xs = jnp.concatenate([top, xs[8:]], axis=0)
            acc += xs * w_ref[CONV_K - 1 - s:CONV_K - s, :]
        c_ref[...] = _b(acc)
        prev_ref[...] = x[blk - 8:]

    return pl.pallas_call(
        body, name=name, grid=(T // blk,),
        in_specs=[pl.BlockSpec((blk, CONV_C), lambda i: (i, 1)), pl.BlockSpec((8, CONV_C), lambda i: (0, 0))],
        out_specs=pl.BlockSpec((blk, CONV_C), lambda i: (i, 0)),
        out_shape=jax.ShapeDtypeStruct((T, CONV_C), BF16),
        scratch_shapes=[pltpu.VMEM((8, CONV_C), F32)],
        compiler_params=_cp("arbitrary"),
    )(pbig, cw8)


def _conv_bwd(dcq, dck, dcv, pbig, cw8, name):
    T = pbig.shape[0]
    blk = min(CONV_BLK, T)
    nb = T // blk

    def body(dq_ref, dk_ref, dv_ref, x_ref, w_ref, dx_ref, dw_ref, nxt_ref):
        @pl.when(pl.program_id(0) == 0)
        def _():
            nxt_ref[...] = jnp.zeros_like(nxt_ref)
            dw_ref[...] = jnp.zeros_like(dw_ref)

        dc = jnp.concatenate([dq_ref[...], dk_ref[...], dv_ref[...]], axis=1).astype(F32)
        x = x_ref[...].astype(F32)
        nxt = nxt_ref[...]
        row8 = _iota2(8, CONV_C, 0)
        acc = dc * w_ref[CONV_K - 1:CONV_K, :]
        dws = [jnp.sum(dc * x, axis=0, keepdims=True)]
        for s in range(1, CONV_K):
            ds = pltpu.roll(dc, blk - s, 0)
            bot = jnp.where(row8 >= 8 - s, pltpu.roll(nxt, 8 - s, 0), ds[blk - 8:])
            ds = jnp.concatenate([ds[:blk - 8], bot], axis=0)
            acc += ds * w_ref[CONV_K - 1 - s:CONV_K - s, :]
            dws.append(jnp.sum(ds * x, axis=0, keepdims=True))
        dx_ref[...] = _b(acc)
        dw_ref[...] += jnp.concatenate(dws[::-1] + [jnp.zeros((8 - CONV_K, CONV_C), F32)], axis=0)
        nxt_ref[...] = dc[:8]

    part = pl.BlockSpec((blk, 1024), lambda i: (nb - 1 - i, 0))
    return pl.pallas_call(
        body, name=name, grid=(nb,),
        in_specs=[part, part, part, pl.BlockSpec((blk, CONV_C), lambda i: (nb - 1 - i, 1)),
                  pl.BlockSpec((8, CONV_C), lambda i: (0, 0))],
        out_specs=[pl.BlockSpec((blk, CONV_C), lambda i: (nb - 1 - i, 0)), pl.BlockSpec((8, CONV_C), lambda i: (0, 0))],
        out_shape=[jax.ShapeDtypeStruct((T, CONV_C), BF16), jax.ShapeDtypeStruct((8, CONV_C), F32)],
        scratch_shapes=[pltpu.VMEM((8, CONV_C), F32)],
        compiler_params=_cp("arbitrary"),
    )(dcq, dck, dcv, pbig, cw8)


def _col(x, lane):
    sel = _iota2(x.shape[0], x.shape[1], 1) == lane
    return jnp.broadcast_to(jnp.sum(jnp.where(sel, x, 0.0), axis=1, keepdims=True), x.shape)


def _bmm(a, b, prec=None):
    return jnp.einsum("bij,bjk->bik", a, b, preferred_element_type=F32, precision=prec)


def _bmm_nt(a, b, prec=None):
    return jnp.einsum("bij,bkj->bik", a, b, preferred_element_type=F32, precision=prec)


def _bmm_tn(a, b, prec=None):
    return jnp.einsum("bji,bjk->bik", a, b, preferred_element_type=F32, precision=prec)


def _unit_lower_inverse(low):
    eye = (_iota2(CHUNK, CHUNK, 0) == _iota2(CHUNK, CHUNK, 1)).astype(F32)
    xk = -low
    inv = eye + xk
    for _ in range(5):
        xb = _b(xk)
        xk = _bmm(xb, xb)
        inv = inv + _bmm(_b(inv), _b(xk))
    resid = eye - _bmm(eye + low, inv, HI)
    return inv + _bmm(inv, resid, HI)


def _heads_first(x, nc, w=128):
    hb = x.shape[1] // w
    return jnp.concatenate([x[:, i * w:(i + 1) * w].reshape(nc, CHUNK, w) for i in range(hb)], axis=0)


def _heads_last(x, nc):
    hb = x.shape[0] // nc
    return jnp.concatenate([x[i * nc:(i + 1) * nc].reshape(nc * CHUNK, x.shape[2]) for i in range(hb)], axis=1)


def _gdn_block(cq_ref, ck_ref, cv_ref, sm_ref, par_ref, h0, hb, nc, masks, solved=None):
    causal, strict, tril, eye = masks
    nbat = hb * nc
    cq = _heads_first(cq_ref[...].astype(F32), nc)
    ck = _heads_first(ck_ref[...].astype(F32), nc)
    cv = _heads_first(cv_ref[...].astype(F32), nc)
    sq, sk, sv = _sigmoid(cq), _sigmoid(ck), _sigmoid(cv)
    q, k, v = cq * sq, ck * sk, cv * sv
    rq = lax.rsqrt(jnp.sum(q * q, axis=-1, keepdims=True) + EPS)
    rk = lax.rsqrt(jnp.sum(k * k, axis=-1, keepdims=True) + EPS)
    qh, kn = q * rq, k * rk
    qn = qh * (DN_DK ** -0.5)
    sm = sm_ref[...]
    per_head = lambda fn: jnp.concatenate([fn(i) for i in range(hb)], axis=0)
    braw = per_head(lambda i: _col(sm, GLA_RANK + h0 + i).reshape(nc, CHUNK, 128))
    araw = per_head(lambda i: _col(sm, GLA_RANK + DN_H + h0 + i).reshape(nc, CHUNK, 128))
    ea = per_head(lambda i: jnp.broadcast_to(jnp.exp(par_ref[i, 0:1, :])[None], (nc, 1, 128)))
    bias = per_head(lambda i: jnp.broadcast_to(par_ref[i, 1:2, :][None], (nc, 1, 128)))
    beta = _sigmoid(braw)
    sp_arg = araw + bias
    g = -ea * _softplus(sp_arg)
    G = _bmm(jnp.broadcast_to(tril, (nbat, CHUNK, CHUNK)), g, HI)
    gc = G[:, :, :CHUNK]
    grow = jnp.sum(eye * gc, axis=1, keepdims=True)
    decay = jnp.exp(jnp.where(causal, gc - grow, -1e30))
    kb = kn * beta
    A = _bmm_nt(_b(kb), _b(kn))
    eG = jnp.exp(G)
    gl = G[:, CHUNK - 1:CHUNK, :]
    eGl = jnp.exp(gl - G)
    g_ch = jnp.exp(gl)
    rv = v * beta
    rkk = kb * eG
    if solved is None:
        tinv_b = _b(_unit_lower_inverse(jnp.where(strict, A * decay, 0.0)))
        u = _bmm(tinv_b, _b(rv))
        w = _b(_bmm(tinv_b, _b(rkk)))
    else:
        tinv_b, u, w = solved
    B = _bmm_nt(_b(qn), _b(kn))
    qk = jnp.where(causal, B * decay, 0.0)
    q_dec = qn * eG
    k_st = kn * eGl
    return dict(cq=cq, ck=ck, cv=cv, sq=sq, sk=sk, sv=sv, q=q, k=k, v=v, rq=rq, rk=rk, qh=qh, kn=kn, qn=qn,
                beta=beta, ea=ea, sp_arg=sp_arg, g=g, G=G, decay=decay, kb=kb, A=A, tinv_b=tinv_b, eG=eG, eGl=eGl,
                g_ch=g_ch, rv=rv, rkk=rkk, u=u, w=w, B=B, qk=qk, q_dec=q_dec, k_st=k_st)


def _gdn_masks():
    r, c = _iota2(CHUNK, CHUNK, 0), _iota2(CHUNK, CHUNK, 1)
    return r >= c, r > c, (r >= c).astype(F32), (r == c).astype(F32)


def _gdn_specs(blk, hb, idx):
    ng = DN_H // hb
    return [pl.BlockSpec((blk, hb * DN_DK), lambda h, j: (idx(j), h)),
            pl.BlockSpec((blk, hb * DN_DK), lambda h, j: (idx(j), ng + h)),
            pl.BlockSpec((blk, hb * DN_DV), lambda h, j: (idx(j), 2 * ng + h)),
            pl.BlockSpec((blk, SMALL), lambda h, j: (idx(j), 0)),
            pl.BlockSpec((hb, 8, 128), lambda h, j: (h, 0, 0))]


def _gdn_solved_specs(blk, hb, idx):
    nc = blk // CHUNK
    spec = lambda w: pl.BlockSpec((hb, nc, CHUNK, w), lambda h, j: (h, idx(j), 0, 0))
    return [spec(CHUNK), spec(DN_DV), spec(DN_DK)]


def _gdn_fwd(conv, psmall, par, name):
    T = conv.shape[0]
    blk = min(GDN_BLK, T)
    nc = blk // CHUNK
    hb = GDN_HEADS
    N = T // CHUNK

    def body(cq_ref, ck_ref, cv_ref, sm_ref, par_ref, o_ref, ss_ref, ti_ref, u_ref, w_ref, s_ref):
        @pl.when(pl.program_id(1) == 0)
        def _():
            s_ref[...] = jnp.zeros_like(s_ref)

        f = _gdn_block(cq_ref, ck_ref, cv_ref, sm_ref, par_ref, pl.program_id(0) * hb, hb, nc, _gdn_masks())
        ti_ref[...] = f["tinv_b"].reshape(hb, nc, CHUNK, CHUNK)
        u_ref[...] = f["u"].reshape(hb, nc, CHUNK, DN_DV)
        w_ref[...] = f["w"].reshape(hb, nc, CHUNK, DN_DK)
        wb, ub, kstb, qkb = f["w"], _b(f["u"]), _b(f["k_st"]), _b(f["qk"])
        mix = _b(_bmm_tn(kstb, wb))
        add = _bmm_tn(kstb, ub)
        q_eff = _b(f["q_dec"] - _bmm(qkb, wb))
        before = [None] * (hb * nc)
        S = [s_ref[i] for i in range(hb)]
        for c in range(nc):
            for i in range(hb):
                n = i * nc + c
                before[n] = S[i]
                S[i] = S[i] * f["g_ch"][n] - _dot(mix[n], _b(S[i])) + add[n]
        for i in range(hb):
            s_ref[i] = S[i]
        states = jnp.stack(before)
        ss_ref[...] = states.reshape(hb, nc, DN_DK, DN_DV)
        o_ref[...] = _b(_heads_last(_bmm(qkb, ub) + _bmm(q_eff, _b(states)), nc))

    return pl.pallas_call(
        body, name=name, grid=(DN_H // hb, T // blk),
        in_specs=_gdn_specs(blk, hb, lambda j: j),
        out_specs=[pl.BlockSpec((blk, hb * DN_DV), lambda h, j: (j, h)),
                   pl.BlockSpec((hb, nc, DN_DK, DN_DV), lambda h, j: (h, j, 0, 0))]
        + _gdn_solved_specs(blk, hb, lambda j: j),
        out_shape=[jax.ShapeDtypeStruct((T, DN_H * DN_DV), BF16), jax.ShapeDtypeStruct((DN_H, N, DN_DK, DN_DV), F32),
                   jax.ShapeDtypeStruct((DN_H, N, CHUNK, CHUNK), BF16), jax.ShapeDtypeStruct((DN_H, N, CHUNK, DN_DV), F32),
                   jax.ShapeDtypeStruct((DN_H, N, CHUNK, DN_DK), BF16)],
        scratch_shapes=[pltpu.VMEM((hb, DN_DK, DN_DV), F32)],
        compiler_params=_cp("parallel", "arbitrary"),
    )(conv, conv, conv, psmall, par)


def _gdn_bwd(conv, psmall, par, states, do, solved, name, carry=None):
    ex_in, ex_args, ex_out, ex_shape, ex_sems = _carry_specs(carry)
    T = conv.shape[0]
    blk = min(GDN_BLK, T)
    nc = blk // CHUNK
    nb = T // blk
    hb = GDN_HEADS
    nbat = hb * nc
    rsum = lambda x: jnp.sum(x, axis=-1, keepdims=True)

    def body(cq_ref, ck_ref, cv_ref, sm_ref, par_ref, ss_ref, do_ref, ti_ref, u_ref, w_ref,
             dcq_ref, dck_ref, dcv_ref, dsm_ref, dpar_ref, ds_ref):
        @pl.when(pl.program_id(1) == 0)
        def _():
            ds_ref[...] = jnp.zeros_like(ds_ref)
            dpar_ref[...] = jnp.zeros_like(dpar_ref)

        masks = _gdn_masks()
        causal, strict, tril, eye = masks
        triu = (_iota2(CHUNK, CHUNK, 0) <= _iota2(CHUNK, CHUNK, 1)).astype(F32)
        lane = _iota2(CHUNK, 128, 1)
        last_row = _iota2(CHUNK, 128, 0) == CHUNK - 1
        h0 = pl.program_id(0) * hb
        solved = (ti_ref[...].reshape(nbat, CHUNK, CHUNK), u_ref[...].reshape(nbat, CHUNK, DN_DV),
                  w_ref[...].reshape(nbat, CHUNK, DN_DK))
        f = _gdn_block(cq_ref, ck_ref, cv_ref, sm_ref, par_ref, h0, hb, nc, masks, solved)
        S = ss_ref[...].reshape(nbat, DN_DK, DN_DV)
        Sb = _b(S)
        do_ = _b(_heads_first(do_ref[...], nc))
        wb, qdb, kstb, qkb = _b(f["w"]), _b(f["q_dec"]), _b(f["k_st"]), _b(f["qk"])
        vnb = _b(f["u"] - _bmm(wb, Sb))
        dvn0 = _bmm_tn(qkb, do_)
        qdo = _bmm_tn(qdb, do_)
        dS = [ds_ref[i] for i in range(hb)]
        after = [None] * nbat
        for c in range(nc - 1, -1, -1):
            for i in range(hb):
                n = i * nc + c
                after[n] = dS[i]
                dvn_c = _b(dvn0[n] + _dot(kstb[n], _b(dS[i])))
                dS[i] = dS[i] * f["g_ch"][n] + qdo[n] - _dot_tn(wb[n], dvn_c)
        for i in range(hb):
            ds_ref[i] = dS[i]
        dSa = jnp.stack(after)
        dSb = _b(dSa)
        dvn = dvn0 + _bmm(kstb, dSb)
        dvnb = _b(dvn)
        dq_dec = _bmm_nt(do_, Sb)
        dqk = jnp.where(causal, _bmm_nt(do_, vnb), 0.0)
        dk_st = _bmm_nt(vnb, dSb)
        dg_ch = jnp.sum(rsum(S * dSa), axis=1, keepdims=True)
        dw = -_bmm_nt(dvnb, Sb)
        drv = _bmm_tn(f["tinv_b"], dvnb)
        drk = _bmm_tn(f["tinv_b"], _b(dw))
        dlow = jnp.where(strict, -(_bmm_nt(_b(drv), _b(f["u"])) + _bmm_nt(_b(drk), wb)), 0.0)
        dv = drv * f["beta"]
        dbeta = rsum(drv * f["v"])
        dkb = drk * f["eG"]
        dG = rsum(drk * f["rkk"])
        dA = dlow * f["decay"]
        ddec = dlow * f["A"]
        dkb += _bmm(_b(dA), _b(f["kn"]))
        dkn = _bmm_tn(_b(dA), _b(f["kb"]))
        dB = dqk * f["decay"]
        ddec += dqk * f["B"]
        dqn = _bmm(_b(dB), _b(f["kn"]))
        dkn += _bmm_tn(_b(dB), _b(f["qn"]))
        dD = ddec * f["decay"]
        dG += rsum(dD) - rsum(eye * jnp.sum(dD, axis=1, keepdims=True))
        dqn += dq_dec * f["eG"]
        dG += rsum(dq_dec * f["q_dec"])
        dkn += dk_st * f["eGl"]
        tks = rsum(dk_st * f["k_st"])
        dG -= tks
        dG_last = jnp.sum(tks, axis=1, keepdims=True) + dg_ch * f["g_ch"][:, :, :1]
        dkn += dkb * f["beta"]
        dbeta += rsum(dkb * f["kn"])
        dGf = jnp.broadcast_to(dG, (nbat, CHUNK, 128)) + jnp.where(last_row, dG_last, 0.0)
        dg = _bmm(jnp.broadcast_to(triu, (nbat, CHUNK, CHUNK)), dGf, HI)
        dbraw = dbeta * f["beta"][:, :, :1] * (1.0 - f["beta"][:, :, :1])
        daraw = dg * (-f["ea"]) * _sigmoid(f["sp_arg"])
        both = lambda t: jnp.sum(jnp.sum(t, axis=1, keepdims=True), axis=0)
        dgg = dg * f["g"]
        dsm = jnp.zeros((nc, CHUNK, SMALL), F32)
        for i in range(hb):
            mine = slice(i * nc, (i + 1) * nc)
            dsm += (jnp.where(lane == GLA_RANK + h0 + i, dbraw[mine], 0.0)
                    + jnp.where(lane == GLA_RANK + DN_H + h0 + i, daraw[mine], 0.0))
            dpar = jnp.where(lane[:1] == 0, both(dgg[mine]), jnp.where(lane[:1] == 1, both(daraw[mine]), 0.0))
            dpar_ref[i] += jnp.broadcast_to(dpar, (8, 128))
        dsm_ref[0] = dsm.reshape(blk, SMALL)
        dqh = dqn * (DN_DK ** -0.5)
        dq = f["rq"] * (dqh - f["qh"] * rsum(dqh * f["qh"]))
        dk = f["rk"] * (dkn - f["kn"] * rsum(dkn * f["kn"]))
        dsilu = lambda x, s: s * (1.0 + x * (1.0 - s))
        dcq_ref[...] = _b(_heads_last(dq * dsilu(f["cq"], f["sq"]), nc))
        dck_ref[...] = _b(_heads_last(dk * dsilu(f["ck"], f["sk"]), nc))
        dcv_ref[...] = _b(_heads_last(dv * dsilu(f["cv"], f["sv"]), nc))

    r = lambda j: nb - 1 - j
    out_blk = pl.BlockSpec((blk, hb * DN_DK), lambda h, j: (r(j), h))
    grid = (DN_H // hb, nb)
    return pl.pallas_call(
        _carry(carry, body, 10, 5, grid), name=name, grid=grid,
        in_specs=_gdn_specs(blk, hb, r) + [pl.BlockSpec((hb, nc, DN_DK, DN_DV), lambda h, j: (h, r(j), 0, 0)),
                                          pl.BlockSpec((blk, hb * DN_DV), lambda h, j: (r(j), h))]
        + _gdn_solved_specs(blk, hb, r) + ex_in,
        out_specs=[out_blk, out_blk, out_blk, pl.BlockSpec((1, blk, SMALL), lambda h, j: (h, r(j), 0)),
                   pl.BlockSpec((hb, 8, 128), lambda h, j: (h, 0, 0))] + ex_out,
        out_shape=[jax.ShapeDtypeStruct((T, DN_H * DN_DK), BF16)] * 3 + [
            jax.ShapeDtypeStruct((DN_H // hb, T, SMALL), F32), jax.ShapeDtypeStruct((DN_H, 8, 128), F32)] + ex_shape,
        scratch_shapes=[pltpu.VMEM((hb, DN_DK, DN_DV), F32)] + ex_sems,
        compiler_params=_cp("arbitrary", "arbitrary"),
    )(conv, conv, conv, psmall, par, states, do, *solved, *ex_args)


def _head_norm(o, w, dv):
    outs, rs = [], []
    for i in range(o.shape[1] // dv):
        oh = o[:, i * dv:(i + 1) * dv]
        r = lax.rsqrt(jnp.mean(oh * oh, axis=-1, keepdims=True) + EPS)
        outs.append(oh * r)
        rs.append(r)
    return outs, rs


def _merge_specs(tm):
    col = lambda c: pl.BlockSpec((tm, D), lambda i: (i, c))
    return [col(0), col(0), col(2), col(6), col(7), col(8),
            pl.BlockSpec((1, GLA_DV), lambda i: (0, 0)), pl.BlockSpec((1, DN_DV), lambda i: (0, 0)),
            pl.BlockSpec((D, D), lambda i: (0, 0))]


def _merge_fwd(h, oa, ob, pbig, gla_hn, dn_hn, wout, name):
    T = h.shape[0]
    tm = min(ROW_BLK, T)

    def body(h_ref, oa_ref, ob_ref, gr_ref, dg_ref, ma_ref, mb_ref, wa_ref, wb_ref, wo_ref, ho_ref, y_ref):
        na, _ = _head_norm(oa_ref[...].astype(F32), wa_ref[...], GLA_DV)
        nbs, _ = _head_norm(ob_ref[...].astype(F32), wb_ref[...], DN_DV)
        hna = jnp.concatenate([t * wa_ref[...] for t in na], axis=1)
        hnb = jnp.concatenate([t * wb_ref[...] for t in nbs], axis=1)
        gr = gr_ref[...].astype(F32)
        dg = dg_ref[...].astype(F32)
        y = (_sigmoid(ma_ref[...].astype(F32)) * hna * (gr * _sigmoid(gr))
             + _sigmoid(mb_ref[...].astype(F32)) * hnb * (dg * _sigmoid(dg)))
        yb = _b(y)
        y_ref[...] = yb
        ho_ref[...] = h_ref[...] + _dot(yb, wo_ref[...])

    row = pl.BlockSpec((tm, D), lambda i: (i, 0))
    return pl.pallas_call(
        body, name=name, grid=(T // tm,),
        in_specs=[row] + _merge_specs(tm),
        out_specs=[row, row],
        out_shape=[jax.ShapeDtypeStruct((T, D), F32), jax.ShapeDtypeStruct((T, D), BF16)],
        compiler_params=_cp("arbitrary"),
    )(h, oa, ob, pbig, pbig, pbig, pbig, gla_hn, dn_hn, wout)


def _merge_bwd(dh, oa, ob, pbig, gla_hn, dn_hn, wout, name):
    T = dh.shape[0]
    tm = min(ROW_BLK, T)

    def branch(dy, o_ref, w_ref, gate_ref, m_ref, dv):
        w = w_ref[...]
        ohat, rs = _head_norm(o_ref[...].astype(F32), w, dv)
        gate = gate_ref[...].astype(F32)
        m = m_ref[...].astype(F32)
        sgate, sm = _sigmoid(gate), _sigmoid(m)
        silu = gate * sgate
        ohat_all = jnp.concatenate(ohat, axis=1)
        hn = jnp.concatenate([t * w for t in ohat], axis=1)
        d_on = dy * sm
        d_m = dy * hn * silu * sm * (1.0 - sm)
        d_hn = d_on * silu
        d_gate = d_on * hn * (sgate * (1.0 + gate * (1.0 - sgate)))
        dw = jnp.zeros((1, dv), F32)
        d_o = []
        for i, (oh, r) in enumerate(zip(ohat, rs)):
            dhn = d_hn[:, i * dv:(i + 1) * dv]
            dw += jnp.sum(dhn * oh, axis=0, keepdims=True)
            dohat = dhn * w
            d_o.append(r * (dohat - oh * jnp.mean(dohat * oh, axis=-1, keepdims=True)))
        return jnp.concatenate(d_o, axis=1), d_gate, d_m, dw

    def body(dh_ref, oa_ref, ob_ref, gr_ref, dg_ref, ma_ref, mb_ref, wa_ref, wb_ref, wo_ref,
             doa_ref, dob_ref, dgr_ref, ddg_ref, dma_ref, dmb_ref, dwa_ref, dwb_ref, dhb_ref):
        @pl.when(pl.program_id(0) == 0)
        def _():
            dwa_ref[...] = jnp.zeros_like(dwa_ref)
            dwb_ref[...] = jnp.zeros_like(dwb_ref)

        dhb = _b(dh_ref[...])
        dhb_ref[...] = dhb
        dy = _dot_nt(dhb, wo_ref[...])
        d_oa, d_gr, d_ma, dwa = branch(dy, oa_ref, wa_ref, gr_ref, ma_ref, GLA_DV)
        d_ob, d_dg, d_mb, dwb = branch(dy, ob_ref, wb_ref, dg_ref, mb_ref, DN_DV)
        doa_ref[...] = _b(d_oa)
        dob_ref[...] = _b(d_ob)
        dgr_ref[...] = _b(d_gr)
        ddg_ref[...] = _b(d_dg)
        dma_ref[...] = _b(d_ma)
        dmb_ref[...] = _b(d_mb)
        dwa_ref[...] += dwa
        dwb_ref[...] += dwb

    row = pl.BlockSpec((tm, D), lambda i: (i, 0))
    f32 = jax.ShapeDtypeStruct((T, D), F32)
    b16 = jax.ShapeDtypeStruct((T, D), BF16)
    return pl.pallas_call(
        body, name=name, grid=(T // tm,),
        in_specs=[row] + _merge_specs(tm),
        out_specs=[row] * 6 + [pl.BlockSpec((1, GLA_DV), lambda i: (0, 0)), pl.BlockSpec((1, DN_DV), lambda i: (0, 0)), row],
        out_shape=[b16, b16, b16, b16, b16, b16, jax.ShapeDtypeStruct((1, GLA_DV), F32),
                   jax.ShapeDtypeStruct((1, DN_DV), F32), b16],
        compiler_params=_cp("arbitrary"),
    )(dh, oa, ob, pbig, pbig, pbig, pbig, gla_hn, dn_hn, wout)


def _loss_head(h, nw, target, name):
    T = h.shape[0]
    tm = min(512, T)

    def body(h_ref, nw_ref, t_ref, dx_ref, loss_ref, dnw_ref):
        @pl.when(pl.program_id(0) == 0)
        def _():
            loss_ref[...] = jnp.zeros_like(loss_ref)
            dnw_ref[...] = jnp.zeros_like(dnw_ref)

        x = h_ref[...]
        w = nw_ref[...]
        r = lax.rsqrt(jnp.mean(x * x, axis=-1, keepdims=True) + EPS)
        xhat = x * r
        err = xhat * w - t_ref[...]
        part = jnp.sum(jnp.sum(err * err, axis=-1, keepdims=True), axis=0, keepdims=True)
        loss_ref[...] += (0.5 / D) * part
        dout = err * (1.0 / D)
        dnw_ref[...] += jnp.sum(dout * xhat, axis=0, keepdims=True)
        dxhat = dout * w
        dx_ref[...] = r * (dxhat - xhat * jnp.mean(dxhat * xhat, axis=-1, keepdims=True))

    row = pl.BlockSpec((tm, D), lambda i: (i, 0))
    one = pl.BlockSpec((1, D), lambda i: (0, 0))
    return pl.pallas_call(
        body, name=name, grid=(T // tm,),
        in_specs=[row, one, row],
        out_specs=[row, pl.BlockSpec((8, 128), lambda i: (0, 0)), one],
        out_shape=[jax.ShapeDtypeStruct((T, D), F32), jax.ShapeDtypeStruct((8, 128), F32),
                   jax.ShapeDtypeStruct((1, D), F32)],
        compiler_params=_cp("arbitrary"),
    )(h, nw, target)


def _adamw(w, g, m, v, rows, name):
    R, C = w.shape
    rows = min(rows, R)
    c1 = 1.0 - ADAM_B1 ** ADAM_STEP
    c2 = 1.0 - ADAM_B2 ** ADAM_STEP

    def body(w_ref, g_ref, m_ref, v_ref, d_ref, mo_ref, vo_ref):
        g_ = g_ref[...]
        m_ = ADAM_B1 * m_ref[...] + (1.0 - ADAM_B1) * g_
        v_ = ADAM_B2 * v_ref[...] + (1.0 - ADAM_B2) * (g_ * g_)
        mo_ref[...] = m_
        vo_ref[...] = v_
        d_ref[...] = -ADAM_LR * ((m_ / c1) / (jnp.sqrt(v_ / c2) + ADAM_EPS) + ADAM_WD * w_ref[...])

    blk = pl.BlockSpec((rows, C), lambda i: (i, 0))
    shp = jax.ShapeDtypeStruct((R, C), F32)
    return pl.pallas_call(
        body, name=name, grid=(R // rows,),
        in_specs=[blk] * 4, out_specs=[blk] * 3, out_shape=[shp] * 3,
        compiler_params=_cp("parallel"),
    )(w, g, m, v)


def _me():
    return lax.axis_index("x"), lax.axis_index("y"), lax.axis_index("c")


def _other_chips(x, y):
    return [(1 - x, y), (x, 1 - y), (1 - x, 1 - y)]


def _half_rows(ref, hf):
    half = ref.shape[-2] // 2
    rows = pl.ds(pl.multiple_of(hf * half, 16), half)
    return ref.at[rows, :] if len(ref.shape) == 2 else ref.at[:, rows, :]


class _GatherBig:
    def __init__(self, big):
        self.arrays = list(big)
        self.out_shape = [jax.ShapeDtypeStruct((N_SHARD,) + w.shape, w.dtype) for w in big]
        self.n_sem = 7 * len(big)

    @staticmethod
    def _copy(sems, k, src, dst, to):
        return pltpu.make_async_remote_copy(src_ref=src, dst_ref=dst, send_sem=sems[0].at[k], recv_sem=sems[1].at[k],
                                            device_id=to, device_id_type=MESH)

    def start(self, ins, outs, *sems):
        x, y, c = _me()
        mine = 2 * x + y
        for i, (w_ref, o_ref) in enumerate(zip(ins, outs)):
            self._copy(sems, 7 * i + 6, w_ref, o_ref.at[mine], (x, y, 1 - c)).start()
            for j, chip in enumerate(_other_chips(x, y)):
                self._copy(sems, 7 * i + j, _half_rows(w_ref, c), _half_rows(o_ref.at[mine], c), (*chip, c)).start()

    def relay(self, ins, outs, *sems):
        x, y, c = _me()
        for i, o_ref in enumerate(outs):
            for j, chip in enumerate(_other_chips(x, y)):
                landed = _half_rows(o_ref.at[2 * chip[0] + chip[1]], c)
                self._copy(sems, 7 * i + j, landed, landed, (x, y, c)).wait_recv()
                self._copy(sems, 7 * i + 3 + j, landed, landed, (x, y, 1 - c)).start()

    def finish(self, ins, outs, *sems):
        x, y, c = _me()
        me, sibling = (x, y, c), (x, y, 1 - c)
        chips = _other_chips(x, y)
        slot = lambda chip: 2 * chip[0] + chip[1]
        for i, (w_ref, o_ref) in enumerate(zip(ins, outs)):
            for j, chip in enumerate(chips):
                passed = _half_rows(o_ref.at[slot(chip)], 1 - c)
                self._copy(sems, 7 * i + 3 + j, passed, passed, me).wait_recv()
            self._copy(sems, 7 * i + 6, o_ref.at[slot((x, y))], o_ref.at[slot((x, y))], me).wait_recv()
        for i, (w_ref, o_ref) in enumerate(zip(ins, outs)):
            self._copy(sems, 7 * i + 6, w_ref, o_ref.at[slot((x, y))], sibling).wait_send()
            for j, chip in enumerate(chips):
                self._copy(sems, 7 * i + j, _half_rows(w_ref, c), _half_rows(o_ref.at[slot((x, y))], c),
                           (*chip, c)).wait_send()
                landed = _half_rows(o_ref.at[slot(chip)], c)
                self._copy(sems, 7 * i + 3 + j, landed, landed, sibling).wait_send()


class _ChipsExchange:
    def __init__(self, pbs):
        self.arrays = list(pbs)
        self.out_shape = [jax.ShapeDtypeStruct((3,) + p.shape[1:], p.dtype) for p in pbs]
        self.n_sem = 3 * len(pbs)

    def _copies(self, ins, outs, send_sems, recv_sems):
        x, y, c = _me()
        return [pltpu.make_async_remote_copy(src_ref=ins[i].at[2 * chip[0] + chip[1]], dst_ref=outs[i].at[j],
                                             send_sem=send_sems.at[3 * i + j], recv_sem=recv_sems.at[3 * i + j],
                                             device_id=(*chip, c), device_id_type=MESH)
                for i in range(len(ins)) for j, chip in enumerate(_other_chips(x, y))]

    def start(self, ins, outs, *sems):
        for cp in self._copies(ins, outs, *sems):
            cp.start()

    def finish(self, ins, outs, *sems):
        for cp in self._copies(ins, outs, *sems):
            cp.wait()


def _carry(ex, body, n_in, n_out, grid):
    if ex is None:
        return body
    ni, no = len(ex.arrays), len(ex.out_shape)

    def carried(*refs):
        ins, ex_in = refs[:n_in], refs[n_in:n_in + ni]
        outs, ex_out = refs[n_in + ni:n_in + ni + n_out], refs[n_in + ni + n_out:n_in + ni + n_out + no]
        scratch, sems = refs[n_in + ni + n_out + no:-2], refs[-2:]
        step = functools.reduce(lambda acc, a: acc * grid[a] + pl.program_id(a), range(len(grid)), 0)
        steps = math.prod(grid)

        @pl.when(step == 0)
        def _():
            ex.start(ex_in, ex_out, *sems)

        body(*ins, *outs, *scratch)

        if hasattr(ex, "relay"):
            @pl.when(step == (3 * steps) // 4)
            def _():
                ex.relay(ex_in, ex_out, *sems)

        @pl.when(step == steps - 1)
        def _():
            ex.finish(ex_in, ex_out, *sems)

    return carried


def _carry_specs(ex):
    if ex is None:
        return [], [], [], [], []
    sems = [pltpu.SemaphoreType.DMA((ex.n_sem,)), pltpu.SemaphoreType.DMA((ex.n_sem,))]
    return [ANY] * len(ex.arrays), ex.arrays, [ANY] * len(ex.out_shape), ex.out_shape, sems


def _gather_weights(big, small, name):
    nbig, nsm = len(big), len(small)
    n = nbig + nsm
    own_sem = 6 * nbig + 3 * nsm

    def body(*refs):
        ins, outs = refs[:n], refs[n:2 * n]
        send_sems, recv_sems = refs[2 * n:]
        x, y, c = _me()
        sibling = (x, y, 1 - c)
        chips = _other_chips(x, y)
        slot = lambda chip: 2 * chip[0] + chip[1]

        def copy(k, src, dst, to):
            return pltpu.make_async_remote_copy(src_ref=src, dst_ref=dst, send_sem=send_sems.at[k],
                                                recv_sem=recv_sems.at[k], device_id=to, device_id_type=MESH)

        sent = []
        for i in range(nbig):
            sent.append(copy(own_sem + i, ins[i], outs[i].at[slot((x, y))], sibling))
            sent[-1].start()
            for j, chip in enumerate(chips):
                sent.append(copy(6 * i + j, _half_rows(ins[i], c), _half_rows(outs[i].at[slot((x, y))], c), (*chip, c)))
                sent[-1].start()
        for t in range(nsm):
            w_ref, o_ref = ins[nbig + t], outs[nbig + t]
            o_ref[slot((x, y))] = w_ref[...]
            for j, chip in enumerate(chips):
                sent.append(copy(6 * nbig + 3 * t + j, w_ref, o_ref.at[slot((x, y))], (*chip, c)))
                sent[-1].start()
        for i in range(nbig):
            for j, chip in enumerate(chips):
                landed = _half_rows(outs[i].at[slot(chip)], c)
                copy(6 * i + j, landed, landed, (x, y, c)).wait_recv()
                sent.append(copy(6 * i + 3 + j, landed, landed, sibling))
                sent[-1].start()
        for t in range(nsm):
            for j, chip in enumerate(chips):
                landed = outs[nbig + t].at[slot(chip)]
                copy(6 * nbig + 3 * t + j, landed, landed, (x, y, c)).wait_recv()
        for i in range(nbig):
            for j, chip in enumerate(chips):
                passed = _half_rows(outs[i].at[slot(chip)], 1 - c)
                copy(6 * i + 3 + j, passed, passed, (x, y, c)).wait_recv()
        for i in range(nbig):
            mine = outs[i].at[slot((x, y))]
            copy(own_sem + i, mine, mine, (x, y, c)).wait_recv()
        for cp in sent:
            cp.wait_send()

    vm = pl.BlockSpec(memory_space=pltpu.VMEM)
    nsem = own_sem + nbig
    return pl.pallas_call(
        body, name=name, in_specs=[ANY] * nbig + [vm] * nsm, out_specs=[ANY] * nbig + [vm] * nsm,
        out_shape=[jax.ShapeDtypeStruct((N_SHARD,) + w.shape, w.dtype) for w in list(big) + list(small)],
        scratch_shapes=[pltpu.SemaphoreType.DMA((nsem,)), pltpu.SemaphoreType.DMA((nsem,))],
        compiler_params=pltpu.CompilerParams(has_side_effects=True),
    )(*big, *small)


def _rs_sibling(gs, name):
    n = len(gs)

    def body(*refs):
        send_sems, recv_sems = refs[2 * n:]
        x, y, c = _me()
        cps = [pltpu.make_async_remote_copy(src_ref=_half_rows(refs[i], 1 - c), dst_ref=refs[n + i],
                                            send_sem=send_sems.at[i], recv_sem=recv_sems.at[i],
                                            device_id=(x, y, 1 - c), device_id_type=MESH) for i in range(n)]
        for cp in cps:
            cp.start()
        for cp in cps:
            cp.wait()

    return pl.pallas_call(
        body, name=name, in_specs=[ANY] * n, out_specs=[ANY] * n,
        out_shape=[jax.ShapeDtypeStruct((g.shape[0], g.shape[1] // 2, g.shape[2]), g.dtype) for g in gs],
        scratch_shapes=[pltpu.SemaphoreType.DMA((n,)), pltpu.SemaphoreType.DMA((n,))],
        compiler_params=pltpu.CompilerParams(has_side_effects=True),
    )(*gs)


def _add_pair(g, other, where, name):
    ns, a, b = g.shape
    half = a // 2

    def body(w_ref, g_ref, o_ref, pb_ref, own_ref):
        t = g_ref[0].astype(F32) + o_ref[0].astype(F32)
        pb_ref[0] = _b(t)

        @pl.when(pl.program_id(0) == w_ref[1])
        def _():
            own_ref[...] = t

    return pl.pallas_call(
        body, name=name,
        grid_spec=pltpu.PrefetchScalarGridSpec(
            num_scalar_prefetch=1, grid=(ns,),
            in_specs=[pl.BlockSpec((1, half, b), lambda s, w: (s, w[0], 0)), pl.BlockSpec((1, half, b), lambda s, w: (s, 0, 0))],
            out_specs=[pl.BlockSpec((1, half, b), lambda s, w: (s, 0, 0)), pl.BlockSpec((half, b), lambda s, w: (0, 0))]),
        out_shape=[jax.ShapeDtypeStruct((ns, half, b), BF16), jax.ShapeDtypeStruct((half, b), F32)],
        compiler_params=_cp("arbitrary"),
    )(where, g, other)


def _add_four(own, got, name):
    rows, cols = own.shape
    rb = rows // 2

    def body(a_ref, b_ref, o_ref):
        o_ref[...] = ((a_ref[...] + b_ref[0].astype(F32)) + b_ref[1].astype(F32)) + b_ref[2].astype(F32)

    return pl.pallas_call(
        body, name=name, grid=(rows // rb,),
        in_specs=[pl.BlockSpec((rb, cols), lambda i: (i, 0)), pl.BlockSpec((3, rb, cols), lambda i: (0, i, 0))],
        out_specs=pl.BlockSpec((rb, cols), lambda i: (i, 0)),
        out_shape=jax.ShapeDtypeStruct((rows, cols), F32),
        compiler_params=_cp("parallel"),
    )(own, got)


def _rs_swap(halves, name):
    n = len(halves)

    def body(*refs):
        send_sems, recv_sems = refs[2 * n:]
        x, y, c = _me()
        cps = [pltpu.make_async_remote_copy(src_ref=refs[i], dst_ref=refs[n + i], send_sem=send_sems.at[i],
                                            recv_sem=recv_sems.at[i], device_id=(x, y, 1 - c), device_id_type=MESH)
               for i in range(n)]
        for cp in cps:
            cp.start()
        for cp in cps:
            cp.wait()

    return pl.pallas_call(
        body, name=name, in_specs=[ANY] * n, out_specs=[ANY] * n,
        out_shape=[jax.ShapeDtypeStruct(h.shape, h.dtype) for h in halves],
        scratch_shapes=[pltpu.SemaphoreType.DMA((n,)), pltpu.SemaphoreType.DMA((n,))],
        compiler_params=pltpu.CompilerParams(has_side_effects=True),
    )(*halves)


def _adamw_halves(w, own, got, m, v, rows, name):
    a, b = w.shape
    nblk = a // 2 // rows
    c1 = 1.0 - ADAM_B1 ** ADAM_STEP
    c2 = 1.0 - ADAM_B2 ** ADAM_STEP

    def body(w_ref, own_ref, got_ref, m_ref, v_ref, g_ref, d_ref, mo_ref, vo_ref):
        g_ = jnp.where(pl.program_id(0) == lax.axis_index("c"), own_ref[...], got_ref[...])
        g_ref[...] = g_
        m_ = ADAM_B1 * m_ref[...] + (1.0 - ADAM_B1) * g_
        v_ = ADAM_B2 * v_ref[...] + (1.0 - ADAM_B2) * (g_ * g_)
        mo_ref[...] = m_
        vo_ref[...] = v_
        d_ref[...] = -ADAM_LR * ((m_ / c1) / (jnp.sqrt(v_ / c2) + ADAM_EPS) + ADAM_WD * w_ref[...])

    whole = pl.BlockSpec((rows, b), lambda h, i: (h * nblk + i, 0))
    part = pl.BlockSpec((rows, b), lambda h, i: (i, 0))
    shp = jax.ShapeDtypeStruct((a, b), F32)
    return pl.pallas_call(
        body, name=name, grid=(2, nblk),
        in_specs=[whole, part, part, whole, whole], out_specs=[whole] * 4, out_shape=[shp] * 4,
        compiler_params=_cp("parallel", "parallel"),
    )(w, own, got, m, v)


def _allsum_small(vec, name):
    def body(v_ref, o_ref, buf_ref, send_sems, recv_sems):
        x, y, c = _me()
        me = 4 * x + 2 * y + c
        buf_ref[me] = v_ref[...]
        cps = []
        for k in range(1, 8):
            peer = (x ^ (k >> 2), y ^ ((k >> 1) & 1), c ^ (k & 1))
            cps.append(pltpu.make_async_remote_copy(src_ref=v_ref, dst_ref=buf_ref.at[me],
                                                    send_sem=send_sems.at[k - 1], recv_sem=recv_sems.at[k - 1],
                                                    device_id=peer, device_id_type=MESH))
        for cp in cps:
            cp.start()
        for k in range(1, 8):
            peer_idx = me ^ k
            pltpu.make_async_remote_copy(src_ref=v_ref, dst_ref=buf_ref.at[peer_idx],
                                         send_sem=send_sems.at[k - 1], recv_sem=recv_sems.at[k - 1],
                                         device_id=(x, y, c), device_id_type=MESH).wait_recv()
        for cp in cps:
            cp.wait_send()
        acc = buf_ref[0]
        for d in range(1, 8):
            acc = acc + buf_ref[d]
        o_ref[...] = acc

    return pl.pallas_call(
        body, name=name,
        in_specs=[pl.BlockSpec(memory_space=pltpu.VMEM)], out_specs=pl.BlockSpec(memory_space=pltpu.VMEM),
        out_shape=jax.ShapeDtypeStruct(vec.shape, F32),
        scratch_shapes=[pltpu.VMEM((8,) + vec.shape, F32), pltpu.SemaphoreType.DMA((7,)), pltpu.SemaphoreType.DMA((7,))],
        compiler_params=pltpu.CompilerParams(has_side_effects=True),
    )(vec)


BIG = ("ffn1_w_gate", "ffn1_w_up", "ffn1_w_down", "w_in", "w_out", "ffn2_w_gate", "ffn2_w_up", "ffn2_w_down")
TINY = ("w_gla_gate", "conv_w")
SHARDED = BIG + TINY


def _join_cols(w4):
    return jnp.transpose(w4, (1, 0, 2)).reshape(w4.shape[1], N_SHARD * w4.shape[2])


def _cut_cols(w):
    return jnp.transpose(w.reshape(w.shape[0], N_SHARD, w.shape[1] // N_SHARD), (1, 0, 2))


def _split_w_in(w):
    o = IN_OFF
    big = jnp.concatenate([w[:, :o[4]], w[:, o[5]:o[9]], w[:, o[11]:]], axis=1)
    small = jnp.concatenate([w[:, o[4]:o[5]], w[:, o[9]:o[11]], jnp.zeros((w.shape[0], SMALL - 32), w.dtype)], axis=1)
    return big, small


def _merge_w_in(big, small):
    return jnp.concatenate([big[:, :3072], small[:, :16], big[:, 3072:7168], small[:, 16:32], big[:, 7168:]], axis=1)


class _Comm:
    def __init__(self, where, rest_shards):
        self.where = where
        self.w_in = _GatherBig(rest_shards[:1])
        self.later = _GatherBig(rest_shards[1:])
        self.pairs, self.got = {}, {}

    @staticmethod
    def w_in_weights(gathered):
        return dict(zip(("w_in_big", "w_in_small"), _split_w_in(_join_cols(gathered[0]))))

    @staticmethod
    def later_weights(gathered):
        W = dict(zip(BIG[4:], gathered))
        W["w_out"] = W["w_out"].reshape(D, D)
        return W

    def begin(self, names, grads):
        from_sibling = _rs_sibling(grads, "rs_sibling_" + names[0])
        for n, g, o in zip(names, grads, from_sibling):
            self.pairs[n] = _add_pair(g, o, self.where, "rs_pair_" + n)
        return _ChipsExchange([self.pairs[n][0] for n in names])

    def landed(self, names, outs):
        self.got.update(zip(names, outs))


def _local_step(x, target, W, P, comm=None):
    wgate_pad = jnp.zeros((SMALL, GLA_H * GLA_DK), F32).at[:GLA_RANK].set(P["w_gla_gate"])
    cw8 = jnp.zeros((8, CONV_C), F32).at[:CONV_K].set(P["conv_w"])
    par = jnp.zeros((DN_H, 8, 128), F32)
    par = par.at[:, 0, :].set(jnp.broadcast_to(P["dn_a_log"].reshape(DN_H, 1), (DN_H, 128)))
    par = par.at[:, 1, :].set(jnp.broadcast_to(P["dn_dt_bias"].reshape(DN_H, 1), (DN_H, 128)))

    h1, n1, g1, u1, *got = _ffn_fwd(x, P["ffn1_norm"], W["ffn1_w_gate"], W["ffn1_w_up"], W["ffn1_w_down"], "ffn1_fwd",
                                    carry=comm.w_in if comm else None)
    if comm:
        W = dict(W, **comm.w_in_weights(got))
    wbig, wsmall = W["w_in_big"], W["w_in_small"]
    pbig, psmall, n2, *got = _norm_proj(h1, P["mix_norm"], wbig, wsmall, "mix_proj", carry=comm.later if comm else None)
    if comm:
        W = dict(W, **comm.later_weights(got))
    oa, sa = _gla_fwd(pbig, psmall, wgate_pad, P["b_gla_gate"], "gla_fwd")
    conv = _conv_fwd(pbig, cw8, "conv_fwd")
    ob, sb, *solved = _gdn_fwd(conv, psmall, par, "gdn_fwd")
    h2, yb = _merge_fwd(h1, oa, ob, pbig, P["gla_head_norm"], P["dn_head_norm"], W["w_out"], "merge_fwd")
    h3, n3, g3, u3 = _ffn_fwd(h2, P["ffn2_norm"], W["ffn2_w_gate"], W["ffn2_w_up"], W["ffn2_w_down"], "ffn2_fwd")
    dh3, loss, d_final = _loss_head(h3, P["final_norm"], target, "loss_head")

    gw, gs = {}, {"final_norm": d_final}

    def ffn_grads(tag, dh, h, n, g, u, before=None):
        names = tuple(tag + s for s in ("_w_gate", "_w_up", "_w_down"))
        dg, du, act, dfb, *landed = _ffn_bwd_hidden(dh, g, u, W[names[2]], tag + "_bwd_hidden", carry=before)
        gw[names[0]] = _mm_tn(n, dg, D, FF_CUT, tag + "_dwg")
        gw[names[1]] = _mm_tn(n, du, D, FF_CUT, tag + "_dwu")
        gw[names[2]] = _mm_tn(act, dfb, FF_CUT, D, tag + "_dwd")
        dx, gs[tag + "_norm"], *own = _ffn_bwd_input(dh, h, P[tag + "_norm"], dg, du, W[names[0]], W[names[1]],
                                                     tag + "_bwd_input",
                                                     carry=comm.begin(names, [gw[n] for n in names]) if comm else None)
        if comm:
            comm.landed(names, own)
        return dx, landed

    dh2, _ = ffn_grads("ffn2", dh3, h2, n3, g3, u3)
    d_oa, d_ob, d_gr, d_dgate, d_ma, d_mb, gs["gla_head_norm"], gs["dn_head_norm"], dh2b = _merge_bwd(
        dh2, oa, ob, pbig, P["gla_head_norm"], P["dn_head_norm"], W["w_out"], "merge_bwd")
    gw["w_out"] = _mm_tn(yb, dh2b, D, D, "dw_out").reshape(N_SHARD, D // N_SHARD, D)
    early = ("w_out",)
    d_gq, d_gk, d_gv, dpre = _gla_bwd(pbig, psmall, wgate_pad, P["b_gla_gate"], sa, d_oa, "gla_bwd")
    dcq, dck, dcv, dsm, dpar, *landed = _gdn_bwd(conv, psmall, par, sb, d_ob, solved, "gdn_bwd",
                                                 carry=comm.begin(early, [gw[n] for n in early]) if comm else None)
    if comm:
        comm.landed(early, landed)
    dsmall, dwgate, gs["b_gla_gate"] = _gla_gate_bwd(dpre, psmall, wgate_pad, dsm, "gla_gate_bwd")
    gs["w_gla_gate"] = dwgate[:GLA_RANK]
    d_x3, dcw = _conv_bwd(dcq, dck, dcv, pbig, cw8, "conv_bwd")
    gs["conv_w"] = dcw[:CONV_K]
    gs["dn_a_log"] = dpar[:, 0, 0].reshape(1, DN_H)
    gs["dn_dt_bias"] = dpar[:, 0, 1].reshape(1, DN_H)
    pieces = (d_gq, d_gk, d_gv, d_gr, d_x3, d_dgate, d_ma, d_mb)
    dh1, gs["mix_norm"] = _proj_bwd(dh2, h1, P["mix_norm"], pieces, dsmall, wbig, wsmall, "proj_bwd")
    dbig = jnp.concatenate([_mm_tn(n2, p, D, 1024, "dw_in_%d" % i) for i, p in enumerate(pieces)], axis=1)
    dsml = _mm_tn(n2, dsmall, D, SMALL, "dw_in_small")
    gw["w_in"] = _cut_cols(_merge_w_in(dbig, dsml))
    grad_x, landed = ffn_grads("ffn1", dh1, x, n1, g1, u1,
                               before=comm.begin(("w_in",), [gw["w_in"]]) if comm else None)
    if comm:
        comm.landed(("w_in",), landed)
    return loss, grad_x, gw, gs


SMALL_NAMES = ("ffn1_norm", "mix_norm", "ffn2_norm", "final_norm", "b_gla_gate", "gla_head_norm", "dn_head_norm",
               "dn_a_log", "dn_dt_bias")
ROW4 = (("b_gla_gate", 512), ("gla_head_norm", 256), ("dn_head_norm", 128), ("dn_a_log", 8), ("dn_dt_bias", 8))


def _pack_small(d, loss=None):
    row4 = [d[n].reshape(-1) for n, _ in ROW4]
    row4.append(jnp.zeros((1,), F32) if loss is None else loss.reshape(1))
    row4 = jnp.concatenate(row4)
    row4 = jnp.pad(row4, (0, D - row4.shape[0]))
    rows = [d[n].reshape(-1) for n in SMALL_NAMES[:4]] + [row4]
    return jnp.concatenate([jnp.stack(rows), jnp.zeros((3, D), F32)], axis=0)


def _unpack_small(a, like):
    out = {n: a[i].reshape(like[n].shape) for i, n in enumerate(SMALL_NAMES[:4])}
    off = 0
    for n, w in ROW4:
        out[n] = a[4, off:off + w].reshape(like[n].shape)
        off += w
    return out, a[4, off]


WEIGHT_ORDER = ("ffn1_norm", "ffn1_w_gate", "ffn1_w_up", "ffn1_w_down", "mix_norm", "w_in", "w_gla_gate", "b_gla_gate",
                "conv_w", "dn_a_log", "dn_dt_bias", "gla_head_norm", "dn_head_norm", "w_out", "ffn2_norm",
                "ffn2_w_gate", "ffn2_w_up", "ffn2_w_down", "final_norm")
ADAM_ROWS = {"ffn1_w_gate": 256, "ffn1_w_up": 256, "ffn1_w_down": 176, "w_in": 128, "w_gla_gate": 16, "conv_w": 4,
             "w_out": 64, "ffn2_w_gate": 256, "ffn2_w_up": 256, "ffn2_w_down": 176}


def kernel(x, ffn1_norm, ffn1_w_gate, ffn1_w_up, ffn1_w_down, mix_norm, w_in, w_gla_gate, b_gla_gate, conv_w, dn_a_log, dn_dt_bias, gla_head_norm, dn_head_norm, w_out, ffn2_norm, ffn2_w_gate, ffn2_w_up, ffn2_w_down, final_norm, loss_target, m_ffn1_norm, m_ffn1_w_gate, m_ffn1_w_up, m_ffn1_w_down, m_mix_norm, m_w_in, m_w_gla_gate, m_b_gla_gate, m_conv_w, m_dn_a_log, m_dn_dt_bias, m_gla_head_norm, m_dn_head_norm, m_w_out, m_ffn2_norm, m_ffn2_w_gate, m_ffn2_w_up, m_ffn2_w_down, m_final_norm, v_ffn1_norm, v_ffn1_w_gate, v_ffn1_w_up, v_ffn1_w_down, v_mix_norm, v_w_in, v_w_gla_gate, v_b_gla_gate, v_conv_w, v_dn_a_log, v_dn_dt_bias, v_gla_head_norm, v_dn_head_norm, v_w_out, v_ffn2_norm, v_ffn2_w_gate, v_ffn2_w_up, v_ffn2_w_down, v_final_norm):
    given = dict(locals())
    wts = {n: given[n] for n in WEIGHT_ORDER}
    moms = {n: given["m_" + n] for n in WEIGHT_ORDER}
    vars_ = {n: given["v_" + n] for n in WEIGHT_ORDER}
    two_d = lambda a: a.reshape(a.shape[-2], a.shape[-1]) if a.ndim == 3 else a.reshape(1, -1)
    shard = {n: two_d(wts[n]) for n in SHARDED}

    gathered = _gather_weights([shard[n].astype(BF16) for n in BIG[:3]], [shard[n] for n in TINY], "gather_first")
    W = dict(zip(BIG[:3], gathered))
    P = {n: two_d(wts[n]) for n in SMALL_NAMES}
    for n, g in zip(TINY, gathered[3:]):
        P[n] = _join_cols(g)

    my_slot = 2 * lax.axis_index("x") + lax.axis_index("y")
    where = jnp.stack([lax.axis_index("c"), my_slot]).astype(jnp.int32)
    comm = _Comm(where, [shard[n].astype(BF16) for n in BIG[3:]])
    loss, grad_x, gw, gs = _local_step(x[0], loss_target[0], W, P, comm)
    halves = [_add_four(comm.pairs[n][1], comm.got[n], "rs_four_" + n) for n in BIG]
    other_halves = _rs_swap(halves, "rs_swap")

    tiny_rows = jnp.concatenate([gs["w_gla_gate"].reshape(8, D), gs["conv_w"].reshape(12, D), jnp.zeros((4, D), F32)])
    all_sum = _allsum_small(jnp.concatenate([_pack_small(gs, loss[0, 0]), tiny_rows]), "allsum_small")
    small_sum = all_sum[:8]
    small_g, loss_total = _unpack_small(small_sum, P)

    grads, delta, new_m, new_v = {}, {}, {}, {}
    for n, own, got in zip(BIG, halves, other_halves):
        res = _adamw_halves(shard[n], own, got, two_d(moms[n]), two_d(vars_[n]), ADAM_ROWS[n], "adamw_" + n)
        grads[n], delta[n], new_m[n], new_v[n] = (t.reshape(wts[n].shape) for t in res)
    for n, rows in (("w_gla_gate", all_sum[8:16]), ("conv_w", all_sum[16:28])):
        cols = shard[n].shape[1]
        grads[n] = lax.dynamic_slice_in_dim(rows.reshape(shard[n].shape[0], N_SHARD * cols), my_slot * cols, cols, axis=1)
        d, m_, v_ = _adamw(shard[n], grads[n], two_d(moms[n]), two_d(vars_[n]), ADAM_ROWS[n], "adamw_" + n)
        delta[n], new_m[n], new_v[n] = (t.reshape(wts[n].shape) for t in (d, m_, v_))
    pk = lambda src: _pack_small({n: two_d(src[n]) for n in SMALL_NAMES})
    sd, sm_, sv_ = _adamw(pk(wts), small_sum, pk(moms), pk(vars_), 8, "adamw_small")
    for res, dst in ((sd, delta), (sm_, new_m), (sv_, new_v)):
        u, _ = _unpack_small(res, wts)
        dst.update(u)
    grad_w = {n: grads[n].reshape(wts[n].shape) for n in SHARDED}
    grad_w.update({n: small_g[n].reshape(wts[n].shape) for n in SMALL_NAMES})
    return (loss_total, grad_x[None], *[grad_w[n] for n in WEIGHT_ORDER], *[delta[n] for n in WEIGHT_ORDER],
            *[new_m[n] for n in WEIGHT_ORDER], *[new_v[n] for n in WEIGHT_ORDER])
```

```python
import functools
import math

import numpy as np
import jax
import jax.numpy as jnp
from jax import lax
from jax.experimental import pallas as pl
from jax.experimental.pallas import tpu as pltpu

F32 = jnp.float32
BF16 = jnp.bfloat16
HI = lax.Precision.HIGH
MESH = pl.DeviceIdType.MESH
ANY = pl.BlockSpec(memory_space=pl.ANY)

EPS = 1e-6
D = 1024
DFF = 2816
FFN_RES = 0.5
GLA_H, GLA_DK, GLA_DV, GLA_RANK, GLA_TAU = 4, 128, 256, 16, 16.0
DN_H, DN_DK, DN_DV = 8, 128, 128
CONV_K = 4
CHUNK = 64
N_SHARD = 4
FF_CUT = DFF // N_SHARD
ADAM_LR, ADAM_B1, ADAM_B2, ADAM_EPS, ADAM_WD, ADAM_STEP = 0.001, 0.9, 0.999, 1e-08, 0.01, 10

IN_SIZES = (512, 512, 1024, 1024, 16, 1024, 1024, 1024, 1024, 8, 8, 1024, 1024)
IN_OFF = tuple(int(v) for v in np.cumsum((0,) + IN_SIZES))
BIG_COLS = 9216
SMALL = 128
PIECES = (512, 512, 1024, 1024, 3072, 1024, 1024, 1024)

VMEM_LIMIT = 56 * 1024 * 1024
ROW_BLK = 256
BIG_ROW_BLK = 512
ATT_BLK = 512
GDN_BLK = 256
GDN_HEADS = 8


def _cp(*sem):
    return pltpu.CompilerParams(dimension_semantics=sem, vmem_limit_bytes=VMEM_LIMIT)


def _sigmoid(x):
    return 1.0 / (1.0 + jnp.exp(-x))


def _softplus(x):
    return jnp.maximum(x, 0.0) + jnp.log(1.0 + jnp.exp(-jnp.abs(x)))


def _log_sigmoid(x):
    return jnp.minimum(x, 0.0) - jnp.log(1.0 + jnp.exp(-jnp.abs(x)))


def _dot(a, b, prec=None):
    return jnp.dot(a, b, preferred_element_type=F32, precision=prec)


def _dot_nt(a, b, prec=None):
    return lax.dot_general(a, b, (((1,), (1,)), ((), ())), preferred_element_type=F32, precision=prec)


def _dot_tn(a, b, prec=None):
    return lax.dot_general(a, b, (((0,), (0,)), ((), ())), preferred_element_type=F32, precision=prec)


def _b(x):
    return x.astype(BF16)


def _iota2(n, m, axis):
    return lax.broadcasted_iota(jnp.int32, (n, m), axis)


def _load_weights(pairs, sem):
    copies = [pltpu.make_async_copy(s, d, sem.at[i]) for i, (s, d) in enumerate(pairs)]
    for c in copies:
        c.start()
    for c in copies:
        c.wait()


def _ffn_fwd(h, nw, wg, wu, wd, name, carry=None):
    T = h.shape[0]
    tm = min(BIG_ROW_BLK, T)
    ex_in, ex_args, ex_out, ex_shape, ex_sems = _carry_specs(carry)

    def body(h_ref, nw_ref, wg_hbm, wu_hbm, wd_hbm, ho_ref, n_ref, g_ref, u_ref, wg_v, wu_v, wd_v, sem):
        @pl.when(pl.program_id(0) == 0)
        def _():
            _load_weights(((wg_hbm, wg_v), (wu_hbm, wu_v), (wd_hbm, wd_v)), sem)

        x = h_ref[...]
        r = lax.rsqrt(jnp.mean(x * x, axis=-1, keepdims=True) + EPS)
        nb = _b((x * r) * nw_ref[...])
        n_ref[...] = nb
        acc = jnp.zeros((tm, D), F32)
        for s in range(N_SHARD):
            g = _dot(nb, wg_v[s])
            u = _dot(nb, wu_v[s])
            g_ref[s] = _b(g)
            u_ref[s] = _b(u)
            acc += _dot(_b(g * _sigmoid(g) * u), wd_v[s])
        ho_ref[...] = x + FFN_RES * acc

    row = lambda w: pl.BlockSpec((tm, w), lambda i: (i, 0))
    cut = pl.BlockSpec((N_SHARD, tm, FF_CUT), lambda i: (0, i, 0))
    return pl.pallas_call(
        _carry(carry, body, 5, 4, (T // tm,)), name=name, grid=(T // tm,),
        in_specs=[row(D), pl.BlockSpec((1, D), lambda i: (0, 0)), ANY, ANY, ANY] + ex_in,
        out_specs=[row(D), row(D), cut, cut] + ex_out,
        out_shape=[jax.ShapeDtypeStruct((T, D), F32), jax.ShapeDtypeStruct((T, D), BF16),
                   jax.ShapeDtypeStruct((N_SHARD, T, FF_CUT), BF16),
                   jax.ShapeDtypeStruct((N_SHARD, T, FF_CUT), BF16)] + ex_shape,
        scratch_shapes=[pltpu.VMEM((N_SHARD, D, FF_CUT), BF16), pltpu.VMEM((N_SHARD, D, FF_CUT), BF16),
                        pltpu.VMEM((N_SHARD, FF_CUT, D), BF16), pltpu.SemaphoreType.DMA((3,))] + ex_sems,
        compiler_params=_cp("arbitrary"),
    )(h, nw, wg, wu, wd, *ex_args)


def _ffn_bwd_hidden(dh, g, u, wd, name, carry=None):
    T = dh.shape[0]
    tm = min(BIG_ROW_BLK, T)
    ex_in, ex_args, ex_out, ex_shape, ex_sems = _carry_specs(carry)

    def body(dh_ref, g_ref, u_ref, wd_hbm, dg_ref, du_ref, a_ref, df_ref, wd_v, sem):
        @pl.when(pl.program_id(0) == 0)
        def _():
            _load_weights(((wd_hbm, wd_v),), sem)

        dfb = _b(FFN_RES * dh_ref[...])
        df_ref[...] = dfb
        for s in range(N_SHARD):
            da = _dot_nt(dfb, wd_v[s])
            gg = g_ref[s].astype(F32)
            uu = u_ref[s].astype(F32)
            sg = _sigmoid(gg)
            silu = gg * sg
            a_ref[s] = _b(silu * uu)
            dg_ref[s] = _b(da * uu * (sg * (1.0 + gg * (1.0 - sg))))
            du_ref[s] = _b(da * silu)

    row = pl.BlockSpec((tm, D), lambda i: (i, 0))
    cut = pl.BlockSpec((N_SHARD, tm, FF_CUT), lambda i: (0, i, 0))
    cut_shape = jax.ShapeDtypeStruct((N_SHARD, T, FF_CUT), BF16)
    return pl.pallas_call(
        _carry(carry, body, 4, 4, (T // tm,)), name=name, grid=(T // tm,),
        in_specs=[row, cut, cut, ANY] + ex_in,
        out_specs=[cut, cut, cut, row] + ex_out,
        out_shape=[cut_shape, cut_shape, cut_shape, jax.ShapeDtypeStruct((T, D), BF16)] + ex_shape,
        scratch_shapes=[pltpu.VMEM((N_SHARD, FF_CUT, D), BF16), pltpu.SemaphoreType.DMA((1,))] + ex_sems,
        compiler_params=_cp("arbitrary"),
    )(dh, g, u, wd, *ex_args)


def _ffn_bwd_input(dh, h, nw, dg, du, wg, wu, name, carry=None):
    T = h.shape[0]
    tm = min(BIG_ROW_BLK, T)
    ex_in, ex_args, ex_out, ex_shape, ex_sems = _carry_specs(carry)

    def body(dh_ref, h_ref, nw_ref, dg_ref, du_ref, wg_hbm, wu_hbm, dx_ref, dnw_ref, wg_v, wu_v, sem):
        @pl.when(pl.program_id(0) == 0)
        def _():
            _load_weights(((wg_hbm, wg_v), (wu_hbm, wu_v)), sem)
            dnw_ref[...] = jnp.zeros_like(dnw_ref)

        dn = jnp.zeros((tm, D), F32)
        for s in range(N_SHARD):
            dn += _dot_nt(dg_ref[s], wg_v[s]) + _dot_nt(du_ref[s], wu_v[s])
        x = h_ref[...]
        r = lax.rsqrt(jnp.mean(x * x, axis=-1, keepdims=True) + EPS)
        xhat = x * r
        dnw_ref[...] += jnp.sum(dn * xhat, axis=0, keepdims=True)
        dxhat = dn * nw_ref[...]
        dx_ref[...] = dh_ref[...] + r * (dxhat - xhat * jnp.mean(dxhat * xhat, axis=-1, keepdims=True))

    row = pl.BlockSpec((tm, D), lambda i: (i, 0))
    one = pl.BlockSpec((1, D), lambda i: (0, 0))
    cut = pl.BlockSpec((N_SHARD, tm, FF_CUT), lambda i: (0, i, 0))
    return pl.pallas_call(
        _carry(carry, body, 7, 2, (T // tm,)), name=name, grid=(T // tm,),
        in_specs=[row, row, one, cut, cut, ANY, ANY] + ex_in,
        out_specs=[row, one] + ex_out,
        out_shape=[jax.ShapeDtypeStruct((T, D), F32), jax.ShapeDtypeStruct((1, D), F32)] + ex_shape,
        scratch_shapes=[pltpu.VMEM((N_SHARD, D, FF_CUT), BF16), pltpu.VMEM((N_SHARD, D, FF_CUT), BF16),
                        pltpu.SemaphoreType.DMA((2,))] + ex_sems,
        compiler_params=_cp("arbitrary"),
    )(dh, h, nw, dg, du, wg, wu, *ex_args)


def _mm_tn(a, b, bm, bn, name, out_dtype=BF16, tk=2048):
    cuts = a.shape[0] if a.ndim == 3 else (b.shape[0] if b.ndim == 3 else None)
    T, M = a.shape[-2:]
    N = b.shape[-1]
    tk = min(tk, T)
    bm, bn = min(bm, M), min(bn, N)
    nk = T // tk

    def body(a_ref, b_ref, o_ref, acc_ref):
        k = pl.program_id(3)

        @pl.when(k == 0)
        def _():
            acc_ref[...] = jnp.zeros_like(acc_ref)

        av = a_ref[0] if a.ndim == 3 else a_ref[...]
        bv = b_ref[0] if b.ndim == 3 else b_ref[...]
        acc_ref[...] += _dot_tn(_b(av), _b(bv))

        @pl.when(k == nk - 1)
        def _():
            res = acc_ref[...].astype(out_dtype)
            if cuts is None:
                o_ref[...] = res
            else:
                o_ref[0] = res

    a_spec = (pl.BlockSpec((1, tk, bm), lambda s, i, j, k: (s, k, i)) if a.ndim == 3
              else pl.BlockSpec((tk, bm), lambda s, i, j, k: (k, i)))
    b_spec = (pl.BlockSpec((1, tk, bn), lambda s, i, j, k: (s, k, j)) if b.ndim == 3
              else pl.BlockSpec((tk, bn), lambda s, i, j, k: (k, j)))
    if cuts is None:
        o_spec, o_shape = pl.BlockSpec((bm, bn), lambda s, i, j, k: (i, j)), (M, N)
    else:
        o_spec, o_shape = pl.BlockSpec((1, bm, bn), lambda s, i, j, k: (s, i, j)), (cuts, M, N)
    return pl.pallas_call(
        body, name=name, grid=(cuts or 1, M // bm, N // bn, nk),
        in_specs=[a_spec, b_spec], out_specs=o_spec,
        out_shape=jax.ShapeDtypeStruct(o_shape, out_dtype),
        scratch_shapes=[pltpu.VMEM((bm, bn), F32)],
        compiler_params=_cp("parallel", "parallel", "parallel", "arbitrary"),
    )(a, b)


def _norm_proj(h, nw, wbig, wsmall, name, carry=None):
    T = h.shape[0]
    tm = min(512, T)
    tn = 1536
    ex_in, ex_args, ex_out, ex_shape, ex_sems = _carry_specs(carry)

    def body(h_ref, nw_ref, wb_hbm, ws_ref, pb_ref, ps_ref, n_ref, wb_v, sem):
        @pl.when(pl.program_id(0) == 0)
        def _():
            _load_weights(((wb_hbm, wb_v),), sem)

        x = h_ref[...]
        r = lax.rsqrt(jnp.mean(x * x, axis=-1, keepdims=True) + EPS)
        nb = _b((x * r) * nw_ref[...])
        n_ref[...] = nb
        ps_ref[...] = _dot(nb, ws_ref[...])
        for j in range(BIG_COLS // tn):
            pb_ref[:, j * tn:(j + 1) * tn] = _b(_dot(nb, wb_v[:, j * tn:(j + 1) * tn]))

    row = lambda w: pl.BlockSpec((tm, w), lambda i: (i, 0))
    return pl.pallas_call(
        _carry(carry, body, 4, 3, (T // tm,)), name=name, grid=(T // tm,),
        in_specs=[row(D), pl.BlockSpec((1, D), lambda i: (0, 0)), ANY, pl.BlockSpec((D, SMALL), lambda i: (0, 0))]
        + ex_in,
        out_specs=[row(BIG_COLS), row(SMALL), row(D)] + ex_out,
        out_shape=[jax.ShapeDtypeStruct((T, BIG_COLS), BF16), jax.ShapeDtypeStruct((T, SMALL), F32),
                   jax.ShapeDtypeStruct((T, D), BF16)] + ex_shape,
        scratch_shapes=[pltpu.VMEM((D, BIG_COLS), BF16), pltpu.SemaphoreType.DMA((1,))] + ex_sems,
        compiler_params=_cp("arbitrary"),
    )(h, nw, wbig, wsmall, *ex_args)


def _proj_bwd(dh, h, nw, pieces, dsmall, wbig, wsmall, name):
    T = h.shape[0]
    tm = min(BIG_ROW_BLK, T)
    offs = tuple(int(v) for v in np.cumsum((0,) + PIECES))

    def body(dh_ref, h_ref, nw_ref, *rest):
        p_refs = rest[:len(PIECES)]
        ds_ref, wb_hbm, ws_ref, dx_ref, dnw_ref, wb_v, sem = rest[len(PIECES):]

        @pl.when(pl.program_id(0) == 0)
        def _():
            _load_weights(((wb_hbm, wb_v),), sem)
            dnw_ref[...] = jnp.zeros_like(dnw_ref)

        dn = _dot_nt(_b(ds_ref[...]), ws_ref[...])
        for p_ref, lo, wdt in zip(p_refs, offs, PIECES):
            dn += _dot_nt(p_ref[...], wb_v[:, lo:lo + wdt])
        x = h_ref[...]
        r = lax.rsqrt(jnp.mean(x * x, axis=-1, keepdims=True) + EPS)
        xhat = x * r
        dnw_ref[...] += jnp.sum(dn * xhat, axis=0, keepdims=True)
        dxhat = dn * nw_ref[...]
        dx_ref[...] = dh_ref[...] + r * (dxhat - xhat * jnp.mean(dxhat * xhat, axis=-1, keepdims=True))

    row = lambda w: pl.BlockSpec((tm, w), lambda i: (i, 0))
    one = pl.BlockSpec((1, D), lambda i: (0, 0))
    return pl.pallas_call(
        body, name=name, grid=(T // tm,),
        in_specs=[row(D), row(D), one] + [row(w) for w in PIECES] + [row(SMALL), ANY, pl.BlockSpec((D, SMALL), lambda i: (0, 0))],
        out_specs=[row(D), one],
        out_shape=[jax.ShapeDtypeStruct((T, D), F32), jax.ShapeDtypeStruct((1, D), F32)],
        scratch_shapes=[pltpu.VMEM((D, BIG_COLS), BF16), pltpu.SemaphoreType.DMA((1,))],
        compiler_params=_cp("arbitrary"),
    )(dh, h, nw, *pieces, dsmall, wbig, wsmall)


def _gla_block(q_ref, k_ref, sm_ref, wg_ref, bg_ref, nc, tril):
    nbat = GLA_H * nc
    q = _heads_first(q_ref[...].astype(F32), nc, GLA_DK)
    k = _heads_first(k_ref[...].astype(F32), nc, GLA_DK)
    pre = _heads_first(_dot(sm_ref[...], wg_ref[...], HI) + bg_ref[...], nc, GLA_DK)
    la = _log_sigmoid(pre) * (1.0 / GLA_TAU)
    bc = _bmm(jnp.broadcast_to(tril, (nbat, CHUNK, CHUNK)), la, HI)
    bl = bc[:, CHUNK - 1:CHUNK, :]
    eb = jnp.exp(bc)
    enb = jnp.exp(-bc)
    ebl = jnp.exp(bl - bc)
    q_in = q * (GLA_DK ** -0.5) * eb
    k_out = k * enb
    k_st = k * ebl
    a_ch = jnp.exp(bl)
    return pre, eb, enb, ebl, q_in, k_out, k_st, a_ch


def _gla_specs(blk, idx):
    hk, hv = GLA_H * GLA_DK, GLA_H * GLA_DV
    return [pl.BlockSpec((blk, hk), lambda j: (idx(j), 0)),
            pl.BlockSpec((blk, hk), lambda j: (idx(j), 1)),
            pl.BlockSpec((blk, hv), lambda j: (idx(j), 1)),
            pl.BlockSpec((blk, SMALL), lambda j: (idx(j), 0)),
            pl.BlockSpec((SMALL, hk), lambda j: (0, 0)),
            pl.BlockSpec((1, hk), lambda j: (0, 0))]


def _gla_fwd(pbig, psmall, wgate, bgate, name):
    T = pbig.shape[0]
    blk = min(ATT_BLK, T)
    nc = blk // CHUNK

    def body(q_ref, k_ref, v_ref, sm_ref, wg_ref, bg_ref, o_ref, ss_ref, st_ref):
        @pl.when(pl.program_id(0) == 0)
        def _():
            st_ref[...] = jnp.zeros_like(st_ref)

        causal = _iota2(CHUNK, CHUNK, 0) >= _iota2(CHUNK, CHUNK, 1)
        _, _, _, _, q_in, k_out, k_st, a_ch = _gla_block(q_ref, k_ref, sm_ref, wg_ref, bg_ref, nc, causal.astype(F32))
        v = _heads_first(v_ref[...], nc, GLA_DV)
        qb = _b(q_in)
        sc = jnp.where(causal, _bmm_nt(qb, _b(k_out)), 0.0)
        kv = _bmm_tn(v, _b(k_st))
        before = [None] * (GLA_H * nc)
        for i in range(GLA_H):
            st = st_ref[i]
            for c in range(nc):
                n = i * nc + c
                before[n] = st
                st = st * a_ch[n] + kv[n]
            st_ref[i] = st
        states = jnp.stack(before)
        ss_ref[...] = states.reshape(GLA_H, nc, GLA_DV, GLA_DK)
        o_ref[...] = _b(_heads_last(_bmm(_b(sc), v) + _bmm_nt(qb, _b(states)), nc))

    return pl.pallas_call(
        body, name=name, grid=(T // blk,),
        in_specs=_gla_specs(blk, lambda j: j),
        out_specs=[pl.BlockSpec((blk, GLA_H * GLA_DV), lambda j: (j, 0)),
                   pl.BlockSpec((GLA_H, nc, GLA_DV, GLA_DK), lambda j: (0, j, 0, 0))],
        out_shape=[jax.ShapeDtypeStruct((T, GLA_H * GLA_DV), BF16),
                   jax.ShapeDtypeStruct((GLA_H, T // CHUNK, GLA_DV, GLA_DK), F32)],
        scratch_shapes=[pltpu.VMEM((GLA_H, GLA_DV, GLA_DK), F32)],
        compiler_params=_cp("arbitrary"),
    )(pbig, pbig, pbig, psmall, wgate, bgate)


def _gla_bwd(pbig, psmall, wgate, bgate, states, do, name, carry=None):
    ex_in, ex_args, ex_out, ex_shape, ex_sems = _carry_specs(carry)
    T = pbig.shape[0]
    blk = min(ATT_BLK, T)
    nc = blk // CHUNK
    nb = T // blk
    nbat = GLA_H * nc

    def body(q_ref, k_ref, v_ref, sm_ref, wg_ref, bg_ref, ss_ref, do_ref, dq_ref, dk_ref, dv_ref, dpre_ref, dst_ref):
        @pl.when(pl.program_id(0) == 0)
        def _():
            dst_ref[...] = jnp.zeros_like(dst_ref)

        causal = _iota2(CHUNK, CHUNK, 0) >= _iota2(CHUNK, CHUNK, 1)
        triu = (_iota2(CHUNK, CHUNK, 0) <= _iota2(CHUNK, CHUNK, 1)).astype(F32)
        pre, eb, enb, ebl, q_in, k_out, k_st, a_ch = _gla_block(q_ref, k_ref, sm_ref, wg_ref, bg_ref, nc,
                                                                causal.astype(F32))
        v = _heads_first(v_ref[...], nc, GLA_DV)
        dob = _b(_heads_first(do_ref[...], nc, GLA_DV))
        st = ss_ref[...].reshape(nbat, GLA_DV, GLA_DK)
        qb, kob, kstb = _b(q_in), _b(k_out), _b(k_st)
        qdo = _bmm_tn(dob, qb)
        after = [None] * nbat
        for i in range(GLA_H):
            dst = dst_ref[i]
            for c in range(nc - 1, -1, -1):
                n = i * nc + c
                after[n] = dst
                dst = dst * a_ch[n] + qdo[n]
            dst_ref[i] = dst
        dsa = jnp.stack(after)
        dsb = _b(dsa)
        sc = jnp.where(causal, _bmm_nt(qb, kob), 0.0)
        dsc = _b(jnp.where(causal, _bmm_nt(dob, v), 0.0))
        dq_in = _bmm(dob, _b(st)) + _bmm(dsc, kob)
        dk_out = _bmm_tn(dsc, qb)
        dk_st = _bmm(v, dsb)
        dv_ref[...] = _b(_heads_last(_bmm_tn(_b(sc), dob) + _bmm_nt(kstb, dsb), nc))
        da_ch = jnp.sum(st * dsa, axis=1, keepdims=True)
        tk = dk_st * k_st
        db = dq_in * q_in - dk_out * k_out - tk
        db_last = jnp.sum(tk, axis=1, keepdims=True) + da_ch * a_ch
        dq_ref[...] = _b(_heads_last(dq_in * (GLA_DK ** -0.5) * eb, nc))
        dk_ref[...] = _b(_heads_last(dk_out * enb + dk_st * ebl, nc))
        dla = _bmm(jnp.broadcast_to(triu, (nbat, CHUNK, CHUNK)), db, HI) + db_last
        dpre_ref[...] = _heads_last(dla * (1.0 / GLA_TAU) * _sigmoid(-pre), nc)

    r = lambda j: nb - 1 - j
    hk, hv = GLA_H * GLA_DK, GLA_H * GLA_DV
    return pl.pallas_call(
        _carry(carry, body, 8, 4, (nb,)), name=name, grid=(nb,),
        in_specs=_gla_specs(blk, r) + [pl.BlockSpec((GLA_H, nc, GLA_DV, GLA_DK), lambda j: (0, r(j), 0, 0)),
                                      pl.BlockSpec((blk, hv), lambda j: (r(j), 0))] + ex_in,
        out_specs=[pl.BlockSpec((blk, hk), lambda j: (r(j), 0)), pl.BlockSpec((blk, hk), lambda j: (r(j), 0)),
                   pl.BlockSpec((blk, hv), lambda j: (r(j), 0)), pl.BlockSpec((blk, hk), lambda j: (r(j), 0))] + ex_out,
        out_shape=[jax.ShapeDtypeStruct((T, hk), BF16), jax.ShapeDtypeStruct((T, hk), BF16),
                   jax.ShapeDtypeStruct((T, hv), BF16), jax.ShapeDtypeStruct((T, hk), F32)] + ex_shape,
        scratch_shapes=[pltpu.VMEM((GLA_H, GLA_DV, GLA_DK), F32)] + ex_sems,
        compiler_params=_cp("arbitrary"),
    )(pbig, pbig, pbig, psmall, wgate, bgate, states, do, *ex_args)


def _gla_gate_bwd(dpre, psmall, wgate, dsm, name):
    T = dpre.shape[0]
    tm = min(512, T)
    W = GLA_H * GLA_DK
    ngrp = dsm.shape[0]

    def body(dp_ref, sm_ref, wg_ref, dsm_ref, ds_ref, dw_ref, db_ref):
        @pl.when(pl.program_id(0) == 0)
        def _():
            dw_ref[...] = jnp.zeros_like(dw_ref)
            db_ref[...] = jnp.zeros_like(db_ref)

        dp = dp_ref[...]
        ds = _dot_nt(dp, wg_ref[...], HI)
        for i in range(ngrp):
            ds += dsm_ref[i]
        ds_ref[...] = ds
        dw_ref[...] += _dot_tn(sm_ref[...], dp, HI)
        db_ref[...] += jnp.sum(dp, axis=0, keepdims=True)

    return pl.pallas_call(
        body, name=name, grid=(T // tm,),
        in_specs=[pl.BlockSpec((tm, W), lambda i: (i, 0)), pl.BlockSpec((tm, SMALL), lambda i: (i, 0)),
                  pl.BlockSpec((SMALL, W), lambda i: (0, 0)), pl.BlockSpec((ngrp, tm, SMALL), lambda i: (0, i, 0))],
        out_specs=[pl.BlockSpec((tm, SMALL), lambda i: (i, 0)), pl.BlockSpec((SMALL, W), lambda i: (0, 0)),
                   pl.BlockSpec((1, W), lambda i: (0, 0))],
        out_shape=[jax.ShapeDtypeStruct((T, SMALL), F32), jax.ShapeDtypeStruct((SMALL, W), F32),
                   jax.ShapeDtypeStruct((1, W), F32)],
        compiler_params=_cp("arbitrary"),
    )(dpre, psmall, wgate, dsm)


CONV_C = 3 * 1024
CONV_BLK = 256


def _conv_fwd(pbig, cw8, name):
    T = pbig.shape[0]
    blk = min(CONV_BLK, T)

    def body(x_ref, w_ref, c_ref, prev_ref):
        @pl.when(pl.program_id(0) == 0)
        def _():
            prev_ref[...] = jnp.zeros_like(prev_ref)

        x = x_ref[...].astype(F32)
        prev = prev_ref[...]
        row8 = _iota2(8, CONV_C, 0)
        acc = x * w_ref[CONV_K - 1:CONV_K, :]
        for s in range(1, CONV_K):
            xs = pltpu.roll(x, s, 0)
            top = jnp.where(row8 < s, pltpu.roll(prev, s, 0), xs[:8])
            xs = jnp.concatenate([top, xs[8:]], axis=0)
            acc += xs * w_ref[CONV_K - 1 - s:CONV_K - s, :]
        c_ref[...] = _b(acc)
        prev_ref[...] = x[blk - 8:]

    return pl.pallas_call(
        body, name=name, grid=(T // blk,),
        in_specs=[pl.BlockSpec((blk, CONV_C), lambda i: (i, 1)), pl.BlockSpec((8, CONV_C), lambda i: (0, 0))],
        out_specs=pl.BlockSpec((blk, CONV_C), lambda i: (i, 0)),
        out_shape=jax.ShapeDtypeStruct((T, CONV_C), BF16),
        scratch_shapes=[pltpu.VMEM((8, CONV_C), F32)],
        compiler_params=_cp("arbitrary"),
    )(pbig, cw8)


def _conv_bwd(dcq, dck, dcv, pbig, cw8, name):
    T = pbig.shape[0]
    blk = min(CONV_BLK, T)
    nb = T // blk

    def body(dq_ref, dk_ref, dv_ref, x_ref, w_ref, dx_ref, dw_ref, nxt_ref):
        @pl.when(pl.program_id(0) == 0)
        def _():
            nxt_ref[...] = jnp.zeros_like(nxt_ref)
            dw_ref[...] = jnp.zeros_like(dw_ref)

        dc = jnp.concatenate([dq_ref[...], dk_ref[...], dv_ref[...]], axis=1).astype(F32)
        x = x_ref[...].astype(F32)
        nxt = nxt_ref[...]
        row8 = _iota2(8, CONV_C, 0)
        acc = dc * w_ref[CONV_K - 1:CONV_K, :]
        dws = [jnp.sum(dc * x, axis=0, keepdims=True)]
        for s in range(1, CONV_K):
            ds = pltpu.roll(dc, blk - s, 0)
            bot = jnp.where(row8 >= 8 - s, pltpu.roll(nxt, 8 - s, 0), ds[blk - 8:])
            ds = jnp.concatenate([ds[:blk - 8], bot], axis=0)
            acc += ds * w_ref[CONV_K - 1 - s:CONV_K - s, :]
            dws.append(jnp.sum(ds * x, axis=0, keepdims=True))
        dx_ref[...] = _b(acc)
        dw_ref[...] += jnp.concatenate(dws[::-1] + [jnp.zeros((8 - CONV_K, CONV_C), F32)], axis=0)
        nxt_ref[...] = dc[:8]

    part = pl.BlockSpec((blk, 1024), lambda i: (nb - 1 - i, 0))
    return pl.pallas_call(
        body, name=name, grid=(nb,),
        in_specs=[part, part, part, pl.BlockSpec((blk, CONV_C), lambda i: (nb - 1 - i, 1)),
                  pl.BlockSpec((8, CONV_C), lambda i: (0, 0))],
        out_specs=[pl.BlockSpec((blk, CONV_C), lambda i: (nb - 1 - i, 0)), pl.BlockSpec((8, CONV_C), lambda i: (0, 0))],
        out_shape=[jax.ShapeDtypeStruct((T, CONV_C), BF16), jax.ShapeDtypeStruct((8, CONV_C), F32)],
        scratch_shapes=[pltpu.VMEM((8, CONV_C), F32)],
        compiler_params=_cp("arbitrary"),
    )(dcq, dck, dcv, pbig, cw8)


def _col(x, lane):
    sel = _iota2(x.shape[0], x.shape[1], 1) == lane
    return jnp.broadcast_to(jnp.sum(jnp.where(sel, x, 0.0), axis=1, keepdims=True), x.shape)


def _bmm(a, b, prec=None):
    return jnp.einsum("bij,bjk->bik", a, b, preferred_element_type=F32, precision=prec)


def _bmm_nt(a, b, prec=None):
    return jnp.einsum("bij,bkj->bik", a, b, preferred_element_type=F32, precision=prec)


def _bmm_tn(a, b, prec=None):
    return jnp.einsum("bji,bjk->bik", a, b, preferred_element_type=F32, precision=prec)


def _unit_lower_inverse(low):
    eye = (_iota2(CHUNK, CHUNK, 0) == _iota2(CHUNK, CHUNK, 1)).astype(F32)
    xk = -low
    inv = eye + xk
    for _ in range(5):
        xb = _b(xk)
        xk = _bmm(xb, xb)
        inv = inv + _bmm(_b(inv), _b(xk))
    resid = eye - _bmm(eye + low, inv, HI)
    return inv + _bmm(inv, resid, HI)


def _heads_first(x, nc, w=128):
    hb = x.shape[1] // w
    return jnp.concatenate([x[:, i * w:(i + 1) * w].reshape(nc, CHUNK, w) for i in range(hb)], axis=0)


def _heads_last(x, nc):
    hb = x.shape[0] // nc
    return jnp.concatenate([x[i * nc:(i + 1) * nc].reshape(nc * CHUNK, x.shape[2]) for i in range(hb)], axis=1)


def _gdn_block(cq_ref, ck_ref, cv_ref, sm_ref, par_ref, h0, hb, nc, masks, solved=None):
    causal, strict, tril, eye = masks
    nbat = hb * nc
    cq = _heads_first(cq_ref[...].astype(F32), nc)
    ck = _heads_first(ck_ref[...].astype(F32), nc)
    cv = _heads_first(cv_ref[...].astype(F32), nc)
    sq, sk, sv = _sigmoid(cq), _sigmoid(ck), _sigmoid(cv)
    q, k, v = cq * sq, ck * sk, cv * sv
    rq = lax.rsqrt(jnp.sum(q * q, axis=-1, keepdims=True) + EPS)
    rk = lax.rsqrt(jnp.sum(k * k, axis=-1, keepdims=True) + EPS)
    qh, kn = q * rq, k * rk
    qn = qh * (DN_DK ** -0.5)
    sm = sm_ref[...]
    per_head = lambda fn: jnp.concatenate([fn(i) for i in range(hb)], axis=0)
    braw = per_head(lambda i: _col(sm, GLA_RANK + h0 + i).reshape(nc, CHUNK, 128))
    araw = per_head(lambda i: _col(sm, GLA_RANK + DN_H + h0 + i).reshape(nc, CHUNK, 128))
    ea = per_head(lambda i: jnp.broadcast_to(jnp.exp(par_ref[i, 0:1, :])[None], (nc, 1, 128)))
    bias = per_head(lambda i: jnp.broadcast_to(par_ref[i, 1:2, :][None], (nc, 1, 128)))
    beta = _sigmoid(braw)
    sp_arg = araw + bias
    g = -ea * _softplus(sp_arg)
    G = _bmm(jnp.broadcast_to(tril, (nbat, CHUNK, CHUNK)), g, HI)
    gc = G[:, :, :CHUNK]
    grow = jnp.sum(eye * gc, axis=1, keepdims=True)
    decay = jnp.exp(jnp.where(causal, gc - grow, -1e30))
    kb = kn * beta
    A = _bmm_nt(_b(kb), _b(kn))
    eG = jnp.exp(G)
    gl = G[:, CHUNK - 1:CHUNK, :]
    eGl = jnp.exp(gl - G)
    g_ch = jnp.exp(gl)
    rv = v * beta
    rkk = kb * eG
    if solved is None:
        tinv_b = _b(_unit_lower_inverse(jnp.where(strict, A * decay, 0.0)))
        u = _bmm(tinv_b, _b(rv))
        w = _b(_bmm(tinv_b, _b(rkk)))
    else:
        tinv_b, u, w = solved
    B = _bmm_nt(_b(qn), _b(kn))
    qk = jnp.where(causal, B * decay, 0.0)
    q_dec = qn * eG
    k_st = kn * eGl
    return dict(cq=cq, ck=ck, cv=cv, sq=sq, sk=sk, sv=sv, q=q, k=k, v=v, rq=rq, rk=rk, qh=qh, kn=kn, qn=qn,
                beta=beta, ea=ea, sp_arg=sp_arg, g=g, G=G, decay=decay, kb=kb, A=A, tinv_b=tinv_b, eG=eG, eGl=eGl,
                g_ch=g_ch, rv=rv, rkk=rkk, u=u, w=w, B=B, qk=qk, q_dec=q_dec, k_st=k_st)


def _gdn_masks():
    r, c = _iota2(CHUNK, CHUNK, 0), _iota2(CHUNK, CHUNK, 1)
    return r >= c, r > c, (r >= c).astype(F32), (r == c).astype(F32)


def _gdn_specs(blk, hb, idx):
    ng = DN_H // hb
    return [pl.BlockSpec((blk, hb * DN_DK), lambda h, j: (idx(j), h)),
            pl.BlockSpec((blk, hb * DN_DK), lambda h, j: (idx(j), ng + h)),
            pl.BlockSpec((blk, hb * DN_DV), lambda h, j: (idx(j), 2 * ng + h)),
            pl.BlockSpec((blk, SMALL), lambda h, j: (idx(j), 0)),
            pl.BlockSpec((hb, 8, 128), lambda h, j: (h, 0, 0))]


def _gdn_solved_specs(blk, hb, idx):
    nc = blk // CHUNK
    spec = lambda w: pl.BlockSpec((hb, nc, CHUNK, w), lambda h, j: (h, idx(j), 0, 0))
    return [spec(CHUNK), spec(DN_DV), spec(DN_DK)]


def _gdn_fwd(conv, psmall, par, name):
    T = conv.shape[0]
    blk = min(GDN_BLK, T)
    nc = blk // CHUNK
    hb = GDN_HEADS
    N = T // CHUNK

    def body(cq_ref, ck_ref, cv_ref, sm_ref, par_ref, o_ref, ss_ref, ti_ref, u_ref, w_ref, s_ref):
        @pl.when(pl.program_id(1) == 0)
        def _():
            s_ref[...] = jnp.zeros_like(s_ref)

        f = _gdn_block(cq_ref, ck_ref, cv_ref, sm_ref, par_ref, pl.program_id(0) * hb, hb, nc, _gdn_masks())
        ti_ref[...] = f["tinv_b"].reshape(hb, nc, CHUNK, CHUNK)
        u_ref[...] = f["u"].reshape(hb, nc, CHUNK, DN_DV)
        w_ref[...] = f["w"].reshape(hb, nc, CHUNK, DN_DK)
        wb, ub, kstb, qkb = f["w"], _b(f["u"]), _b(f["k_st"]), _b(f["qk"])
        mix = _b(_bmm_tn(kstb, wb))
        add = _bmm_tn(kstb, ub)
        q_eff = _b(f["q_dec"] - _bmm(qkb, wb))
        before = [None] * (hb * nc)
        S = [s_ref[i] for i in range(hb)]
        for c in range(nc):
            for i in range(hb):
                n = i * nc + c
                before[n] = S[i]
                S[i] = S[i] * f["g_ch"][n] - _dot(mix[n], _b(S[i])) + add[n]
        for i in range(hb):
            s_ref[i] = S[i]
        states = jnp.stack(before)
        ss_ref[...] = states.reshape(hb, nc, DN_DK, DN_DV)
        o_ref[...] = _b(_heads_last(_bmm(qkb, ub) + _bmm(q_eff, _b(states)), nc))

    return pl.pallas_call(
        body, name=name, grid=(DN_H // hb, T // blk),
        in_specs=_gdn_specs(blk, hb, lambda j: j),
        out_specs=[pl.BlockSpec((blk, hb * DN_DV), lambda h, j: (j, h)),
                   pl.BlockSpec((hb, nc, DN_DK, DN_DV), lambda h, j: (h, j, 0, 0))]
        + _gdn_solved_specs(blk, hb, lambda j: j),
        out_shape=[jax.ShapeDtypeStruct((T, DN_H * DN_DV), BF16), jax.ShapeDtypeStruct((DN_H, N, DN_DK, DN_DV), F32),
                   jax.ShapeDtypeStruct((DN_H, N, CHUNK, CHUNK), BF16), jax.ShapeDtypeStruct((DN_H, N, CHUNK, DN_DV), F32),
                   jax.ShapeDtypeStruct((DN_H, N, CHUNK, DN_DK), BF16)],
        scratch_shapes=[pltpu.VMEM((hb, DN_DK, DN_DV), F32)],
        compiler_params=_cp("parallel", "arbitrary"),
    )(conv, conv, conv, psmall, par)


def _gdn_bwd(conv, psmall, par, states, do, solved, name, carry=None):
    ex_in, ex_args, ex_out, ex_shape, ex_sems = _carry_specs(carry)
    T = conv.shape[0]
    blk = min(GDN_BLK, T)
    nc = blk // CHUNK
    nb = T // blk
    hb = GDN_HEADS
    nbat = hb * nc
    rsum = lambda x: jnp.sum(x, axis=-1, keepdims=True)

    def body(cq_ref, ck_ref, cv_ref, sm_ref, par_ref, ss_ref, do_ref, ti_ref, u_ref, w_ref,
             dcq_ref, dck_ref, dcv_ref, dsm_ref, dpar_ref, ds_ref):
        @pl.when(pl.program_id(1) == 0)
        def _():
            ds_ref[...] = jnp.zeros_like(ds_ref)
            dpar_ref[...] = jnp.zeros_like(dpar_ref)

        masks = _gdn_masks()
        causal, strict, tril, eye = masks
        triu = (_iota2(CHUNK, CHUNK, 0) <= _iota2(CHUNK, CHUNK, 1)).astype(F32)
        lane = _iota2(CHUNK, 128, 1)
        last_row = _iota2(CHUNK, 128, 0) == CHUNK - 1
        h0 = pl.program_id(0) * hb
        solved = (ti_ref[...].reshape(nbat, CHUNK, CHUNK), u_ref[...].reshape(nbat, CHUNK, DN_DV),
                  w_ref[...].reshape(nbat, CHUNK, DN_DK))
        f = _gdn_block(cq_ref, ck_ref, cv_ref, sm_ref, par_ref, h0, hb, nc, masks, solved)
        S = ss_ref[...].reshape(nbat, DN_DK, DN_DV)
        Sb = _b(S)
        do_ = _b(_heads_first(do_ref[...], nc))
        wb, qdb, kstb, qkb = _b(f["w"]), _b(f["q_dec"]), _b(f["k_st"]), _b(f["qk"])
        vnb = _b(f["u"] - _bmm(wb, Sb))
        dvn0 = _bmm_tn(qkb, do_)
        qdo = _bmm_tn(qdb, do_)
        dS = [ds_ref[i] for i in range(hb)]
        after = [None] * nbat
        for c in range(nc - 1, -1, -1):
            for i in range(hb):
                n = i * nc + c
                after[n] = dS[i]
                dvn_c = _b(dvn0[n] + _dot(kstb[n], _b(dS[i])))
                dS[i] = dS[i] * f["g_ch"][n] + qdo[n] - _dot_tn(wb[n], dvn_c)
        for i in range(hb):
            ds_ref[i] = dS[i]
        dSa = jnp.stack(after)
        dSb = _b(dSa)
        dvn = dvn0 + _bmm(kstb, dSb)
        dvnb = _b(dvn)
        dq_dec = _bmm_nt(do_, Sb)
        dqk = jnp.where(causal, _bmm_nt(do_, vnb), 0.0)
        dk_st = _bmm_nt(vnb, dSb)
        dg_ch = jnp.sum(rsum(S * dSa), axis=1, keepdims=True)
        dw = -_bmm_nt(dvnb, Sb)
        drv = _bmm_tn(f["tinv_b"], dvnb)
        drk = _bmm_tn(f["tinv_b"], _b(dw))
        dlow = jnp.where(strict, -(_bmm_nt(_b(drv), _b(f["u"])) + _bmm_nt(_b(drk), wb)), 0.0)
        dv = drv * f["beta"]
        dbeta = rsum(drv * f["v"])
        dkb = drk * f["eG"]
        dG = rsum(drk * f["rkk"])
        dA = dlow * f["decay"]
        ddec = dlow * f["A"]
        dkb += _bmm(_b(dA), _b(f["kn"]))
        dkn = _bmm_tn(_b(dA), _b(f["kb"]))
        dB = dqk * f["decay"]
        ddec += dqk * f["B"]
        dqn = _bmm(_b(dB), _b(f["kn"]))
        dkn += _bmm_tn(_b(dB), _b(f["qn"]))
        dD = ddec * f["decay"]
        dG += rsum(dD) - rsum(eye * jnp.sum(dD, axis=1, keepdims=True))
        dqn += dq_dec * f["eG"]
        dG += rsum(dq_dec * f["q_dec"])
        dkn += dk_st * f["eGl"]
        tks = rsum(dk_st * f["k_st"])
        dG -= tks
        dG_last = jnp.sum(tks, axis=1, keepdims=True) + dg_ch * f["g_ch"][:, :, :1]
        dkn += dkb * f["beta"]
        dbeta += rsum(dkb * f["kn"])
        dGf = jnp.broadcast_to(dG, (nbat, CHUNK, 128)) + jnp.where(last_row, dG_last, 0.0)
        dg = _bmm(jnp.broadcast_to(triu, (nbat, CHUNK, CHUNK)), dGf, HI)
        dbraw = dbeta * f["beta"][:, :, :1] * (1.0 - f["beta"][:, :, :1])
        daraw = dg * (-f["ea"]) * _sigmoid(f["sp_arg"])
        both = lambda t: jnp.sum(jnp.sum(t, axis=1, keepdims=True), axis=0)
        dgg = dg * f["g"]
        dsm = jnp.zeros((nc, CHUNK, SMALL), F32)
        for i in range(hb):
            mine = slice(i * nc, (i + 1) * nc)
            dsm += (jnp.where(lane == GLA_RANK + h0 + i, dbraw[mine], 0.0)
                    + jnp.where(lane == GLA_RANK + DN_H + h0 + i, daraw[mine], 0.0))
            dpar = jnp.where(lane[:1] == 0, both(dgg[mine]), jnp.where(lane[:1] == 1, both(daraw[mine]), 0.0))
            dpar_ref[i] += jnp.broadcast_to(dpar, (8, 128))
        dsm_ref[0] = dsm.reshape(blk, SMALL)
        dqh = dqn * (DN_DK ** -0.5)
        dq = f["rq"] * (dqh - f["qh"] * rsum(dqh * f["qh"]))
        dk = f["rk"] * (dkn - f["kn"] * rsum(dkn * f["kn"]))
        dsilu = lambda x, s: s * (1.0 + x * (1.0 - s))
        dcq_ref[...] = _b(_heads_last(dq * dsilu(f["cq"], f["sq"]), nc))
        dck_ref[...] = _b(_heads_last(dk * dsilu(f["ck"], f["sk"]), nc))
        dcv_ref[...] = _b(_heads_last(dv * dsilu(f["cv"], f["sv"]), nc))

    r = lambda j: nb - 1 - j
    out_blk = pl.BlockSpec((blk, hb * DN_DK), lambda h, j: (r(j), h))
    grid = (DN_H // hb, nb)
    return pl.pallas_call(
        _carry(carry, body, 10, 5, grid), name=name, grid=grid,
        in_specs=_gdn_specs(blk, hb, r) + [pl.BlockSpec((hb, nc, DN_DK, DN_DV), lambda h, j: (h, r(j), 0, 0)),
                                          pl.BlockSpec((blk, hb * DN_DV), lambda h, j: (r(j), h))]
        + _gdn_solved_specs(blk, hb, r) + ex_in,
        out_specs=[out_blk, out_blk, out_blk, pl.BlockSpec((1, blk, SMALL), lambda h, j: (h, r(j), 0)),
                   pl.BlockSpec((hb, 8, 128), lambda h, j: (h, 0, 0))] + ex_out,
        out_shape=[jax.ShapeDtypeStruct((T, DN_H * DN_DK), BF16)] * 3 + [
            jax.ShapeDtypeStruct((DN_H // hb, T, SMALL), F32), jax.ShapeDtypeStruct((DN_H, 8, 128), F32)] + ex_shape,
        scratch_shapes=[pltpu.VMEM((hb, DN_DK, DN_DV), F32)] + ex_sems,
        compiler_params=_cp("arbitrary", "arbitrary"),
    )(conv, conv, conv, psmall, par, states, do, *solved, *ex_args)


def _head_norm(o, w, dv):
    outs, rs = [], []
    for i in range(o.shape[1] // dv):
        oh = o[:, i * dv:(i + 1) * dv]
        r = lax.rsqrt(jnp.mean(oh * oh, axis=-1, keepdims=True) + EPS)
        outs.append(oh * r)
        rs.append(r)
    return outs, rs


def _merge_specs(tm):
    col = lambda c: pl.BlockSpec((tm, D), lambda i: (i, c))
    return [col(0), col(0), col(2), col(6), col(7), col(8),
            pl.BlockSpec((1, GLA_DV), lambda i: (0, 0)), pl.BlockSpec((1, DN_DV), lambda i: (0, 0)),
            pl.BlockSpec((D, D), lambda i: (0, 0))]


def _merge_fwd(h, oa, ob, pbig, gla_hn, dn_hn, wout, name):
    T = h.shape[0]
    tm = min(ROW_BLK, T)

    def body(h_ref, oa_ref, ob_ref, gr_ref, dg_ref, ma_ref, mb_ref, wa_ref, wb_ref, wo_ref, ho_ref, y_ref):
        na, _ = _head_norm(oa_ref[...].astype(F32), wa_ref[...], GLA_DV)
        nbs, _ = _head_norm(ob_ref[...].astype(F32), wb_ref[...], DN_DV)
        hna = jnp.concatenate([t * wa_ref[...] for t in na], axis=1)
        hnb = jnp.concatenate([t * wb_ref[...] for t in nbs], axis=1)
        gr = gr_ref[...].astype(F32)
        dg = dg_ref[...].astype(F32)
        y = (_sigmoid(ma_ref[...].astype(F32)) * hna * (gr * _sigmoid(gr))
             + _sigmoid(mb_ref[...].astype(F32)) * hnb * (dg * _sigmoid(dg)))
        yb = _b(y)
        y_ref[...] = yb
        ho_ref[...] = h_ref[...] + _dot(yb, wo_ref[...])

    row = pl.BlockSpec((tm, D), lambda i: (i, 0))
    return pl.pallas_call(
        body, name=name, grid=(T // tm,),
        in_specs=[row] + _merge_specs(tm),
        out_specs=[row, row],
        out_shape=[jax.ShapeDtypeStruct((T, D), F32), jax.ShapeDtypeStruct((T, D), BF16)],
        compiler_params=_cp("arbitrary"),
    )(h, oa, ob, pbig, pbig, pbig, pbig, gla_hn, dn_hn, wout)


def _merge_bwd(dh, oa, ob, pbig, gla_hn, dn_hn, wout, name, carry=None):
    T = dh.shape[0]
    tm = min(ROW_BLK, T)
    ex_in, ex_args, ex_out, ex_shape, ex_sems = _carry_specs(carry)

    def branch(dy, o_ref, w_ref, gate_ref, m_ref, dv):
        w = w_ref[...]
        ohat, rs = _head_norm(o_ref[...].astype(F32), w, dv)
        gate = gate_ref[...].astype(F32)
        m = m_ref[...].astype(F32)
        sgate, sm = _sigmoid(gate), _sigmoid(m)
        silu = gate * sgate
        ohat_all = jnp.concatenate(ohat, axis=1)
        hn = jnp.concatenate([t * w for t in ohat], axis=1)
        d_on = dy * sm
        d_m = dy * hn * silu * sm * (1.0 - sm)
        d_hn = d_on * silu
        d_gate = d_on * hn * (sgate * (1.0 + gate * (1.0 - sgate)))
        dw = jnp.zeros((1, dv), F32)
        d_o = []
        for i, (oh, r) in enumerate(zip(ohat, rs)):
            dhn = d_hn[:, i * dv:(i + 1) * dv]
            dw += jnp.sum(dhn * oh, axis=0, keepdims=True)
            dohat = dhn * w
            d_o.append(r * (dohat - oh * jnp.mean(dohat * oh, axis=-1, keepdims=True)))
        return jnp.concatenate(d_o, axis=1), d_gate, d_m, dw

    def body(dh_ref, oa_ref, ob_ref, gr_ref, dg_ref, ma_ref, mb_ref, wa_ref, wb_ref, wo_ref,
             doa_ref, dob_ref, dgr_ref, ddg_ref, dma_ref, dmb_ref, dwa_ref, dwb_ref, dhb_ref):
        @pl.when(pl.program_id(0) == 0)
        def _():
            dwa_ref[...] = jnp.zeros_like(dwa_ref)
            dwb_ref[...] = jnp.zeros_like(dwb_ref)

        dhb = _b(dh_ref[...])
        dhb_ref[...] = dhb
        dy = _dot_nt(dhb, wo_ref[...])
        d_oa, d_gr, d_ma, dwa = branch(dy, oa_ref, wa_ref, gr_ref, ma_ref, GLA_DV)
        d_ob, d_dg, d_mb, dwb = branch(dy, ob_ref, wb_ref, dg_ref, mb_ref, DN_DV)
        doa_ref[...] = _b(d_oa)
        dob_ref[...] = _b(d_ob)
        dgr_ref[...] = _b(d_gr)
        ddg_ref[...] = _b(d_dg)
        dma_ref[...] = _b(d_ma)
        dmb_ref[...] = _b(d_mb)
        dwa_ref[...] += dwa
        dwb_ref[...] += dwb

    row = pl.BlockSpec((tm, D), lambda i: (i, 0))
    b16 = jax.ShapeDtypeStruct((T, D), BF16)
    return pl.pallas_call(
        _carry(carry, body, 10, 9, (T // tm,)), name=name, grid=(T // tm,),
        in_specs=[row] + _merge_specs(tm) + ex_in,
        out_specs=[row] * 6 + [pl.BlockSpec((1, GLA_DV), lambda i: (0, 0)), pl.BlockSpec((1, DN_DV), lambda i: (0, 0)), row]
        + ex_out,
        out_shape=[b16, b16, b16, b16, b16, b16, jax.ShapeDtypeStruct((1, GLA_DV), F32),
                   jax.ShapeDtypeStruct((1, DN_DV), F32), b16] + ex_shape,
        scratch_shapes=ex_sems,
        compiler_params=_cp("arbitrary"),
    )(dh, oa, ob, pbig, pbig, pbig, pbig, gla_hn, dn_hn, wout, *ex_args)


def _loss_head(h, nw, target, name):
    T = h.shape[0]
    tm = min(512, T)

    def body(h_ref, nw_ref, t_ref, dx_ref, loss_ref, dnw_ref):
        @pl.when(pl.program_id(0) == 0)
        def _():
            loss_ref[...] = jnp.zeros_like(loss_ref)
            dnw_ref[...] = jnp.zeros_like(dnw_ref)

        x = h_ref[...]
        w = nw_ref[...]
        r = lax.rsqrt(jnp.mean(x * x, axis=-1, keepdims=True) + EPS)
        xhat = x * r
        err = xhat * w - t_ref[...]
        part = jnp.sum(jnp.sum(err * err, axis=-1, keepdims=True), axis=0, keepdims=True)
        loss_ref[...] += (0.5 / D) * part
        dout = err * (1.0 / D)
        dnw_ref[...] += jnp.sum(dout * xhat, axis=0, keepdims=True)
        dxhat = dout * w
        dx_ref[...] = r * (dxhat - xhat * jnp.mean(dxhat * xhat, axis=-1, keepdims=True))

    row = pl.BlockSpec((tm, D), lambda i: (i, 0))
    one = pl.BlockSpec((1, D), lambda i: (0, 0))
    return pl.pallas_call(
        body, name=name, grid=(T // tm,),
        in_specs=[row, one, row],
        out_specs=[row, pl.BlockSpec((8, 128), lambda i: (0, 0)), one],
        out_shape=[jax.ShapeDtypeStruct((T, D), F32), jax.ShapeDtypeStruct((8, 128), F32),
                   jax.ShapeDtypeStruct((1, D), F32)],
        compiler_params=_cp("arbitrary"),
    )(h, nw, target)


def _adamw(w, g, m, v, rows, name):
    R, C = w.shape
    rows = min(rows, R)
    c1 = 1.0 - ADAM_B1 ** ADAM_STEP
    c2 = 1.0 - ADAM_B2 ** ADAM_STEP

    def body(w_ref, g_ref, m_ref, v_ref, d_ref, mo_ref, vo_ref):
        g_ = g_ref[...]
        m_ = ADAM_B1 * m_ref[...] + (1.0 - ADAM_B1) * g_
        v_ = ADAM_B2 * v_ref[...] + (1.0 - ADAM_B2) * (g_ * g_)
        mo_ref[...] = m_
        vo_ref[...] = v_
        d_ref[...] = -ADAM_LR * ((m_ / c1) / (jnp.sqrt(v_ / c2) + ADAM_EPS) + ADAM_WD * w_ref[...])

    blk = pl.BlockSpec((rows, C), lambda i: (i, 0))
    shp = jax.ShapeDtypeStruct((R, C), F32)
    return pl.pallas_call(
        body, name=name, grid=(R // rows,),
        in_specs=[blk] * 4, out_specs=[blk] * 3, out_shape=[shp] * 3,
        compiler_params=_cp("parallel"),
    )(w, g, m, v)


def _me():
    return lax.axis_index("x"), lax.axis_index("y"), lax.axis_index("c")


def _other_chips(x, y):
    return [(1 - x, y), (x, 1 - y), (1 - x, 1 - y)]


def _half_rows(ref, hf):
    half = ref.shape[-2] // 2
    rows = pl.ds(pl.multiple_of(hf * half, 16), half)
    return ref.at[rows, :] if len(ref.shape) == 2 else ref.at[:, rows, :]


class _GatherBig:
    def __init__(self, big):
        self.arrays = list(big)
        self.out_shape = [jax.ShapeDtypeStruct((N_SHARD,) + w.shape, w.dtype) for w in big]
        self.n_sem = 7 * len(big)

    @staticmethod
    def _copy(sems, k, src, dst, to):
        return pltpu.make_async_remote_copy(src_ref=src, dst_ref=dst, send_sem=sems[0].at[k], recv_sem=sems[1].at[k],
                                            device_id=to, device_id_type=MESH)

    def start(self, ins, outs, *sems):
        x, y, c = _me()
        mine = 2 * x + y
        for i, (w_ref, o_ref) in enumerate(zip(ins, outs)):
            self._copy(sems, 7 * i + 6, w_ref, o_ref.at[mine], (x, y, 1 - c)).start()
            for j, chip in enumerate(_other_chips(x, y)):
                self._copy(sems, 7 * i + j, _half_rows(w_ref, c), _half_rows(o_ref.at[mine], c), (*chip, c)).start()

    def relay(self, ins, outs, *sems):
        x, y, c = _me()
        for i, o_ref in enumerate(outs):
            for j, chip in enumerate(_other_chips(x, y)):
                landed = _half_rows(o_ref.at[2 * chip[0] + chip[1]], c)
                self._copy(sems, 7 * i + j, landed, landed, (x, y, c)).wait_recv()
                self._copy(sems, 7 * i + 3 + j, landed, landed, (x, y, 1 - c)).start()

    def finish(self, ins, outs, *sems):
        x, y, c = _me()
        me, sibling = (x, y, c), (x, y, 1 - c)
        chips = _other_chips(x, y)
        slot = lambda chip: 2 * chip[0] + chip[1]
        for i, (w_ref, o_ref) in enumerate(zip(ins, outs)):
            for j, chip in enumerate(chips):
                passed = _half_rows(o_ref.at[slot(chip)], 1 - c)
                self._copy(sems, 7 * i + 3 + j, passed, passed, me).wait_recv()
            self._copy(sems, 7 * i + 6, o_ref.at[slot((x, y))], o_ref.at[slot((x, y))], me).wait_recv()
        for i, (w_ref, o_ref) in enumerate(zip(ins, outs)):
            self._copy(sems, 7 * i + 6, w_ref, o_ref.at[slot((x, y))], sibling).wait_send()
            for j, chip in enumerate(chips):
                self._copy(sems, 7 * i + j, _half_rows(w_ref, c), _half_rows(o_ref.at[slot((x, y))], c),
                           (*chip, c)).wait_send()
                landed = _half_rows(o_ref.at[slot(chip)], c)
                self._copy(sems, 7 * i + 3 + j, landed, landed, sibling).wait_send()


class _SiblingExchange:
    def __init__(self, gs):
        self.arrays = list(gs)
        self.out_shape = [jax.ShapeDtypeStruct((g.shape[0], g.shape[1] // 2, g.shape[2]), g.dtype) for g in gs]
        self.n_sem = len(gs)

    def _copies(self, ins, outs, send_sems, recv_sems):
        x, y, c = _me()
        return [pltpu.make_async_remote_copy(src_ref=_half_rows(ins[i], 1 - c), dst_ref=outs[i],
                                             send_sem=send_sems.at[i], recv_sem=recv_sems.at[i],
                                             device_id=(x, y, 1 - c), device_id_type=MESH) for i in range(len(ins))]

    def start(self, ins, outs, *sems):
        for cp in self._copies(ins, outs, *sems):
            cp.start()

    def finish(self, ins, outs, *sems):
        for cp in self._copies(ins, outs, *sems):
            cp.wait()


class _ChipsExchange:
    def __init__(self, pbs):
        self.arrays = list(pbs)
        self.out_shape = [jax.ShapeDtypeStruct((3,) + p.shape[1:], p.dtype) for p in pbs]
        self.n_sem = 3 * len(pbs)

    def _copies(self, ins, outs, send_sems, recv_sems):
        x, y, c = _me()
        return [pltpu.make_async_remote_copy(src_ref=ins[i].at[2 * chip[0] + chip[1]], dst_ref=outs[i].at[j],
                                             send_sem=send_sems.at[3 * i + j], recv_sem=recv_sems.at[3 * i + j],
                                             device_id=(*chip, c), device_id_type=MESH)
                for i in range(len(ins)) for j, chip in enumerate(_other_chips(x, y))]

    def start(self, ins, outs, *sems):
        for cp in self._copies(ins, outs, *sems):
            cp.start()

    def finish(self, ins, outs, *sems):
        for cp in self._copies(ins, outs, *sems):
            cp.wait()


def _carry(ex, body, n_in, n_out, grid):
    if ex is None:
        return body
    ni, no = len(ex.arrays), len(ex.out_shape)

    def carried(*refs):
        ins, ex_in = refs[:n_in], refs[n_in:n_in + ni]
        outs, ex_out = refs[n_in + ni:n_in + ni + n_out], refs[n_in + ni + n_out:n_in + ni + n_out + no]
        scratch, sems = refs[n_in + ni + n_out + no:-2], refs[-2:]
        step = functools.reduce(lambda acc, a: acc * grid[a] + pl.program_id(a), range(len(grid)), 0)
        steps = math.prod(grid)

        @pl.when(step == 0)
        def _():
            ex.start(ex_in, ex_out, *sems)

        body(*ins, *outs, *scratch)

        if hasattr(ex, "relay"):
            @pl.when(step == (3 * steps) // 4)
            def _():
                ex.relay(ex_in, ex_out, *sems)

        @pl.when(step == steps - 1)
        def _():
            ex.finish(ex_in, ex_out, *sems)

    return carried


def _carry_specs(ex):
    if ex is None:
        return [], [], [], [], []
    sems = [pltpu.SemaphoreType.DMA((ex.n_sem,)), pltpu.SemaphoreType.DMA((ex.n_sem,))]
    return [ANY] * len(ex.arrays), ex.arrays, [ANY] * len(ex.out_shape), ex.out_shape, sems


def _gather_weights(big, small, name):
    nbig, nsm = len(big), len(small)
    n = nbig + nsm
    own_sem = 6 * nbig + 3 * nsm

    def body(*refs):
        ins, outs = refs[:n], refs[n:2 * n]
        send_sems, recv_sems = refs[2 * n:]
        x, y, c = _me()
        sibling = (x, y, 1 - c)
        chips = _other_chips(x, y)
        slot = lambda chip: 2 * chip[0] + chip[1]

        def copy(k, src, dst, to):
            return pltpu.make_async_remote_copy(src_ref=src, dst_ref=dst, send_sem=send_sems.at[k],
                                                recv_sem=recv_sems.at[k], device_id=to, device_id_type=MESH)

        sent = []
        for i in range(nbig):
            sent.append(copy(own_sem + i, ins[i], outs[i].at[slot((x, y))], sibling))
            sent[-1].start()
            for j, chip in enumerate(chips):
                sent.append(copy(6 * i + j, _half_rows(ins[i], c), _half_rows(outs[i].at[slot((x, y))], c), (*chip, c)))
                sent[-1].start()
        for t in range(nsm):
            w_ref, o_ref = ins[nbig + t], outs[nbig + t]
            o_ref[slot((x, y))] = w_ref[...]
            for j, chip in enumerate(chips):
                sent.append(copy(6 * nbig + 3 * t + j, w_ref, o_ref.at[slot((x, y))], (*chip, c)))
                sent[-1].start()
        for i in range(nbig):
            for j, chip in enumerate(chips):
                landed = _half_rows(outs[i].at[slot(chip)], c)
                copy(6 * i + j, landed, landed, (x, y, c)).wait_recv()
                sent.append(copy(6 * i + 3 + j, landed, landed, sibling))
                sent[-1].start()
        for t in range(nsm):
            for j, chip in enumerate(chips):
                landed = outs[nbig + t].at[slot(chip)]
                copy(6 * nbig + 3 * t + j, landed, landed, (x, y, c)).wait_recv()
        for i in range(nbig):
            for j, chip in enumerate(chips):
                passed = _half_rows(outs[i].at[slot(chip)], 1 - c)
                copy(6 * i + 3 + j, passed, passed, (x, y, c)).wait_recv()
        for i in range(nbig):
            mine = outs[i].at[slot((x, y))]
            copy(own_sem + i, mine, mine, (x, y, c)).wait_recv()
        for cp in sent:
            cp.wait_send()

    vm = pl.BlockSpec(memory_space=pltpu.VMEM)
    nsem = own_sem + nbig
    return pl.pallas_call(
        body, name=name, in_specs=[ANY] * nbig + [vm] * nsm, out_specs=[ANY] * nbig + [vm] * nsm,
        out_shape=[jax.ShapeDtypeStruct((N_SHARD,) + w.shape, w.dtype) for w in list(big) + list(small)],
        scratch_shapes=[pltpu.SemaphoreType.DMA((nsem,)), pltpu.SemaphoreType.DMA((nsem,))],
        compiler_params=pltpu.CompilerParams(has_side_effects=True),
    )(*big, *small)


def _rs_sibling(gs, name):
    n = len(gs)

    def body(*refs):
        send_sems, recv_sems = refs[2 * n:]
        x, y, c = _me()
        cps = [pltpu.make_async_remote_copy(src_ref=_half_rows(refs[i], 1 - c), dst_ref=refs[n + i],
                                            send_sem=send_sems.at[i], recv_sem=recv_sems.at[i],
                                            device_id=(x, y, 1 - c), device_id_type=MESH) for i in range(n)]
        for cp in cps:
            cp.start()
        for cp in cps:
            cp.wait()

    return pl.pallas_call(
        body, name=name, in_specs=[ANY] * n, out_specs=[ANY] * n,
        out_shape=[jax.ShapeDtypeStruct((g.shape[0], g.shape[1] // 2, g.shape[2]), g.dtype) for g in gs],
        scratch_shapes=[pltpu.SemaphoreType.DMA((n,)), pltpu.SemaphoreType.DMA((n,))],
        compiler_params=pltpu.CompilerParams(has_side_effects=True),
    )(*gs)


def _add_pair(g, other, where, name):
    ns, a, b = g.shape
    half = a // 2

    def body(w_ref, g_ref, o_ref, pb_ref, own_ref):
        t = g_ref[0].astype(F32) + o_ref[0].astype(F32)
        pb_ref[0] = _b(t)

        @pl.when(pl.program_id(0) == w_ref[1])
        def _():
            own_ref[...] = t

    return pl.pallas_call(
        body, name=name,
        grid_spec=pltpu.PrefetchScalarGridSpec(
            num_scalar_prefetch=1, grid=(ns,),
            in_specs=[pl.BlockSpec((1, half, b), lambda s, w: (s, w[0], 0)), pl.BlockSpec((1, half, b), lambda s, w: (s, 0, 0))],
            out_specs=[pl.BlockSpec((1, half, b), lambda s, w: (s, 0, 0)), pl.BlockSpec((half, b), lambda s, w: (0, 0))]),
        out_shape=[jax.ShapeDtypeStruct((ns, half, b), BF16), jax.ShapeDtypeStruct((half, b), F32)],
        compiler_params=_cp("arbitrary"),
    )(where, g, other)


def _add_four(own, got, name):
    rows, cols = own.shape
    rb = rows // 2

    def body(a_ref, b_ref, o_ref):
        o_ref[...] = ((a_ref[...] + b_ref[0].astype(F32)) + b_ref[1].astype(F32)) + b_ref[2].astype(F32)

    return pl.pallas_call(
        body, name=name, grid=(rows // rb,),
        in_specs=[pl.BlockSpec((rb, cols), lambda i: (i, 0)), pl.BlockSpec((3, rb, cols), lambda i: (0, i, 0))],
        out_specs=pl.BlockSpec((rb, cols), lambda i: (i, 0)),
        out_shape=jax.ShapeDtypeStruct((rows, cols), F32),
        compiler_params=_cp("parallel"),
    )(own, got)


def _rs_swap(halves, name):
    n = len(halves)

    def body(*refs):
        send_sems, recv_sems = refs[2 * n:]
        x, y, c = _me()
        cps = [pltpu.make_async_remote_copy(src_ref=refs[i], dst_ref=refs[n + i], send_sem=send_sems.at[i],
                                            recv_sem=recv_sems.at[i], device_id=(x, y, 1 - c), device_id_type=MESH)
               for i in range(n)]
        for cp in cps:
            cp.start()
        for cp in cps:
            cp.wait()

    return pl.pallas_call(
        body, name=name, in_specs=[ANY] * n, out_specs=[ANY] * n,
        out_shape=[jax.ShapeDtypeStruct(h.shape, h.dtype) for h in halves],
        scratch_shapes=[pltpu.SemaphoreType.DMA((n,)), pltpu.SemaphoreType.DMA((n,))],
        compiler_params=pltpu.CompilerParams(has_side_effects=True),
    )(*halves)


def _adamw_halves(w, own, got, m, v, rows, name):
    a, b = w.shape
    nblk = a // 2 // rows
    c1 = 1.0 - ADAM_B1 ** ADAM_STEP
    c2 = 1.0 - ADAM_B2 ** ADAM_STEP

    def body(w_ref, own_ref, got_ref, m_ref, v_ref, g_ref, d_ref, mo_ref, vo_ref):
        g_ = jnp.where(pl.program_id(0) == lax.axis_index("c"), own_ref[...], got_ref[...])
        g_ref[...] = g_
        m_ = ADAM_B1 * m_ref[...] + (1.0 - ADAM_B1) * g_
        v_ = ADAM_B2 * v_ref[...] + (1.0 - ADAM_B2) * (g_ * g_)
        mo_ref[...] = m_
        vo_ref[...] = v_
        d_ref[...] = -ADAM_LR * ((m_ / c1) / (jnp.sqrt(v_ / c2) + ADAM_EPS) + ADAM_WD * w_ref[...])

    whole = pl.BlockSpec((rows, b), lambda h, i: (h * nblk + i, 0))
    part = pl.BlockSpec((rows, b), lambda h, i: (i, 0))
    shp = jax.ShapeDtypeStruct((a, b), F32)
    return pl.pallas_call(
        body, name=name, grid=(2, nblk),
        in_specs=[whole, part, part, whole, whole], out_specs=[whole] * 4, out_shape=[shp] * 4,
        compiler_params=_cp("parallel", "parallel"),
    )(w, own, got, m, v)


def _allsum_small(vec, name):
    def body(v_ref, o_ref, buf_ref, send_sems, recv_sems):
        x, y, c = _me()
        me = 4 * x + 2 * y + c
        buf_ref[me] = v_ref[...]
        cps = []
        for k in range(1, 8):
            peer = (x ^ (k >> 2), y ^ ((k >> 1) & 1), c ^ (k & 1))
            cps.append(pltpu.make_async_remote_copy(src_ref=v_ref, dst_ref=buf_ref.at[me],
                                                    send_sem=send_sems.at[k - 1], recv_sem=recv_sems.at[k - 1],
                                                    device_id=peer, device_id_type=MESH))
        for cp in cps:
            cp.start()
        for k in range(1, 8):
            peer_idx = me ^ k
            pltpu.make_async_remote_copy(src_ref=v_ref, dst_ref=buf_ref.at[peer_idx],
                                         send_sem=send_sems.at[k - 1], recv_sem=recv_sems.at[k - 1],
                                         device_id=(x, y, c), device_id_type=MESH).wait_recv()
        for cp in cps:
            cp.wait_send()
        acc = buf_ref[0]
        for d in range(1, 8):
            acc = acc + buf_ref[d]
        o_ref[...] = acc

    return pl.pallas_call(
        body, name=name,
        in_specs=[pl.BlockSpec(memory_space=pltpu.VMEM)], out_specs=pl.BlockSpec(memory_space=pltpu.VMEM),
        out_shape=jax.ShapeDtypeStruct(vec.shape, F32),
        scratch_shapes=[pltpu.VMEM((8,) + vec.shape, F32), pltpu.SemaphoreType.DMA((7,)), pltpu.SemaphoreType.DMA((7,))],
        compiler_params=pltpu.CompilerParams(has_side_effects=True),
    )(vec)


BIG = ("ffn1_w_gate", "ffn1_w_up", "ffn1_w_down", "w_in", "w_out", "ffn2_w_gate", "ffn2_w_up", "ffn2_w_down")
TINY = ("w_gla_gate", "conv_w")
SHARDED = BIG + TINY


def _join_cols(w4):
    return jnp.transpose(w4, (1, 0, 2)).reshape(w4.shape[1], N_SHARD * w4.shape[2])


def _cut_cols(w):
    return jnp.transpose(w.reshape(w.shape[0], N_SHARD, w.shape[1] // N_SHARD), (1, 0, 2))


def _split_w_in(w):
    o = IN_OFF
    big = jnp.concatenate([w[:, :o[4]], w[:, o[5]:o[9]], w[:, o[11]:]], axis=1)
    small = jnp.concatenate([w[:, o[4]:o[5]], w[:, o[9]:o[11]], jnp.zeros((w.shape[0], SMALL - 32), w.dtype)], axis=1)
    return big, small


def _merge_w_in(big, small):
    return jnp.concatenate([big[:, :3072], small[:, :16], big[:, 3072:7168], small[:, 16:32], big[:, 7168:]], axis=1)


class _Comm:
    def __init__(self, where, rest_shards):
        self.where = where
        self.w_in = _GatherBig(rest_shards[:1])
        self.later = _GatherBig(rest_shards[1:])
        self.pairs, self.got = {}, {}

    @staticmethod
    def w_in_weights(gathered):
        return dict(zip(("w_in_big", "w_in_small"), _split_w_in(_join_cols(gathered[0]))))

    @staticmethod
    def later_weights(gathered):
        W = dict(zip(BIG[4:], gathered))
        W["w_out"] = W["w_out"].reshape(D, D)
        return W

    def pair(self, names, grads, from_sibling):
        for n, g, o in zip(names, grads, from_sibling):
            self.pairs[n] = _add_pair(g, o, self.where, "rs_pair_" + n)
        return _ChipsExchange([self.pairs[n][0] for n in names])

    def begin(self, names, grads):
        return self.pair(names, grads, _rs_sibling(grads, "rs_sibling_" + names[0]))

    def landed(self, names, outs):
        self.got.update(zip(names, outs))


def _local_step(x, target, W, P, comm=None):
    wgate_pad = jnp.zeros((SMALL, GLA_H * GLA_DK), F32).at[:GLA_RANK].set(P["w_gla_gate"])
    cw8 = jnp.zeros((8, CONV_C), F32).at[:CONV_K].set(P["conv_w"])
    par = jnp.zeros((DN_H, 8, 128), F32)
    par = par.at[:, 0, :].set(jnp.broadcast_to(P["dn_a_log"].reshape(DN_H, 1), (DN_H, 128)))
    par = par.at[:, 1, :].set(jnp.broadcast_to(P["dn_dt_bias"].reshape(DN_H, 1), (DN_H, 128)))

    h1, n1, g1, u1, *got = _ffn_fwd(x, P["ffn1_norm"], W["ffn1_w_gate"], W["ffn1_w_up"], W["ffn1_w_down"], "ffn1_fwd",
                                    carry=comm.w_in if comm else None)
    if comm:
        W = dict(W, **comm.w_in_weights(got))
    wbig, wsmall = W["w_in_big"], W["w_in_small"]
    pbig, psmall, n2, *got = _norm_proj(h1, P["mix_norm"], wbig, wsmall, "mix_proj", carry=comm.later if comm else None)
    if comm:
        W = dict(W, **comm.later_weights(got))
    oa, sa = _gla_fwd(pbig, psmall, wgate_pad, P["b_gla_gate"], "gla_fwd")
    conv = _conv_fwd(pbig, cw8, "conv_fwd")
    ob, sb, *solved = _gdn_fwd(conv, psmall, par, "gdn_fwd")
    h2, yb = _merge_fwd(h1, oa, ob, pbig, P["gla_head_norm"], P["dn_head_norm"], W["w_out"], "merge_fwd")
    h3, n3, g3, u3 = _ffn_fwd(h2, P["ffn2_norm"], W["ffn2_w_gate"], W["ffn2_w_up"], W["ffn2_w_down"], "ffn2_fwd")
    dh3, loss, d_final = _loss_head(h3, P["final_norm"], target, "loss_head")

    gw, gs = {}, {"final_norm": d_final}

    def ffn_grads(tag, dh, h, n, g, u, before=None, later=False):
        names = tuple(tag + s for s in ("_w_gate", "_w_up", "_w_down"))
        dg, du, act, dfb, *landed = _ffn_bwd_hidden(dh, g, u, W[names[2]], tag + "_bwd_hidden", carry=before)
        gw[names[0]] = _mm_tn(n, dg, D, FF_CUT, tag + "_dwg")
        gw[names[1]] = _mm_tn(n, du, D, FF_CUT, tag + "_dwu")
        gw[names[2]] = _mm_tn(act, dfb, FF_CUT, D, tag + "_dwd")
        mine = [gw[n] for n in names]
        ex = None if not comm else _SiblingExchange(mine) if later else comm.begin(names, mine)
        dx, gs[tag + "_norm"], *own = _ffn_bwd_input(dh, h, P[tag + "_norm"], dg, du, W[names[0]], W[names[1]],
                                                     tag + "_bwd_input", carry=ex)
        if comm and not later:
            comm.landed(names, own)
        return dx, landed, own

    second = ("ffn2_w_gate", "ffn2_w_up", "ffn2_w_down")
    dh2, _, swapped = ffn_grads("ffn2", dh3, h2, n3, g3, u3, later=True)
    d_oa, d_ob, d_gr, d_dgate, d_ma, d_mb, gs["gla_head_norm"], gs["dn_head_norm"], dh2b, *landed = _merge_bwd(
        dh2, oa, ob, pbig, P["gla_head_norm"], P["dn_head_norm"], W["w_out"], "merge_bwd",
        carry=comm.pair(second, [gw[n] for n in second], swapped) if comm else None)
    if comm:
        comm.landed(second, landed)
    gw["w_out"] = _mm_tn(yb, dh2b, D, D, "dw_out").reshape(N_SHARD, D // N_SHARD, D)
    early = ("w_out",)
    d_gq, d_gk, d_gv, dpre, *swapped = _gla_bwd(pbig, psmall, wgate_pad, P["b_gla_gate"], sa, d_oa, "gla_bwd",
                                                carry=_SiblingExchange([gw["w_out"]]) if comm else None)
    dcq, dck, dcv, dsm, dpar, *landed = _gdn_bwd(conv, psmall, par, sb, d_ob, solved, "gdn_bwd",
                                                 carry=comm.pair(early, [gw["w_out"]], swapped) if comm else None)
    if comm:
        comm.landed(early, landed)
    dsmall, dwgate, gs["b_gla_gate"] = _gla_gate_bwd(dpre, psmall, wgate_pad, dsm, "gla_gate_bwd")
    gs["w_gla_gate"] = dwgate[:GLA_RANK]
    d_x3, dcw = _conv_bwd(dcq, dck, dcv, pbig, cw8, "conv_bwd")
    gs["conv_w"] = dcw[:CONV_K]
    gs["dn_a_log"] = dpar[:, 0, 0].reshape(1, DN_H)
    gs["dn_dt_bias"] = dpar[:, 0, 1].reshape(1, DN_H)
    pieces = (d_gq, d_gk, d_gv, d_gr, d_x3, d_dgate, d_ma, d_mb)
    dh1, gs["mix_norm"] = _proj_bwd(dh2, h1, P["mix_norm"], pieces, dsmall, wbig, wsmall, "proj_bwd")
    dbig = jnp.concatenate([_mm_tn(n2, p, D, 1024, "dw_in_%d" % i) for i, p in enumerate(pieces)], axis=1)
    dsml = _mm_tn(n2, dsmall, D, SMALL, "dw_in_small")
    gw["w_in"] = _cut_cols(_merge_w_in(dbig, dsml))
    grad_x, landed, _ = ffn_grads("ffn1", dh1, x, n1, g1, u1,
                                  before=comm.begin(("w_in",), [gw["w_in"]]) if comm else None)
    if comm:
        comm.landed(("w_in",), landed)
    return loss, grad_x, gw, gs


SMALL_NAMES = ("ffn1_norm", "mix_norm", "ffn2_norm", "final_norm", "b_gla_gate", "gla_head_norm", "dn_head_norm",
               "dn_a_log", "dn_dt_bias")
ROW4 = (("b_gla_gate", 512), ("gla_head_norm", 256), ("dn_head_norm", 128), ("dn_a_log", 8), ("dn_dt_bias", 8))


def _pack_small(d, loss=None):
    row4 = [d[n].reshape(-1) for n, _ in ROW4]
    row4.append(jnp.zeros((1,), F32) if loss is None else loss.reshape(1))
    row4 = jnp.concatenate(row4)
    row4 = jnp.pad(row4, (0, D - row4.shape[0]))
    rows = [d[n].reshape(-1) for n in SMALL_NAMES[:4]] + [row4]
    return jnp.concatenate([jnp.stack(rows), jnp.zeros((3, D), F32)], axis=0)


def _unpack_small(a, like):
    out = {n: a[i].reshape(like[n].shape) for i, n in enumerate(SMALL_NAMES[:4])}
    off = 0
    for n, w in ROW4:
        out[n] = a[4, off:off + w].reshape(like[n].shape)
        off += w
    return out, a[4, off]


WEIGHT_ORDER = ("ffn1_norm", "ffn1_w_gate", "ffn1_w_up", "ffn1_w_down", "mix_norm", "w_in", "w_gla_gate", "b_gla_gate",
                "conv_w", "dn_a_log", "dn_dt_bias", "gla_head_norm", "dn_head_norm", "w_out", "ffn2_norm",
                "ffn2_w_gate", "ffn2_w_up", "ffn2_w_down", "final_norm")
ADAM_ROWS = {"ffn1_w_gate": 256, "ffn1_w_up": 256, "ffn1_w_down": 176, "w_in": 128, "w_gla_gate": 16, "conv_w": 4,
             "w_out": 64, "ffn2_w_gate": 256, "ffn2_w_up": 256, "ffn2_w_down": 176}


def kernel(x, ffn1_norm, ffn1_w_gate, ffn1_w_up, ffn1_w_down, mix_norm, w_in, w_gla_gate, b_gla_gate, conv_w, dn_a_log, dn_dt_bias, gla_head_norm, dn_head_norm, w_out, ffn2_norm, ffn2_w_gate, ffn2_w_up, ffn2_w_down, final_norm, loss_target, m_ffn1_norm, m_ffn1_w_gate, m_ffn1_w_up, m_ffn1_w_down, m_mix_norm, m_w_in, m_w_gla_gate, m_b_gla_gate, m_conv_w, m_dn_a_log, m_dn_dt_bias, m_gla_head_norm, m_dn_head_norm, m_w_out, m_ffn2_norm, m_ffn2_w_gate, m_ffn2_w_up, m_ffn2_w_down, m_final_norm, v_ffn1_norm, v_ffn1_w_gate, v_ffn1_w_up, v_ffn1_w_down, v_mix_norm, v_w_in, v_w_gla_gate, v_b_gla_gate, v_conv_w, v_dn_a_log, v_dn_dt_bias, v_gla_head_norm, v_dn_head_norm, v_w_out, v_ffn2_norm, v_ffn2_w_gate, v_ffn2_w_up, v_ffn2_w_down, v_final_norm):
    given = dict(locals())
    wts = {n: given[n] for n in WEIGHT_ORDER}
    moms = {n: given["m_" + n] for n in WEIGHT_ORDER}
    vars_ = {n: given["v_" + n] for n in WEIGHT_ORDER}
    two_d = lambda a: a.reshape(a.shape[-2], a.shape[-1]) if a.ndim == 3 else a.reshape(1, -1)
    shard = {n: two_d(wts[n]) for n in SHARDED}

    gathered = _gather_weights([shard[n].astype(BF16) for n in BIG[:3]], [shard[n] for n in TINY], "gather_first")
    W = dict(zip(BIG[:3], gathered))
    P = {n: two_d(wts[n]) for n in SMALL_NAMES}
    for n, g in zip(TINY, gathered[3:]):
        P[n] = _join_cols(g)

    my_slot = 2 * lax.axis_index("x") + lax.axis_index("y")
    where = jnp.stack([lax.axis_index("c"), my_slot]).astype(jnp.int32)
    comm = _Comm(where, [shard[n].astype(BF16) for n in BIG[3:]])
    loss, grad_x, gw, gs = _local_step(x[0], loss_target[0], W, P, comm)
    halves = [_add_four(comm.pairs[n][1], comm.got[n], "rs_four_" + n) for n in BIG]
    other_halves = _rs_swap(halves, "rs_swap")

    tiny_rows = jnp.concatenate([gs["w_gla_gate"].reshape(8, D), gs["conv_w"].reshape(12, D), jnp.zeros((4, D), F32)])
    all_sum = _allsum_small(jnp.concatenate([_pack_small(gs, loss[0, 0]), tiny_rows]), "allsum_small")
    small_sum = all_sum[:8]
    small_g, loss_total = _unpack_small(small_sum, P)

    grads, delta, new_m, new_v = {}, {}, {}, {}
    for n, own, got in zip(BIG, halves, other_halves):
        res = _adamw_halves(shard[n], own, got, two_d(moms[n]), two_d(vars_[n]), ADAM_ROWS[n], "adamw_" + n)
        grads[n], delta[n], new_m[n], new_v[n] = (t.reshape(wts[n].shape) for t in res)
    for n, rows in (("w_gla_gate", all_sum[8:16]), ("conv_w", all_sum[16:28])):
        cols = shard[n].shape[1]
        grads[n] = lax.dynamic_slice_in_dim(rows.reshape(shard[n].shape[0], N_SHARD * cols), my_slot * cols, cols, axis=1)
        d, m_, v_ = _adamw(shard[n], grads[n], two_d(moms[n]), two_d(vars_[n]), ADAM_ROWS[n], "adamw_" + n)
        delta[n], new_m[n], new_v[n] = (t.reshape(wts[n].shape) for t in (d, m_, v_))
    pk = lambda src: _pack_small({n: two_d(src[n]) for n in SMALL_NAMES})
    sd, sm_, sv_ = _adamw(pk(wts), small_sum, pk(moms), pk(vars_), 8, "adamw_small")
    for res, dst in ((sd, delta), (sm_, new_m), (sv_, new_v)):
        u, _ = _unpack_small(res, wts)
        dst.update(u)
    grad_w = {n: grads[n].reshape(wts[n].shape) for n in SHARDED}
    grad_w.update({n: small_g[n].reshape(wts[n].shape) for n in SMALL_NAMES})
    return (loss_total, grad_x[None], *[grad_w[n] for n in WEIGHT_ORDER], *[delta[n] for n in WEIGHT_ORDER],
            *[new_m[n] for n in WEIGHT_ORDER], *[new_v[n] for n in WEIGHT_ORDER])
```

```python
import functools
import math

import numpy as np
import jax
import jax.numpy as jnp
from jax import lax
from jax.experimental import pallas as pl
from jax.experimental.pallas import tpu as pltpu

F32 = jnp.float32
BF16 = jnp.bfloat16
HI = lax.Precision.HIGH
MESH = pl.DeviceIdType.MESH
ANY = pl.BlockSpec(memory_space=pl.ANY)

EPS = 1e-6
D = 1024
DFF = 2816
FFN_RES = 0.5
GLA_H, GLA_DK, GLA_DV, GLA_RANK, GLA_TAU = 4, 128, 256, 16, 16.0
DN_H, DN_DK, DN_DV = 8, 128, 128
CONV_K = 4
CHUNK = 64
N_SHARD = 4
FF_CUT = DFF // N_SHARD
ADAM_LR, ADAM_B1, ADAM_B2, ADAM_EPS, ADAM_WD, ADAM_STEP = 0.001, 0.9, 0.999, 1e-08, 0.01, 10

IN_SIZES = (512, 512, 1024, 1024, 16, 1024, 1024, 1024, 1024, 8, 8, 1024, 1024)
IN_OFF = tuple(int(v) for v in np.cumsum((0,) + IN_SIZES))
BIG_COLS = 9216
SMALL = 128
PIECES = (512, 512, 1024, 1024, 3072, 1024, 1024, 1024)

VMEM_LIMIT = 56 * 1024 * 1024
ROW_BLK = 256
BIG_ROW_BLK = 512
ATT_BLK = 512
GDN_BLK = 256
GDN_HEADS = 8


def _cp(*sem):
    return pltpu.CompilerParams(dimension_semantics=sem, vmem_limit_bytes=VMEM_LIMIT)


def _sigmoid(x):
    return 1.0 / (1.0 + jnp.exp(-x))


def _softplus(x):
    return jnp.maximum(x, 0.0) + jnp.log(1.0 + jnp.exp(-jnp.abs(x)))


def _log_sigmoid(x):
    return jnp.minimum(x, 0.0) - jnp.log(1.0 + jnp.exp(-jnp.abs(x)))


def _dot(a, b, prec=None):
    return jnp.dot(a, b, preferred_element_type=F32, precision=prec)


def _dot_nt(a, b, prec=None):
    return lax.dot_general(a, b, (((1,), (1,)), ((), ())), preferred_element_type=F32, precision=prec)


def _dot_tn(a, b, prec=None):
    return lax.dot_general(a, b, (((0,), (0,)), ((), ())), preferred_element_type=F32, precision=prec)


def _b(x):
    return x.astype(BF16)


def _iota2(n, m, axis):
    return lax.broadcasted_iota(jnp.int32, (n, m), axis)


def _load_weights(pairs, sem):
    copies = [pltpu.make_async_copy(s, d, sem.at[i]) for i, (s, d) in enumerate(pairs)]
    for c in copies:
        c.start()
    for c in copies:
        c.wait()


def _ffn_fwd(h, nw, wg, wu, wd, name, carry=None, head=None):
    T = h.shape[0]
    tm = min(BIG_ROW_BLK, T)
    ex_in, ex_args, ex_out, ex_shape, ex_sems = _carry_specs(carry)
    n_head = 2 if head else 0

    def body(*refs):
        h_ref, nw_ref, wg_hbm, wu_hbm, wd_hbm = refs[:5]
        ho_ref, n_ref, g_ref, u_ref = refs[5 + n_head:9 + n_head]
        wg_v, wu_v, wd_v, sem = refs[9 + 2 * n_head:]

        @pl.when(pl.program_id(0) == 0)
        def _():
            _load_weights(((wg_hbm, wg_v), (wu_hbm, wu_v), (wd_hbm, wd_v)), sem)

        x = h_ref[...]
        r = lax.rsqrt(jnp.mean(x * x, axis=-1, keepdims=True) + EPS)
        nb = _b((x * r) * nw_ref[...])
        n_ref[...] = nb
        acc = jnp.zeros((tm, D), F32)
        for s in range(N_SHARD):
            g = _dot(nb, wg_v[s])
            u = _dot(nb, wu_v[s])
            g_ref[s] = _b(g)
            u_ref[s] = _b(u)
            acc += _dot(_b(g * _sigmoid(g) * u), wd_v[s])
        out = x + FFN_RES * acc
        if not head:
            ho_ref[...] = out
            return
        fw_ref, t_ref = refs[5:7]
        loss_ref, dfw_ref = refs[11:13]

        @pl.when(pl.program_id(0) == 0)
        def _():
            loss_ref[...] = jnp.zeros_like(loss_ref)
            dfw_ref[...] = jnp.zeros_like(dfw_ref)

        w = fw_ref[...]
        r = lax.rsqrt(jnp.mean(out * out, axis=-1, keepdims=True) + EPS)
        xhat = out * r
        err = xhat * w - t_ref[...]
        loss_ref[...] += (0.5 / D) * jnp.sum(jnp.sum(err * err, axis=-1, keepdims=True), axis=0, keepdims=True)
        dout = err * (1.0 / D)
        dfw_ref[...] += jnp.sum(dout * xhat, axis=0, keepdims=True)
        dxhat = dout * w
        ho_ref[...] = r * (dxhat - xhat * jnp.mean(dxhat * xhat, axis=-1, keepdims=True))

    row = lambda w: pl.BlockSpec((tm, w), lambda i: (i, 0))
    one = pl.BlockSpec((1, D), lambda i: (0, 0))
    cut = pl.BlockSpec((N_SHARD, tm, FF_CUT), lambda i: (0, i, 0))
    head_out = [pl.BlockSpec((8, 128), lambda i: (0, 0)), one] if head else []
    head_shape = [jax.ShapeDtypeStruct((8, 128), F32), jax.ShapeDtypeStruct((1, D), F32)] if head else []
    return pl.pallas_call(
        _carry(carry, body, 5 + n_head, 4 + n_head, (T // tm,)), name=name, grid=(T // tm,),
        in_specs=[row(D), one, ANY, ANY, ANY] + ([one, row(D)] if head else []) + ex_in,
        out_specs=[row(D), row(D), cut, cut] + head_out + ex_out,
        out_shape=[jax.ShapeDtypeStruct((T, D), F32), jax.ShapeDtypeStruct((T, D), BF16),
                   jax.ShapeDtypeStruct((N_SHARD, T, FF_CUT), BF16),
                   jax.ShapeDtypeStruct((N_SHARD, T, FF_CUT), BF16)] + head_shape + ex_shape,
        scratch_shapes=[pltpu.VMEM((N_SHARD, D, FF_CUT), BF16), pltpu.VMEM((N_SHARD, D, FF_CUT), BF16),
                        pltpu.VMEM((N_SHARD, FF_CUT, D), BF16), pltpu.SemaphoreType.DMA((3,))] + ex_sems,
        compiler_params=_cp("arbitrary"),
    )(h, nw, wg, wu, wd, *(head or ()), *ex_args)


def _ffn_bwd_hidden(dh, g, u, wd, name, carry=None):
    T = dh.shape[0]
    tm = min(BIG_ROW_BLK, T)
    ex_in, ex_args, ex_out, ex_shape, ex_sems = _carry_specs(carry)

    def body(dh_ref, g_ref, u_ref, wd_hbm, dg_ref, du_ref, a_ref, df_ref, wd_v, sem):
        @pl.when(pl.program_id(0) == 0)
        def _():
            _load_weights(((wd_hbm, wd_v),), sem)

        dfb = _b(FFN_RES * dh_ref[...])
        df_ref[...] = dfb
        for s in range(N_SHARD):
            da = _dot_nt(dfb, wd_v[s])
            gg = g_ref[s].astype(F32)
            uu = u_ref[s].astype(F32)
            sg = _sigmoid(gg)
            silu = gg * sg
            a_ref[s] = _b(silu * uu)
            dg_ref[s] = _b(da * uu * (sg * (1.0 + gg * (1.0 - sg))))
            du_ref[s] = _b(da * silu)

    row = pl.BlockSpec((tm, D), lambda i: (i, 0))
    cut = pl.BlockSpec((N_SHARD, tm, FF_CUT), lambda i: (0, i, 0))
    cut_shape = jax.ShapeDtypeStruct((N_SHARD, T, FF_CUT), BF16)
    return pl.pallas_call(
        _carry(carry, body, 4, 4, (T // tm,)), name=name, grid=(T // tm,),
        in_specs=[row, cut, cut, ANY] + ex_in,
        out_specs=[cut, cut, cut, row] + ex_out,
        out_shape=[cut_shape, cut_shape, cut_shape, jax.ShapeDtypeStruct((T, D), BF16)] + ex_shape,
        scratch_shapes=[pltpu.VMEM((N_SHARD, FF_CUT, D), BF16), pltpu.SemaphoreType.DMA((1,))] + ex_sems,
        compiler_params=_cp("arbitrary"),
    )(dh, g, u, wd, *ex_args)


def _ffn_bwd_input(dh, h, nw, dg, du, wg, wu, name, carry=None):
    T = h.shape[0]
    tm = min(BIG_ROW_BLK, T)
    ex_in, ex_args, ex_out, ex_shape, ex_sems = _carry_specs(carry)

    def body(dh_ref, h_ref, nw_ref, dg_ref, du_ref, wg_hbm, wu_hbm, dx_ref, dnw_ref, wg_v, wu_v, sem):
        @pl.when(pl.program_id(0) == 0)
        def _():
            _load_weights(((wg_hbm, wg_v), (wu_hbm, wu_v)), sem)
            dnw_ref[...] = jnp.zeros_like(dnw_ref)

        dn = jnp.zeros((tm, D), F32)
        for s in range(N_SHARD):
            dn += _dot_nt(dg_ref[s], wg_v[s]) + _dot_nt(du_ref[s], wu_v[s])
        x = h_ref[...]
        r = lax.rsqrt(jnp.mean(x * x, axis=-1, keepdims=True) + EPS)
        xhat = x * r
        dnw_ref[...] += jnp.sum(dn * xhat, axis=0, keepdims=True)
        dxhat = dn * nw_ref[...]
        dx_ref[...] = dh_ref[...] + r * (dxhat - xhat * jnp.mean(dxhat * xhat, axis=-1, keepdims=True))

    row = pl.BlockSpec((tm, D), lambda i: (i, 0))
    one = pl.BlockSpec((1, D), lambda i: (0, 0))
    cut = pl.BlockSpec((N_SHARD, tm, FF_CUT), lambda i: (0, i, 0))
    return pl.pallas_call(
        _carry(carry, body, 7, 2, (T // tm,)), name=name, grid=(T // tm,),
        in_specs=[row, row, one, cut, cut, ANY, ANY] + ex_in,
        out_specs=[row, one] + ex_out,
        out_shape=[jax.ShapeDtypeStruct((T, D), F32), jax.ShapeDtypeStruct((1, D), F32)] + ex_shape,
        scratch_shapes=[pltpu.VMEM((N_SHARD, D, FF_CUT), BF16), pltpu.VMEM((N_SHARD, D, FF_CUT), BF16),
                        pltpu.SemaphoreType.DMA((2,))] + ex_sems,
        compiler_params=_cp("arbitrary"),
    )(dh, h, nw, dg, du, wg, wu, *ex_args)


def _mm_tn(a, b, bm, bn, name, out_dtype=BF16, tk=2048):
    cuts = a.shape[0] if a.ndim == 3 else (b.shape[0] if b.ndim == 3 else None)
    T, M = a.shape[-2:]
    N = b.shape[-1]
    tk = min(tk, T)
    bm, bn = min(bm, M), min(bn, N)
    nk = T // tk

    def body(a_ref, b_ref, o_ref, acc_ref):
        k = pl.program_id(3)

        @pl.when(k == 0)
        def _():
            acc_ref[...] = jnp.zeros_like(acc_ref)

        av = a_ref[0] if a.ndim == 3 else a_ref[...]
        bv = b_ref[0] if b.ndim == 3 else b_ref[...]
        acc_ref[...] += _dot_tn(_b(av), _b(bv))

        @pl.when(k == nk - 1)
        def _():
            res = acc_ref[...].astype(out_dtype)
            if cuts is None:
                o_ref[...] = res
            else:
                o_ref[0] = res

    a_spec = (pl.BlockSpec((1, tk, bm), lambda s, i, j, k: (s, k, i)) if a.ndim == 3
              else pl.BlockSpec((tk, bm), lambda s, i, j, k: (k, i)))
    b_spec = (pl.BlockSpec((1, tk, bn), lambda s, i, j, k: (s, k, j)) if b.ndim == 3
              else pl.BlockSpec((tk, bn), lambda s, i, j, k: (k, j)))
    if cuts is None:
        o_spec, o_shape = pl.BlockSpec((bm, bn), lambda s, i, j, k: (i, j)), (M, N)
    else:
        o_spec, o_shape = pl.BlockSpec((1, bm, bn), lambda s, i, j, k: (s, i, j)), (cuts, M, N)
    return pl.pallas_call(
        body, name=name, grid=(cuts or 1, M // bm, N // bn, nk),
        in_specs=[a_spec, b_spec], out_specs=o_spec,
        out_shape=jax.ShapeDtypeStruct(o_shape, out_dtype),
        scratch_shapes=[pltpu.VMEM((bm, bn), F32)],
        compiler_params=_cp("parallel", "parallel", "parallel", "arbitrary"),
    )(a, b)


def _norm_proj(h, nw, wbig, wsmall, name, carry=None):
    T = h.shape[0]
    tm = min(512, T)
    tn = 1536
    ex_in, ex_args, ex_out, ex_shape, ex_sems = _carry_specs(carry)

    def body(h_ref, nw_ref, wb_hbm, ws_ref, pb_ref, ps_ref, n_ref, wb_v, sem):
        @pl.when(pl.program_id(0) == 0)
        def _():
            _load_weights(((wb_hbm, wb_v),), sem)

        x = h_ref[...]
        r = lax.rsqrt(jnp.mean(x * x, axis=-1, keepdims=True) + EPS)
        nb = _b((x * r) * nw_ref[...])
        n_ref[...] = nb
        ps_ref[...] = _dot(nb, ws_ref[...])
        for j in range(BIG_COLS // tn):
            pb_ref[:, j * tn:(j + 1) * tn] = _b(_dot(nb, wb_v[:, j * tn:(j + 1) * tn]))

    row = lambda w: pl.BlockSpec((tm, w), lambda i: (i, 0))
    return pl.pallas_call(
        _carry(carry, body, 4, 3, (T // tm,)), name=name, grid=(T // tm,),
        in_specs=[row(D), pl.BlockSpec((1, D), lambda i: (0, 0)), ANY, pl.BlockSpec((D, SMALL), lambda i: (0, 0))]
        + ex_in,
        out_specs=[row(BIG_COLS), row(SMALL), row(D)] + ex_out,
        out_shape=[jax.ShapeDtypeStruct((T, BIG_COLS), BF16), jax.ShapeDtypeStruct((T, SMALL), F32),
                   jax.ShapeDtypeStruct((T, D), BF16)] + ex_shape,
        scratch_shapes=[pltpu.VMEM((D, BIG_COLS), BF16), pltpu.SemaphoreType.DMA((1,))] + ex_sems,
        compiler_params=_cp("arbitrary"),
    )(h, nw, wbig, wsmall, *ex_args)


def _proj_bwd(dh, h, nw, pieces, dsmall, wbig, wsmall, name):
    T = h.shape[0]
    tm = min(BIG_ROW_BLK, T)
    offs = tuple(int(v) for v in np.cumsum((0,) + PIECES))

    def body(dh_ref, h_ref, nw_ref, *rest):
        p_refs = rest[:len(PIECES)]
        ds_ref, wb_hbm, ws_ref, dx_ref, dnw_ref, wb_v, sem = rest[len(PIECES):]

        @pl.when(pl.program_id(0) == 0)
        def _():
            _load_weights(((wb_hbm, wb_v),), sem)
            dnw_ref[...] = jnp.zeros_like(dnw_ref)

        dn = _dot_nt(_b(ds_ref[...]), ws_ref[...])
        for p_ref, lo, wdt in zip(p_refs, offs, PIECES):
            dn += _dot_nt(p_ref[...], wb_v[:, lo:lo + wdt])
        x = h_ref[...]
        r = lax.rsqrt(jnp.mean(x * x, axis=-1, keepdims=True) + EPS)
        xhat = x * r
        dnw_ref[...] += jnp.sum(dn * xhat, axis=0, keepdims=True)
        dxhat = dn * nw_ref[...]
        dx_ref[...] = dh_ref[...] + r * (dxhat - xhat * jnp.mean(dxhat * xhat, axis=-1, keepdims=True))

    row = lambda w: pl.BlockSpec((tm, w), lambda i: (i, 0))
    one = pl.BlockSpec((1, D), lambda i: (0, 0))
    return pl.pallas_call(
        body, name=name, grid=(T // tm,),
        in_specs=[row(D), row(D), one] + [row(w) for w in PIECES] + [row(SMALL), ANY, pl.BlockSpec((D, SMALL), lambda i: (0, 0))],
        out_specs=[row(D), one],
        out_shape=[jax.ShapeDtypeStruct((T, D), F32), jax.ShapeDtypeStruct((1, D), F32)],
        scratch_shapes=[pltpu.VMEM((D, BIG_COLS), BF16), pltpu.SemaphoreType.DMA((1,))],
        compiler_params=_cp("arbitrary"),
    )(dh, h, nw, *pieces, dsmall, wbig, wsmall)


def _gla_block(q_ref, k_ref, sm_ref, wg_ref, bg_ref, nc, tril):
    nbat = GLA_H * nc
    q = _heads_first(q_ref[...].astype(F32), nc, GLA_DK)
    k = _heads_first(k_ref[...].astype(F32), nc, GLA_DK)
    pre = _heads_first(_dot(sm_ref[...], wg_ref[...], HI) + bg_ref[...], nc, GLA_DK)
    la = _log_sigmoid(pre) * (1.0 / GLA_TAU)
    bc = _bmm(jnp.broadcast_to(tril, (nbat, CHUNK, CHUNK)), la, HI)
    bl = bc[:, CHUNK - 1:CHUNK, :]
    eb = jnp.exp(bc)
    enb = jnp.exp(-bc)
    ebl = jnp.exp(bl - bc)
    q_in = q * (GLA_DK ** -0.5) * eb
    k_out = k * enb
    k_st = k * ebl
    a_ch = jnp.exp(bl)
    return pre, eb, enb, ebl, q_in, k_out, k_st, a_ch


def _gla_specs(blk, idx):
    hk, hv = GLA_H * GLA_DK, GLA_H * GLA_DV
    return [pl.BlockSpec((blk, hk), lambda j: (idx(j), 0)),
            pl.BlockSpec((blk, hk), lambda j: (idx(j), 1)),
            pl.BlockSpec((blk, hv), lambda j: (idx(j), 1)),
            pl.BlockSpec((blk, SMALL), lambda j: (idx(j), 0)),
            pl.BlockSpec((SMALL, hk), lambda j: (0, 0)),
            pl.BlockSpec((1, hk), lambda j: (0, 0))]


def _gla_fwd(pbig, psmall, wgate, bgate, name):
    T = pbig.shape[0]
    blk = min(ATT_BLK, T)
    nc = blk // CHUNK

    def body(q_ref, k_ref, v_ref, sm_ref, wg_ref, bg_ref, o_ref, ss_ref, st_ref):
        @pl.when(pl.program_id(0) == 0)
        def _():
            st_ref[...] = jnp.zeros_like(st_ref)

        causal = _iota2(CHUNK, CHUNK, 0) >= _iota2(CHUNK, CHUNK, 1)
        _, _, _, _, q_in, k_out, k_st, a_ch = _gla_block(q_ref, k_ref, sm_ref, wg_ref, bg_ref, nc, causal.astype(F32))
        v = _heads_first(v_ref[...], nc, GLA_DV)
        qb = _b(q_in)
        sc = jnp.where(causal, _bmm_nt(qb, _b(k_out)), 0.0)
        kv = _bmm_tn(v, _b(k_st))
        before = [None] * (GLA_H * nc)
        for i in range(GLA_H):
            st = st_ref[i]
            for c in range(nc):
                n = i * nc + c
                before[n] = st
                st = st * a_ch[n] + kv[n]
            st_ref[i] = st
        states = jnp.stack(before)
        ss_ref[...] = states.reshape(GLA_H, nc, GLA_DV, GLA_DK)
        o_ref[...] = _b(_heads_last(_bmm(_b(sc), v) + _bmm_nt(qb, _b(states)), nc))

    return pl.pallas_call(
        body, name=name, grid=(T // blk,),
        in_specs=_gla_specs(blk, lambda j: j),
        out_specs=[pl.BlockSpec((blk, GLA_H * GLA_DV), lambda j: (j, 0)),
                   pl.BlockSpec((GLA_H, nc, GLA_DV, GLA_DK), lambda j: (0, j, 0, 0))],
        out_shape=[jax.ShapeDtypeStruct((T, GLA_H * GLA_DV), BF16),
                   jax.ShapeDtypeStruct((GLA_H, T // CHUNK, GLA_DV, GLA_DK), F32)],
        scratch_shapes=[pltpu.VMEM((GLA_H, GLA_DV, GLA_DK), F32)],
        compiler_params=_cp("arbitrary"),
    )(pbig, pbig, pbig, psmall, wgate, bgate)


def _gla_bwd(pbig, psmall, wgate, bgate, states, do, name, carry=None):
    ex_in, ex_args, ex_out, ex_shape, ex_sems = _carry_specs(carry)
    T = pbig.shape[0]
    blk = min(ATT_BLK, T)
    nc = blk // CHUNK
    nb = T // blk
    nbat = GLA_H * nc

    def body(q_ref, k_ref, v_ref, sm_ref, wg_ref, bg_ref, ss_ref, do_ref, dq_ref, dk_ref, dv_ref, dpre_ref, dst_ref):
        @pl.when(pl.program_id(0) == 0)
        def _():
            dst_ref[...] = jnp.zeros_like(dst_ref)

        causal = _iota2(CHUNK, CHUNK, 0) >= _iota2(CHUNK, CHUNK, 1)
        triu = (_iota2(CHUNK, CHUNK, 0) <= _iota2(CHUNK, CHUNK, 1)).astype(F32)
        pre, eb, enb, ebl, q_in, k_out, k_st, a_ch = _gla_block(q_ref, k_ref, sm_ref, wg_ref, bg_ref, nc,
                                                                causal.astype(F32))
        v = _heads_first(v_ref[...], nc, GLA_DV)
        dob = _b(_heads_first(do_ref[...], nc, GLA_DV))
        st = ss_ref[...].reshape(nbat, GLA_DV, GLA_DK)
        qb, kob, kstb = _b(q_in), _b(k_out), _b(k_st)
        qdo = _bmm_tn(dob, qb)
        after = [None] * nbat
        for i in range(GLA_H):
            dst = dst_ref[i]
            for c in range(nc - 1, -1, -1):
                n = i * nc + c
                after[n] = dst
                dst = dst * a_ch[n] + qdo[n]
            dst_ref[i] = dst
        dsa = jnp.stack(after)
        dsb = _b(dsa)
        sc = jnp.where(causal, _bmm_nt(qb, kob), 0.0)
        dsc = _b(jnp.where(causal, _bmm_nt(dob, v), 0.0))
        dq_in = _bmm(dob, _b(st)) + _bmm(dsc, kob)
        dk_out = _bmm_tn(dsc, qb)
        dk_st = _bmm(v, dsb)
        dv_ref[...] = _b(_heads_last(_bmm_tn(_b(sc), dob) + _bmm_nt(kstb, dsb), nc))
        da_ch = jnp.sum(st * dsa, axis=1, keepdims=True)
        tk = dk_st * k_st
        db = dq_in * q_in - dk_out * k_out - tk
        db_last = jnp.sum(tk, axis=1, keepdims=True) + da_ch * a_ch
        dq_ref[...] = _b(_heads_last(dq_in * (GLA_DK ** -0.5) * eb, nc))
        dk_ref[...] = _b(_heads_last(dk_out * enb + dk_st * ebl, nc))
        dla = _bmm(jnp.broadcast_to(triu, (nbat, CHUNK, CHUNK)), db, HI) + db_last
        dpre_ref[...] = _heads_last(dla * (1.0 / GLA_TAU) * _sigmoid(-pre), nc)

    r = lambda j: nb - 1 - j
    hk, hv = GLA_H * GLA_DK, GLA_H * GLA_DV
    return pl.pallas_call(
        _carry(carry, body, 8, 4, (nb,)), name=name, grid=(nb,),
        in_specs=_gla_specs(blk, r) + [pl.BlockSpec((GLA_H, nc, GLA_DV, GLA_DK), lambda j: (0, r(j), 0, 0)),
                                      pl.BlockSpec((blk, hv), lambda j: (r(j), 0))] + ex_in,
        out_specs=[pl.BlockSpec((blk, hk), lambda j: (r(j), 0)), pl.BlockSpec((blk, hk), lambda j: (r(j), 0)),
                   pl.BlockSpec((blk, hv), lambda j: (r(j), 0)), pl.BlockSpec((blk, hk), lambda j: (r(j), 0))] + ex_out,
        out_shape=[jax.ShapeDtypeStruct((T, hk), BF16), jax.ShapeDtypeStruct((T, hk), BF16),
                   jax.ShapeDtypeStruct((T, hv), BF16), jax.ShapeDtypeStruct((T, hk), F32)] + ex_shape,
        scratch_shapes=[pltpu.VMEM((GLA_H, GLA_DV, GLA_DK), F32)] + ex_sems,
        compiler_params=_cp("arbitrary"),
    )(pbig, pbig, pbig, psmall, wgate, bgate, states, do, *ex_args)


def _gla_gate_bwd(dpre, psmall, wgate, dsm, name):
    T = dpre.shape[0]
    tm = min(512, T)
    W = GLA_H * GLA_DK
    ngrp = dsm.shape[0]

    def body(dp_ref, sm_ref, wg_ref, dsm_ref, ds_ref, dw_ref, db_ref):
        @pl.when(pl.program_id(0) == 0)
        def _():
            dw_ref[...] = jnp.zeros_like(dw_ref)
            db_ref[...] = jnp.zeros_like(db_ref)

        dp = dp_ref[...]
        ds = _dot_nt(dp, wg_ref[...], HI)
        for i in range(ngrp):
            ds += dsm_ref[i]
        ds_ref[...] = ds
        dw_ref[...] += _dot_tn(sm_ref[...], dp, HI)
        db_ref[...] += jnp.sum(dp, axis=0, keepdims=True)

    return pl.pallas_call(
        body, name=name, grid=(T // tm,),
        in_specs=[pl.BlockSpec((tm, W), lambda i: (i, 0)), pl.BlockSpec((tm, SMALL), lambda i: (i, 0)),
                  pl.BlockSpec((SMALL, W), lambda i: (0, 0)), pl.BlockSpec((ngrp, tm, SMALL), lambda i: (0, i, 0))],
        out_specs=[pl.BlockSpec((tm, SMALL), lambda i: (i, 0)), pl.BlockSpec((SMALL, W), lambda i: (0, 0)),
                   pl.BlockSpec((1, W), lambda i: (0, 0))],
        out_shape=[jax.ShapeDtypeStruct((T, SMALL), F32), jax.ShapeDtypeStruct((SMALL, W), F32),
                   jax.ShapeDtypeStruct((1, W), F32)],
        compiler_params=_cp("arbitrary"),
    )(dpre, psmall, wgate, dsm)


CONV_C = 3 * 1024
CONV_BLK = 256


def _conv_fwd(pbig, cw8, name):
    T = pbig.shape[0]
    blk = min(CONV_BLK, T)

    def body(x_ref, w_ref, c_ref, prev_ref):
        @pl.when(pl.program_id(0) == 0)
        def _():
            prev_ref[...] = jnp.zeros_like(prev_ref)

        x = x_ref[...].astype(F32)
        prev = prev_ref[...]
        row8 = _iota2(8, CONV_C, 0)
        acc = x * w_ref[CONV_K - 1:CONV_K, :]
        for s in range(1, CONV_K):
            xs = pltpu.roll(x, s, 0)
            top = jnp.where(row8 < s, pltpu.roll(prev, s, 0), xs[:8])
            xs = jnp.concatenate([top, xs[8:]], axis=0)
            acc += xs * w_ref[CONV_K - 1 - s:CONV_K - s, :]
        c_ref[...] = _b(acc)
        prev_ref[...] = x[blk - 8:]

    return pl.pallas_call(
        body, name=name, grid=(T // blk,),
        in_specs=[pl.BlockSpec((blk, CONV_C), lambda i: (i, 1)), pl.BlockSpec((8, CONV_C), lambda i: (0, 0))],
        out_specs=pl.BlockSpec((blk, CONV_C), lambda i: (i, 0)),
        out_shape=jax.ShapeDtypeStruct((T, CONV_C), BF16),
        scratch_shapes=[pltpu.VMEM((8, CONV_C), F32)],
        compiler_params=_cp("arbitrary"),
    )(pbig, cw8)


def _conv_bwd(dcq, dck, dcv, pbig, cw8, name):
    T = pbig.shape[0]
    blk = min(CONV_BLK, T)
    nb = T // blk

    def body(dq_ref, dk_ref, dv_ref, x_ref, w_ref, dx_ref, dw_ref, nxt_ref):
        @pl.when(pl.program_id(0) == 0)
        def _():
            nxt_ref[...] = jnp.zeros_like(nxt_ref)
            dw_ref[...] = jnp.zeros_like(dw_ref)

        dc = jnp.concatenate([dq_ref[...], dk_ref[...], dv_ref[...]], axis=1).astype(F32)
        x = x_ref[...].astype(F32)
        nxt = nxt_ref[...]
        row8 = _iota2(8, CONV_C, 0)
        acc = dc * w_ref[CONV_K - 1:CONV_K, :]
        dws = [jnp.sum(dc * x, axis=0, keepdims=True)]
        for s in range(1, CONV_K):
            ds = pltpu.roll(dc, blk - s, 0)
            bot = jnp.where(row8 >= 8 - s, pltpu.roll(nxt, 8 - s, 0), ds[blk - 8:])
            ds = jnp.concatenate([ds[:blk - 8], bot], axis=0)
            acc += ds * w_ref[CONV_K - 1 - s:CONV_K - s, :]
            dws.append(jnp.sum(ds * x, axis=0, keepdims=True))
        dx_ref[...] = _b(acc)
        dw_ref[...] += jnp.concatenate(dws[::-1] + [jnp.zeros((8 - CONV_K, CONV_C), F32)], axis=0)
        nxt_ref[...] = dc[:8]

    part = pl.BlockSpec((blk, 1024), lambda i: (nb - 1 - i, 0))
    return pl.pallas_call(
        body, name=name, grid=(nb,),
        in_specs=[part, part, part, pl.BlockSpec((blk, CONV_C), lambda i: (nb - 1 - i, 1)),
                  pl.BlockSpec((8, CONV_C), lambda i: (0, 0))],
        out_specs=[pl.BlockSpec((blk, CONV_C), lambda i: (nb - 1 - i, 0)), pl.BlockSpec((8, CONV_C), lambda i: (0, 0))],
        out_shape=[jax.ShapeDtypeStruct((T, CONV_C), BF16), jax.ShapeDtypeStruct((8, CONV_C), F32)],
        scratch_shapes=[pltpu.VMEM((8, CONV_C), F32)],
        compiler_params=_cp("arbitrary"),
    )(dcq, dck, dcv, pbig, cw8)


def _col(x, lane):
    if isinstance(lane, int):
        return jnp.broadcast_to(x[:, lane:lane + 1], x.shape)
    sel = _iota2(x.shape[0], x.shape[1], 1) == lane
    return jnp.broadcast_to(jnp.sum(jnp.where(sel, x, 0.0), axis=1, keepdims=True), x.shape)


def _bmm(a, b, prec=None):
    return jnp.einsum("bij,bjk->bik", a, b, preferred_element_type=F32, precision=prec)


def _bmm_nt(a, b, prec=None):
    return jnp.einsum("bij,bkj->bik", a, b, preferred_element_type=F32, precision=prec)


def _bmm_tn(a, b, prec=None):
    return jnp.einsum("bji,bjk->bik", a, b, preferred_element_type=F32, precision=prec)


def _unit_lower_inverse(low):
    eye = (_iota2(CHUNK, CHUNK, 0) == _iota2(CHUNK, CHUNK, 1)).astype(F32)
    xk = -low
    inv = eye + xk
    for _ in range(5):
        xb = _b(xk)
        xk = _bmm(xb, xb)
        inv = inv + _bmm(_b(inv), _b(xk))
    resid = eye - _bmm(eye + low, inv, HI)
    return inv + _bmm(inv, resid, HI)


def _heads_first(x, nc, w=128):
    hb = x.shape[1] // w
    return jnp.concatenate([x[:, i * w:(i + 1) * w].reshape(nc, CHUNK, w) for i in range(hb)], axis=0)


def _heads_last(x, nc):
    hb = x.shape[0] // nc
    return jnp.concatenate([x[i * nc:(i + 1) * nc].reshape(nc * CHUNK, x.shape[2]) for i in range(hb)], axis=1)


def _gdn_block(cq_ref, ck_ref, cv_ref, sm_ref, par_ref, h0, hb, nc, masks, solved=None):
    causal, strict, tril, eye = masks
    nbat = hb * nc
    cq = _heads_first(cq_ref[...].astype(F32), nc)
    ck = _heads_first(ck_ref[...].astype(F32), nc)
    cv = _heads_first(cv_ref[...].astype(F32), nc)
    sq, sk, sv = _sigmoid(cq), _sigmoid(ck), _sigmoid(cv)
    q, k, v = cq * sq, ck * sk, cv * sv
    rq = lax.rsqrt(jnp.sum(q * q, axis=-1, keepdims=True) + EPS)
    rk = lax.rsqrt(jnp.sum(k * k, axis=-1, keepdims=True) + EPS)
    qh, kn = q * rq, k * rk
    qn = qh * (DN_DK ** -0.5)
    sm = sm_ref[...]
    per_head = lambda fn: jnp.concatenate([fn(i) for i in range(hb)], axis=0)
    braw = per_head(lambda i: _col(sm, GLA_RANK + h0 + i).reshape(nc, CHUNK, 128))
    araw = per_head(lambda i: _col(sm, GLA_RANK + DN_H + h0 + i).reshape(nc, CHUNK, 128))
    ea = per_head(lambda i: jnp.broadcast_to(jnp.exp(par_ref[i, 0:1, :])[None], (nc, 1, 128)))
    bias = per_head(lambda i: jnp.broadcast_to(par_ref[i, 1:2, :][None], (nc, 1, 128)))
    beta = _sigmoid(braw)
    sp_arg = araw + bias
    g = -ea * _softplus(sp_arg)
    G = _bmm(jnp.broadcast_to(tril, (nbat, CHUNK, CHUNK)), g, HI)
    gc = G[:, :, :CHUNK]
    grow = jnp.sum(eye * gc, axis=1, keepdims=True)
    decay = jnp.exp(jnp.where(causal, gc - grow, -1e30))
    kb = kn * beta
    A = _bmm_nt(_b(kb), _b(kn))
    eG = jnp.exp(G)
    gl = G[:, CHUNK - 1:CHUNK, :]
    eGl = jnp.exp(gl - G)
    g_ch = jnp.exp(gl)
    rv = v * beta
    rkk = kb * eG
    if solved is None:
        tinv_b = _b(_unit_lower_inverse(jnp.where(strict, A * decay, 0.0)))
        u = _bmm(tinv_b, _b(rv))
        w = _b(_bmm(tinv_b, _b(rkk)))
    else:
        tinv_b, u, w = solved
    B = _bmm_nt(_b(qn), _b(kn))
    qk = jnp.where(causal, B * decay, 0.0)
    q_dec = qn * eG
    k_st = kn * eGl
    return dict(cq=cq, ck=ck, cv=cv, sq=sq, sk=sk, sv=sv, q=q, k=k, v=v, rq=rq, rk=rk, qh=qh, kn=kn, qn=qn,
                beta=beta, ea=ea, sp_arg=sp_arg, g=g, G=G, decay=decay, kb=kb, A=A, tinv_b=tinv_b, eG=eG, eGl=eGl,
                g_ch=g_ch, rv=rv, rkk=rkk, u=u, w=w, B=B, qk=qk, q_dec=q_dec, k_st=k_st)


def _gdn_masks():
    r, c = _iota2(CHUNK, CHUNK, 0), _iota2(CHUNK, CHUNK, 1)
    return r >= c, r > c, (r >= c).astype(F32), (r == c).astype(F32)


def _gdn_specs(blk, hb, idx):
    ng = DN_H // hb
    return [pl.BlockSpec((blk, hb * DN_DK), lambda h, j: (idx(j), h)),
            pl.BlockSpec((blk, hb * DN_DK), lambda h, j: (idx(j), ng + h)),
            pl.BlockSpec((blk, hb * DN_DV), lambda h, j: (idx(j), 2 * ng + h)),
            pl.BlockSpec((blk, SMALL), lambda h, j: (idx(j), 0)),
            pl.BlockSpec((hb, 8, 128), lambda h, j: (h, 0, 0))]


def _gdn_solved_specs(blk, hb, idx):
    nc = blk // CHUNK
    spec = lambda w: pl.BlockSpec((hb, nc, CHUNK, w), lambda h, j: (h, idx(j), 0, 0))
    return [spec(CHUNK), spec(DN_DV), spec(DN_DK)]


def _gdn_fwd(conv, psmall, par, name):
    T = conv.shape[0]
    blk = min(GDN_BLK, T)
    nc = blk // CHUNK
    hb = GDN_HEADS
    N = T // CHUNK

    def body(cq_ref, ck_ref, cv_ref, sm_ref, par_ref, o_ref, ss_ref, ti_ref, u_ref, w_ref, s_ref):
        @pl.when(pl.program_id(1) == 0)
        def _():
            s_ref[...] = jnp.zeros_like(s_ref)

        h0 = 0 if hb == DN_H else pl.program_id(0) * hb
        f = _gdn_block(cq_ref, ck_ref, cv_ref, sm_ref, par_ref, h0, hb, nc, _gdn_masks())
        ti_ref[...] = f["tinv_b"].reshape(hb, nc, CHUNK, CHUNK)
        u_ref[...] = f["u"].reshape(hb, nc, CHUNK, DN_DV)
        w_ref[...] = f["w"].reshape(hb, nc, CHUNK, DN_DK)
        wb, ub, kstb, qkb = f["w"], _b(f["u"]), _b(f["k_st"]), _b(f["qk"])
        mix = _b(_bmm_tn(kstb, wb))
        add = _bmm_tn(kstb, ub)
        q_eff = _b(f["q_dec"] - _bmm(qkb, wb))
        before = [None] * (hb * nc)
        S = [s_ref[i] for i in range(hb)]
        for c in range(nc):
            for i in range(hb):
                n = i * nc + c
                before[n] = S[i]
                S[i] = S[i] * f["g_ch"][n] - _dot(mix[n], _b(S[i])) + add[n]
        for i in range(hb):
            s_ref[i] = S[i]
        states = jnp.stack(before)
        ss_ref[...] = states.reshape(hb, nc, DN_DK, DN_DV)
        o_ref[...] = _b(_heads_last(_bmm(qkb, ub) + _bmm(q_eff, _b(states)), nc))

    return pl.pallas_call(
        body, name=name, grid=(DN_H // hb, T // blk),
        in_specs=_gdn_specs(blk, hb, lambda j: j),
        out_specs=[pl.BlockSpec((blk, hb * DN_DV), lambda h, j: (j, h)),
                   pl.BlockSpec((hb, nc, DN_DK, DN_DV), lambda h, j: (h, j, 0, 0))]
        + _gdn_solved_specs(blk, hb, lambda j: j),
        out_shape=[jax.ShapeDtypeStruct((T, DN_H * DN_DV), BF16), jax.ShapeDtypeStruct((DN_H, N, DN_DK, DN_DV), F32),
                   jax.ShapeDtypeStruct((DN_H, N, CHUNK, CHUNK), BF16), jax.ShapeDtypeStruct((DN_H, N, CHUNK, DN_DV), F32),
                   jax.ShapeDtypeStruct((DN_H, N, CHUNK, DN_DK), BF16)],
        scratch_shapes=[pltpu.VMEM((hb, DN_DK, DN_DV), F32)],
        compiler_params=_cp("parallel", "arbitrary"),
    )(conv, conv, conv, psmall, par)


def _gdn_bwd(conv, psmall, par, states, do, solved, name, carry=None):
    ex_in, ex_args, ex_out, ex_shape, ex_sems = _carry_specs(carry)
    T = conv.shape[0]
    blk = min(GDN_BLK, T)
    nc = blk // CHUNK
    nb = T // blk
    hb = GDN_HEADS
    nbat = hb * nc
    rsum = lambda x: jnp.sum(x, axis=-1, keepdims=True)

    def body(cq_ref, ck_ref, cv_ref, sm_ref, par_ref, ss_ref, do_ref, ti_ref, u_ref, w_ref,
             dcq_ref, dck_ref, dcv_ref, dsm_ref, dpar_ref, ds_ref):
        @pl.when(pl.program_id(1) == 0)
        def _():
            ds_ref[...] = jnp.zeros_like(ds_ref)
            dpar_ref[...] = jnp.zeros_like(dpar_ref)

        masks = _gdn_masks()
        causal, strict, tril, eye = masks
        triu = (_iota2(CHUNK, CHUNK, 0) <= _iota2(CHUNK, CHUNK, 1)).astype(F32)
        lane = _iota2(CHUNK, 128, 1)
        last_row = _iota2(CHUNK, 128, 0) == CHUNK - 1
        h0 = 0 if hb == DN_H else pl.program_id(0) * hb
        solved = (ti_ref[...].reshape(nbat, CHUNK, CHUNK), u_ref[...].reshape(nbat, CHUNK, DN_DV),
                  w_ref[...].reshape(nbat, CHUNK, DN_DK))
        f = _gdn_block(cq_ref, ck_ref, cv_ref, sm_ref, par_ref, h0, hb, nc, masks, solved)
        S = ss_ref[...].reshape(nbat, DN_DK, DN_DV)
        Sb = _b(S)
        do_ = _b(_heads_first(do_ref[...], nc))
        wb, qdb, kstb, qkb = _b(f["w"]), _b(f["q_dec"]), _b(f["k_st"]), _b(f["qk"])
        vnb = _b(f["u"] - _bmm(wb, Sb))
        dvn0 = _bmm_tn(qkb, do_)
        qdo = _bmm_tn(qdb, do_)
        dS = [ds_ref[i] for i in range(hb)]
        after = [None] * nbat
        for c in range(nc - 1, -1, -1):
            for i in range(hb):
                n = i * nc + c
                after[n] = dS[i]
                dvn_c = _b(dvn0[n] + _dot(kstb[n], _b(dS[i])))
                dS[i] = dS[i] * f["g_ch"][n] + qdo[n] - _dot_tn(wb[n], dvn_c)
        for i in range(hb):
            ds_ref[i] = dS[i]
        dSa = jnp.stack(after)
        dSb = _b(dSa)
        dvn = dvn0 + _bmm(kstb, dSb)
        dvnb = _b(dvn)
        dq_dec = _bmm_nt(do_, Sb)
        dqk = jnp.where(causal, _bmm_nt(do_, vnb), 0.0)
        dk_st = _bmm_nt(vnb, dSb)
        dg_ch = jnp.sum(rsum(S * dSa), axis=1, keepdims=True)
        dw = -_bmm_nt(dvnb, Sb)
        drv = _bmm_tn(f["tinv_b"], dvnb)
        drk = _bmm_tn(f["tinv_b"], _b(dw))
        dlow = jnp.where(strict, -(_bmm_nt(_b(drv), _b(f["u"])) + _bmm_nt(_b(drk), wb)), 0.0)
        dv = drv * f["beta"]
        dbeta = rsum(drv * f["v"])
        dkb = drk * f["eG"]
        dG = rsum(drk * f["rkk"])
        dA = dlow * f["decay"]
        ddec = dlow * f["A"]
        dkb += _bmm(_b(dA), _b(f["kn"]))
        dkn = _bmm_tn(_b(dA), _b(f["kb"]))
        dB = dqk * f["decay"]
        ddec += dqk * f["B"]
        dqn = _bmm(_b(dB), _b(f["kn"]))
        dkn += _bmm_tn(_b(dB), _b(f["qn"]))
        dD = ddec * f["decay"]
        dG += rsum(dD) - rsum(eye * jnp.sum(dD, axis=1, keepdims=True))
        dqn += dq_dec * f["eG"]
        dG += rsum(dq_dec * f["q_dec"])
        dkn += dk_st * f["eGl"]
        tks = rsum(dk_st * f["k_st"])
        dG -= tks
        dG_last = jnp.sum(tks, axis=1, keepdims=True) + dg_ch * f["g_ch"][:, :, :1]
        dkn += dkb * f["beta"]
        dbeta += rsum(dkb * f["kn"])
        dGf = jnp.broadcast_to(dG, (nbat, CHUNK, 128)) + jnp.where(last_row, dG_last, 0.0)
        dg = _bmm(jnp.broadcast_to(triu, (nbat, CHUNK, CHUNK)), dGf, HI)
        dbraw = dbeta * f["beta"][:, :, :1] * (1.0 - f["beta"][:, :, :1])
        daraw = dg * (-f["ea"]) * _sigmoid(f["sp_arg"])
        both = lambda t: jnp.sum(jnp.sum(t, axis=1, keepdims=True), axis=0)
        dgg = dg * f["g"]
        dsm = jnp.zeros((nc, CHUNK, SMALL), F32)
        for i in range(hb):
            mine = slice(i * nc, (i + 1) * nc)
            dsm += (jnp.where(lane == GLA_RANK + h0 + i, dbraw[mine], 0.0)
                    + jnp.where(lane == GLA_RANK + DN_H + h0 + i, daraw[mine], 0.0))
            dpar = jnp.where(lane[:1] == 0, both(dgg[mine]), jnp.where(lane[:1] == 1, both(daraw[mine]), 0.0))
            dpar_ref[i] += jnp.broadcast_to(dpar, (8, 128))
        dsm_ref[0] = dsm.reshape(blk, SMALL)
        dqh = dqn * (DN_DK ** -0.5)
        dq = f["rq"] * (dqh - f["qh"] * rsum(dqh * f["qh"]))
        dk = f["rk"] * (dkn - f["kn"] * rsum(dkn * f["kn"]))
        dsilu = lambda x, s: s * (1.0 + x * (1.0 - s))
        dcq_ref[...] = _b(_heads_last(dq * dsilu(f["cq"], f["sq"]), nc))
        dck_ref[...] = _b(_heads_last(dk * dsilu(f["ck"], f["sk"]), nc))
        dcv_ref[...] = _b(_heads_last(dv * dsilu(f["cv"], f["sv"]), nc))

    r = lambda j: nb - 1 - j
    out_blk = pl.BlockSpec((blk, hb * DN_DK), lambda h, j: (r(j), h))
    grid = (DN_H // hb, nb)
    return pl.pallas_call(
        _carry(carry, body, 10, 5, grid), name=name, grid=grid,
        in_specs=_gdn_specs(blk, hb, r) + [pl.BlockSpec((hb, nc, DN_DK, DN_DV), lambda h, j: (h, r(j), 0, 0)),
                                          pl.BlockSpec((blk, hb * DN_DV), lambda h, j: (r(j), h))]
        + _gdn_solved_specs(blk, hb, r) + ex_in,
        out_specs=[out_blk, out_blk, out_blk, pl.BlockSpec((1, blk, SMALL), lambda h, j: (h, r(j), 0)),
                   pl.BlockSpec((hb, 8, 128), lambda h, j: (h, 0, 0))] + ex_out,
        out_shape=[jax.ShapeDtypeStruct((T, DN_H * DN_DK), BF16)] * 3 + [
            jax.ShapeDtypeStruct((DN_H // hb, T, SMALL), F32), jax.ShapeDtypeStruct((DN_H, 8, 128), F32)] + ex_shape,
        scratch_shapes=[pltpu.VMEM((hb, DN_DK, DN_DV), F32)] + ex_sems,
        compiler_params=_cp("arbitrary", "arbitrary"),
    )(conv, conv, conv, psmall, par, states, do, *solved, *ex_args)


def _head_norm(o, w, dv):
    outs, rs = [], []
    for i in range(o.shape[1] // dv):
        oh = o[:, i * dv:(i + 1) * dv]
        r = lax.rsqrt(jnp.mean(oh * oh, axis=-1, keepdims=True) + EPS)
        outs.append(oh * r)
        rs.append(r)
    return outs, rs


def _merge_specs(tm):
    col = lambda c: pl.BlockSpec((tm, D), lambda i: (i, c))
    return [col(0), col(0), col(2), col(6), col(7), col(8),
            pl.BlockSpec((1, GLA_DV), lambda i: (0, 0)), pl.BlockSpec((1, DN_DV), lambda i: (0, 0)),
            pl.BlockSpec((D, D), lambda i: (0, 0))]


def _merge_fwd(h, oa, ob, pbig, gla_hn, dn_hn, wout, name):
    T = h.shape[0]
    tm = min(ROW_BLK, T)

    def body(h_ref, oa_ref, ob_ref, gr_ref, dg_ref, ma_ref, mb_ref, wa_ref, wb_ref, wo_ref, ho_ref, y_ref):
        na, _ = _head_norm(oa_ref[...].astype(F32), wa_ref[...], GLA_DV)
        nbs, _ = _head_norm(ob_ref[...].astype(F32), wb_ref[...], DN_DV)
        hna = jnp.concatenate([t * wa_ref[...] for t in na], axis=1)
        hnb = jnp.concatenate([t * wb_ref[...] for t in nbs], axis=1)
        gr = gr_ref[...].astype(F32)
        dg = dg_ref[...].astype(F32)
        y = (_sigmoid(ma_ref[...].astype(F32)) * hna * (gr * _sigmoid(gr))
             + _sigmoid(mb_ref[...].astype(F32)) * hnb * (dg * _sigmoid(dg)))
        yb = _b(y)
        y_ref[...] = yb
        ho_ref[...] = h_ref[...] + _dot(yb, wo_ref[...])

    row = pl.BlockSpec((tm, D), lambda i: (i, 0))
    return pl.pallas_call(
        body, name=name, grid=(T // tm,),
        in_specs=[row] + _merge_specs(tm),
        out_specs=[row, row],
        out_shape=[jax.ShapeDtypeStruct((T, D), F32), jax.ShapeDtypeStruct((T, D), BF16)],
        compiler_params=_cp("arbitrary"),
    )(h, oa, ob, pbig, pbig, pbig, pbig, gla_hn, dn_hn, wout)


def _merge_bwd(dh, oa, ob, pbig, gla_hn, dn_hn, wout, name, carry=None):
    T = dh.shape[0]
    tm = min(ROW_BLK, T)
    ex_in, ex_args, ex_out, ex_shape, ex_sems = _carry_specs(carry)

    def branch(dy, o_ref, w_ref, gate_ref, m_ref, dv):
        w = w_ref[...]
        ohat, rs = _head_norm(o_ref[...].astype(F32), w, dv)
        gate = gate_ref[...].astype(F32)
        m = m_ref[...].astype(F32)
        sgate, sm = _sigmoid(gate), _sigmoid(m)
        silu = gate * sgate
        ohat_all = jnp.concatenate(ohat, axis=1)
        hn = jnp.concatenate([t * w for t in ohat], axis=1)
        d_on = dy * sm
        d_m = dy * hn * silu * sm * (1.0 - sm)
        d_hn = d_on * silu
        d_gate = d_on * hn * (sgate * (1.0 + gate * (1.0 - sgate)))
        dw = jnp.zeros((1, dv), F32)
        d_o = []
        for i, (oh, r) in enumerate(zip(ohat, rs)):
            dhn = d_hn[:, i * dv:(i + 1) * dv]
            dw += jnp.sum(dhn * oh, axis=0, keepdims=True)
            dohat = dhn * w
            d_o.append(r * (dohat - oh * jnp.mean(dohat * oh, axis=-1, keepdims=True)))
        return jnp.concatenate(d_o, axis=1), d_gate, d_m, dw

    def body(dh_ref, oa_ref, ob_ref, gr_ref, dg_ref, ma_ref, mb_ref, wa_ref, wb_ref, wo_ref,
             doa_ref, dob_ref, dgr_ref, ddg_ref, dma_ref, dmb_ref, dwa_ref, dwb_ref, dhb_ref):
        @pl.when(pl.program_id(0) == 0)
        def _():
            dwa_ref[...] = jnp.zeros_like(dwa_ref)
            dwb_ref[...] = jnp.zeros_like(dwb_ref)

        dhb = _b(dh_ref[...])
        dhb_ref[...] = dhb
        dy = _dot_nt(dhb, wo_ref[...])
        d_oa, d_gr, d_ma, dwa = branch(dy, oa_ref, wa_ref, gr_ref, ma_ref, GLA_DV)
        d_ob, d_dg, d_mb, dwb = branch(dy, ob_ref, wb_ref, dg_ref, mb_ref, DN_DV)
        doa_ref[...] = _b(d_oa)
        dob_ref[...] = _b(d_ob)
        dgr_ref[...] = _b(d_gr)
        ddg_ref[...] = _b(d_dg)
        dma_ref[...] = _b(d_ma)
        dmb_ref[...] = _b(d_mb)
        dwa_ref[...] += dwa
        dwb_ref[...] += dwb

    row = pl.BlockSpec((tm, D), lambda i: (i, 0))
    b16 = jax.ShapeDtypeStruct((T, D), BF16)
    return pl.pallas_call(
        _carry(carry, body, 10, 9, (T // tm,)), name=name, grid=(T // tm,),
        in_specs=[row] + _merge_specs(tm) + ex_in,
        out_specs=[row] * 6 + [pl.BlockSpec((1, GLA_DV), lambda i: (0, 0)), pl.BlockSpec((1, DN_DV), lambda i: (0, 0)), row]
        + ex_out,
        out_shape=[b16, b16, b16, b16, b16, b16, jax.ShapeDtypeStruct((1, GLA_DV), F32),
                   jax.ShapeDtypeStruct((1, DN_DV), F32), b16] + ex_shape,
        scratch_shapes=ex_sems,
        compiler_params=_cp("arbitrary"),
    )(dh, oa, ob, pbig, pbig, pbig, pbig, gla_hn, dn_hn, wout, *ex_args)


def _adamw(w, g, m, v, rows, name):
    R, C = w.shape
    rows = min(rows, R)
    c1 = 1.0 - ADAM_B1 ** ADAM_STEP
    c2 = 1.0 - ADAM_B2 ** ADAM_STEP

    def body(w_ref, g_ref, m_ref, v_ref, d_ref, mo_ref, vo_ref):
        g_ = g_ref[...]
        m_ = ADAM_B1 * m_ref[...] + (1.0 - ADAM_B1) * g_
        v_ = ADAM_B2 * v_ref[...] + (1.0 - ADAM_B2) * (g_ * g_)
        mo_ref[...] = m_
        vo_ref[...] = v_
        d_ref[...] = -ADAM_LR * ((m_ / c1) / (jnp.sqrt(v_ / c2) + ADAM_EPS) + ADAM_WD * w_ref[...])

    blk = pl.BlockSpec((rows, C), lambda i: (i, 0))
    shp = jax.ShapeDtypeStruct((R, C), F32)
    return pl.pallas_call(
        body, name=name, grid=(R // rows,),
        in_specs=[blk] * 4, out_specs=[blk] * 3, out_shape=[shp] * 3,
        compiler_params=_cp("parallel"),
    )(w, g, m, v)


def _me():
    return lax.axis_index("x"), lax.axis_index("y"), lax.axis_index("c")


def _other_chips(x, y):
    return [(1 - x, y), (x, 1 - y), (1 - x, 1 - y)]


def _half_rows(ref, hf):
    half = ref.shape[-2] // 2
    rows = pl.ds(pl.multiple_of(hf * half, 16), half)
    return ref.at[rows, :] if len(ref.shape) == 2 else ref.at[:, rows, :]


class _GatherBig:
    def __init__(self, big):
        self.arrays = list(big)
        self.out_shape = [jax.ShapeDtypeStruct((N_SHARD,) + w.shape, w.dtype) for w in big]
        self.n_sem = 7 * len(big)

    @staticmethod
    def _copy(sems, k, src, dst, to):
        return pltpu.make_async_remote_copy(src_ref=src, dst_ref=dst, send_sem=sems[0].at[k], recv_sem=sems[1].at[k],
                                            device_id=to, device_id_type=MESH)

    def start(self, ins, outs, *sems):
        x, y, c = _me()
        mine = 2 * x + y
        for i, (w_ref, o_ref) in enumerate(zip(ins, outs)):
            self._copy(sems, 7 * i + 6, w_ref, o_ref.at[mine], (x, y, 1 - c)).start()
            for j, chip in enumerate(_other_chips(x, y)):
                self._copy(sems, 7 * i + j, _half_rows(w_ref, c), _half_rows(o_ref.at[mine], c), (*chip, c)).start()

    def relay(self, ins, outs, *sems):
        x, y, c = _me()
        for i, o_ref in enumerate(outs):
            for j, chip in enumerate(_other_chips(x, y)):
                landed = _half_rows(o_ref.at[2 * chip[0] + chip[1]], c)
                self._copy(sems, 7 * i + j, landed, landed, (x, y, c)).wait_recv()
                self._copy(sems, 7 * i + 3 + j, landed, landed, (x, y, 1 - c)).start()

    def finish(self, ins, outs, *sems):
        x, y, c = _me()
        me, sibling = (x, y, c), (x, y, 1 - c)
        chips = _other_chips(x, y)
        slot = lambda chip: 2 * chip[0] + chip[1]
        for i, (w_ref, o_ref) in enumerate(zip(ins, outs)):
            for j, chip in enumerate(chips):
                passed = _half_rows(o_ref.at[slot(chip)], 1 - c)
                self._copy(sems, 7 * i + 3 + j, passed, passed, me).wait_recv()
            self._copy(sems, 7 * i + 6, o_ref.at[slot((x, y))], o_ref.at[slot((x, y))], me).wait_recv()
        for i, (w_ref, o_ref) in enumerate(zip(ins, outs)):
            self._copy(sems, 7 * i + 6, w_ref, o_ref.at[slot((x, y))], sibling).wait_send()
            for j, chip in enumerate(chips):
                self._copy(sems, 7 * i + j, _half_rows(w_ref, c), _half_rows(o_ref.at[slot((x, y))], c),
                           (*chip, c)).wait_send()
                landed = _half_rows(o_ref.at[slot(chip)], c)
                self._copy(sems, 7 * i + 3 + j, landed, landed, sibling).wait_send()


class _SiblingExchange:
    def __init__(self, gs):
        self.arrays = list(gs)
        self.out_shape = [jax.ShapeDtypeStruct((g.shape[0], g.shape[1] // 2, g.shape[2]), g.dtype) for g in gs]
        self.n_sem = len(gs)

    def _copies(self, ins, outs, send_sems, recv_sems):
        x, y, c = _me()
        return [pltpu.make_async_remote_copy(src_ref=_half_rows(ins[i], 1 - c), dst_ref=outs[i],
                                             send_sem=send_sems.at[i], recv_sem=recv_sems.at[i],
                                             device_id=(x, y, 1 - c), device_id_type=MESH) for i in range(len(ins))]

    def start(self, ins, outs, *sems):
        for cp in self._copies(ins, outs, *sems):
            cp.start()

    def finish(self, ins, outs, *sems):
        for cp in self._copies(ins, outs, *sems):
            cp.wait()


class _ChipsExchange:
    def __init__(self, pbs):
        self.arrays = list(pbs)
        self.out_shape = [jax.ShapeDtypeStruct((3,) + p.shape[1:], p.dtype) for p in pbs]
        self.n_sem = 3 * len(pbs)

    def _copies(self, ins, outs, send_sems, recv_sems):
        x, y, c = _me()
        return [pltpu.make_async_remote_copy(src_ref=ins[i].at[2 * chip[0] + chip[1]], dst_ref=outs[i].at[j],
                                             send_sem=send_sems.at[3 * i + j], recv_sem=recv_sems.at[3 * i + j],
                                             device_id=(*chip, c), device_id_type=MESH)
                for i in range(len(ins)) for j, chip in enumerate(_other_chips(x, y))]

    def start(self, ins, outs, *sems):
        for cp in self._copies(ins, outs, *sems):
            cp.start()

    def finish(self, ins, outs, *sems):
        for cp in self._copies(ins, outs, *sems):
            cp.wait()


def _carry(ex, body, n_in, n_out, grid):
    if ex is None:
        return body
    ni, no = len(ex.arrays), len(ex.out_shape)

    def carried(*refs):
        ins, ex_in = refs[:n_in], refs[n_in:n_in + ni]
        outs, ex_out = refs[n_in + ni:n_in + ni + n_out], refs[n_in + ni + n_out:n_in + ni + n_out + no]
        scratch, sems = refs[n_in + ni + n_out + no:-2], refs[-2:]
        step = functools.reduce(lambda acc, a: acc * grid[a] + pl.program_id(a), range(len(grid)), 0)
        steps = math.prod(grid)

        @pl.when(step == 0)
        def _():
            ex.start(ex_in, ex_out, *sems)

        body(*ins, *outs, *scratch)

        if hasattr(ex, "relay"):
            @pl.when(step == (3 * steps) // 4)
            def _():
                ex.relay(ex_in, ex_out, *sems)

        @pl.when(step == steps - 1)
        def _():
            ex.finish(ex_in, ex_out, *sems)

    return carried


def _carry_specs(ex):
    if ex is None:
        return [], [], [], [], []
    sems = [pltpu.SemaphoreType.DMA((ex.n_sem,)), pltpu.SemaphoreType.DMA((ex.n_sem,))]
    return [ANY] * len(ex.arrays), ex.arrays, [ANY] * len(ex.out_shape), ex.out_shape, sems


def _gather_weights(big, small, name):
    nbig, nsm = len(big), len(small)
    n = nbig + nsm
    own_sem = 6 * nbig + 3 * nsm

    def body(*refs):
        ins, outs = refs[:n], refs[n:2 * n]
        send_sems, recv_sems = refs[2 * n:]
        x, y, c = _me()
        sibling = (x, y, 1 - c)
        chips = _other_chips(x, y)
        slot = lambda chip: 2 * chip[0] + chip[1]

        def copy(k, src, dst, to):
            return pltpu.make_async_remote_copy(src_ref=src, dst_ref=dst, send_sem=send_sems.at[k],
                                                recv_sem=recv_sems.at[k], device_id=to, device_id_type=MESH)

        sent = []
        for i in range(nbig):
            sent.append(copy(own_sem + i, ins[i], outs[i].at[slot((x, y))], sibling))
            sent[-1].start()
            for j, chip in enumerate(chips):
                sent.append(copy(6 * i + j, _half_rows(ins[i], c), _half_rows(outs[i].at[slot((x, y))], c), (*chip, c)))
                sent[-1].start()
        for t in range(nsm):
            w_ref, o_ref = ins[nbig + t], outs[nbig + t]
            o_ref[slot((x, y))] = w_ref[...]
            for j, chip in enumerate(chips):
                sent.append(copy(6 * nbig + 3 * t + j, w_ref, o_ref.at[slot((x, y))], (*chip, c)))
                sent[-1].start()
        for i in range(nbig):
            for j, chip in enumerate(chips):
                landed = _half_rows(outs[i].at[slot(chip)], c)
                copy(6 * i + j, landed, landed, (x, y, c)).wait_recv()
                sent.append(copy(6 * i + 3 + j, landed, landed, sibling))
                sent[-1].start()
        for t in range(nsm):
            for j, chip in enumerate(chips):
                landed = outs[nbig + t].at[slot(chip)]
                copy(6 * nbig + 3 * t + j, landed, landed, (x, y, c)).wait_recv()
        for i in range(nbig):
            for j, chip in enumerate(chips):
                passed = _half_rows(outs[i].at[slot(chip)], 1 - c)
                copy(6 * i + 3 + j, passed, passed, (x, y, c)).wait_recv()
        for i in range(nbig):
            mine = outs[i].at[slot((x, y))]
            copy(own_sem + i, mine, mine, (x, y, c)).wait_recv()
        for cp in sent:
            cp.wait_send()

    vm = pl.BlockSpec(memory_space=pltpu.VMEM)
    nsem = own_sem + nbig
    return pl.pallas_call(
        body, name=name, in_specs=[ANY] * nbig + [vm] * nsm, out_specs=[ANY] * nbig + [vm] * nsm,
        out_shape=[jax.ShapeDtypeStruct((N_SHARD,) + w.shape, w.dtype) for w in list(big) + list(small)],
        scratch_shapes=[pltpu.SemaphoreType.DMA((nsem,)), pltpu.SemaphoreType.DMA((nsem,))],
        compiler_params=pltpu.CompilerParams(has_side_effects=True),
    )(*big, *small)


def _rs_sibling(gs, name):
    n = len(gs)

    def body(*refs):
        send_sems, recv_sems = refs[2 * n:]
        x, y, c = _me()
        cps = [pltpu.make_async_remote_copy(src_ref=_half_rows(refs[i], 1 - c), dst_ref=refs[n + i],
                                            send_sem=send_sems.at[i], recv_sem=recv_sems.at[i],
                                            device_id=(x, y, 1 - c), device_id_type=MESH) for i in range(n)]
        for cp in cps:
            cp.start()
        for cp in cps:
            cp.wait()

    return pl.pallas_call(
        body, name=name, in_specs=[ANY] * n, out_specs=[ANY] * n,
        out_shape=[jax.ShapeDtypeStruct((g.shape[0], g.shape[1] // 2, g.shape[2]), g.dtype) for g in gs],
        scratch_shapes=[pltpu.SemaphoreType.DMA((n,)), pltpu.SemaphoreType.DMA((n,))],
        compiler_params=pltpu.CompilerParams(has_side_effects=True),
    )(*gs)


def _add_pair(g, other, where, name):
    ns, a, b = g.shape
    half = a // 2

    def body(w_ref, g_ref, o_ref, pb_ref, own_ref):
        t = g_ref[0].astype(F32) + o_ref[0].astype(F32)
        pb_ref[0] = _b(t)

        @pl.when(pl.program_id(0) == w_ref[1])
        def _():
            own_ref[...] = t

    return pl.pallas_call(
        body, name=name,
        grid_spec=pltpu.PrefetchScalarGridSpec(
            num_scalar_prefetch=1, grid=(ns,),
            in_specs=[pl.BlockSpec((1, half, b), lambda s, w: (s, w[0], 0)), pl.BlockSpec((1, half, b), lambda s, w: (s, 0, 0))],
            out_specs=[pl.BlockSpec((1, half, b), lambda s, w: (s, 0, 0)), pl.BlockSpec((half, b), lambda s, w: (0, 0))]),
        out_shape=[jax.ShapeDtypeStruct((ns, half, b), BF16), jax.ShapeDtypeStruct((half, b), F32)],
        compiler_params=_cp("arbitrary"),
    )(where, g, other)


def _add_four(own, got, name):
    rows, cols = own.shape
    rb = rows // 2

    def body(a_ref, b_ref, o_ref):
        o_ref[...] = ((a_ref[...] + b_ref[0].astype(F32)) + b_ref[1].astype(F32)) + b_ref[2].astype(F32)

    return pl.pallas_call(
        body, name=name, grid=(rows // rb,),
        in_specs=[pl.BlockSpec((rb, cols), lambda i: (i, 0)), pl.BlockSpec((3, rb, cols), lambda i: (0, i, 0))],
        out_specs=pl.BlockSpec((rb, cols), lambda i: (i, 0)),
        out_shape=jax.ShapeDtypeStruct((rows, cols), F32),
        compiler_params=_cp("parallel"),
    )(own, got)


def _rs_swap(halves, name):
    n = len(halves)

    def body(*refs):
        send_sems, recv_sems = refs[2 * n:]
        x, y, c = _me()
        cps = [pltpu.make_async_remote_copy(src_ref=refs[i], dst_ref=refs[n + i], send_sem=send_sems.at[i],
                                            recv_sem=recv_sems.at[i], device_id=(x, y, 1 - c), device_id_type=MESH)
               for i in range(n)]
        for cp in cps:
            cp.start()
        for cp in cps:
            cp.wait()

    return pl.pallas_call(
        body, name=name, in_specs=[ANY] * n, out_specs=[ANY] * n,
        out_shape=[jax.ShapeDtypeStruct(h.shape, h.dtype) for h in halves],
        scratch_shapes=[pltpu.SemaphoreType.DMA((n,)), pltpu.SemaphoreType.DMA((n,))],
        compiler_params=pltpu.CompilerParams(has_side_effects=True),
    )(*halves)


def _adamw_halves(w, own, got, m, v, rows, name):
    a, b = w.shape
    nblk = a // 2 // rows
    c1 = 1.0 - ADAM_B1 ** ADAM_STEP
    c2 = 1.0 - ADAM_B2 ** ADAM_STEP

    def body(w_ref, own_ref, got_ref, m_ref, v_ref, g_ref, d_ref, mo_ref, vo_ref):
        g_ = jnp.where(pl.program_id(0) == lax.axis_index("c"), own_ref[...], got_ref[...])
        g_ref[...] = g_
        m_ = ADAM_B1 * m_ref[...] + (1.0 - ADAM_B1) * g_
        v_ = ADAM_B2 * v_ref[...] + (1.0 - ADAM_B2) * (g_ * g_)
        mo_ref[...] = m_
        vo_ref[...] = v_
        d_ref[...] = -ADAM_LR * ((m_ / c1) / (jnp.sqrt(v_ / c2) + ADAM_EPS) + ADAM_WD * w_ref[...])

    whole = pl.BlockSpec((rows, b), lambda h, i: (h * nblk + i, 0))
    part = pl.BlockSpec((rows, b), lambda h, i: (i, 0))
    shp = jax.ShapeDtypeStruct((a, b), F32)
    return pl.pallas_call(
        body, name=name, grid=(2, nblk),
        in_specs=[whole, part, part, whole, whole], out_specs=[whole] * 4, out_shape=[shp] * 4,
        compiler_params=_cp("parallel", "parallel"),
    )(w, own, got, m, v)


def _allsum_small(vec, name):
    def body(v_ref, o_ref, buf_ref, send_sems, recv_sems):
        x, y, c = _me()
        me = 4 * x + 2 * y + c
        buf_ref[me] = v_ref[...]
        cps = []
        for k in range(1, 8):
            peer = (x ^ (k >> 2), y ^ ((k >> 1) & 1), c ^ (k & 1))
            cps.append(pltpu.make_async_remote_copy(src_ref=v_ref, dst_ref=buf_ref.at[me],
                                                    send_sem=send_sems.at[k - 1], recv_sem=recv_sems.at[k - 1],
                                                    device_id=peer, device_id_type=MESH))
        for cp in cps:
            cp.start()
        for k in range(1, 8):
            peer_idx = me ^ k
            pltpu.make_async_remote_copy(src_ref=v_ref, dst_ref=buf_ref.at[peer_idx],
                                         send_sem=send_sems.at[k - 1], recv_sem=recv_sems.at[k - 1],
                                         device_id=(x, y, c), device_id_type=MESH).wait_recv()
        for cp in cps:
            cp.wait_send()
        acc = buf_ref[0]
        for d in range(1, 8):
            acc = acc + buf_ref[d]
        o_ref[...] = acc

    return pl.pallas_call(
        body, name=name,
        in_specs=[pl.BlockSpec(memory_space=pltpu.VMEM)], out_specs=pl.BlockSpec(memory_space=pltpu.VMEM),
        out_shape=jax.ShapeDtypeStruct(vec.shape, F32),
        scratch_shapes=[pltpu.VMEM((8,) + vec.shape, F32), pltpu.SemaphoreType.DMA((7,)), pltpu.SemaphoreType.DMA((7,))],
        compiler_params=pltpu.CompilerParams(has_side_effects=True),
    )(vec)


BIG = ("ffn1_w_gate", "ffn1_w_up", "ffn1_w_down", "w_in", "w_out", "ffn2_w_gate", "ffn2_w_up", "ffn2_w_down")
TINY = ("w_gla_gate", "conv_w")
SHARDED = BIG + TINY


def _join_cols(w4):
    return jnp.transpose(w4, (1, 0, 2)).reshape(w4.shape[1], N_SHARD * w4.shape[2])


def _cut_cols(w):
    return jnp.transpose(w.reshape(w.shape[0], N_SHARD, w.shape[1] // N_SHARD), (1, 0, 2))


def _split_w_in(w):
    o = IN_OFF
    big = jnp.concatenate([w[:, :o[4]], w[:, o[5]:o[9]], w[:, o[11]:]], axis=1)
    small = jnp.concatenate([w[:, o[4]:o[5]], w[:, o[9]:o[11]], jnp.zeros((w.shape[0], SMALL - 32), w.dtype)], axis=1)
    return big, small


def _merge_w_in(big, small):
    return jnp.concatenate([big[:, :3072], small[:, :16], big[:, 3072:7168], small[:, 16:32], big[:, 7168:]], axis=1)


class _Comm:
    def __init__(self, where, rest_shards):
        self.where = where
        self.w_in = _GatherBig(rest_shards[:1])
        self.later = _GatherBig(rest_shards[1:])
        self.pairs, self.got = {}, {}

    @staticmethod
    def w_in_weights(gathered):
        return dict(zip(("w_in_big", "w_in_small"), _split_w_in(_join_cols(gathered[0]))))

    @staticmethod
    def later_weights(gathered):
        W = dict(zip(BIG[4:], gathered))
        W["w_out"] = W["w_out"].reshape(D, D)
        return W

    def pair(self, names, grads, from_sibling):
        for n, g, o in zip(names, grads, from_sibling):
            self.pairs[n] = _add_pair(g, o, self.where, "rs_pair_" + n)
        return _ChipsExchange([self.pairs[n][0] for n in names])

    def begin(self, names, grads):
        return self.pair(names, grads, _rs_sibling(grads, "rs_sibling_" + names[0]))

    def landed(self, names, outs):
        self.got.update(zip(names, outs))


def _local_step(x, target, W, P, comm=None):
    wgate_pad = jnp.zeros((SMALL, GLA_H * GLA_DK), F32).at[:GLA_RANK].set(P["w_gla_gate"])
    cw8 = jnp.zeros((8, CONV_C), F32).at[:CONV_K].set(P["conv_w"])
    par = jnp.zeros((DN_H, 8, 128), F32)
    par = par.at[:, 0, :].set(jnp.broadcast_to(P["dn_a_log"].reshape(DN_H, 1), (DN_H, 128)))
    par = par.at[:, 1, :].set(jnp.broadcast_to(P["dn_dt_bias"].reshape(DN_H, 1), (DN_H, 128)))

    h1, n1, g1, u1, *got = _ffn_fwd(x, P["ffn1_norm"], W["ffn1_w_gate"], W["ffn1_w_up"], W["ffn1_w_down"], "ffn1_fwd",
                                    carry=comm.w_in if comm else None)
    if comm:
        W = dict(W, **comm.w_in_weights(got))
    wbig, wsmall = W["w_in_big"], W["w_in_small"]
    pbig, psmall, n2, *got = _norm_proj(h1, P["mix_norm"], wbig, wsmall, "mix_proj", carry=comm.later if comm else None)
    if comm:
        W = dict(W, **comm.later_weights(got))
    oa, sa = _gla_fwd(pbig, psmall, wgate_pad, P["b_gla_gate"], "gla_fwd")
    conv = _conv_fwd(pbig, cw8, "conv_fwd")
    ob, sb, *solved = _gdn_fwd(conv, psmall, par, "gdn_fwd")
    h2, yb = _merge_fwd(h1, oa, ob, pbig, P["gla_head_norm"], P["dn_head_norm"], W["w_out"], "merge_fwd")
    dh3, n3, g3, u3, loss, d_final = _ffn_fwd(h2, P["ffn2_norm"], W["ffn2_w_gate"], W["ffn2_w_up"], W["ffn2_w_down"],
                                              "ffn2_fwd", head=(P["final_norm"], target))

    gw, gs = {}, {"final_norm": d_final}

    def ffn_grads(tag, dh, h, n, g, u, before=None, later=False):
        names = tuple(tag + s for s in ("_w_gate", "_w_up", "_w_down"))
        dg, du, act, dfb, *landed = _ffn_bwd_hidden(dh, g, u, W[names[2]], tag + "_bwd_hidden", carry=before)
        gw[names[0]] = _mm_tn(n, dg, D, FF_CUT, tag + "_dwg")
        gw[names[1]] = _mm_tn(n, du, D, FF_CUT, tag + "_dwu")
        gw[names[2]] = _mm_tn(act, dfb, FF_CUT, D, tag + "_dwd")
        mine = [gw[n] for n in names]
        ex = None if not comm else _SiblingExchange(mine) if later else comm.begin(names, mine)
        dx, gs[tag + "_norm"], *own = _ffn_bwd_input(dh, h, P[tag + "_norm"], dg, du, W[names[0]], W[names[1]],
                                                     tag + "_bwd_input", carry=ex)
        if comm and not later:
            comm.landed(names, own)
        return dx, landed, own

    second = ("ffn2_w_gate", "ffn2_w_up", "ffn2_w_down")
    dh2, _, swapped = ffn_grads("ffn2", dh3, h2, n3, g3, u3, later=True)
    d_oa, d_ob, d_gr, d_dgate, d_ma, d_mb, gs["gla_head_norm"], gs["dn_head_norm"], dh2b, *landed = _merge_bwd(
        dh2, oa, ob, pbig, P["gla_head_norm"], P["dn_head_norm"], W["w_out"], "merge_bwd",
        carry=comm.pair(second, [gw[n] for n in second], swapped) if comm else None)
    if comm:
        comm.landed(second, landed)
    gw["w_out"] = _mm_tn(yb, dh2b, D, D, "dw_out").reshape(N_SHARD, D // N_SHARD, D)
    early = ("w_out",)
    d_gq, d_gk, d_gv, dpre, *swapped = _gla_bwd(pbig, psmall, wgate_pad, P["b_gla_gate"], sa, d_oa, "gla_bwd",
                                                carry=_SiblingExchange([gw["w_out"]]) if comm else None)
    dcq, dck, dcv, dsm, dpar, *landed = _gdn_bwd(conv, psmall, par, sb, d_ob, solved, "gdn_bwd",
                                                 carry=comm.pair(early, [gw["w_out"]], swapped) if comm else None)
    if comm:
        comm.landed(early, landed)
    dsmall, dwgate, gs["b_gla_gate"] = _gla_gate_bwd(dpre, psmall, wgate_pad, dsm, "gla_gate_bwd")
    gs["w_gla_gate"] = dwgate[:GLA_RANK]
    d_x3, dcw = _conv_bwd(dcq, dck, dcv, pbig, cw8, "conv_bwd")
    gs["conv_w"] = dcw[:CONV_K]
    gs["dn_a_log"] = dpar[:, 0, 0].reshape(1, DN_H)
    gs["dn_dt_bias"] = dpar[:, 0, 1].reshape(1, DN_H)
    pieces = (d_gq, d_gk, d_gv, d_gr, d_x3, d_dgate, d_ma, d_mb)
    dh1, gs["mix_norm"] = _proj_bwd(dh2, h1, P["mix_norm"], pieces, dsmall, wbig, wsmall, "proj_bwd")
    dbig = jnp.concatenate([_mm_tn(n2, p, D, 1024, "dw_in_%d" % i) for i, p in enumerate(pieces)], axis=1)
    dsml = _mm_tn(n2, dsmall, D, SMALL, "dw_in_small")
    gw["w_in"] = _cut_cols(_merge_w_in(dbig, dsml))
    grad_x, landed, _ = ffn_grads("ffn1", dh1, x, n1, g1, u1,
                                  before=comm.begin(("w_in",), [gw["w_in"]]) if comm else None)
    if comm:
        comm.landed(("w_in",), landed)
    return loss, grad_x, gw, gs


SMALL_NAMES = ("ffn1_norm", "mix_norm", "ffn2_norm", "final_norm", "b_gla_gate", "gla_head_norm", "dn_head_norm",
               "dn_a_log", "dn_dt_bias")
ROW4 = (("b_gla_gate", 512), ("gla_head_norm", 256), ("dn_head_norm", 128), ("dn_a_log", 8), ("dn_dt_bias", 8))


def _pack_small(d, loss=None):
    row4 = [d[n].reshape(-1) for n, _ in ROW4]
    row4.append(jnp.zeros((1,), F32) if loss is None else loss.reshape(1))
    row4 = jnp.concatenate(row4)
    row4 = jnp.pad(row4, (0, D - row4.shape[0]))
    rows = [d[n].reshape(-1) for n in SMALL_NAMES[:4]] + [row4]
    return jnp.concatenate([jnp.stack(rows), jnp.zeros((3, D), F32)], axis=0)


def _unpack_small(a, like):
    out = {n: a[i].reshape(like[n].shape) for i, n in enumerate(SMALL_NAMES[:4])}
    off = 0
    for n, w in ROW4:
        out[n] = a[4, off:off + w].reshape(like[n].shape)
        off += w
    return out, a[4, off]


WEIGHT_ORDER = ("ffn1_norm", "ffn1_w_gate", "ffn1_w_up", "ffn1_w_down", "mix_norm", "w_in", "w_gla_gate", "b_gla_gate",
                "conv_w", "dn_a_log", "dn_dt_bias", "gla_head_norm", "dn_head_norm", "w_out", "ffn2_norm",
                "ffn2_w_gate", "ffn2_w_up", "ffn2_w_down", "final_norm")
ADAM_ROWS = {"ffn1_w_gate": 256, "ffn1_w_up": 256, "ffn1_w_down": 176, "w_in": 128, "w_gla_gate": 16, "conv_w": 4,
             "w_out": 64, "ffn2_w_gate": 256, "ffn2_w_up": 256, "ffn2_w_down": 176}


def kernel(x, ffn1_norm, ffn1_w_gate, ffn1_w_up, ffn1_w_down, mix_norm, w_in, w_gla_gate, b_gla_gate, conv_w, dn_a_log, dn_dt_bias, gla_head_norm, dn_head_norm, w_out, ffn2_norm, ffn2_w_gate, ffn2_w_up, ffn2_w_down, final_norm, loss_target, m_ffn1_norm, m_ffn1_w_gate, m_ffn1_w_up, m_ffn1_w_down, m_mix_norm, m_w_in, m_w_gla_gate, m_b_gla_gate, m_conv_w, m_dn_a_log, m_dn_dt_bias, m_gla_head_norm, m_dn_head_norm, m_w_out, m_ffn2_norm, m_ffn2_w_gate, m_ffn2_w_up, m_ffn2_w_down, m_final_norm, v_ffn1_norm, v_ffn1_w_gate, v_ffn1_w_up, v_ffn1_w_down, v_mix_norm, v_w_in, v_w_gla_gate, v_b_gla_gate, v_conv_w, v_dn_a_log, v_dn_dt_bias, v_gla_head_norm, v_dn_head_norm, v_w_out, v_ffn2_norm, v_ffn2_w_gate, v_ffn2_w_up, v_ffn2_w_down, v_final_norm):
    given = dict(locals())
    wts = {n: given[n] for n in WEIGHT_ORDER}
    moms = {n: given["m_" + n] for n in WEIGHT_ORDER}
    vars_ = {n: given["v_" + n] for n in WEIGHT_ORDER}
    two_d = lambda a: a.reshape(a.shape[-2], a.shape[-1]) if a.ndim == 3 else a.reshape(1, -1)
    shard = {n: two_d(wts[n]) for n in SHARDED}

    gathered = _gather_weights([shard[n].astype(BF16) for n in BIG[:3]], [shard[n] for n in TINY], "gather_first")
    W = dict(zip(BIG[:3], gathered))
    P = {n: two_d(wts[n]) for n in SMALL_NAMES}
    for n, g in zip(TINY, gathered[3:]):
        P[n] = _join_cols(g)

    my_slot = 2 * lax.axis_index("x") + lax.axis_index("y")
    where = jnp.stack([lax.axis_index("c"), my_slot]).astype(jnp.int32)
    comm = _Comm(where, [shard[n].astype(BF16) for n in BIG[3:]])
    loss, grad_x, gw, gs = _local_step(x[0], loss_target[0], W, P, comm)
    halves = [_add_four(comm.pairs[n][1], comm.got[n], "rs_four_" + n) for n in BIG]
    other_halves = _rs_swap(halves, "rs_swap")

    tiny_rows = jnp.concatenate([gs["w_gla_gate"].reshape(8, D), gs["conv_w"].reshape(12, D), jnp.zeros((4, D), F32)])
    all_sum = _allsum_small(jnp.concatenate([_pack_small(gs, loss[0, 0]), tiny_rows]), "allsum_small")
    small_sum = all_sum[:8]
    small_g, loss_total = _unpack_small(small_sum, P)

    grads, delta, new_m, new_v = {}, {}, {}, {}
    for n, own, got in zip(BIG, halves, other_halves):
        res = _adamw_halves(shard[n], own, got, two_d(moms[n]), two_d(vars_[n]), ADAM_ROWS[n], "adamw_" + n)
        grads[n], delta[n], new_m[n], new_v[n] = (t.reshape(wts[n].shape) for t in res)
    for n, rows in (("w_gla_gate", all_sum[8:16]), ("conv_w", all_sum[16:28])):
        cols = shard[n].shape[1]
        grads[n] = lax.dynamic_slice_in_dim(rows.reshape(shard[n].shape[0], N_SHARD * cols), my_slot * cols, cols, axis=1)
        d, m_, v_ = _adamw(shard[n], grads[n], two_d(moms[n]), two_d(vars_[n]), ADAM_ROWS[n], "adamw_" + n)
        delta[n], new_m[n], new_v[n] = (t.reshape(wts[n].shape) for t in (d, m_, v_))
    pk = lambda src: _pack_small({n: two_d(src[n]) for n in SMALL_NAMES})
    sd, sm_, sv_ = _adamw(pk(wts), small_sum, pk(moms), pk(vars_), 8, "adamw_small")
    for res, dst in ((sd, delta), (sm_, new_m), (sv_, new_v)):
        u, _ = _unpack_small(res, wts)
        dst.update(u)
    grad_w = {n: grads[n].reshape(wts[n].shape) for n in SHARDED}
    grad_w.update({n: small_g[n].reshape(wts[n].shape) for n in SMALL_NAMES})
    return (loss_total, grad_x[None], *[grad_w[n] for n in WEIGHT_ORDER], *[delta[n] for n in WEIGHT_ORDER],
            *[new_m[n] for n in WEIGHT_ORDER], *[new_v[n] for n in WEIGHT_ORDER])
```

```python
import functools
import math

import numpy as np
import jax
import jax.numpy as jnp
from jax import lax
from jax.experimental import pallas as pl
from jax.experimental.pallas import tpu as pltpu

F32 = jnp.float32
BF16 = jnp.bfloat16
HI = lax.Precision.HIGH
MESH = pl.DeviceIdType.MESH
ANY = pl.BlockSpec(memory_space=pl.ANY)

EPS = 1e-6
D = 1024
DFF = 2816
FFN_RES = 0.5
GLA_H, GLA_DK, GLA_DV, GLA_RANK, GLA_TAU = 4, 128, 256, 16, 16.0
DN_H, DN_DK, DN_DV = 8, 128, 128
CONV_K = 4
CHUNK = 64
N_SHARD = 4
FF_CUT = DFF // N_SHARD
ADAM_LR, ADAM_B1, ADAM_B2, ADAM_EPS, ADAM_WD, ADAM_STEP = 0.001, 0.9, 0.999, 1e-08, 0.01, 10

IN_SIZES = (512, 512, 1024, 1024, 16, 1024, 1024, 1024, 1024, 8, 8, 1024, 1024)
IN_OFF = tuple(int(v) for v in np.cumsum((0,) + IN_SIZES))
BIG_COLS = 9216
SMALL = 128
PIECES = (512, 512, 1024, 1024, 3072, 1024, 1024, 1024)

VMEM_LIMIT = 56 * 1024 * 1024
ROW_BLK = 256
BIG_ROW_BLK = 512
ATT_BLK = 512
GDN_BLK = 256
GDN_HEADS = 8


def _cp(*sem):
    return pltpu.CompilerParams(dimension_semantics=sem, vmem_limit_bytes=VMEM_LIMIT)


def _sigmoid(x):
    return 1.0 / (1.0 + jnp.exp(-x))


def _softplus(x):
    return jnp.maximum(x, 0.0) + jnp.log(1.0 + jnp.exp(-jnp.abs(x)))


def _log_sigmoid(x):
    return jnp.minimum(x, 0.0) - jnp.log(1.0 + jnp.exp(-jnp.abs(x)))


def _dot(a, b, prec=None):
    return jnp.dot(a, b, preferred_element_type=F32, precision=prec)


def _dot_nt(a, b, prec=None):
    return lax.dot_general(a, b, (((1,), (1,)), ((), ())), preferred_element_type=F32, precision=prec)


def _dot_tn(a, b, prec=None):
    return lax.dot_general(a, b, (((0,), (0,)), ((), ())), preferred_element_type=F32, precision=prec)


def _b(x):
    return x.astype(BF16)


def _iota2(n, m, axis):
    return lax.broadcasted_iota(jnp.int32, (n, m), axis)


def _load_weights(pairs, sem):
    copies = [pltpu.make_async_copy(s, d, sem.at[i]) for i, (s, d) in enumerate(pairs)]
    for c in copies:
        c.start()
    for c in copies:
        c.wait()


def _ffn_fwd(h, nw, wg, wu, wd, name, carry=None, head=None):
    T = h.shape[0]
    tm = min(BIG_ROW_BLK, T)
    ex_in, ex_args, ex_out, ex_shape, ex_sems = _carry_specs(carry)
    n_head = 2 if head else 0

    def body(*refs):
        h_ref, nw_ref, wg_hbm, wu_hbm, wd_hbm = refs[:5]
        ho_ref, n_ref, g_ref, u_ref = refs[5 + n_head:9 + n_head]
        wg_v, wu_v, wd_v, sem = refs[9 + 2 * n_head:]

        @pl.when(pl.program_id(0) == 0)
        def _():
            _load_weights(((wg_hbm, wg_v), (wu_hbm, wu_v), (wd_hbm, wd_v)), sem)

        x = h_ref[...]
        r = lax.rsqrt(jnp.mean(x * x, axis=-1, keepdims=True) + EPS)
        nb = _b((x * r) * nw_ref[...])
        n_ref[...] = nb
        acc = jnp.zeros((tm, D), F32)
        for s in range(N_SHARD):
            g = _dot(nb, wg_v[s])
            u = _dot(nb, wu_v[s])
            g_ref[s] = _b(g)
            u_ref[s] = _b(u)
            acc += _dot(_b(g * _sigmoid(g) * u), wd_v[s])
        out = x + FFN_RES * acc
        if not head:
            ho_ref[...] = out
            return
        fw_ref, t_ref = refs[5:7]
        loss_ref, dfw_ref = refs[11:13]

        @pl.when(pl.program_id(0) == 0)
        def _():
            loss_ref[...] = jnp.zeros_like(loss_ref)
            dfw_ref[...] = jnp.zeros_like(dfw_ref)

        w = fw_ref[...]
        r = lax.rsqrt(jnp.mean(out * out, axis=-1, keepdims=True) + EPS)
        xhat = out * r
        err = xhat * w - t_ref[...]
        loss_ref[...] += (0.5 / D) * jnp.sum(jnp.sum(err * err, axis=-1, keepdims=True), axis=0, keepdims=True)
        dout = err * (1.0 / D)
        dfw_ref[...] += jnp.sum(dout * xhat, axis=0, keepdims=True)
        dxhat = dout * w
        ho_ref[...] = r * (dxhat - xhat * jnp.mean(dxhat * xhat, axis=-1, keepdims=True))

    row = lambda w: pl.BlockSpec((tm, w), lambda i: (i, 0))
    one = pl.BlockSpec((1, D), lambda i: (0, 0))
    cut = pl.BlockSpec((N_SHARD, tm, FF_CUT), lambda i: (0, i, 0))
    head_out = [pl.BlockSpec((8, 128), lambda i: (0, 0)), one] if head else []
    head_shape = [jax.ShapeDtypeStruct((8, 128), F32), jax.ShapeDtypeStruct((1, D), F32)] if head else []
    return pl.pallas_call(
        _carry(carry, body, 5 + n_head, 4 + n_head, (T // tm,)), name=name, grid=(T // tm,),
        in_specs=[row(D), one, ANY, ANY, ANY] + ([one, row(D)] if head else []) + ex_in,
        out_specs=[row(D), row(D), cut, cut] + head_out + ex_out,
        out_shape=[jax.ShapeDtypeStruct((T, D), F32), jax.ShapeDtypeStruct((T, D), BF16),
                   jax.ShapeDtypeStruct((N_SHARD, T, FF_CUT), BF16),
                   jax.ShapeDtypeStruct((N_SHARD, T, FF_CUT), BF16)] + head_shape + ex_shape,
        scratch_shapes=[pltpu.VMEM((N_SHARD, D, FF_CUT), BF16), pltpu.VMEM((N_SHARD, D, FF_CUT), BF16),
                        pltpu.VMEM((N_SHARD, FF_CUT, D), BF16), pltpu.SemaphoreType.DMA((3,))] + ex_sems,
        compiler_params=_cp("arbitrary"),
    )(h, nw, wg, wu, wd, *(head or ()), *ex_args)


def _ffn_bwd_hidden(dh, g, u, wd, name, carry=None):
    T = dh.shape[0]
    tm = min(BIG_ROW_BLK, T)
    ex_in, ex_args, ex_out, ex_shape, ex_sems = _carry_specs(carry)

    def body(dh_ref, g_ref, u_ref, wd_hbm, dg_ref, du_ref, a_ref, df_ref, wd_v, sem):
        @pl.when(pl.program_id(0) == 0)
        def _():
            _load_weights(((wd_hbm, wd_v),), sem)

        dfb = _b(FFN_RES * dh_ref[...])
        df_ref[...] = dfb
        for s in range(N_SHARD):
            da = _dot_nt(dfb, wd_v[s])
            gg = g_ref[s].astype(F32)
            uu = u_ref[s].astype(F32)
            sg = _sigmoid(gg)
            silu = gg * sg
            a_ref[s] = _b(silu * uu)
            dg_ref[s] = _b(da * uu * (sg * (1.0 + gg * (1.0 - sg))))
            du_ref[s] = _b(da * silu)

    row = pl.BlockSpec((tm, D), lambda i: (i, 0))
    cut = pl.BlockSpec((N_SHARD, tm, FF_CUT), lambda i: (0, i, 0))
    cut_shape = jax.ShapeDtypeStruct((N_SHARD, T, FF_CUT), BF16)
    return pl.pallas_call(
        _carry(carry, body, 4, 4, (T // tm,)), name=name, grid=(T // tm,),
        in_specs=[row, cut, cut, ANY] + ex_in,
        out_specs=[cut, cut, cut, row] + ex_out,
        out_shape=[cut_shape, cut_shape, cut_shape, jax.ShapeDtypeStruct((T, D), BF16)] + ex_shape,
        scratch_shapes=[pltpu.VMEM((N_SHARD, FF_CUT, D), BF16), pltpu.SemaphoreType.DMA((1,))] + ex_sems,
        compiler_params=_cp("arbitrary"),
    )(dh, g, u, wd, *ex_args)


def _ffn_bwd_input(dh, h, nw, dg, du, wg, wu, name, carry=None):
    T = h.shape[0]
    tm = min(BIG_ROW_BLK, T)
    ex_in, ex_args, ex_out, ex_shape, ex_sems = _carry_specs(carry)

    def body(dh_ref, h_ref, nw_ref, dg_ref, du_ref, wg_hbm, wu_hbm, dx_ref, dnw_ref, wg_v, wu_v, sem):
        @pl.when(pl.program_id(0) == 0)
        def _():
            _load_weights(((wg_hbm, wg_v), (wu_hbm, wu_v)), sem)
            dnw_ref[...] = jnp.zeros_like(dnw_ref)

        dn = jnp.zeros((tm, D), F32)
        for s in range(N_SHARD):
            dn += _dot_nt(dg_ref[s], wg_v[s]) + _dot_nt(du_ref[s], wu_v[s])
        x = h_ref[...]
        r = lax.rsqrt(jnp.mean(x * x, axis=-1, keepdims=True) + EPS)
        xhat = x * r
        dnw_ref[...] += jnp.sum(dn * xhat, axis=0, keepdims=True)
        dxhat = dn * nw_ref[...]
        dx_ref[...] = dh_ref[...] + r * (dxhat - xhat * jnp.mean(dxhat * xhat, axis=-1, keepdims=True))

    row = pl.BlockSpec((tm, D), lambda i: (i, 0))
    one = pl.BlockSpec((1, D), lambda i: (0, 0))
    cut = pl.BlockSpec((N_SHARD, tm, FF_CUT), lambda i: (0, i, 0))
    return pl.pallas_call(
        _carry(carry, body, 7, 2, (T // tm,)), name=name, grid=(T // tm,),
        in_specs=[row, row, one, cut, cut, ANY, ANY] + ex_in,
        out_specs=[row, one] + ex_out,
        out_shape=[jax.ShapeDtypeStruct((T, D), F32), jax.ShapeDtypeStruct((1, D), F32)] + ex_shape,
        scratch_shapes=[pltpu.VMEM((N_SHARD, D, FF_CUT), BF16), pltpu.VMEM((N_SHARD, D, FF_CUT), BF16),
                        pltpu.SemaphoreType.DMA((2,))] + ex_sems,
        compiler_params=_cp("arbitrary"),
    )(dh, h, nw, dg, du, wg, wu, *ex_args)


def _mm_tn(a, b, bm, bn, name, out_dtype=BF16, tk=4096):
    cuts = a.shape[0] if a.ndim == 3 else (b.shape[0] if b.ndim == 3 else None)
    T, M = a.shape[-2:]
    N = b.shape[-1]
    tk = min(tk, T)
    bm, bn = min(bm, M), min(bn, N)
    nk = T // tk

    def body(a_ref, b_ref, o_ref, acc_ref):
        k = pl.program_id(3)

        @pl.when(k == 0)
        def _():
            acc_ref[...] = jnp.zeros_like(acc_ref)

        av = a_ref[0] if a.ndim == 3 else a_ref[...]
        bv = b_ref[0] if b.ndim == 3 else b_ref[...]
        acc_ref[...] += _dot_tn(_b(av), _b(bv))

        @pl.when(k == nk - 1)
        def _():
            res = acc_ref[...].astype(out_dtype)
            if cuts is None:
                o_ref[...] = res
            else:
                o_ref[0] = res

    a_spec = (pl.BlockSpec((1, tk, bm), lambda s, i, j, k: (s, k, i)) if a.ndim == 3
              else pl.BlockSpec((tk, bm), lambda s, i, j, k: (k, i)))
    b_spec = (pl.BlockSpec((1, tk, bn), lambda s, i, j, k: (s, k, j)) if b.ndim == 3
              else pl.BlockSpec((tk, bn), lambda s, i, j, k: (k, j)))
    if cuts is None:
        o_spec, o_shape = pl.BlockSpec((bm, bn), lambda s, i, j, k: (i, j)), (M, N)
    else:
        o_spec, o_shape = pl.BlockSpec((1, bm, bn), lambda s, i, j, k: (s, i, j)), (cuts, M, N)
    return pl.pallas_call(
        body, name=name, grid=(cuts or 1, M // bm, N // bn, nk),
        in_specs=[a_spec, b_spec], out_specs=o_spec,
        out_shape=jax.ShapeDtypeStruct(o_shape, out_dtype),
        scratch_shapes=[pltpu.VMEM((bm, bn), F32)],
        compiler_params=_cp("parallel", "parallel", "parallel", "arbitrary"),
    )(a, b)


def _norm_proj(h, nw, wbig, wsmall, name, carry=None):
    T = h.shape[0]
    tm = min(512, T)
    tn = 1536
    qk = 2 * GLA_H * GLA_DK
    ex_in, ex_args, ex_out, ex_shape, ex_sems = _carry_specs(carry)

    def body(h_ref, nw_ref, wb_hbm, ws_ref, pb_ref, ps_ref, n_ref, qk_ref, wb_v, sem):
        @pl.when(pl.program_id(0) == 0)
        def _():
            _load_weights(((wb_hbm, wb_v),), sem)

        x = h_ref[...]
        r = lax.rsqrt(jnp.mean(x * x, axis=-1, keepdims=True) + EPS)
        nb = _b((x * r) * nw_ref[...])
        n_ref[...] = nb
        ps_ref[...] = _dot(nb, ws_ref[...])
        for j in range(BIG_COLS // tn):
            part = _dot(nb, wb_v[:, j * tn:(j + 1) * tn])
            pb_ref[:, j * tn:(j + 1) * tn] = _b(part)
            if j == 0:
                qk_ref[...] = part[:, :qk]

    row = lambda w: pl.BlockSpec((tm, w), lambda i: (i, 0))
    return pl.pallas_call(
        _carry(carry, body, 4, 4, (T // tm,)), name=name, grid=(T // tm,),
        in_specs=[row(D), pl.BlockSpec((1, D), lambda i: (0, 0)), ANY, pl.BlockSpec((D, SMALL), lambda i: (0, 0))]
        + ex_in,
        out_specs=[row(BIG_COLS), row(SMALL), row(D), row(qk)] + ex_out,
        out_shape=[jax.ShapeDtypeStruct((T, BIG_COLS), BF16), jax.ShapeDtypeStruct((T, SMALL), F32),
                   jax.ShapeDtypeStruct((T, D), BF16), jax.ShapeDtypeStruct((T, qk), F32)] + ex_shape,
        scratch_shapes=[pltpu.VMEM((D, BIG_COLS), BF16), pltpu.SemaphoreType.DMA((1,))] + ex_sems,
        compiler_params=_cp("arbitrary"),
    )(h, nw, wbig, wsmall, *ex_args)


def _proj_bwd(dh, h, nw, pieces, dsmall, wbig, wsmall, name):
    T = h.shape[0]
    tm = min(BIG_ROW_BLK, T)
    offs = tuple(int(v) for v in np.cumsum((0,) + PIECES))

    def body(dh_ref, h_ref, nw_ref, *rest):
        p_refs = rest[:len(PIECES)]
        ds_ref, wb_hbm, ws_ref, dx_ref, dnw_ref, wb_v, sem = rest[len(PIECES):]

        @pl.when(pl.program_id(0) == 0)
        def _():
            _load_weights(((wb_hbm, wb_v),), sem)
            dnw_ref[...] = jnp.zeros_like(dnw_ref)

        dn = _dot_nt(_b(ds_ref[...]), ws_ref[...])
        for p_ref, lo, wdt in zip(p_refs, offs, PIECES):
            dn += _dot_nt(p_ref[...], wb_v[:, lo:lo + wdt])
        x = h_ref[...]
        r = lax.rsqrt(jnp.mean(x * x, axis=-1, keepdims=True) + EPS)
        xhat = x * r
        dnw_ref[...] += jnp.sum(dn * xhat, axis=0, keepdims=True)
        dxhat = dn * nw_ref[...]
        dx_ref[...] = dh_ref[...] + r * (dxhat - xhat * jnp.mean(dxhat * xhat, axis=-1, keepdims=True))

    row = lambda w: pl.BlockSpec((tm, w), lambda i: (i, 0))
    one = pl.BlockSpec((1, D), lambda i: (0, 0))
    return pl.pallas_call(
        body, name=name, grid=(T // tm,),
        in_specs=[row(D), row(D), one] + [row(w) for w in PIECES] + [row(SMALL), ANY, pl.BlockSpec((D, SMALL), lambda i: (0, 0))],
        out_specs=[row(D), one],
        out_shape=[jax.ShapeDtypeStruct((T, D), F32), jax.ShapeDtypeStruct((1, D), F32)],
        scratch_shapes=[pltpu.VMEM((D, BIG_COLS), BF16), pltpu.SemaphoreType.DMA((1,))],
        compiler_params=_cp("arbitrary"),
    )(dh, h, nw, *pieces, dsmall, wbig, wsmall)


def _gla_block(q_ref, k_ref, sm_ref, wg_ref, bg_ref, nc, tril):
    nbat = GLA_H * nc
    q = _heads_first(q_ref[...].astype(F32), nc, GLA_DK)
    k = _heads_first(k_ref[...].astype(F32), nc, GLA_DK)
    pre = _heads_first(_dot(sm_ref[...], wg_ref[...], HI) + bg_ref[...], nc, GLA_DK)
    la = _log_sigmoid(pre) * (1.0 / GLA_TAU)
    bc = _bmm(jnp.broadcast_to(tril, (nbat, CHUNK, CHUNK)), la, HI)
    bl = bc[:, CHUNK - 1:CHUNK, :]
    eb = jnp.exp(bc)
    enb = jnp.exp(-bc)
    ebl = jnp.exp(bl - bc)
    q_in = q * (GLA_DK ** -0.5) * eb
    k_out = k * enb
    k_st = k * ebl
    a_ch = jnp.exp(bl)
    return pre, eb, enb, ebl, q_in, k_out, k_st, a_ch


def _gla_specs(blk, idx):
    hk, hv = GLA_H * GLA_DK, GLA_H * GLA_DV
    return [pl.BlockSpec((blk, hk), lambda j: (idx(j), 0)),
            pl.BlockSpec((blk, hk), lambda j: (idx(j), 1)),
            pl.BlockSpec((blk, hv), lambda j: (idx(j), 1)),
            pl.BlockSpec((blk, SMALL), lambda j: (idx(j), 0)),
            pl.BlockSpec((SMALL, hk), lambda j: (0, 0)),
            pl.BlockSpec((1, hk), lambda j: (0, 0))]


def _gla_fwd(pbig, qk, psmall, wgate, bgate, name):
    T = pbig.shape[0]
    blk = min(ATT_BLK, T)
    nc = blk // CHUNK

    def body(q_ref, k_ref, v_ref, sm_ref, wg_ref, bg_ref, o_ref, ss_ref, st_ref):
        @pl.when(pl.program_id(0) == 0)
        def _():
            st_ref[...] = jnp.zeros_like(st_ref)

        causal = _iota2(CHUNK, CHUNK, 0) >= _iota2(CHUNK, CHUNK, 1)
        _, _, _, _, q_in, k_out, k_st, a_ch = _gla_block(q_ref, k_ref, sm_ref, wg_ref, bg_ref, nc, causal.astype(F32))
        v = _heads_first(v_ref[...], nc, GLA_DV)
        qb = _b(q_in)
        sc = jnp.where(causal, _bmm_nt(qb, _b(k_out)), 0.0)
        kv = _bmm_tn(v, _b(k_st))
        before = [None] * (GLA_H * nc)
        for i in range(GLA_H):
            st = st_ref[i]
            for c in range(nc):
                n = i * nc + c
                before[n] = st
                st = st * a_ch[n] + kv[n]
            st_ref[i] = st
        states = jnp.stack(before)
        ss_ref[...] = states.reshape(GLA_H, nc, GLA_DV, GLA_DK)
        o_ref[...] = _b(_heads_last(_bmm(_b(sc), v) + _bmm_nt(qb, _b(states)), nc))

    return pl.pallas_call(
        body, name=name, grid=(T // blk,),
        in_specs=_gla_specs(blk, lambda j: j),
        out_specs=[pl.BlockSpec((blk, GLA_H * GLA_DV), lambda j: (j, 0)),
                   pl.BlockSpec((GLA_H, nc, GLA_DV, GLA_DK), lambda j: (0, j, 0, 0))],
        out_shape=[jax.ShapeDtypeStruct((T, GLA_H * GLA_DV), BF16),
                   jax.ShapeDtypeStruct((GLA_H, T // CHUNK, GLA_DV, GLA_DK), F32)],
        scratch_shapes=[pltpu.VMEM((GLA_H, GLA_DV, GLA_DK), F32)],
        compiler_params=_cp("arbitrary"),
    )(qk, qk, pbig, psmall, wgate, bgate)


def _gla_bwd(pbig, qk, psmall, wgate, bgate, states, do, name, carry=None):
    ex_in, ex_args, ex_out, ex_shape, ex_sems = _carry_specs(carry)
    T = pbig.shape[0]
    blk = min(ATT_BLK, T)
    nc = blk // CHUNK
    nb = T // blk
    nbat = GLA_H * nc

    def body(q_ref, k_ref, v_ref, sm_ref, wg_ref, bg_ref, ss_ref, do_ref, dq_ref, dk_ref, dv_ref, dpre_ref, dst_ref):
        @pl.when(pl.program_id(0) == 0)
        def _():
            dst_ref[...] = jnp.zeros_like(dst_ref)

        causal = _iota2(CHUNK, CHUNK, 0) >= _iota2(CHUNK, CHUNK, 1)
        triu = (_iota2(CHUNK, CHUNK, 0) <= _iota2(CHUNK, CHUNK, 1)).astype(F32)
        pre, eb, enb, ebl, q_in, k_out, k_st, a_ch = _gla_block(q_ref, k_ref, sm_ref, wg_ref, bg_ref, nc,
                                                                causal.astype(F32))
        v = _heads_first(v_ref[...], nc, GLA_DV)
        dob = _b(_heads_first(do_ref[...], nc, GLA_DV))
        st = ss_ref[...].reshape(nbat, GLA_DV, GLA_DK)
        qb, kob, kstb = _b(q_in), _b(k_out), _b(k_st)
        qdo = _bmm_tn(dob, qb)
        after = [None] * nbat
        for i in range(GLA_H):
            dst = dst_ref[i]
            for c in range(nc - 1, -1, -1):
                n = i * nc + c
                after[n] = dst
                dst = dst * a_ch[n] + qdo[n]
            dst_ref[i] = dst
        dsa = jnp.stack(after)
        dsb = _b(dsa)
        sc = jnp.where(causal, _bmm_nt(qb, kob), 0.0)
        dsc = _b(jnp.where(causal, _bmm_nt(dob, v), 0.0))
        dq_in = _bmm(dob, _b(st)) + _bmm(dsc, kob)
        dk_out = _bmm_tn(dsc, qb)
        dk_st = _bmm(v, dsb)
        dv_ref[...] = _b(_heads_last(_bmm_tn(_b(sc), dob) + _bmm_nt(kstb, dsb), nc))
        da_ch = jnp.sum(st * dsa, axis=1, keepdims=True)
        tk = dk_st * k_st
        db = dq_in * q_in - dk_out * k_out - tk
        db_last = jnp.sum(tk, axis=1, keepdims=True) + da_ch * a_ch
        dq_ref[...] = _b(_heads_last(dq_in * (GLA_DK ** -0.5) * eb, nc))
        dk_ref[...] = _b(_heads_last(dk_out * enb + dk_st * ebl, nc))
        dla = _bmm(jnp.broadcast_to(triu, (nbat, CHUNK, CHUNK)), db, HI) + db_last
        dpre_ref[...] = _heads_last(dla * (1.0 / GLA_TAU) * _sigmoid(-pre), nc)

    r = lambda j: nb - 1 - j
    hk, hv = GLA_H * GLA_DK, GLA_H * GLA_DV
    return pl.pallas_call(
        _carry(carry, body, 8, 4, (nb,)), name=name, grid=(nb,),
        in_specs=_gla_specs(blk, r) + [pl.BlockSpec((GLA_H, nc, GLA_DV, GLA_DK), lambda j: (0, r(j), 0, 0)),
                                      pl.BlockSpec((blk, hv), lambda j: (r(j), 0))] + ex_in,
        out_specs=[pl.BlockSpec((blk, hk), lambda j: (r(j), 0)), pl.BlockSpec((blk, hk), lambda j: (r(j), 0)),
                   pl.BlockSpec((blk, hv), lambda j: (r(j), 0)), pl.BlockSpec((blk, hk), lambda j: (r(j), 0))] + ex_out,
        out_shape=[jax.ShapeDtypeStruct((T, hk), BF16), jax.ShapeDtypeStruct((T, hk), BF16),
                   jax.ShapeDtypeStruct((T, hv), BF16), jax.ShapeDtypeStruct((T, hk), F32)] + ex_shape,
        scratch_shapes=[pltpu.VMEM((GLA_H, GLA_DV, GLA_DK), F32)] + ex_sems,
        compiler_params=_cp("arbitrary"),
    )(qk, qk, pbig, psmall, wgate, bgate, states, do, *ex_args)


def _gla_gate_bwd(dpre, psmall, wgate, dsm, name):
    T = dpre.shape[0]
    tm = min(512, T)
    W = GLA_H * GLA_DK
    ngrp = dsm.shape[0]

    def body(dp_ref, sm_ref, wg_ref, dsm_ref, ds_ref, dw_ref, db_ref):
        @pl.when(pl.program_id(0) == 0)
        def _():
            dw_ref[...] = jnp.zeros_like(dw_ref)
            db_ref[...] = jnp.zeros_like(db_ref)

        dp = dp_ref[...]
        ds = _dot_nt(dp, wg_ref[...], HI)
        for i in range(ngrp):
            ds += dsm_ref[i]
        ds_ref[...] = ds
        dw_ref[...] += _dot_tn(sm_ref[...], dp, HI)
        db_ref[...] += jnp.sum(dp, axis=0, keepdims=True)

    return pl.pallas_call(
        body, name=name, grid=(T // tm,),
        in_specs=[pl.BlockSpec((tm, W), lambda i: (i, 0)), pl.BlockSpec((tm, SMALL), lambda i: (i, 0)),
                  pl.BlockSpec((SMALL, W), lambda i: (0, 0)), pl.BlockSpec((ngrp, tm, SMALL), lambda i: (0, i, 0))],
        out_specs=[pl.BlockSpec((tm, SMALL), lambda i: (i, 0)), pl.BlockSpec((SMALL, W), lambda i: (0, 0)),
                   pl.BlockSpec((1, W), lambda i: (0, 0))],
        out_shape=[jax.ShapeDtypeStruct((T, SMALL), F32), jax.ShapeDtypeStruct((SMALL, W), F32),
                   jax.ShapeDtypeStruct((1, W), F32)],
        compiler_params=_cp("arbitrary"),
    )(dpre, psmall, wgate, dsm)


CONV_C = 3 * 1024
CONV_BLK = 256


def _conv_fwd(pbig, cw8, name):
    T = pbig.shape[0]
    blk = min(CONV_BLK, T)

    def body(x_ref, w_ref, c_ref, prev_ref):
        @pl.when(pl.program_id(0) == 0)
        def _():
            prev_ref[...] = jnp.zeros_like(prev_ref)

        x = x_ref[...].astype(F32)
        prev = prev_ref[...]
        row8 = _iota2(8, CONV_C, 0)
        acc = x * w_ref[CONV_K - 1:CONV_K, :]
        for s in range(1, CONV_K):
            xs = pltpu.roll(x, s, 0)
            top = jnp.where(row8 < s, pltpu.roll(prev, s, 0), xs[:8])
            xs = jnp.concatenate([top, xs[8:]], axis=0)
            acc += xs * w_ref[CONV_K - 1 - s:CONV_K - s, :]
        c_ref[...] = _b(acc)
        prev_ref[...] = x[blk - 8:]

    return pl.pallas_call(
        body, name=name, grid=(T // blk,),
        in_specs=[pl.BlockSpec((blk, CONV_C), lambda i: (i, 1)), pl.BlockSpec((8, CONV_C), lambda i: (0, 0))],
        out_specs=pl.BlockSpec((blk, CONV_C), lambda i: (i, 0)),
        out_shape=jax.ShapeDtypeStruct((T, CONV_C), BF16),
        scratch_shapes=[pltpu.VMEM((8, CONV_C), F32)],
        compiler_params=_cp("arbitrary"),
    )(pbig, cw8)


def _conv_bwd(dcq, dck, dcv, pbig, cw8, name):
    T = pbig.shape[0]
    blk = min(CONV_BLK, T)
    nb = T // blk

    def body(dq_ref, dk_ref, dv_ref, x_ref, w_ref, dx_ref, dw_ref, nxt_ref):
        @pl.when(pl.program_id(0) == 0)
        def _():
            nxt_ref[...] = jnp.zeros_like(nxt_ref)
            dw_ref[...] = jnp.zeros_like(dw_ref)

        dc = jnp.concatenate([dq_ref[...], dk_ref[...], dv_ref[...]], axis=1).astype(F32)
        x = x_ref[...].astype(F32)
        nxt = nxt_ref[...]
        row8 = _iota2(8, CONV_C, 0)
        acc = dc * w_ref[CONV_K - 1:CONV_K, :]
        dws = [jnp.sum(dc * x, axis=0, keepdims=True)]
        for s in range(1, CONV_K):
            ds = pltpu.roll(dc, blk - s, 0)
            bot = jnp.where(row8 >= 8 - s, pltpu.roll(nxt, 8 - s, 0), ds[blk - 8:])
            ds = jnp.concatenate([ds[:blk - 8], bot], axis=0)
            acc += ds * w_ref[CONV_K - 1 - s:CONV_K - s, :]
            dws.append(jnp.sum(ds * x, axis=0, keepdims=True))
        dx_ref[...] = _b(acc)
        dw_ref[...] += jnp.concatenate(dws[::-1] + [jnp.zeros((8 - CONV_K, CONV_C), F32)], axis=0)
        nxt_ref[...] = dc[:8]

    part = pl.BlockSpec((blk, 1024), lambda i: (nb - 1 - i, 0))
    return pl.pallas_call(
        body, name=name, grid=(nb,),
        in_specs=[part, part, part, pl.BlockSpec((blk, CONV_C), lambda i: (nb - 1 - i, 1)),
                  pl.BlockSpec((8, CONV_C), lambda i: (0, 0))],
        out_specs=[pl.BlockSpec((blk, CONV_C), lambda i: (nb - 1 - i, 0)), pl.BlockSpec((8, CONV_C), lambda i: (0, 0))],
        out_shape=[jax.ShapeDtypeStruct((T, CONV_C), BF16), jax.ShapeDtypeStruct((8, CONV_C), F32)],
        scratch_shapes=[pltpu.VMEM((8, CONV_C), F32)],
        compiler_params=_cp("arbitrary"),
    )(dcq, dck, dcv, pbig, cw8)


def _col(x, lane):
    if isinstance(lane, int):
        return jnp.broadcast_to(x[:, lane:lane + 1], x.shape)
    sel = _iota2(x.shape[0], x.shape[1], 1) == lane
    return jnp.broadcast_to(jnp.sum(jnp.where(sel, x, 0.0), axis=1, keepdims=True), x.shape)


def _bmm(a, b, prec=None):
    return jnp.einsum("bij,bjk->bik", a, b, preferred_element_type=F32, precision=prec)


def _bmm_nt(a, b, prec=None):
    return jnp.einsum("bij,bkj->bik", a, b, preferred_element_type=F32, precision=prec)


def _bmm_tn(a, b, prec=None):
    return jnp.einsum("bji,bjk->bik", a, b, preferred_element_type=F32, precision=prec)


def _unit_lower_inverse(low):
    eye = (_iota2(CHUNK, CHUNK, 0) == _iota2(CHUNK, CHUNK, 1)).astype(F32)
    xk = -low
    inv = eye + xk
    for _ in range(5):
        xb = _b(xk)
        xk = _bmm(xb, xb)
        inv = inv + _bmm(_b(inv), _b(xk))
    resid = eye - _bmm(eye + low, inv, HI)
    return inv + _bmm(inv, resid, HI)


def _heads_first(x, nc, w=128):
    hb = x.shape[1] // w
    return jnp.concatenate([x[:, i * w:(i + 1) * w].reshape(nc, CHUNK, w) for i in range(hb)], axis=0)


def _heads_last(x, nc):
    hb = x.shape[0] // nc
    return jnp.concatenate([x[i * nc:(i + 1) * nc].reshape(nc * CHUNK, x.shape[2]) for i in range(hb)], axis=1)


def _gdn_block(cq_ref, ck_ref, cv_ref, sm_ref, par_ref, h0, hb, nc, masks, solved=None):
    causal, strict, tril, eye = masks
    nbat = hb * nc
    cq = _heads_first(cq_ref[...].astype(F32), nc)
    ck = _heads_first(ck_ref[...].astype(F32), nc)
    cv = _heads_first(cv_ref[...].astype(F32), nc)
    sq, sk, sv = _sigmoid(cq), _sigmoid(ck), _sigmoid(cv)
    q, k, v = cq * sq, ck * sk, cv * sv
    rq = lax.rsqrt(jnp.sum(q * q, axis=-1, keepdims=True) + EPS)
    rk = lax.rsqrt(jnp.sum(k * k, axis=-1, keepdims=True) + EPS)
    qh, kn = q * rq, k * rk
    qn = qh * (DN_DK ** -0.5)
    sm = sm_ref[...]
    per_head = lambda fn: jnp.concatenate([fn(i) for i in range(hb)], axis=0)
    braw = per_head(lambda i: _col(sm, GLA_RANK + h0 + i).reshape(nc, CHUNK, 128))
    araw = per_head(lambda i: _col(sm, GLA_RANK + DN_H + h0 + i).reshape(nc, CHUNK, 128))
    ea = per_head(lambda i: jnp.broadcast_to(jnp.exp(par_ref[i, 0:1, :])[None], (nc, 1, 128)))
    bias = per_head(lambda i: jnp.broadcast_to(par_ref[i, 1:2, :][None], (nc, 1, 128)))
    beta = _sigmoid(braw)
    sp_arg = araw + bias
    g = -ea * _softplus(sp_arg)
    G = _bmm(jnp.broadcast_to(tril, (nbat, CHUNK, CHUNK)), g, HI)
    gc = G[:, :, :CHUNK]
    grow = jnp.sum(eye * gc, axis=1, keepdims=True)
    decay = jnp.exp(jnp.where(causal, gc - grow, -1e30))
    kb = kn * beta
    A = _bmm_nt(_b(kb), _b(kn))
    eG = jnp.exp(G)
    gl = G[:, CHUNK - 1:CHUNK, :]
    eGl = jnp.exp(gl - G)
    g_ch = jnp.exp(gl)
    rv = v * beta
    rkk = kb * eG
    if solved is None:
        tinv_b = _b(_unit_lower_inverse(jnp.where(strict, A * decay, 0.0)))
        u = _bmm(tinv_b, _b(rv))
        w = _b(_bmm(tinv_b, _b(rkk)))
    else:
        tinv_b, u, w = solved
    B = _bmm_nt(_b(qn), _b(kn))
    qk = jnp.where(causal, B * decay, 0.0)
    q_dec = qn * eG
    k_st = kn * eGl
    return dict(cq=cq, ck=ck, cv=cv, sq=sq, sk=sk, sv=sv, q=q, k=k, v=v, rq=rq, rk=rk, qh=qh, kn=kn, qn=qn,
                beta=beta, ea=ea, sp_arg=sp_arg, g=g, G=G, decay=decay, kb=kb, A=A, tinv_b=tinv_b, eG=eG, eGl=eGl,
                g_ch=g_ch, rv=rv, rkk=rkk, u=u, w=w, B=B, qk=qk, q_dec=q_dec, k_st=k_st)


def _gdn_masks():
    r, c = _iota2(CHUNK, CHUNK, 0), _iota2(CHUNK, CHUNK, 1)
    return r >= c, r > c, (r >= c).astype(F32), (r == c).astype(F32)


def _gdn_specs(blk, hb, idx):
    ng = DN_H // hb
    return [pl.BlockSpec((blk, hb * DN_DK), lambda h, j: (idx(j), h)),
            pl.BlockSpec((blk, hb * DN_DK), lambda h, j: (idx(j), ng + h)),
            pl.BlockSpec((blk, hb * DN_DV), lambda h, j: (idx(j), 2 * ng + h)),
            pl.BlockSpec((blk, SMALL), lambda h, j: (idx(j), 0)),
            pl.BlockSpec((hb, 8, 128), lambda h, j: (h, 0, 0))]


def _gdn_solved_specs(blk, hb, idx):
    nc = blk // CHUNK
    spec = lambda w: pl.BlockSpec((hb, nc, CHUNK, w), lambda h, j: (h, idx(j), 0, 0))
    return [spec(CHUNK), spec(DN_DV), spec(DN_DK)]


def _gdn_fwd(conv, psmall, par, name):
    T = conv.shape[0]
    blk = min(GDN_BLK, T)
    nc = blk // CHUNK
    hb = GDN_HEADS
    N = T // CHUNK

    def body(cq_ref, ck_ref, cv_ref, sm_ref, par_ref, o_ref, ss_ref, ti_ref, u_ref, w_ref, s_ref):
        @pl.when(pl.program_id(1) == 0)
        def _():
            s_ref[...] = jnp.zeros_like(s_ref)

        h0 = 0 if hb == DN_H else pl.program_id(0) * hb
        f = _gdn_block(cq_ref, ck_ref, cv_ref, sm_ref, par_ref, h0, hb, nc, _gdn_masks())
        ti_ref[...] = f["tinv_b"].reshape(hb, nc, CHUNK, CHUNK)
        u_ref[...] = f["u"].reshape(hb, nc, CHUNK, DN_DV)
        w_ref[...] = f["w"].reshape(hb, nc, CHUNK, DN_DK)
        wb, ub, kstb, qkb = f["w"], _b(f["u"]), _b(f["k_st"]), _b(f["qk"])
        mix = _b(_bmm_tn(kstb, wb))
        add = _bmm_tn(kstb, ub)
        q_eff = _b(f["q_dec"] - _bmm(qkb, wb))
        before = [None] * (hb * nc)
        S = [s_ref[i] for i in range(hb)]
        for c in range(nc):
            for i in range(hb):
                n = i * nc + c
                before[n] = S[i]
                S[i] = S[i] * f["g_ch"][n] - _dot(mix[n], _b(S[i])) + add[n]
        for i in range(hb):
            s_ref[i] = S[i]
        states = jnp.stack(before)
        ss_ref[...] = states.reshape(hb, nc, DN_DK, DN_DV)
        o_ref[...] = _b(_heads_last(_bmm(qkb, ub) + _bmm(q_eff, _b(states)), nc))

    return pl.pallas_call(
        body, name=name, grid=(DN_H // hb, T // blk),
        in_specs=_gdn_specs(blk, hb, lambda j: j),
        out_specs=[pl.BlockSpec((blk, hb * DN_DV), lambda h, j: (j, h)),
                   pl.BlockSpec((hb, nc, DN_DK, DN_DV), lambda h, j: (h, j, 0, 0))]
        + _gdn_solved_specs(blk, hb, lambda j: j),
        out_shape=[jax.ShapeDtypeStruct((T, DN_H * DN_DV), BF16), jax.ShapeDtypeStruct((DN_H, N, DN_DK, DN_DV), F32),
                   jax.ShapeDtypeStruct((DN_H, N, CHUNK, CHUNK), BF16), jax.ShapeDtypeStruct((DN_H, N, CHUNK, DN_DV), F32),
                   jax.ShapeDtypeStruct((DN_H, N, CHUNK, DN_DK), BF16)],
        scratch_shapes=[pltpu.VMEM((hb, DN_DK, DN_DV), F32)],
        compiler_params=_cp("parallel", "arbitrary"),
    )(conv, conv, conv, psmall, par)


def _gdn_bwd(conv, psmall, par, states, do, solved, name, carry=None):
    ex_in, ex_args, ex_out, ex_shape, ex_sems = _carry_specs(carry)
    T = conv.shape[0]
    blk = min(GDN_BLK, T)
    nc = blk // CHUNK
    nb = T // blk
    hb = GDN_HEADS
    nbat = hb * nc
    rsum = lambda x: jnp.sum(x, axis=-1, keepdims=True)

    def body(cq_ref, ck_ref, cv_ref, sm_ref, par_ref, ss_ref, do_ref, ti_ref, u_ref, w_ref,
             dcq_ref, dck_ref, dcv_ref, dsm_ref, dpar_ref, ds_ref):
        @pl.when(pl.program_id(1) == 0)
        def _():
            ds_ref[...] = jnp.zeros_like(ds_ref)
            dpar_ref[...] = jnp.zeros_like(dpar_ref)

        masks = _gdn_masks()
        causal, strict, tril, eye = masks
        triu = (_iota2(CHUNK, CHUNK, 0) <= _iota2(CHUNK, CHUNK, 1)).astype(F32)
        lane = _iota2(CHUNK, 128, 1)
        last_row = _iota2(CHUNK, 128, 0) == CHUNK - 1
        h0 = 0 if hb == DN_H else pl.program_id(0) * hb
        solved = (ti_ref[...].reshape(nbat, CHUNK, CHUNK), u_ref[...].reshape(nbat, CHUNK, DN_DV),
                  w_ref[...].reshape(nbat, CHUNK, DN_DK))
        f = _gdn_block(cq_ref, ck_ref, cv_ref, sm_ref, par_ref, h0, hb, nc, masks, solved)
        S = ss_ref[...].reshape(nbat, DN_DK, DN_DV)
        Sb = _b(S)
        do_ = _b(_heads_first(do_ref[...], nc))
        wb, qdb, kstb, qkb = _b(f["w"]), _b(f["q_dec"]), _b(f["k_st"]), _b(f["qk"])
        vnb = _b(f["u"] - _bmm(wb, Sb))
        dvn0 = _bmm_tn(qkb, do_)
        qdo = _bmm_tn(qdb, do_)
        dS = [ds_ref[i] for i in range(hb)]
        after = [None] * nbat
        for c in range(nc - 1, -1, -1):
            for i in range(hb):
                n = i * nc + c
                after[n] = dS[i]
                dvn_c = _b(dvn0[n] + _dot(kstb[n], _b(dS[i])))
                dS[i] = dS[i] * f["g_ch"][n] + qdo[n] - _dot_tn(wb[n], dvn_c)
        for i in range(hb):
            ds_ref[i] = dS[i]
        dSa = jnp.stack(after)
        dSb = _b(dSa)
        dvn = dvn0 + _bmm(kstb, dSb)
        dvnb = _b(dvn)
        dq_dec = _bmm_nt(do_, Sb)
        dqk = jnp.where(causal, _bmm_nt(do_, vnb), 0.0)
        dk_st = _bmm_nt(vnb, dSb)
        dg_ch = jnp.sum(rsum(S * dSa), axis=1, keepdims=True)
        dw = -_bmm_nt(dvnb, Sb)
        drv = _bmm_tn(f["tinv_b"], dvnb)
        drk = _bmm_tn(f["tinv_b"], _b(dw))
        dlow = jnp.where(strict, -(_bmm_nt(_b(drv), _b(f["u"])) + _bmm_nt(_b(drk), wb)), 0.0)
        dv = drv * f["beta"]
        dbeta = rsum(drv * f["v"])
        dkb = drk * f["eG"]
        dG = rsum(drk * f["rkk"])
        dA = dlow * f["decay"]
        ddec = dlow * f["A"]
        dkb += _bmm(_b(dA), _b(f["kn"]))
        dkn = _bmm_tn(_b(dA), _b(f["kb"]))
        dB = dqk * f["decay"]
        ddec += dqk * f["B"]
        dqn = _bmm(_b(dB), _b(f["kn"]))
        dkn += _bmm_tn(_b(dB), _b(f["qn"]))
        dD = ddec * f["decay"]
        dG += rsum(dD) - rsum(eye * jnp.sum(dD, axis=1, keepdims=True))
        dqn += dq_dec * f["eG"]
        dG += rsum(dq_dec * f["q_dec"])
        dkn += dk_st * f["eGl"]
        tks = rsum(dk_st * f["k_st"])
        dG -= tks
        dG_last = jnp.sum(tks, axis=1, keepdims=True) + dg_ch * f["g_ch"][:, :, :1]
        dkn += dkb * f["beta"]
        dbeta += rsum(dkb * f["kn"])
        dGf = jnp.broadcast_to(dG, (nbat, CHUNK, 128)) + jnp.where(last_row, dG_last, 0.0)
        dg = _bmm(jnp.broadcast_to(triu, (nbat, CHUNK, CHUNK)), dGf, HI)
        dbraw = dbeta * f["beta"][:, :, :1] * (1.0 - f["beta"][:, :, :1])
        daraw = dg * (-f["ea"]) * _sigmoid(f["sp_arg"])
        both = lambda t: jnp.sum(jnp.sum(t, axis=1, keepdims=True), axis=0)
        dgg = dg * f["g"]
        dsm = jnp.zeros((nc, CHUNK, SMALL), F32)
        for i in range(hb):
            mine = slice(i * nc, (i + 1) * nc)
            dsm += (jnp.where(lane == GLA_RANK + h0 + i, dbraw[mine], 0.0)
                    + jnp.where(lane == GLA_RANK + DN_H + h0 + i, daraw[mine], 0.0))
            dpar = jnp.where(lane[:1] == 0, both(dgg[mine]), jnp.where(lane[:1] == 1, both(daraw[mine]), 0.0))
            dpar_ref[i] += jnp.broadcast_to(dpar, (8, 128))
        dsm_ref[0] = dsm.reshape(blk, SMALL)
        dqh = dqn * (DN_DK ** -0.5)
        dq = f["rq"] * (dqh - f["qh"] * rsum(dqh * f["qh"]))
        dk = f["rk"] * (dkn - f["kn"] * rsum(dkn * f["kn"]))
        dsilu = lambda x, s: s * (1.0 + x * (1.0 - s))
        dcq_ref[...] = _b(_heads_last(dq * dsilu(f["cq"], f["sq"]), nc))
        dck_ref[...] = _b(_heads_last(dk * dsilu(f["ck"], f["sk"]), nc))
        dcv_ref[...] = _b(_heads_last(dv * dsilu(f["cv"], f["sv"]), nc))

    r = lambda j: nb - 1 - j
    out_blk = pl.BlockSpec((blk, hb * DN_DK), lambda h, j: (r(j), h))
    grid = (DN_H // hb, nb)
    return pl.pallas_call(
        _carry(carry, body, 10, 5, grid), name=name, grid=grid,
        in_specs=_gdn_specs(blk, hb, r) + [pl.BlockSpec((hb, nc, DN_DK, DN_DV), lambda h, j: (h, r(j), 0, 0)),
                                          pl.BlockSpec((blk, hb * DN_DV), lambda h, j: (r(j), h))]
        + _gdn_solved_specs(blk, hb, r) + ex_in,
        out_specs=[out_blk, out_blk, out_blk, pl.BlockSpec((1, blk, SMALL), lambda h, j: (h, r(j), 0)),
                   pl.BlockSpec((hb, 8, 128), lambda h, j: (h, 0, 0))] + ex_out,
        out_shape=[jax.ShapeDtypeStruct((T, DN_H * DN_DK), BF16)] * 3 + [
            jax.ShapeDtypeStruct((DN_H // hb, T, SMALL), F32), jax.ShapeDtypeStruct((DN_H, 8, 128), F32)] + ex_shape,
        scratch_shapes=[pltpu.VMEM((hb, DN_DK, DN_DV), F32)] + ex_sems,
        compiler_params=_cp("arbitrary", "arbitrary"),
    )(conv, conv, conv, psmall, par, states, do, *solved, *ex_args)


def _head_norm(o, w, dv):
    outs, rs = [], []
    for i in range(o.shape[1] // dv):
        oh = o[:, i * dv:(i + 1) * dv]
        r = lax.rsqrt(jnp.mean(oh * oh, axis=-1, keepdims=True) + EPS)
        outs.append(oh * r)
        rs.append(r)
    return outs, rs


def _merge_specs(tm):
    col = lambda c: pl.BlockSpec((tm, D), lambda i: (i, c))
    return [col(0), col(0), col(2), col(6), col(7), col(8),
            pl.BlockSpec((1, GLA_DV), lambda i: (0, 0)), pl.BlockSpec((1, DN_DV), lambda i: (0, 0)),
            pl.BlockSpec((D, D), lambda i: (0, 0))]


def _merge_fwd(h, oa, ob, pbig, gla_hn, dn_hn, wout, name):
    T = h.shape[0]
    tm = min(ROW_BLK, T)

    def body(h_ref, oa_ref, ob_ref, gr_ref, dg_ref, ma_ref, mb_ref, wa_ref, wb_ref, wo_ref, ho_ref, y_ref):
        na, _ = _head_norm(oa_ref[...].astype(F32), wa_ref[...], GLA_DV)
        nbs, _ = _head_norm(ob_ref[...].astype(F32), wb_ref[...], DN_DV)
        hna = jnp.concatenate([t * wa_ref[...] for t in na], axis=1)
        hnb = jnp.concatenate([t * wb_ref[...] for t in nbs], axis=1)
        gr = gr_ref[...].astype(F32)
        dg = dg_ref[...].astype(F32)
        y = (_sigmoid(ma_ref[...].astype(F32)) * hna * (gr * _sigmoid(gr))
             + _sigmoid(mb_ref[...].astype(F32)) * hnb * (dg * _sigmoid(dg)))
        yb = _b(y)
        y_ref[...] = yb
        ho_ref[...] = h_ref[...] + _dot(yb, wo_ref[...])

    row = pl.BlockSpec((tm, D), lambda i: (i, 0))
    return pl.pallas_call(
        body, name=name, grid=(T // tm,),
        in_specs=[row] + _merge_specs(tm),
        out_specs=[row, row],
        out_shape=[jax.ShapeDtypeStruct((T, D), F32), jax.ShapeDtypeStruct((T, D), BF16)],
        compiler_params=_cp("arbitrary"),
    )(h, oa, ob, pbig, pbig, pbig, pbig, gla_hn, dn_hn, wout)


def _merge_bwd(dh, oa, ob, pbig, gla_hn, dn_hn, wout, name, carry=None):
    T = dh.shape[0]
    tm = min(ROW_BLK, T)
    ex_in, ex_args, ex_out, ex_shape, ex_sems = _carry_specs(carry)

    def branch(dy, o_ref, w_ref, gate_ref, m_ref, dv):
        w = w_ref[...]
        ohat, rs = _head_norm(o_ref[...].astype(F32), w, dv)
        gate = gate_ref[...].astype(F32)
        m = m_ref[...].astype(F32)
        sgate, sm = _sigmoid(gate), _sigmoid(m)
        silu = gate * sgate
        ohat_all = jnp.concatenate(ohat, axis=1)
        hn = jnp.concatenate([t * w for t in ohat], axis=1)
        d_on = dy * sm
        d_m = dy * hn * silu * sm * (1.0 - sm)
        d_hn = d_on * silu
        d_gate = d_on * hn * (sgate * (1.0 + gate * (1.0 - sgate)))
        dw = jnp.zeros((1, dv), F32)
        d_o = []
        for i, (oh, r) in enumerate(zip(ohat, rs)):
            dhn = d_hn[:, i * dv:(i + 1) * dv]
            dw += jnp.sum(dhn * oh, axis=0, keepdims=True)
            dohat = dhn * w
            d_o.append(r * (dohat - oh * jnp.mean(dohat * oh, axis=-1, keepdims=True)))
        return jnp.concatenate(d_o, axis=1), d_gate, d_m, dw

    def body(dh_ref, oa_ref, ob_ref, gr_ref, dg_ref, ma_ref, mb_ref, wa_ref, wb_ref, wo_ref,
             doa_ref, dob_ref, dgr_ref, ddg_ref, dma_ref, dmb_ref, dwa_ref, dwb_ref, dhb_ref):
        @pl.when(pl.program_id(0) == 0)
        def _():
            dwa_ref[...] = jnp.zeros_like(dwa_ref)
            dwb_ref[...] = jnp.zeros_like(dwb_ref)

        dhb = _b(dh_ref[...])
        dhb_ref[...] = dhb
        dy = _dot_nt(dhb, wo_ref[...])
        d_oa, d_gr, d_ma, dwa = branch(dy, oa_ref, wa_ref, gr_ref, ma_ref, GLA_DV)
        d_ob, d_dg, d_mb, dwb = branch(dy, ob_ref, wb_ref, dg_ref, mb_ref, DN_DV)
        doa_ref[...] = _b(d_oa)
        dob_ref[...] = _b(d_ob)
        dgr_ref[...] = _b(d_gr)
        ddg_ref[...] = _b(d_dg)
        dma_ref[...] = _b(d_ma)
        dmb_ref[...] = _b(d_mb)
        dwa_ref[...] += dwa
        dwb_ref[...] += dwb

    row = pl.BlockSpec((tm, D), lambda i: (i, 0))
    b16 = jax.ShapeDtypeStruct((T, D), BF16)
    return pl.pallas_call(
        _carry(carry, body, 10, 9, (T // tm,)), name=name, grid=(T // tm,),
        in_specs=[row] + _merge_specs(tm) + ex_in,
        out_specs=[row] * 6 + [pl.BlockSpec((1, GLA_DV), lambda i: (0, 0)), pl.BlockSpec((1, DN_DV), lambda i: (0, 0)), row]
        + ex_out,
        out_shape=[b16, b16, b16, b16, b16, b16, jax.ShapeDtypeStruct((1, GLA_DV), F32),
                   jax.ShapeDtypeStruct((1, DN_DV), F32), b16] + ex_shape,
        scratch_shapes=ex_sems,
        compiler_params=_cp("arbitrary"),
    )(dh, oa, ob, pbig, pbig, pbig, pbig, gla_hn, dn_hn, wout, *ex_args)


def _adamw(w, g, m, v, rows, name):
    R, C = w.shape
    rows = min(rows, R)
    c1 = 1.0 - ADAM_B1 ** ADAM_STEP
    c2 = 1.0 - ADAM_B2 ** ADAM_STEP

    def body(w_ref, g_ref, m_ref, v_ref, d_ref, mo_ref, vo_ref):
        g_ = g_ref[...]
        m_ = ADAM_B1 * m_ref[...] + (1.0 - ADAM_B1) * g_
        v_ = ADAM_B2 * v_ref[...] + (1.0 - ADAM_B2) * (g_ * g_)
        mo_ref[...] = m_
        vo_ref[...] = v_
        d_ref[...] = -ADAM_LR * ((m_ / c1) / (jnp.sqrt(v_ / c2) + ADAM_EPS) + ADAM_WD * w_ref[...])

    blk = pl.BlockSpec((rows, C), lambda i: (i, 0))
    shp = jax.ShapeDtypeStruct((R, C), F32)
    return pl.pallas_call(
        body, name=name, grid=(R // rows,),
        in_specs=[blk] * 4, out_specs=[blk] * 3, out_shape=[shp] * 3,
        compiler_params=_cp("parallel"),
    )(w, g, m, v)


def _me():
    return lax.axis_index("x"), lax.axis_index("y"), lax.axis_index("c")


def _other_chips(x, y):
    return [(1 - x, y), (x, 1 - y), (1 - x, 1 - y)]


def _half_rows(ref, hf):
    half = ref.shape[-2] // 2
    rows = pl.ds(pl.multiple_of(hf * half, 16), half)
    return ref.at[rows, :] if len(ref.shape) == 2 else ref.at[:, rows, :]


class _GatherBig:
    def __init__(self, big):
        self.arrays = list(big)
        self.out_shape = [jax.ShapeDtypeStruct((N_SHARD,) + w.shape, w.dtype) for w in big]
        self.n_sem = 7 * len(big)

    @staticmethod
    def _copy(sems, k, src, dst, to):
        return pltpu.make_async_remote_copy(src_ref=src, dst_ref=dst, send_sem=sems[0].at[k], recv_sem=sems[1].at[k],
                                            device_id=to, device_id_type=MESH)

    def start(self, ins, outs, *sems):
        x, y, c = _me()
        mine = 2 * x + y
        for i, (w_ref, o_ref) in enumerate(zip(ins, outs)):
            self._copy(sems, 7 * i + 6, w_ref, o_ref.at[mine], (x, y, 1 - c)).start()
            for j, chip in enumerate(_other_chips(x, y)):
                self._copy(sems, 7 * i + j, _half_rows(w_ref, c), _half_rows(o_ref.at[mine], c), (*chip, c)).start()

    def relay(self, ins, outs, *sems):
        x, y, c = _me()
        for i, o_ref in enumerate(outs):
            for j, chip in enumerate(_other_chips(x, y)):
                landed = _half_rows(o_ref.at[2 * chip[0] + chip[1]], c)
                self._copy(sems, 7 * i + j, landed, landed, (x, y, c)).wait_recv()
                self._copy(sems, 7 * i + 3 + j, landed, landed, (x, y, 1 - c)).start()

    def finish(self, ins, outs, *sems):
        x, y, c = _me()
        me, sibling = (x, y, c), (x, y, 1 - c)
        chips = _other_chips(x, y)
        slot = lambda chip: 2 * chip[0] + chip[1]
        for i, (w_ref, o_ref) in enumerate(zip(ins, outs)):
            for j, chip in enumerate(chips):
                passed = _half_rows(o_ref.at[slot(chip)], 1 - c)
                self._copy(sems, 7 * i + 3 + j, passed, passed, me).wait_recv()
            self._copy(sems, 7 * i + 6, o_ref.at[slot((x, y))], o_ref.at[slot((x, y))], me).wait_recv()
        for i, (w_ref, o_ref) in enumerate(zip(ins, outs)):
            self._copy(sems, 7 * i + 6, w_ref, o_ref.at[slot((x, y))], sibling).wait_send()
            for j, chip in enumerate(chips):
                self._copy(sems, 7 * i + j, _half_rows(w_ref, c), _half_rows(o_ref.at[slot((x, y))], c),
                           (*chip, c)).wait_send()
                landed = _half_rows(o_ref.at[slot(chip)], c)
                self._copy(sems, 7 * i + 3 + j, landed, landed, sibling).wait_send()


class _SiblingExchange:
    def __init__(self, gs):
        self.arrays = list(gs)
        self.out_shape = [jax.ShapeDtypeStruct((g.shape[0], g.shape[1] // 2, g.shape[2]), g.dtype) for g in gs]
        self.n_sem = len(gs)

    def _copies(self, ins, outs, send_sems, recv_sems):
        x, y, c = _me()
        return [pltpu.make_async_remote_copy(src_ref=_half_rows(ins[i], 1 - c), dst_ref=outs[i],
                                             send_sem=send_sems.at[i], recv_sem=recv_sems.at[i],
                                             device_id=(x, y, 1 - c), device_id_type=MESH) for i in range(len(ins))]

    def start(self, ins, outs, *sems):
        for cp in self._copies(ins, outs, *sems):
            cp.start()

    def finish(self, ins, outs, *sems):
        for cp in self._copies(ins, outs, *sems):
            cp.wait()


class _ChipsExchange:
    def __init__(self, pbs):
        self.arrays = list(pbs)
        self.out_shape = [jax.ShapeDtypeStruct((3,) + p.shape[1:], p.dtype) for p in pbs]
        self.n_sem = 3 * len(pbs)

    def _copies(self, ins, outs, send_sems, recv_sems):
        x, y, c = _me()
        return [pltpu.make_async_remote_copy(src_ref=ins[i].at[2 * chip[0] + chip[1]], dst_ref=outs[i].at[j],
                                             send_sem=send_sems.at[3 * i + j], recv_sem=recv_sems.at[3 * i + j],
                                             device_id=(*chip, c), device_id_type=MESH)
                for i in range(len(ins)) for j, chip in enumerate(_other_chips(x, y))]

    def start(self, ins, outs, *sems):
        for cp in self._copies(ins, outs, *sems):
            cp.start()

    def finish(self, ins, outs, *sems):
        for cp in self._copies(ins, outs, *sems):
            cp.wait()


def _carry(ex, body, n_in, n_out, grid):
    if ex is None:
        return body
    ni, no = len(ex.arrays), len(ex.out_shape)

    def carried(*refs):
        ins, ex_in = refs[:n_in], refs[n_in:n_in + ni]
        outs, ex_out = refs[n_in + ni:n_in + ni + n_out], refs[n_in + ni + n_out:n_in + ni + n_out + no]
        scratch, sems = refs[n_in + ni + n_out + no:-2], refs[-2:]
        step = functools.reduce(lambda acc, a: acc * grid[a] + pl.program_id(a), range(len(grid)), 0)
        steps = math.prod(grid)

        @pl.when(step == 0)
        def _():
            ex.start(ex_in, ex_out, *sems)

        body(*ins, *outs, *scratch)

        if hasattr(ex, "relay"):
            @pl.when(step == (3 * steps) // 4)
            def _():
                ex.relay(ex_in, ex_out, *sems)

        @pl.when(step == steps - 1)
        def _():
            ex.finish(ex_in, ex_out, *sems)

    return carried


def _carry_specs(ex):
    if ex is None:
        return [], [], [], [], []
    sems = [pltpu.SemaphoreType.DMA((ex.n_sem,)), pltpu.SemaphoreType.DMA((ex.n_sem,))]
    return [ANY] * len(ex.arrays), ex.arrays, [ANY] * len(ex.out_shape), ex.out_shape, sems


def _gather_weights(big, small, name):
    nbig, nsm = len(big), len(small)
    n = nbig + nsm
    own_sem = 6 * nbig + 3 * nsm

    def body(*refs):
        ins, outs = refs[:n], refs[n:2 * n]
        send_sems, recv_sems = refs[2 * n:]
        x, y, c = _me()
        sibling = (x, y, 1 - c)
        chips = _other_chips(x, y)
        slot = lambda chip: 2 * chip[0] + chip[1]

        def copy(k, src, dst, to):
            return pltpu.make_async_remote_copy(src_ref=src, dst_ref=dst, send_sem=send_sems.at[k],
                                                recv_sem=recv_sems.at[k], device_id=to, device_id_type=MESH)

        sent = []
        for i in range(nbig):
            sent.append(copy(own_sem + i, ins[i], outs[i].at[slot((x, y))], sibling))
            sent[-1].start()
            for j, chip in enumerate(chips):
                sent.append(copy(6 * i + j, _half_rows(ins[i], c), _half_rows(outs[i].at[slot((x, y))], c), (*chip, c)))
                sent[-1].start()
        for t in range(nsm):
            w_ref, o_ref = ins[nbig + t], outs[nbig + t]
            o_ref[slot((x, y))] = w_ref[...]
            for j, chip in enumerate(chips):
                sent.append(copy(6 * nbig + 3 * t + j, w_ref, o_ref.at[slot((x, y))], (*chip, c)))
                sent[-1].start()
        for i in range(nbig):
            for j, chip in enumerate(chips):
                landed = _half_rows(outs[i].at[slot(chip)], c)
                copy(6 * i + j, landed, landed, (x, y, c)).wait_recv()
                sent.append(copy(6 * i + 3 + j, landed, landed, sibling))
                sent[-1].start()
        for t in range(nsm):
            for j, chip in enumerate(chips):
                landed = outs[nbig + t].at[slot(chip)]
                copy(6 * nbig + 3 * t + j, landed, landed, (x, y, c)).wait_recv()
        for i in range(nbig):
            for j, chip in enumerate(chips):
                passed = _half_rows(outs[i].at[slot(chip)], 1 - c)
                copy(6 * i + 3 + j, passed, passed, (x, y, c)).wait_recv()
        for i in range(nbig):
            mine = outs[i].at[slot((x, y))]
            copy(own_sem + i, mine, mine, (x, y, c)).wait_recv()
        for cp in sent:
            cp.wait_send()

    vm = pl.BlockSpec(memory_space=pltpu.VMEM)
    nsem = own_sem + nbig
    return pl.pallas_call(
        body, name=name, in_specs=[ANY] * nbig + [vm] * nsm, out_specs=[ANY] * nbig + [vm] * nsm,
        out_shape=[jax.ShapeDtypeStruct((N_SHARD,) + w.shape, w.dtype) for w in list(big) + list(small)],
        scratch_shapes=[pltpu.SemaphoreType.DMA((nsem,)), pltpu.SemaphoreType.DMA((nsem,))],
        compiler_params=pltpu.CompilerParams(has_side_effects=True),
    )(*big, *small)


def _rs_sibling(gs, name):
    n = len(gs)

    def body(*refs):
        send_sems, recv_sems = refs[2 * n:]
        x, y, c = _me()
        cps = [pltpu.make_async_remote_copy(src_ref=_half_rows(refs[i], 1 - c), dst_ref=refs[n + i],
                                            send_sem=send_sems.at[i], recv_sem=recv_sems.at[i],
                                            device_id=(x, y, 1 - c), device_id_type=MESH) for i in range(n)]
        for cp in cps:
            cp.start()
        for cp in cps:
            cp.wait()

    return pl.pallas_call(
        body, name=name, in_specs=[ANY] * n, out_specs=[ANY] * n,
        out_shape=[jax.ShapeDtypeStruct((g.shape[0], g.shape[1] // 2, g.shape[2]), g.dtype) for g in gs],
        scratch_shapes=[pltpu.SemaphoreType.DMA((n,)), pltpu.SemaphoreType.DMA((n,))],
        compiler_params=pltpu.CompilerParams(has_side_effects=True),
    )(*gs)


def _add_pair(g, other, where, name):
    ns, a, b = g.shape
    half = a // 2

    def body(w_ref, g_ref, o_ref, pb_ref, own_ref):
        t = g_ref[0].astype(F32) + o_ref[0].astype(F32)
        pb_ref[0] = _b(t)

        @pl.when(pl.program_id(0) == w_ref[1])
        def _():
            own_ref[...] = t

    return pl.pallas_call(
        body, name=name,
        grid_spec=pltpu.PrefetchScalarGridSpec(
            num_scalar_prefetch=1, grid=(ns,),
            in_specs=[pl.BlockSpec((1, half, b), lambda s, w: (s, w[0], 0)), pl.BlockSpec((1, half, b), lambda s, w: (s, 0, 0))],
            out_specs=[pl.BlockSpec((1, half, b), lambda s, w: (s, 0, 0)), pl.BlockSpec((half, b), lambda s, w: (0, 0))]),
        out_shape=[jax.ShapeDtypeStruct((ns, half, b), BF16), jax.ShapeDtypeStruct((half, b), F32)],
        compiler_params=_cp("arbitrary"),
    )(where, g, other)


def _add_four(own, got, name):
    rows, cols = own.shape
    rb = rows // 2

    def body(a_ref, b_ref, o_ref):
        o_ref[...] = ((a_ref[...] + b_ref[0].astype(F32)) + b_ref[1].astype(F32)) + b_ref[2].astype(F32)

    return pl.pallas_call(
        body, name=name, grid=(rows // rb,),
        in_specs=[pl.BlockSpec((rb, cols), lambda i: (i, 0)), pl.BlockSpec((3, rb, cols), lambda i: (0, i, 0))],
        out_specs=pl.BlockSpec((rb, cols), lambda i: (i, 0)),
        out_shape=jax.ShapeDtypeStruct((rows, cols), F32),
        compiler_params=_cp("parallel"),
    )(own, got)


def _rs_swap(halves, name):
    n = len(halves)

    def body(*refs):
        send_sems, recv_sems = refs[2 * n:]
        x, y, c = _me()
        cps = [pltpu.make_async_remote_copy(src_ref=refs[i], dst_ref=refs[n + i], send_sem=send_sems.at[i],
                                            recv_sem=recv_sems.at[i], device_id=(x, y, 1 - c), device_id_type=MESH)
               for i in range(n)]
        for cp in cps:
            cp.start()
        for cp in cps:
            cp.wait()

    return pl.pallas_call(
        body, name=name, in_specs=[ANY] * n, out_specs=[ANY] * n,
        out_shape=[jax.ShapeDtypeStruct(h.shape, h.dtype) for h in halves],
        scratch_shapes=[pltpu.SemaphoreType.DMA((n,)), pltpu.SemaphoreType.DMA((n,))],
        compiler_params=pltpu.CompilerParams(has_side_effects=True),
    )(*halves)


def _adamw_halves(w, own, got, m, v, rows, name):
    a, b = w.shape
    nblk = a // 2 // rows
    c1 = 1.0 - ADAM_B1 ** ADAM_STEP
    c2 = 1.0 - ADAM_B2 ** ADAM_STEP

    def body(w_ref, own_ref, got_ref, m_ref, v_ref, g_ref, d_ref, mo_ref, vo_ref):
        g_ = jnp.where(pl.program_id(0) == lax.axis_index("c"), own_ref[...], got_ref[...])
        g_ref[...] = g_
        m_ = ADAM_B1 * m_ref[...] + (1.0 - ADAM_B1) * g_
        v_ = ADAM_B2 * v_ref[...] + (1.0 - ADAM_B2) * (g_ * g_)
        mo_ref[...] = m_
        vo_ref[...] = v_
        d_ref[...] = -ADAM_LR * ((m_ / c1) / (jnp.sqrt(v_ / c2) + ADAM_EPS) + ADAM_WD * w_ref[...])

    whole = pl.BlockSpec((rows, b), lambda h, i: (h * nblk + i, 0))
    part = pl.BlockSpec((rows, b), lambda h, i: (i, 0))
    shp = jax.ShapeDtypeStruct((a, b), F32)
    return pl.pallas_call(
        body, name=name, grid=(2, nblk),
        in_specs=[whole, part, part, whole, whole], out_specs=[whole] * 4, out_shape=[shp] * 4,
        compiler_params=_cp("parallel", "parallel"),
    )(w, own, got, m, v)


def _allsum_small(vec, name):
    def body(v_ref, o_ref, buf_ref, send_sems, recv_sems):
        x, y, c = _me()
        me = 4 * x + 2 * y + c
        buf_ref[me] = v_ref[...]
        cps = []
        for k in range(1, 8):
            peer = (x ^ (k >> 2), y ^ ((k >> 1) & 1), c ^ (k & 1))
            cps.append(pltpu.make_async_remote_copy(src_ref=v_ref, dst_ref=buf_ref.at[me],
                                                    send_sem=send_sems.at[k - 1], recv_sem=recv_sems.at[k - 1],
                                                    device_id=peer, device_id_type=MESH))
        for cp in cps:
            cp.start()
        for k in range(1, 8):
            peer_idx = me ^ k
            pltpu.make_async_remote_copy(src_ref=v_ref, dst_ref=buf_ref.at[peer_idx],
                                         send_sem=send_sems.at[k - 1], recv_sem=recv_sems.at[k - 1],
                                         device_id=(x, y, c), device_id_type=MESH).wait_recv()
        for cp in cps:
            cp.wait_send()
        acc = buf_ref[0]
        for d in range(1, 8):
            acc = acc + buf_ref[d]
        o_ref[...] = acc

    return pl.pallas_call(
        body, name=name,
        in_specs=[pl.BlockSpec(memory_space=pltpu.VMEM)], out_specs=pl.BlockSpec(memory_space=pltpu.VMEM),
        out_shape=jax.ShapeDtypeStruct(vec.shape, F32),
        scratch_shapes=[pltpu.VMEM((8,) + vec.shape, F32), pltpu.SemaphoreType.DMA((7,)), pltpu.SemaphoreType.DMA((7,))],
        compiler_params=pltpu.CompilerParams(has_side_effects=True),
    )(vec)


BIG = ("ffn1_w_gate", "ffn1_w_up", "ffn1_w_down", "w_in", "w_out", "ffn2_w_gate", "ffn2_w_up", "ffn2_w_down")
TINY = ("w_gla_gate", "conv_w")
SHARDED = BIG + TINY


def _join_cols(w4):
    return jnp.transpose(w4, (1, 0, 2)).reshape(w4.shape[1], N_SHARD * w4.shape[2])


def _cut_cols(w):
    return jnp.transpose(w.reshape(w.shape[0], N_SHARD, w.shape[1] // N_SHARD), (1, 0, 2))


def _split_w_in(w):
    o = IN_OFF
    big = jnp.concatenate([w[:, :o[4]], w[:, o[5]:o[9]], w[:, o[11]:]], axis=1)
    small = jnp.concatenate([w[:, o[4]:o[5]], w[:, o[9]:o[11]], jnp.zeros((w.shape[0], SMALL - 32), w.dtype)], axis=1)
    return big, small


def _merge_w_in(big, small):
    return jnp.concatenate([big[:, :3072], small[:, :16], big[:, 3072:7168], small[:, 16:32], big[:, 7168:]], axis=1)


class _Comm:
    def __init__(self, where, rest_shards):
        self.where = where
        self.w_in = _GatherBig(rest_shards[:1])
        self.later = _GatherBig(rest_shards[1:])
        self.pairs, self.got = {}, {}

    @staticmethod
    def w_in_weights(gathered):
        return dict(zip(("w_in_big", "w_in_small"), _split_w_in(_join_cols(gathered[0]))))

    @staticmethod
    def later_weights(gathered):
        W = dict(zip(BIG[4:], gathered))
        W["w_out"] = W["w_out"].reshape(D, D)
        return W

    def pair(self, names, grads, from_sibling):
        for n, g, o in zip(names, grads, from_sibling):
            self.pairs[n] = _add_pair(g, o, self.where, "rs_pair_" + n)
        return _ChipsExchange([self.pairs[n][0] for n in names])

    def begin(self, names, grads):
        return self.pair(names, grads, _rs_sibling(grads, "rs_sibling_" + names[0]))

    def landed(self, names, outs):
        self.got.update(zip(names, outs))


def _local_step(x, target, W, P, comm=None):
    wgate_pad = jnp.zeros((SMALL, GLA_H * GLA_DK), F32).at[:GLA_RANK].set(P["w_gla_gate"])
    cw8 = jnp.zeros((8, CONV_C), F32).at[:CONV_K].set(P["conv_w"])
    par = jnp.zeros((DN_H, 8, 128), F32)
    par = par.at[:, 0, :].set(jnp.broadcast_to(P["dn_a_log"].reshape(DN_H, 1), (DN_H, 128)))
    par = par.at[:, 1, :].set(jnp.broadcast_to(P["dn_dt_bias"].reshape(DN_H, 1), (DN_H, 128)))

    h1, n1, g1, u1, *got = _ffn_fwd(x, P["ffn1_norm"], W["ffn1_w_gate"], W["ffn1_w_up"], W["ffn1_w_down"], "ffn1_fwd",
                                    carry=comm.w_in if comm else None)
    if comm:
        W = dict(W, **comm.w_in_weights(got))
    wbig, wsmall = W["w_in_big"], W["w_in_small"]
    pbig, psmall, n2, qk, *got = _norm_proj(h1, P["mix_norm"], wbig, wsmall, "mix_proj",
                                            carry=comm.later if comm else None)
    if comm:
        W = dict(W, **comm.later_weights(got))
    oa, sa = _gla_fwd(pbig, qk, psmall, wgate_pad, P["b_gla_gate"], "gla_fwd")
    conv = _conv_fwd(pbig, cw8, "conv_fwd")
    ob, sb, *solved = _gdn_fwd(conv, psmall, par, "gdn_fwd")
    h2, yb = _merge_fwd(h1, oa, ob, pbig, P["gla_head_norm"], P["dn_head_norm"], W["w_out"], "merge_fwd")
    dh3, n3, g3, u3, loss, d_final = _ffn_fwd(h2, P["ffn2_norm"], W["ffn2_w_gate"], W["ffn2_w_up"], W["ffn2_w_down"],
                                              "ffn2_fwd", head=(P["final_norm"], target))

    gw, gs = {}, {"final_norm": d_final}

    def ffn_grads(tag, dh, h, n, g, u, before=None, later=False):
        names = tuple(tag + s for s in ("_w_gate", "_w_up", "_w_down"))
        dg, du, act, dfb, *landed = _ffn_bwd_hidden(dh, g, u, W[names[2]], tag + "_bwd_hidden", carry=before)
        gw[names[0]] = _mm_tn(n, dg, D, FF_CUT, tag + "_dwg")
        gw[names[1]] = _mm_tn(n, du, D, FF_CUT, tag + "_dwu")
        gw[names[2]] = _mm_tn(act, dfb, FF_CUT, D, tag + "_dwd")
        mine = [gw[n] for n in names]
        ex = None if not comm else _SiblingExchange(mine) if later else comm.begin(names, mine)
        dx, gs[tag + "_norm"], *own = _ffn_bwd_input(dh, h, P[tag + "_norm"], dg, du, W[names[0]], W[names[1]],
                                                     tag + "_bwd_input", carry=ex)
        if comm and not later:
            comm.landed(names, own)
        return dx, landed, own

    second = ("ffn2_w_gate", "ffn2_w_up", "ffn2_w_down")
    dh2, _, swapped = ffn_grads("ffn2", dh3, h2, n3, g3, u3, later=True)
    d_oa, d_ob, d_gr, d_dgate, d_ma, d_mb, gs["gla_head_norm"], gs["dn_head_norm"], dh2b, *landed = _merge_bwd(
        dh2, oa, ob, pbig, P["gla_head_norm"], P["dn_head_norm"], W["w_out"], "merge_bwd",
        carry=comm.pair(second, [gw[n] for n in second], swapped) if comm else None)
    if comm:
        comm.landed(second, landed)
    gw["w_out"] = _mm_tn(yb, dh2b, D, D, "dw_out").reshape(N_SHARD, D // N_SHARD, D)
    early = ("w_out",)
    d_gq, d_gk, d_gv, dpre, *swapped = _gla_bwd(pbig, qk, psmall, wgate_pad, P["b_gla_gate"], sa, d_oa, "gla_bwd",
                                                carry=_SiblingExchange([gw["w_out"]]) if comm else None)
    dcq, dck, dcv, dsm, dpar, *landed = _gdn_bwd(conv, psmall, par, sb, d_ob, solved, "gdn_bwd",
                                                 carry=comm.pair(early, [gw["w_out"]], swapped) if comm else None)
    if comm:
        comm.landed(early, landed)
    dsmall, dwgate, gs["b_gla_gate"] = _gla_gate_bwd(dpre, psmall, wgate_pad, dsm, "gla_gate_bwd")
    gs["w_gla_gate"] = dwgate[:GLA_RANK]
    d_x3, dcw = _conv_bwd(dcq, dck, dcv, pbig, cw8, "conv_bwd")
    gs["conv_w"] = dcw[:CONV_K]
    gs["dn_a_log"] = dpar[:, 0, 0].reshape(1, DN_H)
    gs["dn_dt_bias"] = dpar[:, 0, 1].reshape(1, DN_H)
    pieces = (d_gq, d_gk, d_gv, d_gr, d_x3, d_dgate, d_ma, d_mb)
    dh1, gs["mix_norm"] = _proj_bwd(dh2, h1, P["mix_norm"], pieces, dsmall, wbig, wsmall, "proj_bwd")
    dbig = jnp.concatenate([_mm_tn(n2, p, D, 1024, "dw_in_%d" % i) for i, p in enumerate(pieces)], axis=1)
    dsml = _mm_tn(n2, dsmall, D, SMALL, "dw_in_small")
    gw["w_in"] = _cut_cols(_merge_w_in(dbig, dsml))
    grad_x, landed, _ = ffn_grads("ffn1", dh1, x, n1, g1, u1,
                                  before=comm.begin(("w_in",), [gw["w_in"]]) if comm else None)
    if comm:
        comm.landed(("w_in",), landed)
    return loss, grad_x, gw, gs


SMALL_NAMES = ("ffn1_norm", "mix_norm", "ffn2_norm", "final_norm", "b_gla_gate", "gla_head_norm", "dn_head_norm",
               "dn_a_log", "dn_dt_bias")
ROW4 = (("b_gla_gate", 512), ("gla_head_norm", 256), ("dn_head_norm", 128), ("dn_a_log", 8), ("dn_dt_bias", 8))


def _pack_small(d, loss=None):
    row4 = [d[n].reshape(-1) for n, _ in ROW4]
    row4.append(jnp.zeros((1,), F32) if loss is None else loss.reshape(1))
    row4 = jnp.concatenate(row4)
    row4 = jnp.pad(row4, (0, D - row4.shape[0]))
    rows = [d[n].reshape(-1) for n in SMALL_NAMES[:4]] + [row4]
    return jnp.concatenate([jnp.stack(rows), jnp.zeros((3, D), F32)], axis=0)


def _unpack_small(a, like):
    out = {n: a[i].reshape(like[n].shape) for i, n in enumerate(SMALL_NAMES[:4])}
    off = 0
    for n, w in ROW4:
        out[n] = a[4, off:off + w].reshape(like[n].shape)
        off += w
    return out, a[4, off]


WEIGHT_ORDER = ("ffn1_norm", "ffn1_w_gate", "ffn1_w_up", "ffn1_w_down", "mix_norm", "w_in", "w_gla_gate", "b_gla_gate",
                "conv_w", "dn_a_log", "dn_dt_bias", "gla_head_norm", "dn_head_norm", "w_out", "ffn2_norm",
                "ffn2_w_gate", "ffn2_w_up", "ffn2_w_down", "final_norm")
ADAM_ROWS = {"ffn1_w_gate": 256, "ffn1_w_up": 256, "ffn1_w_down": 176, "w_in": 128, "w_gla_gate": 16, "conv_w": 4,
             "w_out": 64, "ffn2_w_gate": 256, "ffn2_w_up": 256, "ffn2_w_down": 176}


def kernel(x, ffn1_norm, ffn1_w_gate, ffn1_w_up, ffn1_w_down, mix_norm, w_in, w_gla_gate, b_gla_gate, conv_w, dn_a_log, dn_dt_bias, gla_head_norm, dn_head_norm, w_out, ffn2_norm, ffn2_w_gate, ffn2_w_up, ffn2_w_down, final_norm, loss_target, m_ffn1_norm, m_ffn1_w_gate, m_ffn1_w_up, m_ffn1_w_down, m_mix_norm, m_w_in, m_w_gla_gate, m_b_gla_gate, m_conv_w, m_dn_a_log, m_dn_dt_bias, m_gla_head_norm, m_dn_head_norm, m_w_out, m_ffn2_norm, m_ffn2_w_gate, m_ffn2_w_up, m_ffn2_w_down, m_final_norm, v_ffn1_norm, v_ffn1_w_gate, v_ffn1_w_up, v_ffn1_w_down, v_mix_norm, v_w_in, v_w_gla_gate, v_b_gla_gate, v_conv_w, v_dn_a_log, v_dn_dt_bias, v_gla_head_norm, v_dn_head_norm, v_w_out, v_ffn2_norm, v_ffn2_w_gate, v_ffn2_w_up, v_ffn2_w_down, v_final_norm):
    given = dict(locals())
    wts = {n: given[n] for n in WEIGHT_ORDER}
    moms = {n: given["m_" + n] for n in WEIGHT_ORDER}
    vars_ = {n: given["v_" + n] for n in WEIGHT_ORDER}
    two_d = lambda a: a.reshape(a.shape[-2], a.shape[-1]) if a.ndim == 3 else a.reshape(1, -1)
    shard = {n: two_d(wts[n]) for n in SHARDED}

    gathered = _gather_weights([shard[n].astype(BF16) for n in BIG[:3]], [shard[n] for n in TINY], "gather_first")
    W = dict(zip(BIG[:3], gathered))
    P = {n: two_d(wts[n]) for n in SMALL_NAMES}
    for n, g in zip(TINY, gathered[3:]):
        P[n] = _join_cols(g)

    my_slot = 2 * lax.axis_index("x") + lax.axis_index("y")
    where = jnp.stack([lax.axis_index("c"), my_slot]).astype(jnp.int32)
    comm = _Comm(where, [shard[n].astype(BF16) for n in BIG[3:]])
    loss, grad_x, gw, gs = _local_step(x[0], loss_target[0], W, P, comm)
    halves = [_add_four(comm.pairs[n][1], comm.got[n], "rs_four_" + n) for n in BIG]
    other_halves = _rs_swap(halves, "rs_swap")

    tiny_rows = jnp.concatenate([gs["w_gla_gate"].reshape(8, D), gs["conv_w"].reshape(12, D), jnp.zeros((4, D), F32)])
    all_sum = _allsum_small(jnp.concatenate([_pack_small(gs, loss[0, 0]), tiny_rows]), "allsum_small")
    small_sum = all_sum[:8]
    small_g, loss_total = _unpack_small(small_sum, P)

    grads, delta, new_m, new_v = {}, {}, {}, {}
    for n, own, got in zip(BIG, halves, other_halves):
        res = _adamw_halves(shard[n], own, got, two_d(moms[n]), two_d(vars_[n]), ADAM_ROWS[n], "adamw_" + n)
        grads[n], delta[n], new_m[n], new_v[n] = (t.reshape(wts[n].shape) for t in res)
    for n, rows in (("w_gla_gate", all_sum[8:16]), ("conv_w", all_sum[16:28])):
        cols = shard[n].shape[1]
        grads[n] = lax.dynamic_slice_in_dim(rows.reshape(shard[n].shape[0], N_SHARD * cols), my_slot * cols, cols, axis=1)
        d, m_, v_ = _adamw(shard[n], grads[n], two_d(moms[n]), two_d(vars_[n]), ADAM_ROWS[n], "adamw_" + n)
        delta[n], new_m[n], new_v[n] = (t.reshape(wts[n].shape) for t in (d, m_, v_))
    pk = lambda src: _pack_small({n: two_d(src[n]) for n in SMALL_NAMES})
    sd, sm_, sv_ = _adamw(pk(wts), small_sum, pk(moms), pk(vars_), 8, "adamw_small")
    for res, dst in ((sd, delta), (sm_, new_m), (sv_, new_v)):
        u, _ = _unpack_small(res, wts)
        dst.update(u)
    grad_w = {n: grads[n].reshape(wts[n].shape) for n in SHARDED}
    grad_w.update({n: small_g[n].reshape(wts[n].shape) for n in SMALL_NAMES})
    return (loss_total, grad_x[None], *[grad_w[n] for n in WEIGHT_ORDER], *[delta[n] for n in WEIGHT_ORDER],
            *[new_m[n] for n in WEIGHT_ORDER], *[new_v[n] for n in WEIGHT_ORDER])
```

```python
import functools
import math

import numpy as np
import jax
import jax.numpy as jnp
from jax import lax
from jax.experimental import pallas as pl
from jax.experimental.pallas import tpu as pltpu

F32 = jnp.float32
BF16 = jnp.bfloat16
HI = lax.Precision.HIGH
MESH = pl.DeviceIdType.MESH
ANY = pl.BlockSpec(memory_space=pl.ANY)

EPS = 1e-6
D = 1024
DFF = 2816
FFN_RES = 0.5
GLA_H, GLA_DK, GLA_DV, GLA_RANK, GLA_TAU = 4, 128, 256, 16, 16.0
DN_H, DN_DK, DN_DV = 8, 128, 128
CONV_K = 4
CHUNK = 64
N_SHARD = 4
FF_CUT = DFF // N_SHARD
ADAM_LR, ADAM_B1, ADAM_B2, ADAM_EPS, ADAM_WD, ADAM_STEP = 0.001, 0.9, 0.999, 1e-08, 0.01, 10

IN_SIZES = (512, 512, 1024, 1024, 16, 1024, 1024, 1024, 1024, 8, 8, 1024, 1024)
IN_OFF = tuple(int(v) for v in np.cumsum((0,) + IN_SIZES))
BIG_COLS = 9216
SMALL = 128
PIECES = (512, 512, 1024, 1024, 3072, 1024, 1024, 1024)

VMEM_LIMIT = 56 * 1024 * 1024
ROW_BLK = 256
BIG_ROW_BLK = 512
ATT_BLK = 512
GDN_BLK = 256
GDN_HEADS = 8


def _cp(*sem):
    return pltpu.CompilerParams(dimension_semantics=sem, vmem_limit_bytes=VMEM_LIMIT)


def _sigmoid(x):
    return 1.0 / (1.0 + jnp.exp(-x))


def _softplus(x):
    return jnp.maximum(x, 0.0) + jnp.log(1.0 + jnp.exp(-jnp.abs(x)))


def _log_sigmoid(x):
    return jnp.minimum(x, 0.0) - jnp.log(1.0 + jnp.exp(-jnp.abs(x)))


def _dot(a, b, prec=None):
    return jnp.dot(a, b, preferred_element_type=F32, precision=prec)


def _dot_nt(a, b, prec=None):
    return lax.dot_general(a, b, (((1,), (1,)), ((), ())), preferred_element_type=F32, precision=prec)


def _dot_tn(a, b, prec=None):
    return lax.dot_general(a, b, (((0,), (0,)), ((), ())), preferred_element_type=F32, precision=prec)


def _b(x):
    return x.astype(BF16)


def _iota2(n, m, axis):
    return lax.broadcasted_iota(jnp.int32, (n, m), axis)


def _load_weights(pairs, sem):
    copies = [pltpu.make_async_copy(s, d, sem.at[i]) for i, (s, d) in enumerate(pairs)]
    for c in copies:
        c.start()
    for c in copies:
        c.wait()


def _ffn_fwd(h, nw, wg, wu, wd, name, carry=None, head=None):
    T = h.shape[0]
    tm = min(BIG_ROW_BLK, T)
    ex_in, ex_args, ex_out, ex_shape, ex_sems = _carry_specs(carry)
    n_head = 2 if head else 0

    def body(*refs):
        h_ref, nw_ref, wg_hbm, wu_hbm, wd_hbm = refs[:5]
        ho_ref, n_ref, g_ref, u_ref = refs[5 + n_head:9 + n_head]
        wg_v, wu_v, wd_v, sem = refs[9 + 2 * n_head:]

        @pl.when(pl.program_id(0) == 0)
        def _():
            _load_weights(((wg_hbm, wg_v), (wu_hbm, wu_v), (wd_hbm, wd_v)), sem)

        x = h_ref[...]
        r = lax.rsqrt(jnp.mean(x * x, axis=-1, keepdims=True) + EPS)
        nb = _b((x * r) * nw_ref[...])
        n_ref[...] = nb
        acc = jnp.zeros((tm, D), F32)
        for s in range(N_SHARD):
            g = _dot(nb, wg_v[s])
            u = _dot(nb, wu_v[s])
            g_ref[s] = _b(g)
            u_ref[s] = _b(u)
            acc += _dot(_b(g * _sigmoid(g) * u), wd_v[s])
        out = x + FFN_RES * acc
        if not head:
            ho_ref[...] = out
            return
        fw_ref, t_ref = refs[5:7]
        loss_ref, dfw_ref = refs[11:13]

        @pl.when(pl.program_id(0) == 0)
        def _():
            loss_ref[...] = jnp.zeros_like(loss_ref)
            dfw_ref[...] = jnp.zeros_like(dfw_ref)

        w = fw_ref[...]
        r = lax.rsqrt(jnp.mean(out * out, axis=-1, keepdims=True) + EPS)
        xhat = out * r
        err = xhat * w - t_ref[...]
        loss_ref[...] += (0.5 / D) * jnp.sum(jnp.sum(err * err, axis=-1, keepdims=True), axis=0, keepdims=True)
        dout = err * (1.0 / D)
        dfw_ref[...] += jnp.sum(dout * xhat, axis=0, keepdims=True)
        dxhat = dout * w
        ho_ref[...] = r * (dxhat - xhat * jnp.mean(dxhat * xhat, axis=-1, keepdims=True))

    row = lambda w: pl.BlockSpec((tm, w), lambda i: (i, 0))
    one = pl.BlockSpec((1, D), lambda i: (0, 0))
    cut = pl.BlockSpec((N_SHARD, tm, FF_CUT), lambda i: (0, i, 0))
    head_out = [pl.BlockSpec((8, 128), lambda i: (0, 0)), one] if head else []
    head_shape = [jax.ShapeDtypeStruct((8, 128), F32), jax.ShapeDtypeStruct((1, D), F32)] if head else []
    return pl.pallas_call(
        _carry(carry, body, 5 + n_head, 4 + n_head, (T // tm,)), name=name, grid=(T // tm,),
        in_specs=[row(D), one, ANY, ANY, ANY] + ([one, row(D)] if head else []) + ex_in,
        out_specs=[row(D), row(D), cut, cut] + head_out + ex_out,
        out_shape=[jax.ShapeDtypeStruct((T, D), F32), jax.ShapeDtypeStruct((T, D), BF16),
                   jax.ShapeDtypeStruct((N_SHARD, T, FF_CUT), BF16),
                   jax.ShapeDtypeStruct((N_SHARD, T, FF_CUT), BF16)] + head_shape + ex_shape,
        scratch_shapes=[pltpu.VMEM((N_SHARD, D, FF_CUT), BF16), pltpu.VMEM((N_SHARD, D, FF_CUT), BF16),
                        pltpu.VMEM((N_SHARD, FF_CUT, D), BF16), pltpu.SemaphoreType.DMA((3,))] + ex_sems,
        compiler_params=_cp("arbitrary"),
    )(h, nw, wg, wu, wd, *(head or ()), *ex_args)


def _ffn_bwd_hidden(dh, g, u, wd, name, carry=None):
    T = dh.shape[0]
    tm = min(BIG_ROW_BLK, T)
    ex_in, ex_args, ex_out, ex_shape, ex_sems = _carry_specs(carry)

    def body(dh_ref, g_ref, u_ref, wd_hbm, dg_ref, du_ref, a_ref, df_ref, wd_v, sem):
        @pl.when(pl.program_id(0) == 0)
        def _():
            _load_weights(((wd_hbm, wd_v),), sem)

        dfb = _b(FFN_RES * dh_ref[...])
        df_ref[...] = dfb
        for s in range(N_SHARD):
            da = _dot_nt(dfb, wd_v[s])
            gg = g_ref[s].astype(F32)
            uu = u_ref[s].astype(F32)
            sg = _sigmoid(gg)
            silu = gg * sg
            a_ref[s] = _b(silu * uu)
            dg_ref[s] = _b(da * uu * (sg * (1.0 + gg * (1.0 - sg))))
            du_ref[s] = _b(da * silu)

    row = pl.BlockSpec((tm, D), lambda i: (i, 0))
    cut = pl.BlockSpec((N_SHARD, tm, FF_CUT), lambda i: (0, i, 0))
    cut_shape = jax.ShapeDtypeStruct((N_SHARD, T, FF_CUT), BF16)
    return pl.pallas_call(
        _carry(carry, body, 4, 4, (T // tm,)), name=name, grid=(T // tm,),
        in_specs=[row, cut, cut, ANY] + ex_in,
        out_specs=[cut, cut, cut, row] + ex_out,
        out_shape=[cut_shape, cut_shape, cut_shape, jax.ShapeDtypeStruct((T, D), BF16)] + ex_shape,
        scratch_shapes=[pltpu.VMEM((N_SHARD, FF_CUT, D), BF16), pltpu.SemaphoreType.DMA((1,))] + ex_sems,
        compiler_params=_cp("arbitrary"),
    )(dh, g, u, wd, *ex_args)


def _ffn_bwd_input(dh, h, nw, dg, du, wg, wu, name, carry=None):
    T = h.shape[0]
    tm = min(BIG_ROW_BLK, T)
    ex_in, ex_args, ex_out, ex_shape, ex_sems = _carry_specs(carry)

    def body(dh_ref, h_ref, nw_ref, dg_ref, du_ref, wg_hbm, wu_hbm, dx_ref, dnw_ref, wg_v, wu_v, sem):
        @pl.when(pl.program_id(0) == 0)
        def _():
            _load_weights(((wg_hbm, wg_v), (wu_hbm, wu_v)), sem)
            dnw_ref[...] = jnp.zeros_like(dnw_ref)

        dn = jnp.zeros((tm, D), F32)
        for s in range(N_SHARD):
            dn += _dot_nt(dg_ref[s], wg_v[s]) + _dot_nt(du_ref[s], wu_v[s])
        x = h_ref[...]
        r = lax.rsqrt(jnp.mean(x * x, axis=-1, keepdims=True) + EPS)
        xhat = x * r
        dnw_ref[...] += jnp.sum(dn * xhat, axis=0, keepdims=True)
        dxhat = dn * nw_ref[...]
        dx_ref[...] = dh_ref[...] + r * (dxhat - xhat * jnp.mean(dxhat * xhat, axis=-1, keepdims=True))

    row = pl.BlockSpec((tm, D), lambda i: (i, 0))
    one = pl.BlockSpec((1, D), lambda i: (0, 0))
    cut = pl.BlockSpec((N_SHARD, tm, FF_CUT), lambda i: (0, i, 0))
    return pl.pallas_call(
        _carry(carry, body, 7, 2, (T // tm,)), name=name, grid=(T // tm,),
        in_specs=[row, row, one, cut, cut, ANY, ANY] + ex_in,
        out_specs=[row, one] + ex_out,
        out_shape=[jax.ShapeDtypeStruct((T, D), F32), jax.ShapeDtypeStruct((1, D), F32)] + ex_shape,
        scratch_shapes=[pltpu.VMEM((N_SHARD, D, FF_CUT), BF16), pltpu.VMEM((N_SHARD, D, FF_CUT), BF16),
                        pltpu.SemaphoreType.DMA((2,))] + ex_sems,
        compiler_params=_cp("arbitrary"),
    )(dh, h, nw, dg, du, wg, wu, *ex_args)


def _mm_tn(a, b, bm, bn, name, out_dtype=BF16, tk=2048):
    cuts = a.shape[0] if a.ndim == 3 else (b.shape[0] if b.ndim == 3 else None)
    T, M = a.shape[-2:]
    N = b.shape[-1]
    tk = min(tk, T)
    bm, bn = min(bm, M), min(bn, N)
    nk = T // tk

    def body(a_ref, b_ref, o_ref, acc_ref):
        k = pl.program_id(3)

        @pl.when(k == 0)
        def _():
            acc_ref[...] = jnp.zeros_like(acc_ref)

        av = a_ref[0] if a.ndim == 3 else a_ref[...]
        bv = b_ref[0] if b.ndim == 3 else b_ref[...]
        acc_ref[...] += _dot_tn(_b(av), _b(bv))

        @pl.when(k == nk - 1)
        def _():
            res = acc_ref[...].astype(out_dtype)
            if cuts is None:
                o_ref[...] = res
            else:
                o_ref[0] = res

    a_spec = (pl.BlockSpec((1, tk, bm), lambda s, i, j, k: (s, k, i)) if a.ndim == 3
              else pl.BlockSpec((tk, bm), lambda s, i, j, k: (k, i)))
    b_spec = (pl.BlockSpec((1, tk, bn), lambda s, i, j, k: (s, k, j)) if b.ndim == 3
              else pl.BlockSpec((tk, bn), lambda s, i, j, k: (k, j)))
    if cuts is None:
        o_spec, o_shape = pl.BlockSpec((bm, bn), lambda s, i, j, k: (i, j)), (M, N)
    else:
        o_spec, o_shape = pl.BlockSpec((1, bm, bn), lambda s, i, j, k: (s, i, j)), (cuts, M, N)
    return pl.pallas_call(
        body, name=name, grid=(cuts or 1, M // bm, N // bn, nk),
        in_specs=[a_spec, b_spec], out_specs=o_spec,
        out_shape=jax.ShapeDtypeStruct(o_shape, out_dtype),
        scratch_shapes=[pltpu.VMEM((bm, bn), F32)],
        compiler_params=_cp("parallel", "parallel", "parallel", "arbitrary"),
    )(a, b)


def _norm_proj(h, nw, wbig, wsmall, name, carry=None):
    T = h.shape[0]
    tm = min(512, T)
    tn = 1536
    qk = 2 * GLA_H * GLA_DK
    ex_in, ex_args, ex_out, ex_shape, ex_sems = _carry_specs(carry)

    def body(h_ref, nw_ref, wb_hbm, ws_ref, pb_ref, ps_ref, n_ref, qk_ref, wb_v, sem):
        @pl.when(pl.program_id(0) == 0)
        def _():
            _load_weights(((wb_hbm, wb_v),), sem)

        x = h_ref[...]
        r = lax.rsqrt(jnp.mean(x * x, axis=-1, keepdims=True) + EPS)
        nb = _b((x * r) * nw_ref[...])
        n_ref[...] = nb
        ps_ref[...] = _dot(nb, ws_ref[...])
        for j in range(BIG_COLS // tn):
            part = _dot(nb, wb_v[:, j * tn:(j + 1) * tn])
            pb_ref[:, j * tn:(j + 1) * tn] = _b(part)
            if j == 0:
                qk_ref[...] = part[:, :qk]

    row = lambda w: pl.BlockSpec((tm, w), lambda i: (i, 0))
    return pl.pallas_call(
        _carry(carry, body, 4, 4, (T // tm,)), name=name, grid=(T // tm,),
        in_specs=[row(D), pl.BlockSpec((1, D), lambda i: (0, 0)), ANY, pl.BlockSpec((D, SMALL), lambda i: (0, 0))]
        + ex_in,
        out_specs=[row(BIG_COLS), row(SMALL), row(D), row(qk)] + ex_out,
        out_shape=[jax.ShapeDtypeStruct((T, BIG_COLS), BF16), jax.ShapeDtypeStruct((T, SMALL), F32),
                   jax.ShapeDtypeStruct((T, D), BF16), jax.ShapeDtypeStruct((T, qk), F32)] + ex_shape,
        scratch_shapes=[pltpu.VMEM((D, BIG_COLS), BF16), pltpu.SemaphoreType.DMA((1,))] + ex_sems,
        compiler_params=_cp("arbitrary"),
    )(h, nw, wbig, wsmall, *ex_args)


def _proj_bwd(dh, h, nw, pieces, dsmall, wbig, wsmall, name):
    T = h.shape[0]
    tm = min(BIG_ROW_BLK, T)
    offs = tuple(int(v) for v in np.cumsum((0,) + PIECES))

    def body(dh_ref, h_ref, nw_ref, *rest):
        p_refs = rest[:len(PIECES)]
        ds_ref, wb_hbm, ws_ref, dx_ref, dnw_ref, wb_v, sem = rest[len(PIECES):]

        @pl.when(pl.program_id(0) == 0)
        def _():
            _load_weights(((wb_hbm, wb_v),), sem)
            dnw_ref[...] = jnp.zeros_like(dnw_ref)

        dn = _dot_nt(_b(ds_ref[...]), ws_ref[...])
        for p_ref, lo, wdt in zip(p_refs, offs, PIECES):
            dn += _dot_nt(p_ref[...], wb_v[:, lo:lo + wdt])
        x = h_ref[...]
        r = lax.rsqrt(jnp.mean(x * x, axis=-1, keepdims=True) + EPS)
        xhat = x * r
        dnw_ref[...] += jnp.sum(dn * xhat, axis=0, keepdims=True)
        dxhat = dn * nw_ref[...]
        dx_ref[...] = dh_ref[...] + r * (dxhat - xhat * jnp.mean(dxhat * xhat, axis=-1, keepdims=True))

    row = lambda w: pl.BlockSpec((tm, w), lambda i: (i, 0))
    one = pl.BlockSpec((1, D), lambda i: (0, 0))
    return pl.pallas_call(
        body, name=name, grid=(T // tm,),
        in_specs=[row(D), row(D), one] + [row(w) for w in PIECES] + [row(SMALL), ANY, pl.BlockSpec((D, SMALL), lambda i: (0, 0))],
        out_specs=[row(D), one],
        out_shape=[jax.ShapeDtypeStruct((T, D), F32), jax.ShapeDtypeStruct((1, D), F32)],
        scratch_shapes=[pltpu.VMEM((D, BIG_COLS), BF16), pltpu.SemaphoreType.DMA((1,))],
        compiler_params=_cp("arbitrary"),
    )(dh, h, nw, *pieces, dsmall, wbig, wsmall)


def _gla_block(q_ref, k_ref, sm_ref, wg_ref, bg_ref, nc, tril):
    nbat = GLA_H * nc
    q = _heads_first(q_ref[...].astype(F32), nc, GLA_DK)
    k = _heads_first(k_ref[...].astype(F32), nc, GLA_DK)
    pre = _heads_first(_dot(sm_ref[...], wg_ref[...], HI) + bg_ref[...], nc, GLA_DK)
    la = _log_sigmoid(pre) * (1.0 / GLA_TAU)
    bc = _bmm(jnp.broadcast_to(tril, (nbat, CHUNK, CHUNK)), la, HI)
    bl = bc[:, CHUNK - 1:CHUNK, :]
    eb = jnp.exp(bc)
    enb = jnp.exp(-bc)
    ebl = jnp.exp(bl - bc)
    q_in = q * (GLA_DK ** -0.5) * eb
    k_out = k * enb
    k_st = k * ebl
    a_ch = jnp.exp(bl)
    return pre, eb, enb, ebl, q_in, k_out, k_st, a_ch


def _gla_specs(blk, idx):
    hk, hv = GLA_H * GLA_DK, GLA_H * GLA_DV
    return [pl.BlockSpec((blk, hk), lambda j: (idx(j), 0)),
            pl.BlockSpec((blk, hk), lambda j: (idx(j), 1)),
            pl.BlockSpec((blk, hv), lambda j: (idx(j), 1)),
            pl.BlockSpec((blk, SMALL), lambda j: (idx(j), 0)),
            pl.BlockSpec((SMALL, hk), lambda j: (0, 0)),
            pl.BlockSpec((1, hk), lambda j: (0, 0))]


def _gla_fwd(pbig, qk, psmall, wgate, bgate, name):
    T = pbig.shape[0]
    blk = min(ATT_BLK, T)
    nc = blk // CHUNK

    def body(q_ref, k_ref, v_ref, sm_ref, wg_ref, bg_ref, o_ref, ss_ref, st_ref):
        @pl.when(pl.program_id(0) == 0)
        def _():
            st_ref[...] = jnp.zeros_like(st_ref)

        causal = _iota2(CHUNK, CHUNK, 0) >= _iota2(CHUNK, CHUNK, 1)
        _, _, _, _, q_in, k_out, k_st, a_ch = _gla_block(q_ref, k_ref, sm_ref, wg_ref, bg_ref, nc, causal.astype(F32))
        v = _heads_first(v_ref[...], nc, GLA_DV)
        qb = _b(q_in)
        sc = jnp.where(causal, _bmm_nt(qb, _b(k_out)), 0.0)
        kv = _bmm_tn(v, _b(k_st))
        before = [None] * (GLA_H * nc)
        for i in range(GLA_H):
            st = st_ref[i]
            for c in range(nc):
                n = i * nc + c
                before[n] = st
                st = st * a_ch[n] + kv[n]
            st_ref[i] = st
        states = jnp.stack(before)
        ss_ref[...] = states.reshape(GLA_H, nc, GLA_DV, GLA_DK)
        o_ref[...] = _b(_heads_last(_bmm(_b(sc), v) + _bmm_nt(qb, _b(states)), nc))

    return pl.pallas_call(
        body, name=name, grid=(T // blk,),
        in_specs=_gla_specs(blk, lambda j: j),
        out_specs=[pl.BlockSpec((blk, GLA_H * GLA_DV), lambda j: (j, 0)),
                   pl.BlockSpec((GLA_H, nc, GLA_DV, GLA_DK), lambda j: (0, j, 0, 0))],
        out_shape=[jax.ShapeDtypeStruct((T, GLA_H * GLA_DV), BF16),
                   jax.ShapeDtypeStruct((GLA_H, T // CHUNK, GLA_DV, GLA_DK), F32)],
        scratch_shapes=[pltpu.VMEM((GLA_H, GLA_DV, GLA_DK), F32)],
        compiler_params=_cp("arbitrary"),
    )(qk, qk, pbig, psmall, wgate, bgate)


def _gla_bwd(pbig, qk, psmall, wgate, bgate, states, do, name, carry=None):
    ex_in, ex_args, ex_out, ex_shape, ex_sems = _carry_specs(carry)
    T = pbig.shape[0]
    blk = min(ATT_BLK, T)
    nc = blk // CHUNK
    nb = T // blk
    nbat = GLA_H * nc

    def body(q_ref, k_ref, v_ref, sm_ref, wg_ref, bg_ref, ss_ref, do_ref, dq_ref, dk_ref, dv_ref, dpre_ref, dst_ref):
        @pl.when(pl.program_id(0) == 0)
        def _():
            dst_ref[...] = jnp.zeros_like(dst_ref)

        causal = _iota2(CHUNK, CHUNK, 0) >= _iota2(CHUNK, CHUNK, 1)
        triu = (_iota2(CHUNK, CHUNK, 0) <= _iota2(CHUNK, CHUNK, 1)).astype(F32)
        pre, eb, enb, ebl, q_in, k_out, k_st, a_ch = _gla_block(q_ref, k_ref, sm_ref, wg_ref, bg_ref, nc,
                                                                causal.astype(F32))
        v = _heads_first(v_ref[...], nc, GLA_DV)
        dob = _b(_heads_first(do_ref[...], nc, GLA_DV))
        st = ss_ref[...].reshape(nbat, GLA_DV, GLA_DK)
        qb, kob, kstb = _b(q_in), _b(k_out), _b(k_st)
        qdo = _bmm_tn(dob, qb)
        after = [None] * nbat
        for i in range(GLA_H):
            dst = dst_ref[i]
            for c in range(nc - 1, -1, -1):
                n = i * nc + c
                after[n] = dst
                dst = dst * a_ch[n] + qdo[n]
            dst_ref[i] = dst
        dsa = jnp.stack(after)
        dsb = _b(dsa)
        sc = jnp.where(causal, _bmm_nt(qb, kob), 0.0)
        dsc = _b(jnp.where(causal, _bmm_nt(dob, v), 0.0))
        dq_in = _bmm(dob, _b(st)) + _bmm(dsc, kob)
        dk_out = _bmm_tn(dsc, qb)
        dk_st = _bmm(v, dsb)
        dv_ref[...] = _b(_heads_last(_bmm_tn(_b(sc), dob) + _bmm_nt(kstb, dsb), nc))
        da_ch = jnp.sum(st * dsa, axis=1, keepdims=True)
        tk = dk_st * k_st
        db = dq_in * q_in - dk_out * k_out - tk
        db_last = jnp.sum(tk, axis=1, keepdims=True) + da_ch * a_ch
        dq_ref[...] = _b(_heads_last(dq_in * (GLA_DK ** -0.5) * eb, nc))
        dk_ref[...] = _b(_heads_last(dk_out * enb + dk_st * ebl, nc))
        dla = _bmm(jnp.broadcast_to(triu, (nbat, CHUNK, CHUNK)), db, HI) + db_last
        dpre_ref[...] = _heads_last(dla * (1.0 / GLA_TAU) * _sigmoid(-pre), nc)

    r = lambda j: nb - 1 - j
    hk, hv = GLA_H * GLA_DK, GLA_H * GLA_DV
    return pl.pallas_call(
        _carry(carry, body, 8, 4, (nb,)), name=name, grid=(nb,),
        in_specs=_gla_specs(blk, r) + [pl.BlockSpec((GLA_H, nc, GLA_DV, GLA_DK), lambda j: (0, r(j), 0, 0)),
                                      pl.BlockSpec((blk, hv), lambda j: (r(j), 0))] + ex_in,
        out_specs=[pl.BlockSpec((blk, hk), lambda j: (r(j), 0)), pl.BlockSpec((blk, hk), lambda j: (r(j), 0)),
                   pl.BlockSpec((blk, hv), lambda j: (r(j), 0)), pl.BlockSpec((blk, hk), lambda j: (r(j), 0))] + ex_out,
        out_shape=[jax.ShapeDtypeStruct((T, hk), BF16), jax.ShapeDtypeStruct((T, hk), BF16),
                   jax.ShapeDtypeStruct((T, hv), BF16), jax.ShapeDtypeStruct((T, hk), F32)] + ex_shape,
        scratch_shapes=[pltpu.VMEM((GLA_H, GLA_DV, GLA_DK), F32)] + ex_sems,
        compiler_params=_cp("arbitrary"),
    )(qk, qk, pbig, psmall, wgate, bgate, states, do, *ex_args)


def _gla_gate_bwd(dpre, psmall, wgate, dsm, name):
    T = dpre.shape[0]
    tm = min(512, T)
    W = GLA_H * GLA_DK
    ngrp = dsm.shape[0]

    def body(dp_ref, sm_ref, wg_ref, dsm_ref, ds_ref, dw_ref, db_ref):
        @pl.when(pl.program_id(0) == 0)
        def _():
            dw_ref[...] = jnp.zeros_like(dw_ref)
            db_ref[...] = jnp.zeros_like(db_ref)

        dp = dp_ref[...]
        ds = _dot_nt(dp, wg_ref[...], HI)
        for i in range(ngrp):
            ds += dsm_ref[i]
        ds_ref[...] = ds
        dw_ref[...] += _dot_tn(sm_ref[...], dp, HI)
        db_ref[...] += jnp.sum(dp, axis=0, keepdims=True)

    return pl.pallas_call(
        body, name=name, grid=(T // tm,),
        in_specs=[pl.BlockSpec((tm, W), lambda i: (i, 0)), pl.BlockSpec((tm, SMALL), lambda i: (i, 0)),
                  pl.BlockSpec((SMALL, W), lambda i: (0, 0)), pl.BlockSpec((ngrp, tm, SMALL), lambda i: (0, i, 0))],
        out_specs=[pl.BlockSpec((tm, SMALL), lambda i: (i, 0)), pl.BlockSpec((SMALL, W), lambda i: (0, 0)),
                   pl.BlockSpec((1, W), lambda i: (0, 0))],
        out_shape=[jax.ShapeDtypeStruct((T, SMALL), F32), jax.ShapeDtypeStruct((SMALL, W), F32),
                   jax.ShapeDtypeStruct((1, W), F32)],
        compiler_params=_cp("arbitrary"),
    )(dpre, psmall, wgate, dsm)


CONV_C = 3 * 1024
CONV_BLK = 256


def _conv_fwd(pbig, cw8, name):
    T = pbig.shape[0]
    blk = min(CONV_BLK, T)

    def body(x_ref, w_ref, c_ref, prev_ref):
        @pl.when(pl.program_id(0) == 0)
        def _():
            prev_ref[...] = jnp.zeros_like(prev_ref)

        x = x_ref[...].astype(F32)
        prev = prev_ref[...]
        row8 = _iota2(8, CONV_C, 0)
        acc = x * w_ref[CONV_K - 1:CONV_K, :]
        for s in range(1, CONV_K):
            xs = pltpu.roll(x, s, 0)
            top = jnp.where(row8 < s, pltpu.roll(prev, s, 0), xs[:8])
            xs = jnp.concatenate([top, xs[8:]], axis=0)
            acc += xs * w_ref[CONV_K - 1 - s:CONV_K - s, :]
        c_ref[...] = _b(acc)
        prev_ref[...] = x[blk - 8:]

    return pl.pallas_call(
        body, name=name, grid=(T // blk,),
        in_specs=[pl.BlockSpec((blk, CONV_C), lambda i: (i, 1)), pl.BlockSpec((8, CONV_C), lambda i: (0, 0))],
        out_specs=pl.BlockSpec((blk, CONV_C), lambda i: (i, 0)),
        out_shape=jax.ShapeDtypeStruct((T, CONV_C), BF16),
        scratch_shapes=[pltpu.VMEM((8, CONV_C), F32)],
        compiler_params=_cp("arbitrary"),
    )(pbig, cw8)


def _conv_bwd(dcq, dck, dcv, pbig, cw8, name):
    T = pbig.shape[0]
    blk = min(CONV_BLK, T)
    nb = T // blk

    def body(dq_ref, dk_ref, dv_ref, x_ref, w_ref, dx_ref, dw_ref, nxt_ref):
        @pl.when(pl.program_id(0) == 0)
        def _():
            nxt_ref[...] = jnp.zeros_like(nxt_ref)
            dw_ref[...] = jnp.zeros_like(dw_ref)

        dc = jnp.concatenate([dq_ref[...], dk_ref[...], dv_ref[...]], axis=1).astype(F32)
        x = x_ref[...].astype(F32)
        nxt = nxt_ref[...]
        row8 = _iota2(8, CONV_C, 0)
        acc = dc * w_ref[CONV_K - 1:CONV_K, :]
        dws = [jnp.sum(dc * x, axis=0, keepdims=True)]
        for s in range(1, CONV_K):
            ds = pltpu.roll(dc, blk - s, 0)
            bot = jnp.where(row8 >= 8 - s, pltpu.roll(nxt, 8 - s, 0), ds[blk - 8:])
            ds = jnp.concatenate([ds[:blk - 8], bot], axis=0)
            acc += ds * w_ref[CONV_K - 1 - s:CONV_K - s, :]
            dws.append(jnp.sum(ds * x, axis=0, keepdims=True))
        dx_ref[...] = _b(acc)
        dw_ref[...] += jnp.concatenate(dws[::-1] + [jnp.zeros((8 - CONV_K, CONV_C), F32)], axis=0)
        nxt_ref[...] = dc[:8]

    part = pl.BlockSpec((blk, 1024), lambda i: (nb - 1 - i, 0))
    return pl.pallas_call(
        body, name=name, grid=(nb,),
        in_specs=[part, part, part, pl.BlockSpec((blk, CONV_C), lambda i: (nb - 1 - i, 1)),
                  pl.BlockSpec((8, CONV_C), lambda i: (0, 0))],
        out_specs=[pl.BlockSpec((blk, CONV_C), lambda i: (nb - 1 - i, 0)), pl.BlockSpec((8, CONV_C), lambda i: (0, 0))],
        out_shape=[jax.ShapeDtypeStruct((T, CONV_C), BF16), jax.ShapeDtypeStruct((8, CONV_C), F32)],
        scratch_shapes=[pltpu.VMEM((8, CONV_C), F32)],
        compiler_params=_cp("arbitrary"),
    )(dcq, dck, dcv, pbig, cw8)


def _col(x, lane):
    if isinstance(lane, int):
        return jnp.broadcast_to(x[:, lane:lane + 1], x.shape)
    sel = _iota2(x.shape[0], x.shape[1], 1) == lane
    return jnp.broadcast_to(jnp.sum(jnp.where(sel, x, 0.0), axis=1, keepdims=True), x.shape)


def _bmm(a, b, prec=None):
    return jnp.einsum("bij,bjk->bik", a, b, preferred_element_type=F32, precision=prec)


def _bmm_nt(a, b, prec=None):
    return jnp.einsum("bij,bkj->bik", a, b, preferred_element_type=F32, precision=prec)


def _bmm_tn(a, b, prec=None):
    return jnp.einsum("bji,bjk->bik", a, b, preferred_element_type=F32, precision=prec)


def _unit_lower_inverse(low):
    eye = (_iota2(CHUNK, CHUNK, 0) == _iota2(CHUNK, CHUNK, 1)).astype(F32)
    xk = -low
    inv = eye + xk
    for _ in range(5):
        xb = _b(xk)
        xk = _bmm(xb, xb)
        inv = inv + _bmm(_b(inv), _b(xk))
    resid = eye - _bmm(eye + low, inv, HI)
    return inv + _bmm(inv, resid, HI)


def _heads_first(x, nc, w=128):
    hb = x.shape[1] // w
    return jnp.concatenate([x[:, i * w:(i + 1) * w].reshape(nc, CHUNK, w) for i in range(hb)], axis=0)


def _heads_last(x, nc):
    hb = x.shape[0] // nc
    return jnp.concatenate([x[i * nc:(i + 1) * nc].reshape(nc * CHUNK, x.shape[2]) for i in range(hb)], axis=1)


def _gdn_block(cq_ref, ck_ref, cv_ref, sm_ref, par_ref, h0, hb, nc, masks, solved=None):
    causal, strict, tril, eye = masks
    nbat = hb * nc
    cq = _heads_first(cq_ref[...].astype(F32), nc)
    ck = _heads_first(ck_ref[...].astype(F32), nc)
    cv = _heads_first(cv_ref[...].astype(F32), nc)
    sq, sk, sv = _sigmoid(cq), _sigmoid(ck), _sigmoid(cv)
    q, k, v = cq * sq, ck * sk, cv * sv
    rq = lax.rsqrt(jnp.sum(q * q, axis=-1, keepdims=True) + EPS)
    rk = lax.rsqrt(jnp.sum(k * k, axis=-1, keepdims=True) + EPS)
    qh, kn = q * rq, k * rk
    qn = qh * (DN_DK ** -0.5)
    sm = sm_ref[...]
    per_head = lambda fn: jnp.concatenate([fn(i) for i in range(hb)], axis=0)
    braw = per_head(lambda i: _col(sm, GLA_RANK + h0 + i).reshape(nc, CHUNK, 128))
    araw = per_head(lambda i: _col(sm, GLA_RANK + DN_H + h0 + i).reshape(nc, CHUNK, 128))
    ea = per_head(lambda i: jnp.broadcast_to(jnp.exp(par_ref[i, 0:1, :])[None], (nc, 1, 128)))
    bias = per_head(lambda i: jnp.broadcast_to(par_ref[i, 1:2, :][None], (nc, 1, 128)))
    beta = _sigmoid(braw)
    sp_arg = araw + bias
    g = -ea * _softplus(sp_arg)
    G = _bmm(jnp.broadcast_to(tril, (nbat, CHUNK, CHUNK)), g, HI)
    gc = G[:, :, :CHUNK]
    grow = jnp.sum(eye * gc, axis=1, keepdims=True)
    decay = jnp.exp(jnp.where(causal, gc - grow, -1e30))
    kb = kn * beta
    A = _bmm_nt(_b(kb), _b(kn))
    eG = jnp.exp(G)
    gl = G[:, CHUNK - 1:CHUNK, :]
    eGl = jnp.exp(gl - G)
    g_ch = jnp.exp(gl)
    rv = v * beta
    rkk = kb * eG
    if solved is None:
        tinv_b = _b(_unit_lower_inverse(jnp.where(strict, A * decay, 0.0)))
        u = _bmm(tinv_b, _b(rv))
        w = _b(_bmm(tinv_b, _b(rkk)))
    else:
        tinv_b, u, w = solved
    B = _bmm_nt(_b(qn), _b(kn))
    qk = jnp.where(causal, B * decay, 0.0)
    q_dec = qn * eG
    k_st = kn * eGl
    return dict(cq=cq, ck=ck, cv=cv, sq=sq, sk=sk, sv=sv, q=q, k=k, v=v, rq=rq, rk=rk, qh=qh, kn=kn, qn=qn,
                beta=beta, ea=ea, sp_arg=sp_arg, g=g, G=G, decay=decay, kb=kb, A=A, tinv_b=tinv_b, eG=eG, eGl=eGl,
                g_ch=g_ch, rv=rv, rkk=rkk, u=u, w=w, B=B, qk=qk, q_dec=q_dec, k_st=k_st)


def _gdn_masks():
    r, c = _iota2(CHUNK, CHUNK, 0), _iota2(CHUNK, CHUNK, 1)
    return r >= c, r > c, (r >= c).astype(F32), (r == c).astype(F32)


def _gdn_specs(blk, hb, idx):
    ng = DN_H // hb
    return [pl.BlockSpec((blk, hb * DN_DK), lambda h, j: (idx(j), h)),
            pl.BlockSpec((blk, hb * DN_DK), lambda h, j: (idx(j), ng + h)),
            pl.BlockSpec((blk, hb * DN_DV), lambda h, j: (idx(j), 2 * ng + h)),
            pl.BlockSpec((blk, SMALL), lambda h, j: (idx(j), 0)),
            pl.BlockSpec((hb, 8, 128), lambda h, j: (h, 0, 0))]


def _gdn_solved_specs(blk, hb, idx):
    nc = blk // CHUNK
    spec = lambda w: pl.BlockSpec((hb, nc, CHUNK, w), lambda h, j: (h, idx(j), 0, 0))
    return [spec(CHUNK), spec(DN_DV), spec(DN_DK)]


def _gdn_fwd(conv, psmall, par, name):
    T = conv.shape[0]
    blk = min(GDN_BLK, T)
    nc = blk // CHUNK
    hb = GDN_HEADS
    N = T // CHUNK

    def body(cq_ref, ck_ref, cv_ref, sm_ref, par_ref, o_ref, ss_ref, ti_ref, u_ref, w_ref, s_ref):
        @pl.when(pl.program_id(1) == 0)
        def _():
            s_ref[...] = jnp.zeros_like(s_ref)

        h0 = 0 if hb == DN_H else pl.program_id(0) * hb
        f = _gdn_block(cq_ref, ck_ref, cv_ref, sm_ref, par_ref, h0, hb, nc, _gdn_masks())
        ti_ref[...] = f["tinv_b"].reshape(hb, nc, CHUNK, CHUNK)
        u_ref[...] = f["u"].reshape(hb, nc, CHUNK, DN_DV)
        w_ref[...] = f["w"].reshape(hb, nc, CHUNK, DN_DK)
        wb, ub, kstb, qkb = f["w"], _b(f["u"]), _b(f["k_st"]), _b(f["qk"])
        mix = _b(_bmm_tn(kstb, wb))
        add = _bmm_tn(kstb, ub)
        q_eff = _b(f["q_dec"] - _bmm(qkb, wb))
        before = [None] * (hb * nc)
        S = [s_ref[i] for i in range(hb)]
        for c in range(nc):
            for i in range(hb):
                n = i * nc + c
                before[n] = S[i]
                S[i] = S[i] * f["g_ch"][n] - _dot(mix[n], _b(S[i])) + add[n]
        for i in range(hb):
            s_ref[i] = S[i]
        states = jnp.stack(before)
        ss_ref[...] = states.reshape(hb, nc, DN_DK, DN_DV)
        o_ref[...] = _b(_heads_last(_bmm(qkb, ub) + _bmm(q_eff, _b(states)), nc))

    return pl.pallas_call(
        body, name=name, grid=(DN_H // hb, T // blk),
        in_specs=_gdn_specs(blk, hb, lambda j: j),
        out_specs=[pl.BlockSpec((blk, hb * DN_DV), lambda h, j: (j, h)),
                   pl.BlockSpec((hb, nc, DN_DK, DN_DV), lambda h, j: (h, j, 0, 0))]
        + _gdn_solved_specs(blk, hb, lambda j: j),
        out_shape=[jax.ShapeDtypeStruct((T, DN_H * DN_DV), BF16), jax.ShapeDtypeStruct((DN_H, N, DN_DK, DN_DV), F32),
                   jax.ShapeDtypeStruct((DN_H, N, CHUNK, CHUNK), BF16), jax.ShapeDtypeStruct((DN_H, N, CHUNK, DN_DV), F32),
                   jax.ShapeDtypeStruct((DN_H, N, CHUNK, DN_DK), BF16)],
        scratch_shapes=[pltpu.VMEM((hb, DN_DK, DN_DV), F32)],
        compiler_params=_cp("parallel", "arbitrary"),
    )(conv, conv, conv, psmall, par)


def _gdn_bwd(conv, psmall, par, states, do, solved, name, carry=None):
    ex_in, ex_args, ex_out, ex_shape, ex_sems = _carry_specs(carry)
    T = conv.shape[0]
    blk = min(GDN_BLK, T)
    nc = blk // CHUNK
    nb = T // blk
    hb = GDN_HEADS
    nbat = hb * nc
    rsum = lambda x: jnp.sum(x, axis=-1, keepdims=True)

    def body(cq_ref, ck_ref, cv_ref, sm_ref, par_ref, ss_ref, do_ref, ti_ref, u_ref, w_ref,
             dcq_ref, dck_ref, dcv_ref, dsm_ref, dpar_ref, ds_ref):
        @pl.when(pl.program_id(1) == 0)
        def _():
            ds_ref[...] = jnp.zeros_like(ds_ref)
            dpar_ref[...] = jnp.zeros_like(dpar_ref)

        masks = _gdn_masks()
        causal, strict, tril, eye = masks
        triu = (_iota2(CHUNK, CHUNK, 0) <= _iota2(CHUNK, CHUNK, 1)).astype(F32)
        lane = _iota2(CHUNK, 128, 1)
        last_row = _iota2(CHUNK, 128, 0) == CHUNK - 1
        h0 = 0 if hb == DN_H else pl.program_id(0) * hb
        solved = (ti_ref[...].reshape(nbat, CHUNK, CHUNK), u_ref[...].reshape(nbat, CHUNK, DN_DV),
                  w_ref[...].reshape(nbat, CHUNK, DN_DK))
        f = _gdn_block(cq_ref, ck_ref, cv_ref, sm_ref, par_ref, h0, hb, nc, masks, solved)
        S = ss_ref[...].reshape(nbat, DN_DK, DN_DV)
        Sb = _b(S)
        do_ = _b(_heads_first(do_ref[...], nc))
        wb, qdb, kstb, qkb = _b(f["w"]), _b(f["q_dec"]), _b(f["k_st"]), _b(f["qk"])
        vnb = _b(f["u"] - _bmm(wb, Sb))
        dvn0 = _bmm_tn(qkb, do_)
        qdo = _bmm_tn(qdb, do_)
        dS = [ds_ref[i] for i in range(hb)]
        after = [None] * nbat
        for c in range(nc - 1, -1, -1):
            for i in range(hb):
                n = i * nc + c
                after[n] = dS[i]
                dvn_c = _b(dvn0[n] + _dot(kstb[n], _b(dS[i])))
                dS[i] = dS[i] * f["g_ch"][n] + qdo[n] - _dot_tn(wb[n], dvn_c)
        for i in range(hb):
            ds_ref[i] = dS[i]
        dSa = jnp.stack(after)
        dSb = _b(dSa)
        dvn = dvn0 + _bmm(kstb, dSb)
        dvnb = _b(dvn)
        dq_dec = _bmm_nt(do_, Sb)
        dqk = jnp.where(causal, _bmm_nt(do_, vnb), 0.0)
        dk_st = _bmm_nt(vnb, dSb)
        dg_ch = jnp.sum(rsum(S * dSa), axis=1, keepdims=True)
        dw = -_bmm_nt(dvnb, Sb)
        drv = _bmm_tn(f["tinv_b"], dvnb)
        drk = _bmm_tn(f["tinv_b"], _b(dw))
        dlow = jnp.where(strict, -(_bmm_nt(_b(drv), _b(f["u"])) + _bmm_nt(_b(drk), wb)), 0.0)
        dv = drv * f["beta"]
        dbeta = rsum(drv * f["v"])
        dkb = drk * f["eG"]
        dG = rsum(drk * f["rkk"])
        dA = dlow * f["decay"]
        ddec = dlow * f["A"]
        dkb += _bmm(_b(dA), _b(f["kn"]))
        dkn = _bmm_tn(_b(dA), _b(f["kb"]))
        dB = dqk * f["decay"]
        ddec += dqk * f["B"]
        dqn = _bmm(_b(dB), _b(f["kn"]))
        dkn += _bmm_tn(_b(dB), _b(f["qn"]))
        dD = ddec * f["decay"]
        dG += rsum(dD) - rsum(eye * jnp.sum(dD, axis=1, keepdims=True))
        dqn += dq_dec * f["eG"]
        dG += rsum(dq_dec * f["q_dec"])
        dkn += dk_st * f["eGl"]
        tks = rsum(dk_st * f["k_st"])
        dG -= tks
        dG_last = jnp.sum(tks, axis=1, keepdims=True) + dg_ch * f["g_ch"][:, :, :1]
        dkn += dkb * f["beta"]
        dbeta += rsum(dkb * f["kn"])
        dGf = jnp.broadcast_to(dG, (nbat, CHUNK, 128)) + jnp.where(last_row, dG_last, 0.0)
        dg = _bmm(jnp.broadcast_to(triu, (nbat, CHUNK, CHUNK)), dGf, HI)
        dbraw = dbeta * f["beta"][:, :, :1] * (1.0 - f["beta"][:, :, :1])
        daraw = dg * (-f["ea"]) * _sigmoid(f["sp_arg"])
        both = lambda t: jnp.sum(jnp.sum(t, axis=1, keepdims=True), axis=0)
        dgg = dg * f["g"]
        dsm = jnp.zeros((nc, CHUNK, SMALL), F32)
        for i in range(hb):
            mine = slice(i * nc, (i + 1) * nc)
            dsm += (jnp.where(lane == GLA_RANK + h0 + i, dbraw[mine], 0.0)
                    + jnp.where(lane == GLA_RANK + DN_H + h0 + i, daraw[mine], 0.0))
            dpar = jnp.where(lane[:1] == 0, both(dgg[mine]), jnp.where(lane[:1] == 1, both(daraw[mine]), 0.0))
            dpar_ref[i] += jnp.broadcast_to(dpar, (8, 128))
        dsm_ref[0] = dsm.reshape(blk, SMALL)
        dqh = dqn * (DN_DK ** -0.5)
        dq = f["rq"] * (dqh - f["qh"] * rsum(dqh * f["qh"]))
        dk = f["rk"] * (dkn - f["kn"] * rsum(dkn * f["kn"]))
        dsilu = lambda x, s: s * (1.0 + x * (1.0 - s))
        dcq_ref[...] = _b(_heads_last(dq * dsilu(f["cq"], f["sq"]), nc))
        dck_ref[...] = _b(_heads_last(dk * dsilu(f["ck"], f["sk"]), nc))
        dcv_ref[...] = _b(_heads_last(dv * dsilu(f["cv"], f["sv"]), nc))

    r = lambda j: nb - 1 - j
    out_blk = pl.BlockSpec((blk, hb * DN_DK), lambda h, j: (r(j), h))
    grid = (DN_H // hb, nb)
    return pl.pallas_call(
        _carry(carry, body, 10, 5, grid), name=name, grid=grid,
        in_specs=_gdn_specs(blk, hb, r) + [pl.BlockSpec((hb, nc, DN_DK, DN_DV), lambda h, j: (h, r(j), 0, 0)),
                                          pl.BlockSpec((blk, hb * DN_DV), lambda h, j: (r(j), h))]
        + _gdn_solved_specs(blk, hb, r) + ex_in,
        out_specs=[out_blk, out_blk, out_blk, pl.BlockSpec((1, blk, SMALL), lambda h, j: (h, r(j), 0)),
                   pl.BlockSpec((hb, 8, 128), lambda h, j: (h, 0, 0))] + ex_out,
        out_shape=[jax.ShapeDtypeStruct((T, DN_H * DN_DK), BF16)] * 3 + [
            jax.ShapeDtypeStruct((DN_H // hb, T, SMALL), F32), jax.ShapeDtypeStruct((DN_H, 8, 128), F32)] + ex_shape,
        scratch_shapes=[pltpu.VMEM((hb, DN_DK, DN_DV), F32)] + ex_sems,
        compiler_params=_cp("arbitrary", "arbitrary"),
    )(conv, conv, conv, psmall, par, states, do, *solved, *ex_args)


def _head_norm(o, w, dv):
    outs, rs = [], []
    for i in range(o.shape[1] // dv):
        oh = o[:, i * dv:(i + 1) * dv]
        r = lax.rsqrt(jnp.mean(oh * oh, axis=-1, keepdims=True) + EPS)
        outs.append(oh * r)
        rs.append(r)
    return outs, rs


def _merge_specs(tm):
    col = lambda c: pl.BlockSpec((tm, D), lambda i: (i, c))
    return [col(0), col(0), col(2), col(6), col(7), col(8),
            pl.BlockSpec((1, GLA_DV), lambda i: (0, 0)), pl.BlockSpec((1, DN_DV), lambda i: (0, 0)),
            pl.BlockSpec((D, D), lambda i: (0, 0))]


def _merge_fwd(h, oa, ob, pbig, gla_hn, dn_hn, wout, name):
    T = h.shape[0]
    tm = min(ROW_BLK, T)

    def body(h_ref, oa_ref, ob_ref, gr_ref, dg_ref, ma_ref, mb_ref, wa_ref, wb_ref, wo_ref, ho_ref, y_ref):
        na, _ = _head_norm(oa_ref[...].astype(F32), wa_ref[...], GLA_DV)
        nbs, _ = _head_norm(ob_ref[...].astype(F32), wb_ref[...], DN_DV)
        hna = jnp.concatenate([t * wa_ref[...] for t in na], axis=1)
        hnb = jnp.concatenate([t * wb_ref[...] for t in nbs], axis=1)
        gr = gr_ref[...].astype(F32)
        dg = dg_ref[...].astype(F32)
        y = (_sigmoid(ma_ref[...].astype(F32)) * hna * (gr * _sigmoid(gr))
             + _sigmoid(mb_ref[...].astype(F32)) * hnb * (dg * _sigmoid(dg)))
        yb = _b(y)
        y_ref[...] = yb
        ho_ref[...] = h_ref[...] + _dot(yb, wo_ref[...])

    row = pl.BlockSpec((tm, D), lambda i: (i, 0))
    return pl.pallas_call(
        body, name=name, grid=(T // tm,),
        in_specs=[row] + _merge_specs(tm),
        out_specs=[row, row],
        out_shape=[jax.ShapeDtypeStruct((T, D), F32), jax.ShapeDtypeStruct((T, D), BF16)],
        compiler_params=_cp("arbitrary"),
    )(h, oa, ob, pbig, pbig, pbig, pbig, gla_hn, dn_hn, wout)


def _merge_bwd(dh, oa, ob, pbig, gla_hn, dn_hn, wout, name, carry=None):
    T = dh.shape[0]
    tm = min(ROW_BLK, T)
    ex_in, ex_args, ex_out, ex_shape, ex_sems = _carry_specs(carry)

    def branch(dy, o_ref, w_ref, gate_ref, m_ref, dv):
        w = w_ref[...]
        ohat, rs = _head_norm(o_ref[...].astype(F32), w, dv)
        gate = gate_ref[...].astype(F32)
        m = m_ref[...].astype(F32)
        sgate, sm = _sigmoid(gate), _sigmoid(m)
        silu = gate * sgate
        ohat_all = jnp.concatenate(ohat, axis=1)
        hn = jnp.concatenate([t * w for t in ohat], axis=1)
        d_on = dy * sm
        d_m = dy * hn * silu * sm * (1.0 - sm)
        d_hn = d_on * silu
        d_gate = d_on * hn * (sgate * (1.0 + gate * (1.0 - sgate)))
        dw = jnp.zeros((1, dv), F32)
        d_o = []
        for i, (oh, r) in enumerate(zip(ohat, rs)):
            dhn = d_hn[:, i * dv:(i + 1) * dv]
            dw += jnp.sum(dhn * oh, axis=0, keepdims=True)
            dohat = dhn * w
            d_o.append(r * (dohat - oh * jnp.mean(dohat * oh, axis=-1, keepdims=True)))
        return jnp.concatenate(d_o, axis=1), d_gate, d_m, dw

    def body(dh_ref, oa_ref, ob_ref, gr_ref, dg_ref, ma_ref, mb_ref, wa_ref, wb_ref, wo_ref,
             doa_ref, dob_ref, dgr_ref, ddg_ref, dma_ref, dmb_ref, dwa_ref, dwb_ref, dhb_ref):
        @pl.when(pl.program_id(0) == 0)
        def _():
            dwa_ref[...] = jnp.zeros_like(dwa_ref)
            dwb_ref[...] = jnp.zeros_like(dwb_ref)

        dhb = _b(dh_ref[...])
        dhb_ref[...] = dhb
        dy = _dot_nt(dhb, wo_ref[...])
        d_oa, d_gr, d_ma, dwa = branch(dy, oa_ref, wa_ref, gr_ref, ma_ref, GLA_DV)
        d_ob, d_dg, d_mb, dwb = branch(dy, ob_ref, wb_ref, dg_ref, mb_ref, DN_DV)
        doa_ref[...] = _b(d_oa)
        dob_ref[...] = _b(d_ob)
        dgr_ref[...] = _b(d_gr)
        ddg_ref[...] = _b(d_dg)
        dma_ref[...] = _b(d_ma)
        dmb_ref[...] = _b(d_mb)
        dwa_ref[...] += dwa
        dwb_ref[...] += dwb

    row = pl.BlockSpec((tm, D), lambda i: (i, 0))
    b16 = jax.ShapeDtypeStruct((T, D), BF16)
    return pl.pallas_call(
        _carry(carry, body, 10, 9, (T // tm,)), name=name, grid=(T // tm,),
        in_specs=[row] + _merge_specs(tm) + ex_in,
        out_specs=[row] * 6 + [pl.BlockSpec((1, GLA_DV), lambda i: (0, 0)), pl.BlockSpec((1, DN_DV), lambda i: (0, 0)), row]
        + ex_out,
        out_shape=[b16, b16, b16, b16, b16, b16, jax.ShapeDtypeStruct((1, GLA_DV), F32),
                   jax.ShapeDtypeStruct((1, DN_DV), F32), b16] + ex_shape,
        scratch_shapes=ex_sems,
        compiler_params=_cp("arbitrary"),
    )(dh, oa, ob, pbig, pbig, pbig, pbig, gla_hn, dn_hn, wout, *ex_args)


def _adamw(w, g, m, v, rows, name):
    R, C = w.shape
    rows = min(rows, R)
    c1 = 1.0 - ADAM_B1 ** ADAM_STEP
    c2 = 1.0 - ADAM_B2 ** ADAM_STEP

    def body(w_ref, g_ref, m_ref, v_ref, d_ref, mo_ref, vo_ref):
        g_ = g_ref[...]
        m_ = ADAM_B1 * m_ref[...] + (1.0 - ADAM_B1) * g_
        v_ = ADAM_B2 * v_ref[...] + (1.0 - ADAM_B2) * (g_ * g_)
        mo_ref[...] = m_
        vo_ref[...] = v_
        d_ref[...] = -ADAM_LR * ((m_ / c1) / (jnp.sqrt(v_ / c2) + ADAM_EPS) + ADAM_WD * w_ref[...])

    blk = pl.BlockSpec((rows, C), lambda i: (i, 0))
    shp = jax.ShapeDtypeStruct((R, C), F32)
    return pl.pallas_call(
        body, name=name, grid=(R // rows,),
        in_specs=[blk] * 4, out_specs=[blk] * 3, out_shape=[shp] * 3,
        compiler_params=_cp("parallel"),
    )(w, g, m, v)


def _me():
    return lax.axis_index("x"), lax.axis_index("y"), lax.axis_index("c")


def _other_chips(x, y):
    return [(1 - x, y), (x, 1 - y), (1 - x, 1 - y)]


def _half_rows(ref, hf):
    half = ref.shape[-2] // 2
    rows = pl.ds(pl.multiple_of(hf * half, 16), half)
    return ref.at[rows, :] if len(ref.shape) == 2 else ref.at[:, rows, :]


class _GatherBig:
    def __init__(self, big):
        self.arrays = list(big)
        self.out_shape = [jax.ShapeDtypeStruct((N_SHARD,) + w.shape, w.dtype) for w in big]
        self.n_sem = 7 * len(big)

    @staticmethod
    def _copy(sems, k, src, dst, to):
        return pltpu.make_async_remote_copy(src_ref=src, dst_ref=dst, send_sem=sems[0].at[k], recv_sem=sems[1].at[k],
                                            device_id=to, device_id_type=MESH)

    def start(self, ins, outs, *sems):
        x, y, c = _me()
        mine = 2 * x + y
        for i, (w_ref, o_ref) in enumerate(zip(ins, outs)):
            self._copy(sems, 7 * i + 6, w_ref, o_ref.at[mine], (x, y, 1 - c)).start()
            for j, chip in enumerate(_other_chips(x, y)):
                self._copy(sems, 7 * i + j, _half_rows(w_ref, c), _half_rows(o_ref.at[mine], c), (*chip, c)).start()

    def relay(self, ins, outs, *sems):
        x, y, c = _me()
        for i, o_ref in enumerate(outs):
            for j, chip in enumerate(_other_chips(x, y)):
                landed = _half_rows(o_ref.at[2 * chip[0] + chip[1]], c)
                self._copy(sems, 7 * i + j, landed, landed, (x, y, c)).wait_recv()
                self._copy(sems, 7 * i + 3 + j, landed, landed, (x, y, 1 - c)).start()

    def finish(self, ins, outs, *sems):
        x, y, c = _me()
        me, sibling = (x, y, c), (x, y, 1 - c)
        chips = _other_chips(x, y)
        slot = lambda chip: 2 * chip[0] + chip[1]
        for i, (w_ref, o_ref) in enumerate(zip(ins, outs)):
            for j, chip in enumerate(chips):
                passed = _half_rows(o_ref.at[slot(chip)], 1 - c)
                self._copy(sems, 7 * i + 3 + j, passed, passed, me).wait_recv()
            self._copy(sems, 7 * i + 6, o_ref.at[slot((x, y))], o_ref.at[slot((x, y))], me).wait_recv()
        for i, (w_ref, o_ref) in enumerate(zip(ins, outs)):
            self._copy(sems, 7 * i + 6, w_ref, o_ref.at[slot((x, y))], sibling).wait_send()
            for j, chip in enumerate(chips):
                self._copy(sems, 7 * i + j, _half_rows(w_ref, c), _half_rows(o_ref.at[slot((x, y))], c),
                           (*chip, c)).wait_send()
                landed = _half_rows(o_ref.at[slot(chip)], c)
                self._copy(sems, 7 * i + 3 + j, landed, landed, sibling).wait_send()


class _SiblingExchange:
    def __init__(self, gs):
        self.arrays = list(gs)
        self.out_shape = [jax.ShapeDtypeStruct((g.shape[0], g.shape[1] // 2, g.shape[2]), g.dtype) for g in gs]
        self.n_sem = len(gs)

    def _copies(self, ins, outs, send_sems, recv_sems):
        x, y, c = _me()
        return [pltpu.make_async_remote_copy(src_ref=_half_rows(ins[i], 1 - c), dst_ref=outs[i],
                                             send_sem=send_sems.at[i], recv_sem=recv_sems.at[i],
                                             device_id=(x, y, 1 - c), device_id_type=MESH) for i in range(len(ins))]

    def start(self, ins, outs, *sems):
        for cp in self._copies(ins, outs, *sems):
            cp.start()

    def finish(self, ins, outs, *sems):
        for cp in self._copies(ins, outs, *sems):
            cp.wait()


class _ChipsExchange:
    def __init__(self, pbs):
        self.arrays = list(pbs)
        self.out_shape = [jax.ShapeDtypeStruct((3,) + p.shape[1:], p.dtype) for p in pbs]
        self.n_sem = 3 * len(pbs)

    def _copies(self, ins, outs, send_sems, recv_sems):
        x, y, c = _me()
        return [pltpu.make_async_remote_copy(src_ref=ins[i].at[2 * chip[0] + chip[1]], dst_ref=outs[i].at[j],
                                             send_sem=send_sems.at[3 * i + j], recv_sem=recv_sems.at[3 * i + j],
                                             device_id=(*chip, c), device_id_type=MESH)
                for i in range(len(ins)) for j, chip in enumerate(_other_chips(x, y))]

    def start(self, ins, outs, *sems):
        for cp in self._copies(ins, outs, *sems):
            cp.start()

    def finish(self, ins, outs, *sems):
        for cp in self._copies(ins, outs, *sems):
            cp.wait()


def _carry(ex, body, n_in, n_out, grid):
    if ex is None:
        return body
    ni, no = len(ex.arrays), len(ex.out_shape)

    def carried(*refs):
        ins, ex_in = refs[:n_in], refs[n_in:n_in + ni]
        outs, ex_out = refs[n_in + ni:n_in + ni + n_out], refs[n_in + ni + n_out:n_in + ni + n_out + no]
        scratch, sems = refs[n_in + ni + n_out + no:-2], refs[-2:]
        step = functools.reduce(lambda acc, a: acc * grid[a] + pl.program_id(a), range(len(grid)), 0)
        steps = math.prod(grid)

        @pl.when(step == 0)
        def _():
            ex.start(ex_in, ex_out, *sems)

        body(*ins, *outs, *scratch)

        if hasattr(ex, "relay"):
            @pl.when(step == (3 * steps) // 4)
            def _():
                ex.relay(ex_in, ex_out, *sems)

        @pl.when(step == steps - 1)
        def _():
            ex.finish(ex_in, ex_out, *sems)

    return carried


def _carry_specs(ex):
    if ex is None:
        return [], [], [], [], []
    sems = [pltpu.SemaphoreType.DMA((ex.n_sem,)), pltpu.SemaphoreType.DMA((ex.n_sem,))]
    return [ANY] * len(ex.arrays), ex.arrays, [ANY] * len(ex.out_shape), ex.out_shape, sems


def _gather_weights(big, small, name):
    nbig, nsm = len(big), len(small)
    n = nbig + nsm
    own_sem = 6 * nbig + 3 * nsm

    def body(*refs):
        ins, outs = refs[:n], refs[n:2 * n]
        send_sems, recv_sems = refs[2 * n:]
        x, y, c = _me()
        sibling = (x, y, 1 - c)
        chips = _other_chips(x, y)
        slot = lambda chip: 2 * chip[0] + chip[1]

        def copy(k, src, dst, to):
            return pltpu.make_async_remote_copy(src_ref=src, dst_ref=dst, send_sem=send_sems.at[k],
                                                recv_sem=recv_sems.at[k], device_id=to, device_id_type=MESH)

        sent = []
        for i in range(nbig):
            sent.append(copy(own_sem + i, ins[i], outs[i].at[slot((x, y))], sibling))
            sent[-1].start()
            for j, chip in enumerate(chips):
                sent.append(copy(6 * i + j, _half_rows(ins[i], c), _half_rows(outs[i].at[slot((x, y))], c), (*chip, c)))
                sent[-1].start()
        for t in range(nsm):
            w_ref, o_ref = ins[nbig + t], outs[nbig + t]
            o_ref[slot((x, y))] = w_ref[...]
            for j, chip in enumerate(chips):
                sent.append(copy(6 * nbig + 3 * t + j, w_ref, o_ref.at[slot((x, y))], (*chip, c)))
                sent[-1].start()
        for i in range(nbig):
            for j, chip in enumerate(chips):
                landed = _half_rows(outs[i].at[slot(chip)], c)
                copy(6 * i + j, landed, landed, (x, y, c)).wait_recv()
                sent.append(copy(6 * i + 3 + j, landed, landed, sibling))
                sent[-1].start()
        for t in range(nsm):
            for j, chip in enumerate(chips):
                landed = outs[nbig + t].at[slot(chip)]
                copy(6 * nbig + 3 * t + j, landed, landed, (x, y, c)).wait_recv()
        for i in range(nbig):
            for j, chip in enumerate(chips):
                passed = _half_rows(outs[i].at[slot(chip)], 1 - c)
                copy(6 * i + 3 + j, passed, passed, (x, y, c)).wait_recv()
        for i in range(nbig):
            mine = outs[i].at[slot((x, y))]
            copy(own_sem + i, mine, mine, (x, y, c)).wait_recv()
        for cp in sent:
            cp.wait_send()

    vm = pl.BlockSpec(memory_space=pltpu.VMEM)
    nsem = own_sem + nbig
    return pl.pallas_call(
        body, name=name, in_specs=[ANY] * nbig + [vm] * nsm, out_specs=[ANY] * nbig + [vm] * nsm,
        out_shape=[jax.ShapeDtypeStruct((N_SHARD,) + w.shape, w.dtype) for w in list(big) + list(small)],
        scratch_shapes=[pltpu.SemaphoreType.DMA((nsem,)), pltpu.SemaphoreType.DMA((nsem,))],
        compiler_params=pltpu.CompilerParams(has_side_effects=True),
    )(*big, *small)


def _rs_sibling(gs, name):
    n = len(gs)

    def body(*refs):
        send_sems, recv_sems = refs[2 * n:]
        x, y, c = _me()
        cps = [pltpu.make_async_remote_copy(src_ref=_half_rows(refs[i], 1 - c), dst_ref=refs[n + i],
                                            send_sem=send_sems.at[i], recv_sem=recv_sems.at[i],
                                            device_id=(x, y, 1 - c), device_id_type=MESH) for i in range(n)]
        for cp in cps:
            cp.start()
        for cp in cps:
            cp.wait()

    return pl.pallas_call(
        body, name=name, in_specs=[ANY] * n, out_specs=[ANY] * n,
        out_shape=[jax.ShapeDtypeStruct((g.shape[0], g.shape[1] // 2, g.shape[2]), g.dtype) for g in gs],
        scratch_shapes=[pltpu.SemaphoreType.DMA((n,)), pltpu.SemaphoreType.DMA((n,))],
        compiler_params=pltpu.CompilerParams(has_side_effects=True),
    )(*gs)


def _add_pair(g, other, where, name):
    ns, a, b = g.shape
    half = a // 2

    def body(w_ref, g_ref, o_ref, pb_ref, own_ref):
        t = g_ref[0].astype(F32) + o_ref[0].astype(F32)
        pb_ref[0] = _b(t)

        @pl.when(pl.program_id(0) == w_ref[1])
        def _():
            own_ref[...] = t

    return pl.pallas_call(
        body, name=name,
        grid_spec=pltpu.PrefetchScalarGridSpec(
            num_scalar_prefetch=1, grid=(ns,),
            in_specs=[pl.BlockSpec((1, half, b), lambda s, w: (s, w[0], 0)), pl.BlockSpec((1, half, b), lambda s, w: (s, 0, 0))],
            out_specs=[pl.BlockSpec((1, half, b), lambda s, w: (s, 0, 0)), pl.BlockSpec((half, b), lambda s, w: (0, 0))]),
        out_shape=[jax.ShapeDtypeStruct((ns, half, b), BF16), jax.ShapeDtypeStruct((half, b), F32)],
        compiler_params=_cp("arbitrary"),
    )(where, g, other)


def _add_four(own, got, name):
    rows, cols = own.shape
    rb = rows // 2

    def body(a_ref, b_ref, o_ref):
        o_ref[...] = ((a_ref[...] + b_ref[0].astype(F32)) + b_ref[1].astype(F32)) + b_ref[2].astype(F32)

    return pl.pallas_call(
        body, name=name, grid=(rows // rb,),
        in_specs=[pl.BlockSpec((rb, cols), lambda i: (i, 0)), pl.BlockSpec((3, rb, cols), lambda i: (0, i, 0))],
        out_specs=pl.BlockSpec((rb, cols), lambda i: (i, 0)),
        out_shape=jax.ShapeDtypeStruct((rows, cols), F32),
        compiler_params=_cp("parallel"),
    )(own, got)


def _rs_swap(halves, name):
    n = len(halves)

    def body(*refs):
        send_sems, recv_sems = refs[2 * n:]
        x, y, c = _me()
        cps = [pltpu.make_async_remote_copy(src_ref=refs[i], dst_ref=refs[n + i], send_sem=send_sems.at[i],
                                            recv_sem=recv_sems.at[i], device_id=(x, y, 1 - c), device_id_type=MESH)
               for i in range(n)]
        for cp in cps:
            cp.start()
        for cp in cps:
            cp.wait()

    return pl.pallas_call(
        body, name=name, in_specs=[ANY] * n, out_specs=[ANY] * n,
        out_shape=[jax.ShapeDtypeStruct(h.shape, h.dtype) for h in halves],
        scratch_shapes=[pltpu.SemaphoreType.DMA((n,)), pltpu.SemaphoreType.DMA((n,))],
        compiler_params=pltpu.CompilerParams(has_side_effects=True),
    )(*halves)


def _adamw_halves(w, own, got, m, v, rows, name):
    a, b = w.shape
    nblk = a // 2 // rows
    c1 = 1.0 - ADAM_B1 ** ADAM_STEP
    c2 = 1.0 - ADAM_B2 ** ADAM_STEP

    def body(w_ref, own_ref, got_ref, m_ref, v_ref, g_ref, d_ref, mo_ref, vo_ref):
        g_ = jnp.where(pl.program_id(0) == lax.axis_index("c"), own_ref[...], got_ref[...])
        g_ref[...] = g_
        m_ = ADAM_B1 * m_ref[...] + (1.0 - ADAM_B1) * g_
        v_ = ADAM_B2 * v_ref[...] + (1.0 - ADAM_B2) * (g_ * g_)
        mo_ref[...] = m_
        vo_ref[...] = v_
        d_ref[...] = -ADAM_LR * ((m_ / c1) / (jnp.sqrt(v_ / c2) + ADAM_EPS) + ADAM_WD * w_ref[...])

    whole = pl.BlockSpec((rows, b), lambda h, i: (h * nblk + i, 0))
    part = pl.BlockSpec((rows, b), lambda h, i: (i, 0))
    shp = jax.ShapeDtypeStruct((a, b), F32)
    return pl.pallas_call(
        body, name=name, grid=(2, nblk),
        in_specs=[whole, part, part, whole, whole], out_specs=[whole] * 4, out_shape=[shp] * 4,
        compiler_params=_cp("parallel", "parallel"),
    )(w, own, got, m, v)


def _allsum_small(vec, name):
    def body(v_ref, o_ref, buf_ref, send_sems, recv_sems):
        x, y, c = _me()
        me = 4 * x + 2 * y + c
        buf_ref[me] = v_ref[...]
        cps = []
        for k in range(1, 8):
            peer = (x ^ (k >> 2), y ^ ((k >> 1) & 1), c ^ (k & 1))
            cps.append(pltpu.make_async_remote_copy(src_ref=v_ref, dst_ref=buf_ref.at[me],
                                                    send_sem=send_sems.at[k - 1], recv_sem=recv_sems.at[k - 1],
                                                    device_id=peer, device_id_type=MESH))
        for cp in cps:
            cp.start()
        for k in range(1, 8):
            peer_idx = me ^ k
            pltpu.make_async_remote_copy(src_ref=v_ref, dst_ref=buf_ref.at[peer_idx],
                                         send_sem=send_sems.at[k - 1], recv_sem=recv_sems.at[k - 1],
                                         device_id=(x, y, c), device_id_type=MESH).wait_recv()
        for cp in cps:
            cp.wait_send()
        acc = buf_ref[0]
        for d in range(1, 8):
            acc = acc + buf_ref[d]
        o_ref[...] = acc

    return pl.pallas_call(
        body, name=name,
        in_specs=[pl.BlockSpec(memory_space=pltpu.VMEM)], out_specs=pl.BlockSpec(memory_space=pltpu.VMEM),
        out_shape=jax.ShapeDtypeStruct(vec.shape, F32),
        scratch_shapes=[pltpu.VMEM((8,) + vec.shape, F32), pltpu.SemaphoreType.DMA((7,)), pltpu.SemaphoreType.DMA((7,))],
        compiler_params=pltpu.CompilerParams(has_side_effects=True),
    )(vec)


BIG = ("ffn1_w_gate", "ffn1_w_up", "ffn1_w_down", "w_in", "w_out", "ffn2_w_gate", "ffn2_w_up", "ffn2_w_down")
TINY = ("w_gla_gate", "conv_w")
SHARDED = BIG + TINY


def _join_cols(w4):
    return jnp.transpose(w4, (1, 0, 2)).reshape(w4.shape[1], N_SHARD * w4.shape[2])


def _cut_cols(w):
    return jnp.transpose(w.reshape(w.shape[0], N_SHARD, w.shape[1] // N_SHARD), (1, 0, 2))


def _split_w_in(w):
    o = IN_OFF
    big = jnp.concatenate([w[:, :o[4]], w[:, o[5]:o[9]], w[:, o[11]:]], axis=1)
    small = jnp.concatenate([w[:, o[4]:o[5]], w[:, o[9]:o[11]], jnp.zeros((w.shape[0], SMALL - 32), w.dtype)], axis=1)
    return big, small


def _merge_w_in(big, small):
    return jnp.concatenate([big[:, :3072], small[:, :16], big[:, 3072:7168], small[:, 16:32], big[:, 7168:]], axis=1)


class _Comm:
    def __init__(self, where, rest_shards):
        self.where = where
        self.w_in = _GatherBig(rest_shards[:1])
        self.later = _GatherBig(rest_shards[1:])
        self.pairs, self.got = {}, {}

    @staticmethod
    def w_in_weights(gathered):
        return dict(zip(("w_in_big", "w_in_small"), _split_w_in(_join_cols(gathered[0]))))

    @staticmethod
    def later_weights(gathered):
        W = dict(zip(BIG[4:], gathered))
        W["w_out"] = W["w_out"].reshape(D, D)
        return W

    def pair(self, names, grads, from_sibling):
        for n, g, o in zip(names, grads, from_sibling):
            self.pairs[n] = _add_pair(g, o, self.where, "rs_pair_" + n)
        return _ChipsExchange([self.pairs[n][0] for n in names])

    def begin(self, names, grads):
        return self.pair(names, grads, _rs_sibling(grads, "rs_sibling_" + names[0]))

    def landed(self, names, outs):
        self.got.update(zip(names, outs))


def _local_step(x, target, W, P, comm=None):
    wgate_pad = jnp.zeros((SMALL, GLA_H * GLA_DK), F32).at[:GLA_RANK].set(P["w_gla_gate"])
    cw8 = jnp.zeros((8, CONV_C), F32).at[:CONV_K].set(P["conv_w"])
    par = jnp.zeros((DN_H, 8, 128), F32)
    par = par.at[:, 0, :].set(jnp.broadcast_to(P["dn_a_log"].reshape(DN_H, 1), (DN_H, 128)))
    par = par.at[:, 1, :].set(jnp.broadcast_to(P["dn_dt_bias"].reshape(DN_H, 1), (DN_H, 128)))

    h1, n1, g1, u1, *got = _ffn_fwd(x, P["ffn1_norm"], W["ffn1_w_gate"], W["ffn1_w_up"], W["ffn1_w_down"], "ffn1_fwd",
                                    carry=comm.w_in if comm else None)
    if comm:
        W = dict(W, **comm.w_in_weights(got))
    wbig, wsmall = W["w_in_big"], W["w_in_small"]
    pbig, psmall, n2, qk, *got = _norm_proj(h1, P["mix_norm"], wbig, wsmall, "mix_proj",
                                            carry=comm.later if comm else None)
    if comm:
        W = dict(W, **comm.later_weights(got))
    oa, sa = _gla_fwd(pbig, qk, psmall, wgate_pad, P["b_gla_gate"], "gla_fwd")
    conv = _conv_fwd(pbig, cw8, "conv_fwd")
    ob, sb, *solved = _gdn_fwd(conv, psmall, par, "gdn_fwd")
    h2, yb = _merge_fwd(h1, oa, ob, pbig, P["gla_head_norm"], P["dn_head_norm"], W["w_out"], "merge_fwd")
    dh3, n3, g3, u3, loss, d_final = _ffn_fwd(h2, P["ffn2_norm"], W["ffn2_w_gate"], W["ffn2_w_up"], W["ffn2_w_down"],
                                              "ffn2_fwd", head=(P["final_norm"], target))

    gw, gs = {}, {"final_norm": d_final}

    def ffn_grads(tag, dh, h, n, g, u, before=None, later=False):
        names = tuple(tag + s for s in ("_w_gate", "_w_up", "_w_down"))
        dg, du, act, dfb, *landed = _ffn_bwd_hidden(dh, g, u, W[names[2]], tag + "_bwd_hidden", carry=before)
        gw[names[0]] = _mm_tn(n, dg, D, FF_CUT, tag + "_dwg", tk=4096)
        gw[names[1]] = _mm_tn(n, du, D, FF_CUT, tag + "_dwu", tk=4096)
        gw[names[2]] = _mm_tn(act, dfb, FF_CUT, D, tag + "_dwd", tk=4096)
        mine = [gw[n] for n in names]
        ex = None if not comm else _SiblingExchange(mine) if later else comm.begin(names, mine)
        dx, gs[tag + "_norm"], *own = _ffn_bwd_input(dh, h, P[tag + "_norm"], dg, du, W[names[0]], W[names[1]],
                                                     tag + "_bwd_input", carry=ex)
        if comm and not later:
            comm.landed(names, own)
        return dx, landed, own

    second = ("ffn2_w_gate", "ffn2_w_up", "ffn2_w_down")
    dh2, _, swapped = ffn_grads("ffn2", dh3, h2, n3, g3, u3, later=True)
    d_oa, d_ob, d_gr, d_dgate, d_ma, d_mb, gs["gla_head_norm"], gs["dn_head_norm"], dh2b, *landed = _merge_bwd(
        dh2, oa, ob, pbig, P["gla_head_norm"], P["dn_head_norm"], W["w_out"], "merge_bwd",
        carry=comm.pair(second, [gw[n] for n in second], swapped) if comm else None)
    if comm:
        comm.landed(second, landed)
    gw["w_out"] = _mm_tn(yb, dh2b, D, D, "dw_out").reshape(N_SHARD, D // N_SHARD, D)
    early = ("w_out",)
    d_gq, d_gk, d_gv, dpre, *swapped = _gla_bwd(pbig, qk, psmall, wgate_pad, P["b_gla_gate"], sa, d_oa, "gla_bwd",
                                                carry=_SiblingExchange([gw["w_out"]]) if comm else None)
    dcq, dck, dcv, dsm, dpar, *landed = _gdn_bwd(conv, psmall, par, sb, d_ob, solved, "gdn_bwd",
                                                 carry=comm.pair(early, [gw["w_out"]], swapped) if comm else None)
    if comm:
        comm.landed(early, landed)
    dsmall, dwgate, gs["b_gla_gate"] = _gla_gate_bwd(dpre, psmall, wgate_pad, dsm, "gla_gate_bwd")
    gs["w_gla_gate"] = dwgate[:GLA_RANK]
    d_x3, dcw = _conv_bwd(dcq, dck, dcv, pbig, cw8, "conv_bwd")
    gs["conv_w"] = dcw[:CONV_K]
    gs["dn_a_log"] = dpar[:, 0, 0].reshape(1, DN_H)
    gs["dn_dt_bias"] = dpar[:, 0, 1].reshape(1, DN_H)
    pieces = (d_gq, d_gk, d_gv, d_gr, d_x3, d_dgate, d_ma, d_mb)
    dh1, gs["mix_norm"] = _proj_bwd(dh2, h1, P["mix_norm"], pieces, dsmall, wbig, wsmall, "proj_bwd")
    dbig = jnp.concatenate([_mm_tn(n2, p, D, 1024, "dw_in_%d" % i) for i, p in enumerate(pieces)], axis=1)
    dsml = _mm_tn(n2, dsmall, D, SMALL, "dw_in_small")
    gw["w_in"] = _cut_cols(_merge_w_in(dbig, dsml))
    grad_x, landed, _ = ffn_grads("ffn1", dh1, x, n1, g1, u1,
                                  before=comm.begin(("w_in",), [gw["w_in"]]) if comm else None)
    if comm:
        comm.landed(("w_in",), landed)
    return loss, grad_x, gw, gs


SMALL_NAMES = ("ffn1_norm", "mix_norm", "ffn2_norm", "final_norm", "b_gla_gate", "gla_head_norm", "dn_head_norm",
               "dn_a_log", "dn_dt_bias")
ROW4 = (("b_gla_gate", 512), ("gla_head_norm", 256), ("dn_head_norm", 128), ("dn_a_log", 8), ("dn_dt_bias", 8))


def _pack_small(d, loss=None):
    row4 = [d[n].reshape(-1) for n, _ in ROW4]
    row4.append(jnp.zeros((1,), F32) if loss is None else loss.reshape(1))
    row4 = jnp.concatenate(row4)
    row4 = jnp.pad(row4, (0, D - row4.shape[0]))
    rows = [d[n].reshape(-1) for n in SMALL_NAMES[:4]] + [row4]
    return jnp.concatenate([jnp.stack(rows), jnp.zeros((3, D), F32)], axis=0)


def _unpack_small(a, like):
    out = {n: a[i].reshape(like[n].shape) for i, n in enumerate(SMALL_NAMES[:4])}
    off = 0
    for n, w in ROW4:
        out[n] = a[4, off:off + w].reshape(like[n].shape)
        off += w
    return out, a[4, off]


WEIGHT_ORDER = ("ffn1_norm", "ffn1_w_gate", "ffn1_w_up", "ffn1_w_down", "mix_norm", "w_in", "w_gla_gate", "b_gla_gate",
                "conv_w", "dn_a_log", "dn_dt_bias", "gla_head_norm", "dn_head_norm", "w_out", "ffn2_norm",
                "ffn2_w_gate", "ffn2_w_up", "ffn2_w_down", "final_norm")
ADAM_ROWS = {"ffn1_w_gate": 256, "ffn1_w_up": 256, "ffn1_w_down": 176, "w_in": 128, "w_gla_gate": 16, "conv_w": 4,
             "w_out": 64, "ffn2_w_gate": 256, "ffn2_w_up": 256, "ffn2_w_down": 176}


def kernel(x, ffn1_norm, ffn1_w_gate, ffn1_w_up, ffn1_w_down, mix_norm, w_in, w_gla_gate, b_gla_gate, conv_w, dn_a_log, dn_dt_bias, gla_head_norm, dn_head_norm, w_out, ffn2_norm, ffn2_w_gate, ffn2_w_up, ffn2_w_down, final_norm, loss_target, m_ffn1_norm, m_ffn1_w_gate, m_ffn1_w_up, m_ffn1_w_down, m_mix_norm, m_w_in, m_w_gla_gate, m_b_gla_gate, m_conv_w, m_dn_a_log, m_dn_dt_bias, m_gla_head_norm, m_dn_head_norm, m_w_out, m_ffn2_norm, m_ffn2_w_gate, m_ffn2_w_up, m_ffn2_w_down, m_final_norm, v_ffn1_norm, v_ffn1_w_gate, v_ffn1_w_up, v_ffn1_w_down, v_mix_norm, v_w_in, v_w_gla_gate, v_b_gla_gate, v_conv_w, v_dn_a_log, v_dn_dt_bias, v_gla_head_norm, v_dn_head_norm, v_w_out, v_ffn2_norm, v_ffn2_w_gate, v_ffn2_w_up, v_ffn2_w_down, v_final_norm):
    given = dict(locals())
    wts = {n: given[n] for n in WEIGHT_ORDER}
    moms = {n: given["m_" + n] for n in WEIGHT_ORDER}
    vars_ = {n: given["v_" + n] for n in WEIGHT_ORDER}
    two_d = lambda a: a.reshape(a.shape[-2], a.shape[-1]) if a.ndim == 3 else a.reshape(1, -1)
    shard = {n: two_d(wts[n]) for n in SHARDED}

    gathered = _gather_weights([shard[n].astype(BF16) for n in BIG[:3]], [shard[n] for n in TINY], "gather_first")
    W = dict(zip(BIG[:3], gathered))
    P = {n: two_d(wts[n]) for n in SMALL_NAMES}
    for n, g in zip(TINY, gathered[3:]):
        P[n] = _join_cols(g)

    my_slot = 2 * lax.axis_index("x") + lax.axis_index("y")
    where = jnp.stack([lax.axis_index("c"), my_slot]).astype(jnp.int32)
    comm = _Comm(where, [shard[n].astype(BF16) for n in BIG[3:]])
    loss, grad_x, gw, gs = _local_step(x[0], loss_target[0], W, P, comm)
    halves = [_add_four(comm.pairs[n][1], comm.got[n], "rs_four_" + n) for n in BIG]
    other_halves = _rs_swap(halves, "rs_swap")

    tiny_rows = jnp.concatenate([gs["w_gla_gate"].reshape(8, D), gs["conv_w"].reshape(12, D), jnp.zeros((4, D), F32)])
    all_sum = _allsum_small(jnp.concatenate([_pack_small(gs, loss[0, 0]), tiny_rows]), "allsum_small")
    small_sum = all_sum[:8]
    small_g, loss_total = _unpack_small(small_sum, P)

    grads, delta, new_m, new_v = {}, {}, {}, {}
    for n, own, got in zip(BIG, halves, other_halves):
        res = _adamw_halves(shard[n], own, got, two_d(moms[n]), two_d(vars_[n]), ADAM_ROWS[n], "adamw_" + n)
        grads[n], delta[n], new_m[n], new_v[n] = (t.reshape(wts[n].shape) for t in res)
    for n, rows in (("w_gla_gate", all_sum[8:16]), ("conv_w", all_sum[16:28])):
        cols = shard[n].shape[1]
        grads[n] = lax.dynamic_slice_in_dim(rows.reshape(shard[n].shape[0], N_SHARD * cols), my_slot * cols, cols, axis=1)
        d, m_, v_ = _adamw(shard[n], grads[n], two_d(moms[n]), two_d(vars_[n]), ADAM_ROWS[n], "adamw_" + n)
        delta[n], new_m[n], new_v[n] = (t.reshape(wts[n].shape) for t in (d, m_, v_))
    pk = lambda src: _pack_small({n: two_d(src[n]) for n in SMALL_NAMES})
    sd, sm_, sv_ = _adamw(pk(wts), small_sum, pk(moms), pk(vars_), 8, "adamw_small")
    for res, dst in ((sd, delta), (sm_, new_m), (sv_, new_v)):
        u, _ = _unpack_small(res, wts)
        dst.update(u)
    grad_w = {n: grads[n].reshape(wts[n].shape) for n in SHARDED}
    grad_w.update({n: small_g[n].reshape(wts[n].shape) for n in SMALL_NAMES})
    return (loss_total, grad_x[None], *[grad_w[n] for n in WEIGHT_ORDER], *[delta[n] for n in WEIGHT_ORDER],
            *[new_m[n] for n in WEIGHT_ORDER], *[new_v[n] for n in WEIGHT_ORDER])
```

```python
import functools
import math

import numpy as np
import jax
import jax.numpy as jnp
from jax import lax
from jax.experimental import pallas as pl
from jax.experimental.pallas import tpu as pltpu

F32 = jnp.float32
BF16 = jnp.bfloat16
HI = lax.Precision.HIGH
MESH = pl.DeviceIdType.MESH
ANY = pl.BlockSpec(memory_space=pl.ANY)

EPS = 1e-6
D = 1024
DFF = 2816
FFN_RES = 0.5
GLA_H, GLA_DK, GLA_DV, GLA_RANK, GLA_TAU = 4, 128, 256, 16, 16.0
DN_H, DN_DK, DN_DV = 8, 128, 128
CONV_K = 4
CHUNK = 64
N_SHARD = 4
FF_CUT = DFF // N_SHARD
ADAM_LR, ADAM_B1, ADAM_B2, ADAM_EPS, ADAM_WD, ADAM_STEP = 0.001, 0.9, 0.999, 1e-08, 0.01, 10

IN_SIZES = (512, 512, 1024, 1024, 16, 1024, 1024, 1024, 1024, 8, 8, 1024, 1024)
IN_OFF = tuple(int(v) for v in np.cumsum((0,) + IN_SIZES))
BIG_COLS = 9216
SMALL = 128
PIECES = (512, 512, 1024, 1024, 3072, 1024, 1024, 1024)

VMEM_LIMIT = 56 * 1024 * 1024
ROW_BLK = 256
BIG_ROW_BLK = 512
ATT_BLK = 512
GDN_BLK = 256
GDN_HEADS = 8


def _cp(*sem):
    return pltpu.CompilerParams(dimension_semantics=sem, vmem_limit_bytes=VMEM_LIMIT)


def _sigmoid(x):
    return 1.0 / (1.0 + jnp.exp(-x))


def _softplus(x):
    return jnp.maximum(x, 0.0) + jnp.log(1.0 + jnp.exp(-jnp.abs(x)))


def _log_sigmoid(x):
    return jnp.minimum(x, 0.0) - jnp.log(1.0 + jnp.exp(-jnp.abs(x)))


def _dot(a, b, prec=None):
    return jnp.dot(a, b, preferred_element_type=F32, precision=prec)


def _dot_nt(a, b, prec=None):
    return lax.dot_general(a, b, (((1,), (1,)), ((), ())), preferred_element_type=F32, precision=prec)


def _dot_tn(a, b, prec=None):
    return lax.dot_general(a, b, (((0,), (0,)), ((), ())), preferred_element_type=F32, precision=prec)


def _b(x):
    return x.astype(BF16)


def _iota2(n, m, axis):
    return lax.broadcasted_iota(jnp.int32, (n, m), axis)


def _load_weights(pairs, sem):
    copies = [pltpu.make_async_copy(s, d, sem.at[i]) for i, (s, d) in enumerate(pairs)]
    for c in copies:
        c.start()
    for c in copies:
        c.wait()


def _ffn_fwd(h, nw, wg, wu, wd, name, carry=None, head=None):
    T = h.shape[0]
    tm = min(BIG_ROW_BLK, T)
    ex_in, ex_args, ex_out, ex_shape, ex_sems = _carry_specs(carry)
    n_head = 2 if head else 0

    def body(*refs):
        h_ref, nw_ref, wg_hbm, wu_hbm, wd_hbm = refs[:5]
        ho_ref, n_ref, g_ref, u_ref = refs[5 + n_head:9 + n_head]
        wg_v, wu_v, wd_v, sem = refs[9 + 2 * n_head:]

        @pl.when(pl.program_id(0) == 0)
        def _():
            _load_weights(((wg_hbm, wg_v), (wu_hbm, wu_v), (wd_hbm, wd_v)), sem)

        x = h_ref[...]
        r = lax.rsqrt(jnp.mean(x * x, axis=-1, keepdims=True) + EPS)
        nb = _b((x * r) * nw_ref[...])
        n_ref[...] = nb
        acc = jnp.zeros((tm, D), F32)
        for s in range(N_SHARD):
            g = _dot(nb, wg_v[s])
            u = _dot(nb, wu_v[s])
            g_ref[s] = _b(g)
            u_ref[s] = _b(u)
            acc += _dot(_b(g * _sigmoid(g) * u), wd_v[s])
        out = x + FFN_RES * acc
        if not head:
            ho_ref[...] = out
            return
        fw_ref, t_ref = refs[5:7]
        loss_ref, dfw_ref = refs[11:13]

        @pl.when(pl.program_id(0) == 0)
        def _():
            loss_ref[...] = jnp.zeros_like(loss_ref)
            dfw_ref[...] = jnp.zeros_like(dfw_ref)

        w = fw_ref[...]
        r = lax.rsqrt(jnp.mean(out * out, axis=-1, keepdims=True) + EPS)
        xhat = out * r
        err = xhat * w - t_ref[...]
        loss_ref[...] += (0.5 / D) * jnp.sum(jnp.sum(err * err, axis=-1, keepdims=True), axis=0, keepdims=True)
        dout = err * (1.0 / D)
        dfw_ref[...] += jnp.sum(dout * xhat, axis=0, keepdims=True)
        dxhat = dout * w
        ho_ref[...] = r * (dxhat - xhat * jnp.mean(dxhat * xhat, axis=-1, keepdims=True))

    row = lambda w: pl.BlockSpec((tm, w), lambda i: (i, 0))
    one = pl.BlockSpec((1, D), lambda i: (0, 0))
    cut = pl.BlockSpec((N_SHARD, tm, FF_CUT), lambda i: (0, i, 0))
    head_out = [pl.BlockSpec((8, 128), lambda i: (0, 0)), one] if head else []
    head_shape = [jax.ShapeDtypeStruct((8, 128), F32), jax.ShapeDtypeStruct((1, D), F32)] if head else []
    return pl.pallas_call(
        _carry(carry, body, 5 + n_head, 4 + n_head, (T // tm,)), name=name, grid=(T // tm,),
        in_specs=[row(D), one, ANY, ANY, ANY] + ([one, row(D)] if head else []) + ex_in,
        out_specs=[row(D), row(D), cut, cut] + head_out + ex_out,
        out_shape=[jax.ShapeDtypeStruct((T, D), F32), jax.ShapeDtypeStruct((T, D), BF16),
                   jax.ShapeDtypeStruct((N_SHARD, T, FF_CUT), BF16),
                   jax.ShapeDtypeStruct((N_SHARD, T, FF_CUT), BF16)] + head_shape + ex_shape,
        scratch_shapes=[pltpu.VMEM((N_SHARD, D, FF_CUT), BF16), pltpu.VMEM((N_SHARD, D, FF_CUT), BF16),
                        pltpu.VMEM((N_SHARD, FF_CUT, D), BF16), pltpu.SemaphoreType.DMA((3,))] + ex_sems,
        compiler_params=_cp("arbitrary"),
    )(h, nw, wg, wu, wd, *(head or ()), *ex_args)


def _ffn_bwd_hidden(dh, g, u, wd, name, carry=None):
    T = dh.shape[0]
    tm = min(BIG_ROW_BLK, T)
    ex_in, ex_args, ex_out, ex_shape, ex_sems = _carry_specs(carry)

    def body(dh_ref, g_ref, u_ref, wd_hbm, dg_ref, du_ref, a_ref, df_ref, wd_v, sem):
        @pl.when(pl.program_id(0) == 0)
        def _():
            _load_weights(((wd_hbm, wd_v),), sem)

        dfb = _b(FFN_RES * dh_ref[...])
        df_ref[...] = dfb
        for s in range(N_SHARD):
            da = _dot_nt(dfb, wd_v[s])
            gg = g_ref[s].astype(F32)
            uu = u_ref[s].astype(F32)
            sg = _sigmoid(gg)
            silu = gg * sg
            a_ref[s] = _b(silu * uu)
            dg_ref[s] = _b(da * uu * (sg * (1.0 + gg * (1.0 - sg))))
            du_ref[s] = _b(da * silu)

    row = pl.BlockSpec((tm, D), lambda i: (i, 0))
    cut = pl.BlockSpec((N_SHARD, tm, FF_CUT), lambda i: (0, i, 0))
    cut_shape = jax.ShapeDtypeStruct((N_SHARD, T, FF_CUT), BF16)
    return pl.pallas_call(
        _carry(carry, body, 4, 4, (T // tm,)), name=name, grid=(T // tm,),
        in_specs=[row, cut, cut, ANY] + ex_in,
        out_specs=[cut, cut, cut, row] + ex_out,
        out_shape=[cut_shape, cut_shape, cut_shape, jax.ShapeDtypeStruct((T, D), BF16)] + ex_shape,
        scratch_shapes=[pltpu.VMEM((N_SHARD, FF_CUT, D), BF16), pltpu.SemaphoreType.DMA((1,))] + ex_sems,
        compiler_params=_cp("arbitrary"),
    )(dh, g, u, wd, *ex_args)


def _ffn_bwd_input(dh, h, nw, dg, du, wg, wu, name, carry=None):
    T = h.shape[0]
    tm = min(BIG_ROW_BLK, T)
    ex_in, ex_args, ex_out, ex_shape, ex_sems = _carry_specs(carry)

    def body(dh_ref, h_ref, nw_ref, dg_ref, du_ref, wg_hbm, wu_hbm, dx_ref, dnw_ref, wg_v, wu_v, sem):
        @pl.when(pl.program_id(0) == 0)
        def _():
            _load_weights(((wg_hbm, wg_v), (wu_hbm, wu_v)), sem)
            dnw_ref[...] = jnp.zeros_like(dnw_ref)

        dn = jnp.zeros((tm, D), F32)
        for s in range(N_SHARD):
            dn += _dot_nt(dg_ref[s], wg_v[s]) + _dot_nt(du_ref[s], wu_v[s])
        x = h_ref[...]
        r = lax.rsqrt(jnp.mean(x * x, axis=-1, keepdims=True) + EPS)
        xhat = x * r
        dnw_ref[...] += jnp.sum(dn * xhat, axis=0, keepdims=True)
        dxhat = dn * nw_ref[...]
        dx_ref[...] = dh_ref[...] + r * (dxhat - xhat * jnp.mean(dxhat * xhat, axis=-1, keepdims=True))

    row = pl.BlockSpec((tm, D), lambda i: (i, 0))
    one = pl.BlockSpec((1, D), lambda i: (0, 0))
    cut = pl.BlockSpec((N_SHARD, tm, FF_CUT), lambda i: (0, i, 0))
    return pl.pallas_call(
        _carry(carry, body, 7, 2, (T // tm,)), name=name, grid=(T // tm,),
        in_specs=[row, row, one, cut, cut, ANY, ANY] + ex_in,
        out_specs=[row, one] + ex_out,
        out_shape=[jax.ShapeDtypeStruct((T, D), F32), jax.ShapeDtypeStruct((1, D), F32)] + ex_shape,
        scratch_shapes=[pltpu.VMEM((N_SHARD, D, FF_CUT), BF16), pltpu.VMEM((N_SHARD, D, FF_CUT), BF16),
                        pltpu.SemaphoreType.DMA((2,))] + ex_sems,
        compiler_params=_cp("arbitrary"),
    )(dh, h, nw, dg, du, wg, wu, *ex_args)


def _mm_tn(a, b, bm, bn, name, out_dtype=BF16, tk=2048):
    cuts = a.shape[0] if a.ndim == 3 else (b.shape[0] if b.ndim == 3 else None)
    T, M = a.shape[-2:]
    N = b.shape[-1]
    tk = min(tk, T)
    bm, bn = min(bm, M), min(bn, N)
    nk = T // tk

    def body(a_ref, b_ref, o_ref, acc_ref):
        k = pl.program_id(3)

        @pl.when(k == 0)
        def _():
            acc_ref[...] = jnp.zeros_like(acc_ref)

        av = a_ref[0] if a.ndim == 3 else a_ref[...]
        bv = b_ref[0] if b.ndim == 3 else b_ref[...]
        acc_ref[...] += _dot_tn(_b(av), _b(bv))

        @pl.when(k == nk - 1)
        def _():
            res = acc_ref[...].astype(out_dtype)
            if cuts is None:
                o_ref[...] = res
            else:
                o_ref[0] = res

    a_spec = (pl.BlockSpec((1, tk, bm), lambda s, i, j, k: (s, k, i)) if a.ndim == 3
              else pl.BlockSpec((tk, bm), lambda s, i, j, k: (k, i)))
    b_spec = (pl.BlockSpec((1, tk, bn), lambda s, i, j, k: (s, k, j)) if b.ndim == 3
              else pl.BlockSpec((tk, bn), lambda s, i, j, k: (k, j)))
    if cuts is None:
        o_spec, o_shape = pl.BlockSpec((bm, bn), lambda s, i, j, k: (i, j)), (M, N)
    else:
        o_spec, o_shape = pl.BlockSpec((1, bm, bn), lambda s, i, j, k: (s, i, j)), (cuts, M, N)
    return pl.pallas_call(
        body, name=name, grid=(cuts or 1, M // bm, N // bn, nk),
        in_specs=[a_spec, b_spec], out_specs=o_spec,
        out_shape=jax.ShapeDtypeStruct(o_shape, out_dtype),
        scratch_shapes=[pltpu.VMEM((bm, bn), F32)],
        compiler_params=_cp("parallel", "parallel", "parallel", "arbitrary"),
    )(a, b)


def _norm_proj(h, nw, wbig, wsmall, name, carry=None):
    T = h.shape[0]
    tm = min(512, T)
    tn = 1536
    qk = 2 * GLA_H * GLA_DK
    ex_in, ex_args, ex_out, ex_shape, ex_sems = _carry_specs(carry)

    def body(h_ref, nw_ref, wb_hbm, ws_ref, pb_ref, ps_ref, n_ref, qk_ref, wb_v, sem):
        @pl.when(pl.program_id(0) == 0)
        def _():
            _load_weights(((wb_hbm, wb_v),), sem)

        x = h_ref[...]
        r = lax.rsqrt(jnp.mean(x * x, axis=-1, keepdims=True) + EPS)
        nb = _b((x * r) * nw_ref[...])
        n_ref[...] = nb
        ps_ref[...] = _dot(nb, ws_ref[...])
        for j in range(BIG_COLS // tn):
            part = _dot(nb, wb_v[:, j * tn:(j + 1) * tn])
            pb_ref[:, j * tn:(j + 1) * tn] = _b(part)
            if j == 0:
                qk_ref[...] = part[:, :qk]

    row = lambda w: pl.BlockSpec((tm, w), lambda i: (i, 0))
    return pl.pallas_call(
        _carry(carry, body, 4, 4, (T // tm,)), name=name, grid=(T // tm,),
        in_specs=[row(D), pl.BlockSpec((1, D), lambda i: (0, 0)), ANY, pl.BlockSpec((D, SMALL), lambda i: (0, 0))]
        + ex_in,
        out_specs=[row(BIG_COLS), row(SMALL), row(D), row(qk)] + ex_out,
        out_shape=[jax.ShapeDtypeStruct((T, BIG_COLS), BF16), jax.ShapeDtypeStruct((T, SMALL), F32),
                   jax.ShapeDtypeStruct((T, D), BF16), jax.ShapeDtypeStruct((T, qk), F32)] + ex_shape,
        scratch_shapes=[pltpu.VMEM((D, BIG_COLS), BF16), pltpu.SemaphoreType.DMA((1,))] + ex_sems,
        compiler_params=_cp("arbitrary"),
    )(h, nw, wbig, wsmall, *ex_args)


def _proj_bwd(dh, h, nw, pieces, dsmall, wbig, wsmall, name):
    T = h.shape[0]
    tm = min(BIG_ROW_BLK, T)
    offs = tuple(int(v) for v in np.cumsum((0,) + PIECES))

    def body(dh_ref, h_ref, nw_ref, *rest):
        p_refs = rest[:len(PIECES)]
        ds_ref, wb_hbm, ws_ref, dx_ref, dnw_ref, wb_v, sem = rest[len(PIECES):]

        @pl.when(pl.program_id(0) == 0)
        def _():
            _load_weights(((wb_hbm, wb_v),), sem)
            dnw_ref[...] = jnp.zeros_like(dnw_ref)

        dn = _dot_nt(_b(ds_ref[...]), ws_ref[...])
        for p_ref, lo, wdt in zip(p_refs, offs, PIECES):
            dn += _dot_nt(p_ref[...], wb_v[:, lo:lo + wdt])
        x = h_ref[...]
        r = lax.rsqrt(jnp.mean(x * x, axis=-1, keepdims=True) + EPS)
        xhat = x * r
        dnw_ref[...] += jnp.sum(dn * xhat, axis=0, keepdims=True)
        dxhat = dn * nw_ref[...]
        dx_ref[...] = dh_ref[...] + r * (dxhat - xhat * jnp.mean(dxhat * xhat, axis=-1, keepdims=True))

    row = lambda w: pl.BlockSpec((tm, w), lambda i: (i, 0))
    one = pl.BlockSpec((1, D), lambda i: (0, 0))
    return pl.pallas_call(
        body, name=name, grid=(T // tm,),
        in_specs=[row(D), row(D), one] + [row(w) for w in PIECES] + [row(SMALL), ANY, pl.BlockSpec((D, SMALL), lambda i: (0, 0))],
        out_specs=[row(D), one],
        out_shape=[jax.ShapeDtypeStruct((T, D), F32), jax.ShapeDtypeStruct((1, D), F32)],
        scratch_shapes=[pltpu.VMEM((D, BIG_COLS), BF16), pltpu.SemaphoreType.DMA((1,))],
        compiler_params=_cp("arbitrary"),
    )(dh, h, nw, *pieces, dsmall, wbig, wsmall)


def _gla_block(q_ref, k_ref, sm_ref, wg_ref, bg_ref, nc, tril):
    nbat = GLA_H * nc
    q = _heads_first(q_ref[...].astype(F32), nc, GLA_DK)
    k = _heads_first(k_ref[...].astype(F32), nc, GLA_DK)
    pre = _heads_first(_dot(sm_ref[...], wg_ref[...], HI) + bg_ref[...], nc, GLA_DK)
    la = _log_sigmoid(pre) * (1.0 / GLA_TAU)
    bc = _bmm(jnp.broadcast_to(tril, (nbat, CHUNK, CHUNK)), la, HI)
    bl = bc[:, CHUNK - 1:CHUNK, :]
    eb = jnp.exp(bc)
    enb = jnp.exp(-bc)
    ebl = jnp.exp(bl - bc)
    q_in = q * (GLA_DK ** -0.5) * eb
    k_out = k * enb
    k_st = k * ebl
    a_ch = jnp.exp(bl)
    return pre, eb, enb, ebl, q_in, k_out, k_st, a_ch


def _gla_specs(blk, idx):
    hk, hv = GLA_H * GLA_DK, GLA_H * GLA_DV
    return [pl.BlockSpec((blk, hk), lambda j: (idx(j), 0)),
            pl.BlockSpec((blk, hk), lambda j: (idx(j), 1)),
            pl.BlockSpec((blk, hv), lambda j: (idx(j), 1)),
            pl.BlockSpec((blk, SMALL), lambda j: (idx(j), 0)),
            pl.BlockSpec((SMALL, hk), lambda j: (0, 0)),
            pl.BlockSpec((1, hk), lambda j: (0, 0))]


def _gla_fwd(pbig, qk, psmall, wgate, bgate, name):
    T = pbig.shape[0]
    blk = min(ATT_BLK, T)
    nc = blk // CHUNK

    def body(q_ref, k_ref, v_ref, sm_ref, wg_ref, bg_ref, o_ref, ss_ref, st_ref):
        @pl.when(pl.program_id(0) == 0)
        def _():
            st_ref[...] = jnp.zeros_like(st_ref)

        causal = _iota2(CHUNK, CHUNK, 0) >= _iota2(CHUNK, CHUNK, 1)
        _, _, _, _, q_in, k_out, k_st, a_ch = _gla_block(q_ref, k_ref, sm_ref, wg_ref, bg_ref, nc, causal.astype(F32))
        v = _heads_first(v_ref[...], nc, GLA_DV)
        qb = _b(q_in)
        sc = jnp.where(causal, _bmm_nt(qb, _b(k_out)), 0.0)
        kv = _bmm_tn(v, _b(k_st))
        before = [None] * (GLA_H * nc)
        for i in range(GLA_H):
            st = st_ref[i]
            for c in range(nc):
                n = i * nc + c
                before[n] = st
                st = st * a_ch[n] + kv[n]
            st_ref[i] = st
        states = jnp.stack(before)
        ss_ref[...] = states.reshape(GLA_H, nc, GLA_DV, GLA_DK)
        o_ref[...] = _b(_heads_last(_bmm(_b(sc), v) + _bmm_nt(qb, _b(states)), nc))

    return pl.pallas_call(
        body, name=name, grid=(T // blk,),
        in_specs=_gla_specs(blk, lambda j: j),
        out_specs=[pl.BlockSpec((blk, GLA_H * GLA_DV), lambda j: (j, 0)),
                   pl.BlockSpec((GLA_H, nc, GLA_DV, GLA_DK), lambda j: (0, j, 0, 0))],
        out_shape=[jax.ShapeDtypeStruct((T, GLA_H * GLA_DV), BF16),
                   jax.ShapeDtypeStruct((GLA_H, T // CHUNK, GLA_DV, GLA_DK), F32)],
        scratch_shapes=[pltpu.VMEM((GLA_H, GLA_DV, GLA_DK), F32)],
        compiler_params=_cp("arbitrary"),
    )(qk, qk, pbig, psmall, wgate, bgate)


def _gla_bwd(pbig, qk, psmall, wgate, bgate, states, do, name, carry=None):
    ex_in, ex_args, ex_out, ex_shape, ex_sems = _carry_specs(carry)
    T = pbig.shape[0]
    blk = min(ATT_BLK, T)
    nc = blk // CHUNK
    nb = T // blk
    nbat = GLA_H * nc

    def body(q_ref, k_ref, v_ref, sm_ref, wg_ref, bg_ref, ss_ref, do_ref, dq_ref, dk_ref, dv_ref, dpre_ref, dst_ref):
        @pl.when(pl.program_id(0) == 0)
        def _():
            dst_ref[...] = jnp.zeros_like(dst_ref)

        causal = _iota2(CHUNK, CHUNK, 0) >= _iota2(CHUNK, CHUNK, 1)
        triu = (_iota2(CHUNK, CHUNK, 0) <= _iota2(CHUNK, CHUNK, 1)).astype(F32)
        pre, eb, enb, ebl, q_in, k_out, k_st, a_ch = _gla_block(q_ref, k_ref, sm_ref, wg_ref, bg_ref, nc,
                                                                causal.astype(F32))
        v = _heads_first(v_ref[...], nc, GLA_DV)
        dob = _b(_heads_first(do_ref[...], nc, GLA_DV))
        st = ss_ref[...].reshape(nbat, GLA_DV, GLA_DK)
        qb, kob, kstb = _b(q_in), _b(k_out), _b(k_st)
        qdo = _bmm_tn(dob, qb)
        after = [None] * nbat
        for i in range(GLA_H):
            dst = dst_ref[i]
            for c in range(nc - 1, -1, -1):
                n = i * nc + c
                after[n] = dst
                dst = dst * a_ch[n] + qdo[n]
            dst_ref[i] = dst
        dsa = jnp.stack(after)
        dsb = _b(dsa)
        sc = jnp.where(causal, _bmm_nt(qb, kob), 0.0)
        dsc = _b(jnp.where(causal, _bmm_nt(dob, v), 0.0))
        dq_in = _bmm(dob, _b(st)) + _bmm(dsc, kob)
        dk_out = _bmm_tn(dsc, qb)
        dk_st = _bmm(v, dsb)
        dv_ref[...] = _b(_heads_last(_bmm_tn(_b(sc), dob) + _bmm_nt(kstb, dsb), nc))
        da_ch = jnp.sum(st * dsa, axis=1, keepdims=True)
        tk = dk_st * k_st
        db = dq_in * q_in - dk_out * k_out - tk
        db_last = jnp.sum(tk, axis=1, keepdims=True) + da_ch * a_ch
        dq_ref[...] = _b(_heads_last(dq_in * (GLA_DK ** -0.5) * eb, nc))
        dk_ref[...] = _b(_heads_last(dk_out * enb + dk_st * ebl, nc))
        dla = _bmm(jnp.broadcast_to(triu, (nbat, CHUNK, CHUNK)), db, HI) + db_last
        dpre_ref[...] = _heads_last(dla * (1.0 / GLA_TAU) * _sigmoid(-pre), nc)

    r = lambda j: nb - 1 - j
    hk, hv = GLA_H * GLA_DK, GLA_H * GLA_DV
    return pl.pallas_call(
        _carry(carry, body, 8, 4, (nb,)), name=name, grid=(nb,),
        in_specs=_gla_specs(blk, r) + [pl.BlockSpec((GLA_H, nc, GLA_DV, GLA_DK), lambda j: (0, r(j), 0, 0)),
                                      pl.BlockSpec((blk, hv), lambda j: (r(j), 0))] + ex_in,
        out_specs=[pl.BlockSpec((blk, hk), lambda j: (r(j), 0)), pl.BlockSpec((blk, hk), lambda j: (r(j), 0)),
                   pl.BlockSpec((blk, hv), lambda j: (r(j), 0)), pl.BlockSpec((blk, hk), lambda j: (r(j), 0))] + ex_out,
        out_shape=[jax.ShapeDtypeStruct((T, hk), BF16), jax.ShapeDtypeStruct((T, hk), BF16),
                   jax.ShapeDtypeStruct((T, hv), BF16), jax.ShapeDtypeStruct((T, hk), F32)] + ex_shape,
        scratch_shapes=[pltpu.VMEM((GLA_H, GLA_DV, GLA_DK), F32)] + ex_sems,
        compiler_params=_cp("arbitrary"),
    )(qk, qk, pbig, psmall, wgate, bgate, states, do, *ex_args)


def _gla_gate_bwd(dpre, psmall, wgate, dsm, name):
    T = dpre.shape[0]
    tm = min(512, T)
    W = GLA_H * GLA_DK
    ngrp = dsm.shape[0]

    def body(dp_ref, sm_ref, wg_ref, dsm_ref, ds_ref, dw_ref, db_ref):
        @pl.when(pl.program_id(0) == 0)
        def _():
            dw_ref[...] = jnp.zeros_like(dw_ref)
            db_ref[...] = jnp.zeros_like(db_ref)

        dp = dp_ref[...]
        ds = _dot_nt(dp, wg_ref[...], HI)
        for i in range(ngrp):
            ds += dsm_ref[i]
        ds_ref[...] = ds
        dw_ref[...] += _dot_tn(sm_ref[...], dp, HI)
        db_ref[...] += jnp.sum(dp, axis=0, keepdims=True)

    return pl.pallas_call(
        body, name=name, grid=(T // tm,),
        in_specs=[pl.BlockSpec((tm, W), lambda i: (i, 0)), pl.BlockSpec((tm, SMALL), lambda i: (i, 0)),
                  pl.BlockSpec((SMALL, W), lambda i: (0, 0)), pl.BlockSpec((ngrp, tm, SMALL), lambda i: (0, i, 0))],
        out_specs=[pl.BlockSpec((tm, SMALL), lambda i: (i, 0)), pl.BlockSpec((SMALL, W), lambda i: (0, 0)),
                   pl.BlockSpec((1, W), lambda i: (0, 0))],
        out_shape=[jax.ShapeDtypeStruct((T, SMALL), F32), jax.ShapeDtypeStruct((SMALL, W), F32),
                   jax.ShapeDtypeStruct((1, W), F32)],
        compiler_params=_cp("arbitrary"),
    )(dpre, psmall, wgate, dsm)


CONV_C = 3 * 1024
CONV_BLK = 256


def _conv_fwd(pbig, cw8, name):
    T = pbig.shape[0]
    blk = min(CONV_BLK, T)

    def body(x_ref, w_ref, c_ref, prev_ref):
        @pl.when(pl.program_id(0) == 0)
        def _():
            prev_ref[...] = jnp.zeros_like(prev_ref)

        x = x_ref[...].astype(F32)
        prev = prev_ref[...]
        row8 = _iota2(8, CONV_C, 0)
        acc = x * w_ref[CONV_K - 1:CONV_K, :]
        for s in range(1, CONV_K):
            xs = pltpu.roll(x, s, 0)
            top = jnp.where(row8 < s, pltpu.roll(prev, s, 0), xs[:8])
            xs = jnp.concatenate([top, xs[8:]], axis=0)
            acc += xs * w_ref[CONV_K - 1 - s:CONV_K - s, :]
        c_ref[...] = _b(acc)
        prev_ref[...] = x[blk - 8:]

    return pl.pallas_call(
        body, name=name, grid=(T // blk,),
        in_specs=[pl.BlockSpec((blk, CONV_C), lambda i: (i, 1)), pl.BlockSpec((8, CONV_C), lambda i: (0, 0))],
        out_specs=pl.BlockSpec((blk, CONV_C), lambda i: (i, 0)),
        out_shape=jax.ShapeDtypeStruct((T, CONV_C), BF16),
        scratch_shapes=[pltpu.VMEM((8, CONV_C), F32)],
        compiler_params=_cp("arbitrary"),
    )(pbig, cw8)


def _conv_bwd(dcq, dck, dcv, pbig, cw8, name):
    T = pbig.shape[0]
    blk = min(CONV_BLK, T)
    nb = T // blk

    def body(dq_ref, dk_ref, dv_ref, x_ref, w_ref, dx_ref, dw_ref, nxt_ref):
        @pl.when(pl.program_id(0) == 0)
        def _():
            nxt_ref[...] = jnp.zeros_like(nxt_ref)
            dw_ref[...] = jnp.zeros_like(dw_ref)

        dc = jnp.concatenate([dq_ref[...], dk_ref[...], dv_ref[...]], axis=1).astype(F32)
        x = x_ref[...].astype(F32)
        nxt = nxt_ref[...]
        row8 = _iota2(8, CONV_C, 0)
        acc = dc * w_ref[CONV_K - 1:CONV_K, :]
        dws = [jnp.sum(dc * x, axis=0, keepdims=True)]
        for s in range(1, CONV_K):
            ds = pltpu.roll(dc, blk - s, 0)
            bot = jnp.where(row8 >= 8 - s, pltpu.roll(nxt, 8 - s, 0), ds[blk - 8:])
            ds = jnp.concatenate([ds[:blk - 8], bot], axis=0)
            acc += ds * w_ref[CONV_K - 1 - s:CONV_K - s, :]
            dws.append(jnp.sum(ds * x, axis=0, keepdims=True))
        dx_ref[...] = _b(acc)
        dw_ref[...] += jnp.concatenate(dws[::-1] + [jnp.zeros((8 - CONV_K, CONV_C), F32)], axis=0)
        nxt_ref[...] = dc[:8]

    part = pl.BlockSpec((blk, 1024), lambda i: (nb - 1 - i, 0))
    return pl.pallas_call(
        body, name=name, grid=(nb,),
        in_specs=[part, part, part, pl.BlockSpec((blk, CONV_C), lambda i: (nb - 1 - i, 1)),
                  pl.BlockSpec((8, CONV_C), lambda i: (0, 0))],
        out_specs=[pl.BlockSpec((blk, CONV_C), lambda i: (nb - 1 - i, 0)), pl.BlockSpec((8, CONV_C), lambda i: (0, 0))],
        out_shape=[jax.ShapeDtypeStruct((T, CONV_C), BF16), jax.ShapeDtypeStruct((8, CONV_C), F32)],
        scratch_shapes=[pltpu.VMEM((8, CONV_C), F32)],
        compiler_params=_cp("arbitrary"),
    )(dcq, dck, dcv, pbig, cw8)


def _col(x, lane):
    if isinstance(lane, int):
        return jnp.broadcast_to(x[:, lane:lane + 1], x.shape)
    sel = _iota2(x.shape[0], x.shape[1], 1) == lane
    return jnp.broadcast_to(jnp.sum(jnp.where(sel, x, 0.0), axis=1, keepdims=True), x.shape)


def _bmm(a, b, prec=None):
    return jnp.einsum("bij,bjk->bik", a, b, preferred_element_type=F32, precision=prec)


def _bmm_nt(a, b, prec=None):
    return jnp.einsum("bij,bkj->bik", a, b, preferred_element_type=F32, precision=prec)


def _bmm_tn(a, b, prec=None):
    return jnp.einsum("bji,bjk->bik", a, b, preferred_element_type=F32, precision=prec)


def _unit_lower_inverse(low):
    eye = (_iota2(CHUNK, CHUNK, 0) == _iota2(CHUNK, CHUNK, 1)).astype(F32)
    xk = -low
    inv = eye + xk
    for _ in range(5):
        xb = _b(xk)
        xk = _bmm(xb, xb)
        inv = inv + _bmm(_b(inv), _b(xk))
    resid = eye - _bmm(eye + low, inv, HI)
    return inv + _bmm(inv, resid, HI)


def _heads_first(x, nc, w=128):
    hb = x.shape[1] // w
    return jnp.concatenate([x[:, i * w:(i + 1) * w].reshape(nc, CHUNK, w) for i in range(hb)], axis=0)


def _heads_last(x, nc):
    hb = x.shape[0] // nc
    return jnp.concatenate([x[i * nc:(i + 1) * nc].reshape(nc * CHUNK, x.shape[2]) for i in range(hb)], axis=1)


def _gdn_block(cq_ref, ck_ref, cv_ref, sm_ref, par_ref, h0, hb, nc, masks, solved=None):
    causal, strict, tril, eye = masks
    nbat = hb * nc
    cq = _heads_first(cq_ref[...].astype(F32), nc)
    ck = _heads_first(ck_ref[...].astype(F32), nc)
    cv = _heads_first(cv_ref[...].astype(F32), nc)
    sq, sk, sv = _sigmoid(cq), _sigmoid(ck), _sigmoid(cv)
    q, k, v = cq * sq, ck * sk, cv * sv
    rq = lax.rsqrt(jnp.sum(q * q, axis=-1, keepdims=True) + EPS)
    rk = lax.rsqrt(jnp.sum(k * k, axis=-1, keepdims=True) + EPS)
    qh, kn = q * rq, k * rk
    qn = qh * (DN_DK ** -0.5)
    sm = sm_ref[...]
    per_head = lambda fn: jnp.concatenate([fn(i) for i in range(hb)], axis=0)
    braw = per_head(lambda i: _col(sm, GLA_RANK + h0 + i).reshape(nc, CHUNK, 128))
    araw = per_head(lambda i: _col(sm, GLA_RANK + DN_H + h0 + i).reshape(nc, CHUNK, 128))
    ea = per_head(lambda i: jnp.broadcast_to(jnp.exp(par_ref[i, 0:1, :])[None], (nc, 1, 128)))
    bias = per_head(lambda i: jnp.broadcast_to(par_ref[i, 1:2, :][None], (nc, 1, 128)))
    beta = _sigmoid(braw)
    sp_arg = araw + bias
    g = -ea * _softplus(sp_arg)
    G = _bmm(jnp.broadcast_to(tril, (nbat, CHUNK, CHUNK)), g, HI)
    gc = G[:, :, :CHUNK]
    grow = jnp.sum(eye * gc, axis=1, keepdims=True)
    decay = jnp.exp(jnp.where(causal, gc - grow, -1e30))
    kb = kn * beta
    A = _bmm_nt(_b(kb), _b(kn))
    eG = jnp.exp(G)
    gl = G[:, CHUNK - 1:CHUNK, :]
    eGl = jnp.exp(gl - G)
    g_ch = jnp.exp(gl)
    rv = v * beta
    rkk = kb * eG
    if solved is None:
        tinv_b = _b(_unit_lower_inverse(jnp.where(strict, A * decay, 0.0)))
        u = _bmm(tinv_b, _b(rv))
        w = _b(_bmm(tinv_b, _b(rkk)))
    else:
        tinv_b, u, w = solved
    B = _bmm_nt(_b(qn), _b(kn))
    qk = jnp.where(causal, B * decay, 0.0)
    q_dec = qn * eG
    k_st = kn * eGl
    return dict(cq=cq, ck=ck, cv=cv, sq=sq, sk=sk, sv=sv, q=q, k=k, v=v, rq=rq, rk=rk, qh=qh, kn=kn, qn=qn,
                beta=beta, ea=ea, sp_arg=sp_arg, g=g, G=G, decay=decay, kb=kb, A=A, tinv_b=tinv_b, eG=eG, eGl=eGl,
                g_ch=g_ch, rv=rv, rkk=rkk, u=u, w=w, B=B, qk=qk, q_dec=q_dec, k_st=k_st)


def _gdn_masks():
    r, c = _iota2(CHUNK, CHUNK, 0), _iota2(CHUNK, CHUNK, 1)
    return r >= c, r > c, (r >= c).astype(F32), (r == c).astype(F32)


def _gdn_specs(blk, hb, idx):
    ng = DN_H // hb
    return [pl.BlockSpec((blk, hb * DN_DK), lambda h, j: (idx(j), h)),
            pl.BlockSpec((blk, hb * DN_DK), lambda h, j: (idx(j), ng + h)),
            pl.BlockSpec((blk, hb * DN_DV), lambda h, j: (idx(j), 2 * ng + h)),
            pl.BlockSpec((blk, SMALL), lambda h, j: (idx(j), 0)),
            pl.BlockSpec((hb, 8, 128), lambda h, j: (h, 0, 0))]


def _gdn_solved_specs(blk, hb, idx):
    nc = blk // CHUNK
    spec = lambda w: pl.BlockSpec((hb, nc, CHUNK, w), lambda h, j: (h, idx(j), 0, 0))
    return [spec(CHUNK), spec(DN_DV), spec(DN_DK)]


def _gdn_fwd(conv, psmall, par, name):
    T = conv.shape[0]
    blk = min(GDN_BLK, T)
    nc = blk // CHUNK
    hb = GDN_HEADS
    N = T // CHUNK

    def body(cq_ref, ck_ref, cv_ref, sm_ref, par_ref, o_ref, ss_ref, ti_ref, u_ref, w_ref, s_ref):
        @pl.when(pl.program_id(1) == 0)
        def _():
            s_ref[...] = jnp.zeros_like(s_ref)

        h0 = 0 if hb == DN_H else pl.program_id(0) * hb
        f = _gdn_block(cq_ref, ck_ref, cv_ref, sm_ref, par_ref, h0, hb, nc, _gdn_masks())
        ti_ref[...] = f["tinv_b"].reshape(hb, nc, CHUNK, CHUNK)
        u_ref[...] = f["u"].reshape(hb, nc, CHUNK, DN_DV)
        w_ref[...] = f["w"].reshape(hb, nc, CHUNK, DN_DK)
        wb, ub, kstb, qkb = f["w"], _b(f["u"]), _b(f["k_st"]), _b(f["qk"])
        mix = _b(_bmm_tn(kstb, wb))
        add = _bmm_tn(kstb, ub)
        q_eff = _b(f["q_dec"] - _bmm(qkb, wb))
        before = [None] * (hb * nc)
        S = [s_ref[i] for i in range(hb)]
        for c in range(nc):
            for i in range(hb):
                n = i * nc + c
                before[n] = S[i]
                S[i] = S[i] * f["g_ch"][n] - _dot(mix[n], _b(S[i])) + add[n]
        for i in range(hb):
            s_ref[i] = S[i]
        states = jnp.stack(before)
        ss_ref[...] = states.reshape(hb, nc, DN_DK, DN_DV)
        o_ref[...] = _b(_heads_last(_bmm(qkb, ub) + _bmm(q_eff, _b(states)), nc))

    return pl.pallas_call(
        body, name=name, grid=(DN_H // hb, T // blk),
        in_specs=_gdn_specs(blk, hb, lambda j: j),
        out_specs=[pl.BlockSpec((blk, hb * DN_DV), lambda h, j: (j, h)),
                   pl.BlockSpec((hb, nc, DN_DK, DN_DV), lambda h, j: (h, j, 0, 0))]
        + _gdn_solved_specs(blk, hb, lambda j: j),
        out_shape=[jax.ShapeDtypeStruct((T, DN_H * DN_DV), BF16), jax.ShapeDtypeStruct((DN_H, N, DN_DK, DN_DV), F32),
                   jax.ShapeDtypeStruct((DN_H, N, CHUNK, CHUNK), BF16), jax.ShapeDtypeStruct((DN_H, N, CHUNK, DN_DV), F32),
                   jax.ShapeDtypeStruct((DN_H, N, CHUNK, DN_DK), BF16)],
        scratch_shapes=[pltpu.VMEM((hb, DN_DK, DN_DV), F32)],
        compiler_params=_cp("parallel", "arbitrary"),
    )(conv, conv, conv, psmall, par)


def _gdn_bwd(conv, psmall, par, states, do, solved, name, carry=None):
    ex_in, ex_args, ex_out, ex_shape, ex_sems = _carry_specs(carry)
    T = conv.shape[0]
    blk = min(GDN_BLK, T)
    nc = blk // CHUNK
    nb = T // blk
    hb = GDN_HEADS
    nbat = hb * nc
    rsum = lambda x: jnp.sum(x, axis=-1, keepdims=True)

    def body(cq_ref, ck_ref, cv_ref, sm_ref, par_ref, ss_ref, do_ref, ti_ref, u_ref, w_ref,
             dcq_ref, dck_ref, dcv_ref, dsm_ref, dpar_ref, ds_ref):
        @pl.when(pl.program_id(1) == 0)
        def _():
            ds_ref[...] = jnp.zeros_like(ds_ref)
            dpar_ref[...] = jnp.zeros_like(dpar_ref)

        masks = _gdn_masks()
        causal, strict, tril, eye = masks
        triu = (_iota2(CHUNK, CHUNK, 0) <= _iota2(CHUNK, CHUNK, 1)).astype(F32)
        lane = _iota2(CHUNK, 128, 1)
        last_row = _iota2(CHUNK, 128, 0) == CHUNK - 1
        h0 = 0 if hb == DN_H else pl.program_id(0) * hb
        solved = (ti_ref[...].reshape(nbat, CHUNK, CHUNK), u_ref[...].reshape(nbat, CHUNK, DN_DV),
                  w_ref[...].reshape(nbat, CHUNK, DN_DK))
        f = _gdn_block(cq_ref, ck_ref, cv_ref, sm_ref, par_ref, h0, hb, nc, masks, solved)
        S = ss_ref[...].reshape(nbat, DN_DK, DN_DV)
        Sb = _b(S)
        do_ = _b(_heads_first(do_ref[...], nc))
        wb, qdb, kstb, qkb = _b(f["w"]), _b(f["q_dec"]), _b(f["k_st"]), _b(f["qk"])
        vnb = _b(f["u"] - _bmm(wb, Sb))
        dvn0 = _bmm_tn(qkb, do_)
        qdo = _bmm_tn(qdb, do_)
        dS = [ds_ref[i] for i in range(hb)]
        after = [None] * nbat
        for c in range(nc - 1, -1, -1):
            for i in range(hb):
                n = i * nc + c
                after[n] = dS[i]
                dvn_c = _b(dvn0[n] + _dot(kstb[n], _b(dS[i])))
                dS[i] = dS[i] * f["g_ch"][n] + qdo[n] - _dot_tn(wb[n], dvn_c)
        for i in range(hb):
            ds_ref[i] = dS[i]
        dSa = jnp.stack(after)
        dSb = _b(dSa)
        dvn = dvn0 + _bmm(kstb, dSb)
        dvnb = _b(dvn)
        dq_dec = _bmm_nt(do_, Sb)
        dqk = jnp.where(causal, _bmm_nt(do_, vnb), 0.0)
        dk_st = _bmm_nt(vnb, dSb)
        dg_ch = jnp.sum(rsum(S * dSa), axis=1, keepdims=True)
        dw = -_bmm_nt(dvnb, Sb)
        drv = _bmm_tn(f["tinv_b"], dvnb)
        drk = _bmm_tn(f["tinv_b"], _b(dw))
        dlow = jnp.where(strict, -(_bmm_nt(_b(drv), _b(f["u"])) + _bmm_nt(_b(drk), wb)), 0.0)
        dv = drv * f["beta"]
        dbeta = rsum(drv * f["v"])
        dkb = drk * f["eG"]
        dG = rsum(drk * f["rkk"])
        dA = dlow * f["decay"]
        ddec = dlow * f["A"]
        dkb += _bmm(_b(dA), _b(f["kn"]))
        dkn = _bmm_tn(_b(dA), _b(f["kb"]))
        dB = dqk * f["decay"]
        ddec += dqk * f["B"]
        dqn = _bmm(_b(dB), _b(f["kn"]))
        dkn += _bmm_tn(_b(dB), _b(f["qn"]))
        dD = ddec * f["decay"]
        dG += rsum(dD) - rsum(eye * jnp.sum(dD, axis=1, keepdims=True))
        dqn += dq_dec * f["eG"]
        dG += rsum(dq_dec * f["q_dec"])
        dkn += dk_st * f["eGl"]
        tks = rsum(dk_st * f["k_st"])
        dG -= tks
        dG_last = jnp.sum(tks, axis=1, keepdims=True) + dg_ch * f["g_ch"][:, :, :1]
        dkn += dkb * f["beta"]
        dbeta += rsum(dkb * f["kn"])
        dGf = jnp.broadcast_to(dG, (nbat, CHUNK, 128)) + jnp.where(last_row, dG_last, 0.0)
        dg = _bmm(jnp.broadcast_to(triu, (nbat, CHUNK, CHUNK)), dGf, HI)
        dbraw = dbeta * f["beta"][:, :, :1] * (1.0 - f["beta"][:, :, :1])
        daraw = dg * (-f["ea"]) * _sigmoid(f["sp_arg"])
        both = lambda t: jnp.sum(jnp.sum(t, axis=1, keepdims=True), axis=0)
        dgg = dg * f["g"]
        dsm = jnp.zeros((nc, CHUNK, SMALL), F32)
        for i in range(hb):
            mine = slice(i * nc, (i + 1) * nc)
            dsm += (jnp.where(lane == GLA_RANK + h0 + i, dbraw[mine], 0.0)
                    + jnp.where(lane == GLA_RANK + DN_H + h0 + i, daraw[mine], 0.0))
            dpar = jnp.where(lane[:1] == 0, both(dgg[mine]), jnp.where(lane[:1] == 1, both(daraw[mine]), 0.0))
            dpar_ref[i] += jnp.broadcast_to(dpar, (8, 128))
        dsm_ref[0] = dsm.reshape(blk, SMALL)
        dqh = dqn * (DN_DK ** -0.5)
        dq = f["rq"] * (dqh - f["qh"] * rsum(dqh * f["qh"]))
        dk = f["rk"] * (dkn - f["kn"] * rsum(dkn * f["kn"]))
        dsilu = lambda x, s: s * (1.0 + x * (1.0 - s))
        dcq_ref[...] = _b(_heads_last(dq * dsilu(f["cq"], f["sq"]), nc))
        dck_ref[...] = _b(_heads_last(dk * dsilu(f["ck"], f["sk"]), nc))
        dcv_ref[...] = _b(_heads_last(dv * dsilu(f["cv"], f["sv"]), nc))

    r = lambda j: nb - 1 - j
    out_blk = pl.BlockSpec((blk, hb * DN_DK), lambda h, j: (r(j), h))
    grid = (DN_H // hb, nb)
    return pl.pallas_call(
        _carry(carry, body, 10, 5, grid), name=name, grid=grid,
        in_specs=_gdn_specs(blk, hb, r) + [pl.BlockSpec((hb, nc, DN_DK, DN_DV), lambda h, j: (h, r(j), 0, 0)),
                                          pl.BlockSpec((blk, hb * DN_DV), lambda h, j: (r(j), h))]
        + _gdn_solved_specs(blk, hb, r) + ex_in,
        out_specs=[out_blk, out_blk, out_blk, pl.BlockSpec((1, blk, SMALL), lambda h, j: (h, r(j), 0)),
                   pl.BlockSpec((hb, 8, 128), lambda h, j: (h, 0, 0))] + ex_out,
        out_shape=[jax.ShapeDtypeStruct((T, DN_H * DN_DK), BF16)] * 3 + [
            jax.ShapeDtypeStruct((DN_H // hb, T, SMALL), F32), jax.ShapeDtypeStruct((DN_H, 8, 128), F32)] + ex_shape,
        scratch_shapes=[pltpu.VMEM((hb, DN_DK, DN_DV), F32)] + ex_sems,
        compiler_params=_cp("arbitrary", "arbitrary"),
    )(conv, conv, conv, psmall, par, states, do, *solved, *ex_args)


def _head_norm(o, w, dv):
    outs, rs = [], []
    for i in range(o.shape[1] // dv):
        oh = o[:, i * dv:(i + 1) * dv]
        r = lax.rsqrt(jnp.mean(oh * oh, axis=-1, keepdims=True) + EPS)
        outs.append(oh * r)
        rs.append(r)
    return outs, rs


def _merge_specs(tm):
    col = lambda c: pl.BlockSpec((tm, D), lambda i: (i, c))
    return [col(0), col(0), col(2), col(6), col(7), col(8),
            pl.BlockSpec((1, GLA_DV), lambda i: (0, 0)), pl.BlockSpec((1, DN_DV), lambda i: (0, 0)),
            pl.BlockSpec((D, D), lambda i: (0, 0))]


def _merge_fwd(h, oa, ob, pbig, gla_hn, dn_hn, wout, name):
    T = h.shape[0]
    tm = min(ROW_BLK, T)

    def body(h_ref, oa_ref, ob_ref, gr_ref, dg_ref, ma_ref, mb_ref, wa_ref, wb_ref, wo_ref, ho_ref, y_ref):
        na, _ = _head_norm(oa_ref[...].astype(F32), wa_ref[...], GLA_DV)
        nbs, _ = _head_norm(ob_ref[...].astype(F32), wb_ref[...], DN_DV)
        hna = jnp.concatenate([t * wa_ref[...] for t in na], axis=1)
        hnb = jnp.concatenate([t * wb_ref[...] for t in nbs], axis=1)
        gr = gr_ref[...].astype(F32)
        dg = dg_ref[...].astype(F32)
        y = (_sigmoid(ma_ref[...].astype(F32)) * hna * (gr * _sigmoid(gr))
             + _sigmoid(mb_ref[...].astype(F32)) * hnb * (dg * _sigmoid(dg)))
        yb = _b(y)
        y_ref[...] = yb
        ho_ref[...] = h_ref[...] + _dot(yb, wo_ref[...])

    row = pl.BlockSpec((tm, D), lambda i: (i, 0))
    return pl.pallas_call(
        body, name=name, grid=(T // tm,),
        in_specs=[row] + _merge_specs(tm),
        out_specs=[row, row],
        out_shape=[jax.ShapeDtypeStruct((T, D), F32), jax.ShapeDtypeStruct((T, D), BF16)],
        compiler_params=_cp("arbitrary"),
    )(h, oa, ob, pbig, pbig, pbig, pbig, gla_hn, dn_hn, wout)


def _merge_bwd(dh, oa, ob, pbig, gla_hn, dn_hn, wout, name, carry=None):
    T = dh.shape[0]
    tm = min(ROW_BLK, T)
    ex_in, ex_args, ex_out, ex_shape, ex_sems = _carry_specs(carry)

    def branch(dy, o_ref, w_ref, gate_ref, m_ref, dv):
        w = w_ref[...]
        ohat, rs = _head_norm(o_ref[...].astype(F32), w, dv)
        gate = gate_ref[...].astype(F32)
        m = m_ref[...].astype(F32)
        sgate, sm = _sigmoid(gate), _sigmoid(m)
        silu = gate * sgate
        ohat_all = jnp.concatenate(ohat, axis=1)
        hn = jnp.concatenate([t * w for t in ohat], axis=1)
        d_on = dy * sm
        d_m = dy * hn * silu * sm * (1.0 - sm)
        d_hn = d_on * silu
        d_gate = d_on * hn * (sgate * (1.0 + gate * (1.0 - sgate)))
        dw = jnp.zeros((1, dv), F32)
        d_o = []
        for i, (oh, r) in enumerate(zip(ohat, rs)):
            dhn = d_hn[:, i * dv:(i + 1) * dv]
            dw += jnp.sum(dhn * oh, axis=0, keepdims=True)
            dohat = dhn * w
            d_o.append(r * (dohat - oh * jnp.mean(dohat * oh, axis=-1, keepdims=True)))
        return jnp.concatenate(d_o, axis=1), d_gate, d_m, dw

    def body(dh_ref, oa_ref, ob_ref, gr_ref, dg_ref, ma_ref, mb_ref, wa_ref, wb_ref, wo_ref,
             doa_ref, dob_ref, dgr_ref, ddg_ref, dma_ref, dmb_ref, dwa_ref, dwb_ref, dhb_ref):
        @pl.when(pl.program_id(0) == 0)
        def _():
            dwa_ref[...] = jnp.zeros_like(dwa_ref)
            dwb_ref[...] = jnp.zeros_like(dwb_ref)

        dhb = _b(dh_ref[...])
        dhb_ref[...] = dhb
        dy = _dot_nt(dhb, wo_ref[...])
        d_oa, d_gr, d_ma, dwa = branch(dy, oa_ref, wa_ref, gr_ref, ma_ref, GLA_DV)
        d_ob, d_dg, d_mb, dwb = branch(dy, ob_ref, wb_ref, dg_ref, mb_ref, DN_DV)
        doa_ref[...] = _b(d_oa)
        dob_ref[...] = _b(d_ob)
        dgr_ref[...] = _b(d_gr)
        ddg_ref[...] = _b(d_dg)
        dma_ref[...] = _b(d_ma)
        dmb_ref[...] = _b(d_mb)
        dwa_ref[...] += dwa
        dwb_ref[...] += dwb

    row = pl.BlockSpec((tm, D), lambda i: (i, 0))
    b16 = jax.ShapeDtypeStruct((T, D), BF16)
    return pl.pallas_call(
        _carry(carry, body, 10, 9, (T // tm,)), name=name, grid=(T // tm,),
        in_specs=[row] + _merge_specs(tm) + ex_in,
        out_specs=[row] * 6 + [pl.BlockSpec((1, GLA_DV), lambda i: (0, 0)), pl.BlockSpec((1, DN_DV), lambda i: (0, 0)), row]
        + ex_out,
        out_shape=[b16, b16, b16, b16, b16, b16, jax.ShapeDtypeStruct((1, GLA_DV), F32),
                   jax.ShapeDtypeStruct((1, DN_DV), F32), b16] + ex_shape,
        scratch_shapes=ex_sems,
        compiler_params=_cp("arbitrary"),
    )(dh, oa, ob, pbig, pbig, pbig, pbig, gla_hn, dn_hn, wout, *ex_args)


def _adamw(w, g, m, v, rows, name):
    R, C = w.shape
    rows = min(rows, R)
    c1 = 1.0 - ADAM_B1 ** ADAM_STEP
    c2 = 1.0 - ADAM_B2 ** ADAM_STEP

    def body(w_ref, g_ref, m_ref, v_ref, d_ref, mo_ref, vo_ref):
        g_ = g_ref[...]
        m_ = ADAM_B1 * m_ref[...] + (1.0 - ADAM_B1) * g_
        v_ = ADAM_B2 * v_ref[...] + (1.0 - ADAM_B2) * (g_ * g_)
        mo_ref[...] = m_
        vo_ref[...] = v_
        d_ref[...] = -ADAM_LR * ((m_ / c1) / (jnp.sqrt(v_ / c2) + ADAM_EPS) + ADAM_WD * w_ref[...])

    blk = pl.BlockSpec((rows, C), lambda i: (i, 0))
    shp = jax.ShapeDtypeStruct((R, C), F32)
    return pl.pallas_call(
        body, name=name, grid=(R // rows,),
        in_specs=[blk] * 4, out_specs=[blk] * 3, out_shape=[shp] * 3,
        compiler_params=_cp("parallel"),
    )(w, g, m, v)


def _me():
    return lax.axis_index("x"), lax.axis_index("y"), lax.axis_index("c")


def _other_chips(x, y):
    return [(1 - x, y), (x, 1 - y), (1 - x, 1 - y)]


def _half_rows(ref, hf):
    half = ref.shape[-2] // 2
    rows = pl.ds(pl.multiple_of(hf * half, 16), half)
    return ref.at[rows, :] if len(ref.shape) == 2 else ref.at[:, rows, :]


class _GatherBig:
    def __init__(self, big):
        self.arrays = list(big)
        self.out_shape = [jax.ShapeDtypeStruct((N_SHARD,) + w.shape, w.dtype) for w in big]
        self.n_sem = 7 * len(big)

    @staticmethod
    def _copy(sems, k, src, dst, to):
        return pltpu.make_async_remote_copy(src_ref=src, dst_ref=dst, send_sem=sems[0].at[k], recv_sem=sems[1].at[k],
                                            device_id=to, device_id_type=MESH)

    def start(self, ins, outs, *sems):
        x, y, c = _me()
        mine = 2 * x + y
        for i, (w_ref, o_ref) in enumerate(zip(ins, outs)):
            self._copy(sems, 7 * i + 6, w_ref, o_ref.at[mine], (x, y, 1 - c)).start()
            for j, chip in enumerate(_other_chips(x, y)):
                self._copy(sems, 7 * i + j, _half_rows(w_ref, c), _half_rows(o_ref.at[mine], c), (*chip, c)).start()

    def relay(self, ins, outs, *sems):
        x, y, c = _me()
        for i, o_ref in enumerate(outs):
            for j, chip in enumerate(_other_chips(x, y)):
                landed = _half_rows(o_ref.at[2 * chip[0] + chip[1]], c)
                self._copy(sems, 7 * i + j, landed, landed, (x, y, c)).wait_recv()
                self._copy(sems, 7 * i + 3 + j, landed, landed, (x, y, 1 - c)).start()

    def finish(self, ins, outs, *sems):
        x, y, c = _me()
        me, sibling = (x, y, c), (x, y, 1 - c)
        chips = _other_chips(x, y)
        slot = lambda chip: 2 * chip[0] + chip[1]
        for i, (w_ref, o_ref) in enumerate(zip(ins, outs)):
            for j, chip in enumerate(chips):
                passed = _half_rows(o_ref.at[slot(chip)], 1 - c)
                self._copy(sems, 7 * i + 3 + j, passed, passed, me).wait_recv()
            self._copy(sems, 7 * i + 6, o_ref.at[slot((x, y))], o_ref.at[slot((x, y))], me).wait_recv()
        for i, (w_ref, o_ref) in enumerate(zip(ins, outs)):
            self._copy(sems, 7 * i + 6, w_ref, o_ref.at[slot((x, y))], sibling).wait_send()
            for j, chip in enumerate(chips):
                self._copy(sems, 7 * i + j, _half_rows(w_ref, c), _half_rows(o_ref.at[slot((x, y))], c),
                           (*chip, c)).wait_send()
                landed = _half_rows(o_ref.at[slot(chip)], c)
                self._copy(sems, 7 * i + 3 + j, landed, landed, sibling).wait_send()


class _SiblingExchange:
    def __init__(self, gs):
        self.arrays = list(gs)
        self.out_shape = [jax.ShapeDtypeStruct((g.shape[0], g.shape[1] // 2, g.shape[2]), g.dtype) for g in gs]
        self.n_sem = len(gs)

    def _copies(self, ins, outs, send_sems, recv_sems):
        x, y, c = _me()
        return [pltpu.make_async_remote_copy(src_ref=_half_rows(ins[i], 1 - c), dst_ref=outs[i],
                                             send_sem=send_sems.at[i], recv_sem=recv_sems.at[i],
                                             device_id=(x, y, 1 - c), device_id_type=MESH) for i in range(len(ins))]

    def start(self, ins, outs, *sems):
        for cp in self._copies(ins, outs, *sems):
            cp.start()

    def finish(self, ins, outs, *sems):
        for cp in self._copies(ins, outs, *sems):
            cp.wait()


class _ChipsExchange:
    def __init__(self, pbs):
        self.arrays = list(pbs)
        self.out_shape = [jax.ShapeDtypeStruct((3,) + p.shape[1:], p.dtype) for p in pbs]
        self.n_sem = 3 * len(pbs)

    def _copies(self, ins, outs, send_sems, recv_sems):
        x, y, c = _me()
        return [pltpu.make_async_remote_copy(src_ref=ins[i].at[2 * chip[0] + chip[1]], dst_ref=outs[i].at[j],
                                             send_sem=send_sems.at[3 * i + j], recv_sem=recv_sems.at[3 * i + j],
                                             device_id=(*chip, c), device_id_type=MESH)
                for i in range(len(ins)) for j, chip in enumerate(_other_chips(x, y))]

    def start(self, ins, outs, *sems):
        for cp in self._copies(ins, outs, *sems):
            cp.start()

    def finish(self, ins, outs, *sems):
        for cp in self._copies(ins, outs, *sems):
            cp.wait()


def _carry(ex, body, n_in, n_out, grid):
    if ex is None:
        return body
    ni, no = len(ex.arrays), len(ex.out_shape)

    def carried(*refs):
        ins, ex_in = refs[:n_in], refs[n_in:n_in + ni]
        outs, ex_out = refs[n_in + ni:n_in + ni + n_out], refs[n_in + ni + n_out:n_in + ni + n_out + no]
        scratch, sems = refs[n_in + ni + n_out + no:-2], refs[-2:]
        step = functools.reduce(lambda acc, a: acc * grid[a] + pl.program_id(a), range(len(grid)), 0)
        steps = math.prod(grid)

        @pl.when(step == 0)
        def _():
            ex.start(ex_in, ex_out, *sems)

        body(*ins, *outs, *scratch)

        if hasattr(ex, "relay"):
            @pl.when(step == (3 * steps) // 4)
            def _():
                ex.relay(ex_in, ex_out, *sems)

        @pl.when(step == steps - 1)
        def _():
            ex.finish(ex_in, ex_out, *sems)

    return carried


def _carry_specs(ex):
    if ex is None:
        return [], [], [], [], []
    sems = [pltpu.SemaphoreType.DMA((ex.n_sem,)), pltpu.SemaphoreType.DMA((ex.n_sem,))]
    return [ANY] * len(ex.arrays), ex.arrays, [ANY] * len(ex.out_shape), ex.out_shape, sems


def _gather_weights(big, small, name):
    nbig, nsm = len(big), len(small)
    n = nbig + nsm
    own_sem = 6 * nbig + 3 * nsm

    def body(*refs):
        ins, outs = refs[:n], refs[n:2 * n]
        send_sems, recv_sems = refs[2 * n:]
        x, y, c = _me()
        sibling = (x, y, 1 - c)
        chips = _other_chips(x, y)
        slot = lambda chip: 2 * chip[0] + chip[1]

        def copy(k, src, dst, to):
            return pltpu.make_async_remote_copy(src_ref=src, dst_ref=dst, send_sem=send_sems.at[k],
                                                recv_sem=recv_sems.at[k], device_id=to, device_id_type=MESH)

        sent = []
        for i in range(nbig):
            sent.append(copy(own_sem + i, ins[i], outs[i].at[slot((x, y))], sibling))
            sent[-1].start()
            for j, chip in enumerate(chips):
                sent.append(copy(6 * i + j, _half_rows(ins[i], c), _half_rows(outs[i].at[slot((x, y))], c), (*chip, c)))
                sent[-1].start()
        for t in range(nsm):
            w_ref, o_ref = ins[nbig + t], outs[nbig + t]
            o_ref[slot((x, y))] = w_ref[...]
            for j, chip in enumerate(chips):
                sent.append(copy(6 * nbig + 3 * t + j, w_ref, o_ref.at[slot((x, y))], (*chip, c)))
                sent[-1].start()
        for i in range(nbig):
            for j, chip in enumerate(chips):
                landed = _half_rows(outs[i].at[slot(chip)], c)
                copy(6 * i + j, landed, landed, (x, y, c)).wait_recv()
                sent.append(copy(6 * i + 3 + j, landed, landed, sibling))
                sent[-1].start()
        for t in range(nsm):
            for j, chip in enumerate(chips):
                landed = outs[nbig + t].at[slot(chip)]
                copy(6 * nbig + 3 * t + j, landed, landed, (x, y, c)).wait_recv()
        for i in range(nbig):
            for j, chip in enumerate(chips):
                passed = _half_rows(outs[i].at[slot(chip)], 1 - c)
                copy(6 * i + 3 + j, passed, passed, (x, y, c)).wait_recv()
        for i in range(nbig):
            mine = outs[i].at[slot((x, y))]
            copy(own_sem + i, mine, mine, (x, y, c)).wait_recv()
        for cp in sent:
            cp.wait_send()

    vm = pl.BlockSpec(memory_space=pltpu.VMEM)
    nsem = own_sem + nbig
    return pl.pallas_call(
        body, name=name, in_specs=[ANY] * nbig + [vm] * nsm, out_specs=[ANY] * nbig + [vm] * nsm,
        out_shape=[jax.ShapeDtypeStruct((N_SHARD,) + w.shape, w.dtype) for w in list(big) + list(small)],
        scratch_shapes=[pltpu.SemaphoreType.DMA((nsem,)), pltpu.SemaphoreType.DMA((nsem,))],
        compiler_params=pltpu.CompilerParams(has_side_effects=True),
    )(*big, *small)


def _rs_sibling(gs, name):
    n = len(gs)

    def body(*refs):
        send_sems, recv_sems = refs[2 * n:]
        x, y, c = _me()
        cps = [pltpu.make_async_remote_copy(src_ref=_half_rows(refs[i], 1 - c), dst_ref=refs[n + i],
                                            send_sem=send_sems.at[i], recv_sem=recv_sems.at[i],
                                            device_id=(x, y, 1 - c), device_id_type=MESH) for i in range(n)]
        for cp in cps:
            cp.start()
        for cp in cps:
            cp.wait()

    return pl.pallas_call(
        body, name=name, in_specs=[ANY] * n, out_specs=[ANY] * n,
        out_shape=[jax.ShapeDtypeStruct((g.shape[0], g.shape[1] // 2, g.shape[2]), g.dtype) for g in gs],
        scratch_shapes=[pltpu.SemaphoreType.DMA((n,)), pltpu.SemaphoreType.DMA((n,))],
        compiler_params=pltpu.CompilerParams(has_side_effects=True),
    )(*gs)


def _add_pair(g, other, where, name):
    ns, a, b = g.shape
    half = a // 2

    def body(w_ref, g_ref, o_ref, pb_ref, own_ref):
        t = g_ref[0].astype(F32) + o_ref[0].astype(F32)
        pb_ref[0] = _b(t)

        @pl.when(pl.program_id(0) == w_ref[1])
        def _():
            own_ref[...] = t

    return pl.pallas_call(
        body, name=name,
        grid_spec=pltpu.PrefetchScalarGridSpec(
            num_scalar_prefetch=1, grid=(ns,),
            in_specs=[pl.BlockSpec((1, half, b), lambda s, w: (s, w[0], 0)), pl.BlockSpec((1, half, b), lambda s, w: (s, 0, 0))],
            out_specs=[pl.BlockSpec((1, half, b), lambda s, w: (s, 0, 0)), pl.BlockSpec((half, b), lambda s, w: (0, 0))]),
        out_shape=[jax.ShapeDtypeStruct((ns, half, b), BF16), jax.ShapeDtypeStruct((half, b), F32)],
        compiler_params=_cp("arbitrary"),
    )(where, g, other)


def _add_four(own, got, name):
    rows, cols = own.shape
    rb = rows // 2

    def body(a_ref, b_ref, o_ref):
        o_ref[...] = ((a_ref[...] + b_ref[0].astype(F32)) + b_ref[1].astype(F32)) + b_ref[2].astype(F32)

    return pl.pallas_call(
        body, name=name, grid=(rows // rb,),
        in_specs=[pl.BlockSpec((rb, cols), lambda i: (i, 0)), pl.BlockSpec((3, rb, cols), lambda i: (0, i, 0))],
        out_specs=pl.BlockSpec((rb, cols), lambda i: (i, 0)),
        out_shape=jax.ShapeDtypeStruct((rows, cols), F32),
        compiler_params=_cp("parallel"),
    )(own, got)


def _rs_swap(halves, name):
    n = len(halves)

    def body(*refs):
        send_sems, recv_sems = refs[2 * n:]
        x, y, c = _me()
        cps = [pltpu.make_async_remote_copy(src_ref=refs[i], dst_ref=refs[n + i], send_sem=send_sems.at[i],
                                            recv_sem=recv_sems.at[i], device_id=(x, y, 1 - c), device_id_type=MESH)
               for i in range(n)]
        for cp in cps:
            cp.start()
        for cp in cps:
            cp.wait()

    return pl.pallas_call(
        body, name=name, in_specs=[ANY] * n, out_specs=[ANY] * n,
        out_shape=[jax.ShapeDtypeStruct(h.shape, h.dtype) for h in halves],
        scratch_shapes=[pltpu.SemaphoreType.DMA((n,)), pltpu.SemaphoreType.DMA((n,))],
        compiler_params=pltpu.CompilerParams(has_side_effects=True),
    )(*halves)


def _adamw_halves(w, own, got, m, v, rows, name):
    a, b = w.shape
    nblk = a // 2 // rows
    c1 = 1.0 - ADAM_B1 ** ADAM_STEP
    c2 = 1.0 - ADAM_B2 ** ADAM_STEP

    def body(w_ref, own_ref, got_ref, m_ref, v_ref, g_ref, d_ref, mo_ref, vo_ref):
        g_ = jnp.where(pl.program_id(0) == lax.axis_index("c"), own_ref[...], got_ref[...])
        g_ref[...] = g_
        m_ = ADAM_B1 * m_ref[...] + (1.0 - ADAM_B1) * g_
        v_ = ADAM_B2 * v_ref[...] + (1.0 - ADAM_B2) * (g_ * g_)
        mo_ref[...] = m_
        vo_ref[...] = v_
        d_ref[...] = -ADAM_LR * ((m_ / c1) / (jnp.sqrt(v_ / c2) + ADAM_EPS) + ADAM_WD * w_ref[...])

    whole = pl.BlockSpec((rows, b), lambda h, i: (h * nblk + i, 0))
    part = pl.BlockSpec((rows, b), lambda h, i: (i, 0))
    shp = jax.ShapeDtypeStruct((a, b), F32)
    return pl.pallas_call(
        body, name=name, grid=(2, nblk),
        in_specs=[whole, part, part, whole, whole], out_specs=[whole] * 4, out_shape=[shp] * 4,
        compiler_params=_cp("parallel", "parallel"),
    )(w, own, got, m, v)


def _allsum_small(vec, name):
    def body(v_ref, o_ref, buf_ref, send_sems, recv_sems):
        x, y, c = _me()
        me = 4 * x + 2 * y + c
        buf_ref[me] = v_ref[...]
        cps = []
        for k in range(1, 8):
            peer = (x ^ (k >> 2), y ^ ((k >> 1) & 1), c ^ (k & 1))
            cps.append(pltpu.make_async_remote_copy(src_ref=v_ref, dst_ref=buf_ref.at[me],
                                                    send_sem=send_sems.at[k - 1], recv_sem=recv_sems.at[k - 1],
                                                    device_id=peer, device_id_type=MESH))
        for cp in cps:
            cp.start()
        for k in range(1, 8):
            peer_idx = me ^ k
            pltpu.make_async_remote_copy(src_ref=v_ref, dst_ref=buf_ref.at[peer_idx],
                                         send_sem=send_sems.at[k - 1], recv_sem=recv_sems.at[k - 1],
                                         device_id=(x, y, c), device_id_type=MESH).wait_recv()
        for cp in cps:
            cp.wait_send()
        acc = buf_ref[0]
        for d in range(1, 8):
            acc = acc + buf_ref[d]
        o_ref[...] = acc

    return pl.pallas_call(
        body, name=name,
        in_specs=[pl.BlockSpec(memory_space=pltpu.VMEM)], out_specs=pl.BlockSpec(memory_space=pltpu.VMEM),
        out_shape=jax.ShapeDtypeStruct(vec.shape, F32),
        scratch_shapes=[pltpu.VMEM((8,) + vec.shape, F32), pltpu.SemaphoreType.DMA((7,)), pltpu.SemaphoreType.DMA((7,))],
        compiler_params=pltpu.CompilerParams(has_side_effects=True),
    )(vec)


BIG = ("ffn1_w_gate", "ffn1_w_up", "ffn1_w_down", "w_in", "w_out", "ffn2_w_gate", "ffn2_w_up", "ffn2_w_down")
TINY = ("w_gla_gate", "conv_w")
SHARDED = BIG + TINY


def _join_cols(w4):
    return jnp.transpose(w4, (1, 0, 2)).reshape(w4.shape[1], N_SHARD * w4.shape[2])


def _cut_cols(w):
    return jnp.transpose(w.reshape(w.shape[0], N_SHARD, w.shape[1] // N_SHARD), (1, 0, 2))


def _split_w_in(w):
    o = IN_OFF
    big = jnp.concatenate([w[:, :o[4]], w[:, o[5]:o[9]], w[:, o[11]:]], axis=1)
    small = jnp.concatenate([w[:, o[4]:o[5]], w[:, o[9]:o[11]], jnp.zeros((w.shape[0], SMALL - 32), w.dtype)], axis=1)
    return big, small


def _merge_w_in(big, small):
    return jnp.concatenate([big[:, :3072], small[:, :16], big[:, 3072:7168], small[:, 16:32], big[:, 7168:]], axis=1)


class _Comm:
    def __init__(self, where, rest_shards):
        self.where = where
        self.w_in = _GatherBig(rest_shards[:1])
        self.later = _GatherBig(rest_shards[1:])
        self.pairs, self.got = {}, {}

    @staticmethod
    def w_in_weights(gathered):
        return dict(zip(("w_in_big", "w_in_small"), _split_w_in(_join_cols(gathered[0]))))

    @staticmethod
    def later_weights(gathered):
        W = dict(zip(BIG[4:], gathered))
        W["w_out"] = W["w_out"].reshape(D, D)
        return W

    def pair(self, names, grads, from_sibling):
        for n, g, o in zip(names, grads, from_sibling):
            self.pairs[n] = _add_pair(g, o, self.where, "rs_pair_" + n)
        return _ChipsExchange([self.pairs[n][0] for n in names])

    def begin(self, names, grads):
        return self.pair(names, grads, _rs_sibling(grads, "rs_sibling_" + names[0]))

    def landed(self, names, outs):
        self.got.update(zip(names, outs))


def _local_step(x, target, W, P, comm=None):
    wgate_pad = jnp.zeros((SMALL, GLA_H * GLA_DK), F32).at[:GLA_RANK].set(P["w_gla_gate"])
    cw8 = jnp.zeros((8, CONV_C), F32).at[:CONV_K].set(P["conv_w"])
    par = jnp.zeros((DN_H, 8, 128), F32)
    par = par.at[:, 0, :].set(jnp.broadcast_to(P["dn_a_log"].reshape(DN_H, 1), (DN_H, 128)))
    par = par.at[:, 1, :].set(jnp.broadcast_to(P["dn_dt_bias"].reshape(DN_H, 1), (DN_H, 128)))

    h1, n1, g1, u1, *got = _ffn_fwd(x, P["ffn1_norm"], W["ffn1_w_gate"], W["ffn1_w_up"], W["ffn1_w_down"], "ffn1_fwd",
                                    carry=comm.w_in if comm else None)
    if comm:
        W = dict(W, **comm.w_in_weights(got))
    wbig, wsmall = W["w_in_big"], W["w_in_small"]
    pbig, psmall, n2, qk, *got = _norm_proj(h1, P["mix_norm"], wbig, wsmall, "mix_proj",
                                            carry=comm.later if comm else None)
    if comm:
        W = dict(W, **comm.later_weights(got))
    oa, sa = _gla_fwd(pbig, qk, psmall, wgate_pad, P["b_gla_gate"], "gla_fwd")
    conv = _conv_fwd(pbig, cw8, "conv_fwd")
    ob, sb, *solved = _gdn_fwd(conv, psmall, par, "gdn_fwd")
    h2, yb = _merge_fwd(h1, oa, ob, pbig, P["gla_head_norm"], P["dn_head_norm"], W["w_out"], "merge_fwd")
    dh3, n3, g3, u3, loss, d_final = _ffn_fwd(h2, P["ffn2_norm"], W["ffn2_w_gate"], W["ffn2_w_up"], W["ffn2_w_down"],
                                              "ffn2_fwd", head=(P["final_norm"], target))

    gw, gs = {}, {"final_norm": d_final}

    def ffn_grads(tag, dh, h, n, g, u, before=None, later=False):
        names = tuple(tag + s for s in ("_w_gate", "_w_up", "_w_down"))
        dg, du, act, dfb, *landed = _ffn_bwd_hidden(dh, g, u, W[names[2]], tag + "_bwd_hidden", carry=before)
        gw[names[0]] = _mm_tn(n, dg, D, FF_CUT, tag + "_dwg", tk=4096)
        gw[names[1]] = _mm_tn(n, du, D, FF_CUT, tag + "_dwu", tk=4096)
        gw[names[2]] = _mm_tn(act, dfb, FF_CUT, D, tag + "_dwd", tk=4096)
        mine = [gw[n] for n in names]
        ex = None if not comm else _SiblingExchange(mine) if later else comm.begin(names, mine)
        dx, gs[tag + "_norm"], *own = _ffn_bwd_input(dh, h, P[tag + "_norm"], dg, du, W[names[0]], W[names[1]],
                                                     tag + "_bwd_input", carry=ex)
        if comm and not later:
            comm.landed(names, own)
        return dx, landed, own

    second = ("ffn2_w_gate", "ffn2_w_up", "ffn2_w_down")
    dh2, _, swapped = ffn_grads("ffn2", dh3, h2, n3, g3, u3, later=True)
    d_oa, d_ob, d_gr, d_dgate, d_ma, d_mb, gs["gla_head_norm"], gs["dn_head_norm"], dh2b, *landed = _merge_bwd(
        dh2, oa, ob, pbig, P["gla_head_norm"], P["dn_head_norm"], W["w_out"], "merge_bwd",
        carry=comm.pair(second, [gw[n] for n in second], swapped) if comm else None)
    if comm:
        comm.landed(second, landed)
    gw["w_out"] = _mm_tn(yb, dh2b, D, D, "dw_out").reshape(N_SHARD, D // N_SHARD, D)
    early = ("w_out",)
    d_gq, d_gk, d_gv, dpre, *swapped = _gla_bwd(pbig, qk, psmall, wgate_pad, P["b_gla_gate"], sa, d_oa, "gla_bwd",
                                                carry=_SiblingExchange([gw["w_out"]]) if comm else None)
    dcq, dck, dcv, dsm, dpar, *landed = _gdn_bwd(conv, psmall, par, sb, d_ob, solved, "gdn_bwd",
                                                 carry=comm.pair(early, [gw["w_out"]], swapped) if comm else None)
    if comm:
        comm.landed(early, landed)
    dsmall, dwgate, gs["b_gla_gate"] = _gla_gate_bwd(dpre, psmall, wgate_pad, dsm, "gla_gate_bwd")
    gs["w_gla_gate"] = dwgate[:GLA_RANK]
    d_x3, dcw = _conv_bwd(dcq, dck, dcv, pbig, cw8, "conv_bwd")
    gs["conv_w"] = dcw[:CONV_K]
    gs["dn_a_log"] = dpar[:, 0, 0].reshape(1, DN_H)
    gs["dn_dt_bias"] = dpar[:, 0, 1].reshape(1, DN_H)
    pieces = (d_gq, d_gk, d_gv, d_gr, d_x3, d_dgate, d_ma, d_mb)
    dh1, gs["mix_norm"] = _proj_bwd(dh2, h1, P["mix_norm"], pieces, dsmall, wbig, wsmall, "proj_bwd")
    dbig = jnp.concatenate([_mm_tn(n2, p, D, 1024, "dw_in_%d" % i) for i, p in enumerate(pieces)], axis=1)
    dsml = _mm_tn(n2, dsmall, D, SMALL, "dw_in_small")
    gw["w_in"] = _cut_cols(_merge_w_in(dbig, dsml))
    grad_x, landed, _ = ffn_grads("ffn1", dh1, x, n1, g1, u1,
                                  before=comm.begin(("w_in",), [gw["w_in"]]) if comm else None)
    if comm:
        comm.landed(("w_in",), landed)
    return loss, grad_x, gw, gs


SMALL_NAMES = ("ffn1_norm", "mix_norm", "ffn2_norm", "final_norm", "b_gla_gate", "gla_head_norm", "dn_head_norm",
               "dn_a_log", "dn_dt_bias")
ROW4 = (("b_gla_gate", 512), ("gla_head_norm", 256), ("dn_head_norm", 128), ("dn_a_log", 8), ("dn_dt_bias", 8))


def _pack_small(d, loss=None):
    row4 = [d[n].reshape(-1) for n, _ in ROW4]
    row4.append(jnp.zeros((1,), F32) if loss is None else loss.reshape(1))
    row4 = jnp.concatenate(row4)
    row4 = jnp.pad(row4, (0, D - row4.shape[0]))
    rows = [d[n].reshape(-1) for n in SMALL_NAMES[:4]] + [row4]
    return jnp.concatenate([jnp.stack(rows), jnp.zeros((3, D), F32)], axis=0)


def _unpack_small(a, like):
    out = {n: a[i].reshape(like[n].shape) for i, n in enumerate(SMALL_NAMES[:4])}
    off = 0
    for n, w in ROW4:
        out[n] = a[4, off:off + w].reshape(like[n].shape)
        off += w
    return out, a[4, off]


WEIGHT_ORDER = ("ffn1_norm", "ffn1_w_gate", "ffn1_w_up", "ffn1_w_down", "mix_norm", "w_in", "w_gla_gate", "b_gla_gate",
                "conv_w", "dn_a_log", "dn_dt_bias", "gla_head_norm", "dn_head_norm", "w_out", "ffn2_norm",
                "ffn2_w_gate", "ffn2_w_up", "ffn2_w_down", "final_norm")
ADAM_ROWS = {"ffn1_w_gate": 512, "ffn1_w_up": 512, "ffn1_w_down": 352, "w_in": 128, "w_gla_gate": 16, "conv_w": 4,
             "w_out": 128, "ffn2_w_gate": 512, "ffn2_w_up": 512, "ffn2_w_down": 352}


def kernel(x, ffn1_norm, ffn1_w_gate, ffn1_w_up, ffn1_w_down, mix_norm, w_in, w_gla_gate, b_gla_gate, conv_w, dn_a_log, dn_dt_bias, gla_head_norm, dn_head_norm, w_out, ffn2_norm, ffn2_w_gate, ffn2_w_up, ffn2_w_down, final_norm, loss_target, m_ffn1_norm, m_ffn1_w_gate, m_ffn1_w_up, m_ffn1_w_down, m_mix_norm, m_w_in, m_w_gla_gate, m_b_gla_gate, m_conv_w, m_dn_a_log, m_dn_dt_bias, m_gla_head_norm, m_dn_head_norm, m_w_out, m_ffn2_norm, m_ffn2_w_gate, m_ffn2_w_up, m_ffn2_w_down, m_final_norm, v_ffn1_norm, v_ffn1_w_gate, v_ffn1_w_up, v_ffn1_w_down, v_mix_norm, v_w_in, v_w_gla_gate, v_b_gla_gate, v_conv_w, v_dn_a_log, v_dn_dt_bias, v_gla_head_norm, v_dn_head_norm, v_w_out, v_ffn2_norm, v_ffn2_w_gate, v_ffn2_w_up, v_ffn2_w_down, v_final_norm):
    given = dict(locals())
    wts = {n: given[n] for n in WEIGHT_ORDER}
    moms = {n: given["m_" + n] for n in WEIGHT_ORDER}
    vars_ = {n: given["v_" + n] for n in WEIGHT_ORDER}
    two_d = lambda a: a.reshape(a.shape[-2], a.shape[-1]) if a.ndim == 3 else a.reshape(1, -1)
    shard = {n: two_d(wts[n]) for n in SHARDED}

    gathered = _gather_weights([shard[n].astype(BF16) for n in BIG[:3]], [shard[n] for n in TINY], "gather_first")
    W = dict(zip(BIG[:3], gathered))
    P = {n: two_d(wts[n]) for n in SMALL_NAMES}
    for n, g in zip(TINY, gathered[3:]):
        P[n] = _join_cols(g)

    my_slot = 2 * lax.axis_index("x") + lax.axis_index("y")
    where = jnp.stack([lax.axis_index("c"), my_slot]).astype(jnp.int32)
    comm = _Comm(where, [shard[n].astype(BF16) for n in BIG[3:]])
    loss, grad_x, gw, gs = _local_step(x[0], loss_target[0], W, P, comm)
    halves = [_add_four(comm.pairs[n][1], comm.got[n], "rs_four_" + n) for n in BIG]
    other_halves = _rs_swap(halves, "rs_swap")

    tiny_rows = jnp.concatenate([gs["w_gla_gate"].reshape(8, D), gs["conv_w"].reshape(12, D), jnp.zeros((4, D), F32)])
    all_sum = _allsum_small(jnp.concatenate([_pack_small(gs, loss[0, 0]), tiny_rows]), "allsum_small")
    small_sum = all_sum[:8]
    small_g, loss_total = _unpack_small(small_sum, P)

    grads, delta, new_m, new_v = {}, {}, {}, {}
    for n, own, got in zip(BIG, halves, other_halves):
        res = _adamw_halves(shard[n], own, got, two_d(moms[n]), two_d(vars_[n]), ADAM_ROWS[n], "adamw_" + n)
        grads[n], delta[n], new_m[n], new_v[n] = (t.reshape(wts[n].shape) for t in res)
    for n, rows in (("w_gla_gate", all_sum[8:16]), ("conv_w", all_sum[16:28])):
        cols = shard[n].shape[1]
        grads[n] = lax.dynamic_slice_in_dim(rows.reshape(shard[n].shape[0], N_SHARD * cols), my_slot * cols, cols, axis=1)
        d, m_, v_ = _adamw(shard[n], grads[n], two_d(moms[n]), two_d(vars_[n]), ADAM_ROWS[n], "adamw_" + n)
        delta[n], new_m[n], new_v[n] = (t.reshape(wts[n].shape) for t in (d, m_, v_))
    pk = lambda src: _pack_small({n: two_d(src[n]) for n in SMALL_NAMES})
    sd, sm_, sv_ = _adamw(pk(wts), small_sum, pk(moms), pk(vars_), 8, "adamw_small")
    for res, dst in ((sd, delta), (sm_, new_m), (sv_, new_v)):
        u, _ = _unpack_small(res, wts)
        dst.update(u)
    grad_w = {n: grads[n].reshape(wts[n].shape) for n in SHARDED}
    grad_w.update({n: small_g[n].reshape(wts[n].shape) for n in SMALL_NAMES})
    return (loss_total, grad_x[None], *[grad_w[n] for n in WEIGHT_ORDER], *[delta[n] for n in WEIGHT_ORDER],
            *[new_m[n] for n in WEIGHT_ORDER], *[new_v[n] for n in WEIGHT_ORDER])
```

```python
import functools
import math

import numpy as np
import jax
import jax.numpy as jnp
from jax import lax
from jax.experimental import pallas as pl
from jax.experimental.pallas import tpu as pltpu

F32 = jnp.float32
BF16 = jnp.bfloat16
HI = lax.Precision.HIGH
MESH = pl.DeviceIdType.MESH
ANY = pl.BlockSpec(memory_space=pl.ANY)

EPS = 1e-6
D = 1024
DFF = 2816
FFN_RES = 0.5
GLA_H, GLA_DK, GLA_DV, GLA_RANK, GLA_TAU = 4, 128, 256, 16, 16.0
DN_H, DN_DK, DN_DV = 8, 128, 128
CONV_K = 4
CHUNK = 64
N_SHARD = 4
FF_CUT = DFF // N_SHARD
ADAM_LR, ADAM_B1, ADAM_B2, ADAM_EPS, ADAM_WD, ADAM_STEP = 0.001, 0.9, 0.999, 1e-08, 0.01, 10

IN_SIZES = (512, 512, 1024, 1024, 16, 1024, 1024, 1024, 1024, 8, 8, 1024, 1024)
IN_OFF = tuple(int(v) for v in np.cumsum((0,) + IN_SIZES))
BIG_COLS = 9216
SMALL = 128
PIECES = (512, 512, 1024, 1024, 3072, 1024, 1024, 1024)

VMEM_LIMIT = 56 * 1024 * 1024
ROW_BLK = 256
BIG_ROW_BLK = 512
ATT_BLK = 512
GDN_BLK = 256
GDN_HEADS = 8


def _cp(*sem):
    return pltpu.CompilerParams(dimension_semantics=sem, vmem_limit_bytes=VMEM_LIMIT)


def _sigmoid(x):
    return 1.0 / (1.0 + jnp.exp(-x))


def _softplus(x):
    return jnp.maximum(x, 0.0) + jnp.log(1.0 + jnp.exp(-jnp.abs(x)))


def _log_sigmoid(x):
    return jnp.minimum(x, 0.0) - jnp.log(1.0 + jnp.exp(-jnp.abs(x)))


def _dot(a, b, prec=None):
    return jnp.dot(a, b, preferred_element_type=F32, precision=prec)


def _dot_nt(a, b, prec=None):
    return lax.dot_general(a, b, (((1,), (1,)), ((), ())), preferred_element_type=F32, precision=prec)


def _dot_tn(a, b, prec=None):
    return lax.dot_general(a, b, (((0,), (0,)), ((), ())), preferred_element_type=F32, precision=prec)


def _b(x):
    return x.astype(BF16)


def _iota2(n, m, axis):
    return lax.broadcasted_iota(jnp.int32, (n, m), axis)


def _load_weights(pairs, sem):
    copies = [pltpu.make_async_copy(s, d, sem.at[i]) for i, (s, d) in enumerate(pairs)]
    for c in copies:
        c.start()
    for c in copies:
        c.wait()


def _ffn_fwd(h, nw, wg, wu, wd, name, carry=None, head=None):
    T = h.shape[0]
    tm = min(BIG_ROW_BLK, T)
    ex_in, ex_args, ex_out, ex_shape, ex_sems = _carry_specs(carry)
    n_head = 2 if head else 0

    def body(*refs):
        h_ref, nw_ref, wg_hbm, wu_hbm, wd_hbm = refs[:5]
        ho_ref, n_ref, g_ref, u_ref = refs[5 + n_head:9 + n_head]
        wg_v, wu_v, wd_v, sem = refs[9 + 2 * n_head:]

        @pl.when(pl.program_id(0) == 0)
        def _():
            _load_weights(((wg_hbm, wg_v), (wu_hbm, wu_v), (wd_hbm, wd_v)), sem)

        x = h_ref[...]
        r = lax.rsqrt(jnp.mean(x * x, axis=-1, keepdims=True) + EPS)
        nb = _b((x * r) * nw_ref[...])
        n_ref[...] = nb
        acc = jnp.zeros((tm, D), F32)
        for s in range(N_SHARD):
            g = _dot(nb, wg_v[s])
            u = _dot(nb, wu_v[s])
            g_ref[s] = _b(g)
            u_ref[s] = _b(u)
            acc += _dot(_b(g * _sigmoid(g) * u), wd_v[s])
        out = x + FFN_RES * acc
        if not head:
            ho_ref[...] = out
            return
        fw_ref, t_ref = refs[5:7]
        loss_ref, dfw_ref = refs[11:13]

        @pl.when(pl.program_id(0) == 0)
        def _():
            loss_ref[...] = jnp.zeros_like(loss_ref)
            dfw_ref[...] = jnp.zeros_like(dfw_ref)

        w = fw_ref[...]
        r = lax.rsqrt(jnp.mean(out * out, axis=-1, keepdims=True) + EPS)
        xhat = out * r
        err = xhat * w - t_ref[...]
        loss_ref[...] += (0.5 / D) * jnp.sum(jnp.sum(err * err, axis=-1, keepdims=True), axis=0, keepdims=True)
        dout = err * (1.0 / D)
        dfw_ref[...] += jnp.sum(dout * xhat, axis=0, keepdims=True)
        dxhat = dout * w
        ho_ref[...] = r * (dxhat - xhat * jnp.mean(dxhat * xhat, axis=-1, keepdims=True))

    row = lambda w: pl.BlockSpec((tm, w), lambda i: (i, 0))
    one = pl.BlockSpec((1, D), lambda i: (0, 0))
    cut = pl.BlockSpec((N_SHARD, tm, FF_CUT), lambda i: (0, i, 0))
    head_out = [pl.BlockSpec((8, 128), lambda i: (0, 0)), one] if head else []
    head_shape = [jax.ShapeDtypeStruct((8, 128), F32), jax.ShapeDtypeStruct((1, D), F32)] if head else []
    return pl.pallas_call(
        _carry(carry, body, 5 + n_head, 4 + n_head, (T // tm,)), name=name, grid=(T // tm,),
        in_specs=[row(D), one, ANY, ANY, ANY] + ([one, row(D)] if head else []) + ex_in,
        out_specs=[row(D), row(D), cut, cut] + head_out + ex_out,
        out_shape=[jax.ShapeDtypeStruct((T, D), F32), jax.ShapeDtypeStruct((T, D), BF16),
                   jax.ShapeDtypeStruct((N_SHARD, T, FF_CUT), BF16),
                   jax.ShapeDtypeStruct((N_SHARD, T, FF_CUT), BF16)] + head_shape + ex_shape,
        scratch_shapes=[pltpu.VMEM((N_SHARD, D, FF_CUT), BF16), pltpu.VMEM((N_SHARD, D, FF_CUT), BF16),
                        pltpu.VMEM((N_SHARD, FF_CUT, D), BF16), pltpu.SemaphoreType.DMA((3,))] + ex_sems,
        compiler_params=_cp("arbitrary"),
    )(h, nw, wg, wu, wd, *(head or ()), *ex_args)


def _ffn_bwd_hidden(dh, g, u, wd, name, carry=None):
    T = dh.shape[0]
    tm = min(BIG_ROW_BLK, T)
    ex_in, ex_args, ex_out, ex_shape, ex_sems = _carry_specs(carry)

    def body(dh_ref, g_ref, u_ref, wd_hbm, dg_ref, du_ref, a_ref, df_ref, wd_v, sem):
        @pl.when(pl.program_id(0) == 0)
        def _():
            _load_weights(((wd_hbm, wd_v),), sem)

        dfb = _b(FFN_RES * dh_ref[...])
        df_ref[...] = dfb
        for s in range(N_SHARD):
            da = _dot_nt(dfb, wd_v[s])
            gg = g_ref[s].astype(F32)
            uu = u_ref[s].astype(F32)
            sg = _sigmoid(gg)
            silu = gg * sg
            a_ref[s] = _b(silu * uu)
            dg_ref[s] = _b(da * uu * (sg * (1.0 + gg * (1.0 - sg))))
            du_ref[s] = _b(da * silu)

    row = pl.BlockSpec((tm, D), lambda i: (i, 0))
    cut = pl.BlockSpec((N_SHARD, tm, FF_CUT), lambda i: (0, i, 0))
    cut_shape = jax.ShapeDtypeStruct((N_SHARD, T, FF_CUT), BF16)
    return pl.pallas_call(
        _carry(carry, body, 4, 4, (T // tm,)), name=name, grid=(T // tm,),
        in_specs=[row, cut, cut, ANY] + ex_in,
        out_specs=[cut, cut, cut, row] + ex_out,
        out_shape=[cut_shape, cut_shape, cut_shape, jax.ShapeDtypeStruct((T, D), BF16)] + ex_shape,
        scratch_shapes=[pltpu.VMEM((N_SHARD, FF_CUT, D), BF16), pltpu.SemaphoreType.DMA((1,))] + ex_sems,
        compiler_params=_cp("arbitrary"),
    )(dh, g, u, wd, *ex_args)


def _ffn_bwd_input(dh, h, nw, dg, du, wg, wu, name, carry=None):
    T = h.shape[0]
    tm = min(BIG_ROW_BLK, T)
    ex_in, ex_args, ex_out, ex_shape, ex_sems = _carry_specs(carry)

    def body(dh_ref, h_ref, nw_ref, dg_ref, du_ref, wg_hbm, wu_hbm, dx_ref, dnw_ref, wg_v, wu_v, sem):
        @pl.when(pl.program_id(0) == 0)
        def _():
            _load_weights(((wg_hbm, wg_v), (wu_hbm, wu_v)), sem)
            dnw_ref[...] = jnp.zeros_like(dnw_ref)

        dn = jnp.zeros((tm, D), F32)
        for s in range(N_SHARD):
            dn += _dot_nt(dg_ref[s], wg_v[s]) + _dot_nt(du_ref[s], wu_v[s])
        x = h_ref[...]
        r = lax.rsqrt(jnp.mean(x * x, axis=-1, keepdims=True) + EPS)
        xhat = x * r
        dnw_ref[...] += jnp.sum(dn * xhat, axis=0, keepdims=True)
        dxhat = dn * nw_ref[...]
        dx_ref[...] = dh_ref[...] + r * (dxhat - xhat * jnp.mean(dxhat * xhat, axis=-1, keepdims=True))

    row = pl.BlockSpec((tm, D), lambda i: (i, 0))
    one = pl.BlockSpec((1, D), lambda i: (0, 0))
    cut = pl.BlockSpec((N_SHARD, tm, FF_CUT), lambda i: (0, i, 0))
    return pl.pallas_call(
        _carry(carry, body, 7, 2, (T // tm,)), name=name, grid=(T // tm,),
        in_specs=[row, row, one, cut, cut, ANY, ANY] + ex_in,
        out_specs=[row, one] + ex_out,
        out_shape=[jax.ShapeDtypeStruct((T, D), F32), jax.ShapeDtypeStruct((1, D), F32)] + ex_shape,
        scratch_shapes=[pltpu.VMEM((N_SHARD, D, FF_CUT), BF16), pltpu.VMEM((N_SHARD, D, FF_CUT), BF16),
                        pltpu.SemaphoreType.DMA((2,))] + ex_sems,
        compiler_params=_cp("arbitrary"),
    )(dh, h, nw, dg, du, wg, wu, *ex_args)


def _mm_tn(a, b, bm, bn, name, out_dtype=BF16, tk=2048):
    cuts = a.shape[0] if a.ndim == 3 else (b.shape[0] if b.ndim == 3 else None)
    T, M = a.shape[-2:]
    N = b.shape[-1]
    tk = min(tk, T)
    bm, bn = min(bm, M), min(bn, N)
    nk = T // tk

    def body(a_ref, b_ref, o_ref, acc_ref):
        k = pl.program_id(3)

        @pl.when(k == 0)
        def _():
            acc_ref[...] = jnp.zeros_like(acc_ref)

        av = a_ref[0] if a.ndim == 3 else a_ref[...]
        bv = b_ref[0] if b.ndim == 3 else b_ref[...]
        acc_ref[...] += _dot_tn(_b(av), _b(bv))

        @pl.when(k == nk - 1)
        def _():
            res = acc_ref[...].astype(out_dtype)
            if cuts is None:
                o_ref[...] = res
            else:
                o_ref[0] = res

    a_spec = (pl.BlockSpec((1, tk, bm), lambda s, i, j, k: (s, k, i)) if a.ndim == 3
              else pl.BlockSpec((tk, bm), lambda s, i, j, k: (k, i)))
    b_spec = (pl.BlockSpec((1, tk, bn), lambda s, i, j, k: (s, k, j)) if b.ndim == 3
              else pl.BlockSpec((tk, bn), lambda s, i, j, k: (k, j)))
    if cuts is None:
        o_spec, o_shape = pl.BlockSpec((bm, bn), lambda s, i, j, k: (i, j)), (M, N)
    else:
        o_spec, o_shape = pl.BlockSpec((1, bm, bn), lambda s, i, j, k: (s, i, j)), (cuts, M, N)
    return pl.pallas_call(
        body, name=name, grid=(cuts or 1, M // bm, N // bn, nk),
        in_specs=[a_spec, b_spec], out_specs=o_spec,
        out_shape=jax.ShapeDtypeStruct(o_shape, out_dtype),
        scratch_shapes=[pltpu.VMEM((bm, bn), F32)],
        compiler_params=_cp("parallel", "parallel", "parallel", "arbitrary"),
    )(a, b)


def _norm_proj(h, nw, wbig, wsmall, name, carry=None):
    T = h.shape[0]
    tm = min(512, T)
    tn = 1536
    qk = 2 * GLA_H * GLA_DK
    ex_in, ex_args, ex_out, ex_shape, ex_sems = _carry_specs(carry)

    def body(h_ref, nw_ref, wb_hbm, ws_ref, pb_ref, ps_ref, n_ref, qk_ref, wb_v, sem):
        @pl.when(pl.program_id(0) == 0)
        def _():
            _load_weights(((wb_hbm, wb_v),), sem)

        x = h_ref[...]
        r = lax.rsqrt(jnp.mean(x * x, axis=-1, keepdims=True) + EPS)
        nb = _b((x * r) * nw_ref[...])
        n_ref[...] = nb
        ps_ref[...] = _dot(nb, ws_ref[...])
        for j in range(BIG_COLS // tn):
            part = _dot(nb, wb_v[:, j * tn:(j + 1) * tn])
            pb_ref[:, j * tn:(j + 1) * tn] = _b(part)
            if j == 0:
                qk_ref[...] = part[:, :qk]

    row = lambda w: pl.BlockSpec((tm, w), lambda i: (i, 0))
    return pl.pallas_call(
        _carry(carry, body, 4, 4, (T // tm,)), name=name, grid=(T // tm,),
        in_specs=[row(D), pl.BlockSpec((1, D), lambda i: (0, 0)), ANY, pl.BlockSpec((D, SMALL), lambda i: (0, 0))]
        + ex_in,
        out_specs=[row(BIG_COLS), row(SMALL), row(D), row(qk)] + ex_out,
        out_shape=[jax.ShapeDtypeStruct((T, BIG_COLS), BF16), jax.ShapeDtypeStruct((T, SMALL), F32),
                   jax.ShapeDtypeStruct((T, D), BF16), jax.ShapeDtypeStruct((T, qk), F32)] + ex_shape,
        scratch_shapes=[pltpu.VMEM((D, BIG_COLS), BF16), pltpu.SemaphoreType.DMA((1,))] + ex_sems,
        compiler_params=_cp("arbitrary"),
    )(h, nw, wbig, wsmall, *ex_args)


def _proj_bwd(dh, h, nw, pieces, dsmall, wbig, wsmall, name):
    T = h.shape[0]
    tm = min(BIG_ROW_BLK, T)
    offs = tuple(int(v) for v in np.cumsum((0,) + PIECES))

    def body(dh_ref, h_ref, nw_ref, *rest):
        p_refs = rest[:len(PIECES)]
        ds_ref, wb_hbm, ws_ref, dx_ref, dnw_ref, wb_v, sem = rest[len(PIECES):]

        @pl.when(pl.program_id(0) == 0)
        def _():
            _load_weights(((wb_hbm, wb_v),), sem)
            dnw_ref[...] = jnp.zeros_like(dnw_ref)

        dn = _dot_nt(_b(ds_ref[...]), ws_ref[...])
        for p_ref, lo, wdt in zip(p_refs, offs, PIECES):
            dn += _dot_nt(p_ref[...], wb_v[:, lo:lo + wdt])
        x = h_ref[...]
        r = lax.rsqrt(jnp.mean(x * x, axis=-1, keepdims=True) + EPS)
        xhat = x * r
        dnw_ref[...] += jnp.sum(dn * xhat, axis=0, keepdims=True)
        dxhat = dn * nw_ref[...]
        dx_ref[...] = dh_ref[...] + r * (dxhat - xhat * jnp.mean(dxhat * xhat, axis=-1, keepdims=True))

    row = lambda w: pl.BlockSpec((tm, w), lambda i: (i, 0))
    one = pl.BlockSpec((1, D), lambda i: (0, 0))
    return pl.pallas_call(
        body, name=name, grid=(T // tm,),
        in_specs=[row(D), row(D), one] + [row(w) for w in PIECES] + [row(SMALL), ANY, pl.BlockSpec((D, SMALL), lambda i: (0, 0))],
        out_specs=[row(D), one],
        out_shape=[jax.ShapeDtypeStruct((T, D), F32), jax.ShapeDtypeStruct((1, D), F32)],
        scratch_shapes=[pltpu.VMEM((D, BIG_COLS), BF16), pltpu.SemaphoreType.DMA((1,))],
        compiler_params=_cp("arbitrary"),
    )(dh, h, nw, *pieces, dsmall, wbig, wsmall)


def _gla_block(q_ref, k_ref, sm_ref, wg_ref, bg_ref, nc, tril):
    nbat = GLA_H * nc
    q = _heads_first(q_ref[...].astype(F32), nc, GLA_DK)
    k = _heads_first(k_ref[...].astype(F32), nc, GLA_DK)
    pre = _heads_first(_dot(sm_ref[...], wg_ref[...], HI) + bg_ref[...], nc, GLA_DK)
    la = _log_sigmoid(pre) * (1.0 / GLA_TAU)
    bc = _bmm(jnp.broadcast_to(tril, (nbat, CHUNK, CHUNK)), la, HI)
    bl = bc[:, CHUNK - 1:CHUNK, :]
    eb = jnp.exp(bc)
    enb = jnp.exp(-bc)
    ebl = jnp.exp(bl - bc)
    q_in = q * (GLA_DK ** -0.5) * eb
    k_out = k * enb
    k_st = k * ebl
    a_ch = jnp.exp(bl)
    return pre, eb, enb, ebl, q_in, k_out, k_st, a_ch


def _gla_specs(blk, idx):
    hk, hv = GLA_H * GLA_DK, GLA_H * GLA_DV
    return [pl.BlockSpec((blk, hk), lambda j: (idx(j), 0)),
            pl.BlockSpec((blk, hk), lambda j: (idx(j), 1)),
            pl.BlockSpec((blk, hv), lambda j: (idx(j), 1)),
            pl.BlockSpec((blk, SMALL), lambda j: (idx(j), 0)),
            pl.BlockSpec((SMALL, hk), lambda j: (0, 0)),
            pl.BlockSpec((1, hk), lambda j: (0, 0))]


def _gla_fwd(pbig, qk, psmall, wgate, bgate, name):
    T = pbig.shape[0]
    blk = min(ATT_BLK, T)
    nc = blk // CHUNK

    def body(q_ref, k_ref, v_ref, sm_ref, wg_ref, bg_ref, o_ref, ss_ref, st_ref):
        @pl.when(pl.program_id(0) == 0)
        def _():
            st_ref[...] = jnp.zeros_like(st_ref)

        causal = _iota2(CHUNK, CHUNK, 0) >= _iota2(CHUNK, CHUNK, 1)
        _, _, _, _, q_in, k_out, k_st, a_ch = _gla_block(q_ref, k_ref, sm_ref, wg_ref, bg_ref, nc, causal.astype(F32))
        v = _heads_first(v_ref[...], nc, GLA_DV)
        qb = _b(q_in)
        sc = jnp.where(causal, _bmm_nt(qb, _b(k_out)), 0.0)
        kv = _bmm_tn(v, _b(k_st))
        before = [None] * (GLA_H * nc)
        for i in range(GLA_H):
            st = st_ref[i]
            for c in range(nc):
                n = i * nc + c
                before[n] = st
                st = st * a_ch[n] + kv[n]
            st_ref[i] = st
        states = jnp.stack(before)
        ss_ref[...] = _b(states.reshape(GLA_H, nc, GLA_DV, GLA_DK))
        o_ref[...] = _b(_heads_last(_bmm(_b(sc), v) + _bmm_nt(qb, _b(states)), nc))

    return pl.pallas_call(
        body, name=name, grid=(T // blk,),
        in_specs=_gla_specs(blk, lambda j: j),
        out_specs=[pl.BlockSpec((blk, GLA_H * GLA_DV), lambda j: (j, 0)),
                   pl.BlockSpec((GLA_H, nc, GLA_DV, GLA_DK), lambda j: (0, j, 0, 0))],
        out_shape=[jax.ShapeDtypeStruct((T, GLA_H * GLA_DV), BF16),
                   jax.ShapeDtypeStruct((GLA_H, T // CHUNK, GLA_DV, GLA_DK), BF16)],
        scratch_shapes=[pltpu.VMEM((GLA_H, GLA_DV, GLA_DK), F32)],
        compiler_params=_cp("arbitrary"),
    )(qk, qk, pbig, psmall, wgate, bgate)


def _gla_bwd(pbig, qk, psmall, wgate, bgate, states, do, name, carry=None):
    ex_in, ex_args, ex_out, ex_shape, ex_sems = _carry_specs(carry)
    T = pbig.shape[0]
    blk = min(ATT_BLK, T)
    nc = blk // CHUNK
    nb = T // blk
    nbat = GLA_H * nc

    def body(q_ref, k_ref, v_ref, sm_ref, wg_ref, bg_ref, ss_ref, do_ref, dq_ref, dk_ref, dv_ref, dpre_ref, dst_ref):
        @pl.when(pl.program_id(0) == 0)
        def _():
            dst_ref[...] = jnp.zeros_like(dst_ref)

        causal = _iota2(CHUNK, CHUNK, 0) >= _iota2(CHUNK, CHUNK, 1)
        triu = (_iota2(CHUNK, CHUNK, 0) <= _iota2(CHUNK, CHUNK, 1)).astype(F32)
        pre, eb, enb, ebl, q_in, k_out, k_st, a_ch = _gla_block(q_ref, k_ref, sm_ref, wg_ref, bg_ref, nc,
                                                                causal.astype(F32))
        v = _heads_first(v_ref[...], nc, GLA_DV)
        dob = _b(_heads_first(do_ref[...], nc, GLA_DV))
        st = ss_ref[...].reshape(nbat, GLA_DV, GLA_DK).astype(F32)
        qb, kob, kstb = _b(q_in), _b(k_out), _b(k_st)
        qdo = _bmm_tn(dob, qb)
        after = [None] * nbat
        for i in range(GLA_H):
            dst = dst_ref[i]
            for c in range(nc - 1, -1, -1):
                n = i * nc + c
                after[n] = dst
                dst = dst * a_ch[n] + qdo[n]
            dst_ref[i] = dst
        dsa = jnp.stack(after)
        dsb = _b(dsa)
        sc = jnp.where(causal, _bmm_nt(qb, kob), 0.0)
        dsc = _b(jnp.where(causal, _bmm_nt(dob, v), 0.0))
        dq_in = _bmm(dob, _b(st)) + _bmm(dsc, kob)
        dk_out = _bmm_tn(dsc, qb)
        dk_st = _bmm(v, dsb)
        dv_ref[...] = _b(_heads_last(_bmm_tn(_b(sc), dob) + _bmm_nt(kstb, dsb), nc))
        da_ch = jnp.sum(st * dsa, axis=1, keepdims=True)
        tk = dk_st * k_st
        db = dq_in * q_in - dk_out * k_out - tk
        db_last = jnp.sum(tk, axis=1, keepdims=True) + da_ch * a_ch
        dq_ref[...] = _b(_heads_last(dq_in * (GLA_DK ** -0.5) * eb, nc))
        dk_ref[...] = _b(_heads_last(dk_out * enb + dk_st * ebl, nc))
        dla = _bmm(jnp.broadcast_to(triu, (nbat, CHUNK, CHUNK)), db, HI) + db_last
        dpre_ref[...] = _heads_last(dla * (1.0 / GLA_TAU) * _sigmoid(-pre), nc)

    r = lambda j: nb - 1 - j
    hk, hv = GLA_H * GLA_DK, GLA_H * GLA_DV
    return pl.pallas_call(
        _carry(carry, body, 8, 4, (nb,)), name=name, grid=(nb,),
        in_specs=_gla_specs(blk, r) + [pl.BlockSpec((GLA_H, nc, GLA_DV, GLA_DK), lambda j: (0, r(j), 0, 0)),
                                      pl.BlockSpec((blk, hv), lambda j: (r(j), 0))] + ex_in,
        out_specs=[pl.BlockSpec((blk, hk), lambda j: (r(j), 0)), pl.BlockSpec((blk, hk), lambda j: (r(j), 0)),
                   pl.BlockSpec((blk, hv), lambda j: (r(j), 0)), pl.BlockSpec((blk, hk), lambda j: (r(j), 0))] + ex_out,
        out_shape=[jax.ShapeDtypeStruct((T, hk), BF16), jax.ShapeDtypeStruct((T, hk), BF16),
                   jax.ShapeDtypeStruct((T, hv), BF16), jax.ShapeDtypeStruct((T, hk), F32)] + ex_shape,
        scratch_shapes=[pltpu.VMEM((GLA_H, GLA_DV, GLA_DK), F32)] + ex_sems,
        compiler_params=_cp("arbitrary"),
    )(qk, qk, pbig, psmall, wgate, bgate, states, do, *ex_args)


def _gla_gate_bwd(dpre, psmall, wgate, dsm, name):
    T = dpre.shape[0]
    tm = min(512, T)
    W = GLA_H * GLA_DK
    ngrp = dsm.shape[0]

    def body(dp_ref, sm_ref, wg_ref, dsm_ref, ds_ref, dw_ref, db_ref):
        @pl.when(pl.program_id(0) == 0)
        def _():
            dw_ref[...] = jnp.zeros_like(dw_ref)
            db_ref[...] = jnp.zeros_like(db_ref)

        dp = dp_ref[...]
        ds = _dot_nt(dp, wg_ref[...], HI)
        for i in range(ngrp):
            ds += dsm_ref[i]
        ds_ref[...] = ds
        dw_ref[...] += _dot_tn(sm_ref[...], dp, HI)
        db_ref[...] += jnp.sum(dp, axis=0, keepdims=True)

    return pl.pallas_call(
        body, name=name, grid=(T // tm,),
        in_specs=[pl.BlockSpec((tm, W), lambda i: (i, 0)), pl.BlockSpec((tm, SMALL), lambda i: (i, 0)),
                  pl.BlockSpec((SMALL, W), lambda i: (0, 0)), pl.BlockSpec((ngrp, tm, SMALL), lambda i: (0, i, 0))],
        out_specs=[pl.BlockSpec((tm, SMALL), lambda i: (i, 0)), pl.BlockSpec((SMALL, W), lambda i: (0, 0)),
                   pl.BlockSpec((1, W), lambda i: (0, 0))],
        out_shape=[jax.ShapeDtypeStruct((T, SMALL), F32), jax.ShapeDtypeStruct((SMALL, W), F32),
                   jax.ShapeDtypeStruct((1, W), F32)],
        compiler_params=_cp("arbitrary"),
    )(dpre, psmall, wgate, dsm)


CONV_C = 3 * 1024
CONV_BLK = 256


def _conv_fwd(pbig, cw8, name):
    T = pbig.shape[0]
    blk = min(CONV_BLK, T)

    def body(x_ref, w_ref, c_ref, prev_ref):
        @pl.when(pl.program_id(0) == 0)
        def _():
            prev_ref[...] = jnp.zeros_like(prev_ref)

        x = x_ref[...].astype(F32)
        prev = prev_ref[...]
        row8 = _iota2(8, CONV_C, 0)
        acc = x * w_ref[CONV_K - 1:CONV_K, :]
        for s in range(1, CONV_K):
            xs = pltpu.roll(x, s, 0)
            top = jnp.where(row8 < s, pltpu.roll(prev, s, 0), xs[:8])
            xs = jnp.concatenate([top, xs[8:]], axis=0)
            acc += xs * w_ref[CONV_K - 1 - s:CONV_K - s, :]
        c_ref[...] = _b(acc)
        prev_ref[...] = x[blk - 8:]

    return pl.pallas_call(
        body, name=name, grid=(T // blk,),
        in_specs=[pl.BlockSpec((blk, CONV_C), lambda i: (i, 1)), pl.BlockSpec((8, CONV_C), lambda i: (0, 0))],
        out_specs=pl.BlockSpec((blk, CONV_C), lambda i: (i, 0)),
        out_shape=jax.ShapeDtypeStruct((T, CONV_C), BF16),
        scratch_shapes=[pltpu.VMEM((8, CONV_C), F32)],
        compiler_params=_cp("arbitrary"),
    )(pbig, cw8)


def _conv_bwd(dcq, dck, dcv, pbig, cw8, name):
    T = pbig.shape[0]
    blk = min(CONV_BLK, T)
    nb = T // blk

    def body(dq_ref, dk_ref, dv_ref, x_ref, w_ref, dx_ref, dw_ref, nxt_ref):
        @pl.when(pl.program_id(0) == 0)
        def _():
            nxt_ref[...] = jnp.zeros_like(nxt_ref)
            dw_ref[...] = jnp.zeros_like(dw_ref)

        dc = jnp.concatenate([dq_ref[...], dk_ref[...], dv_ref[...]], axis=1).astype(F32)
        x = x_ref[...].astype(F32)
        nxt = nxt_ref[...]
        row8 = _iota2(8, CONV_C, 0)
        acc = dc * w_ref[CONV_K - 1:CONV_K, :]
        dws = [jnp.sum(dc * x, axis=0, keepdims=True)]
        for s in range(1, CONV_K):
            ds = pltpu.roll(dc, blk - s, 0)
            bot = jnp.where(row8 >= 8 - s, pltpu.roll(nxt, 8 - s, 0), ds[blk - 8:])
            ds = jnp.concatenate([ds[:blk - 8], bot], axis=0)
            acc += ds * w_ref[CONV_K - 1 - s:CONV_K - s, :]
            dws.append(jnp.sum(ds * x, axis=0, keepdims=True))
        dx_ref[...] = _b(acc)
        dw_ref[...] += jnp.concatenate(dws[::-1] + [jnp.zeros((8 - CONV_K, CONV_C), F32)], axis=0)
        nxt_ref[...] = dc[:8]

    part = pl.BlockSpec((blk, 1024), lambda i: (nb - 1 - i, 0))
    return pl.pallas_call(
        body, name=name, grid=(nb,),
        in_specs=[part, part, part, pl.BlockSpec((blk, CONV_C), lambda i: (nb - 1 - i, 1)),
                  pl.BlockSpec((8, CONV_C), lambda i: (0, 0))],
        out_specs=[pl.BlockSpec((blk, CONV_C), lambda i: (nb - 1 - i, 0)), pl.BlockSpec((8, CONV_C), lambda i: (0, 0))],
        out_shape=[jax.ShapeDtypeStruct((T, CONV_C), BF16), jax.ShapeDtypeStruct((8, CONV_C), F32)],
        scratch_shapes=[pltpu.VMEM((8, CONV_C), F32)],
        compiler_params=_cp("arbitrary"),
    )(dcq, dck, dcv, pbig, cw8)


def _col(x, lane):
    if isinstance(lane, int):
        return jnp.broadcast_to(x[:, lane:lane + 1], x.shape)
    sel = _iota2(x.shape[0], x.shape[1], 1) == lane
    return jnp.broadcast_to(jnp.sum(jnp.where(sel, x, 0.0), axis=1, keepdims=True), x.shape)


def _bmm(a, b, prec=None):
    return jnp.einsum("bij,bjk->bik", a, b, preferred_element_type=F32, precision=prec)


def _bmm_nt(a, b, prec=None):
    return jnp.einsum("bij,bkj->bik", a, b, preferred_element_type=F32, precision=prec)


def _bmm_tn(a, b, prec=None):
    return jnp.einsum("bji,bjk->bik", a, b, preferred_element_type=F32, precision=prec)


def _unit_lower_inverse(low):
    eye = (_iota2(CHUNK, CHUNK, 0) == _iota2(CHUNK, CHUNK, 1)).astype(F32)
    xk = -low
    inv = eye + xk
    for _ in range(5):
        xb = _b(xk)
        xk = _bmm(xb, xb)
        inv = inv + _bmm(_b(inv), _b(xk))
    resid = eye - _bmm(eye + low, inv, HI)
    return inv + _bmm(inv, resid, HI)


def _heads_first(x, nc, w=128):
    hb = x.shape[1] // w
    return jnp.concatenate([x[:, i * w:(i + 1) * w].reshape(nc, CHUNK, w) for i in range(hb)], axis=0)


def _heads_last(x, nc):
    hb = x.shape[0] // nc
    return jnp.concatenate([x[i * nc:(i + 1) * nc].reshape(nc * CHUNK, x.shape[2]) for i in range(hb)], axis=1)


def _gdn_block(cq_ref, ck_ref, cv_ref, sm_ref, par_ref, h0, hb, nc, masks, solved=None):
    causal, strict, tril, eye = masks
    nbat = hb * nc
    cq = _heads_first(cq_ref[...].astype(F32), nc)
    ck = _heads_first(ck_ref[...].astype(F32), nc)
    cv = _heads_first(cv_ref[...].astype(F32), nc)
    sq, sk, sv = _sigmoid(cq), _sigmoid(ck), _sigmoid(cv)
    q, k, v = cq * sq, ck * sk, cv * sv
    rq = lax.rsqrt(jnp.sum(q * q, axis=-1, keepdims=True) + EPS)
    rk = lax.rsqrt(jnp.sum(k * k, axis=-1, keepdims=True) + EPS)
    qh, kn = q * rq, k * rk
    qn = qh * (DN_DK ** -0.5)
    sm = sm_ref[...]
    per_head = lambda fn: jnp.concatenate([fn(i) for i in range(hb)], axis=0)
    braw = per_head(lambda i: _col(sm, GLA_RANK + h0 + i).reshape(nc, CHUNK, 128))
    araw = per_head(lambda i: _col(sm, GLA_RANK + DN_H + h0 + i).reshape(nc, CHUNK, 128))
    ea = per_head(lambda i: jnp.broadcast_to(jnp.exp(par_ref[i, 0:1, :])[None], (nc, 1, 128)))
    bias = per_head(lambda i: jnp.broadcast_to(par_ref[i, 1:2, :][None], (nc, 1, 128)))
    beta = _sigmoid(braw)
    sp_arg = araw + bias
    g = -ea * _softplus(sp_arg)
    G = _bmm(jnp.broadcast_to(tril, (nbat, CHUNK, CHUNK)), g, HI)
    gc = G[:, :, :CHUNK]
    grow = jnp.sum(eye * gc, axis=1, keepdims=True)
    decay = jnp.exp(jnp.where(causal, gc - grow, -1e30))
    kb = kn * beta
    A = _bmm_nt(_b(kb), _b(kn))
    eG = jnp.exp(G)
    gl = G[:, CHUNK - 1:CHUNK, :]
    eGl = jnp.exp(gl - G)
    g_ch = jnp.exp(gl)
    rv = v * beta
    rkk = kb * eG
    if solved is None:
        tinv_b = _b(_unit_lower_inverse(jnp.where(strict, A * decay, 0.0)))
        u = _bmm(tinv_b, _b(rv))
        w = _b(_bmm(tinv_b, _b(rkk)))
    else:
        tinv_b, u, w = solved
    B = _bmm_nt(_b(qn), _b(kn))
    qk = jnp.where(causal, B * decay, 0.0)
    q_dec = qn * eG
    k_st = kn * eGl
    return dict(cq=cq, ck=ck, cv=cv, sq=sq, sk=sk, sv=sv, q=q, k=k, v=v, rq=rq, rk=rk, qh=qh, kn=kn, qn=qn,
                beta=beta, ea=ea, sp_arg=sp_arg, g=g, G=G, decay=decay, kb=kb, A=A, tinv_b=tinv_b, eG=eG, eGl=eGl,
                g_ch=g_ch, rv=rv, rkk=rkk, u=u, w=w, B=B, qk=qk, q_dec=q_dec, k_st=k_st)


def _gdn_masks():
    r, c = _iota2(CHUNK, CHUNK, 0), _iota2(CHUNK, CHUNK, 1)
    return r >= c, r > c, (r >= c).astype(F32), (r == c).astype(F32)


def _gdn_specs(blk, hb, idx):
    ng = DN_H // hb
    return [pl.BlockSpec((blk, hb * DN_DK), lambda h, j: (idx(j), h)),
            pl.BlockSpec((blk, hb * DN_DK), lambda h, j: (idx(j), ng + h)),
            pl.BlockSpec((blk, hb * DN_DV), lambda h, j: (idx(j), 2 * ng + h)),
            pl.BlockSpec((blk, SMALL), lambda h, j: (idx(j), 0)),
            pl.BlockSpec((hb, 8, 128), lambda h, j: (h, 0, 0))]


def _gdn_solved_specs(blk, hb, idx):
    nc = blk // CHUNK
    spec = lambda w: pl.BlockSpec((hb, nc, CHUNK, w), lambda h, j: (h, idx(j), 0, 0))
    return [spec(CHUNK), spec(DN_DV), spec(DN_DK)]


def _gdn_fwd(conv, psmall, par, name):
    T = conv.shape[0]
    blk = min(GDN_BLK, T)
    nc = blk // CHUNK
    hb = GDN_HEADS
    N = T // CHUNK

    def body(cq_ref, ck_ref, cv_ref, sm_ref, par_ref, o_ref, ss_ref, ti_ref, u_ref, w_ref, s_ref):
        @pl.when(pl.program_id(1) == 0)
        def _():
            s_ref[...] = jnp.zeros_like(s_ref)

        h0 = 0 if hb == DN_H else pl.program_id(0) * hb
        f = _gdn_block(cq_ref, ck_ref, cv_ref, sm_ref, par_ref, h0, hb, nc, _gdn_masks())
        ti_ref[...] = f["tinv_b"].reshape(hb, nc, CHUNK, CHUNK)
        u_ref[...] = f["u"].reshape(hb, nc, CHUNK, DN_DV)
        w_ref[...] = f["w"].reshape(hb, nc, CHUNK, DN_DK)
        wb, ub, kstb, qkb = f["w"], _b(f["u"]), _b(f["k_st"]), _b(f["qk"])
        mix = _b(_bmm_tn(kstb, wb))
        add = _bmm_tn(kstb, ub)
        q_eff = _b(f["q_dec"] - _bmm(qkb, wb))
        before = [None] * (hb * nc)
        S = [s_ref[i] for i in range(hb)]
        for c in range(nc):
            for i in range(hb):
                n = i * nc + c
                before[n] = S[i]
                S[i] = S[i] * f["g_ch"][n] - _dot(mix[n], _b(S[i])) + add[n]
        for i in range(hb):
            s_ref[i] = S[i]
        states = jnp.stack(before)
        ss_ref[...] = _b(states.reshape(hb, nc, DN_DK, DN_DV))
        o_ref[...] = _b(_heads_last(_bmm(qkb, ub) + _bmm(q_eff, _b(states)), nc))

    return pl.pallas_call(
        body, name=name, grid=(DN_H // hb, T // blk),
        in_specs=_gdn_specs(blk, hb, lambda j: j),
        out_specs=[pl.BlockSpec((blk, hb * DN_DV), lambda h, j: (j, h)),
                   pl.BlockSpec((hb, nc, DN_DK, DN_DV), lambda h, j: (h, j, 0, 0))]
        + _gdn_solved_specs(blk, hb, lambda j: j),
        out_shape=[jax.ShapeDtypeStruct((T, DN_H * DN_DV), BF16), jax.ShapeDtypeStruct((DN_H, N, DN_DK, DN_DV), BF16),
                   jax.ShapeDtypeStruct((DN_H, N, CHUNK, CHUNK), BF16), jax.ShapeDtypeStruct((DN_H, N, CHUNK, DN_DV), F32),
                   jax.ShapeDtypeStruct((DN_H, N, CHUNK, DN_DK), BF16)],
        scratch_shapes=[pltpu.VMEM((hb, DN_DK, DN_DV), F32)],
        compiler_params=_cp("parallel", "arbitrary"),
    )(conv, conv, conv, psmall, par)


def _gdn_bwd(conv, psmall, par, states, do, solved, name, carry=None):
    ex_in, ex_args, ex_out, ex_shape, ex_sems = _carry_specs(carry)
    T = conv.shape[0]
    blk = min(GDN_BLK, T)
    nc = blk // CHUNK
    nb = T // blk
    hb = GDN_HEADS
    nbat = hb * nc
    rsum = lambda x: jnp.sum(x, axis=-1, keepdims=True)

    def body(cq_ref, ck_ref, cv_ref, sm_ref, par_ref, ss_ref, do_ref, ti_ref, u_ref, w_ref,
             dcq_ref, dck_ref, dcv_ref, dsm_ref, dpar_ref, ds_ref):
        @pl.when(pl.program_id(1) == 0)
        def _():
            ds_ref[...] = jnp.zeros_like(ds_ref)
            dpar_ref[...] = jnp.zeros_like(dpar_ref)

        masks = _gdn_masks()
        causal, strict, tril, eye = masks
        triu = (_iota2(CHUNK, CHUNK, 0) <= _iota2(CHUNK, CHUNK, 1)).astype(F32)
        lane = _iota2(CHUNK, 128, 1)
        last_row = _iota2(CHUNK, 128, 0) == CHUNK - 1
        h0 = 0 if hb == DN_H else pl.program_id(0) * hb
        solved = (ti_ref[...].reshape(nbat, CHUNK, CHUNK), u_ref[...].reshape(nbat, CHUNK, DN_DV),
                  w_ref[...].reshape(nbat, CHUNK, DN_DK))
        f = _gdn_block(cq_ref, ck_ref, cv_ref, sm_ref, par_ref, h0, hb, nc, masks, solved)
        S = ss_ref[...].reshape(nbat, DN_DK, DN_DV).astype(F32)
        Sb = _b(S)
        do_ = _b(_heads_first(do_ref[...], nc))
        wb, qdb, kstb, qkb = _b(f["w"]), _b(f["q_dec"]), _b(f["k_st"]), _b(f["qk"])
        vnb = _b(f["u"] - _bmm(wb, Sb))
        dvn0 = _bmm_tn(qkb, do_)
        qdo = _bmm_tn(qdb, do_)
        dS = [ds_ref[i] for i in range(hb)]
        after = [None] * nbat
        for c in range(nc - 1, -1, -1):
            for i in range(hb):
                n = i * nc + c
                after[n] = dS[i]
                dvn_c = _b(dvn0[n] + _dot(kstb[n], _b(dS[i])))
                dS[i] = dS[i] * f["g_ch"][n] + qdo[n] - _dot_tn(wb[n], dvn_c)
        for i in range(hb):
            ds_ref[i] = dS[i]
        dSa = jnp.stack(after)
        dSb = _b(dSa)
        dvn = dvn0 + _bmm(kstb, dSb)
        dvnb = _b(dvn)
        dq_dec = _bmm_nt(do_, Sb)
        dqk = jnp.where(causal, _bmm_nt(do_, vnb), 0.0)
        dk_st = _bmm_nt(vnb, dSb)
        dg_ch = jnp.sum(rsum(S * dSa), axis=1, keepdims=True)
        dw = -_bmm_nt(dvnb, Sb)
        drv = _bmm_tn(f["tinv_b"], dvnb)
        drk = _bmm_tn(f["tinv_b"], _b(dw))
        dlow = jnp.where(strict, -(_bmm_nt(_b(drv), _b(f["u"])) + _bmm_nt(_b(drk), wb)), 0.0)
        dv = drv * f["beta"]
        dbeta = rsum(drv * f["v"])
        dkb = drk * f["eG"]
        dG = rsum(drk * f["rkk"])
        dA = dlow * f["decay"]
        ddec = dlow * f["A"]
        dkb += _bmm(_b(dA), _b(f["kn"]))
        dkn = _bmm_tn(_b(dA), _b(f["kb"]))
        dB = dqk * f["decay"]
        ddec += dqk * f["B"]
        dqn = _bmm(_b(dB), _b(f["kn"]))
        dkn += _bmm_tn(_b(dB), _b(f["qn"]))
        dD = ddec * f["decay"]
        dG += rsum(dD) - rsum(eye * jnp.sum(dD, axis=1, keepdims=True))
        dqn += dq_dec * f["eG"]
        dG += rsum(dq_dec * f["q_dec"])
        dkn += dk_st * f["eGl"]
        tks = rsum(dk_st * f["k_st"])
        dG -= tks
        dG_last = jnp.sum(tks, axis=1, keepdims=True) + dg_ch * f["g_ch"][:, :, :1]
        dkn += dkb * f["beta"]
        dbeta += rsum(dkb * f["kn"])
        dGf = jnp.broadcast_to(dG, (nbat, CHUNK, 128)) + jnp.where(last_row, dG_last, 0.0)
        dg = _bmm(jnp.broadcast_to(triu, (nbat, CHUNK, CHUNK)), dGf, HI)
        dbraw = dbeta * f["beta"][:, :, :1] * (1.0 - f["beta"][:, :, :1])
        daraw = dg * (-f["ea"]) * _sigmoid(f["sp_arg"])
        both = lambda t: jnp.sum(jnp.sum(t, axis=1, keepdims=True), axis=0)
        dgg = dg * f["g"]
        dsm = jnp.zeros((nc, CHUNK, SMALL), F32)
        for i in range(hb):
            mine = slice(i * nc, (i + 1) * nc)
            dsm += (jnp.where(lane == GLA_RANK + h0 + i, dbraw[mine], 0.0)
                    + jnp.where(lane == GLA_RANK + DN_H + h0 + i, daraw[mine], 0.0))
            dpar = jnp.where(lane[:1] == 0, both(dgg[mine]), jnp.where(lane[:1] == 1, both(daraw[mine]), 0.0))
            dpar_ref[i] += jnp.broadcast_to(dpar, (8, 128))
        dsm_ref[0] = dsm.reshape(blk, SMALL)
        dqh = dqn * (DN_DK ** -0.5)
        dq = f["rq"] * (dqh - f["qh"] * rsum(dqh * f["qh"]))
        dk = f["rk"] * (dkn - f["kn"] * rsum(dkn * f["kn"]))
        dsilu = lambda x, s: s * (1.0 + x * (1.0 - s))
        dcq_ref[...] = _b(_heads_last(dq * dsilu(f["cq"], f["sq"]), nc))
        dck_ref[...] = _b(_heads_last(dk * dsilu(f["ck"], f["sk"]), nc))
        dcv_ref[...] = _b(_heads_last(dv * dsilu(f["cv"], f["sv"]), nc))

    r = lambda j: nb - 1 - j
    out_blk = pl.BlockSpec((blk, hb * DN_DK), lambda h, j: (r(j), h))
    grid = (DN_H // hb, nb)
    return pl.pallas_call(
        _carry(carry, body, 10, 5, grid), name=name, grid=grid,
        in_specs=_gdn_specs(blk, hb, r) + [pl.BlockSpec((hb, nc, DN_DK, DN_DV), lambda h, j: (h, r(j), 0, 0)),
                                          pl.BlockSpec((blk, hb * DN_DV), lambda h, j: (r(j), h))]
        + _gdn_solved_specs(blk, hb, r) + ex_in,
        out_specs=[out_blk, out_blk, out_blk, pl.BlockSpec((1, blk, SMALL), lambda h, j: (h, r(j), 0)),
                   pl.BlockSpec((hb, 8, 128), lambda h, j: (h, 0, 0))] + ex_out,
        out_shape=[jax.ShapeDtypeStruct((T, DN_H * DN_DK), BF16)] * 3 + [
            jax.ShapeDtypeStruct((DN_H // hb, T, SMALL), F32), jax.ShapeDtypeStruct((DN_H, 8, 128), F32)] + ex_shape,
        scratch_shapes=[pltpu.VMEM((hb, DN_DK, DN_DV), F32)] + ex_sems,
        compiler_params=_cp("arbitrary", "arbitrary"),
    )(conv, conv, conv, psmall, par, states, do, *solved, *ex_args)


def _head_norm(o, w, dv):
    outs, rs = [], []
    for i in range(o.shape[1] // dv):
        oh = o[:, i * dv:(i + 1) * dv]
        r = lax.rsqrt(jnp.mean(oh * oh, axis=-1, keepdims=True) + EPS)
        outs.append(oh * r)
        rs.append(r)
    return outs, rs


def _merge_specs(tm):
    col = lambda c: pl.BlockSpec((tm, D), lambda i: (i, c))
    return [col(0), col(0), col(2), col(6), col(7), col(8),
            pl.BlockSpec((1, GLA_DV), lambda i: (0, 0)), pl.BlockSpec((1, DN_DV), lambda i: (0, 0)),
            pl.BlockSpec((D, D), lambda i: (0, 0))]


def _merge_fwd(h, oa, ob, pbig, gla_hn, dn_hn, wout, name):
    T = h.shape[0]
    tm = min(ROW_BLK, T)

    def body(h_ref, oa_ref, ob_ref, gr_ref, dg_ref, ma_ref, mb_ref, wa_ref, wb_ref, wo_ref, ho_ref, y_ref):
        na, _ = _head_norm(oa_ref[...].astype(F32), wa_ref[...], GLA_DV)
        nbs, _ = _head_norm(ob_ref[...].astype(F32), wb_ref[...], DN_DV)
        hna = jnp.concatenate([t * wa_ref[...] for t in na], axis=1)
        hnb = jnp.concatenate([t * wb_ref[...] for t in nbs], axis=1)
        gr = gr_ref[...].astype(F32)
        dg = dg_ref[...].astype(F32)
        y = (_sigmoid(ma_ref[...].astype(F32)) * hna * (gr * _sigmoid(gr))
             + _sigmoid(mb_ref[...].astype(F32)) * hnb * (dg * _sigmoid(dg)))
        yb = _b(y)
        y_ref[...] = yb
        ho_ref[...] = h_ref[...] + _dot(yb, wo_ref[...])

    row = pl.BlockSpec((tm, D), lambda i: (i, 0))
    return pl.pallas_call(
        body, name=name, grid=(T // tm,),
        in_specs=[row] + _merge_specs(tm),
        out_specs=[row, row],
        out_shape=[jax.ShapeDtypeStruct((T, D), F32), jax.ShapeDtypeStruct((T, D), BF16)],
        compiler_params=_cp("arbitrary"),
    )(h, oa, ob, pbig, pbig, pbig, pbig, gla_hn, dn_hn, wout)


def _merge_bwd(dh, oa, ob, pbig, gla_hn, dn_hn, wout, name, carry=None):
    T = dh.shape[0]
    tm = min(ROW_BLK, T)
    ex_in, ex_args, ex_out, ex_shape, ex_sems = _carry_specs(carry)

    def branch(dy, o_ref, w_ref, gate_ref, m_ref, dv):
        w = w_ref[...]
        ohat, rs = _head_norm(o_ref[...].astype(F32), w, dv)
        gate = gate_ref[...].astype(F32)
        m = m_ref[...].astype(F32)
        sgate, sm = _sigmoid(gate), _sigmoid(m)
        silu = gate * sgate
        ohat_all = jnp.concatenate(ohat, axis=1)
        hn = jnp.concatenate([t * w for t in ohat], axis=1)
        d_on = dy * sm
        d_m = dy * hn * silu * sm * (1.0 - sm)
        d_hn = d_on * silu
        d_gate = d_on * hn * (sgate * (1.0 + gate * (1.0 - sgate)))
        dw = jnp.zeros((1, dv), F32)
        d_o = []
        for i, (oh, r) in enumerate(zip(ohat, rs)):
            dhn = d_hn[:, i * dv:(i + 1) * dv]
            dw += jnp.sum(dhn * oh, axis=0, keepdims=True)
            dohat = dhn * w
            d_o.append(r * (dohat - oh * jnp.mean(dohat * oh, axis=-1, keepdims=True)))
        return jnp.concatenate(d_o, axis=1), d_gate, d_m, dw

    def body(dh_ref, oa_ref, ob_ref, gr_ref, dg_ref, ma_ref, mb_ref, wa_ref, wb_ref, wo_ref,
             doa_ref, dob_ref, dgr_ref, ddg_ref, dma_ref, dmb_ref, dwa_ref, dwb_ref, dhb_ref):
        @pl.when(pl.program_id(0) == 0)
        def _():
            dwa_ref[...] = jnp.zeros_like(dwa_ref)
            dwb_ref[...] = jnp.zeros_like(dwb_ref)

        dhb = _b(dh_ref[...])
        dhb_ref[...] = dhb
        dy = _dot_nt(dhb, wo_ref[...])
        d_oa, d_gr, d_ma, dwa = branch(dy, oa_ref, wa_ref, gr_ref, ma_ref, GLA_DV)
        d_ob, d_dg, d_mb, dwb = branch(dy, ob_ref, wb_ref, dg_ref, mb_ref, DN_DV)
        doa_ref[...] = _b(d_oa)
        dob_ref[...] = _b(d_ob)
        dgr_ref[...] = _b(d_gr)
        ddg_ref[...] = _b(d_dg)
        dma_ref[...] = _b(d_ma)
        dmb_ref[...] = _b(d_mb)
        dwa_ref[...] += dwa
        dwb_ref[...] += dwb

    row = pl.BlockSpec((tm, D), lambda i: (i, 0))
    b16 = jax.ShapeDtypeStruct((T, D), BF16)
    return pl.pallas_call(
        _carry(carry, body, 10, 9, (T // tm,)), name=name, grid=(T // tm,),
        in_specs=[row] + _merge_specs(tm) + ex_in,
        out_specs=[row] * 6 + [pl.BlockSpec((1, GLA_DV), lambda i: (0, 0)), pl.BlockSpec((1, DN_DV), lambda i: (0, 0)), row]
        + ex_out,
        out_shape=[b16, b16, b16, b16, b16, b16, jax.ShapeDtypeStruct((1, GLA_DV), F32),
                   jax.ShapeDtypeStruct((1, DN_DV), F32), b16] + ex_shape,
        scratch_shapes=ex_sems,
        compiler_params=_cp("arbitrary"),
    )(dh, oa, ob, pbig, pbig, pbig, pbig, gla_hn, dn_hn, wout, *ex_args)


def _adamw(w, g, m, v, rows, name):
    R, C = w.shape
    rows = min(rows, R)
    c1 = 1.0 - ADAM_B1 ** ADAM_STEP
    c2 = 1.0 - ADAM_B2 ** ADAM_STEP

    def body(w_ref, g_ref, m_ref, v_ref, d_ref, mo_ref, vo_ref):
        g_ = g_ref[...]
        m_ = ADAM_B1 * m_ref[...] + (1.0 - ADAM_B1) * g_
        v_ = ADAM_B2 * v_ref[...] + (1.0 - ADAM_B2) * (g_ * g_)
        mo_ref[...] = m_
        vo_ref[...] = v_
        d_ref[...] = -ADAM_LR * ((m_ / c1) / (jnp.sqrt(v_ / c2) + ADAM_EPS) + ADAM_WD * w_ref[...])

    blk = pl.BlockSpec((rows, C), lambda i: (i, 0))
    shp = jax.ShapeDtypeStruct((R, C), F32)
    return pl.pallas_call(
        body, name=name, grid=(R // rows,),
        in_specs=[blk] * 4, out_specs=[blk] * 3, out_shape=[shp] * 3,
        compiler_params=_cp("parallel"),
    )(w, g, m, v)


def _me():
    return lax.axis_index("x"), lax.axis_index("y"), lax.axis_index("c")


def _other_chips(x, y):
    return [(1 - x, y), (x, 1 - y), (1 - x, 1 - y)]


def _half_rows(ref, hf):
    half = ref.shape[-2] // 2
    rows = pl.ds(pl.multiple_of(hf * half, 16), half)
    return ref.at[rows, :] if len(ref.shape) == 2 else ref.at[:, rows, :]


class _GatherBig:
    def __init__(self, big):
        self.arrays = list(big)
        self.out_shape = [jax.ShapeDtypeStruct((N_SHARD,) + w.shape, w.dtype) for w in big]
        self.n_sem = 7 * len(big)

    @staticmethod
    def _copy(sems, k, src, dst, to):
        return pltpu.make_async_remote_copy(src_ref=src, dst_ref=dst, send_sem=sems[0].at[k], recv_sem=sems[1].at[k],
                                            device_id=to, device_id_type=MESH)

    def start(self, ins, outs, *sems):
        x, y, c = _me()
        mine = 2 * x + y
        for i, (w_ref, o_ref) in enumerate(zip(ins, outs)):
            self._copy(sems, 7 * i + 6, w_ref, o_ref.at[mine], (x, y, 1 - c)).start()
            for j, chip in enumerate(_other_chips(x, y)):
                self._copy(sems, 7 * i + j, _half_rows(w_ref, c), _half_rows(o_ref.at[mine], c), (*chip, c)).start()

    def relay(self, ins, outs, *sems):
        x, y, c = _me()
        for i, o_ref in enumerate(outs):
            for j, chip in enumerate(_other_chips(x, y)):
                landed = _half_rows(o_ref.at[2 * chip[0] + chip[1]], c)
                self._copy(sems, 7 * i + j, landed, landed, (x, y, c)).wait_recv()
                self._copy(sems, 7 * i + 3 + j, landed, landed, (x, y, 1 - c)).start()

    def finish(self, ins, outs, *sems):
        x, y, c = _me()
        me, sibling = (x, y, c), (x, y, 1 - c)
        chips = _other_chips(x, y)
        slot = lambda chip: 2 * chip[0] + chip[1]
        for i, (w_ref, o_ref) in enumerate(zip(ins, outs)):
            for j, chip in enumerate(chips):
                passed = _half_rows(o_ref.at[slot(chip)], 1 - c)
                self._copy(sems, 7 * i + 3 + j, passed, passed, me).wait_recv()
            self._copy(sems, 7 * i + 6, o_ref.at[slot((x, y))], o_ref.at[slot((x, y))], me).wait_recv()
        for i, (w_ref, o_ref) in enumerate(zip(ins, outs)):
            self._copy(sems, 7 * i + 6, w_ref, o_ref.at[slot((x, y))], sibling).wait_send()
            for j, chip in enumerate(chips):
                self._copy(sems, 7 * i + j, _half_rows(w_ref, c), _half_rows(o_ref.at[slot((x, y))], c),
                           (*chip, c)).wait_send()
                landed = _half_rows(o_ref.at[slot(chip)], c)
                self._copy(sems, 7 * i + 3 + j, landed, landed, sibling).wait_send()


class _SiblingExchange:
    def __init__(self, gs):
        self.arrays = list(gs)
        self.out_shape = [jax.ShapeDtypeStruct((g.shape[0], g.shape[1] // 2, g.shape[2]), g.dtype) for g in gs]
        self.n_sem = len(gs)

    def _copies(self, ins, outs, send_sems, recv_sems):
        x, y, c = _me()
        return [pltpu.make_async_remote_copy(src_ref=_half_rows(ins[i], 1 - c), dst_ref=outs[i],
                                             send_sem=send_sems.at[i], recv_sem=recv_sems.at[i],
                                             device_id=(x, y, 1 - c), device_id_type=MESH) for i in range(len(ins))]

    def start(self, ins, outs, *sems):
        for cp in self._copies(ins, outs, *sems):
            cp.start()

    def finish(self, ins, outs, *sems):
        for cp in self._copies(ins, outs, *sems):
            cp.wait()


class _ChipsExchange:
    def __init__(self, pbs):
        self.arrays = list(pbs)
        self.out_shape = [jax.ShapeDtypeStruct((3,) + p.shape[1:], p.dtype) for p in pbs]
        self.n_sem = 3 * len(pbs)

    def _copies(self, ins, outs, send_sems, recv_sems):
        x, y, c = _me()
        return [pltpu.make_async_remote_copy(src_ref=ins[i].at[2 * chip[0] + chip[1]], dst_ref=outs[i].at[j],
                                             send_sem=send_sems.at[3 * i + j], recv_sem=recv_sems.at[3 * i + j],
                                             device_id=(*chip, c), device_id_type=MESH)
                for i in range(len(ins)) for j, chip in enumerate(_other_chips(x, y))]

    def start(self, ins, outs, *sems):
        for cp in self._copies(ins, outs, *sems):
            cp.start()

    def finish(self, ins, outs, *sems):
        for cp in self._copies(ins, outs, *sems):
            cp.wait()


def _carry(ex, body, n_in, n_out, grid):
    if ex is None:
        return body
    ni, no = len(ex.arrays), len(ex.out_shape)

    def carried(*refs):
        ins, ex_in = refs[:n_in], refs[n_in:n_in + ni]
        outs, ex_out = refs[n_in + ni:n_in + ni + n_out], refs[n_in + ni + n_out:n_in + ni + n_out + no]
        scratch, sems = refs[n_in + ni + n_out + no:-2], refs[-2:]
        step = functools.reduce(lambda acc, a: acc * grid[a] + pl.program_id(a), range(len(grid)), 0)
        steps = math.prod(grid)

        @pl.when(step == 0)
        def _():
            ex.start(ex_in, ex_out, *sems)

        body(*ins, *outs, *scratch)

        if hasattr(ex, "relay"):
            @pl.when(step == (3 * steps) // 4)
            def _():
                ex.relay(ex_in, ex_out, *sems)

        @pl.when(step == steps - 1)
        def _():
            ex.finish(ex_in, ex_out, *sems)

    return carried


def _carry_specs(ex):
    if ex is None:
        return [], [], [], [], []
    sems = [pltpu.SemaphoreType.DMA((ex.n_sem,)), pltpu.SemaphoreType.DMA((ex.n_sem,))]
    return [ANY] * len(ex.arrays), ex.arrays, [ANY] * len(ex.out_shape), ex.out_shape, sems


def _gather_weights(big, small, name):
    nbig, nsm = len(big), len(small)
    n = nbig + nsm
    own_sem = 6 * nbig + 3 * nsm

    def body(*refs):
        ins, outs = refs[:n], refs[n:2 * n]
        send_sems, recv_sems = refs[2 * n:]
        x, y, c = _me()
        sibling = (x, y, 1 - c)
        chips = _other_chips(x, y)
        slot = lambda chip: 2 * chip[0] + chip[1]

        def copy(k, src, dst, to):
            return pltpu.make_async_remote_copy(src_ref=src, dst_ref=dst, send_sem=send_sems.at[k],
                                                recv_sem=recv_sems.at[k], device_id=to, device_id_type=MESH)

        sent = []
        for i in range(nbig):
            sent.append(copy(own_sem + i, ins[i], outs[i].at[slot((x, y))], sibling))
            sent[-1].start()
            for j, chip in enumerate(chips):
                sent.append(copy(6 * i + j, _half_rows(ins[i], c), _half_rows(outs[i].at[slot((x, y))], c), (*chip, c)))
                sent[-1].start()
        for t in range(nsm):
            w_ref, o_ref = ins[nbig + t], outs[nbig + t]
            o_ref[slot((x, y))] = w_ref[...]
            for j, chip in enumerate(chips):
                sent.append(copy(6 * nbig + 3 * t + j, w_ref, o_ref.at[slot((x, y))], (*chip, c)))
                sent[-1].start()
        for i in range(nbig):
            for j, chip in enumerate(chips):
                landed = _half_rows(outs[i].at[slot(chip)], c)
                copy(6 * i + j, landed, landed, (x, y, c)).wait_recv()
                sent.append(copy(6 * i + 3 + j, landed, landed, sibling))
                sent[-1].start()
        for t in range(nsm):
            for j, chip in enumerate(chips):
                landed = outs[nbig + t].at[slot(chip)]
                copy(6 * nbig + 3 * t + j, landed, landed, (x, y, c)).wait_recv()
        for i in range(nbig):
            for j, chip in enumerate(chips):
                passed = _half_rows(outs[i].at[slot(chip)], 1 - c)
                copy(6 * i + 3 + j, passed, passed, (x, y, c)).wait_recv()
        for i in range(nbig):
            mine = outs[i].at[slot((x, y))]
            copy(own_sem + i, mine, mine, (x, y, c)).wait_recv()
        for cp in sent:
            cp.wait_send()

    vm = pl.BlockSpec(memory_space=pltpu.VMEM)
    nsem = own_sem + nbig
    return pl.pallas_call(
        body, name=name, in_specs=[ANY] * nbig + [vm] * nsm, out_specs=[ANY] * nbig + [vm] * nsm,
        out_shape=[jax.ShapeDtypeStruct((N_SHARD,) + w.shape, w.dtype) for w in list(big) + list(small)],
        scratch_shapes=[pltpu.SemaphoreType.DMA((nsem,)), pltpu.SemaphoreType.DMA((nsem,))],
        compiler_params=pltpu.CompilerParams(has_side_effects=True),
    )(*big, *small)


def _rs_sibling(gs, name):
    n = len(gs)

    def body(*refs):
        send_sems, recv_sems = refs[2 * n:]
        x, y, c = _me()
        cps = [pltpu.make_async_remote_copy(src_ref=_half_rows(refs[i], 1 - c), dst_ref=refs[n + i],
                                            send_sem=send_sems.at[i], recv_sem=recv_sems.at[i],
                                            device_id=(x, y, 1 - c), device_id_type=MESH) for i in range(n)]
        for cp in cps:
            cp.start()
        for cp in cps:
            cp.wait()

    return pl.pallas_call(
        body, name=name, in_specs=[ANY] * n, out_specs=[ANY] * n,
        out_shape=[jax.ShapeDtypeStruct((g.shape[0], g.shape[1] // 2, g.shape[2]), g.dtype) for g in gs],
        scratch_shapes=[pltpu.SemaphoreType.DMA((n,)), pltpu.SemaphoreType.DMA((n,))],
        compiler_params=pltpu.CompilerParams(has_side_effects=True),
    )(*gs)


def _add_pair(g, other, where, name):
    ns, a, b = g.shape
    half = a // 2

    def body(w_ref, g_ref, o_ref, pb_ref, own_ref):
        t = g_ref[0].astype(F32) + o_ref[0].astype(F32)
        pb_ref[0] = _b(t)

        @pl.when(pl.program_id(0) == w_ref[1])
        def _():
            own_ref[...] = t

    return pl.pallas_call(
        body, name=name,
        grid_spec=pltpu.PrefetchScalarGridSpec(
            num_scalar_prefetch=1, grid=(ns,),
            in_specs=[pl.BlockSpec((1, half, b), lambda s, w: (s, w[0], 0)), pl.BlockSpec((1, half, b), lambda s, w: (s, 0, 0))],
            out_specs=[pl.BlockSpec((1, half, b), lambda s, w: (s, 0, 0)), pl.BlockSpec((half, b), lambda s, w: (0, 0))]),
        out_shape=[jax.ShapeDtypeStruct((ns, half, b), BF16), jax.ShapeDtypeStruct((half, b), F32)],
        compiler_params=_cp("arbitrary"),
    )(where, g, other)


def _add_four(own, got, name):
    rows, cols = own.shape
    rb = rows // 2

    def body(a_ref, b_ref, o_ref):
        o_ref[...] = ((a_ref[...] + b_ref[0].astype(F32)) + b_ref[1].astype(F32)) + b_ref[2].astype(F32)

    return pl.pallas_call(
        body, name=name, grid=(rows // rb,),
        in_specs=[pl.BlockSpec((rb, cols), lambda i: (i, 0)), pl.BlockSpec((3, rb, cols), lambda i: (0, i, 0))],
        out_specs=pl.BlockSpec((rb, cols), lambda i: (i, 0)),
        out_shape=jax.ShapeDtypeStruct((rows, cols), F32),
        compiler_params=_cp("parallel"),
    )(own, got)


def _rs_swap(halves, name):
    n = len(halves)

    def body(*refs):
        send_sems, recv_sems = refs[2 * n:]
        x, y, c = _me()
        cps = [pltpu.make_async_remote_copy(src_ref=refs[i], dst_ref=refs[n + i], send_sem=send_sems.at[i],
                                            recv_sem=recv_sems.at[i], device_id=(x, y, 1 - c), device_id_type=MESH)
               for i in range(n)]
        for cp in cps:
            cp.start()
        for cp in cps:
            cp.wait()

    return pl.pallas_call(
        body, name=name, in_specs=[ANY] * n, out_specs=[ANY] * n,
        out_shape=[jax.ShapeDtypeStruct(h.shape, h.dtype) for h in halves],
        scratch_shapes=[pltpu.SemaphoreType.DMA((n,)), pltpu.SemaphoreType.DMA((n,))],
        compiler_params=pltpu.CompilerParams(has_side_effects=True),
    )(*halves)


def _adamw_halves(w, own, got, m, v, rows, name):
    a, b = w.shape
    nblk = a // 2 // rows
    c1 = 1.0 - ADAM_B1 ** ADAM_STEP
    c2 = 1.0 - ADAM_B2 ** ADAM_STEP

    def body(w_ref, own_ref, got_ref, m_ref, v_ref, g_ref, d_ref, mo_ref, vo_ref):
        g_ = jnp.where(pl.program_id(0) == lax.axis_index("c"), own_ref[...], got_ref[...])
        g_ref[...] = g_
        m_ = ADAM_B1 * m_ref[...] + (1.0 - ADAM_B1) * g_
        v_ = ADAM_B2 * v_ref[...] + (1.0 - ADAM_B2) * (g_ * g_)
        mo_ref[...] = m_
        vo_ref[...] = v_
        d_ref[...] = -ADAM_LR * ((m_ / c1) / (jnp.sqrt(v_ / c2) + ADAM_EPS) + ADAM_WD * w_ref[...])

    whole = pl.BlockSpec((rows, b), lambda h, i: (h * nblk + i, 0))
    part = pl.BlockSpec((rows, b), lambda h, i: (i, 0))
    shp = jax.ShapeDtypeStruct((a, b), F32)
    return pl.pallas_call(
        body, name=name, grid=(2, nblk),
        in_specs=[whole, part, part, whole, whole], out_specs=[whole] * 4, out_shape=[shp] * 4,
        compiler_params=_cp("parallel", "parallel"),
    )(w, own, got, m, v)


def _allsum_small(vec, name):
    def body(v_ref, o_ref, buf_ref, send_sems, recv_sems):
        x, y, c = _me()
        me = 4 * x + 2 * y + c
        buf_ref[me] = v_ref[...]
        cps = []
        for k in range(1, 8):
            peer = (x ^ (k >> 2), y ^ ((k >> 1) & 1), c ^ (k & 1))
            cps.append(pltpu.make_async_remote_copy(src_ref=v_ref, dst_ref=buf_ref.at[me],
                                                    send_sem=send_sems.at[k - 1], recv_sem=recv_sems.at[k - 1],
                                                    device_id=peer, device_id_type=MESH))
        for cp in cps:
            cp.start()
        for k in range(1, 8):
            peer_idx = me ^ k
            pltpu.make_async_remote_copy(src_ref=v_ref, dst_ref=buf_ref.at[peer_idx],
                                         send_sem=send_sems.at[k - 1], recv_sem=recv_sems.at[k - 1],
                                         device_id=(x, y, c), device_id_type=MESH).wait_recv()
        for cp in cps:
            cp.wait_send()
        acc = buf_ref[0]
        for d in range(1, 8):
            acc = acc + buf_ref[d]
        o_ref[...] = acc

    return pl.pallas_call(
        body, name=name,
        in_specs=[pl.BlockSpec(memory_space=pltpu.VMEM)], out_specs=pl.BlockSpec(memory_space=pltpu.VMEM),
        out_shape=jax.ShapeDtypeStruct(vec.shape, F32),
        scratch_shapes=[pltpu.VMEM((8,) + vec.shape, F32), pltpu.SemaphoreType.DMA((7,)), pltpu.SemaphoreType.DMA((7,))],
        compiler_params=pltpu.CompilerParams(has_side_effects=True),
    )(vec)


BIG = ("ffn1_w_gate", "ffn1_w_up", "ffn1_w_down", "w_in", "w_out", "ffn2_w_gate", "ffn2_w_up", "ffn2_w_down")
TINY = ("w_gla_gate", "conv_w")
SHARDED = BIG + TINY


def _join_cols(w4):
    return jnp.transpose(w4, (1, 0, 2)).reshape(w4.shape[1], N_SHARD * w4.shape[2])


def _cut_cols(w):
    return jnp.transpose(w.reshape(w.shape[0], N_SHARD, w.shape[1] // N_SHARD), (1, 0, 2))


def _split_w_in(w):
    o = IN_OFF
    big = jnp.concatenate([w[:, :o[4]], w[:, o[5]:o[9]], w[:, o[11]:]], axis=1)
    small = jnp.concatenate([w[:, o[4]:o[5]], w[:, o[9]:o[11]], jnp.zeros((w.shape[0], SMALL - 32), w.dtype)], axis=1)
    return big, small


def _merge_w_in(big, small):
    return jnp.concatenate([big[:, :3072], small[:, :16], big[:, 3072:7168], small[:, 16:32], big[:, 7168:]], axis=1)


class _Comm:
    def __init__(self, where, rest_shards):
        self.where = where
        self.w_in = _GatherBig(rest_shards[:1])
        self.later = _GatherBig(rest_shards[1:])
        self.pairs, self.got = {}, {}

    @staticmethod
    def w_in_weights(gathered):
        return dict(zip(("w_in_big", "w_in_small"), _split_w_in(_join_cols(gathered[0]))))

    @staticmethod
    def later_weights(gathered):
        W = dict(zip(BIG[4:], gathered))
        W["w_out"] = W["w_out"].reshape(D, D)
        return W

    def pair(self, names, grads, from_sibling):
        for n, g, o in zip(names, grads, from_sibling):
            self.pairs[n] = _add_pair(g, o, self.where, "rs_pair_" + n)
        return _ChipsExchange([self.pairs[n][0] for n in names])

    def begin(self, names, grads):
        return self.pair(names, grads, _rs_sibling(grads, "rs_sibling_" + names[0]))

    def landed(self, names, outs):
        self.got.update(zip(names, outs))


def _local_step(x, target, W, P, comm=None):
    wgate_pad = jnp.zeros((SMALL, GLA_H * GLA_DK), F32).at[:GLA_RANK].set(P["w_gla_gate"])
    cw8 = jnp.zeros((8, CONV_C), F32).at[:CONV_K].set(P["conv_w"])
    par = jnp.zeros((DN_H, 8, 128), F32)
    par = par.at[:, 0, :].set(jnp.broadcast_to(P["dn_a_log"].reshape(DN_H, 1), (DN_H, 128)))
    par = par.at[:, 1, :].set(jnp.broadcast_to(P["dn_dt_bias"].reshape(DN_H, 1), (DN_H, 128)))

    h1, n1, g1, u1, *got = _ffn_fwd(x, P["ffn1_norm"], W["ffn1_w_gate"], W["ffn1_w_up"], W["ffn1_w_down"], "ffn1_fwd",
                                    carry=comm.w_in if comm else None)
    if comm:
        W = dict(W, **comm.w_in_weights(got))
    wbig, wsmall = W["w_in_big"], W["w_in_small"]
    pbig, psmall, n2, qk, *got = _norm_proj(h1, P["mix_norm"], wbig, wsmall, "mix_proj",
                                            carry=comm.later if comm else None)
    if comm:
        W = dict(W, **comm.later_weights(got))
    oa, sa = _gla_fwd(pbig, qk, psmall, wgate_pad, P["b_gla_gate"], "gla_fwd")
    conv = _conv_fwd(pbig, cw8, "conv_fwd")
    ob, sb, *solved = _gdn_fwd(conv, psmall, par, "gdn_fwd")
    h2, yb = _merge_fwd(h1, oa, ob, pbig, P["gla_head_norm"], P["dn_head_norm"], W["w_out"], "merge_fwd")
    dh3, n3, g3, u3, loss, d_final = _ffn_fwd(h2, P["ffn2_norm"], W["ffn2_w_gate"], W["ffn2_w_up"], W["ffn2_w_down"],
                                              "ffn2_fwd", head=(P["final_norm"], target))

    gw, gs = {}, {"final_norm": d_final}

    def ffn_grads(tag, dh, h, n, g, u, before=None, later=False):
        names = tuple(tag + s for s in ("_w_gate", "_w_up", "_w_down"))
        dg, du, act, dfb, *landed = _ffn_bwd_hidden(dh, g, u, W[names[2]], tag + "_bwd_hidden", carry=before)
        gw[names[0]] = _mm_tn(n, dg, D, FF_CUT, tag + "_dwg", tk=4096)
        gw[names[1]] = _mm_tn(n, du, D, FF_CUT, tag + "_dwu", tk=4096)
        gw[names[2]] = _mm_tn(act, dfb, FF_CUT, D, tag + "_dwd", tk=4096)
        mine = [gw[n] for n in names]
        ex = None if not comm else _SiblingExchange(mine) if later else comm.begin(names, mine)
        dx, gs[tag + "_norm"], *own = _ffn_bwd_input(dh, h, P[tag + "_norm"], dg, du, W[names[0]], W[names[1]],
                                                     tag + "_bwd_input", carry=ex)
        if comm and not later:
            comm.landed(names, own)
        return dx, landed, own

    second = ("ffn2_w_gate", "ffn2_w_up", "ffn2_w_down")
    dh2, _, swapped = ffn_grads("ffn2", dh3, h2, n3, g3, u3, later=True)
    d_oa, d_ob, d_gr, d_dgate, d_ma, d_mb, gs["gla_head_norm"], gs["dn_head_norm"], dh2b, *landed = _merge_bwd(
        dh2, oa, ob, pbig, P["gla_head_norm"], P["dn_head_norm"], W["w_out"], "merge_bwd",
        carry=comm.pair(second, [gw[n] for n in second], swapped) if comm else None)
    if comm:
        comm.landed(second, landed)
    gw["w_out"] = _mm_tn(yb, dh2b, D, D, "dw_out").reshape(N_SHARD, D // N_SHARD, D)
    early = ("w_out",)
    d_gq, d_gk, d_gv, dpre, *swapped = _gla_bwd(pbig, qk, psmall, wgate_pad, P["b_gla_gate"], sa, d_oa, "gla_bwd",
                                                carry=_SiblingExchange([gw["w_out"]]) if comm else None)
    dcq, dck, dcv, dsm, dpar, *landed = _gdn_bwd(conv, psmall, par, sb, d_ob, solved, "gdn_bwd",
                                                 carry=comm.pair(early, [gw["w_out"]], swapped) if comm else None)
    if comm:
        comm.landed(early, landed)
    dsmall, dwgate, gs["b_gla_gate"] = _gla_gate_bwd(dpre, psmall, wgate_pad, dsm, "gla_gate_bwd")
    gs["w_gla_gate"] = dwgate[:GLA_RANK]
    d_x3, dcw = _conv_bwd(dcq, dck, dcv, pbig, cw8, "conv_bwd")
    gs["conv_w"] = dcw[:CONV_K]
    gs["dn_a_log"] = dpar[:, 0, 0].reshape(1, DN_H)
    gs["dn_dt_bias"] = dpar[:, 0, 1].reshape(1, DN_H)
    pieces = (d_gq, d_gk, d_gv, d_gr, d_x3, d_dgate, d_ma, d_mb)
    dh1, gs["mix_norm"] = _proj_bwd(dh2, h1, P["mix_norm"], pieces, dsmall, wbig, wsmall, "proj_bwd")
    dbig = jnp.concatenate([_mm_tn(n2, p, D, 1024, "dw_in_%d" % i) for i, p in enumerate(pieces)], axis=1)
    dsml = _mm_tn(n2, dsmall, D, SMALL, "dw_in_small")
    gw["w_in"] = _cut_cols(_merge_w_in(dbig, dsml))
    grad_x, landed, _ = ffn_grads("ffn1", dh1, x, n1, g1, u1,
                                  before=comm.begin(("w_in",), [gw["w_in"]]) if comm else None)
    if comm:
        comm.landed(("w_in",), landed)
    return loss, grad_x, gw, gs


SMALL_NAMES = ("ffn1_norm", "mix_norm", "ffn2_norm", "final_norm", "b_gla_gate", "gla_head_norm", "dn_head_norm",
               "dn_a_log", "dn_dt_bias")
ROW4 = (("b_gla_gate", 512), ("gla_head_norm", 256), ("dn_head_norm", 128), ("dn_a_log", 8), ("dn_dt_bias", 8))


def _pack_small(d, loss=None):
    row4 = [d[n].reshape(-1) for n, _ in ROW4]
    row4.append(jnp.zeros((1,), F32) if loss is None else loss.reshape(1))
    row4 = jnp.concatenate(row4)
    row4 = jnp.pad(row4, (0, D - row4.shape[0]))
    rows = [d[n].reshape(-1) for n in SMALL_NAMES[:4]] + [row4]
    return jnp.concatenate([jnp.stack(rows), jnp.zeros((3, D), F32)], axis=0)


def _unpack_small(a, like):
    out = {n: a[i].reshape(like[n].shape) for i, n in enumerate(SMALL_NAMES[:4])}
    off = 0
    for n, w in ROW4:
        out[n] = a[4, off:off + w].reshape(like[n].shape)
        off += w
    return out, a[4, off]


WEIGHT_ORDER = ("ffn1_norm", "ffn1_w_gate", "ffn1_w_up", "ffn1_w_down", "mix_norm", "w_in", "w_gla_gate", "b_gla_gate",
                "conv_w", "dn_a_log", "dn_dt_bias", "gla_head_norm", "dn_head_norm", "w_out", "ffn2_norm",
                "ffn2_w_gate", "ffn2_w_up", "ffn2_w_down", "final_norm")
ADAM_ROWS = {"ffn1_w_gate": 512, "ffn1_w_up": 512, "ffn1_w_down": 352, "w_in": 128, "w_gla_gate": 16, "conv_w": 4,
             "w_out": 128, "ffn2_w_gate": 512, "ffn2_w_up": 512, "ffn2_w_down": 352}


def kernel(x, ffn1_norm, ffn1_w_gate, ffn1_w_up, ffn1_w_down, mix_norm, w_in, w_gla_gate, b_gla_gate, conv_w, dn_a_log, dn_dt_bias, gla_head_norm, dn_head_norm, w_out, ffn2_norm, ffn2_w_gate, ffn2_w_up, ffn2_w_down, final_norm, loss_target, m_ffn1_norm, m_ffn1_w_gate, m_ffn1_w_up, m_ffn1_w_down, m_mix_norm, m_w_in, m_w_gla_gate, m_b_gla_gate, m_conv_w, m_dn_a_log, m_dn_dt_bias, m_gla_head_norm, m_dn_head_norm, m_w_out, m_ffn2_norm, m_ffn2_w_gate, m_ffn2_w_up, m_ffn2_w_down, m_final_norm, v_ffn1_norm, v_ffn1_w_gate, v_ffn1_w_up, v_ffn1_w_down, v_mix_norm, v_w_in, v_w_gla_gate, v_b_gla_gate, v_conv_w, v_dn_a_log, v_dn_dt_bias, v_gla_head_norm, v_dn_head_norm, v_w_out, v_ffn2_norm, v_ffn2_w_gate, v_ffn2_w_up, v_ffn2_w_down, v_final_norm):
    given = dict(locals())
    wts = {n: given[n] for n in WEIGHT_ORDER}
    moms = {n: given["m_" + n] for n in WEIGHT_ORDER}
    vars_ = {n: given["v_" + n] for n in WEIGHT_ORDER}
    two_d = lambda a: a.reshape(a.shape[-2], a.shape[-1]) if a.ndim == 3 else a.reshape(1, -1)
    shard = {n: two_d(wts[n]) for n in SHARDED}

    gathered = _gather_weights([shard[n].astype(BF16) for n in BIG[:3]], [shard[n] for n in TINY], "gather_first")
    W = dict(zip(BIG[:3], gathered))
    P = {n: two_d(wts[n]) for n in SMALL_NAMES}
    for n, g in zip(TINY, gathered[3:]):
        P[n] = _join_cols(g)

    my_slot = 2 * lax.axis_index("x") + lax.axis_index("y")
    where = jnp.stack([lax.axis_index("c"), my_slot]).astype(jnp.int32)
    comm = _Comm(where, [shard[n].astype(BF16) for n in BIG[3:]])
    loss, grad_x, gw, gs = _local_step(x[0], loss_target[0], W, P, comm)
    halves = [_add_four(comm.pairs[n][1], comm.got[n], "rs_four_" + n) for n in BIG]
    other_halves = _rs_swap(halves, "rs_swap")

    tiny_rows = jnp.concatenate([gs["w_gla_gate"].reshape(8, D), gs["conv_w"].reshape(12, D), jnp.zeros((4, D), F32)])
    all_sum = _allsum_small(jnp.concatenate([_pack_small(gs, loss[0, 0]), tiny_rows]), "allsum_small")
    small_sum = all_sum[:8]
    small_g, loss_total = _unpack_small(small_sum, P)

    grads, delta, new_m, new_v = {}, {}, {}, {}
    for n, own, got in zip(BIG, halves, other_halves):
        res = _adamw_halves(shard[n], own, got, two_d(moms[n]), two_d(vars_[n]), ADAM_ROWS[n], "adamw_" + n)
        grads[n], delta[n], new_m[n], new_v[n] = (t.reshape(wts[n].shape) for t in res)
    for n, rows in (("w_gla_gate", all_sum[8:16]), ("conv_w", all_sum[16:28])):
        cols = shard[n].shape[1]
        grads[n] = lax.dynamic_slice_in_dim(rows.reshape(shard[n].shape[0], N_SHARD * cols), my_slot * cols, cols, axis=1)
        d, m_, v_ = _adamw(shard[n], grads[n], two_d(moms[n]), two_d(vars_[n]), ADAM_ROWS[n], "adamw_" + n)
        delta[n], new_m[n], new_v[n] = (t.reshape(wts[n].shape) for t in (d, m_, v_))
    pk = lambda src: _pack_small({n: two_d(src[n]) for n in SMALL_NAMES})
    sd, sm_, sv_ = _adamw(pk(wts), small_sum, pk(moms), pk(vars_), 8, "adamw_small")
    for res, dst in ((sd, delta), (sm_, new_m), (sv_, new_v)):
        u, _ = _unpack_small(res, wts)
        dst.update(u)
    grad_w = {n: grads[n].reshape(wts[n].shape) for n in SHARDED}
    grad_w.update({n: small_g[n].reshape(wts[n].shape) for n in SMALL_NAMES})
    return (loss_total, grad_x[None], *[grad_w[n] for n in WEIGHT_ORDER], *[delta[n] for n in WEIGHT_ORDER],
            *[new_m[n] for n in WEIGHT_ORDER], *[new_v[n] for n in WEIGHT_ORDER])
```
